```python
import jax, jax.numpy as jnp
from jax import lax
import numpy as np

D_MODEL = 1024
BATCH = 8
SEQ = 8192
DEPTH = 2

MEM_LEN = 256
HEAD_DIM = 64
ROPE_THETA = 10000.0
NORM_EPS = 1e-6
BLOCK = 128

SWA_HEADS = 8
SWA_KV_HEADS = 2
SWA_WINDOW = 128
MLA_HEADS = 8
MLA_Q_RANK = 384
MLA_KV_RANK = 256
MLA_NOPE_DIM = 64
MLA_ROPE_DIM = 32
MLA_V_DIM = 64
A_Q = SWA_HEADS * HEAD_DIM
A_KV = SWA_KV_HEADS * HEAD_DIM
EVEN_IN = A_Q + 2 * A_KV + MLA_Q_RANK + MLA_KV_RANK + MLA_ROPE_DIM
EVEN_SPLITS = [A_Q, A_Q + A_KV, A_Q + 2 * A_KV, A_Q + 2 * A_KV + MLA_Q_RANK,
               A_Q + 2 * A_KV + MLA_Q_RANK + MLA_KV_RANK]
EVEN_OUT = SWA_HEADS * HEAD_DIM + MLA_HEADS * MLA_V_DIM
DIL_HEADS = D_MODEL // HEAD_DIM
DIL_PATTERNS = ((128, 1), (512, 4), (2048, 16))
X_HEADS = 4
X_HEAD_DIM = 128
FFN_HIDDEN = -(-8 * D_MODEL // (3 * 256)) * 256

kernel_name = 'hybrid_swa_mla_dilated_block'


def rms_norm(x, g):
    xf = x.astype(jnp.float32)
    y = xf * lax.rsqrt(jnp.mean(xf * xf, axis=-1, keepdims=True) + NORM_EPS)
    return (y * g.astype(jnp.float32)).astype(x.dtype)


def rope(x, positions):
    dh = x.shape[-1]
    inv_freq = ROPE_THETA ** (-jnp.arange(0, dh, 2, dtype=jnp.float32) / dh)
    ang = positions.astype(jnp.float32)[..., None] * inv_freq
    c = jnp.cos(ang)[:, :, None, :]
    s = jnp.sin(ang)[:, :, None, :]
    x1, x2 = jnp.split(x.astype(jnp.float32), 2, axis=-1)
    return jnp.concatenate([x1 * c - x2 * s, x2 * c + x1 * s], axis=-1).astype(x.dtype)


def banded_attention(q, k, v, max_dist, sink=None):
    b, L, h, dh = q.shape
    g = k.shape[2]
    rep = h // g
    nb = L // BLOCK
    qb = q.reshape(b, nb, BLOCK, g, rep, dh)

    def two_blocks(t):
        tb = t.reshape(b, nb, BLOCK, g, t.shape[-1])
        prev = jnp.pad(tb, ((0, 0), (1, 0), (0, 0), (0, 0), (0, 0)))[:, :-1]
        return jnp.concatenate([prev, tb], axis=2)

    kk = two_blocks(k)
    vv = two_blocks(v)
    s = jnp.einsum('bnqgrd,bnkgd->bngrqk', qb, kk).astype(jnp.float32) * (dh ** -0.5)
    qi = jnp.arange(BLOCK)[:, None]
    kj = jnp.arange(2 * BLOCK)[None, :]
    dist = BLOCK + qi - kj
    band = (dist >= 0) & (dist <= max_dist)
    exists = (jnp.arange(nb)[:, None, None] > 0) | (kj >= BLOCK)[None]
    mask = band[None] & exists
    s = jnp.where(mask[None, :, None, None], s, -jnp.inf)
    m = jnp.max(s, axis=-1, keepdims=True)
    if sink is not None:
        sk = sink.astype(jnp.float32).reshape(g, rep)[None, None, :, :, None, None]
        m = jnp.maximum(m, sk)
    p = jnp.exp(s - m)
    l = jnp.sum(p, axis=-1, keepdims=True)
    if sink is not None:
        l = l + jnp.exp(sk - m)
    o = jnp.einsum('bngrqk,bnkgd->bnqgrd', (p / l).astype(v.dtype), vv)
    lse = (m + jnp.log(l))[..., 0].transpose(0, 1, 4, 2, 3).reshape(b, L, h)
    return o.reshape(b, L, h, -1), lse


def causal_mla_attention(q_nope, q_rope, k_nope, k_rope, v):
    S = q_nope.shape[1]
    scale = (q_nope.shape[-1] + q_rope.shape[-1]) ** -0.5
    outs = []
    for i in range(S // BLOCK):
        q0, q1 = i * BLOCK, (i + 1) * BLOCK
        s = (jnp.einsum('bqhd,bkhd->bhqk', q_nope[:, q0:q1], k_nope[:, :q1])
             + jnp.einsum('bqhd,bkd->bhqk', q_rope[:, q0:q1], k_rope[:, :q1])).astype(jnp.float32) * scale
        causal = jnp.arange(q0, q1)[:, None] >= jnp.arange(q1)[None, :]
        p = jax.nn.softmax(jnp.where(causal, s, -jnp.inf), axis=-1).astype(v.dtype)
        outs.append(jnp.einsum('bhqk,bkhd->bqhd', p, v[:, :q1]))
    return jnp.concatenate(outs, axis=1)


def even_mixer(h, positions, w_in, sinks, q_norm, w_uq, kv_norm, w_ukv, w_out):
    b, s, _ = h.shape
    z = h @ w_in
    qa, ka, va, cq, ckv, kr = jnp.split(z, EVEN_SPLITS, axis=-1)
    qa = rope(qa.reshape(b, s, SWA_HEADS, HEAD_DIM), positions)
    ka = rope(ka.reshape(b, s, SWA_KV_HEADS, HEAD_DIM), positions)
    va = va.reshape(b, s, SWA_KV_HEADS, HEAD_DIM)
    oa, _ = banded_attention(qa, ka, va, SWA_WINDOW - 1, sink=sinks)
    qb = (rms_norm(cq, q_norm) @ w_uq).reshape(b, s, MLA_HEADS, MLA_NOPE_DIM + MLA_ROPE_DIM)
    q_nope, q_rope = jnp.split(qb, [MLA_NOPE_DIM], axis=-1)
    q_rope = rope(q_rope, positions)
    kvb = (rms_norm(ckv, kv_norm) @ w_ukv).reshape(b, s, MLA_HEADS, MLA_NOPE_DIM + MLA_V_DIM)
    k_nope, vb = jnp.split(kvb, [MLA_NOPE_DIM], axis=-1)
    k_rope = rope(kr[:, :, None, :], positions)[:, :, 0]
    ob = causal_mla_attention(q_nope, q_rope, k_nope, k_rope, vb)
    o = jnp.concatenate([oa.reshape(b, s, -1), ob.reshape(b, s, -1)], axis=-1)
    return o @ w_out


def dilated_attention(q, k, v):
    b, s, h, dh = q.shape
    outs, lses = [], []
    for window, dil in DIL_PATTERNS:
        span = dil * BLOCK
        L = -(-s // span) * span
        n = L // dil

        def deinterleave(t):
            t = jnp.pad(t, ((0, 0), (0, L - s), (0, 0), (0, 0)))
            return t.reshape(b, n, dil, h, dh).transpose(0, 2, 1, 3, 4).reshape(b * dil, n, h, dh)

        o, lse = banded_attention(deinterleave(q), deinterleave(k), deinterleave(v), window // dil)
        outs.append(o.reshape(b, dil, n, h, dh).transpose(0, 2, 1, 3, 4).reshape(b, L, h, dh)[:, :s])
        lses.append(lse.reshape(b, dil, n, h).transpose(0, 2, 1, 3).reshape(b, L, h)[:, :s])
    wts = jax.nn.softmax(jnp.stack(lses, axis=-1), axis=-1).astype(q.dtype)
    return jnp.einsum('bshdn,bshn->bshd', jnp.stack(outs, axis=-1), wts)


def odd_mixer(h, positions, w_qkv, w_out):
    b, s, _ = h.shape
    q, k, v = jnp.split((h @ w_qkv).reshape(b, s, 3 * DIL_HEADS, HEAD_DIM), 3, axis=2)
    o = dilated_attention(rope(q, positions), rope(k, positions), v)
    return o.reshape(b, s, -1) @ w_out


def memory_cross_attention(h, mem_n, w_q, w_kv, w_o):
    b, s, _ = h.shape
    q = (h @ w_q).reshape(b, s, X_HEADS, X_HEAD_DIM)
    k, v = jnp.split((mem_n @ w_kv).reshape(b, mem_n.shape[1], 2 * X_HEADS, X_HEAD_DIM), 2, axis=2)
    sc = jnp.einsum('bqhd,bkhd->bhqk', q, k).astype(jnp.float32) * (X_HEAD_DIM ** -0.5)
    p = jax.nn.softmax(sc, axis=-1).astype(v.dtype)
    o = jnp.einsum('bhqk,bkhd->bqhd', p, v).reshape(b, s, -1)
    return o @ w_o


def swiglu(h, w_gate, w_up, w_down):
    return (jax.nn.silu(h @ w_gate) * (h @ w_up)) @ w_down


def _fwd_setup_inputs(seed: int = 0) -> dict:
    key = jax.random.key(seed)
    keys = iter(jax.random.split(key, 64))

    def w(shape, fan_in, gain=1.0):
        return jax.random.normal(next(keys), shape, jnp.float32) * (gain * fan_in ** -0.5)

    def gain_vec(n):
        return 1.0 + 0.02 * jax.random.normal(next(keys), (n,), jnp.float32)

    res_gain = (2.0 * DEPTH) ** -0.5
    inp = {}
    inp['x'] = jax.random.normal(next(keys), (BATCH, SEQ, D_MODEL), jnp.float32)
    inp['mem'] = jax.random.normal(next(keys), (BATCH, MEM_LEN, D_MODEL), jnp.float32)
    offsets = jax.random.randint(next(keys), (BATCH, 1), 0, 4096, dtype=jnp.int32)
    inp['positions'] = jnp.arange(SEQ, dtype=jnp.int32)[None, :] + offsets
    for l in range(DEPTH):
        p = 'l%d_' % l
        inp[p + 'mix_norm'] = gain_vec(D_MODEL)
        if l % 2 == 0:
            inp[p + 'w_in'] = w((D_MODEL, EVEN_IN), D_MODEL)
            inp[p + 'sinks'] = jax.random.normal(next(keys), (SWA_HEADS,), jnp.float32)
            inp[p + 'q_norm'] = gain_vec(MLA_Q_RANK)
            inp[p + 'w_uq'] = w((MLA_Q_RANK, MLA_HEADS * (MLA_NOPE_DIM + MLA_ROPE_DIM)), MLA_Q_RANK)
            inp[p + 'kv_norm'] = gain_vec(MLA_KV_RANK)
            inp[p + 'w_ukv'] = w((MLA_KV_RANK, MLA_HEADS * (MLA_NOPE_DIM + MLA_V_DIM)), MLA_KV_RANK)
            inp[p + 'w_out'] = w((EVEN_OUT, D_MODEL), EVEN_OUT, res_gain)
        else:
            inp[p + 'w_qkv'] = w((D_MODEL, 3 * DIL_HEADS * HEAD_DIM), D_MODEL)
            inp[p + 'w_out'] = w((DIL_HEADS * HEAD_DIM, D_MODEL), DIL_HEADS * HEAD_DIM, res_gain)
        inp[p + 'x_norm'] = gain_vec(D_MODEL)
        inp[p + 'mem_norm'] = gain_vec(D_MODEL)
        inp[p + 'w_xq'] = w((D_MODEL, X_HEADS * X_HEAD_DIM), D_MODEL)
        inp[p + 'w_xkv'] = w((D_MODEL, 2 * X_HEADS * X_HEAD_DIM), D_MODEL)
        inp[p + 'w_xo'] = w((X_HEADS * X_HEAD_DIM, D_MODEL), X_HEADS * X_HEAD_DIM, res_gain)
        inp[p + 'ffn_norm'] = gain_vec(D_MODEL)
        inp[p + 'w_gate'] = w((D_MODEL, FFN_HIDDEN), D_MODEL)
        inp[p + 'w_up'] = w((D_MODEL, FFN_HIDDEN), D_MODEL)
        inp[p + 'w_down'] = w((FFN_HIDDEN, D_MODEL), FFN_HIDDEN, res_gain)
    inp['final_norm'] = gain_vec(D_MODEL)
    return inp


def _fwd_reference(x, mem, positions,
              l0_mix_norm, l0_w_in, l0_sinks, l0_q_norm, l0_w_uq, l0_kv_norm, l0_w_ukv, l0_w_out,
              l0_x_norm, l0_mem_norm, l0_w_xq, l0_w_xkv, l0_w_xo,
              l0_ffn_norm, l0_w_gate, l0_w_up, l0_w_down,
              l1_mix_norm, l1_w_qkv, l1_w_out,
              l1_x_norm, l1_mem_norm, l1_w_xq, l1_w_xkv, l1_w_xo,
              l1_ffn_norm, l1_w_gate, l1_w_up, l1_w_down,
              final_norm):
    mixers = [
        lambda h: even_mixer(h, positions, l0_w_in, l0_sinks, l0_q_norm, l0_w_uq,
                             l0_kv_norm, l0_w_ukv, l0_w_out),
        lambda h: odd_mixer(h, positions, l1_w_qkv, l1_w_out),
    ]
    mix_norms = [l0_mix_norm, l1_mix_norm]
    xattn = [(l0_x_norm, l0_mem_norm, l0_w_xq, l0_w_xkv, l0_w_xo),
             (l1_x_norm, l1_mem_norm, l1_w_xq, l1_w_xkv, l1_w_xo)]
    ffns = [(l0_ffn_norm, l0_w_gate, l0_w_up, l0_w_down),
            (l1_ffn_norm, l1_w_gate, l1_w_up, l1_w_down)]
    for layer in range(DEPTH):
        x = x + mixers[layer](rms_norm(x, mix_norms[layer]))
        xn, mn, wq, wkv, wo = xattn[layer]
        x = x + memory_cross_attention(rms_norm(x, xn), rms_norm(mem, mn), wq, wkv, wo)
        fn, wg, wu, wd = ffns[layer]
        x = x + swiglu(rms_norm(x, fn), wg, wu, wd)
    return rms_norm(x, final_norm)


import jax as _jax
import jax.numpy as _jnp

TWIN_FORMAT = 'train_step'
FWD_PARAMS = ['x', 'mem', 'positions', 'l0_mix_norm', 'l0_w_in', 'l0_sinks', 'l0_q_norm', 'l0_w_uq', 'l0_kv_norm', 'l0_w_ukv', 'l0_w_out', 'l0_x_norm', 'l0_mem_norm', 'l0_w_xq', 'l0_w_xkv', 'l0_w_xo', 'l0_ffn_norm', 'l0_w_gate', 'l0_w_up', 'l0_w_down', 'l1_mix_norm', 'l1_w_qkv', 'l1_w_out', 'l1_x_norm', 'l1_mem_norm', 'l1_w_xq', 'l1_w_xkv', 'l1_w_xo', 'l1_ffn_norm', 'l1_w_gate', 'l1_w_up', 'l1_w_down', 'final_norm']
TWIN_WEIGHTS = ['l0_mix_norm', 'l0_w_in', 'l0_sinks', 'l0_q_norm', 'l0_w_uq', 'l0_kv_norm', 'l0_w_ukv', 'l0_w_out', 'l0_x_norm', 'l0_mem_norm', 'l0_w_xq', 'l0_w_xkv', 'l0_w_xo', 'l0_ffn_norm', 'l0_w_gate', 'l0_w_up', 'l0_w_down', 'l1_mix_norm', 'l1_w_qkv', 'l1_w_out', 'l1_x_norm', 'l1_mem_norm', 'l1_w_xq', 'l1_w_xkv', 'l1_w_xo', 'l1_ffn_norm', 'l1_w_gate', 'l1_w_up', 'l1_w_down', 'final_norm']
TWIN_DIFF_INPUT = 'x'
TWIN_INPUTS = ['x', 'mem', 'positions', 'l0_mix_norm', 'l0_w_in', 'l0_sinks', 'l0_q_norm', 'l0_w_uq', 'l0_kv_norm', 'l0_w_ukv', 'l0_w_out', 'l0_x_norm', 'l0_mem_norm', 'l0_w_xq', 'l0_w_xkv', 'l0_w_xo', 'l0_ffn_norm', 'l0_w_gate', 'l0_w_up', 'l0_w_down', 'l1_mix_norm', 'l1_w_qkv', 'l1_w_out', 'l1_x_norm', 'l1_mem_norm', 'l1_w_xq', 'l1_w_xkv', 'l1_w_xo', 'l1_ffn_norm', 'l1_w_gate', 'l1_w_up', 'l1_w_down', 'final_norm', 'loss_target', 'm_l0_mix_norm', 'm_l0_w_in', 'm_l0_sinks', 'm_l0_q_norm', 'm_l0_w_uq', 'm_l0_kv_norm', 'm_l0_w_ukv', 'm_l0_w_out', 'm_l0_x_norm', 'm_l0_mem_norm', 'm_l0_w_xq', 'm_l0_w_xkv', 'm_l0_w_xo', 'm_l0_ffn_norm', 'm_l0_w_gate', 'm_l0_w_up', 'm_l0_w_down', 'm_l1_mix_norm', 'm_l1_w_qkv', 'm_l1_w_out', 'm_l1_x_norm', 'm_l1_mem_norm', 'm_l1_w_xq', 'm_l1_w_xkv', 'm_l1_w_xo', 'm_l1_ffn_norm', 'm_l1_w_gate', 'm_l1_w_up', 'm_l1_w_down', 'm_final_norm', 'v_l0_mix_norm', 'v_l0_w_in', 'v_l0_sinks', 'v_l0_q_norm', 'v_l0_w_uq', 'v_l0_kv_norm', 'v_l0_w_ukv', 'v_l0_w_out', 'v_l0_x_norm', 'v_l0_mem_norm', 'v_l0_w_xq', 'v_l0_w_xkv', 'v_l0_w_xo', 'v_l0_ffn_norm', 'v_l0_w_gate', 'v_l0_w_up', 'v_l0_w_down', 'v_l1_mix_norm', 'v_l1_w_qkv', 'v_l1_w_out', 'v_l1_x_norm', 'v_l1_mem_norm', 'v_l1_w_xq', 'v_l1_w_xkv', 'v_l1_w_xo', 'v_l1_ffn_norm', 'v_l1_w_gate', 'v_l1_w_up', 'v_l1_w_down', 'v_final_norm']
TWIN_OUTPUTS = ['loss', 'grad_x', 'grad_l0_mix_norm', 'grad_l0_w_in', 'grad_l0_sinks', 'grad_l0_q_norm', 'grad_l0_w_uq', 'grad_l0_kv_norm', 'grad_l0_w_ukv', 'grad_l0_w_out', 'grad_l0_x_norm', 'grad_l0_mem_norm', 'grad_l0_w_xq', 'grad_l0_w_xkv', 'grad_l0_w_xo', 'grad_l0_ffn_norm', 'grad_l0_w_gate', 'grad_l0_w_up', 'grad_l0_w_down', 'grad_l1_mix_norm', 'grad_l1_w_qkv', 'grad_l1_w_out', 'grad_l1_x_norm', 'grad_l1_mem_norm', 'grad_l1_w_xq', 'grad_l1_w_xkv', 'grad_l1_w_xo', 'grad_l1_ffn_norm', 'grad_l1_w_gate', 'grad_l1_w_up', 'grad_l1_w_down', 'grad_final_norm', 'delta_l0_mix_norm', 'delta_l0_w_in', 'delta_l0_sinks', 'delta_l0_q_norm', 'delta_l0_w_uq', 'delta_l0_kv_norm', 'delta_l0_w_ukv', 'delta_l0_w_out', 'delta_l0_x_norm', 'delta_l0_mem_norm', 'delta_l0_w_xq', 'delta_l0_w_xkv', 'delta_l0_w_xo', 'delta_l0_ffn_norm', 'delta_l0_w_gate', 'delta_l0_w_up', 'delta_l0_w_down', 'delta_l1_mix_norm', 'delta_l1_w_qkv', 'delta_l1_w_out', 'delta_l1_x_norm', 'delta_l1_mem_norm', 'delta_l1_w_xq', 'delta_l1_w_xkv', 'delta_l1_w_xo', 'delta_l1_ffn_norm', 'delta_l1_w_gate', 'delta_l1_w_up', 'delta_l1_w_down', 'delta_final_norm', 'new_m_l0_mix_norm', 'new_m_l0_w_in', 'new_m_l0_sinks', 'new_m_l0_q_norm', 'new_m_l0_w_uq', 'new_m_l0_kv_norm', 'new_m_l0_w_ukv', 'new_m_l0_w_out', 'new_m_l0_x_norm', 'new_m_l0_mem_norm', 'new_m_l0_w_xq', 'new_m_l0_w_xkv', 'new_m_l0_w_xo', 'new_m_l0_ffn_norm', 'new_m_l0_w_gate', 'new_m_l0_w_up', 'new_m_l0_w_down', 'new_m_l1_mix_norm', 'new_m_l1_w_qkv', 'new_m_l1_w_out', 'new_m_l1_x_norm', 'new_m_l1_mem_norm', 'new_m_l1_w_xq', 'new_m_l1_w_xkv', 'new_m_l1_w_xo', 'new_m_l1_ffn_norm', 'new_m_l1_w_gate', 'new_m_l1_w_up', 'new_m_l1_w_down', 'new_m_final_norm', 'new_v_l0_mix_norm', 'new_v_l0_w_in', 'new_v_l0_sinks', 'new_v_l0_q_norm', 'new_v_l0_w_uq', 'new_v_l0_kv_norm', 'new_v_l0_w_ukv', 'new_v_l0_w_out', 'new_v_l0_x_norm', 'new_v_l0_mem_norm', 'new_v_l0_w_xq', 'new_v_l0_w_xkv', 'new_v_l0_w_xo', 'new_v_l0_ffn_norm', 'new_v_l0_w_gate', 'new_v_l0_w_up', 'new_v_l0_w_down', 'new_v_l1_mix_norm', 'new_v_l1_w_qkv', 'new_v_l1_w_out', 'new_v_l1_x_norm', 'new_v_l1_mem_norm', 'new_v_l1_w_xq', 'new_v_l1_w_xkv', 'new_v_l1_w_xo', 'new_v_l1_ffn_norm', 'new_v_l1_w_gate', 'new_v_l1_w_up', 'new_v_l1_w_down', 'new_v_final_norm']
TWIN_LEAF_KINDS = {'loss': 'loss', 'grad_x': 'grad_x', 'grad_l0_mix_norm': 'grad_w', 'grad_l0_w_in': 'grad_w', 'grad_l0_sinks': 'grad_w', 'grad_l0_q_norm': 'grad_w', 'grad_l0_w_uq': 'grad_w', 'grad_l0_kv_norm': 'grad_w', 'grad_l0_w_ukv': 'grad_w', 'grad_l0_w_out': 'grad_w', 'grad_l0_x_norm': 'grad_w', 'grad_l0_mem_norm': 'grad_w', 'grad_l0_w_xq': 'grad_w', 'grad_l0_w_xkv': 'grad_w', 'grad_l0_w_xo': 'grad_w', 'grad_l0_ffn_norm': 'grad_w', 'grad_l0_w_gate': 'grad_w', 'grad_l0_w_up': 'grad_w', 'grad_l0_w_down': 'grad_w', 'grad_l1_mix_norm': 'grad_w', 'grad_l1_w_qkv': 'grad_w', 'grad_l1_w_out': 'grad_w', 'grad_l1_x_norm': 'grad_w', 'grad_l1_mem_norm': 'grad_w', 'grad_l1_w_xq': 'grad_w', 'grad_l1_w_xkv': 'grad_w', 'grad_l1_w_xo': 'grad_w', 'grad_l1_ffn_norm': 'grad_w', 'grad_l1_w_gate': 'grad_w', 'grad_l1_w_up': 'grad_w', 'grad_l1_w_down': 'grad_w', 'grad_final_norm': 'grad_w', 'delta_l0_mix_norm': 'delta_w', 'delta_l0_w_in': 'delta_w', 'delta_l0_sinks': 'delta_w', 'delta_l0_q_norm': 'delta_w', 'delta_l0_w_uq': 'delta_w', 'delta_l0_kv_norm': 'delta_w', 'delta_l0_w_ukv': 'delta_w', 'delta_l0_w_out': 'delta_w', 'delta_l0_x_norm': 'delta_w', 'delta_l0_mem_norm': 'delta_w', 'delta_l0_w_xq': 'delta_w', 'delta_l0_w_xkv': 'delta_w', 'delta_l0_w_xo': 'delta_w', 'delta_l0_ffn_norm': 'delta_w', 'delta_l0_w_gate': 'delta_w', 'delta_l0_w_up': 'delta_w', 'delta_l0_w_down': 'delta_w', 'delta_l1_mix_norm': 'delta_w', 'delta_l1_w_qkv': 'delta_w', 'delta_l1_w_out': 'delta_w', 'delta_l1_x_norm': 'delta_w', 'delta_l1_mem_norm': 'delta_w', 'delta_l1_w_xq': 'delta_w', 'delta_l1_w_xkv': 'delta_w', 'delta_l1_w_xo': 'delta_w', 'delta_l1_ffn_norm': 'delta_w', 'delta_l1_w_gate': 'delta_w', 'delta_l1_w_up': 'delta_w', 'delta_l1_w_down': 'delta_w', 'delta_final_norm': 'delta_w', 'new_m_l0_mix_norm': 'new_m', 'new_m_l0_w_in': 'new_m', 'new_m_l0_sinks': 'new_m', 'new_m_l0_q_norm': 'new_m', 'new_m_l0_w_uq': 'new_m', 'new_m_l0_kv_norm': 'new_m', 'new_m_l0_w_ukv': 'new_m', 'new_m_l0_w_out': 'new_m', 'new_m_l0_x_norm': 'new_m', 'new_m_l0_mem_norm': 'new_m', 'new_m_l0_w_xq': 'new_m', 'new_m_l0_w_xkv': 'new_m', 'new_m_l0_w_xo': 'new_m', 'new_m_l0_ffn_norm': 'new_m', 'new_m_l0_w_gate': 'new_m', 'new_m_l0_w_up': 'new_m', 'new_m_l0_w_down': 'new_m', 'new_m_l1_mix_norm': 'new_m', 'new_m_l1_w_qkv': 'new_m', 'new_m_l1_w_out': 'new_m', 'new_m_l1_x_norm': 'new_m', 'new_m_l1_mem_norm': 'new_m', 'new_m_l1_w_xq': 'new_m', 'new_m_l1_w_xkv': 'new_m', 'new_m_l1_w_xo': 'new_m', 'new_m_l1_ffn_norm': 'new_m', 'new_m_l1_w_gate': 'new_m', 'new_m_l1_w_up': 'new_m', 'new_m_l1_w_down': 'new_m', 'new_m_final_norm': 'new_m', 'new_v_l0_mix_norm': 'new_v', 'new_v_l0_w_in': 'new_v', 'new_v_l0_sinks': 'new_v', 'new_v_l0_q_norm': 'new_v', 'new_v_l0_w_uq': 'new_v', 'new_v_l0_kv_norm': 'new_v', 'new_v_l0_w_ukv': 'new_v', 'new_v_l0_w_out': 'new_v', 'new_v_l0_x_norm': 'new_v', 'new_v_l0_mem_norm': 'new_v', 'new_v_l0_w_xq': 'new_v', 'new_v_l0_w_xkv': 'new_v', 'new_v_l0_w_xo': 'new_v', 'new_v_l0_ffn_norm': 'new_v', 'new_v_l0_w_gate': 'new_v', 'new_v_l0_w_up': 'new_v', 'new_v_l0_w_down': 'new_v', 'new_v_l1_mix_norm': 'new_v', 'new_v_l1_w_qkv': 'new_v', 'new_v_l1_w_out': 'new_v', 'new_v_l1_x_norm': 'new_v', 'new_v_l1_mem_norm': 'new_v', 'new_v_l1_w_xq': 'new_v', 'new_v_l1_w_xkv': 'new_v', 'new_v_l1_w_xo': 'new_v', 'new_v_l1_ffn_norm': 'new_v', 'new_v_l1_w_gate': 'new_v', 'new_v_l1_w_up': 'new_v', 'new_v_l1_w_down': 'new_v', 'new_v_final_norm': 'new_v'}


def _forward(args):
    return _fwd_reference(*[args[k] for k in FWD_PARAMS])


def _output_shape():
    out = _jax.eval_shape(lambda: _forward(_fwd_setup_inputs(0)))
    return out.shape, out.dtype

N_MICROBATCH = 1
ADAM_LR = 0.001
ADAM_B1 = 0.9
ADAM_B2 = 0.999
ADAM_EPS = 1e-08
ADAM_WD = 0.01
ADAM_STEP = 10
PER_EXAMPLE_BATCH_AXIS = {'x': 0, 'mem': 0, 'positions': 0, 'loss_target': 0}
SHARED_INPUTS = []
_WEIGHT_DTYPES = {'l0_mix_norm': _jnp.float32, 'l0_w_in': _jnp.float32, 'l0_sinks': _jnp.float32, 'l0_q_norm': _jnp.float32, 'l0_w_uq': _jnp.float32, 'l0_kv_norm': _jnp.float32, 'l0_w_ukv': _jnp.float32, 'l0_w_out': _jnp.float32, 'l0_x_norm': _jnp.float32, 'l0_mem_norm': _jnp.float32, 'l0_w_xq': _jnp.float32, 'l0_w_xkv': _jnp.float32, 'l0_w_xo': _jnp.float32, 'l0_ffn_norm': _jnp.float32, 'l0_w_gate': _jnp.float32, 'l0_w_up': _jnp.float32, 'l0_w_down': _jnp.float32, 'l1_mix_norm': _jnp.float32, 'l1_w_qkv': _jnp.float32, 'l1_w_out': _jnp.float32, 'l1_x_norm': _jnp.float32, 'l1_mem_norm': _jnp.float32, 'l1_w_xq': _jnp.float32, 'l1_w_xkv': _jnp.float32, 'l1_w_xo': _jnp.float32, 'l1_ffn_norm': _jnp.float32, 'l1_w_gate': _jnp.float32, 'l1_w_up': _jnp.float32, 'l1_w_down': _jnp.float32, 'final_norm': _jnp.float32}
MOMENT_SCALE = {'l0_mix_norm': 4.538818e-02, 'l0_w_in': 3.806957e-02, 'l0_sinks': 2.648775e-02, 'l0_q_norm': 2.839763e-02, 'l0_w_uq': 2.037244e-02, 'l0_kv_norm': 5.540838e-02, 'l0_w_ukv': 2.686034e-02, 'l0_w_out': 6.012863e-02, 'l0_x_norm': 1.522949e-02, 'l0_mem_norm': 2.177388e-02, 'l0_w_xq': 2.022946e-02, 'l0_w_xkv': 2.119243e-02, 'l0_w_xo': 3.009698e-02, 'l0_ffn_norm': 1.087899e-01, 'l0_w_gate': 4.643702e-02, 'l0_w_up': 4.497555e-02, 'l0_w_down': 1.491242e-01, 'l1_mix_norm': 4.356478e-02, 'l1_w_qkv': 2.514389e-02, 'l1_w_out': 5.874637e-02, 'l1_x_norm': 1.359594e-02, 'l1_mem_norm': 1.960217e-02, 'l1_w_xq': 1.849902e-02, 'l1_w_xkv': 1.890308e-02, 'l1_w_xo': 2.696003e-02, 'l1_ffn_norm': 1.135251e-01, 'l1_w_gate': 4.302490e-02, 'l1_w_up': 4.176320e-02, 'l1_w_down': 1.377563e-01, 'final_norm': 6.388582e+01}


def _to_microbatches(a, axis):
    t = _jnp.moveaxis(a, axis, 0)
    t = t.reshape((N_MICROBATCH, t.shape[0] // N_MICROBATCH) + t.shape[1:])
    return _jnp.moveaxis(t, 1, axis + 1)


def setup_inputs(seed: int = 0) -> dict:
    inp = _fwd_setup_inputs(seed)
    key = _jax.random.fold_in(_jax.random.key(seed), 7919)
    shape, _ = _output_shape()
    out = dict(inp)
    out["loss_target"] = _jax.random.normal(_jax.random.fold_in(key, 0), shape, _jnp.float32)
    for i, name in enumerate(TWIN_WEIGHTS):
        w = inp[name].astype(_jnp.float32)
        if MOMENT_SCALE is None:
            s = _jnp.sqrt(_jnp.mean(_jnp.square(w)) + 1e-30)
        else:
            s = MOMENT_SCALE[name]
        km, kv = _jax.random.split(_jax.random.fold_in(key, i + 1))
        out[name] = w
        out["m_" + name] = s * _jax.random.normal(km, w.shape, _jnp.float32)
        out["v_" + name] = (s * s) * _jax.random.uniform(kv, w.shape, _jnp.float32, 0.5, 1.5)
    if N_MICROBATCH > 1:
        for name, axis in PER_EXAMPLE_BATCH_AXIS.items():
            out[name] = _to_microbatches(out[name], axis)
    return {'x': out['x'], 'mem': out['mem'], 'positions': out['positions'], 'l0_mix_norm': out['l0_mix_norm'], 'l0_w_in': out['l0_w_in'], 'l0_sinks': out['l0_sinks'], 'l0_q_norm': out['l0_q_norm'], 'l0_w_uq': out['l0_w_uq'], 'l0_kv_norm': out['l0_kv_norm'], 'l0_w_ukv': out['l0_w_ukv'], 'l0_w_out': out['l0_w_out'], 'l0_x_norm': out['l0_x_norm'], 'l0_mem_norm': out['l0_mem_norm'], 'l0_w_xq': out['l0_w_xq'], 'l0_w_xkv': out['l0_w_xkv'], 'l0_w_xo': out['l0_w_xo'], 'l0_ffn_norm': out['l0_ffn_norm'], 'l0_w_gate': out['l0_w_gate'], 'l0_w_up': out['l0_w_up'], 'l0_w_down': out['l0_w_down'], 'l1_mix_norm': out['l1_mix_norm'], 'l1_w_qkv': out['l1_w_qkv'], 'l1_w_out': out['l1_w_out'], 'l1_x_norm': out['l1_x_norm'], 'l1_mem_norm': out['l1_mem_norm'], 'l1_w_xq': out['l1_w_xq'], 'l1_w_xkv': out['l1_w_xkv'], 'l1_w_xo': out['l1_w_xo'], 'l1_ffn_norm': out['l1_ffn_norm'], 'l1_w_gate': out['l1_w_gate'], 'l1_w_up': out['l1_w_up'], 'l1_w_down': out['l1_w_down'], 'final_norm': out['final_norm'], 'loss_target': out['loss_target'], 'm_l0_mix_norm': out['m_l0_mix_norm'], 'm_l0_w_in': out['m_l0_w_in'], 'm_l0_sinks': out['m_l0_sinks'], 'm_l0_q_norm': out['m_l0_q_norm'], 'm_l0_w_uq': out['m_l0_w_uq'], 'm_l0_kv_norm': out['m_l0_kv_norm'], 'm_l0_w_ukv': out['m_l0_w_ukv'], 'm_l0_w_out': out['m_l0_w_out'], 'm_l0_x_norm': out['m_l0_x_norm'], 'm_l0_mem_norm': out['m_l0_mem_norm'], 'm_l0_w_xq': out['m_l0_w_xq'], 'm_l0_w_xkv': out['m_l0_w_xkv'], 'm_l0_w_xo': out['m_l0_w_xo'], 'm_l0_ffn_norm': out['m_l0_ffn_norm'], 'm_l0_w_gate': out['m_l0_w_gate'], 'm_l0_w_up': out['m_l0_w_up'], 'm_l0_w_down': out['m_l0_w_down'], 'm_l1_mix_norm': out['m_l1_mix_norm'], 'm_l1_w_qkv': out['m_l1_w_qkv'], 'm_l1_w_out': out['m_l1_w_out'], 'm_l1_x_norm': out['m_l1_x_norm'], 'm_l1_mem_norm': out['m_l1_mem_norm'], 'm_l1_w_xq': out['m_l1_w_xq'], 'm_l1_w_xkv': out['m_l1_w_xkv'], 'm_l1_w_xo': out['m_l1_w_xo'], 'm_l1_ffn_norm': out['m_l1_ffn_norm'], 'm_l1_w_gate': out['m_l1_w_gate'], 'm_l1_w_up': out['m_l1_w_up'], 'm_l1_w_down': out['m_l1_w_down'], 'm_final_norm': out['m_final_norm'], 'v_l0_mix_norm': out['v_l0_mix_norm'], 'v_l0_w_in': out['v_l0_w_in'], 'v_l0_sinks': out['v_l0_sinks'], 'v_l0_q_norm': out['v_l0_q_norm'], 'v_l0_w_uq': out['v_l0_w_uq'], 'v_l0_kv_norm': out['v_l0_kv_norm'], 'v_l0_w_ukv': out['v_l0_w_ukv'], 'v_l0_w_out': out['v_l0_w_out'], 'v_l0_x_norm': out['v_l0_x_norm'], 'v_l0_mem_norm': out['v_l0_mem_norm'], 'v_l0_w_xq': out['v_l0_w_xq'], 'v_l0_w_xkv': out['v_l0_w_xkv'], 'v_l0_w_xo': out['v_l0_w_xo'], 'v_l0_ffn_norm': out['v_l0_ffn_norm'], 'v_l0_w_gate': out['v_l0_w_gate'], 'v_l0_w_up': out['v_l0_w_up'], 'v_l0_w_down': out['v_l0_w_down'], 'v_l1_mix_norm': out['v_l1_mix_norm'], 'v_l1_w_qkv': out['v_l1_w_qkv'], 'v_l1_w_out': out['v_l1_w_out'], 'v_l1_x_norm': out['v_l1_x_norm'], 'v_l1_mem_norm': out['v_l1_mem_norm'], 'v_l1_w_xq': out['v_l1_w_xq'], 'v_l1_w_xkv': out['v_l1_w_xkv'], 'v_l1_w_xo': out['v_l1_w_xo'], 'v_l1_ffn_norm': out['v_l1_ffn_norm'], 'v_l1_w_gate': out['v_l1_w_gate'], 'v_l1_w_up': out['v_l1_w_up'], 'v_l1_w_down': out['v_l1_w_down'], 'v_final_norm': out['v_final_norm']}


def _loss(weights, diff, rest, loss_target):
    with _jax.named_scope("forward"):
        args = {**rest, TWIN_DIFF_INPUT: diff, **{k: w.astype(_WEIGHT_DTYPES[k]) for k, w in weights.items()}}
        y = _forward(args)
    with _jax.named_scope("loss_head"):
        err = _jnp.square(y.astype(_jnp.float32) - loss_target)
        return 0.5 * _jnp.sum(_jnp.mean(err, axis=-1)) if err.ndim else 0.5 * err


def _adamw(w, g, m, v):
    m = ADAM_B1 * m + (1.0 - ADAM_B1) * g
    v = ADAM_B2 * v + (1.0 - ADAM_B2) * _jnp.square(g)
    m_hat = m / (1.0 - ADAM_B1 ** ADAM_STEP)
    v_hat = v / (1.0 - ADAM_B2 ** ADAM_STEP)
    delta = -ADAM_LR * (m_hat / (_jnp.sqrt(v_hat) + ADAM_EPS) + ADAM_WD * w)
    return delta, m, v


def reference(x, mem, positions, l0_mix_norm, l0_w_in, l0_sinks, l0_q_norm, l0_w_uq, l0_kv_norm, l0_w_ukv, l0_w_out, l0_x_norm, l0_mem_norm, l0_w_xq, l0_w_xkv, l0_w_xo, l0_ffn_norm, l0_w_gate, l0_w_up, l0_w_down, l1_mix_norm, l1_w_qkv, l1_w_out, l1_x_norm, l1_mem_norm, l1_w_xq, l1_w_xkv, l1_w_xo, l1_ffn_norm, l1_w_gate, l1_w_up, l1_w_down, final_norm, loss_target, m_l0_mix_norm, m_l0_w_in, m_l0_sinks, m_l0_q_norm, m_l0_w_uq, m_l0_kv_norm, m_l0_w_ukv, m_l0_w_out, m_l0_x_norm, m_l0_mem_norm, m_l0_w_xq, m_l0_w_xkv, m_l0_w_xo, m_l0_ffn_norm, m_l0_w_gate, m_l0_w_up, m_l0_w_down, m_l1_mix_norm, m_l1_w_qkv, m_l1_w_out, m_l1_x_norm, m_l1_mem_norm, m_l1_w_xq, m_l1_w_xkv, m_l1_w_xo, m_l1_ffn_norm, m_l1_w_gate, m_l1_w_up, m_l1_w_down, m_final_norm, v_l0_mix_norm, v_l0_w_in, v_l0_sinks, v_l0_q_norm, v_l0_w_uq, v_l0_kv_norm, v_l0_w_ukv, v_l0_w_out, v_l0_x_norm, v_l0_mem_norm, v_l0_w_xq, v_l0_w_xkv, v_l0_w_xo, v_l0_ffn_norm, v_l0_w_gate, v_l0_w_up, v_l0_w_down, v_l1_mix_norm, v_l1_w_qkv, v_l1_w_out, v_l1_x_norm, v_l1_mem_norm, v_l1_w_xq, v_l1_w_xkv, v_l1_w_xo, v_l1_ffn_norm, v_l1_w_gate, v_l1_w_up, v_l1_w_down, v_final_norm):
    given = dict(x=x, mem=mem, positions=positions, l0_mix_norm=l0_mix_norm, l0_w_in=l0_w_in, l0_sinks=l0_sinks, l0_q_norm=l0_q_norm, l0_w_uq=l0_w_uq, l0_kv_norm=l0_kv_norm, l0_w_ukv=l0_w_ukv, l0_w_out=l0_w_out, l0_x_norm=l0_x_norm, l0_mem_norm=l0_mem_norm, l0_w_xq=l0_w_xq, l0_w_xkv=l0_w_xkv, l0_w_xo=l0_w_xo, l0_ffn_norm=l0_ffn_norm, l0_w_gate=l0_w_gate, l0_w_up=l0_w_up, l0_w_down=l0_w_down, l1_mix_norm=l1_mix_norm, l1_w_qkv=l1_w_qkv, l1_w_out=l1_w_out, l1_x_norm=l1_x_norm, l1_mem_norm=l1_mem_norm, l1_w_xq=l1_w_xq, l1_w_xkv=l1_w_xkv, l1_w_xo=l1_w_xo, l1_ffn_norm=l1_ffn_norm, l1_w_gate=l1_w_gate, l1_w_up=l1_w_up, l1_w_down=l1_w_down, final_norm=final_norm, loss_target=loss_target, m_l0_mix_norm=m_l0_mix_norm, m_l0_w_in=m_l0_w_in, m_l0_sinks=m_l0_sinks, m_l0_q_norm=m_l0_q_norm, m_l0_w_uq=m_l0_w_uq, m_l0_kv_norm=m_l0_kv_norm, m_l0_w_ukv=m_l0_w_ukv, m_l0_w_out=m_l0_w_out, m_l0_x_norm=m_l0_x_norm, m_l0_mem_norm=m_l0_mem_norm, m_l0_w_xq=m_l0_w_xq, m_l0_w_xkv=m_l0_w_xkv, m_l0_w_xo=m_l0_w_xo, m_l0_ffn_norm=m_l0_ffn_norm, m_l0_w_gate=m_l0_w_gate, m_l0_w_up=m_l0_w_up, m_l0_w_down=m_l0_w_down, m_l1_mix_norm=m_l1_mix_norm, m_l1_w_qkv=m_l1_w_qkv, m_l1_w_out=m_l1_w_out, m_l1_x_norm=m_l1_x_norm, m_l1_mem_norm=m_l1_mem_norm, m_l1_w_xq=m_l1_w_xq, m_l1_w_xkv=m_l1_w_xkv, m_l1_w_xo=m_l1_w_xo, m_l1_ffn_norm=m_l1_ffn_norm, m_l1_w_gate=m_l1_w_gate, m_l1_w_up=m_l1_w_up, m_l1_w_down=m_l1_w_down, m_final_norm=m_final_norm, v_l0_mix_norm=v_l0_mix_norm, v_l0_w_in=v_l0_w_in, v_l0_sinks=v_l0_sinks, v_l0_q_norm=v_l0_q_norm, v_l0_w_uq=v_l0_w_uq, v_l0_kv_norm=v_l0_kv_norm, v_l0_w_ukv=v_l0_w_ukv, v_l0_w_out=v_l0_w_out, v_l0_x_norm=v_l0_x_norm, v_l0_mem_norm=v_l0_mem_norm, v_l0_w_xq=v_l0_w_xq, v_l0_w_xkv=v_l0_w_xkv, v_l0_w_xo=v_l0_w_xo, v_l0_ffn_norm=v_l0_ffn_norm, v_l0_w_gate=v_l0_w_gate, v_l0_w_up=v_l0_w_up, v_l0_w_down=v_l0_w_down, v_l1_mix_norm=v_l1_mix_norm, v_l1_w_qkv=v_l1_w_qkv, v_l1_w_out=v_l1_w_out, v_l1_x_norm=v_l1_x_norm, v_l1_mem_norm=v_l1_mem_norm, v_l1_w_xq=v_l1_w_xq, v_l1_w_xkv=v_l1_w_xkv, v_l1_w_xo=v_l1_w_xo, v_l1_ffn_norm=v_l1_ffn_norm, v_l1_w_gate=v_l1_w_gate, v_l1_w_up=v_l1_w_up, v_l1_w_down=v_l1_w_down, v_final_norm=v_final_norm)
    weights = {n: given[n] for n in TWIN_WEIGHTS}
    shared = {n: given[n] for n in SHARED_INPUTS}
    per_example = {n: given[n] for n in ['x', 'mem', 'positions']}
    grad_fn = _jax.value_and_grad(_loss, argnums=(0, 1))

    def one_microbatch(ex, loss_target):
        ex = dict(ex)
        diff = ex.pop(TWIN_DIFF_INPUT)
        return grad_fn(weights, diff, {**shared, **ex}, loss_target)

    if N_MICROBATCH == 1:
        loss, (grad_w, grad_x) = one_microbatch(per_example, given["loss_target"])
    else:
        def body(carry, xs):
            loss_sum, grad_sum = carry
            l_k, (gw_k, gx_k) = one_microbatch(xs[0], xs[1])
            with _jax.named_scope("update"):
                return (loss_sum + l_k, _jax.tree.map(_jnp.add, grad_sum, gw_k)), gx_k

        init = (_jnp.zeros((), _jnp.float32), _jax.tree.map(_jnp.zeros_like, weights))
        (loss, grad_w), grad_x = _jax.lax.scan(body, init, (per_example, given["loss_target"]))
    with _jax.named_scope("update"):
        delta_w, new_m, new_v = {}, {}, {}
        for n in TWIN_WEIGHTS:
            delta_w[n], new_m[n], new_v[n] = _adamw(weights[n], grad_w[n], given["m_" + n], given["v_" + n])
    return (loss, grad_x, *[grad_w[n] for n in TWIN_WEIGHTS], *[delta_w[n] for n in TWIN_WEIGHTS],
            *[new_m[n] for n in TWIN_WEIGHTS], *[new_v[n] for n in TWIN_WEIGHTS])
```

```python
import functools
import math

import numpy as np
import jax
import jax.numpy as jnp
from jax import lax
from jax.experimental import pallas as pl
from jax.experimental.pallas import tpu as pltpu

F32 = jnp.float32
BF16 = jnp.bfloat16

LANES = 128
VMEM_LIMIT_BYTES = 56 * 1024 * 1024

D_MODEL = 1024
HEAD_DIM = 64
ROPE_THETA = 10000.0
NORM_EPS = 1e-6
BLOCK = 128
SWA_HEADS = 8
SWA_KV_HEADS = 2
SWA_WINDOW = 128
MLA_HEADS = 8
MLA_Q_RANK = 384
MLA_KV_RANK = 256
MLA_NOPE_DIM = 64
MLA_ROPE_DIM = 32
A_Q = SWA_HEADS * HEAD_DIM
A_KV = SWA_KV_HEADS * HEAD_DIM
EVEN_IN = A_Q + 2 * A_KV + MLA_Q_RANK + MLA_KV_RANK + MLA_ROPE_DIM
EVEN_IN_PAD = 1536
DIL_PATTERNS = ((128, 1), (512, 4), (2048, 16))
X_HEADS = 4
X_HEAD_DIM = 128
FFN_HIDDEN = 2816

ADAM_LR = 0.001
ADAM_B1 = 0.9
ADAM_B2 = 0.999
ADAM_EPS = 1e-08
ADAM_WD = 0.01
ADAM_STEP = 10

N_DEV = 8
NEG_MASK = -1e30
NEG_INIT = -1e20

MATRICES = (
    ("l0_w_in", "c", 1024, 1440), ("l0_w_uq", "c", 384, 768), ("l0_w_ukv", "c", 256, 1024),
    ("l0_w_out", "r", 1024, 1024), ("l0_w_xq", "r", 1024, 512), ("l0_w_xkv", "r", 1024, 1024),
    ("l0_w_xo", "c", 512, 1024), ("l0_w_gate", "c", 1024, 2816), ("l0_w_up", "c", 1024, 2816),
    ("l0_w_down", "r", 2816, 1024),
    ("l1_w_qkv", "c", 1024, 3072), ("l1_w_out", "r", 1024, 1024), ("l1_w_xq", "r", 1024, 512),
    ("l1_w_xkv", "r", 1024, 1024), ("l1_w_xo", "c", 512, 1024), ("l1_w_gate", "c", 1024, 2816),
    ("l1_w_up", "c", 1024, 2816), ("l1_w_down", "r", 2816, 1024),
)
VECTORS = (
    ("l0_mix_norm", 1024), ("l0_sinks", 8), ("l0_q_norm", 384), ("l0_kv_norm", 256), ("l0_x_norm", 1024),
    ("l0_mem_norm", 1024), ("l0_ffn_norm", 1024), ("l1_mix_norm", 1024), ("l1_x_norm", 1024),
    ("l1_mem_norm", 1024), ("l1_ffn_norm", 1024), ("final_norm", 1024),
)
WEIGHT_ORDER = (
    "l0_mix_norm", "l0_w_in", "l0_sinks", "l0_q_norm", "l0_w_uq", "l0_kv_norm", "l0_w_ukv", "l0_w_out", "l0_x_norm",
    "l0_mem_norm", "l0_w_xq", "l0_w_xkv", "l0_w_xo", "l0_ffn_norm", "l0_w_gate", "l0_w_up", "l0_w_down",
    "l1_mix_norm", "l1_w_qkv", "l1_w_out", "l1_x_norm", "l1_mem_norm", "l1_w_xq", "l1_w_xkv", "l1_w_xo",
    "l1_ffn_norm", "l1_w_gate", "l1_w_up", "l1_w_down", "final_norm",
)
PACK_COLS = 1024
MAT_ROWS = sum(k * n // N_DEV // PACK_COLS for _, _, k, n in MATRICES)
MAT_ROWS_PAD = -(-MAT_ROWS // 64) * 64
VEC_ROWS = 16


def _pick(n, cands):
    for c in cands:
        if n % c == 0:
            return c
    return n


def _params(*sem):
    return pltpu.CompilerParams(dimension_semantics=sem, vmem_limit_bytes=VMEM_LIMIT_BYTES)


_DIMS = {"nn": (((1,), (0,)), ((), ())), "nt": (((1,), (1,)), ((), ())), "tn": (((0,), (0,)), ((), ()))}


def _mm(a, b, mode, name, out_dtype=F32, res=None):
    if mode == "nn":
        (m, k), (k2, n) = a.shape, b.shape
    elif mode == "nt":
        (m, k), (n, k2) = a.shape, b.shape
    else:
        (k, m), (k2, n) = a.shape, b.shape
    assert k == k2, (name, a.shape, b.shape)
    bm = _pick(m, (512, 256, 128))
    bn = _pick(n, (512, 384, 256, 128))
    bk = _pick(k, (1024, 768, 512, 384, 256, 128))
    nk = k // bk
    dims = _DIMS[mode]
    a_spec = pl.BlockSpec((bk, bm), lambda i, j, kk: (kk, i)) if mode == "tn" else pl.BlockSpec((bm, bk), lambda i, j, kk: (i, kk))
    b_spec = pl.BlockSpec((bn, bk), lambda i, j, kk: (j, kk)) if mode == "nt" else pl.BlockSpec((bk, bn), lambda i, j, kk: (kk, j))
    o_spec = pl.BlockSpec((bm, bn), lambda i, j, kk: (i, j))
    has_res = res is not None

    def body(*refs):
        if has_res:
            a_ref, b_ref, r_ref, o_ref, acc = refs
        else:
            a_ref, b_ref, o_ref, acc = refs
        kk = pl.program_id(2)

        @pl.when(kk == 0)
        def _():
            acc[...] = jnp.zeros_like(acc)

        acc[...] += lax.dot_general(a_ref[...].astype(BF16), b_ref[...].astype(BF16), dims,
                                    preferred_element_type=F32)

        @pl.when(kk == nk - 1)
        def _():
            r = acc[...]
            if has_res:
                r = r + r_ref[...]
            o_ref[...] = r.astype(out_dtype)

    args = (a, b, res) if has_res else (a, b)
    in_specs = [a_spec, b_spec] + ([o_spec] if has_res else [])
    return pl.pallas_call(
        body, name=name, grid=(m // bm, n // bn, nk), in_specs=in_specs, out_specs=o_spec,
        out_shape=jax.ShapeDtypeStruct((m, n), out_dtype), scratch_shapes=[pltpu.VMEM((bm, bn), F32)],
        compiler_params=_params("parallel", "parallel", "arbitrary"),
    )(*args)


def _rms_fwd(x, g, name, out_dtype=BF16):
    s, d = x.shape
    bs = _pick(s, (512, 256, 128))

    def body(x_ref, g_ref, o_ref):
        xv = x_ref[...]
        r = lax.rsqrt(jnp.mean(xv * xv, axis=-1, keepdims=True) + NORM_EPS)
        o_ref[...] = ((xv * r) * g_ref[...]).astype(out_dtype)

    return pl.pallas_call(
        body, name=name, grid=(s // bs,),
        in_specs=[pl.BlockSpec((bs, d), lambda i: (i, 0)), pl.BlockSpec((1, d), lambda i: (0, 0))],
        out_specs=pl.BlockSpec((bs, d), lambda i: (i, 0)), out_shape=jax.ShapeDtypeStruct((s, d), out_dtype),
        compiler_params=_params("parallel"),
    )(x, g.reshape(1, d))


def _rms_bwd(x, g, dy, name, dres=None):
    s, d = x.shape
    bs = _pick(s, (512, 256, 128))
    has_res = dres is not None

    def body(*refs):
        if has_res:
            x_ref, g_ref, dy_ref, r_ref, dx_ref, dg_ref = refs
        else:
            x_ref, g_ref, dy_ref, dx_ref, dg_ref = refs
        i = pl.program_id(0)
        xv = x_ref[...]
        dy = dy_ref[...]
        r = lax.rsqrt(jnp.mean(xv * xv, axis=-1, keepdims=True) + NORM_EPS)
        xh = xv * r
        dxh = dy * g_ref[...]
        dx = r * (dxh - xh * jnp.mean(dxh * xh, axis=-1, keepdims=True))
        if has_res:
            dx = dx + r_ref[...]
        dx_ref[...] = dx

        @pl.when(i == 0)
        def _():
            dg_ref[...] = jnp.zeros_like(dg_ref)

        dg_ref[...] += jnp.sum(dy * xh, axis=0, keepdims=True)

    row = pl.BlockSpec((bs, d), lambda i: (i, 0))
    vec = pl.BlockSpec((1, d), lambda i: (0, 0))
    args = (x, g.reshape(1, d), dy) + ((dres,) if has_res else ())
    dx, dg = pl.pallas_call(
        body, name=name, grid=(s // bs,), in_specs=[row, vec, row] + ([row] if has_res else []),
        out_specs=(row, vec), out_shape=(jax.ShapeDtypeStruct((s, d), F32), jax.ShapeDtypeStruct((1, d), F32)),
        compiler_params=_params("arbitrary"),
    )(*args)
    return dx, dg.reshape(d)


def _rope_tables(positions, dh, offset, period):
    role = np.zeros(LANES, np.int32)
    for base in range(0, LANES, period):
        role[base + offset:base + offset + dh // 2] = 1
        role[base + offset + dh // 2:base + offset + dh] = 2
    inv_freq = ROPE_THETA ** (-jnp.arange(0, dh, 2, dtype=F32) / dh)
    one_period = jnp.concatenate([jnp.zeros((offset,), F32), inv_freq, inv_freq,
                                  jnp.zeros((period - offset - dh,), F32)])
    ang = positions.astype(F32)[:, None] * jnp.tile(one_period, LANES // period)[None, :]
    c, s = jnp.cos(ang), jnp.sin(ang)
    role = role[None, :]
    a = jnp.where(role == 0, 1.0, c).astype(F32)
    bm = jnp.where(role == 2, s, 0.0).astype(F32)
    bp = jnp.where(role == 1, -s, 0.0).astype(F32)
    return a, bm, bp


def _rope_apply(x, tabs, half, transpose, name):
    s, w = x.shape
    bs = _pick(s, (512, 256, 128))
    nc = w // LANES
    a, bm, bp = tabs

    def body(x_ref, a_ref, bm_ref, bp_ref, o_ref):
        av, bmv, bpv = a_ref[...], bm_ref[...], bp_ref[...]
        for c in range(nc):
            sl = slice(c * LANES, (c + 1) * LANES)
            xv = x_ref[:, sl]
            if transpose:
                o_ref[:, sl] = xv * av + pltpu.roll(xv * bmv, LANES - half, 1) + pltpu.roll(xv * bpv, half, 1)
            else:
                o_ref[:, sl] = xv * av + pltpu.roll(xv, half, 1) * bmv + pltpu.roll(xv, LANES - half, 1) * bpv

    row = pl.BlockSpec((bs, w), lambda i: (i, 0))
    tab = pl.BlockSpec((bs, LANES), lambda i: (i, 0))
    return pl.pallas_call(
        body, name=name, grid=(s // bs,), in_specs=[row, tab, tab, tab], out_specs=row,
        out_shape=jax.ShapeDtypeStruct((s, w), F32), compiler_params=_params("parallel"),
    )(x, a, bm, bp)


def _make_rope(half, name):
    @jax.custom_vjp
    def rope(x, a, bm, bp):
        return _rope_apply(x, (a, bm, bp), half, False, name + "_fwd")

    def fwd(x, a, bm, bp):
        return rope(x, a, bm, bp), (a, bm, bp)

    def bwd(tabs, dy):
        return _rope_apply(dy, tabs, half, True, name + "_bwd"), None, None, None

    rope.defvjp(fwd, bwd)
    return rope


class AttnCfg:
    def __init__(self, mode, scale, hpb, bq, bk, upb, max_dist=0):
        self.mode, self.scale, self.hpb, self.bq, self.bk, self.upb, self.max_dist = mode, scale, hpb, bq, bk, upb, max_dist


def _kv_of_q(cfg, nq, nk):
    if cfg.mode == "causal":
        return nk, lambda i, j: (jnp.minimum(j, i), j <= i)
    if cfg.mode == "band":
        return 2, lambda i, j: (jnp.maximum(i - 1 + j, 0), i - 1 + j >= 0)
    return nk, lambda i, j: (j, j >= 0)


def _q_of_kv(cfg, nq, nk):
    if cfg.mode == "causal":
        return nq, lambda kb, j: (jnp.maximum(j, kb), j >= kb)
    if cfg.mode == "band":
        return 2, lambda kb, j: (jnp.minimum(kb + j, nq - 1), kb + j <= nq - 1)
    return nq, lambda kb, j: (j, j >= 0)


def _attn_mask(cfg, i, kb):
    if cfg.mode == "full":
        return None
    qpos = i * cfg.bq + lax.broadcasted_iota(jnp.int32, (cfg.bq, cfg.bk), 0)
    kpos = kb * cfg.bk + lax.broadcasted_iota(jnp.int32, (cfg.bq, cfg.bk), 1)
    dist = qpos - kpos
    if cfg.mode == "causal":
        return dist >= 0
    return (dist >= 0) & (dist <= cfg.max_dist)


def _lane_masks():
    lane = lax.broadcasted_iota(jnp.int32, (1, LANES), 1)
    lo = lane < HEAD_DIM
    return [lo, jnp.logical_not(lo)]


def _sel(mask, v):
    return jnp.where(mask, v, jnp.zeros_like(v))


_NT = (((1,), (1,)), ((), ()))
_NN = (((1,), (0,)), ((), ()))
_TN = (((0,), (0,)), ((), ()))


def _dot(a, b, dims):
    return lax.dot_general(a, b, dims, preferred_element_type=F32)


def _attn_fwd(q, k, v, sinkrow, cfg, name):
    sq, w = q.shape
    sk = k.shape[0]
    bq, bk, upb, hpb = cfg.bq, cfg.bk, cfg.upb, cfg.hpb
    nq, nk, nub = sq // bq, sk // bk, w // (LANES * upb)
    nj, sched = _kv_of_q(cfg, nq, nk)
    wb = LANES * upb
    has_sink = sinkrow is not None

    def body(*refs):
        if has_sink:
            q_ref, k_ref, v_ref, s_ref, o_ref, l_ref, m_sc, l_sc, acc_sc = refs
        else:
            q_ref, k_ref, v_ref, o_ref, l_ref, m_sc, l_sc, acc_sc = refs
        i, j = pl.program_id(1), pl.program_id(2)
        kb, active = sched(i, j)
        lms = _lane_masks()

        @pl.when(j == 0)
        def _():
            for u in range(upb):
                for a in range(hpb):
                    if has_sink:
                        srow = s_ref[:, u * LANES:(u + 1) * LANES]
                        sk_a = jnp.max(jnp.where(lms[a], srow, -jnp.inf), axis=-1, keepdims=True)
                        m_sc[u * hpb + a] = jnp.broadcast_to(sk_a, (bq, 1))
                        l_sc[u * hpb + a] = jnp.ones((bq, 1), F32)
                    else:
                        m_sc[u * hpb + a] = jnp.full((bq, 1), NEG_INIT, F32)
                        l_sc[u * hpb + a] = jnp.zeros((bq, 1), F32)
            acc_sc[...] = jnp.zeros_like(acc_sc)

        @pl.when(active)
        def _():
            mask = _attn_mask(cfg, i, kb)
            for u in range(upb):
                sl = slice(u * LANES, (u + 1) * LANES)
                qv = q_ref[:, sl].astype(BF16)
                kv = k_ref[:, sl].astype(BF16)
                vv = v_ref[:, sl].astype(BF16)
                pv_tot, alphas = None, []
                for a in range(hpb):
                    idx = u * hpb + a
                    qa = _sel(lms[a], qv) if hpb == 2 else qv
                    s = _dot(qa, kv, _NT) * cfg.scale
                    if mask is not None:
                        s = jnp.where(mask, s, NEG_MASK)
                    m_prev = m_sc[idx]
                    m_new = jnp.maximum(m_prev, jnp.max(s, axis=-1, keepdims=True))
                    alpha = jnp.exp(m_prev - m_new)
                    p = jnp.exp(s - m_new)
                    l_sc[idx] = alpha * l_sc[idx] + jnp.sum(p, axis=-1, keepdims=True)
                    m_sc[idx] = m_new
                    va = _sel(lms[a], vv) if hpb == 2 else vv
                    pv = _dot(p.astype(BF16), va, _NN)
                    pv_tot = pv if pv_tot is None else pv_tot + pv
                    alphas.append(alpha)
                af = alphas[0] if hpb == 1 else jnp.where(lms[0], alphas[0], alphas[1])
                acc_sc[u] = acc_sc[u] * af + pv_tot

        @pl.when(j == nj - 1)
        def _():
            for u in range(upb):
                sl = slice(u * LANES, (u + 1) * LANES)
                if hpb == 1:
                    lf = jnp.broadcast_to(l_sc[u], (bq, LANES))
                    mf = jnp.broadcast_to(m_sc[u], (bq, LANES))
                else:
                    lf = jnp.where(lms[0], l_sc[2 * u], l_sc[2 * u + 1])
                    mf = jnp.where(lms[0], m_sc[2 * u], m_sc[2 * u + 1])
                o_ref[:, sl] = acc_sc[u] / lf
                l_ref[:, sl] = mf + jnp.log(lf)

    qspec = pl.BlockSpec((bq, wb), lambda ub, i, j: (i, ub))
    kspec = pl.BlockSpec((bk, wb), lambda ub, i, j: (sched(i, j)[0], ub))
    in_specs = [qspec, kspec, kspec] + ([pl.BlockSpec((1, wb), lambda ub, i, j: (0, ub))] if has_sink else [])
    args = (q, k, v) + ((sinkrow,) if has_sink else ())
    return pl.pallas_call(
        body, name=name, grid=(nub, nq, nj), in_specs=in_specs, out_specs=(qspec, qspec),
        out_shape=(jax.ShapeDtypeStruct((sq, w), F32), jax.ShapeDtypeStruct((sq, w), F32)),
        scratch_shapes=[pltpu.VMEM((upb * hpb, bq, 1), F32), pltpu.VMEM((upb * hpb, bq, 1), F32),
                        pltpu.VMEM((upb, bq, LANES), F32)],
        compiler_params=_params("parallel", "parallel", "arbitrary"),
    )(*args)


def _softmax_grad_terms(cfg, lms, a, qv, kv, vv, dob, prod, lv, mask):
    hpb = cfg.hpb
    if hpb == 2:
        t = jnp.sum(_sel(lms[a], prod), axis=-1, keepdims=True)
        lse = jnp.max(jnp.where(lms[a], lv, -jnp.inf), axis=-1, keepdims=True)
        qa, doa = _sel(lms[a], qv), _sel(lms[a], dob)
    else:
        t = jnp.sum(prod, axis=-1, keepdims=True)
        lse = jnp.max(lv, axis=-1, keepdims=True)
        qa, doa = qv, dob
    s = _dot(qa, kv, _NT) * cfg.scale
    if mask is not None:
        s = jnp.where(mask, s, NEG_MASK)
    p = jnp.exp(s - lse)
    dp = _dot(doa, vv, _NT)
    ds = (p * (dp - t)) * cfg.scale
    return p, ds, qa, doa, t


def _attn_dq(q, k, v, o, lse, do, sinkrow, cfg, name):
    sq, w = q.shape
    sk = k.shape[0]
    bq, bk, upb, hpb = cfg.bq, cfg.bk, cfg.upb, cfg.hpb
    nq, nk, nub = sq // bq, sk // bk, w // (LANES * upb)
    nj, sched = _kv_of_q(cfg, nq, nk)
    wb = LANES * upb
    has_sink = sinkrow is not None

    def body(*refs):
        if has_sink:
            q_ref, k_ref, v_ref, o_ref, l_ref, do_ref, s_ref, dq_ref, dsink_ref, acc = refs
        else:
            q_ref, k_ref, v_ref, o_ref, l_ref, do_ref, dq_ref, acc = refs
        i, j = pl.program_id(1), pl.program_id(2)
        kb, active = sched(i, j)
        lms = _lane_masks()

        @pl.when(j == 0)
        def _():
            acc[...] = jnp.zeros_like(acc)

        @pl.when(active)
        def _():
            mask = _attn_mask(cfg, i, kb)
            for u in range(upb):
                sl = slice(u * LANES, (u + 1) * LANES)
                qv = q_ref[:, sl].astype(BF16)
                kv = k_ref[:, sl].astype(BF16)
                vv = v_ref[:, sl].astype(BF16)
                dov = do_ref[:, sl]
                prod = dov * o_ref[:, sl]
                dob = dov.astype(BF16)
                lv = l_ref[:, sl]
                tot = None
                for a in range(hpb):
                    _, ds, _, _, _ = _softmax_grad_terms(cfg, lms, a, qv, kv, vv, dob, prod, lv, mask)
                    ka = _sel(lms[a], kv) if hpb == 2 else kv
                    c = _dot(ds.astype(BF16), ka, _NN)
                    tot = c if tot is None else tot + c
                acc[u] = acc[u] + tot

        @pl.when(j == nj - 1)
        def _():
            for u in range(upb):
                dq_ref[:, u * LANES:(u + 1) * LANES] = acc[u]
            if has_sink:
                @pl.when(i == 0)
                def _():
                    dsink_ref[...] = jnp.zeros_like(dsink_ref)

                for u in range(upb):
                    sl = slice(u * LANES, (u + 1) * LANES)
                    prod = do_ref[:, sl] * o_ref[:, sl]
                    t0 = jnp.sum(_sel(lms[0], prod), axis=-1, keepdims=True)
                    t1 = jnp.sum(_sel(lms[1], prod), axis=-1, keepdims=True)
                    tf = jnp.where(lms[0], t0, t1)
                    rs = -jnp.exp(s_ref[:, sl] - l_ref[:, sl]) * tf
                    dsink_ref[0:1, sl] += jnp.sum(rs, axis=0, keepdims=True)

    qspec = pl.BlockSpec((bq, wb), lambda ub, i, j: (i, ub))
    kspec = pl.BlockSpec((bk, wb), lambda ub, i, j: (sched(i, j)[0], ub))
    in_specs = [qspec, kspec, kspec, qspec, qspec, qspec]
    args = (q, k, v, o, lse, do)
    out_specs = qspec
    out_shape = jax.ShapeDtypeStruct((sq, w), F32)
    sem = ("parallel", "parallel", "arbitrary")
    if has_sink:
        in_specs = in_specs + [pl.BlockSpec((1, wb), lambda ub, i, j: (0, ub))]
        args = args + (sinkrow,)
        out_specs = (qspec, pl.BlockSpec((8, wb), lambda ub, i, j: (0, ub)))
        out_shape = (out_shape, jax.ShapeDtypeStruct((8, w), F32))
        sem = ("parallel", "arbitrary", "arbitrary")
    return pl.pallas_call(
        body, name=name, grid=(nub, nq, nj), in_specs=in_specs, out_specs=out_specs, out_shape=out_shape,
        scratch_shapes=[pltpu.VMEM((upb, bq, LANES), F32)], compiler_params=_params(*sem),
    )(*args)


def _attn_dkv(q, k, v, o, lse, do, cfg, name):
    sq, w = q.shape
    sk = k.shape[0]
    bq, bk, upb, hpb = cfg.bq, cfg.bk, cfg.upb, cfg.hpb
    nq, nk, nub = sq // bq, sk // bk, w // (LANES * upb)
    nj, sched = _q_of_kv(cfg, nq, nk)
    wb = LANES * upb

    def body(q_ref, k_ref, v_ref, o_ref, l_ref, do_ref, dk_ref, dv_ref, dk_acc, dv_acc):
        kb, j = pl.program_id(1), pl.program_id(2)
        i, active = sched(kb, j)
        lms = _lane_masks()

        @pl.when(j == 0)
        def _():
            dk_acc[...] = jnp.zeros_like(dk_acc)
            dv_acc[...] = jnp.zeros_like(dv_acc)

        @pl.when(active)
        def _():
            mask = _attn_mask(cfg, i, kb)
            for u in range(upb):
                sl = slice(u * LANES, (u + 1) * LANES)
                qv = q_ref[:, sl].astype(BF16)
                kv = k_ref[:, sl].astype(BF16)
                vv = v_ref[:, sl].astype(BF16)
                dov = do_ref[:, sl]
                prod = dov * o_ref[:, sl]
                dob = dov.astype(BF16)
                lv = l_ref[:, sl]
                dk_tot, dv_tot = None, None
                for a in range(hpb):
                    p, ds, qa, doa, _ = _softmax_grad_terms(cfg, lms, a, qv, kv, vv, dob, prod, lv, mask)
                    dvc = _dot(p.astype(BF16), doa, _TN)
                    dkc = _dot(ds.astype(BF16), qa, _TN)
                    dv_tot = dvc if dv_tot is None else dv_tot + dvc
                    dk_tot = dkc if dk_tot is None else dk_tot + dkc
                dk_acc[u] = dk_acc[u] + dk_tot
                dv_acc[u] = dv_acc[u] + dv_tot

        @pl.when(j == nj - 1)
        def _():
            for u in range(upb):
                sl = slice(u * LANES, (u + 1) * LANES)
                dk_ref[:, sl] = dk_acc[u]
                dv_ref[:, sl] = dv_acc[u]

    qspec = pl.BlockSpec((bq, wb), lambda ub, kb, j: (sched(kb, j)[0], ub))
    kspec = pl.BlockSpec((bk, wb), lambda ub, kb, j: (kb, ub))
    return pl.pallas_call(
        body, name=name, grid=(nub, nk, nj), in_specs=[qspec, kspec, kspec, qspec, qspec, qspec],
        out_specs=(kspec, kspec),
        out_shape=(jax.ShapeDtypeStruct((sk, w), F32), jax.ShapeDtypeStruct((sk, w), F32)),
        scratch_shapes=[pltpu.VMEM((upb, bk, LANES), F32), pltpu.VMEM((upb, bk, LANES), F32)],
        compiler_params=_params("parallel", "parallel", "arbitrary"),
    )(q, k, v, o, lse, do)


def _make_attention(cfg, name, with_sink=False):
    if with_sink:
        @jax.custom_vjp
        def attn(q, k, v, sinks):
            return _attn_fwd(q, k, v, _sink_row(sinks), cfg, name + "_fwd")[0]

        def fwd(q, k, v, sinks):
            o, lse = _attn_fwd(q, k, v, _sink_row(sinks), cfg, name + "_fwd")
            return o, (q, k, v, o, lse, sinks)

        def bwd(res, do):
            q, k, v, o, lse, sinks = res
            dq, dsink = _attn_dq(q, k, v, o, lse, do, _sink_row(sinks), cfg, name + "_dq")
            dk, dv = _attn_dkv(q, k, v, o, lse, do, cfg, name + "_dkv")
            return dq, dk, dv, dsink[0].reshape(-1, HEAD_DIM)[:, 0]
    else:
        @jax.custom_vjp
        def attn(q, k, v):
            return _attn_fwd(q, k, v, None, cfg, name + "_fwd")[0]

        def fwd(q, k, v):
            o, lse = _attn_fwd(q, k, v, None, cfg, name + "_fwd")
            return o, (q, k, v, o, lse)

        def bwd(res, do):
            q, k, v, o, lse = res
            dq = _attn_dq(q, k, v, o, lse, do, None, cfg, name + "_dq")
            dk, dv = _attn_dkv(q, k, v, o, lse, do, cfg, name + "_dkv")
            return dq, dk, dv

    attn.defvjp(fwd, bwd)
    return attn


def _sink_row(sinks):
    return jnp.repeat(sinks.astype(F32), HEAD_DIM).reshape(1, -1)


def _merge3(os_, ls_, name):
    s, w = os_[0].shape
    bs = _pick(s, (256, 128))

    def body(o1, o2, o3, l1, l2, l3, out_ref, lse_ref):
        a1, a2, a3 = l1[...], l2[...], l3[...]
        m = jnp.maximum(jnp.maximum(a1, a2), a3)
        e1, e2, e3 = jnp.exp(a1 - m), jnp.exp(a2 - m), jnp.exp(a3 - m)
        z = e1 + e2 + e3
        out_ref[...] = (e1 * o1[...] + e2 * o2[...] + e3 * o3[...]) / z
        lse_ref[...] = m + jnp.log(z)

    row = pl.BlockSpec((bs, w), lambda i: (i, 0))
    return pl.pallas_call(
        body, name=name, grid=(s // bs,), in_specs=[row] * 6, out_specs=(row, row),
        out_shape=(jax.ShapeDtypeStruct((s, w), F32), jax.ShapeDtypeStruct((s, w), F32)),
        compiler_params=_params("parallel"),
    )(*os_, *ls_)


def _add3(a, b, c, name):
    s, w = a.shape
    bs = _pick(s, (512, 256, 128))

    def body(a_ref, b_ref, c_ref, o_ref):
        o_ref[...] = (a_ref[...] + b_ref[...]) + c_ref[...]

    row = pl.BlockSpec((bs, w), lambda i: (i, 0))
    return pl.pallas_call(
        body, name=name, grid=(s // bs,), in_specs=[row] * 3, out_specs=row,
        out_shape=jax.ShapeDtypeStruct((s, w), F32), compiler_params=_params("parallel"),
    )(a, b, c)


def _dil_cfg():
    return AttnCfg("band", HEAD_DIM ** -0.5, 2, BLOCK, BLOCK, 4, max_dist=BLOCK)


def _make_dilated(name):
    cfg = _dil_cfg()

    def view(t, dil):
        s, w = t.shape
        return t.reshape(s // dil, dil * w)

    def forward(q, k, v):
        s, w = q.shape
        os_, ls_ = [], []
        for n, (_, dil) in enumerate(DIL_PATTERNS):
            o, l = _attn_fwd(view(q, dil), view(k, dil), view(v, dil), None, cfg, "%s_b%d_fwd" % (name, n))
            os_.append(o.reshape(s, w))
            ls_.append(l.reshape(s, w))
        return _merge3(os_, ls_, name + "_merge")

    @jax.custom_vjp
    def dilated(q, k, v):
        return forward(q, k, v)[0]

    def fwd(q, k, v):
        out, lse = forward(q, k, v)
        return out, (q, k, v, out, lse)

    def bwd(res, do):
        q, k, v, out, lse = res
        s, w = q.shape
        dqs, dks, dvs = [], [], []
        for n, (_, dil) in enumerate(DIL_PATTERNS):
            args = tuple(view(t, dil) for t in (q, k, v, out, lse, do))
            dqs.append(_attn_dq(*args, None, cfg, "%s_b%d_dq" % (name, n)).reshape(s, w))
            dk, dv = _attn_dkv(*args, cfg, "%s_b%d_dkv" % (name, n))
            dks.append(dk.reshape(s, w))
            dvs.append(dv.reshape(s, w))
        return (_add3(*dqs, name + "_dq_sum"), _add3(*dks, name + "_dk_sum"), _add3(*dvs, name + "_dv_sum"))

    dilated.defvjp(fwd, bwd)
    return dilated


def _make_norm_linear(name):
    @jax.custom_vjp
    def op(x, g, wslot, w):
        return _mm(_rms_fwd(x, g, name + "_norm"), w, "nn", name + "_mm")

    def fwd(x, g, wslot, w):
        h = _rms_fwd(x, g, name + "_norm")
        return _mm(h, w, "nn", name + "_mm"), (x, g, h, w)

    def bwd(res, dz):
        x, g, h, w = res
        dh = _mm(dz, w, "nt", name + "_dh")
        dw = _mm(h, dz, "tn", name + "_dw")
        dx, dg = _rms_bwd(x, g, dh, name + "_norm_bwd")
        return dx, dg, dw, None

    op.defvjp(fwd, bwd)
    return op


def _make_linear_res(name):
    @jax.custom_vjp
    def op(a, wslot, w, res):
        return _mm(a, w, "nn", name + "_mm", res=res)

    def fwd(a, wslot, w, res):
        return _mm(a, w, "nn", name + "_mm", res=res), (a, w)

    def bwd(saved, dout):
        a, w = saved
        da = _mm(dout, w, "nt", name + "_da")
        dw = _mm(a, dout, "tn", name + "_dw")
        return da, dw, None, dout

    op.defvjp(fwd, bwd)
    return op


def _swiglu_fwd(gu, name):
    s, w2 = gu.shape
    hdim = w2 // 2
    bs = _pick(s, (256, 128))

    def body(g_ref, u_ref, a_ref):
        g = g_ref[...]
        a_ref[...] = (g / (1.0 + jnp.exp(-g)) * u_ref[...]).astype(BF16)

    return pl.pallas_call(
        body, name=name, grid=(s // bs,),
        in_specs=[pl.BlockSpec((bs, hdim), lambda i: (i, 0)), pl.BlockSpec((bs, hdim), lambda i: (i, 1))],
        out_specs=pl.BlockSpec((bs, hdim), lambda i: (i, 0)), out_shape=jax.ShapeDtypeStruct((s, hdim), BF16),
        compiler_params=_params("parallel"),
    )(gu, gu)


def _swiglu_bwd_joint(gu, da, name):
    s, w2 = gu.shape
    hdim = w2 // 2
    bs = _pick(s, (256, 128))

    def body(g_ref, u_ref, da_ref, dgu_ref):
        g, u, d = g_ref[...], u_ref[...], da_ref[...]
        sig = 1.0 / (1.0 + jnp.exp(-g))
        dgu_ref[:, :hdim] = (d * u * (sig * (1.0 + g * (1.0 - sig)))).astype(BF16)
        dgu_ref[:, hdim:] = (d * (g * sig)).astype(BF16)

    lo = pl.BlockSpec((bs, hdim), lambda i: (i, 0))
    hi = pl.BlockSpec((bs, hdim), lambda i: (i, 1))
    return pl.pallas_call(
        body, name=name, grid=(s // bs,), in_specs=[lo, hi, lo], out_specs=pl.BlockSpec((bs, w2), lambda i: (i, 0)),
        out_shape=jax.ShapeDtypeStruct((s, w2), BF16), compiler_params=_params("parallel"),
    )(gu, gu, da)


def _make_ffn(name):
    def forward(x, g, wgu, wd):
        h = _rms_fwd(x, g, name + "_norm")
        gu = _mm(h, wgu, "nn", name + "_gu")
        a = _swiglu_fwd(gu, name + "_act")
        return _mm(a, wd, "nn", name + "_down", res=x), (x, g, h, gu, a, wgu, wd)

    @jax.custom_vjp
    def op(x, g, wgu_slot, wd_slot, wgu, wd):
        return forward(x, g, wgu, wd)[0]

    def fwd(x, g, wgu_slot, wd_slot, wgu, wd):
        return forward(x, g, wgu, wd)

    def bwd(saved, dout):
        x, g, h, gu, a, wgu, wd = saved
        da = _mm(dout, wd, "nt", name + "_da")
        dwd = _mm(a, dout, "tn", name + "_dwd")
        dgu = _swiglu_bwd_joint(gu, da, name + "_act_bwd")
        dwgu = _mm(h, dgu, "tn", name + "_dwgu")
        dh = _mm(dgu, wgu, "nt", name + "_dh")
        dx, dg = _rms_bwd(x, g, dh, name + "_norm_bwd", dres=dout)
        return dx, dg, dwgu, dwd, None, None

    op.defvjp(fwd, bwd)
    return op


def _make_final_loss(name):
    def run(x, g, tgt):
        s, d = x.shape
        bs = _pick(s, (512, 256, 128))

        def body(x_ref, g_ref, t_ref, loss_ref, dx_ref, dg_ref):
            i = pl.program_id(0)
            xv = x_ref[...]
            gv = g_ref[...]
            r = lax.rsqrt(jnp.mean(xv * xv, axis=-1, keepdims=True) + NORM_EPS)
            xh = xv * r
            e = xh * gv - t_ref[...]
            dy = e * (1.0 / d)
            dxh = dy * gv
            dx_ref[...] = r * (dxh - xh * jnp.mean(dxh * xh, axis=-1, keepdims=True))
            part = 0.5 * jnp.sum(jnp.sum(e * e, axis=-1, keepdims=True) * (1.0 / d), axis=0, keepdims=True)

            @pl.when(i == 0)
            def _():
                loss_ref[...] = jnp.zeros_like(loss_ref)
                dg_ref[...] = jnp.zeros_like(dg_ref)

            loss_ref[...] += jnp.broadcast_to(part, loss_ref.shape)
            dg_ref[...] += jnp.sum(dy * xh, axis=0, keepdims=True)

        row = pl.BlockSpec((bs, d), lambda i: (i, 0))
        vec = pl.BlockSpec((1, d), lambda i: (0, 0))
        loss, dx, dg = pl.pallas_call(
            body, name=name, grid=(s // bs,), in_specs=[row, vec, row],
            out_specs=(pl.BlockSpec((8, LANES), lambda i: (0, 0)), row, vec),
            out_shape=(jax.ShapeDtypeStruct((8, LANES), F32), jax.ShapeDtypeStruct((s, d), F32),
                       jax.ShapeDtypeStruct((1, d), F32)),
            compiler_params=_params("arbitrary"),
        )(x, g.reshape(1, d), tgt)
        return loss[0, 0], dx, dg.reshape(d)

    @jax.custom_vjp
    def op(x, g, tgt):
        return run(x, g, tgt)[0]

    def fwd(x, g, tgt):
        loss, dx, dg = run(x, g, tgt)
        return loss, (dx, dg)

    def bwd(saved, ct):
        dx, dg = saved
        return dx * ct, dg * ct, None

    op.defvjp(fwd, bwd)
    return op


def _model_loss(diff, consts):
    x = diff["x"]
    w = consts["w"]
    slot = diff["slots"]
    vec = diff["vec"]
    tab64, tab_mla = consts["tab64"], consts["tab_mla"]
    mem = consts["mem"]
    s = x.shape[0]

    rope64 = lambda t, nm: _make_rope(HEAD_DIM // 2, nm)(t, *tab64)
    rope_mla = lambda t, nm: _make_rope(MLA_ROPE_DIM // 2, nm)(t, *tab_mla)

    def nl(nm, inp, gain, wname):
        return _make_norm_linear(nm)(inp, gain, slot[wname], w[wname])

    def cross(layer, xin):
        p = "l%d_" % layer
        q = nl(p + "xq", xin, vec[p + "x_norm"], p + "w_xq")
        kv = nl(p + "xkv", mem, vec[p + "mem_norm"], p + "w_xkv")
        half = X_HEADS * X_HEAD_DIM
        cfg = AttnCfg("full", X_HEAD_DIM ** -0.5, 1, _pick(s, (512, 256, 128)), kv.shape[0], 4)
        o = _make_attention(cfg, p + "xattn")(q, kv[:, :half], kv[:, half:])
        return _make_linear_res(p + "xo")(o, slot[p + "w_xo"], w[p + "w_xo"], xin)

    def ffn(layer, xin):
        p = "l%d_" % layer
        return _make_ffn(p + "ffn")(xin, vec[p + "ffn_norm"], slot[p + "w_gu"], slot[p + "w_down"], w[p + "w_gu"],
                                    w[p + "w_down"])

    z = nl("l0_in", x, vec["l0_mix_norm"], "l0_w_in")
    qa = rope64(z[:, :A_Q], "l0_rope_qa")
    ka = rope64(z[:, A_Q:A_Q + A_KV], "l0_rope_ka")
    va = z[:, A_Q + A_KV:A_Q + 2 * A_KV]
    rep = SWA_HEADS // SWA_KV_HEADS
    expand = lambda t: jnp.broadcast_to(t.reshape(s, SWA_KV_HEADS, 1, HEAD_DIM),
                                        (s, SWA_KV_HEADS, rep, HEAD_DIM)).reshape(s, A_Q)
    swa_cfg = AttnCfg("band", HEAD_DIM ** -0.5, 2, BLOCK, BLOCK, 4, max_dist=SWA_WINDOW - 1)
    oa = _make_attention(swa_cfg, "l0_swa", with_sink=True)(qa, expand(ka), expand(va), vec["l0_sinks"])

    c0 = A_Q + 2 * A_KV
    cq = z[:, c0:c0 + MLA_Q_RANK]
    ckv = z[:, c0 + MLA_Q_RANK:c0 + MLA_Q_RANK + MLA_KV_RANK]
    kr = z[:, c0 + MLA_Q_RANK + MLA_KV_RANK:EVEN_IN]
    qb = nl("l0_uq", cq, vec["l0_q_norm"], "l0_w_uq").reshape(s, MLA_HEADS, MLA_NOPE_DIM + MLA_ROPE_DIM)
    qfull = jnp.pad(qb, ((0, 0), (0, 0), (0, LANES - MLA_NOPE_DIM - MLA_ROPE_DIM))).reshape(s, MLA_HEADS * LANES)
    qfull = rope_mla(qfull, "l0_rope_q")
    kvb = nl("l0_ukv", ckv, vec["l0_kv_norm"], "l0_w_ukv")
    kvb3 = kvb.reshape(s, MLA_HEADS, LANES)
    kfull = jnp.concatenate(
        [kvb3[:, :, :MLA_NOPE_DIM], jnp.broadcast_to(kr[:, None, :], (s, MLA_HEADS, MLA_ROPE_DIM)),
         jnp.zeros((s, MLA_HEADS, LANES - MLA_NOPE_DIM - MLA_ROPE_DIM), F32)], axis=-1).reshape(s, MLA_HEADS * LANES)
    kfull = rope_mla(kfull, "l0_rope_k")
    bq = _pick(s, (512, 256, 128))
    mla_cfg = AttnCfg("causal", (MLA_NOPE_DIM + MLA_ROPE_DIM) ** -0.5, 1, bq, bq, 1)
    ob = _make_attention(mla_cfg, "l0_mla")(qfull, kfull, kvb).reshape(s, MLA_HEADS, LANES)[:, :, MLA_NOPE_DIM:]
    o = jnp.concatenate([oa, ob.reshape(s, MLA_HEADS * HEAD_DIM)], axis=-1)
    x = _make_linear_res("l0_out")(o, slot["l0_w_out"], w["l0_w_out"], x)
    x = cross(0, x)
    x = ffn(0, x)

    qkv = nl("l1_qkv", x, vec["l1_mix_norm"], "l1_w_qkv")
    q = rope64(qkv[:, :D_MODEL], "l1_rope_q")
    k = rope64(qkv[:, D_MODEL:2 * D_MODEL], "l1_rope_k")
    o = _make_dilated("l1_dil")(q, k, qkv[:, 2 * D_MODEL:])
    x = _make_linear_res("l1_out")(o, slot["l1_w_out"], w["l1_w_out"], x)
    x = cross(1, x)
    x = ffn(1, x)

    return _make_final_loss("final_loss")(x, vec["final_norm"], consts["target"])


MESH_IDS = pl.DeviceIdType.MESH
HBM_SPEC = pl.BlockSpec(memory_space=pltpu.HBM)


def _my_place():
    return lax.axis_index("x"), lax.axis_index("y"), lax.axis_index("c")


def _flip(v, bit):
    return 1 - v if bit else v


def _all_gather_rows(shard):
    r, c_ = shard.shape

    def body(x_ref, out_ref, send_sems, recv_sems, local_sem):
        x, y, c = _my_place()
        me, sibling = (x, y, c), (x, y, 1 - c)
        chips = [(1 - x, y), (x, 1 - y), (1 - x, 1 - y)]

        def slot(px, py, pc):
            return out_ref.at[4 * px + 2 * py + pc]

        def copy(k, block, to, src=None):
            return pltpu.make_async_remote_copy(
                src_ref=slot(*block) if src is None else src, dst_ref=slot(*block), send_sem=send_sems.at[k],
                recv_sem=recv_sems.at[k], device_id=to, device_id_type=MESH_IDS)

        mine = pltpu.make_async_copy(x_ref, slot(*me), local_sem)
        mine.start()
        first = [copy(0, me, sibling, src=x_ref)]
        first += [copy(1 + j, me, (*chip, c), src=x_ref) for j, chip in enumerate(chips)]
        for cp in first:
            cp.start()
        passed = [copy(4 + j, (*chip, c), sibling) for j, chip in enumerate(chips)]
        for j, chip in enumerate(chips):
            copy(1 + j, (*chip, c), me).wait_recv()
            passed[j].start()
        copy(0, sibling, me).wait_recv()
        for j, chip in enumerate(chips):
            copy(4 + j, (*chip, 1 - c), me).wait_recv()
        for cp in first + passed:
            cp.wait_send()
        mine.wait()

    return pl.pallas_call(
        body, name="weights_all_gather", out_shape=jax.ShapeDtypeStruct((N_DEV, r, c_), shard.dtype),
        in_specs=[HBM_SPEC], out_specs=HBM_SPEC,
        scratch_shapes=[pltpu.SemaphoreType.DMA((7,)), pltpu.SemaphoreType.DMA((7,)), pltpu.SemaphoreType.DMA],
    )(shard)


def _exchange_slabs(slabs):
    _, r, c_ = slabs.shape

    def body(p_ref, out_ref, send_sems, recv_sems, local_sem):
        x, y, c = _my_place()
        me = 4 * x + 2 * y + c
        local = pltpu.make_async_copy(p_ref.at[me], out_ref.at[me], local_sem)
        local.start()
        sends, recvs = [], []
        for k in range(1, N_DEV):
            px, py, pc = _flip(x, k & 4), _flip(y, k & 2), _flip(c, k & 1)
            peer = 4 * px + 2 * py + pc
            sends.append(pltpu.make_async_remote_copy(
                src_ref=p_ref.at[peer], dst_ref=out_ref.at[me], send_sem=send_sems.at[k - 1],
                recv_sem=recv_sems.at[k - 1], device_id=(px, py, pc), device_id_type=MESH_IDS))
            recvs.append(pltpu.make_async_remote_copy(
                src_ref=p_ref.at[me], dst_ref=out_ref.at[peer], send_sem=send_sems.at[k - 1],
                recv_sem=recv_sems.at[k - 1], device_id=(px, py, pc), device_id_type=MESH_IDS))
        for cp in sends:
            cp.start()
        for cp in recvs:
            cp.wait_recv()
        for cp in sends:
            cp.wait_send()
        local.wait()

    return pl.pallas_call(
        body, name="grad_slab_exchange", out_shape=jax.ShapeDtypeStruct(slabs.shape, slabs.dtype),
        in_specs=[HBM_SPEC], out_specs=HBM_SPEC,
        scratch_shapes=[pltpu.SemaphoreType.DMA((7,)), pltpu.SemaphoreType.DMA((7,)), pltpu.SemaphoreType.DMA],
    )(slabs)


def _all_reduce_small(v):
    r, c_ = v.shape

    def body(v_ref, out_ref, buf, send_sems, recv_sems):
        x, y, c = _my_place()
        me = 4 * x + 2 * y + c
        buf[me] = v_ref[...]
        sends, recvs = [], []
        for k in range(1, N_DEV):
            px, py, pc = _flip(x, k & 4), _flip(y, k & 2), _flip(c, k & 1)
            peer = 4 * px + 2 * py + pc
            sends.append(pltpu.make_async_remote_copy(
                src_ref=v_ref, dst_ref=buf.at[me], send_sem=send_sems.at[k - 1], recv_sem=recv_sems.at[k - 1],
                device_id=(px, py, pc), device_id_type=MESH_IDS))
            recvs.append(pltpu.make_async_remote_copy(
                src_ref=v_ref, dst_ref=buf.at[peer], send_sem=send_sems.at[k - 1], recv_sem=recv_sems.at[k - 1],
                device_id=(px, py, pc), device_id_type=MESH_IDS))
        for cp in sends:
            cp.start()
        for cp in recvs:
            cp.wait_recv()
        for cp in sends:
            cp.wait_send()
        acc = buf[0]
        for d in range(1, N_DEV):
            acc = acc + buf[d]
        out_ref[...] = acc

    vm = pl.BlockSpec(memory_space=pltpu.VMEM)
    return pl.pallas_call(
        body, name="vector_grad_all_reduce", out_shape=jax.ShapeDtypeStruct((r, c_), F32), in_specs=[vm], out_specs=vm,
        scratch_shapes=[pltpu.VMEM((N_DEV, r, c_), F32), pltpu.SemaphoreType.DMA((7,)), pltpu.SemaphoreType.DMA((7,))],
    )(v)


def _adamw_math(w, g, m, v):
    m = ADAM_B1 * m + (1.0 - ADAM_B1) * g
    v = ADAM_B2 * v + (1.0 - ADAM_B2) * (g * g)
    m_hat = m / (1.0 - ADAM_B1 ** ADAM_STEP)
    v_hat = v / (1.0 - ADAM_B2 ** ADAM_STEP)
    delta = -ADAM_LR * (m_hat / (jnp.sqrt(v_hat) + ADAM_EPS) + ADAM_WD * w)
    return delta, m, v


def _sum_and_adamw(parts, w, m, v):
    _, r, c_ = parts.shape
    br = _pick(r, (160, 128, 64, 32, 16, 8))

    def body(p_ref, w_ref, m_ref, v_ref, g_ref, d_ref, nm_ref, nv_ref):
        g = p_ref[0]
        for d in range(1, N_DEV):
            g = g + p_ref[d]
        g_ref[...] = g
        d_ref[...], nm_ref[...], nv_ref[...] = _adamw_math(w_ref[...], g, m_ref[...], v_ref[...])

    row = pl.BlockSpec((br, c_), lambda i: (i, 0))
    return pl.pallas_call(
        body, name="grad_sum_adamw", grid=(r // br,),
        in_specs=[pl.BlockSpec((N_DEV, br, c_), lambda i: (0, i, 0)), row, row, row], out_specs=(row,) * 4,
        out_shape=(jax.ShapeDtypeStruct((r, c_), F32),) * 4, compiler_params=_params("parallel"),
    )(parts, w, m, v)


def _adamw_small(w, g, m, v):
    vm = pl.BlockSpec(memory_space=pltpu.VMEM)

    def body(w_ref, g_ref, m_ref, v_ref, d_ref, nm_ref, nv_ref):
        d_ref[...], nm_ref[...], nv_ref[...] = _adamw_math(w_ref[...], g_ref[...], m_ref[...], v_ref[...])

    return pl.pallas_call(
        body, name="vector_adamw", in_specs=[vm] * 4, out_specs=(vm,) * 3,
        out_shape=(jax.ShapeDtypeStruct(w.shape, F32),) * 3,
    )(w, g, m, v)


def _pack_local(named):
    rows = [named[n].reshape(-1, PACK_COLS) for n, _, _, _ in MATRICES]
    rows.append(jnp.zeros((MAT_ROWS_PAD - MAT_ROWS, PACK_COLS), rows[0].dtype))
    return jnp.concatenate(rows, axis=0)


def _unpack_local(packed):
    out, r0 = {}, 0
    for n, kind, k, nn in MATRICES:
        nr = k * nn // N_DEV // PACK_COLS
        shape = (k, nn // N_DEV) if kind == "c" else (k // N_DEV, nn)
        out[n] = packed[r0:r0 + nr].reshape(shape)
        r0 += nr
    return out


def _unpack_gathered(g):
    out, r0 = {}, 0
    for n, kind, k, nn in MATRICES:
        nr = k * nn // N_DEV // PACK_COLS
        blk = g[:, r0:r0 + nr]
        if kind == "c":
            out[n] = blk.reshape(N_DEV, k, nn // N_DEV).transpose(1, 0, 2).reshape(k, nn)
        else:
            out[n] = blk.reshape(k, nn)
        r0 += nr
    return out


def _pack_full_grads(grads):
    rows = []
    for n, kind, k, nn in MATRICES:
        gmat = grads[n]
        if kind == "c":
            gmat = gmat.reshape(k, N_DEV, nn // N_DEV).transpose(1, 0, 2)
        rows.append(gmat.reshape(N_DEV, -1, PACK_COLS))
    rows.append(jnp.zeros((N_DEV, MAT_ROWS_PAD - MAT_ROWS, PACK_COLS), F32))
    return jnp.concatenate(rows, axis=1)


def _pack_vectors(named):
    rows = [jnp.pad(named[n].astype(F32), (0, PACK_COLS - d)) for n, d in VECTORS]
    rows += [jnp.zeros((PACK_COLS,), F32)] * (VEC_ROWS - len(VECTORS))
    return jnp.stack(rows, axis=0)


def _unpack_vectors(packed):
    return {n: packed[i, :d] for i, (n, d) in enumerate(VECTORS)}


def _step(inputs):
    x = inputs["x"][0]
    mem = inputs["mem"][0]
    positions = inputs["positions"][0]
    target = inputs["loss_target"][0]

    local_w = _pack_local({n: inputs[n] for n, _, _, _ in MATRICES})
    gathered = _all_gather_rows(local_w.astype(BF16))
    wfull = _unpack_gathered(gathered)
    vec = {n: inputs[n] for n, _ in VECTORS}

    loss_part, grad_x, gfull, gvec = _local_grads(wfull, vec, x, mem, positions, target)
    loss = lax.psum(loss_part, ("x", "y", "c"))

    parts = _exchange_slabs(_pack_full_grads(gfull))
    local_m = _pack_local({n: inputs["m_" + n] for n, _, _, _ in MATRICES})
    local_v = _pack_local({n: inputs["v_" + n] for n, _, _, _ in MATRICES})
    g_pk, d_pk, m_pk, v_pk = _sum_and_adamw(parts, local_w, local_m, local_v)
    g_mat, d_mat, m_mat, v_mat = (_unpack_local(t) for t in (g_pk, d_pk, m_pk, v_pk))

    g_vec_pk = _all_reduce_small(_pack_vectors(gvec))
    d_vec_pk, m_vec_pk, v_vec_pk = _adamw_small(
        _pack_vectors(vec), g_vec_pk, _pack_vectors({n: inputs["m_" + n] for n, _ in VECTORS}),
        _pack_vectors({n: inputs["v_" + n] for n, _ in VECTORS}))
    g_vec, d_vec, m_vec, v_vec = (_unpack_vectors(t) for t in (g_vec_pk, d_vec_pk, m_vec_pk, v_vec_pk))

    def pick(mats, vecs, n):
        return mats[n] if n in mats else vecs[n]

    outs = [loss, grad_x[None]]
    for mats, vecs in ((g_mat, g_vec), (d_mat, d_vec), (m_mat, m_vec), (v_mat, v_vec)):
        outs += [pick(mats, vecs, n) for n in WEIGHT_ORDER]
    return tuple(outs)


def _local_grads(wfull, vec, x, mem, positions, target):
    w = {}
    for n, _, _, _ in MATRICES:
        if n.endswith("w_gate") or n.endswith("w_up"):
            continue
        w[n] = wfull[n]
    w["l0_w_in"] = jnp.pad(wfull["l0_w_in"], ((0, 0), (0, EVEN_IN_PAD - EVEN_IN)))
    for layer in (0, 1):
        p = "l%d_" % layer
        w[p + "w_gu"] = jnp.concatenate([wfull[p + "w_gate"], wfull[p + "w_up"]], axis=1)
    slots = {n: jnp.zeros(t.shape, F32) for n, t in w.items()}

    tab64 = _rope_tables(positions, HEAD_DIM, 0, HEAD_DIM)
    tab_mla = _rope_tables(positions, MLA_ROPE_DIM, MLA_NOPE_DIM, LANES)
    diff = {"x": x, "slots": slots, "vec": vec}
    consts = {"w": w, "mem": mem, "tab64": tab64, "tab_mla": tab_mla, "target": target}
    loss_part, grads = jax.value_and_grad(_model_loss)(diff, consts)

    gfull = dict(grads["slots"])
    gfull["l0_w_in"] = gfull["l0_w_in"][:, :EVEN_IN]
    for layer in (0, 1):
        p = "l%d_" % layer
        gu = gfull.pop(p + "w_gu")
        gfull[p + "w_gate"], gfull[p + "w_up"] = gu[:, :FFN_HIDDEN], gu[:, FFN_HIDDEN:]
    return loss_part, grads["x"], gfull, grads["vec"]


_INPUT_NAMES = (("x", "mem", "positions") + WEIGHT_ORDER + ("loss_target",)
                + tuple("m_" + n for n in WEIGHT_ORDER) + tuple("v_" + n for n in WEIGHT_ORDER))


def kernel(*args):
    assert len(args) == len(_INPUT_NAMES)
    return _step(dict(zip(_INPUT_NAMES, args)))
```

```python
import functools
import math

import numpy as np
import jax
import jax.numpy as jnp
from jax import lax
from jax.experimental import pallas as pl
from jax.experimental.pallas import tpu as pltpu

F32 = jnp.float32
BF16 = jnp.bfloat16

LANES = 128
VMEM_LIMIT_BYTES = 56 * 1024 * 1024
MM_VMEM_BUDGET = 40 * 1024 * 1024
MM_MIN_FLOP_PER_STEP = 1e9
BAND_UNITS_PER_STEP = 4

D_MODEL = 1024
HEAD_DIM = 64
ROPE_THETA = 10000.0
NORM_EPS = 1e-6
BLOCK = 128
SWA_HEADS = 8
SWA_KV_HEADS = 2
SWA_WINDOW = 128
MLA_HEADS = 8
MLA_Q_RANK = 384
MLA_KV_RANK = 256
MLA_NOPE_DIM = 64
MLA_ROPE_DIM = 32
A_Q = SWA_HEADS * HEAD_DIM
A_KV = SWA_KV_HEADS * HEAD_DIM
EVEN_IN = A_Q + 2 * A_KV + MLA_Q_RANK + MLA_KV_RANK + MLA_ROPE_DIM
EVEN_IN_PAD = 1536
DIL_PATTERNS = ((128, 1), (512, 4), (2048, 16))
X_HEADS = 4
X_HEAD_DIM = 128
FFN_HIDDEN = 2816

ADAM_LR = 0.001
ADAM_B1 = 0.9
ADAM_B2 = 0.999
ADAM_EPS = 1e-08
ADAM_WD = 0.01
ADAM_STEP = 10

N_DEV = 8
NEG_MASK = -1e30
NEG_INIT = -1e20

MATRICES = (
    ("l0_w_in", "c", 1024, 1440), ("l0_w_uq", "c", 384, 768), ("l0_w_ukv", "c", 256, 1024),
    ("l0_w_out", "r", 1024, 1024), ("l0_w_xq", "r", 1024, 512), ("l0_w_xkv", "r", 1024, 1024),
    ("l0_w_xo", "c", 512, 1024), ("l0_w_gate", "c", 1024, 2816), ("l0_w_up", "c", 1024, 2816),
    ("l0_w_down", "r", 2816, 1024),
    ("l1_w_qkv", "c", 1024, 3072), ("l1_w_out", "r", 1024, 1024), ("l1_w_xq", "r", 1024, 512),
    ("l1_w_xkv", "r", 1024, 1024), ("l1_w_xo", "c", 512, 1024), ("l1_w_gate", "c", 1024, 2816),
    ("l1_w_up", "c", 1024, 2816), ("l1_w_down", "r", 2816, 1024),
)
VECTORS = (
    ("l0_mix_norm", 1024), ("l0_sinks", 8), ("l0_q_norm", 384), ("l0_kv_norm", 256), ("l0_x_norm", 1024),
    ("l0_mem_norm", 1024), ("l0_ffn_norm", 1024), ("l1_mix_norm", 1024), ("l1_x_norm", 1024),
    ("l1_mem_norm", 1024), ("l1_ffn_norm", 1024), ("final_norm", 1024),
)
WEIGHT_ORDER = (
    "l0_mix_norm", "l0_w_in", "l0_sinks", "l0_q_norm", "l0_w_uq", "l0_kv_norm", "l0_w_ukv", "l0_w_out", "l0_x_norm",
    "l0_mem_norm", "l0_w_xq", "l0_w_xkv", "l0_w_xo", "l0_ffn_norm", "l0_w_gate", "l0_w_up", "l0_w_down",
    "l1_mix_norm", "l1_w_qkv", "l1_w_out", "l1_x_norm", "l1_mem_norm", "l1_w_xq", "l1_w_xkv", "l1_w_xo",
    "l1_ffn_norm", "l1_w_gate", "l1_w_up", "l1_w_down", "final_norm",
)
PACK_COLS = 1024
MAT_ROWS = sum(k * n // N_DEV // PACK_COLS for _, _, k, n in MATRICES)
MAT_ROWS_PAD = -(-MAT_ROWS // 64) * 64
VEC_ROWS = 16


def _pick(n, cands):
    for c in cands:
        if n % c == 0:
            return c
    return n


def _params(*sem):
    return pltpu.CompilerParams(dimension_semantics=sem, vmem_limit_bytes=VMEM_LIMIT_BYTES)


_DIMS = {"nn": (((1,), (0,)), ((), ())), "nt": (((1,), (1,)), ((), ())), "tn": (((0,), (0,)), ((), ()))}


def _div128(n, cap):
    d = (min(n, cap) // LANES) * LANES
    while d >= LANES:
        if n % d == 0:
            return d
        d -= LANES
    return n


def _mm_vmem_bytes(bm, bn, bk, nk, sa, sb, so, has_res):
    est = 2 * (bm * bk * sa + bk * bn * sb + bm * bn * so) + bm * bn * 4
    est += bm * bn * 4 if nk > 1 else 0
    est += 2 * bm * bn * 4 if has_res else 0
    est += bm * bk * 2 if sa == 4 else 0
    est += bk * bn * 2 if sb == 4 else 0
    return est


def _mm_tiles(m, n, k, sa, sb, so, has_res, mode):
    bn = _div128(n, 1536)
    kcap = 2048 if mode == "tn" else k
    for bm_cap in ((1024, 2048) if mode == "tn" else (512, 1024, 2048)):
        bm = _div128(m, bm_cap)
        bk = (min(k, kcap) // LANES) * LANES
        while bk > LANES and (k % bk or _mm_vmem_bytes(bm, bn, bk, k // bk, sa, sb, so, has_res) > MM_VMEM_BUDGET):
            bk -= LANES
        if 2 * bm * bn * bk >= MM_MIN_FLOP_PER_STEP or bm == m:
            break
    return bm, bn, bk


def _mm(a, b, mode, name, out_dtype=F32, res=None):
    if mode == "nn":
        (m, k), (k2, n) = a.shape, b.shape
    elif mode == "nt":
        (m, k), (n, k2) = a.shape, b.shape
    else:
        (k, m), (k2, n) = a.shape, b.shape
    assert k == k2, (name, a.shape, b.shape)
    has_res = res is not None
    bm, bn, bk = _mm_tiles(m, n, k, a.dtype.itemsize, b.dtype.itemsize, jnp.dtype(out_dtype).itemsize, has_res, mode)
    nk = k // bk
    dims = _DIMS[mode]
    a_spec = pl.BlockSpec((bk, bm), lambda i, j, kk: (kk, i)) if mode == "tn" else pl.BlockSpec((bm, bk), lambda i, j, kk: (i, kk))
    b_spec = pl.BlockSpec((bn, bk), lambda i, j, kk: (j, kk)) if mode == "nt" else pl.BlockSpec((bk, bn), lambda i, j, kk: (kk, j))
    o_spec = pl.BlockSpec((bm, bn), lambda i, j, kk: (i, j))

    def body(*refs):
        a_ref, b_ref = refs[0], refs[1]
        r_ref = refs[2] if has_res else None
        o_ref = refs[3] if has_res else refs[2]
        part = lax.dot_general(a_ref[...].astype(BF16), b_ref[...].astype(BF16), dims, preferred_element_type=F32)
        if nk == 1:
            o_ref[...] = (part + r_ref[...] if has_res else part).astype(out_dtype)
            return
        acc = refs[-1]
        kk = pl.program_id(2)

        @pl.when(kk == 0)
        def _():
            acc[...] = part

        @pl.when(jnp.logical_and(kk > 0, kk < nk - 1))
        def _():
            acc[...] += part

        @pl.when(kk == nk - 1)
        def _():
            r = acc[...] + part
            if has_res:
                r = r + r_ref[...]
            o_ref[...] = r.astype(out_dtype)

    args = (a, b, res) if has_res else (a, b)
    in_specs = [a_spec, b_spec] + ([o_spec] if has_res else [])
    return pl.pallas_call(
        body, name=name, grid=(m // bm, n // bn, nk), in_specs=in_specs, out_specs=o_spec,
        out_shape=jax.ShapeDtypeStruct((m, n), out_dtype),
        scratch_shapes=[pltpu.VMEM((bm, bn), F32)] if nk > 1 else [],
        compiler_params=_params("parallel", "parallel", "arbitrary"),
    )(*args)


def _rms_fwd(x, g, name, out_dtype=BF16):
    s, d = x.shape
    bs = _pick(s, (512, 256, 128))

    def body(x_ref, g_ref, o_ref):
        xv = x_ref[...]
        r = lax.rsqrt(jnp.mean(xv * xv, axis=-1, keepdims=True) + NORM_EPS)
        o_ref[...] = ((xv * r) * g_ref[...]).astype(out_dtype)

    return pl.pallas_call(
        body, name=name, grid=(s // bs,),
        in_specs=[pl.BlockSpec((bs, d), lambda i: (i, 0)), pl.BlockSpec((1, d), lambda i: (0, 0))],
        out_specs=pl.BlockSpec((bs, d), lambda i: (i, 0)), out_shape=jax.ShapeDtypeStruct((s, d), out_dtype),
        compiler_params=_params("parallel"),
    )(x, g.reshape(1, d))


def _rms_bwd(x, g, dy, name, dres=None):
    s, d = x.shape
    bs = _pick(s, (512, 256, 128))
    has_res = dres is not None

    def body(*refs):
        if has_res:
            x_ref, g_ref, dy_ref, r_ref, dx_ref, dg_ref = refs
        else:
            x_ref, g_ref, dy_ref, dx_ref, dg_ref = refs
        i = pl.program_id(0)
        xv = x_ref[...]
        dy = dy_ref[...]
        r = lax.rsqrt(jnp.mean(xv * xv, axis=-1, keepdims=True) + NORM_EPS)
        xh = xv * r
        dxh = dy * g_ref[...]
        dx = r * (dxh - xh * jnp.mean(dxh * xh, axis=-1, keepdims=True))
        if has_res:
            dx = dx + r_ref[...]
        dx_ref[...] = dx

        @pl.when(i == 0)
        def _():
            dg_ref[...] = jnp.zeros_like(dg_ref)

        dg_ref[...] += jnp.sum(dy * xh, axis=0, keepdims=True)

    row = pl.BlockSpec((bs, d), lambda i: (i, 0))
    vec = pl.BlockSpec((1, d), lambda i: (0, 0))
    args = (x, g.reshape(1, d), dy) + ((dres,) if has_res else ())
    dx, dg = pl.pallas_call(
        body, name=name, grid=(s // bs,), in_specs=[row, vec, row] + ([row] if has_res else []),
        out_specs=(row, vec), out_shape=(jax.ShapeDtypeStruct((s, d), F32), jax.ShapeDtypeStruct((1, d), F32)),
        compiler_params=_params("arbitrary"),
    )(*args)
    return dx, dg.reshape(d)


def _rope_tables(positions, dh, offset, period):
    role = np.zeros(LANES, np.int32)
    for base in range(0, LANES, period):
        role[base + offset:base + offset + dh // 2] = 1
        role[base + offset + dh // 2:base + offset + dh] = 2
    inv_freq = ROPE_THETA ** (-jnp.arange(0, dh, 2, dtype=F32) / dh)
    one_period = jnp.concatenate([jnp.zeros((offset,), F32), inv_freq, inv_freq,
                                  jnp.zeros((period - offset - dh,), F32)])
    ang = positions.astype(F32)[:, None] * jnp.tile(one_period, LANES // period)[None, :]
    c, s = jnp.cos(ang), jnp.sin(ang)
    role = role[None, :]
    a = jnp.where(role == 0, 1.0, c).astype(F32)
    bm = jnp.where(role == 2, s, 0.0).astype(F32)
    bp = jnp.where(role == 1, -s, 0.0).astype(F32)
    return a, bm, bp


def _rope_apply(x, tabs, half, transpose, name):
    s, w = x.shape
    bs = _pick(s, (512, 256, 128))
    nc = w // LANES
    a, bm, bp = tabs

    def body(x_ref, a_ref, bm_ref, bp_ref, o_ref):
        av, bmv, bpv = a_ref[...], bm_ref[...], bp_ref[...]
        for c in range(nc):
            sl = slice(c * LANES, (c + 1) * LANES)
            xv = x_ref[:, sl]
            if transpose:
                o_ref[:, sl] = xv * av + pltpu.roll(xv * bmv, LANES - half, 1) + pltpu.roll(xv * bpv, half, 1)
            else:
                o_ref[:, sl] = xv * av + pltpu.roll(xv, half, 1) * bmv + pltpu.roll(xv, LANES - half, 1) * bpv

    row = pl.BlockSpec((bs, w), lambda i: (i, 0))
    tab = pl.BlockSpec((bs, LANES), lambda i: (i, 0))
    return pl.pallas_call(
        body, name=name, grid=(s // bs,), in_specs=[row, tab, tab, tab], out_specs=row,
        out_shape=jax.ShapeDtypeStruct((s, w), F32), compiler_params=_params("parallel"),
    )(x, a, bm, bp)


def _make_rope(half, name):
    @jax.custom_vjp
    def rope(x, a, bm, bp):
        return _rope_apply(x, (a, bm, bp), half, False, name + "_fwd")

    def fwd(x, a, bm, bp):
        return rope(x, a, bm, bp), (a, bm, bp)

    def bwd(tabs, dy):
        return _rope_apply(dy, tabs, half, True, name + "_bwd"), None, None, None

    rope.defvjp(fwd, bwd)
    return rope


class AttnCfg:
    def __init__(self, mode, scale, hpb, bq, bk, upb, max_dist=0):
        self.mode, self.scale, self.hpb, self.bq, self.bk, self.upb, self.max_dist = mode, scale, hpb, bq, bk, upb, max_dist


def _kv_of_q(cfg, nq, nk):
    if cfg.mode == "causal":
        return nk, lambda i, j: (jnp.minimum(j, i), j <= i)
    if cfg.mode == "band":
        return 2, lambda i, j: (jnp.maximum(i - 1 + j, 0), i - 1 + j >= 0)
    return nk, lambda i, j: (j, j >= 0)


def _q_of_kv(cfg, nq, nk):
    if cfg.mode == "causal":
        return nq, lambda kb, j: (jnp.maximum(j, kb), j >= kb)
    if cfg.mode == "band":
        return 2, lambda kb, j: (jnp.minimum(kb + j, nq - 1), kb + j <= nq - 1)
    return nq, lambda kb, j: (j, j >= 0)


def _attn_mask(cfg, i, kb):
    if cfg.mode == "full":
        return None
    qpos = i * cfg.bq + lax.broadcasted_iota(jnp.int32, (cfg.bq, cfg.bk), 0)
    kpos = kb * cfg.bk + lax.broadcasted_iota(jnp.int32, (cfg.bq, cfg.bk), 1)
    dist = qpos - kpos
    if cfg.mode == "causal":
        return dist >= 0
    return (dist >= 0) & (dist <= cfg.max_dist)


def _lane_masks():
    lane = lax.broadcasted_iota(jnp.int32, (1, LANES), 1)
    lo = lane < HEAD_DIM
    return [lo, jnp.logical_not(lo)]


def _sel(mask, v):
    return jnp.where(mask, v, jnp.zeros_like(v))


_NT = (((1,), (1,)), ((), ()))
_NN = (((1,), (0,)), ((), ()))
_TN = (((0,), (0,)), ((), ()))


def _dot(a, b, dims):
    return lax.dot_general(a, b, dims, preferred_element_type=F32)


def _attn_fwd(q, k, v, sinkrow, cfg, name):
    sq, w = q.shape
    sk = k.shape[0]
    bq, bk, upb, hpb = cfg.bq, cfg.bk, cfg.upb, cfg.hpb
    nq, nk, nub = sq // bq, sk // bk, w // (LANES * upb)
    nj, sched = _kv_of_q(cfg, nq, nk)
    wb = LANES * upb
    has_sink = sinkrow is not None

    def body(*refs):
        if has_sink:
            q_ref, k_ref, v_ref, s_ref, o_ref, l_ref, m_sc, l_sc, acc_sc = refs
        else:
            q_ref, k_ref, v_ref, o_ref, l_ref, m_sc, l_sc, acc_sc = refs
        i, j = pl.program_id(1), pl.program_id(2)
        kb, active = sched(i, j)
        lms = _lane_masks()

        @pl.when(j == 0)
        def _():
            for u in range(upb):
                for a in range(hpb):
                    if has_sink:
                        srow = s_ref[:, u * LANES:(u + 1) * LANES]
                        sk_a = jnp.max(jnp.where(lms[a], srow, -jnp.inf), axis=-1, keepdims=True)
                        m_sc[u * hpb + a] = jnp.broadcast_to(sk_a, (bq, 1))
                        l_sc[u * hpb + a] = jnp.ones((bq, 1), F32)
                    else:
                        m_sc[u * hpb + a] = jnp.full((bq, 1), NEG_INIT, F32)
                        l_sc[u * hpb + a] = jnp.zeros((bq, 1), F32)
            acc_sc[...] = jnp.zeros_like(acc_sc)

        @pl.when(active)
        def _():
            mask = _attn_mask(cfg, i, kb)
            for u in range(upb):
                sl = slice(u * LANES, (u + 1) * LANES)
                qv = q_ref[:, sl].astype(BF16)
                kv = k_ref[:, sl].astype(BF16)
                vv = v_ref[:, sl].astype(BF16)
                pv_tot, alphas = None, []
                for a in range(hpb):
                    idx = u * hpb + a
                    qa = _sel(lms[a], qv) if hpb == 2 else qv
                    s = _dot(qa, kv, _NT) * cfg.scale
                    if mask is not None:
                        s = jnp.where(mask, s, NEG_MASK)
                    m_prev = m_sc[idx]
                    m_new = jnp.maximum(m_prev, jnp.max(s, axis=-1, keepdims=True))
                    alpha = jnp.exp(m_prev - m_new)
                    p = jnp.exp(s - m_new)
                    l_sc[idx] = alpha * l_sc[idx] + jnp.sum(p, axis=-1, keepdims=True)
                    m_sc[idx] = m_new
                    va = _sel(lms[a], vv) if hpb == 2 else vv
                    pv = _dot(p.astype(BF16), va, _NN)
                    pv_tot = pv if pv_tot is None else pv_tot + pv
                    alphas.append(alpha)
                af = alphas[0] if hpb == 1 else jnp.where(lms[0], alphas[0], alphas[1])
                acc_sc[u] = acc_sc[u] * af + pv_tot

        @pl.when(j == nj - 1)
        def _():
            for u in range(upb):
                sl = slice(u * LANES, (u + 1) * LANES)
                if hpb == 1:
                    lf = jnp.broadcast_to(l_sc[u], (bq, LANES))
                    mf = jnp.broadcast_to(m_sc[u], (bq, LANES))
                else:
                    lf = jnp.where(lms[0], l_sc[2 * u], l_sc[2 * u + 1])
                    mf = jnp.where(lms[0], m_sc[2 * u], m_sc[2 * u + 1])
                o_ref[:, sl] = acc_sc[u] / lf
                l_ref[:, sl] = mf + jnp.log(lf)

    qspec = pl.BlockSpec((bq, wb), lambda ub, i, j: (i, ub))
    kspec = pl.BlockSpec((bk, wb), lambda ub, i, j: (sched(i, j)[0], ub))
    in_specs = [qspec, kspec, kspec] + ([pl.BlockSpec((1, wb), lambda ub, i, j: (0, ub))] if has_sink else [])
    args = (q, k, v) + ((sinkrow,) if has_sink else ())
    return pl.pallas_call(
        body, name=name, grid=(nub, nq, nj), in_specs=in_specs, out_specs=(qspec, qspec),
        out_shape=(jax.ShapeDtypeStruct((sq, w), F32), jax.ShapeDtypeStruct((sq, w), F32)),
        scratch_shapes=[pltpu.VMEM((upb * hpb, bq, 1), F32), pltpu.VMEM((upb * hpb, bq, 1), F32),
                        pltpu.VMEM((upb, bq, LANES), F32)],
        compiler_params=_params("parallel", "parallel", "arbitrary"),
    )(*args)


def _softmax_grad_terms(cfg, lms, a, qv, kv, vv, dob, prod, lv, mask):
    hpb = cfg.hpb
    if hpb == 2:
        t = jnp.sum(_sel(lms[a], prod), axis=-1, keepdims=True)
        lse = jnp.max(jnp.where(lms[a], lv, -jnp.inf), axis=-1, keepdims=True)
        qa, doa = _sel(lms[a], qv), _sel(lms[a], dob)
    else:
        t = jnp.sum(prod, axis=-1, keepdims=True)
        lse = jnp.max(lv, axis=-1, keepdims=True)
        qa, doa = qv, dob
    s = _dot(qa, kv, _NT) * cfg.scale
    if mask is not None:
        s = jnp.where(mask, s, NEG_MASK)
    p = jnp.exp(s - lse)
    dp = _dot(doa, vv, _NT)
    ds = (p * (dp - t)) * cfg.scale
    return p, ds, qa, doa, t


def _attn_dq(q, k, v, o, lse, do, sinkrow, cfg, name):
    sq, w = q.shape
    sk = k.shape[0]
    bq, bk, upb, hpb = cfg.bq, cfg.bk, cfg.upb, cfg.hpb
    nq, nk, nub = sq // bq, sk // bk, w // (LANES * upb)
    nj, sched = _kv_of_q(cfg, nq, nk)
    wb = LANES * upb
    has_sink = sinkrow is not None

    def body(*refs):
        if has_sink:
            q_ref, k_ref, v_ref, o_ref, l_ref, do_ref, s_ref, dq_ref, dsink_ref, acc = refs
        else:
            q_ref, k_ref, v_ref, o_ref, l_ref, do_ref, dq_ref, acc = refs
        i, j = pl.program_id(1), pl.program_id(2)
        kb, active = sched(i, j)
        lms = _lane_masks()

        @pl.when(j == 0)
        def _():
            acc[...] = jnp.zeros_like(acc)

        @pl.when(active)
        def _():
            mask = _attn_mask(cfg, i, kb)
            for u in range(upb):
                sl = slice(u * LANES, (u + 1) * LANES)
                qv = q_ref[:, sl].astype(BF16)
                kv = k_ref[:, sl].astype(BF16)
                vv = v_ref[:, sl].astype(BF16)
                dov = do_ref[:, sl]
                prod = dov * o_ref[:, sl]
                dob = dov.astype(BF16)
                lv = l_ref[:, sl]
                tot = None
                for a in range(hpb):
                    _, ds, _, _, _ = _softmax_grad_terms(cfg, lms, a, qv, kv, vv, dob, prod, lv, mask)
                    ka = _sel(lms[a], kv) if hpb == 2 else kv
                    c = _dot(ds.astype(BF16), ka, _NN)
                    tot = c if tot is None else tot + c
                acc[u] = acc[u] + tot

        @pl.when(j == nj - 1)
        def _():
            for u in range(upb):
                dq_ref[:, u * LANES:(u + 1) * LANES] = acc[u]
            if has_sink:
                @pl.when(i == 0)
                def _():
                    dsink_ref[...] = jnp.zeros_like(dsink_ref)

                for u in range(upb):
                    sl = slice(u * LANES, (u + 1) * LANES)
                    prod = do_ref[:, sl] * o_ref[:, sl]
                    t0 = jnp.sum(_sel(lms[0], prod), axis=-1, keepdims=True)
                    t1 = jnp.sum(_sel(lms[1], prod), axis=-1, keepdims=True)
                    tf = jnp.where(lms[0], t0, t1)
                    rs = -jnp.exp(s_ref[:, sl] - l_ref[:, sl]) * tf
                    dsink_ref[0:1, sl] += jnp.sum(rs, axis=0, keepdims=True)

    qspec = pl.BlockSpec((bq, wb), lambda ub, i, j: (i, ub))
    kspec = pl.BlockSpec((bk, wb), lambda ub, i, j: (sched(i, j)[0], ub))
    in_specs = [qspec, kspec, kspec, qspec, qspec, qspec]
    args = (q, k, v, o, lse, do)
    out_specs = qspec
    out_shape = jax.ShapeDtypeStruct((sq, w), F32)
    sem = ("parallel", "parallel", "arbitrary")
    if has_sink:
        in_specs = in_specs + [pl.BlockSpec((1, wb), lambda ub, i, j: (0, ub))]
        args = args + (sinkrow,)
        out_specs = (qspec, pl.BlockSpec((8, wb), lambda ub, i, j: (0, ub)))
        out_shape = (out_shape, jax.ShapeDtypeStruct((8, w), F32))
        sem = ("parallel", "arbitrary", "arbitrary")
    return pl.pallas_call(
        body, name=name, grid=(nub, nq, nj), in_specs=in_specs, out_specs=out_specs, out_shape=out_shape,
        scratch_shapes=[pltpu.VMEM((upb, bq, LANES), F32)], compiler_params=_params(*sem),
    )(*args)


def _attn_dkv(q, k, v, o, lse, do, cfg, name):
    sq, w = q.shape
    sk = k.shape[0]
    bq, bk, upb, hpb = cfg.bq, cfg.bk, cfg.upb, cfg.hpb
    nq, nk, nub = sq // bq, sk // bk, w // (LANES * upb)
    nj, sched = _q_of_kv(cfg, nq, nk)
    wb = LANES * upb

    def body(q_ref, k_ref, v_ref, o_ref, l_ref, do_ref, dk_ref, dv_ref, dk_acc, dv_acc):
        kb, j = pl.program_id(1), pl.program_id(2)
        i, active = sched(kb, j)
        lms = _lane_masks()

        @pl.when(j == 0)
        def _():
            dk_acc[...] = jnp.zeros_like(dk_acc)
            dv_acc[...] = jnp.zeros_like(dv_acc)

        @pl.when(active)
        def _():
            mask = _attn_mask(cfg, i, kb)
            for u in range(upb):
                sl = slice(u * LANES, (u + 1) * LANES)
                qv = q_ref[:, sl].astype(BF16)
                kv = k_ref[:, sl].astype(BF16)
                vv = v_ref[:, sl].astype(BF16)
                dov = do_ref[:, sl]
                prod = dov * o_ref[:, sl]
                dob = dov.astype(BF16)
                lv = l_ref[:, sl]
                dk_tot, dv_tot = None, None
                for a in range(hpb):
                    p, ds, qa, doa, _ = _softmax_grad_terms(cfg, lms, a, qv, kv, vv, dob, prod, lv, mask)
                    dvc = _dot(p.astype(BF16), doa, _TN)
                    dkc = _dot(ds.astype(BF16), qa, _TN)
                    dv_tot = dvc if dv_tot is None else dv_tot + dvc
                    dk_tot = dkc if dk_tot is None else dk_tot + dkc
                dk_acc[u] = dk_acc[u] + dk_tot
                dv_acc[u] = dv_acc[u] + dv_tot

        @pl.when(j == nj - 1)
        def _():
            for u in range(upb):
                sl = slice(u * LANES, (u + 1) * LANES)
                dk_ref[:, sl] = dk_acc[u]
                dv_ref[:, sl] = dv_acc[u]

    qspec = pl.BlockSpec((bq, wb), lambda ub, kb, j: (sched(kb, j)[0], ub))
    kspec = pl.BlockSpec((bk, wb), lambda ub, kb, j: (kb, ub))
    return pl.pallas_call(
        body, name=name, grid=(nub, nk, nj), in_specs=[qspec, kspec, kspec, qspec, qspec, qspec],
        out_specs=(kspec, kspec),
        out_shape=(jax.ShapeDtypeStruct((sk, w), F32), jax.ShapeDtypeStruct((sk, w), F32)),
        scratch_shapes=[pltpu.VMEM((upb, bk, LANES), F32), pltpu.VMEM((upb, bk, LANES), F32)],
        compiler_params=_params("parallel", "parallel", "arbitrary"),
    )(q, k, v, o, lse, do)


def _band_masks(max_dist):
    assert BLOCK - 1 <= max_dist <= BLOCK
    r = lax.broadcasted_iota(jnp.int32, (BLOCK, BLOCK), 0)
    c = lax.broadcasted_iota(jnp.int32, (BLOCK, BLOCK), 1)
    return (BLOCK + r - c) <= max_dist, r >= c


def _head_terms(lms, a, prod, lv):
    t = jnp.sum(_sel(lms[a], prod), axis=-1, keepdims=True)
    lse = jnp.max(jnp.where(lms[a], lv, -jnp.inf), axis=-1, keepdims=True)
    return t, lse


def _band_fwd(q, k, v, sinkrow, scale, max_dist, upb, name):
    sq, w = q.shape
    nq, nub, wb = sq // BLOCK, w // (LANES * upb), LANES * upb
    has_sink = sinkrow is not None

    def body(*refs):
        if has_sink:
            q_ref, kp_ref, kc_ref, vp_ref, vc_ref, s_ref, o_ref, l_ref = refs
        else:
            q_ref, kp_ref, kc_ref, vp_ref, vc_ref, o_ref, l_ref = refs
        lms = _lane_masks()
        mprev, mcur = _band_masks(max_dist)
        mprev = jnp.logical_and(mprev, pl.program_id(1) > 0)
        for u in range(upb):
            sl = slice(u * LANES, (u + 1) * LANES)
            qv = (q_ref[:, sl] * scale).astype(BF16)
            kp, kc = kp_ref[:, sl].astype(BF16), kc_ref[:, sl].astype(BF16)
            vp, vc = vp_ref[:, sl].astype(BF16), vc_ref[:, sl].astype(BF16)
            outs, lses = [], []
            for a in range(2):
                qa = _sel(lms[a], qv)
                s0 = jnp.where(mprev, _dot(qa, kp, _NT), NEG_MASK)
                s1 = jnp.where(mcur, _dot(qa, kc, _NT), NEG_MASK)
                m = jnp.maximum(jnp.max(s0, axis=-1, keepdims=True), jnp.max(s1, axis=-1, keepdims=True))
                if has_sink:
                    sk_a = jnp.max(jnp.where(lms[a], s_ref[:, sl], -jnp.inf), axis=-1, keepdims=True)
                    m = jnp.maximum(m, sk_a)
                p0, p1 = jnp.exp(s0 - m), jnp.exp(s1 - m)
                l = jnp.sum(p0, axis=-1, keepdims=True) + jnp.sum(p1, axis=-1, keepdims=True)
                if has_sink:
                    l = l + jnp.exp(sk_a - m)
                pv = _dot(p0.astype(BF16), vp, _NN) + _dot(p1.astype(BF16), vc, _NN)
                outs.append(pv / l)
                lses.append(m + jnp.log(l))
            o_ref[:, sl] = jnp.where(lms[0], outs[0], outs[1])
            l_ref[:, sl] = jnp.where(lms[0], lses[0], lses[1])

    cur = pl.BlockSpec((BLOCK, wb), lambda ub, i: (i, ub))
    prev = pl.BlockSpec((BLOCK, wb), lambda ub, i: (jnp.maximum(i - 1, 0), ub))
    in_specs = [cur, prev, cur, prev, cur] + ([pl.BlockSpec((1, wb), lambda ub, i: (0, ub))] if has_sink else [])
    args = (q, k, k, v, v) + ((sinkrow,) if has_sink else ())
    return pl.pallas_call(
        body, name=name, grid=(nub, nq), in_specs=in_specs, out_specs=(cur, cur),
        out_shape=(jax.ShapeDtypeStruct((sq, w), F32), jax.ShapeDtypeStruct((sq, w), F32)),
        compiler_params=_params("parallel", "parallel"),
    )(*args)


def _band_dq(q, k, v, o, lse, do, sinkrow, scale, max_dist, upb, name):
    sq, w = q.shape
    nq, nub, wb = sq // BLOCK, w // (LANES * upb), LANES * upb
    has_sink = sinkrow is not None

    def body(*refs):
        if has_sink:
            q_ref, kp_ref, kc_ref, vp_ref, vc_ref, o_ref, l_ref, do_ref, s_ref, dq_ref, dsink_ref = refs
        else:
            q_ref, kp_ref, kc_ref, vp_ref, vc_ref, o_ref, l_ref, do_ref, dq_ref = refs
        i = pl.program_id(1)
        lms = _lane_masks()
        mprev, mcur = _band_masks(max_dist)
        mprev = jnp.logical_and(mprev, i > 0)
        if has_sink:
            @pl.when(i == 0)
            def _():
                dsink_ref[...] = jnp.zeros_like(dsink_ref)

        for u in range(upb):
            sl = slice(u * LANES, (u + 1) * LANES)
            qv = (q_ref[:, sl] * scale).astype(BF16)
            kp, kc = kp_ref[:, sl].astype(BF16), kc_ref[:, sl].astype(BF16)
            vp, vc = vp_ref[:, sl].astype(BF16), vc_ref[:, sl].astype(BF16)
            dov = do_ref[:, sl]
            prod = dov * o_ref[:, sl]
            dob = dov.astype(BF16)
            lv = l_ref[:, sl]
            dqs, ts = [], []
            for a in range(2):
                t, lse_a = _head_terms(lms, a, prod, lv)
                qa, doa = _sel(lms[a], qv), _sel(lms[a], dob)
                p0 = jnp.exp(jnp.where(mprev, _dot(qa, kp, _NT), NEG_MASK) - lse_a)
                p1 = jnp.exp(jnp.where(mcur, _dot(qa, kc, _NT), NEG_MASK) - lse_a)
                ds0 = (p0 * (_dot(doa, vp, _NT) - t)).astype(BF16)
                ds1 = (p1 * (_dot(doa, vc, _NT) - t)).astype(BF16)
                dqs.append((_dot(ds0, kp, _NN) + _dot(ds1, kc, _NN)) * scale)
                ts.append(t)
            dq_ref[:, sl] = jnp.where(lms[0], dqs[0], dqs[1])
            if has_sink:
                rs = -jnp.exp(s_ref[:, sl] - lv) * jnp.where(lms[0], ts[0], ts[1])
                dsink_ref[0:1, sl] += jnp.sum(rs, axis=0, keepdims=True)

    cur = pl.BlockSpec((BLOCK, wb), lambda ub, i: (i, ub))
    prev = pl.BlockSpec((BLOCK, wb), lambda ub, i: (jnp.maximum(i - 1, 0), ub))
    in_specs = [cur, prev, cur, prev, cur, cur, cur, cur]
    args = (q, k, k, v, v, o, lse, do)
    out_specs, out_shape = cur, jax.ShapeDtypeStruct((sq, w), F32)
    sem = ("parallel", "parallel")
    if has_sink:
        in_specs = in_specs + [pl.BlockSpec((1, wb), lambda ub, i: (0, ub))]
        args = args + (sinkrow,)
        out_specs = (cur, pl.BlockSpec((8, wb), lambda ub, i: (0, ub)))
        out_shape = (out_shape, jax.ShapeDtypeStruct((8, w), F32))
        sem = ("parallel", "arbitrary")
    return pl.pallas_call(
        body, name=name, grid=(nub, nq), in_specs=in_specs, out_specs=out_specs, out_shape=out_shape,
        compiler_params=_params(*sem),
    )(*args)


def _band_dkv(q, k, v, o, lse, do, scale, max_dist, upb, name):
    sq, w = q.shape
    nq, nub, wb = sq // BLOCK, w // (LANES * upb), LANES * upb

    def body(k_ref, v_ref, qs_ref, qn_ref, os_ref, on_ref, ls_ref, ln_ref, dos_ref, don_ref, dk_ref, dv_ref):
        kb = pl.program_id(1)
        lms = _lane_masks()
        mnext, msame = _band_masks(max_dist)
        mnext = jnp.logical_and(mnext, kb < nq - 1)
        for u in range(upb):
            sl = slice(u * LANES, (u + 1) * LANES)
            kv, vv = k_ref[:, sl].astype(BF16), v_ref[:, sl].astype(BF16)
            dk_tot, dv_tot = None, None
            for q_ref, o_ref, l_ref, do_ref, mask in ((qs_ref, os_ref, ls_ref, dos_ref, msame),
                                                      (qn_ref, on_ref, ln_ref, don_ref, mnext)):
                qv = (q_ref[:, sl] * scale).astype(BF16)
                dov = do_ref[:, sl]
                prod = dov * o_ref[:, sl]
                dob = dov.astype(BF16)
                lv = l_ref[:, sl]
                for a in range(2):
                    t, lse_a = _head_terms(lms, a, prod, lv)
                    qa, doa = _sel(lms[a], qv), _sel(lms[a], dob)
                    p = jnp.exp(jnp.where(mask, _dot(qa, kv, _NT), NEG_MASK) - lse_a)
                    ds = (p * (_dot(doa, vv, _NT) - t)).astype(BF16)
                    dvc = _dot(p.astype(BF16), doa, _TN)
                    dkc = _dot(ds, qa, _TN)
                    dv_tot = dvc if dv_tot is None else dv_tot + dvc
                    dk_tot = dkc if dk_tot is None else dk_tot + dkc
            dk_ref[:, sl] = dk_tot
            dv_ref[:, sl] = dv_tot

    same = pl.BlockSpec((BLOCK, wb), lambda ub, kb: (kb, ub))
    nxt = pl.BlockSpec((BLOCK, wb), lambda ub, kb: (jnp.minimum(kb + 1, nq - 1), ub))
    return pl.pallas_call(
        body, name=name, grid=(nub, nq), in_specs=[same, same, same, nxt, same, nxt, same, nxt, same, nxt],
        out_specs=(same, same),
        out_shape=(jax.ShapeDtypeStruct((sq, w), F32), jax.ShapeDtypeStruct((sq, w), F32)),
        compiler_params=_params("parallel", "parallel"),
    )(k, v, q, q, o, o, lse, lse, do, do)


def _make_band_attention(scale, max_dist, upb, name):
    @jax.custom_vjp
    def attn(q, k, v, sinks):
        return _band_fwd(q, k, v, _sink_row(sinks), scale, max_dist, upb, name + "_fwd")[0]

    def fwd(q, k, v, sinks):
        o, lse = _band_fwd(q, k, v, _sink_row(sinks), scale, max_dist, upb, name + "_fwd")
        return o, (q, k, v, o, lse, sinks)

    def bwd(res, do):
        q, k, v, o, lse, sinks = res
        dq, dsink = _band_dq(q, k, v, o, lse, do, _sink_row(sinks), scale, max_dist, upb, name + "_dq")
        dk, dv = _band_dkv(q, k, v, o, lse, do, scale, max_dist, upb, name + "_dkv")
        return dq, dk, dv, dsink[0].reshape(-1, HEAD_DIM)[:, 0]

    attn.defvjp(fwd, bwd)
    return attn


def _triangle(n, by_key):
    if by_key:
        pairs = [(i, kb) for kb in range(n) for i in range(kb, n)]
    else:
        pairs = [(i, j) for i in range(n) for j in range(i + 1)]
    qi = np.asarray([p[0] for p in pairs], np.int32)
    kj = np.asarray([p[1] for p in pairs], np.int32)
    return jnp.asarray(qi), jnp.asarray(kj)


def _causal_fwd(q, k, v, scale, blk, name):
    s, w = q.shape
    nq, nub = s // blk, w // LANES
    qi, kj = _triangle(nq, by_key=False)

    def body(qi_ref, kj_ref, q_ref, k_ref, v_ref, o_ref, l_ref, m_sc, l_sc, acc_sc):
        t = pl.program_id(1)
        i, j = qi_ref[t], kj_ref[t]

        @pl.when(j == 0)
        def _():
            m_sc[...] = jnp.full_like(m_sc, NEG_INIT)
            l_sc[...] = jnp.zeros_like(l_sc)
            acc_sc[...] = jnp.zeros_like(acc_sc)

        def step(diagonal):
            qv = (q_ref[...] * scale).astype(BF16)
            sc = _dot(qv, k_ref[...].astype(BF16), _NT)
            if diagonal:
                r = lax.broadcasted_iota(jnp.int32, (blk, blk), 0)
                c = lax.broadcasted_iota(jnp.int32, (blk, blk), 1)
                sc = jnp.where(r >= c, sc, NEG_MASK)
            m_prev = m_sc[...]
            m_new = jnp.maximum(m_prev, jnp.max(sc, axis=-1, keepdims=True))
            alpha = jnp.exp(m_prev - m_new)
            p = jnp.exp(sc - m_new)
            l_sc[...] = alpha * l_sc[...] + jnp.sum(p, axis=-1, keepdims=True)
            m_sc[...] = m_new
            acc_sc[...] = acc_sc[...] * alpha + _dot(p.astype(BF16), v_ref[...].astype(BF16), _NN)

        @pl.when(j < i)
        def _():
            step(False)

        @pl.when(j == i)
        def _():
            step(True)
            lf = l_sc[...]
            o_ref[...] = acc_sc[...] / lf
            l_ref[...] = jnp.broadcast_to(m_sc[...] + jnp.log(lf), (blk, LANES))

    qspec = pl.BlockSpec((blk, LANES), lambda ub, t, qi_ref, kj_ref: (qi_ref[t], ub))
    kspec = pl.BlockSpec((blk, LANES), lambda ub, t, qi_ref, kj_ref: (kj_ref[t], ub))
    return pl.pallas_call(
        body, name=name,
        grid_spec=pltpu.PrefetchScalarGridSpec(
            num_scalar_prefetch=2, grid=(nub, qi.shape[0]), in_specs=[qspec, kspec, kspec], out_specs=(qspec, qspec),
            scratch_shapes=[pltpu.VMEM((blk, 1), F32), pltpu.VMEM((blk, 1), F32), pltpu.VMEM((blk, LANES), F32)]),
        out_shape=(jax.ShapeDtypeStruct((s, w), F32), jax.ShapeDtypeStruct((s, w), F32)),
        compiler_params=_params("parallel", "arbitrary"),
    )(qi, kj, q, k, v)


def _causal_bwd(q, k, v, o, lse, do, scale, blk, name):
    s, w = q.shape
    nq, nub = s // blk, w // LANES
    qi, kj = _triangle(nq, by_key=True)

    def body(qi_ref, kj_ref, q_ref, k_ref, v_ref, o_ref, l_ref, do_ref, dq_ref, dk_ref, dv_ref, dk_acc, dv_acc):
        t = pl.program_id(1)
        i, kb = qi_ref[t], kj_ref[t]

        @pl.when(t == 0)
        def _():
            dq_ref[...] = jnp.zeros_like(dq_ref)

        @pl.when(i == kb)
        def _():
            dk_acc[...] = jnp.zeros_like(dk_acc)
            dv_acc[...] = jnp.zeros_like(dv_acc)

        def step(diagonal):
            qv = (q_ref[...] * scale).astype(BF16)
            kv, vv = k_ref[...].astype(BF16), v_ref[...].astype(BF16)
            dov = do_ref[...]
            tsum = jnp.sum(dov * o_ref[...], axis=-1, keepdims=True)
            dob = dov.astype(BF16)
            sc = _dot(qv, kv, _NT)
            if diagonal:
                r = lax.broadcasted_iota(jnp.int32, (blk, blk), 0)
                c = lax.broadcasted_iota(jnp.int32, (blk, blk), 1)
                sc = jnp.where(r >= c, sc, NEG_MASK)
            p = jnp.exp(sc - l_ref[:, 0:1])
            ds = (p * (_dot(dob, vv, _NT) - tsum)).astype(BF16)
            dv_acc[...] += _dot(p.astype(BF16), dob, _TN)
            dk_acc[...] += _dot(ds, qv, _TN)
            rows = pl.ds(pl.multiple_of(i * blk, blk), blk)
            dq_ref[rows, :] += _dot(ds, kv, _NN) * scale

        @pl.when(i == kb)
        def _():
            step(True)

        @pl.when(i > kb)
        def _():
            step(False)

        @pl.when(i == nq - 1)
        def _():
            dk_ref[...] = dk_acc[...]
            dv_ref[...] = dv_acc[...]

    qspec = pl.BlockSpec((blk, LANES), lambda ub, t, qi_ref, kj_ref: (qi_ref[t], ub))
    kspec = pl.BlockSpec((blk, LANES), lambda ub, t, qi_ref, kj_ref: (kj_ref[t], ub))
    whole = pl.BlockSpec((s, LANES), lambda ub, t, qi_ref, kj_ref: (0, ub))
    out = jax.ShapeDtypeStruct((s, w), F32)
    return pl.pallas_call(
        body, name=name,
        grid_spec=pltpu.PrefetchScalarGridSpec(
            num_scalar_prefetch=2, grid=(nub, qi.shape[0]), in_specs=[qspec, kspec, kspec, qspec, qspec, qspec],
            out_specs=(whole, kspec, kspec),
            scratch_shapes=[pltpu.VMEM((blk, LANES), F32), pltpu.VMEM((blk, LANES), F32)]),
        out_shape=(out, out, out), compiler_params=_params("parallel", "arbitrary"),
    )(qi, kj, q, k, v, o, lse, do)


def _make_causal_attention(scale, blk, name):
    @jax.custom_vjp
    def attn(q, k, v):
        return _causal_fwd(q, k, v, scale, blk, name + "_fwd")[0]

    def fwd(q, k, v):
        o, lse = _causal_fwd(q, k, v, scale, blk, name + "_fwd")
        return o, (q, k, v, o, lse)

    def bwd(res, do):
        q, k, v, o, lse = res
        return _causal_bwd(q, k, v, o, lse, do, scale, blk, name + "_bwd")

    attn.defvjp(fwd, bwd)
    return attn


def _make_attention(cfg, name, with_sink=False):
    if with_sink:
        @jax.custom_vjp
        def attn(q, k, v, sinks):
            return _attn_fwd(q, k, v, _sink_row(sinks), cfg, name + "_fwd")[0]

        def fwd(q, k, v, sinks):
            o, lse = _attn_fwd(q, k, v, _sink_row(sinks), cfg, name + "_fwd")
            return o, (q, k, v, o, lse, sinks)

        def bwd(res, do):
            q, k, v, o, lse, sinks = res
            dq, dsink = _attn_dq(q, k, v, o, lse, do, _sink_row(sinks), cfg, name + "_dq")
            dk, dv = _attn_dkv(q, k, v, o, lse, do, cfg, name + "_dkv")
            return dq, dk, dv, dsink[0].reshape(-1, HEAD_DIM)[:, 0]
    else:
        @jax.custom_vjp
        def attn(q, k, v):
            return _attn_fwd(q, k, v, None, cfg, name + "_fwd")[0]

        def fwd(q, k, v):
            o, lse = _attn_fwd(q, k, v, None, cfg, name + "_fwd")
            return o, (q, k, v, o, lse)

        def bwd(res, do):
            q, k, v, o, lse = res
            dq = _attn_dq(q, k, v, o, lse, do, None, cfg, name + "_dq")
            dk, dv = _attn_dkv(q, k, v, o, lse, do, cfg, name + "_dkv")
            return dq, dk, dv

    attn.defvjp(fwd, bwd)
    return attn


def _sink_row(sinks):
    return jnp.repeat(sinks.astype(F32), HEAD_DIM).reshape(1, -1)


def _merge3(os_, ls_, name):
    s, w = os_[0].shape
    bs = _pick(s, (256, 128))

    def body(o1, o2, o3, l1, l2, l3, out_ref, lse_ref):
        a1, a2, a3 = l1[...], l2[...], l3[...]
        m = jnp.maximum(jnp.maximum(a1, a2), a3)
        e1, e2, e3 = jnp.exp(a1 - m), jnp.exp(a2 - m), jnp.exp(a3 - m)
        z = e1 + e2 + e3
        out_ref[...] = (e1 * o1[...] + e2 * o2[...] + e3 * o3[...]) / z
        lse_ref[...] = m + jnp.log(z)

    row = pl.BlockSpec((bs, w), lambda i: (i, 0))
    return pl.pallas_call(
        body, name=name, grid=(s // bs,), in_specs=[row] * 6, out_specs=(row, row),
        out_shape=(jax.ShapeDtypeStruct((s, w), F32), jax.ShapeDtypeStruct((s, w), F32)),
        compiler_params=_params("parallel"),
    )(*os_, *ls_)


def _add3(a, b, c, name):
    s, w = a.shape
    bs = _pick(s, (512, 256, 128))

    def body(a_ref, b_ref, c_ref, o_ref):
        o_ref[...] = (a_ref[...] + b_ref[...]) + c_ref[...]

    row = pl.BlockSpec((bs, w), lambda i: (i, 0))
    return pl.pallas_call(
        body, name=name, grid=(s // bs,), in_specs=[row] * 3, out_specs=row,
        out_shape=jax.ShapeDtypeStruct((s, w), F32), compiler_params=_params("parallel"),
    )(a, b, c)


def _make_dilated(name):
    scale, max_dist, upb = HEAD_DIM ** -0.5, BLOCK, BAND_UNITS_PER_STEP

    def view(t, dil):
        s, w = t.shape
        return t.reshape(s // dil, dil * w)

    def forward(q, k, v):
        s, w = q.shape
        os_, ls_ = [], []
        for n, (_, dil) in enumerate(DIL_PATTERNS):
            o, l = _band_fwd(view(q, dil), view(k, dil), view(v, dil), None, scale, max_dist, upb,
                             "%s_b%d_fwd" % (name, n))
            os_.append(o.reshape(s, w))
            ls_.append(l.reshape(s, w))
        return _merge3(os_, ls_, name + "_merge")

    @jax.custom_vjp
    def dilated(q, k, v):
        return forward(q, k, v)[0]

    def fwd(q, k, v):
        out, lse = forward(q, k, v)
        return out, (q, k, v, out, lse)

    def bwd(res, do):
        q, k, v, out, lse = res
        s, w = q.shape
        dqs, dks, dvs = [], [], []
        for n, (_, dil) in enumerate(DIL_PATTERNS):
            args = tuple(view(t, dil) for t in (q, k, v, out, lse, do))
            dqs.append(_band_dq(*args, None, scale, max_dist, upb, "%s_b%d_dq" % (name, n)).reshape(s, w))
            dk, dv = _band_dkv(*args, scale, max_dist, upb, "%s_b%d_dkv" % (name, n))
            dks.append(dk.reshape(s, w))
            dvs.append(dv.reshape(s, w))
        return (_add3(*dqs, name + "_dq_sum"), _add3(*dks, name + "_dk_sum"), _add3(*dvs, name + "_dv_sum"))

    dilated.defvjp(fwd, bwd)
    return dilated


def _make_norm_linear(name):
    @jax.custom_vjp
    def op(x, g, wslot, w):
        return _mm(_rms_fwd(x, g, name + "_norm"), w, "nn", name + "_mm")

    def fwd(x, g, wslot, w):
        h = _rms_fwd(x, g, name + "_norm")
        return _mm(h, w, "nn", name + "_mm"), (x, g, h, w)

    def bwd(res, dz):
        x, g, h, w = res
        dh = _mm(dz, w, "nt", name + "_dh")
        dw = _mm(h, dz, "tn", name + "_dw")
        dx, dg = _rms_bwd(x, g, dh, name + "_norm_bwd")
        return dx, dg, dw, None

    op.defvjp(fwd, bwd)
    return op


def _make_linear_res(name):
    @jax.custom_vjp
    def op(a, wslot, w, res):
        return _mm(a, w, "nn", name + "_mm", res=res)

    def fwd(a, wslot, w, res):
        return _mm(a, w, "nn", name + "_mm", res=res), (a, w)

    def bwd(saved, dout):
        a, w = saved
        da = _mm(dout, w, "nt", name + "_da")
        dw = _mm(a, dout, "tn", name + "_dw")
        return da, dw, None, dout

    op.defvjp(fwd, bwd)
    return op


def _swiglu_fwd(gu, name):
    s, w2 = gu.shape
    hdim = w2 // 2
    bs = _pick(s, (256, 128))

    def body(g_ref, u_ref, a_ref):
        g = g_ref[...]
        a_ref[...] = (g / (1.0 + jnp.exp(-g)) * u_ref[...]).astype(BF16)

    return pl.pallas_call(
        body, name=name, grid=(s // bs,),
        in_specs=[pl.BlockSpec((bs, hdim), lambda i: (i, 0)), pl.BlockSpec((bs, hdim), lambda i: (i, 1))],
        out_specs=pl.BlockSpec((bs, hdim), lambda i: (i, 0)), out_shape=jax.ShapeDtypeStruct((s, hdim), BF16),
        compiler_params=_params("parallel"),
    )(gu, gu)


def _swiglu_bwd_joint(gu, da, name):
    s, w2 = gu.shape
    hdim = w2 // 2
    bs = _pick(s, (256, 128))

    def body(g_ref, u_ref, da_ref, dgu_ref):
        g, u, d = g_ref[...], u_ref[...], da_ref[...]
        sig = 1.0 / (1.0 + jnp.exp(-g))
        dgu_ref[:, :hdim] = (d * u * (sig * (1.0 + g * (1.0 - sig)))).astype(BF16)
        dgu_ref[:, hdim:] = (d * (g * sig)).astype(BF16)

    lo = pl.BlockSpec((bs, hdim), lambda i: (i, 0))
    hi = pl.BlockSpec((bs, hdim), lambda i: (i, 1))
    return pl.pallas_call(
        body, name=name, grid=(s // bs,), in_specs=[lo, hi, lo], out_specs=pl.BlockSpec((bs, w2), lambda i: (i, 0)),
        out_shape=jax.ShapeDtypeStruct((s, w2), BF16), compiler_params=_params("parallel"),
    )(gu, gu, da)


def _make_ffn(name):
    def forward(x, g, wgu, wd):
        h = _rms_fwd(x, g, name + "_norm")
        gu = _mm(h, wgu, "nn", name + "_gu")
        a = _swiglu_fwd(gu, name + "_act")
        return _mm(a, wd, "nn", name + "_down", res=x), (x, g, h, gu, a, wgu, wd)

    @jax.custom_vjp
    def op(x, g, wgu_slot, wd_slot, wgu, wd):
        return forward(x, g, wgu, wd)[0]

    def fwd(x, g, wgu_slot, wd_slot, wgu, wd):
        return forward(x, g, wgu, wd)

    def bwd(saved, dout):
        x, g, h, gu, a, wgu, wd = saved
        da = _mm(dout, wd, "nt", name + "_da")
        dwd = _mm(a, dout, "tn", name + "_dwd")
        dgu = _swiglu_bwd_joint(gu, da, name + "_act_bwd")
        dwgu = _mm(h, dgu, "tn", name + "_dwgu")
        dh = _mm(dgu, wgu, "nt", name + "_dh")
        dx, dg = _rms_bwd(x, g, dh, name + "_norm_bwd", dres=dout)
        return dx, dg, dwgu, dwd, None, None

    op.defvjp(fwd, bwd)
    return op


def _make_final_loss(name):
    def run(x, g, tgt):
        s, d = x.shape
        bs = _pick(s, (512, 256, 128))

        def body(x_ref, g_ref, t_ref, loss_ref, dx_ref, dg_ref):
            i = pl.program_id(0)
            xv = x_ref[...]
            gv = g_ref[...]
            r = lax.rsqrt(jnp.mean(xv * xv, axis=-1, keepdims=True) + NORM_EPS)
            xh = xv * r
            e = xh * gv - t_ref[...]
            dy = e * (1.0 / d)
            dxh = dy * gv
            dx_ref[...] = r * (dxh - xh * jnp.mean(dxh * xh, axis=-1, keepdims=True))
            part = 0.5 * jnp.sum(jnp.sum(e * e, axis=-1, keepdims=True) * (1.0 / d), axis=0, keepdims=True)

            @pl.when(i == 0)
            def _():
                loss_ref[...] = jnp.zeros_like(loss_ref)
                dg_ref[...] = jnp.zeros_like(dg_ref)

            loss_ref[...] += jnp.broadcast_to(part, loss_ref.shape)
            dg_ref[...] += jnp.sum(dy * xh, axis=0, keepdims=True)

        row = pl.BlockSpec((bs, d), lambda i: (i, 0))
        vec = pl.BlockSpec((1, d), lambda i: (0, 0))
        loss, dx, dg = pl.pallas_call(
            body, name=name, grid=(s // bs,), in_specs=[row, vec, row],
            out_specs=(pl.BlockSpec((8, LANES), lambda i: (0, 0)), row, vec),
            out_shape=(jax.ShapeDtypeStruct((8, LANES), F32), jax.ShapeDtypeStruct((s, d), F32),
                       jax.ShapeDtypeStruct((1, d), F32)),
            compiler_params=_params("arbitrary"),
        )(x, g.reshape(1, d), tgt)
        return loss[0, 0], dx, dg.reshape(d)

    @jax.custom_vjp
    def op(x, g, tgt):
        return run(x, g, tgt)[0]

    def fwd(x, g, tgt):
        loss, dx, dg = run(x, g, tgt)
        return loss, (dx, dg)

    def bwd(saved, ct):
        dx, dg = saved
        return dx * ct, dg * ct, None

    op.defvjp(fwd, bwd)
    return op


def _model_loss(diff, consts):
    x = diff["x"]
    w = consts["w"]
    slot = diff["slots"]
    vec = diff["vec"]
    tab64, tab_mla = consts["tab64"], consts["tab_mla"]
    mem = consts["mem"]
    s = x.shape[0]

    rope64 = lambda t, nm: _make_rope(HEAD_DIM // 2, nm)(t, *tab64)
    rope_mla = lambda t, nm: _make_rope(MLA_ROPE_DIM // 2, nm)(t, *tab_mla)

    def nl(nm, inp, gain, wname):
        return _make_norm_linear(nm)(inp, gain, slot[wname], w[wname])

    def cross(layer, xin):
        p = "l%d_" % layer
        q = nl(p + "xq", xin, vec[p + "x_norm"], p + "w_xq")
        kv = nl(p + "xkv", mem, vec[p + "mem_norm"], p + "w_xkv")
        half = X_HEADS * X_HEAD_DIM
        cfg = AttnCfg("full", X_HEAD_DIM ** -0.5, 1, _pick(s, (512, 256, 128)), kv.shape[0], 4)
        o = _make_attention(cfg, p + "xattn")(q, kv[:, :half], kv[:, half:])
        return _make_linear_res(p + "xo")(o, slot[p + "w_xo"], w[p + "w_xo"], xin)

    def ffn(layer, xin):
        p = "l%d_" % layer
        return _make_ffn(p + "ffn")(xin, vec[p + "ffn_norm"], slot[p + "w_gu"], slot[p + "w_down"], w[p + "w_gu"],
                                    w[p + "w_down"])

    z = nl("l0_in", x, vec["l0_mix_norm"], "l0_w_in")
    qa = rope64(z[:, :A_Q], "l0_rope_qa")
    ka = rope64(z[:, A_Q:A_Q + A_KV], "l0_rope_ka")
    va = z[:, A_Q + A_KV:A_Q + 2 * A_KV]
    rep = SWA_HEADS // SWA_KV_HEADS
    expand = lambda t: jnp.broadcast_to(t.reshape(s, SWA_KV_HEADS, 1, HEAD_DIM),
                                        (s, SWA_KV_HEADS, rep, HEAD_DIM)).reshape(s, A_Q)
    swa = _make_band_attention(HEAD_DIM ** -0.5, SWA_WINDOW - 1, BAND_UNITS_PER_STEP, "l0_swa")
    oa = swa(qa, expand(ka), expand(va), vec["l0_sinks"])

    c0 = A_Q + 2 * A_KV
    cq = z[:, c0:c0 + MLA_Q_RANK]
    ckv = z[:, c0 + MLA_Q_RANK:c0 + MLA_Q_RANK + MLA_KV_RANK]
    kr = z[:, c0 + MLA_Q_RANK + MLA_KV_RANK:EVEN_IN]
    qb = nl("l0_uq", cq, vec["l0_q_norm"], "l0_w_uq").reshape(s, MLA_HEADS, MLA_NOPE_DIM + MLA_ROPE_DIM)
    qfull = jnp.pad(qb, ((0, 0), (0, 0), (0, LANES - MLA_NOPE_DIM - MLA_ROPE_DIM))).reshape(s, MLA_HEADS * LANES)
    qfull = rope_mla(qfull, "l0_rope_q")
    kvb = nl("l0_ukv", ckv, vec["l0_kv_norm"], "l0_w_ukv")
    kvb3 = kvb.reshape(s, MLA_HEADS, LANES)
    kfull = jnp.concatenate(
        [kvb3[:, :, :MLA_NOPE_DIM], jnp.broadcast_to(kr[:, None, :], (s, MLA_HEADS, MLA_ROPE_DIM)),
         jnp.zeros((s, MLA_HEADS, LANES - MLA_NOPE_DIM - MLA_ROPE_DIM), F32)], axis=-1).reshape(s, MLA_HEADS * LANES)
    kfull = rope_mla(kfull, "l0_rope_k")
    mla = _make_causal_attention((MLA_NOPE_DIM + MLA_ROPE_DIM) ** -0.5, _pick(s, (512, 256, 128)), "l0_mla")
    ob = mla(qfull, kfull, kvb).reshape(s, MLA_HEADS, LANES)[:, :, MLA_NOPE_DIM:]
    o = jnp.concatenate([oa, ob.reshape(s, MLA_HEADS * HEAD_DIM)], axis=-1)
    x = _make_linear_res("l0_out")(o, slot["l0_w_out"], w["l0_w_out"], x)
    x = cross(0, x)
    x = ffn(0, x)

    qkv = nl("l1_qkv", x, vec["l1_mix_norm"], "l1_w_qkv")
    q = rope64(qkv[:, :D_MODEL], "l1_rope_q")
    k = rope64(qkv[:, D_MODEL:2 * D_MODEL], "l1_rope_k")
    o = _make_dilated("l1_dil")(q, k, qkv[:, 2 * D_MODEL:])
    x = _make_linear_res("l1_out")(o, slot["l1_w_out"], w["l1_w_out"], x)
    x = cross(1, x)
    x = ffn(1, x)

    return _make_final_loss("final_loss")(x, vec["final_norm"], consts["target"])


MESH_IDS = pl.DeviceIdType.MESH
HBM_SPEC = pl.BlockSpec(memory_space=pltpu.HBM)


def _my_place():
    return lax.axis_index("x"), lax.axis_index("y"), lax.axis_index("c")


def _flip(v, bit):
    return 1 - v if bit else v


def _all_gather_rows(shard):
    r, c_ = shard.shape

    def body(x_ref, out_ref, send_sems, recv_sems, local_sem):
        x, y, c = _my_place()
        me, sibling = (x, y, c), (x, y, 1 - c)
        chips = [(1 - x, y), (x, 1 - y), (1 - x, 1 - y)]

        def slot(px, py, pc):
            return out_ref.at[4 * px + 2 * py + pc]

        def copy(k, block, to, src=None):
            return pltpu.make_async_remote_copy(
                src_ref=slot(*block) if src is None else src, dst_ref=slot(*block), send_sem=send_sems.at[k],
                recv_sem=recv_sems.at[k], device_id=to, device_id_type=MESH_IDS)

        mine = pltpu.make_async_copy(x_ref, slot(*me), local_sem)
        mine.start()
        first = [copy(0, me, sibling, src=x_ref)]
        first += [copy(1 + j, me, (*chip, c), src=x_ref) for j, chip in enumerate(chips)]
        for cp in first:
            cp.start()
        passed = [copy(4 + j, (*chip, c), sibling) for j, chip in enumerate(chips)]
        for j, chip in enumerate(chips):
            copy(1 + j, (*chip, c), me).wait_recv()
            passed[j].start()
        copy(0, sibling, me).wait_recv()
        for j, chip in enumerate(chips):
            copy(4 + j, (*chip, 1 - c), me).wait_recv()
        for cp in first + passed:
            cp.wait_send()
        mine.wait()

    return pl.pallas_call(
        body, name="weights_all_gather", out_shape=jax.ShapeDtypeStruct((N_DEV, r, c_), shard.dtype),
        in_specs=[HBM_SPEC], out_specs=HBM_SPEC,
        scratch_shapes=[pltpu.SemaphoreType.DMA((7,)), pltpu.SemaphoreType.DMA((7,)), pltpu.SemaphoreType.DMA],
    )(shard)


def _exchange_slabs(slabs):
    _, r, c_ = slabs.shape

    def body(p_ref, out_ref, send_sems, recv_sems, local_sem):
        x, y, c = _my_place()
        me = 4 * x + 2 * y + c
        local = pltpu.make_async_copy(p_ref.at[me], out_ref.at[me], local_sem)
        local.start()
        sends, recvs = [], []
        for k in range(1, N_DEV):
            px, py, pc = _flip(x, k & 4), _flip(y, k & 2), _flip(c, k & 1)
            peer = 4 * px + 2 * py + pc
            sends.append(pltpu.make_async_remote_copy(
                src_ref=p_ref.at[peer], dst_ref=out_ref.at[me], send_sem=send_sems.at[k - 1],
                recv_sem=recv_sems.at[k - 1], device_id=(px, py, pc), device_id_type=MESH_IDS))
            recvs.append(pltpu.make_async_remote_copy(
                src_ref=p_ref.at[me], dst_ref=out_ref.at[peer], send_sem=send_sems.at[k - 1],
                recv_sem=recv_sems.at[k - 1], device_id=(px, py, pc), device_id_type=MESH_IDS))
        for cp in sends:
            cp.start()
        for cp in recvs:
            cp.wait_recv()
        for cp in sends:
            cp.wait_send()
        local.wait()

    return pl.pallas_call(
        body, name="grad_slab_exchange", out_shape=jax.ShapeDtypeStruct(slabs.shape, slabs.dtype),
        in_specs=[HBM_SPEC], out_specs=HBM_SPEC,
        scratch_shapes=[pltpu.SemaphoreType.DMA((7,)), pltpu.SemaphoreType.DMA((7,)), pltpu.SemaphoreType.DMA],
    )(slabs)


def _all_reduce_small(v):
    r, c_ = v.shape

    def body(v_ref, out_ref, buf, send_sems, recv_sems):
        x, y, c = _my_place()
        me = 4 * x + 2 * y + c
        buf[me] = v_ref[...]
        sends, recvs = [], []
        for k in range(1, N_DEV):
            px, py, pc = _flip(x, k & 4), _flip(y, k & 2), _flip(c, k & 1)
            peer = 4 * px + 2 * py + pc
            sends.append(pltpu.make_async_remote_copy(
                src_ref=v_ref, dst_ref=buf.at[me], send_sem=send_sems.at[k - 1], recv_sem=recv_sems.at[k - 1],
                device_id=(px, py, pc), device_id_type=MESH_IDS))
            recvs.append(pltpu.make_async_remote_copy(
                src_ref=v_ref, dst_ref=buf.at[peer], send_sem=send_sems.at[k - 1], recv_sem=recv_sems.at[k - 1],
                device_id=(px, py, pc), device_id_type=MESH_IDS))
        for cp in sends:
            cp.start()
        for cp in recvs:
            cp.wait_recv()
        for cp in sends:
            cp.wait_send()
        acc = buf[0]
        for d in range(1, N_DEV):
            acc = acc + buf[d]
        out_ref[...] = acc

    vm = pl.BlockSpec(memory_space=pltpu.VMEM)
    return pl.pallas_call(
        body, name="vector_grad_all_reduce", out_shape=jax.ShapeDtypeStruct((r, c_), F32), in_specs=[vm], out_specs=vm,
        scratch_shapes=[pltpu.VMEM((N_DEV, r, c_), F32), pltpu.SemaphoreType.DMA((7,)), pltpu.SemaphoreType.DMA((7,))],
    )(v)


def _adamw_math(w, g, m, v):
    m = ADAM_B1 * m + (1.0 - ADAM_B1) * g
    v = ADAM_B2 * v + (1.0 - ADAM_B2) * (g * g)
    m_hat = m / (1.0 - ADAM_B1 ** ADAM_STEP)
    v_hat = v / (1.0 - ADAM_B2 ** ADAM_STEP)
    delta = -ADAM_LR * (m_hat / (jnp.sqrt(v_hat) + ADAM_EPS) + ADAM_WD * w)
    return delta, m, v


def _sum_and_adamw(parts, w, m, v):
    _, r, c_ = parts.shape
    br = _pick(r, (160, 128, 64, 32, 16, 8))

    def body(p_ref, w_ref, m_ref, v_ref, g_ref, d_ref, nm_ref, nv_ref):
        g = p_ref[0]
        for d in range(1, N_DEV):
            g = g + p_ref[d]
        g_ref[...] = g
        d_ref[...], nm_ref[...], nv_ref[...] = _adamw_math(w_ref[...], g, m_ref[...], v_ref[...])

    row = pl.BlockSpec((br, c_), lambda i: (i, 0))
    return pl.pallas_call(
        body, name="grad_sum_adamw", grid=(r // br,),
        in_specs=[pl.BlockSpec((N_DEV, br, c_), lambda i: (0, i, 0)), row, row, row], out_specs=(row,) * 4,
        out_shape=(jax.ShapeDtypeStruct((r, c_), F32),) * 4, compiler_params=_params("parallel"),
    )(parts, w, m, v)


def _adamw_small(w, g, m, v):
    vm = pl.BlockSpec(memory_space=pltpu.VMEM)

    def body(w_ref, g_ref, m_ref, v_ref, d_ref, nm_ref, nv_ref):
        d_ref[...], nm_ref[...], nv_ref[...] = _adamw_math(w_ref[...], g_ref[...], m_ref[...], v_ref[...])

    return pl.pallas_call(
        body, name="vector_adamw", in_specs=[vm] * 4, out_specs=(vm,) * 3,
        out_shape=(jax.ShapeDtypeStruct(w.shape, F32),) * 3,
    )(w, g, m, v)


def _pack_local(named):
    rows = [named[n].reshape(-1, PACK_COLS) for n, _, _, _ in MATRICES]
    rows.append(jnp.zeros((MAT_ROWS_PAD - MAT_ROWS, PACK_COLS), rows[0].dtype))
    return jnp.concatenate(rows, axis=0)


def _unpack_local(packed):
    out, r0 = {}, 0
    for n, kind, k, nn in MATRICES:
        nr = k * nn // N_DEV // PACK_COLS
        shape = (k, nn // N_DEV) if kind == "c" else (k // N_DEV, nn)
        out[n] = packed[r0:r0 + nr].reshape(shape)
        r0 += nr
    return out


def _unpack_gathered(g):
    out, r0 = {}, 0
    for n, kind, k, nn in MATRICES:
        nr = k * nn // N_DEV // PACK_COLS
        blk = g[:, r0:r0 + nr]
        if kind == "c":
            out[n] = blk.reshape(N_DEV, k, nn // N_DEV).transpose(1, 0, 2).reshape(k, nn)
        else:
            out[n] = blk.reshape(k, nn)
        r0 += nr
    return out


def _pack_full_grads(grads):
    rows = []
    for n, kind, k, nn in MATRICES:
        gmat = grads[n]
        if kind == "c":
            gmat = gmat.reshape(k, N_DEV, nn // N_DEV).transpose(1, 0, 2)
        rows.append(gmat.reshape(N_DEV, -1, PACK_COLS))
    rows.append(jnp.zeros((N_DEV, MAT_ROWS_PAD - MAT_ROWS, PACK_COLS), F32))
    return jnp.concatenate(rows, axis=1)


def _pack_vectors(named):
    rows = [jnp.pad(named[n].astype(F32), (0, PACK_COLS - d)) for n, d in VECTORS]
    rows += [jnp.zeros((PACK_COLS,), F32)] * (VEC_ROWS - len(VECTORS))
    return jnp.stack(rows, axis=0)


def _unpack_vectors(packed):
    return {n: packed[i, :d] for i, (n, d) in enumerate(VECTORS)}


def _step(inputs):
    x = inputs["x"][0]
    mem = inputs["mem"][0]
    positions = inputs["positions"][0]
    target = inputs["loss_target"][0]

    local_w = _pack_local({n: inputs[n] for n, _, _, _ in MATRICES})
    gathered = _all_gather_rows(local_w.astype(BF16))
    wfull = _unpack_gathered(gathered)
    vec = {n: inputs[n] for n, _ in VECTORS}

    loss_part, grad_x, gfull, gvec = _local_grads(wfull, vec, x, mem, positions, target)
    loss = lax.psum(loss_part, ("x", "y", "c"))

    parts = _exchange_slabs(_pack_full_grads(gfull))
    local_m = _pack_local({n: inputs["m_" + n] for n, _, _, _ in MATRICES})
    local_v = _pack_local({n: inputs["v_" + n] for n, _, _, _ in MATRICES})
    g_pk, d_pk, m_pk, v_pk = _sum_and_adamw(parts, local_w, local_m, local_v)
    g_mat, d_mat, m_mat, v_mat = (_unpack_local(t) for t in (g_pk, d_pk, m_pk, v_pk))

    g_vec_pk = _all_reduce_small(_pack_vectors(gvec))
    d_vec_pk, m_vec_pk, v_vec_pk = _adamw_small(
        _pack_vectors(vec), g_vec_pk, _pack_vectors({n: inputs["m_" + n] for n, _ in VECTORS}),
        _pack_vectors({n: inputs["v_" + n] for n, _ in VECTORS}))
    g_vec, d_vec, m_vec, v_vec = (_unpack_vectors(t) for t in (g_vec_pk, d_vec_pk, m_vec_pk, v_vec_pk))

    def pick(mats, vecs, n):
        return mats[n] if n in mats else vecs[n]

    outs = [loss, grad_x[None]]
    for mats, vecs in ((g_mat, g_vec), (d_mat, d_vec), (m_mat, m_vec), (v_mat, v_vec)):
        outs += [pick(mats, vecs, n) for n in WEIGHT_ORDER]
    return tuple(outs)


def _local_grads(wfull, vec, x, mem, positions, target):
    w = {}
    for n, _, _, _ in MATRICES:
        if n.endswith("w_gate") or n.endswith("w_up"):
            continue
        w[n] = wfull[n]
    w["l0_w_in"] = jnp.pad(wfull["l0_w_in"], ((0, 0), (0, EVEN_IN_PAD - EVEN_IN)))
    for layer in (0, 1):
        p = "l%d_" % layer
        w[p + "w_gu"] = jnp.concatenate([wfull[p + "w_gate"], wfull[p + "w_up"]], axis=1)
    slots = {n: jnp.zeros(t.shape, F32) for n, t in w.items()}

    tab64 = _rope_tables(positions, HEAD_DIM, 0, HEAD_DIM)
    tab_mla = _rope_tables(positions, MLA_ROPE_DIM, MLA_NOPE_DIM, LANES)
    diff = {"x": x, "slots": slots, "vec": vec}
    consts = {"w": w, "mem": mem, "tab64": tab64, "tab_mla": tab_mla, "target": target}
    loss_part, grads = jax.value_and_grad(_model_loss)(diff, consts)

    gfull = dict(grads["slots"])
    gfull["l0_w_in"] = gfull["l0_w_in"][:, :EVEN_IN]
    for layer in (0, 1):
        p = "l%d_" % layer
        gu = gfull.pop(p + "w_gu")
        gfull[p + "w_gate"], gfull[p + "w_up"] = gu[:, :FFN_HIDDEN], gu[:, FFN_HIDDEN:]
    return loss_part, grads["x"], gfull, grads["vec"]


_INPUT_NAMES = (("x", "mem", "positions") + WEIGHT_ORDER + ("loss_target",)
                + tuple("m_" + n for n in WEIGHT_ORDER) + tuple("v_" + n for n in WEIGHT_ORDER))


def kernel(*args):
    assert len(args) == len(_INPUT_NAMES)
    return _step(dict(zip(_INPUT_NAMES, args)))
```

```python
import functools
import math

import numpy as np
import jax
import jax.numpy as jnp
from jax import lax
from jax.experimental import pallas as pl
from jax.experimental.pallas import tpu as pltpu

F32 = jnp.float32
BF16 = jnp.bfloat16

LANES = 128
VMEM_LIMIT_BYTES = 56 * 1024 * 1024
MM_VMEM_BUDGET = 40 * 1024 * 1024
MM_MIN_FLOP_PER_STEP = 1e9
BAND_UNITS_PER_STEP = 4
CAUSAL_ROW_CHAIN = 128

D_MODEL = 1024
HEAD_DIM = 64
ROPE_THETA = 10000.0
NORM_EPS = 1e-6
BLOCK = 128
SWA_HEADS = 8
SWA_KV_HEADS = 2
SWA_WINDOW = 128
MLA_HEADS = 8
MLA_Q_RANK = 384
MLA_KV_RANK = 256
MLA_NOPE_DIM = 64
MLA_ROPE_DIM = 32
A_Q = SWA_HEADS * HEAD_DIM
A_KV = SWA_KV_HEADS * HEAD_DIM
EVEN_IN = A_Q + 2 * A_KV + MLA_Q_RANK + MLA_KV_RANK + MLA_ROPE_DIM
EVEN_IN_PAD = 1536
DIL_PATTERNS = ((128, 1), (512, 4), (2048, 16))
X_HEADS = 4
X_HEAD_DIM = 128
FFN_HIDDEN = 2816

ADAM_LR = 0.001
ADAM_B1 = 0.9
ADAM_B2 = 0.999
ADAM_EPS = 1e-08
ADAM_WD = 0.01
ADAM_STEP = 10

N_DEV = 8
GRAD_WIRE_DTYPE = BF16
NEG_MASK = -1e30
NEG_INIT = -1e20

MATRICES = (
    ("l0_w_in", "c", 1024, 1440), ("l0_w_uq", "c", 384, 768), ("l0_w_ukv", "c", 256, 1024),
    ("l0_w_out", "r", 1024, 1024), ("l0_w_xq", "r", 1024, 512), ("l0_w_xkv", "r", 1024, 1024),
    ("l0_w_xo", "c", 512, 1024), ("l0_w_gate", "c", 1024, 2816), ("l0_w_up", "c", 1024, 2816),
    ("l0_w_down", "r", 2816, 1024),
    ("l1_w_qkv", "c", 1024, 3072), ("l1_w_out", "r", 1024, 1024), ("l1_w_xq", "r", 1024, 512),
    ("l1_w_xkv", "r", 1024, 1024), ("l1_w_xo", "c", 512, 1024), ("l1_w_gate", "c", 1024, 2816),
    ("l1_w_up", "c", 1024, 2816), ("l1_w_down", "r", 2816, 1024),
)
VECTORS = (
    ("l0_mix_norm", 1024), ("l0_sinks", 8), ("l0_q_norm", 384), ("l0_kv_norm", 256), ("l0_x_norm", 1024),
    ("l0_mem_norm", 1024), ("l0_ffn_norm", 1024), ("l1_mix_norm", 1024), ("l1_x_norm", 1024),
    ("l1_mem_norm", 1024), ("l1_ffn_norm", 1024), ("final_norm", 1024),
)
WEIGHT_ORDER = (
    "l0_mix_norm", "l0_w_in", "l0_sinks", "l0_q_norm", "l0_w_uq", "l0_kv_norm", "l0_w_ukv", "l0_w_out", "l0_x_norm",
    "l0_mem_norm", "l0_w_xq", "l0_w_xkv", "l0_w_xo", "l0_ffn_norm", "l0_w_gate", "l0_w_up", "l0_w_down",
    "l1_mix_norm", "l1_w_qkv", "l1_w_out", "l1_x_norm", "l1_mem_norm", "l1_w_xq", "l1_w_xkv", "l1_w_xo",
    "l1_ffn_norm", "l1_w_gate", "l1_w_up", "l1_w_down", "final_norm",
)
PACK_COLS = 1024
PART_ROW_ALIGN = 16


def _part_rows(k, n):
    return -(-(k * n // N_DEV // PACK_COLS) // PART_ROW_ALIGN) * PART_ROW_ALIGN


MAT_ROWS_USED = sum(_part_rows(k, n) for _, _, k, n in MATRICES)
MAT_ROWS = -(-MAT_ROWS_USED // 256) * 256
VEC_ROWS = 16


def _pick(n, cands):
    for c in cands:
        if n % c == 0:
            return c
    return n


def _params(*sem):
    return pltpu.CompilerParams(dimension_semantics=sem, vmem_limit_bytes=VMEM_LIMIT_BYTES)


_DIMS = {"nn": (((1,), (0,)), ((), ())), "nt": (((1,), (1,)), ((), ())), "tn": (((0,), (0,)), ((), ()))}


def _div128(n, cap):
    d = (min(n, cap) // LANES) * LANES
    while d >= LANES:
        if n % d == 0:
            return d
        d -= LANES
    return n


def _mm_vmem_bytes(bm, bn, bk, nk, sa, sb, so, has_res):
    est = 2 * (bm * bk * sa + bk * bn * sb + bm * bn * so) + bm * bn * 4
    est += bm * bn * 4 if nk > 1 else 0
    est += 2 * bm * bn * 4 if has_res else 0
    est += bm * bk * 2 if sa == 4 else 0
    est += bk * bn * 2 if sb == 4 else 0
    return est


def _mm_tiles(m, n, k, sa, sb, so, has_res, mode):
    bn = _div128(n, 1536)
    kcap = 2048 if mode == "tn" else k
    for bm_cap in ((1024, 2048) if mode == "tn" else (512, 1024, 2048)):
        bm = _div128(m, bm_cap)
        bk = (min(k, kcap) // LANES) * LANES
        while bk > LANES and (k % bk or _mm_vmem_bytes(bm, bn, bk, k // bk, sa, sb, so, has_res) > MM_VMEM_BUDGET):
            bk -= LANES
        if 2 * bm * bn * bk >= MM_MIN_FLOP_PER_STEP or bm == m:
            break
    return bm, bn, bk


def _mm(a, b, mode, name, out_dtype=F32, res=None):
    if mode == "nn":
        (m, k), (k2, n) = a.shape, b.shape
    elif mode == "nt":
        (m, k), (n, k2) = a.shape, b.shape
    else:
        (k, m), (k2, n) = a.shape, b.shape
    assert k == k2, (name, a.shape, b.shape)
    has_res = res is not None
    bm, bn, bk = _mm_tiles(m, n, k, a.dtype.itemsize, b.dtype.itemsize, jnp.dtype(out_dtype).itemsize, has_res, mode)
    nk = k // bk
    dims = _DIMS[mode]
    a_spec = pl.BlockSpec((bk, bm), lambda i, j, kk: (kk, i)) if mode == "tn" else pl.BlockSpec((bm, bk), lambda i, j, kk: (i, kk))
    b_spec = pl.BlockSpec((bn, bk), lambda i, j, kk: (j, kk)) if mode == "nt" else pl.BlockSpec((bk, bn), lambda i, j, kk: (kk, j))
    o_spec = pl.BlockSpec((bm, bn), lambda i, j, kk: (i, j))

    def body(*refs):
        a_ref, b_ref = refs[0], refs[1]
        r_ref = refs[2] if has_res else None
        o_ref = refs[3] if has_res else refs[2]
        part = lax.dot_general(a_ref[...].astype(BF16), b_ref[...].astype(BF16), dims, preferred_element_type=F32)
        if nk == 1:
            o_ref[...] = (part + r_ref[...] if has_res else part).astype(out_dtype)
            return
        acc = refs[-1]
        kk = pl.program_id(2)

        @pl.when(kk == 0)
        def _():
            acc[...] = part

        @pl.when(jnp.logical_and(kk > 0, kk < nk - 1))
        def _():
            acc[...] += part

        @pl.when(kk == nk - 1)
        def _():
            r = acc[...] + part
            if has_res:
                r = r + r_ref[...]
            o_ref[...] = r.astype(out_dtype)

    args = (a, b, res) if has_res else (a, b)
    in_specs = [a_spec, b_spec] + ([o_spec] if has_res else [])
    return pl.pallas_call(
        body, name=name, grid=(m // bm, n // bn, nk), in_specs=in_specs, out_specs=o_spec,
        out_shape=jax.ShapeDtypeStruct((m, n), out_dtype),
        scratch_shapes=[pltpu.VMEM((bm, bn), F32)] if nk > 1 else [],
        compiler_params=_params("parallel", "parallel", "arbitrary"),
    )(*args)


def _rms_fwd(x, g, name, out_dtype=BF16):
    s, d = x.shape
    bs = _pick(s, (512, 256, 128))

    def body(x_ref, g_ref, o_ref):
        xv = x_ref[...]
        r = lax.rsqrt(jnp.mean(xv * xv, axis=-1, keepdims=True) + NORM_EPS)
        o_ref[...] = ((xv * r) * g_ref[...]).astype(out_dtype)

    return pl.pallas_call(
        body, name=name, grid=(s // bs,),
        in_specs=[pl.BlockSpec((bs, d), lambda i: (i, 0)), pl.BlockSpec((1, d), lambda i: (0, 0))],
        out_specs=pl.BlockSpec((bs, d), lambda i: (i, 0)), out_shape=jax.ShapeDtypeStruct((s, d), out_dtype),
        compiler_params=_params("parallel"),
    )(x, g.reshape(1, d))


def _rms_bwd(x, g, dy, name, dres=None):
    s, d = x.shape
    bs = _pick(s, (512, 256, 128))
    has_res = dres is not None

    def body(*refs):
        if has_res:
            x_ref, g_ref, dy_ref, r_ref, dx_ref, dg_ref = refs
        else:
            x_ref, g_ref, dy_ref, dx_ref, dg_ref = refs
        i = pl.program_id(0)
        xv = x_ref[...]
        dy = dy_ref[...]
        r = lax.rsqrt(jnp.mean(xv * xv, axis=-1, keepdims=True) + NORM_EPS)
        xh = xv * r
        dxh = dy * g_ref[...]
        dx = r * (dxh - xh * jnp.mean(dxh * xh, axis=-1, keepdims=True))
        if has_res:
            dx = dx + r_ref[...]
        dx_ref[...] = dx

        @pl.when(i == 0)
        def _():
            dg_ref[...] = jnp.zeros_like(dg_ref)

        dg_ref[...] += jnp.sum(dy * xh, axis=0, keepdims=True)

    row = pl.BlockSpec((bs, d), lambda i: (i, 0))
    vec = pl.BlockSpec((1, d), lambda i: (0, 0))
    args = (x, g.reshape(1, d), dy) + ((dres,) if has_res else ())
    dx, dg = pl.pallas_call(
        body, name=name, grid=(s // bs,), in_specs=[row, vec, row] + ([row] if has_res else []),
        out_specs=(row, vec), out_shape=(jax.ShapeDtypeStruct((s, d), F32), jax.ShapeDtypeStruct((1, d), F32)),
        compiler_params=_params("arbitrary"),
    )(*args)
    return dx, dg.reshape(d)


def _rope_tables(positions, dh, offset, period):
    role = np.zeros(LANES, np.int32)
    for base in range(0, LANES, period):
        role[base + offset:base + offset + dh // 2] = 1
        role[base + offset + dh // 2:base + offset + dh] = 2
    inv_freq = ROPE_THETA ** (-jnp.arange(0, dh, 2, dtype=F32) / dh)
    one_period = jnp.concatenate([jnp.zeros((offset,), F32), inv_freq, inv_freq,
                                  jnp.zeros((period - offset - dh,), F32)])
    ang = positions.astype(F32)[:, None] * jnp.tile(one_period, LANES // period)[None, :]
    c, s = jnp.cos(ang), jnp.sin(ang)
    role = role[None, :]
    a = jnp.where(role == 0, 1.0, c).astype(F32)
    bm = jnp.where(role == 2, s, 0.0).astype(F32)
    bp = jnp.where(role == 1, -s, 0.0).astype(F32)
    return a, bm, bp


def _rope_apply(x, tabs, half, transpose, name):
    s, w = x.shape
    bs = _pick(s, (512, 256, 128))
    nc = w // LANES
    a, bm, bp = tabs

    def body(x_ref, a_ref, bm_ref, bp_ref, o_ref):
        av, bmv, bpv = a_ref[...], bm_ref[...], bp_ref[...]
        for c in range(nc):
            sl = slice(c * LANES, (c + 1) * LANES)
            xv = x_ref[:, sl]
            if transpose:
                o_ref[:, sl] = xv * av + pltpu.roll(xv * bmv, LANES - half, 1) + pltpu.roll(xv * bpv, half, 1)
            else:
                o_ref[:, sl] = xv * av + pltpu.roll(xv, half, 1) * bmv + pltpu.roll(xv, LANES - half, 1) * bpv

    row = pl.BlockSpec((bs, w), lambda i: (i, 0))
    tab = pl.BlockSpec((bs, LANES), lambda i: (i, 0))
    return pl.pallas_call(
        body, name=name, grid=(s // bs,), in_specs=[row, tab, tab, tab], out_specs=row,
        out_shape=jax.ShapeDtypeStruct((s, w), F32), compiler_params=_params("parallel"),
    )(x, a, bm, bp)


def _make_rope(half, name):
    @jax.custom_vjp
    def rope(x, a, bm, bp):
        return _rope_apply(x, (a, bm, bp), half, False, name + "_fwd")

    def fwd(x, a, bm, bp):
        return rope(x, a, bm, bp), (a, bm, bp)

    def bwd(tabs, dy):
        return _rope_apply(dy, tabs, half, True, name + "_bwd"), None, None, None

    rope.defvjp(fwd, bwd)
    return rope


class AttnCfg:
    def __init__(self, mode, scale, hpb, bq, bk, upb, max_dist=0):
        self.mode, self.scale, self.hpb, self.bq, self.bk, self.upb, self.max_dist = mode, scale, hpb, bq, bk, upb, max_dist


def _kv_of_q(cfg, nq, nk):
    if cfg.mode == "causal":
        return nk, lambda i, j: (jnp.minimum(j, i), j <= i)
    if cfg.mode == "band":
        return 2, lambda i, j: (jnp.maximum(i - 1 + j, 0), i - 1 + j >= 0)
    return nk, lambda i, j: (j, j >= 0)


def _q_of_kv(cfg, nq, nk):
    if cfg.mode == "causal":
        return nq, lambda kb, j: (jnp.maximum(j, kb), j >= kb)
    if cfg.mode == "band":
        return 2, lambda kb, j: (jnp.minimum(kb + j, nq - 1), kb + j <= nq - 1)
    return nq, lambda kb, j: (j, j >= 0)


def _attn_mask(cfg, i, kb):
    if cfg.mode == "full":
        return None
    qpos = i * cfg.bq + lax.broadcasted_iota(jnp.int32, (cfg.bq, cfg.bk), 0)
    kpos = kb * cfg.bk + lax.broadcasted_iota(jnp.int32, (cfg.bq, cfg.bk), 1)
    dist = qpos - kpos
    if cfg.mode == "causal":
        return dist >= 0
    return (dist >= 0) & (dist <= cfg.max_dist)


def _lane_masks():
    lane = lax.broadcasted_iota(jnp.int32, (1, LANES), 1)
    lo = lane < HEAD_DIM
    return [lo, jnp.logical_not(lo)]


def _sel(mask, v):
    return jnp.where(mask, v, jnp.zeros_like(v))


_NT = (((1,), (1,)), ((), ()))
_NN = (((1,), (0,)), ((), ()))
_TN = (((0,), (0,)), ((), ()))


def _dot(a, b, dims):
    return lax.dot_general(a, b, dims, preferred_element_type=F32)


def _attn_fwd(q, k, v, sinkrow, cfg, name):
    sq, w = q.shape
    sk = k.shape[0]
    bq, bk, upb, hpb = cfg.bq, cfg.bk, cfg.upb, cfg.hpb
    nq, nk, nub = sq // bq, sk // bk, w // (LANES * upb)
    nj, sched = _kv_of_q(cfg, nq, nk)
    wb = LANES * upb
    has_sink = sinkrow is not None

    def body(*refs):
        if has_sink:
            q_ref, k_ref, v_ref, s_ref, o_ref, l_ref, m_sc, l_sc, acc_sc = refs
        else:
            q_ref, k_ref, v_ref, o_ref, l_ref, m_sc, l_sc, acc_sc = refs
        i, j = pl.program_id(1), pl.program_id(2)
        kb, active = sched(i, j)
        lms = _lane_masks()

        @pl.when(j == 0)
        def _():
            for u in range(upb):
                for a in range(hpb):
                    if has_sink:
                        srow = s_ref[:, u * LANES:(u + 1) * LANES]
                        sk_a = jnp.max(jnp.where(lms[a], srow, -jnp.inf), axis=-1, keepdims=True)
                        m_sc[u * hpb + a] = jnp.broadcast_to(sk_a, (bq, 1))
                        l_sc[u * hpb + a] = jnp.ones((bq, 1), F32)
                    else:
                        m_sc[u * hpb + a] = jnp.full((bq, 1), NEG_INIT, F32)
                        l_sc[u * hpb + a] = jnp.zeros((bq, 1), F32)
            acc_sc[...] = jnp.zeros_like(acc_sc)

        @pl.when(active)
        def _():
            mask = _attn_mask(cfg, i, kb)
            for u in range(upb):
                sl = slice(u * LANES, (u + 1) * LANES)
                qv = q_ref[:, sl].astype(BF16)
                kv = k_ref[:, sl].astype(BF16)
                vv = v_ref[:, sl].astype(BF16)
                pv_tot, alphas = None, []
                for a in range(hpb):
                    idx = u * hpb + a
                    qa = _sel(lms[a], qv) if hpb == 2 else qv
                    s = _dot(qa, kv, _NT) * cfg.scale
                    if mask is not None:
                        s = jnp.where(mask, s, NEG_MASK)
                    m_prev = m_sc[idx]
                    m_new = jnp.maximum(m_prev, jnp.max(s, axis=-1, keepdims=True))
                    alpha = jnp.exp(m_prev - m_new)
                    p = jnp.exp(s - m_new)
                    l_sc[idx] = alpha * l_sc[idx] + jnp.sum(p, axis=-1, keepdims=True)
                    m_sc[idx] = m_new
                    va = _sel(lms[a], vv) if hpb == 2 else vv
                    pv = _dot(p.astype(BF16), va, _NN)
                    pv_tot = pv if pv_tot is None else pv_tot + pv
                    alphas.append(alpha)
                af = alphas[0] if hpb == 1 else jnp.where(lms[0], alphas[0], alphas[1])
                acc_sc[u] = acc_sc[u] * af + pv_tot

        @pl.when(j == nj - 1)
        def _():
            for u in range(upb):
                sl = slice(u * LANES, (u + 1) * LANES)
                if hpb == 1:
                    lf = jnp.broadcast_to(l_sc[u], (bq, LANES))
                    mf = jnp.broadcast_to(m_sc[u], (bq, LANES))
                else:
                    lf = jnp.where(lms[0], l_sc[2 * u], l_sc[2 * u + 1])
                    mf = jnp.where(lms[0], m_sc[2 * u], m_sc[2 * u + 1])
                o_ref[:, sl] = acc_sc[u] / lf
                l_ref[:, sl] = mf + jnp.log(lf)

    qspec = pl.BlockSpec((bq, wb), lambda ub, i, j: (i, ub))
    kspec = pl.BlockSpec((bk, wb), lambda ub, i, j: (sched(i, j)[0], ub))
    in_specs = [qspec, kspec, kspec] + ([pl.BlockSpec((1, wb), lambda ub, i, j: (0, ub))] if has_sink else [])
    args = (q, k, v) + ((sinkrow,) if has_sink else ())
    return pl.pallas_call(
        body, name=name, grid=(nub, nq, nj), in_specs=in_specs, out_specs=(qspec, qspec),
        out_shape=(jax.ShapeDtypeStruct((sq, w), F32), jax.ShapeDtypeStruct((sq, w), F32)),
        scratch_shapes=[pltpu.VMEM((upb * hpb, bq, 1), F32), pltpu.VMEM((upb * hpb, bq, 1), F32),
                        pltpu.VMEM((upb, bq, LANES), F32)],
        compiler_params=_params("parallel", "parallel", "arbitrary"),
    )(*args)


def _softmax_grad_terms(cfg, lms, a, qv, kv, vv, dob, prod, lv, mask):
    hpb = cfg.hpb
    if hpb == 2:
        t = jnp.sum(_sel(lms[a], prod), axis=-1, keepdims=True)
        lse = jnp.max(jnp.where(lms[a], lv, -jnp.inf), axis=-1, keepdims=True)
        qa, doa = _sel(lms[a], qv), _sel(lms[a], dob)
    else:
        t = jnp.sum(prod, axis=-1, keepdims=True)
        lse = jnp.max(lv, axis=-1, keepdims=True)
        qa, doa = qv, dob
    s = _dot(qa, kv, _NT) * cfg.scale
    if mask is not None:
        s = jnp.where(mask, s, NEG_MASK)
    p = jnp.exp(s - lse)
    dp = _dot(doa, vv, _NT)
    ds = (p * (dp - t)) * cfg.scale
    return p, ds, qa, doa, t


def _attn_dq(q, k, v, o, lse, do, sinkrow, cfg, name):
    sq, w = q.shape
    sk = k.shape[0]
    bq, bk, upb, hpb = cfg.bq, cfg.bk, cfg.upb, cfg.hpb
    nq, nk, nub = sq // bq, sk // bk, w // (LANES * upb)
    nj, sched = _kv_of_q(cfg, nq, nk)
    wb = LANES * upb
    has_sink = sinkrow is not None

    def body(*refs):
        if has_sink:
            q_ref, k_ref, v_ref, o_ref, l_ref, do_ref, s_ref, dq_ref, dsink_ref, acc = refs
        else:
            q_ref, k_ref, v_ref, o_ref, l_ref, do_ref, dq_ref, acc = refs
        i, j = pl.program_id(1), pl.program_id(2)
        kb, active = sched(i, j)
        lms = _lane_masks()

        @pl.when(j == 0)
        def _():
            acc[...] = jnp.zeros_like(acc)

        @pl.when(active)
        def _():
            mask = _attn_mask(cfg, i, kb)
            for u in range(upb):
                sl = slice(u * LANES, (u + 1) * LANES)
                qv = q_ref[:, sl].astype(BF16)
                kv = k_ref[:, sl].astype(BF16)
                vv = v_ref[:, sl].astype(BF16)
                dov = do_ref[:, sl]
                prod = dov * o_ref[:, sl]
                dob = dov.astype(BF16)
                lv = l_ref[:, sl]
                tot = None
                for a in range(hpb):
                    _, ds, _, _, _ = _softmax_grad_terms(cfg, lms, a, qv, kv, vv, dob, prod, lv, mask)
                    ka = _sel(lms[a], kv) if hpb == 2 else kv
                    c = _dot(ds.astype(BF16), ka, _NN)
                    tot = c if tot is None else tot + c
                acc[u] = acc[u] + tot

        @pl.when(j == nj - 1)
        def _():
            for u in range(upb):
                dq_ref[:, u * LANES:(u + 1) * LANES] = acc[u]
            if has_sink:
                @pl.when(i == 0)
                def _():
                    dsink_ref[...] = jnp.zeros_like(dsink_ref)

                for u in range(upb):
                    sl = slice(u * LANES, (u + 1) * LANES)
                    prod = do_ref[:, sl] * o_ref[:, sl]
                    t0 = jnp.sum(_sel(lms[0], prod), axis=-1, keepdims=True)
                    t1 = jnp.sum(_sel(lms[1], prod), axis=-1, keepdims=True)
                    tf = jnp.where(lms[0], t0, t1)
                    rs = -jnp.exp(s_ref[:, sl] - l_ref[:, sl]) * tf
                    dsink_ref[0:1, sl] += jnp.sum(rs, axis=0, keepdims=True)

    qspec = pl.BlockSpec((bq, wb), lambda ub, i, j: (i, ub))
    kspec = pl.BlockSpec((bk, wb), lambda ub, i, j: (sched(i, j)[0], ub))
    in_specs = [qspec, kspec, kspec, qspec, qspec, qspec]
    args = (q, k, v, o, lse, do)
    out_specs = qspec
    out_shape = jax.ShapeDtypeStruct((sq, w), F32)
    sem = ("parallel", "parallel", "arbitrary")
    if has_sink:
        in_specs = in_specs + [pl.BlockSpec((1, wb), lambda ub, i, j: (0, ub))]
        args = args + (sinkrow,)
        out_specs = (qspec, pl.BlockSpec((8, wb), lambda ub, i, j: (0, ub)))
        out_shape = (out_shape, jax.ShapeDtypeStruct((8, w), F32))
        sem = ("parallel", "arbitrary", "arbitrary")
    return pl.pallas_call(
        body, name=name, grid=(nub, nq, nj), in_specs=in_specs, out_specs=out_specs, out_shape=out_shape,
        scratch_shapes=[pltpu.VMEM((upb, bq, LANES), F32)], compiler_params=_params(*sem),
    )(*args)


def _attn_dkv(q, k, v, o, lse, do, cfg, name):
    sq, w = q.shape
    sk = k.shape[0]
    bq, bk, upb, hpb = cfg.bq, cfg.bk, cfg.upb, cfg.hpb
    nq, nk, nub = sq // bq, sk // bk, w // (LANES * upb)
    nj, sched = _q_of_kv(cfg, nq, nk)
    wb = LANES * upb

    def body(q_ref, k_ref, v_ref, o_ref, l_ref, do_ref, dk_ref, dv_ref, dk_acc, dv_acc):
        kb, j = pl.program_id(1), pl.program_id(2)
        i, active = sched(kb, j)
        lms = _lane_masks()

        @pl.when(j == 0)
        def _():
            dk_acc[...] = jnp.zeros_like(dk_acc)
            dv_acc[...] = jnp.zeros_like(dv_acc)

        @pl.when(active)
        def _():
            mask = _attn_mask(cfg, i, kb)
            for u in range(upb):
                sl = slice(u * LANES, (u + 1) * LANES)
                qv = q_ref[:, sl].astype(BF16)
                kv = k_ref[:, sl].astype(BF16)
                vv = v_ref[:, sl].astype(BF16)
                dov = do_ref[:, sl]
                prod = dov * o_ref[:, sl]
                dob = dov.astype(BF16)
                lv = l_ref[:, sl]
                dk_tot, dv_tot = None, None
                for a in range(hpb):
                    p, ds, qa, doa, _ = _softmax_grad_terms(cfg, lms, a, qv, kv, vv, dob, prod, lv, mask)
                    dvc = _dot(p.astype(BF16), doa, _TN)
                    dkc = _dot(ds.astype(BF16), qa, _TN)
                    dv_tot = dvc if dv_tot is None else dv_tot + dvc
                    dk_tot = dkc if dk_tot is None else dk_tot + dkc
                dk_acc[u] = dk_acc[u] + dk_tot
                dv_acc[u] = dv_acc[u] + dv_tot

        @pl.when(j == nj - 1)
        def _():
            for u in range(upb):
                sl = slice(u * LANES, (u + 1) * LANES)
                dk_ref[:, sl] = dk_acc[u]
                dv_ref[:, sl] = dv_acc[u]

    qspec = pl.BlockSpec((bq, wb), lambda ub, kb, j: (sched(kb, j)[0], ub))
    kspec = pl.BlockSpec((bk, wb), lambda ub, kb, j: (kb, ub))
    return pl.pallas_call(
        body, name=name, grid=(nub, nk, nj), in_specs=[qspec, kspec, kspec, qspec, qspec, qspec],
        out_specs=(kspec, kspec),
        out_shape=(jax.ShapeDtypeStruct((sk, w), F32), jax.ShapeDtypeStruct((sk, w), F32)),
        scratch_shapes=[pltpu.VMEM((upb, bk, LANES), F32), pltpu.VMEM((upb, bk, LANES), F32)],
        compiler_params=_params("parallel", "parallel", "arbitrary"),
    )(q, k, v, o, lse, do)


def _band_masks(max_dist):
    assert BLOCK - 1 <= max_dist <= BLOCK
    r = lax.broadcasted_iota(jnp.int32, (BLOCK, BLOCK), 0)
    c = lax.broadcasted_iota(jnp.int32, (BLOCK, BLOCK), 1)
    return (BLOCK + r - c) <= max_dist, r >= c


def _head_terms(lms, a, prod, lv):
    t = jnp.sum(_sel(lms[a], prod), axis=-1, keepdims=True)
    lse = jnp.max(jnp.where(lms[a], lv, -jnp.inf), axis=-1, keepdims=True)
    return t, lse


class _Residue:
    def __init__(self, ref, r, dil):
        self.ref, self.rows = ref, pl.ds(r, BLOCK, stride=dil)

    def __getitem__(self, idx):
        return self.ref[self.rows, idx[1]]

    def __setitem__(self, idx, val):
        self.ref[self.rows, idx[1]] = val


def _residues(refs, dil):
    if dil == 1:
        return [tuple(refs)]
    return [tuple(_Residue(x, r, dil) for x in refs) for r in range(dil)]


def _band_fwd(q, k, v, sinkrow, scale, max_dist, upb, dil, name):
    sq, w = q.shape
    rb = BLOCK * dil
    nq, nub, wb = sq // rb, w // (LANES * upb), LANES * upb
    has_sink = sinkrow is not None

    def body(*refs):
        s_ref = refs[5] if has_sink else None
        lms = _lane_masks()
        mprev, mcur = _band_masks(max_dist)
        mprev = jnp.logical_and(mprev, pl.program_id(1) > 0)
        for q_ref, kp_ref, kc_ref, vp_ref, vc_ref, o_ref, l_ref in _residues(refs[:5] + refs[-2:], dil):
          for u in range(upb):
            sl = slice(u * LANES, (u + 1) * LANES)
            qv = (q_ref[:, sl] * scale).astype(BF16)
            kp, kc = kp_ref[:, sl].astype(BF16), kc_ref[:, sl].astype(BF16)
            vp, vc = vp_ref[:, sl].astype(BF16), vc_ref[:, sl].astype(BF16)
            outs, lses = [], []
            for a in range(2):
                qa = _sel(lms[a], qv)
                s0 = jnp.where(mprev, _dot(qa, kp, _NT), NEG_MASK)
                s1 = jnp.where(mcur, _dot(qa, kc, _NT), NEG_MASK)
                m = jnp.maximum(jnp.max(s0, axis=-1, keepdims=True), jnp.max(s1, axis=-1, keepdims=True))
                if has_sink:
                    sk_a = jnp.max(jnp.where(lms[a], s_ref[:, sl], -jnp.inf), axis=-1, keepdims=True)
                    m = jnp.maximum(m, sk_a)
                p0, p1 = jnp.exp(s0 - m), jnp.exp(s1 - m)
                l = jnp.sum(p0, axis=-1, keepdims=True) + jnp.sum(p1, axis=-1, keepdims=True)
                if has_sink:
                    l = l + jnp.exp(sk_a - m)
                pv = _dot(p0.astype(BF16), vp, _NN) + _dot(p1.astype(BF16), vc, _NN)
                outs.append(pv / l)
                lses.append(m + jnp.log(l))
            o_ref[:, sl] = jnp.where(lms[0], outs[0], outs[1])
            l_ref[:, sl] = jnp.where(lms[0], lses[0], lses[1])

    cur = pl.BlockSpec((rb, wb), lambda ub, i: (i, ub))
    prev = pl.BlockSpec((rb, wb), lambda ub, i: (jnp.maximum(i - 1, 0), ub))
    in_specs = [cur, prev, cur, prev, cur] + ([pl.BlockSpec((1, wb), lambda ub, i: (0, ub))] if has_sink else [])
    args = (q, k, k, v, v) + ((sinkrow,) if has_sink else ())
    return pl.pallas_call(
        body, name=name, grid=(nub, nq), in_specs=in_specs, out_specs=(cur, cur),
        out_shape=(jax.ShapeDtypeStruct((sq, w), F32), jax.ShapeDtypeStruct((sq, w), F32)),
        compiler_params=_params("parallel", "parallel"),
    )(*args)


def _band_dq(q, k, v, o, lse, do, sinkrow, scale, max_dist, upb, dil, name):
    sq, w = q.shape
    rb = BLOCK * dil
    nq, nub, wb = sq // rb, w // (LANES * upb), LANES * upb
    has_sink = sinkrow is not None

    def body(*refs):
        if has_sink:
            s_ref, dq_block, dsink_ref = refs[8], refs[9], refs[10]
        else:
            dq_block = refs[8]
        i = pl.program_id(1)
        lms = _lane_masks()
        mprev, mcur = _band_masks(max_dist)
        mprev = jnp.logical_and(mprev, i > 0)
        if has_sink:
            @pl.when(i == 0)
            def _():
                dsink_ref[...] = jnp.zeros_like(dsink_ref)

        for q_ref, kp_ref, kc_ref, vp_ref, vc_ref, o_ref, l_ref, do_ref, dq_ref in _residues(refs[:8] + (dq_block,), dil):
          for u in range(upb):
            sl = slice(u * LANES, (u + 1) * LANES)
            qv = (q_ref[:, sl] * scale).astype(BF16)
            kp, kc = kp_ref[:, sl].astype(BF16), kc_ref[:, sl].astype(BF16)
            vp, vc = vp_ref[:, sl].astype(BF16), vc_ref[:, sl].astype(BF16)
            dov = do_ref[:, sl]
            prod = dov * o_ref[:, sl]
            dob = dov.astype(BF16)
            lv = l_ref[:, sl]
            dqs, ts = [], []
            for a in range(2):
                t, lse_a = _head_terms(lms, a, prod, lv)
                qa, doa = _sel(lms[a], qv), _sel(lms[a], dob)
                p0 = jnp.exp(jnp.where(mprev, _dot(qa, kp, _NT), NEG_MASK) - lse_a)
                p1 = jnp.exp(jnp.where(mcur, _dot(qa, kc, _NT), NEG_MASK) - lse_a)
                ds0 = (p0 * (_dot(doa, vp, _NT) - t)).astype(BF16)
                ds1 = (p1 * (_dot(doa, vc, _NT) - t)).astype(BF16)
                dqs.append((_dot(ds0, kp, _NN) + _dot(ds1, kc, _NN)) * scale)
                ts.append(t)
            dq_ref[:, sl] = jnp.where(lms[0], dqs[0], dqs[1])
            if has_sink:
                rs = -jnp.exp(s_ref[:, sl] - lv) * jnp.where(lms[0], ts[0], ts[1])
                dsink_ref[0:1, sl] += jnp.sum(rs, axis=0, keepdims=True)

    cur = pl.BlockSpec((rb, wb), lambda ub, i: (i, ub))
    prev = pl.BlockSpec((rb, wb), lambda ub, i: (jnp.maximum(i - 1, 0), ub))
    in_specs = [cur, prev, cur, prev, cur, cur, cur, cur]
    args = (q, k, k, v, v, o, lse, do)
    out_specs, out_shape = cur, jax.ShapeDtypeStruct((sq, w), F32)
    sem = ("parallel", "parallel")
    if has_sink:
        in_specs = in_specs + [pl.BlockSpec((1, wb), lambda ub, i: (0, ub))]
        args = args + (sinkrow,)
        out_specs = (cur, pl.BlockSpec((8, wb), lambda ub, i: (0, ub)))
        out_shape = (out_shape, jax.ShapeDtypeStruct((8, w), F32))
        sem = ("parallel", "arbitrary")
    return pl.pallas_call(
        body, name=name, grid=(nub, nq), in_specs=in_specs, out_specs=out_specs, out_shape=out_shape,
        compiler_params=_params(*sem),
    )(*args)


def _band_dkv(q, k, v, o, lse, do, scale, max_dist, upb, dil, name):
    sq, w = q.shape
    rb = BLOCK * dil
    nq, nub, wb = sq // rb, w // (LANES * upb), LANES * upb

    def body(*refs):
        kb = pl.program_id(1)
        lms = _lane_masks()
        mnext, msame = _band_masks(max_dist)
        mnext = jnp.logical_and(mnext, kb < nq - 1)
        for (k_ref, v_ref, qs_ref, qn_ref, os_ref, on_ref, ls_ref, ln_ref, dos_ref, don_ref, dk_ref,
             dv_ref) in _residues(refs, dil):
          for u in range(upb):
            sl = slice(u * LANES, (u + 1) * LANES)
            kv, vv = k_ref[:, sl].astype(BF16), v_ref[:, sl].astype(BF16)
            dk_tot, dv_tot = None, None
            for q_ref, o_ref, l_ref, do_ref, mask in ((qs_ref, os_ref, ls_ref, dos_ref, msame),
                                                      (qn_ref, on_ref, ln_ref, don_ref, mnext)):
                qv = (q_ref[:, sl] * scale).astype(BF16)
                dov = do_ref[:, sl]
                prod = dov * o_ref[:, sl]
                dob = dov.astype(BF16)
                lv = l_ref[:, sl]
                for a in range(2):
                    t, lse_a = _head_terms(lms, a, prod, lv)
                    qa, doa = _sel(lms[a], qv), _sel(lms[a], dob)
                    p = jnp.exp(jnp.where(mask, _dot(qa, kv, _NT), NEG_MASK) - lse_a)
                    ds = (p * (_dot(doa, vv, _NT) - t)).astype(BF16)
                    dvc = _dot(p.astype(BF16), doa, _TN)
                    dkc = _dot(ds, qa, _TN)
                    dv_tot = dvc if dv_tot is None else dv_tot + dvc
                    dk_tot = dkc if dk_tot is None else dk_tot + dkc
            dk_ref[:, sl] = dk_tot
            dv_ref[:, sl] = dv_tot

    same = pl.BlockSpec((rb, wb), lambda ub, kb: (kb, ub))
    nxt = pl.BlockSpec((rb, wb), lambda ub, kb: (jnp.minimum(kb + 1, nq - 1), ub))
    return pl.pallas_call(
        body, name=name, grid=(nub, nq), in_specs=[same, same, same, nxt, same, nxt, same, nxt, same, nxt],
        out_specs=(same, same),
        out_shape=(jax.ShapeDtypeStruct((sq, w), F32), jax.ShapeDtypeStruct((sq, w), F32)),
        compiler_params=_params("parallel", "parallel"),
    )(k, v, q, q, o, o, lse, lse, do, do)


def _make_band_attention(scale, max_dist, upb, name):
    @jax.custom_vjp
    def attn(q, k, v, sinks):
        return _band_fwd(q, k, v, _sink_row(sinks), scale, max_dist, upb, 1, name + "_fwd")[0]

    def fwd(q, k, v, sinks):
        o, lse = _band_fwd(q, k, v, _sink_row(sinks), scale, max_dist, upb, 1, name + "_fwd")
        return o, (q, k, v, o, lse, sinks)

    def bwd(res, do):
        q, k, v, o, lse, sinks = res
        dq, dsink = _band_dq(q, k, v, o, lse, do, _sink_row(sinks), scale, max_dist, upb, 1, name + "_dq")
        dk, dv = _band_dkv(q, k, v, o, lse, do, scale, max_dist, upb, 1, name + "_dkv")
        return dq, dk, dv, dsink[0].reshape(-1, HEAD_DIM)[:, 0]

    attn.defvjp(fwd, bwd)
    return attn


def _triangle(n, by_key):
    if by_key:
        pairs = [(i, kb) for kb in range(n) for i in range(kb, n)]
    else:
        pairs = [(i, j) for i in range(n) for j in range(i + 1)]
    qi = np.asarray([p[0] for p in pairs], np.int32)
    kj = np.asarray([p[1] for p in pairs], np.int32)
    return jnp.asarray(qi), jnp.asarray(kj)


def _causal_fwd(q, k, v, scale, blk, name):
    s, w = q.shape
    nq, nub = s // blk, w // LANES
    qi, kj = _triangle(nq, by_key=False)

    def body(qi_ref, kj_ref, q_ref, k_ref, v_ref, o_ref, l_ref, m_sc, l_sc, acc_sc):
        t = pl.program_id(1)
        i, j = qi_ref[t], kj_ref[t]

        @pl.when(j == 0)
        def _():
            m_sc[...] = jnp.full_like(m_sc, NEG_INIT)
            l_sc[...] = jnp.zeros_like(l_sc)
            acc_sc[...] = jnp.zeros_like(acc_sc)

        def step(diagonal):
            kv, vv = k_ref[...].astype(BF16), v_ref[...].astype(BF16)
            chains = range(0, blk, CAUSAL_ROW_CHAIN)
            scs = [_dot((q_ref[c0:c0 + CAUSAL_ROW_CHAIN, :] * scale).astype(BF16), kv, _NT) for c0 in chains]
            m_all, l_all, acc_all = m_sc[...], l_sc[...], acc_sc[...]
            m_out, l_out, acc_out = [], [], []
            for sc, c0 in zip(scs, chains):
                rows = slice(c0, c0 + CAUSAL_ROW_CHAIN)
                if diagonal:
                    r = c0 + lax.broadcasted_iota(jnp.int32, (CAUSAL_ROW_CHAIN, blk), 0)
                    c = lax.broadcasted_iota(jnp.int32, (CAUSAL_ROW_CHAIN, blk), 1)
                    sc = jnp.where(r >= c, sc, NEG_MASK)
                m_prev = m_all[rows]
                m_new = jnp.maximum(m_prev, jnp.max(sc, axis=-1, keepdims=True))
                alpha = jnp.exp(m_prev - m_new)
                p = jnp.exp(sc - m_new)
                l_out.append(alpha * l_all[rows] + jnp.sum(p, axis=-1, keepdims=True))
                m_out.append(m_new)
                acc_out.append(acc_all[rows] * alpha + _dot(p.astype(BF16), vv, _NN))
            m_sc[...] = jnp.concatenate(m_out, axis=0)
            l_sc[...] = jnp.concatenate(l_out, axis=0)
            acc_sc[...] = jnp.concatenate(acc_out, axis=0)

        @pl.when(j < i)
        def _():
            step(False)

        @pl.when(j == i)
        def _():
            step(True)
            lf = l_sc[...]
            o_ref[...] = acc_sc[...] / lf
            l_ref[...] = jnp.broadcast_to(m_sc[...] + jnp.log(lf), (blk, LANES))

    qspec = pl.BlockSpec((blk, LANES), lambda ub, t, qi_ref, kj_ref: (qi_ref[t], ub))
    kspec = pl.BlockSpec((blk, LANES), lambda ub, t, qi_ref, kj_ref: (kj_ref[t], ub))
    return pl.pallas_call(
        body, name=name,
        grid_spec=pltpu.PrefetchScalarGridSpec(
            num_scalar_prefetch=2, grid=(nub, qi.shape[0]), in_specs=[qspec, kspec, kspec], out_specs=(qspec, qspec),
            scratch_shapes=[pltpu.VMEM((blk, 1), F32), pltpu.VMEM((blk, 1), F32), pltpu.VMEM((blk, LANES), F32)]),
        out_shape=(jax.ShapeDtypeStruct((s, w), F32), jax.ShapeDtypeStruct((s, w), F32)),
        compiler_params=_params("parallel", "arbitrary"),
    )(qi, kj, q, k, v)


def _causal_bwd(q, k, v, o, lse, do, scale, blk, name):
    s, w = q.shape
    nq, nub = s // blk, w // LANES
    qi, kj = _triangle(nq, by_key=True)

    def body(qi_ref, kj_ref, q_ref, k_ref, v_ref, o_ref, l_ref, do_ref, dq_ref, dk_ref, dv_ref, dk_acc, dv_acc):
        t = pl.program_id(1)
        i, kb = qi_ref[t], kj_ref[t]

        @pl.when(t == 0)
        def _():
            dq_ref[...] = jnp.zeros_like(dq_ref)

        @pl.when(i == kb)
        def _():
            dk_acc[...] = jnp.zeros_like(dk_acc)
            dv_acc[...] = jnp.zeros_like(dv_acc)

        def step(diagonal):
            qv = (q_ref[...] * scale).astype(BF16)
            kv, vv = k_ref[...].astype(BF16), v_ref[...].astype(BF16)
            dov = do_ref[...]
            tsum = jnp.sum(dov * o_ref[...], axis=-1, keepdims=True)
            dob = dov.astype(BF16)
            sc = _dot(qv, kv, _NT)
            if diagonal:
                r = lax.broadcasted_iota(jnp.int32, (blk, blk), 0)
                c = lax.broadcasted_iota(jnp.int32, (blk, blk), 1)
                sc = jnp.where(r >= c, sc, NEG_MASK)
            p = jnp.exp(sc - l_ref[:, 0:1])
            ds = (p * (_dot(dob, vv, _NT) - tsum)).astype(BF16)
            dv_acc[...] += _dot(p.astype(BF16), dob, _TN)
            dk_acc[...] += _dot(ds, qv, _TN)
            rows = pl.ds(pl.multiple_of(i * blk, blk), blk)
            dq_ref[rows, :] += _dot(ds, kv, _NN) * scale

        @pl.when(i == kb)
        def _():
            step(True)

        @pl.when(i > kb)
        def _():
            step(False)

        @pl.when(i == nq - 1)
        def _():
            dk_ref[...] = dk_acc[...]
            dv_ref[...] = dv_acc[...]

    qspec = pl.BlockSpec((blk, LANES), lambda ub, t, qi_ref, kj_ref: (qi_ref[t], ub))
    kspec = pl.BlockSpec((blk, LANES), lambda ub, t, qi_ref, kj_ref: (kj_ref[t], ub))
    whole = pl.BlockSpec((s, LANES), lambda ub, t, qi_ref, kj_ref: (0, ub))
    out = jax.ShapeDtypeStruct((s, w), F32)
    return pl.pallas_call(
        body, name=name,
        grid_spec=pltpu.PrefetchScalarGridSpec(
            num_scalar_prefetch=2, grid=(nub, qi.shape[0]), in_specs=[qspec, kspec, kspec, qspec, qspec, qspec],
            out_specs=(whole, kspec, kspec),
            scratch_shapes=[pltpu.VMEM((blk, LANES), F32), pltpu.VMEM((blk, LANES), F32)]),
        out_shape=(out, out, out), compiler_params=_params("parallel", "arbitrary"),
    )(qi, kj, q, k, v, o, lse, do)


def _make_causal_attention(scale, blk, name):
    @jax.custom_vjp
    def attn(q, k, v):
        return _causal_fwd(q, k, v, scale, blk, name + "_fwd")[0]

    def fwd(q, k, v):
        o, lse = _causal_fwd(q, k, v, scale, blk, name + "_fwd")
        return o, (q, k, v, o, lse)

    def bwd(res, do):
        q, k, v, o, lse = res
        return _causal_bwd(q, k, v, o, lse, do, scale, blk, name + "_bwd")

    attn.defvjp(fwd, bwd)
    return attn


def _make_attention(cfg, name, with_sink=False):
    if with_sink:
        @jax.custom_vjp
        def attn(q, k, v, sinks):
            return _attn_fwd(q, k, v, _sink_row(sinks), cfg, name + "_fwd")[0]

        def fwd(q, k, v, sinks):
            o, lse = _attn_fwd(q, k, v, _sink_row(sinks), cfg, name + "_fwd")
            return o, (q, k, v, o, lse, sinks)

        def bwd(res, do):
            q, k, v, o, lse, sinks = res
            dq, dsink = _attn_dq(q, k, v, o, lse, do, _sink_row(sinks), cfg, name + "_dq")
            dk, dv = _attn_dkv(q, k, v, o, lse, do, cfg, name + "_dkv")
            return dq, dk, dv, dsink[0].reshape(-1, HEAD_DIM)[:, 0]
    else:
        @jax.custom_vjp
        def attn(q, k, v):
            return _attn_fwd(q, k, v, None, cfg, name + "_fwd")[0]

        def fwd(q, k, v):
            o, lse = _attn_fwd(q, k, v, None, cfg, name + "_fwd")
            return o, (q, k, v, o, lse)

        def bwd(res, do):
            q, k, v, o, lse = res
            dq = _attn_dq(q, k, v, o, lse, do, None, cfg, name + "_dq")
            dk, dv = _attn_dkv(q, k, v, o, lse, do, cfg, name + "_dkv")
            return dq, dk, dv

    attn.defvjp(fwd, bwd)
    return attn


def _sink_row(sinks):
    return jnp.repeat(sinks.astype(F32), HEAD_DIM).reshape(1, -1)


def _merge3(os_, ls_, name):
    s, w = os_[0].shape
    bs = _pick(s, (256, 128))

    def body(o1, o2, o3, l1, l2, l3, out_ref, lse_ref):
        a1, a2, a3 = l1[...], l2[...], l3[...]
        m = jnp.maximum(jnp.maximum(a1, a2), a3)
        e1, e2, e3 = jnp.exp(a1 - m), jnp.exp(a2 - m), jnp.exp(a3 - m)
        z = e1 + e2 + e3
        out_ref[...] = (e1 * o1[...] + e2 * o2[...] + e3 * o3[...]) / z
        lse_ref[...] = m + jnp.log(z)

    row = pl.BlockSpec((bs, w), lambda i: (i, 0))
    return pl.pallas_call(
        body, name=name, grid=(s // bs,), in_specs=[row] * 6, out_specs=(row, row),
        out_shape=(jax.ShapeDtypeStruct((s, w), F32), jax.ShapeDtypeStruct((s, w), F32)),
        compiler_params=_params("parallel"),
    )(*os_, *ls_)


def _add3(a, b, c, name):
    s, w = a.shape
    bs = _pick(s, (512, 256, 128))

    def body(a_ref, b_ref, c_ref, o_ref):
        o_ref[...] = (a_ref[...] + b_ref[...]) + c_ref[...]

    row = pl.BlockSpec((bs, w), lambda i: (i, 0))
    return pl.pallas_call(
        body, name=name, grid=(s // bs,), in_specs=[row] * 3, out_specs=row,
        out_shape=jax.ShapeDtypeStruct((s, w), F32), compiler_params=_params("parallel"),
    )(a, b, c)


def _make_dilated(name):
    scale, max_dist = HEAD_DIM ** -0.5, BLOCK

    def upb_of(dil):
        return BAND_UNITS_PER_STEP if dil == 1 else 1

    def forward(q, k, v):
        os_, ls_ = [], []
        for n, (_, dil) in enumerate(DIL_PATTERNS):
            o, l = _band_fwd(q, k, v, None, scale, max_dist, upb_of(dil), dil, "%s_b%d_fwd" % (name, n))
            os_.append(o)
            ls_.append(l)
        return _merge3(os_, ls_, name + "_merge")

    @jax.custom_vjp
    def dilated(q, k, v):
        return forward(q, k, v)[0]

    def fwd(q, k, v):
        out, lse = forward(q, k, v)
        return out, (q, k, v, out, lse)

    def bwd(res, do):
        q, k, v, out, lse = res
        dqs, dks, dvs = [], [], []
        for n, (_, dil) in enumerate(DIL_PATTERNS):
            args = (q, k, v, out, lse, do)
            dqs.append(_band_dq(*args, None, scale, max_dist, upb_of(dil), dil, "%s_b%d_dq" % (name, n)))
            dk, dv = _band_dkv(*args, scale, max_dist, upb_of(dil), dil, "%s_b%d_dkv" % (name, n))
            dks.append(dk)
            dvs.append(dv)
        return (_add3(*dqs, name + "_dq_sum"), _add3(*dks, name + "_dk_sum"), _add3(*dvs, name + "_dv_sum"))

    dilated.defvjp(fwd, bwd)
    return dilated


def _make_norm_linear(name):
    @jax.custom_vjp
    def op(x, g, wslot, w):
        return _mm(_rms_fwd(x, g, name + "_norm"), w, "nn", name + "_mm")

    def fwd(x, g, wslot, w):
        h = _rms_fwd(x, g, name + "_norm")
        return _mm(h, w, "nn", name + "_mm"), (x, g, h, w)

    def bwd(res, dz):
        x, g, h, w = res
        dh = _mm(dz, w, "nt", name + "_dh")
        dw = _mm(h, dz, "tn", name + "_dw", out_dtype=GRAD_WIRE_DTYPE)
        dx, dg = _rms_bwd(x, g, dh, name + "_norm_bwd")
        return dx, dg, dw, None

    op.defvjp(fwd, bwd)
    return op


def _make_linear_res(name):
    @jax.custom_vjp
    def op(a, wslot, w, res):
        return _mm(a, w, "nn", name + "_mm", res=res)

    def fwd(a, wslot, w, res):
        return _mm(a, w, "nn", name + "_mm", res=res), (a, w)

    def bwd(saved, dout):
        a, w = saved
        da = _mm(dout, w, "nt", name + "_da")
        dw = _mm(a, dout, "tn", name + "_dw", out_dtype=GRAD_WIRE_DTYPE)
        return da, dw, None, dout

    op.defvjp(fwd, bwd)
    return op


def _swiglu_fwd(gu, name):
    s, w2 = gu.shape
    hdim = w2 // 2
    bs = _pick(s, (256, 128))

    def body(g_ref, u_ref, a_ref):
        g = g_ref[...]
        a_ref[...] = (g / (1.0 + jnp.exp(-g)) * u_ref[...]).astype(BF16)

    return pl.pallas_call(
        body, name=name, grid=(s // bs,),
        in_specs=[pl.BlockSpec((bs, hdim), lambda i: (i, 0)), pl.BlockSpec((bs, hdim), lambda i: (i, 1))],
        out_specs=pl.BlockSpec((bs, hdim), lambda i: (i, 0)), out_shape=jax.ShapeDtypeStruct((s, hdim), BF16),
        compiler_params=_params("parallel"),
    )(gu, gu)


def _swiglu_bwd_joint(gu, da, name):
    s, w2 = gu.shape
    hdim = w2 // 2
    bs = _pick(s, (256, 128))

    def body(g_ref, u_ref, da_ref, dgu_ref):
        g, u, d = g_ref[...], u_ref[...], da_ref[...]
        sig = 1.0 / (1.0 + jnp.exp(-g))
        dgu_ref[:, :hdim] = (d * u * (sig * (1.0 + g * (1.0 - sig)))).astype(BF16)
        dgu_ref[:, hdim:] = (d * (g * sig)).astype(BF16)

    lo = pl.BlockSpec((bs, hdim), lambda i: (i, 0))
    hi = pl.BlockSpec((bs, hdim), lambda i: (i, 1))
    return pl.pallas_call(
        body, name=name, grid=(s // bs,), in_specs=[lo, hi, lo], out_specs=pl.BlockSpec((bs, w2), lambda i: (i, 0)),
        out_shape=jax.ShapeDtypeStruct((s, w2), BF16), compiler_params=_params("parallel"),
    )(gu, gu, da)


def _make_ffn(name):
    def forward(x, g, wgu, wd):
        h = _rms_fwd(x, g, name + "_norm")
        gu = _mm(h, wgu, "nn", name + "_gu")
        a = _swiglu_fwd(gu, name + "_act")
        return _mm(a, wd, "nn", name + "_down", res=x), (x, g, h, gu, a, wgu, wd)

    @jax.custom_vjp
    def op(x, g, wgu_slot, wd_slot, wgu, wd):
        return forward(x, g, wgu, wd)[0]

    def fwd(x, g, wgu_slot, wd_slot, wgu, wd):
        return forward(x, g, wgu, wd)

    def bwd(saved, dout):
        x, g, h, gu, a, wgu, wd = saved
        da = _mm(dout, wd, "nt", name + "_da")
        dwd = _mm(a, dout, "tn", name + "_dwd", out_dtype=GRAD_WIRE_DTYPE)
        dgu = _swiglu_bwd_joint(gu, da, name + "_act_bwd")
        dwgu = _mm(h, dgu, "tn", name + "_dwgu", out_dtype=GRAD_WIRE_DTYPE)
        dh = _mm(dgu, wgu, "nt", name + "_dh")
        dx, dg = _rms_bwd(x, g, dh, name + "_norm_bwd", dres=dout)
        return dx, dg, dwgu, dwd, None, None

    op.defvjp(fwd, bwd)
    return op


def _make_final_loss(name):
    def run(x, g, tgt):
        s, d = x.shape
        bs = _pick(s, (512, 256, 128))

        def body(x_ref, g_ref, t_ref, loss_ref, dx_ref, dg_ref):
            i = pl.program_id(0)
            xv = x_ref[...]
            gv = g_ref[...]
            r = lax.rsqrt(jnp.mean(xv * xv, axis=-1, keepdims=True) + NORM_EPS)
            xh = xv * r
            e = xh * gv - t_ref[...]
            dy = e * (1.0 / d)
            dxh = dy * gv
            dx_ref[...] = r * (dxh - xh * jnp.mean(dxh * xh, axis=-1, keepdims=True))
            part = 0.5 * jnp.sum(jnp.sum(e * e, axis=-1, keepdims=True) * (1.0 / d), axis=0, keepdims=True)

            @pl.when(i == 0)
            def _():
                loss_ref[...] = jnp.zeros_like(loss_ref)
                dg_ref[...] = jnp.zeros_like(dg_ref)

            loss_ref[...] += jnp.broadcast_to(part, loss_ref.shape)
            dg_ref[...] += jnp.sum(dy * xh, axis=0, keepdims=True)

        row = pl.BlockSpec((bs, d), lambda i: (i, 0))
        vec = pl.BlockSpec((1, d), lambda i: (0, 0))
        loss, dx, dg = pl.pallas_call(
            body, name=name, grid=(s // bs,), in_specs=[row, vec, row],
            out_specs=(pl.BlockSpec((8, LANES), lambda i: (0, 0)), row, vec),
            out_shape=(jax.ShapeDtypeStruct((8, LANES), F32), jax.ShapeDtypeStruct((s, d), F32),
                       jax.ShapeDtypeStruct((1, d), F32)),
            compiler_params=_params("arbitrary"),
        )(x, g.reshape(1, d), tgt)
        return loss[0, 0], dx, dg.reshape(d)

    @jax.custom_vjp
    def op(x, g, tgt):
        return run(x, g, tgt)[0]

    def fwd(x, g, tgt):
        loss, dx, dg = run(x, g, tgt)
        return loss, (dx, dg)

    def bwd(saved, ct):
        dx, dg = saved
        return dx * ct, dg * ct, None

    op.defvjp(fwd, bwd)
    return op


def _model_loss(diff, consts):
    x = diff["x"]
    w = consts["w"]
    slot = diff["slots"]
    vec = diff["vec"]
    tab64, tab_mla = consts["tab64"], consts["tab_mla"]
    mem = consts["mem"]
    s = x.shape[0]

    rope64 = lambda t, nm: _make_rope(HEAD_DIM // 2, nm)(t, *tab64)
    rope_mla = lambda t, nm: _make_rope(MLA_ROPE_DIM // 2, nm)(t, *tab_mla)

    def nl(nm, inp, gain, wname):
        return _make_norm_linear(nm)(inp, gain, slot[wname], w[wname])

    def cross(layer, xin):
        p = "l%d_" % layer
        q = nl(p + "xq", xin, vec[p + "x_norm"], p + "w_xq")
        kv = nl(p + "xkv", mem, vec[p + "mem_norm"], p + "w_xkv")
        half = X_HEADS * X_HEAD_DIM
        cfg = AttnCfg("full", X_HEAD_DIM ** -0.5, 1, _pick(s, (512, 256, 128)), kv.shape[0], 4)
        o = _make_attention(cfg, p + "xattn")(q, kv[:, :half], kv[:, half:])
        return _make_linear_res(p + "xo")(o, slot[p + "w_xo"], w[p + "w_xo"], xin)

    def ffn(layer, xin):
        p = "l%d_" % layer
        return _make_ffn(p + "ffn")(xin, vec[p + "ffn_norm"], slot[p + "w_gu"], slot[p + "w_down"], w[p + "w_gu"],
                                    w[p + "w_down"])

    z = nl("l0_in", x, vec["l0_mix_norm"], "l0_w_in")
    qa = rope64(z[:, :A_Q], "l0_rope_qa")
    ka = rope64(z[:, A_Q:A_Q + A_KV], "l0_rope_ka")
    va = z[:, A_Q + A_KV:A_Q + 2 * A_KV]
    rep = SWA_HEADS // SWA_KV_HEADS
    expand = lambda t: jnp.broadcast_to(t.reshape(s, SWA_KV_HEADS, 1, HEAD_DIM),
                                        (s, SWA_KV_HEADS, rep, HEAD_DIM)).reshape(s, A_Q)
    swa = _make_band_attention(HEAD_DIM ** -0.5, SWA_WINDOW - 1, BAND_UNITS_PER_STEP, "l0_swa")
    oa = swa(qa, expand(ka), expand(va), vec["l0_sinks"])

    c0 = A_Q + 2 * A_KV
    cq = z[:, c0:c0 + MLA_Q_RANK]
    ckv = z[:, c0 + MLA_Q_RANK:c0 + MLA_Q_RANK + MLA_KV_RANK]
    kr = z[:, c0 + MLA_Q_RANK + MLA_KV_RANK:EVEN_IN]
    qb = nl("l0_uq", cq, vec["l0_q_norm"], "l0_w_uq").reshape(s, MLA_HEADS, MLA_NOPE_DIM + MLA_ROPE_DIM)
    qfull = jnp.pad(qb, ((0, 0), (0, 0), (0, LANES - MLA_NOPE_DIM - MLA_ROPE_DIM))).reshape(s, MLA_HEADS * LANES)
    qfull = rope_mla(qfull, "l0_rope_q")
    kvb = nl("l0_ukv", ckv, vec["l0_kv_norm"], "l0_w_ukv")
    kvb3 = kvb.reshape(s, MLA_HEADS, LANES)
    kfull = jnp.concatenate(
        [kvb3[:, :, :MLA_NOPE_DIM], jnp.broadcast_to(kr[:, None, :], (s, MLA_HEADS, MLA_ROPE_DIM)),
         jnp.zeros((s, MLA_HEADS, LANES - MLA_NOPE_DIM - MLA_ROPE_DIM), F32)], axis=-1).reshape(s, MLA_HEADS * LANES)
    kfull = rope_mla(kfull, "l0_rope_k")
    mla = _make_causal_attention((MLA_NOPE_DIM + MLA_ROPE_DIM) ** -0.5, _pick(s, (512, 256, 128)), "l0_mla")
    ob = mla(qfull, kfull, kvb).reshape(s, MLA_HEADS, LANES)[:, :, MLA_NOPE_DIM:]
    o = jnp.concatenate([oa, ob.reshape(s, MLA_HEADS * HEAD_DIM)], axis=-1)
    x = _make_linear_res("l0_out")(o, slot["l0_w_out"], w["l0_w_out"], x)
    x = cross(0, x)
    x = ffn(0, x)

    qkv = nl("l1_qkv", x, vec["l1_mix_norm"], "l1_w_qkv")
    q = rope64(qkv[:, :D_MODEL], "l1_rope_q")
    k = rope64(qkv[:, D_MODEL:2 * D_MODEL], "l1_rope_k")
    o = _make_dilated("l1_dil")(q, k, qkv[:, 2 * D_MODEL:])
    x = _make_linear_res("l1_out")(o, slot["l1_w_out"], w["l1_w_out"], x)
    x = cross(1, x)
    x = ffn(1, x)

    return _make_final_loss("final_loss")(x, vec["final_norm"], consts["target"])


MESH_IDS = pl.DeviceIdType.MESH
HBM_SPEC = pl.BlockSpec(memory_space=pltpu.HBM)


def _my_place():
    return lax.axis_index("x"), lax.axis_index("y"), lax.axis_index("c")


def _flip(v, bit):
    return 1 - v if bit else v


def _all_gather_rows(shard):
    r, c_ = shard.shape

    def body(x_ref, out_ref, send_sems, recv_sems, local_sem):
        x, y, c = _my_place()
        me, sibling = (x, y, c), (x, y, 1 - c)
        chips = [(1 - x, y), (x, 1 - y), (1 - x, 1 - y)]

        def slot(px, py, pc):
            return out_ref.at[4 * px + 2 * py + pc]

        def copy(k, block, to, src=None):
            return pltpu.make_async_remote_copy(
                src_ref=slot(*block) if src is None else src, dst_ref=slot(*block), send_sem=send_sems.at[k],
                recv_sem=recv_sems.at[k], device_id=to, device_id_type=MESH_IDS)

        mine = pltpu.make_async_copy(x_ref, slot(*me), local_sem)
        mine.start()
        first = [copy(0, me, sibling, src=x_ref)]
        first += [copy(1 + j, me, (*chip, c), src=x_ref) for j, chip in enumerate(chips)]
        for cp in first:
            cp.start()
        passed = [copy(4 + j, (*chip, c), sibling) for j, chip in enumerate(chips)]
        for j, chip in enumerate(chips):
            copy(1 + j, (*chip, c), me).wait_recv()
            passed[j].start()
        copy(0, sibling, me).wait_recv()
        for j, chip in enumerate(chips):
            copy(4 + j, (*chip, 1 - c), me).wait_recv()
        for cp in first + passed:
            cp.wait_send()
        mine.wait()

    return pl.pallas_call(
        body, name="weights_all_gather", out_shape=jax.ShapeDtypeStruct((N_DEV, r, c_), shard.dtype),
        in_specs=[HBM_SPEC], out_specs=HBM_SPEC,
        scratch_shapes=[pltpu.SemaphoreType.DMA((7,)), pltpu.SemaphoreType.DMA((7,)), pltpu.SemaphoreType.DMA],
    )(shard)


N_CHIPS = 4


def _exchange_with_sibling(slabs):
    _, nq, r, c_ = slabs.shape

    def body(p_ref, out_ref, send_sem, recv_sem):
        x, y, c = _my_place()
        cp = pltpu.make_async_remote_copy(
            src_ref=p_ref.at[1 - c], dst_ref=out_ref, send_sem=send_sem, recv_sem=recv_sem,
            device_id=(x, y, 1 - c), device_id_type=MESH_IDS)
        cp.start()
        cp.wait_recv()
        cp.wait_send()

    return pl.pallas_call(
        body, name="grad_exchange_sibling", out_shape=jax.ShapeDtypeStruct((nq, r, c_), slabs.dtype),
        in_specs=[HBM_SPEC], out_specs=HBM_SPEC,
        scratch_shapes=[pltpu.SemaphoreType.DMA, pltpu.SemaphoreType.DMA],
    )(slabs)


def _add_pairs(a, b):
    nq, r, c_ = a.shape
    br = _pick(r, (256, 128, 64, 32, 16, 8))

    def body(a_ref, b_ref, o_ref):
        o_ref[...] = (a_ref[...].astype(F32) + b_ref[...].astype(F32)).astype(o_ref.dtype)

    blk = pl.BlockSpec((1, br, c_), lambda q, i: (q, i, 0))
    return pl.pallas_call(
        body, name="grad_chip_sum", grid=(nq, r // br), in_specs=[blk, blk], out_specs=blk,
        out_shape=jax.ShapeDtypeStruct(a.shape, a.dtype), compiler_params=_params("parallel", "parallel"),
    )(a, b)


def _exchange_between_chips(slabs):
    nq, r, c_ = slabs.shape

    def body(t_ref, out_ref, send_sems, recv_sems, local_sem):
        x, y, c = _my_place()
        myq = 2 * x + y
        local = pltpu.make_async_copy(t_ref.at[myq], out_ref.at[myq], local_sem)
        local.start()
        sends, recvs = [], []
        for k in range(1, N_CHIPS):
            px, py = _flip(x, k & 2), _flip(y, k & 1)
            peer = 2 * px + py
            sends.append(pltpu.make_async_remote_copy(
                src_ref=t_ref.at[peer], dst_ref=out_ref.at[myq], send_sem=send_sems.at[k - 1],
                recv_sem=recv_sems.at[k - 1], device_id=(px, py, c), device_id_type=MESH_IDS))
            recvs.append(pltpu.make_async_remote_copy(
                src_ref=t_ref.at[myq], dst_ref=out_ref.at[peer], send_sem=send_sems.at[k - 1],
                recv_sem=recv_sems.at[k - 1], device_id=(px, py, c), device_id_type=MESH_IDS))
        for cp in sends:
            cp.start()
        for cp in recvs:
            cp.wait_recv()
        for cp in sends:
            cp.wait_send()
        local.wait()

    return pl.pallas_call(
        body, name="grad_exchange_chips", out_shape=jax.ShapeDtypeStruct(slabs.shape, slabs.dtype),
        in_specs=[HBM_SPEC], out_specs=HBM_SPEC,
        scratch_shapes=[pltpu.SemaphoreType.DMA((N_CHIPS - 1,)), pltpu.SemaphoreType.DMA((N_CHIPS - 1,)),
                        pltpu.SemaphoreType.DMA],
    )(slabs)


def _all_reduce_small(v):
    r, c_ = v.shape

    def body(v_ref, out_ref, buf, send_sems, recv_sems):
        x, y, c = _my_place()
        me = 4 * x + 2 * y + c
        buf[me] = v_ref[...]
        sends, recvs = [], []
        for k in range(1, N_DEV):
            px, py, pc = _flip(x, k & 4), _flip(y, k & 2), _flip(c, k & 1)
            peer = 4 * px + 2 * py + pc
            sends.append(pltpu.make_async_remote_copy(
                src_ref=v_ref, dst_ref=buf.at[me], send_sem=send_sems.at[k - 1], recv_sem=recv_sems.at[k - 1],
                device_id=(px, py, pc), device_id_type=MESH_IDS))
            recvs.append(pltpu.make_async_remote_copy(
                src_ref=v_ref, dst_ref=buf.at[peer], send_sem=send_sems.at[k - 1], recv_sem=recv_sems.at[k - 1],
                device_id=(px, py, pc), device_id_type=MESH_IDS))
        for cp in sends:
            cp.start()
        for cp in recvs:
            cp.wait_recv()
        for cp in sends:
            cp.wait_send()
        acc = buf[0]
        for d in range(1, N_DEV):
            acc = acc + buf[d]
        out_ref[...] = acc

    vm = pl.BlockSpec(memory_space=pltpu.VMEM)
    return pl.pallas_call(
        body, name="vector_grad_all_reduce", out_shape=jax.ShapeDtypeStruct((r, c_), F32), in_specs=[vm], out_specs=vm,
        scratch_shapes=[pltpu.VMEM((N_DEV, r, c_), F32), pltpu.SemaphoreType.DMA((7,)), pltpu.SemaphoreType.DMA((7,))],
    )(v)


def _adamw_math(w, g, m, v):
    m = ADAM_B1 * m + (1.0 - ADAM_B1) * g
    v = ADAM_B2 * v + (1.0 - ADAM_B2) * (g * g)
    m_hat = m / (1.0 - ADAM_B1 ** ADAM_STEP)
    v_hat = v / (1.0 - ADAM_B2 ** ADAM_STEP)
    delta = -ADAM_LR * (m_hat / (jnp.sqrt(v_hat) + ADAM_EPS) + ADAM_WD * w)
    return delta, m, v


def _sum_and_adamw(parts, w, m, v):
    nparts, r, c_ = parts.shape
    br = _pick(r, (256, 128, 64, 32, 16, 8))

    def body(p_ref, w_ref, m_ref, v_ref, g_ref, d_ref, nm_ref, nv_ref):
        g = p_ref[0].astype(F32)
        for d in range(1, nparts):
            g = g + p_ref[d].astype(F32)
        g_ref[...] = g
        d_ref[...], nm_ref[...], nv_ref[...] = _adamw_math(w_ref[...], g, m_ref[...], v_ref[...])

    row = pl.BlockSpec((br, c_), lambda i: (i, 0))
    return pl.pallas_call(
        body, name="grad_sum_adamw", grid=(r // br,),
        in_specs=[pl.BlockSpec((nparts, br, c_), lambda i: (0, i, 0)), row, row, row], out_specs=(row,) * 4,
        out_shape=(jax.ShapeDtypeStruct((r, c_), F32),) * 4, compiler_params=_params("parallel"),
    )(parts, w, m, v)


def _adamw_small(w, g, m, v):
    vm = pl.BlockSpec(memory_space=pltpu.VMEM)

    def body(w_ref, g_ref, m_ref, v_ref, d_ref, nm_ref, nv_ref):
        d_ref[...], nm_ref[...], nv_ref[...] = _adamw_math(w_ref[...], g_ref[...], m_ref[...], v_ref[...])

    return pl.pallas_call(
        body, name="vector_adamw", in_specs=[vm] * 4, out_specs=(vm,) * 3,
        out_shape=(jax.ShapeDtypeStruct(w.shape, F32),) * 3,
    )(w, g, m, v)


def _pad_rows(t, axis):
    extra = -t.shape[axis] % PART_ROW_ALIGN
    if extra == 0:
        return t
    widths = [(0, 0)] * t.ndim
    widths[axis] = (0, extra)
    return jnp.pad(t, widths)


def _pack_local(named):
    rows = [_pad_rows(named[n].reshape(-1, PACK_COLS), 0) for n, _, _, _ in MATRICES]
    rows.append(jnp.zeros((MAT_ROWS - MAT_ROWS_USED, PACK_COLS), rows[0].dtype))
    return jnp.concatenate(rows, axis=0)


def _unpack_local(packed):
    out, r0 = {}, 0
    for n, kind, k, nn in MATRICES:
        nr = k * nn // N_DEV // PACK_COLS
        shape = (k, nn // N_DEV) if kind == "c" else (k // N_DEV, nn)
        out[n] = packed[r0:r0 + nr].reshape(shape)
        r0 += _part_rows(k, nn)
    return out


def _unpack_gathered(g):
    out, r0 = {}, 0
    for n, kind, k, nn in MATRICES:
        nr = k * nn // N_DEV // PACK_COLS
        blk = g[:, r0:r0 + nr]
        if kind == "c":
            out[n] = blk.reshape(N_DEV, k, nn // N_DEV).transpose(1, 0, 2).reshape(k, nn)
        else:
            out[n] = blk.reshape(k, nn)
        r0 += _part_rows(k, nn)
    return out


def _pack_full_grads(grads):
    rows = []
    for n, kind, k, nn in MATRICES:
        gmat = grads[n]
        if kind == "c":
            gmat = gmat.reshape(k, N_CHIPS, 2, nn // N_DEV).transpose(2, 1, 0, 3)
        else:
            gmat = gmat.reshape(N_CHIPS, 2, k // N_DEV, nn).transpose(1, 0, 2, 3)
        rows.append(_pad_rows(gmat.reshape(2, N_CHIPS, -1, PACK_COLS), 2))
    rows.append(jnp.zeros((2, N_CHIPS, MAT_ROWS - MAT_ROWS_USED, PACK_COLS), rows[0].dtype))
    return jnp.concatenate(rows, axis=2)


def _pack_vectors(named):
    rows = [jnp.pad(named[n].astype(F32), (0, PACK_COLS - d)) for n, d in VECTORS]
    rows += [jnp.zeros((PACK_COLS,), F32)] * (VEC_ROWS - len(VECTORS))
    return jnp.stack(rows, axis=0)


def _unpack_vectors(packed):
    return {n: packed[i, :d] for i, (n, d) in enumerate(VECTORS)}


def _step(inputs):
    x = inputs["x"][0]
    mem = inputs["mem"][0]
    positions = inputs["positions"][0]
    target = inputs["loss_target"][0]

    local_w = _pack_local({n: inputs[n] for n, _, _, _ in MATRICES})
    gathered = _all_gather_rows(local_w.astype(BF16))
    wfull = _unpack_gathered(gathered)
    vec = {n: inputs[n] for n, _ in VECTORS}

    loss_part, grad_x, gfull, gvec = _local_grads(wfull, vec, x, mem, positions, target)
    loss = lax.psum(loss_part, ("x", "y", "c"))

    slabs = _pack_full_grads(gfull)
    from_sibling = _exchange_with_sibling(slabs)
    mine = lax.dynamic_index_in_dim(slabs, lax.axis_index("c"), axis=0, keepdims=False)
    parts = _exchange_between_chips(_add_pairs(mine, from_sibling))
    local_m = _pack_local({n: inputs["m_" + n] for n, _, _, _ in MATRICES})
    local_v = _pack_local({n: inputs["v_" + n] for n, _, _, _ in MATRICES})
    g_pk, d_pk, m_pk, v_pk = _sum_and_adamw(parts, local_w, local_m, local_v)
    g_mat, d_mat, m_mat, v_mat = (_unpack_local(t) for t in (g_pk, d_pk, m_pk, v_pk))

    g_vec_pk = _all_reduce_small(_pack_vectors(gvec))
    d_vec_pk, m_vec_pk, v_vec_pk = _adamw_small(
        _pack_vectors(vec), g_vec_pk, _pack_vectors({n: inputs["m_" + n] for n, _ in VECTORS}),
        _pack_vectors({n: inputs["v_" + n] for n, _ in VECTORS}))
    g_vec, d_vec, m_vec, v_vec = (_unpack_vectors(t) for t in (g_vec_pk, d_vec_pk, m_vec_pk, v_vec_pk))

    def pick(mats, vecs, n):
        return mats[n] if n in mats else vecs[n]

    outs = [loss, grad_x[None]]
    for mats, vecs in ((g_mat, g_vec), (d_mat, d_vec), (m_mat, m_vec), (v_mat, v_vec)):
        outs += [pick(mats, vecs, n) for n in WEIGHT_ORDER]
    return tuple(outs)


def _local_grads(wfull, vec, x, mem, positions, target):
    w = {}
    for n, _, _, _ in MATRICES:
        if n.endswith("w_gate") or n.endswith("w_up"):
            continue
        w[n] = wfull[n]
    w["l0_w_in"] = jnp.pad(wfull["l0_w_in"], ((0, 0), (0, EVEN_IN_PAD - EVEN_IN)))
    for layer in (0, 1):
        p = "l%d_" % layer
        w[p + "w_gu"] = jnp.concatenate([wfull[p + "w_gate"], wfull[p + "w_up"]], axis=1)
    slots = {n: jnp.zeros(t.shape, GRAD_WIRE_DTYPE) for n, t in w.items()}

    tab64 = _rope_tables(positions, HEAD_DIM, 0, HEAD_DIM)
    tab_mla = _rope_tables(positions, MLA_ROPE_DIM, MLA_NOPE_DIM, LANES)
    diff = {"x": x, "slots": slots, "vec": vec}
    consts = {"w": w, "mem": mem, "tab64": tab64, "tab_mla": tab_mla, "target": target}
    loss_part, grads = jax.value_and_grad(_model_loss)(diff, consts)

    gfull = dict(grads["slots"])
    gfull["l0_w_in"] = gfull["l0_w_in"][:, :EVEN_IN]
    for layer in (0, 1):
        p = "l%d_" % layer
        gu = gfull.pop(p + "w_gu")
        gfull[p + "w_gate"], gfull[p + "w_up"] = gu[:, :FFN_HIDDEN], gu[:, FFN_HIDDEN:]
    return loss_part, grads["x"], gfull, grads["vec"]


_INPUT_NAMES = (("x", "mem", "positions") + WEIGHT_ORDER + ("loss_target",)
                + tuple("m_" + n for n in WEIGHT_ORDER) + tuple("v_" + n for n in WEIGHT_ORDER))


def kernel(*args):
    assert len(args) == len(_INPUT_NAMES)
    return _step(dict(zip(_INPUT_NAMES, args)))
```

```python
import functools
import math

import numpy as np
import jax
import jax.numpy as jnp
from jax import lax
from jax.experimental import pallas as pl
from jax.experimental.pallas import tpu as pltpu

F32 = jnp.float32
BF16 = jnp.bfloat16

LANES = 128
VMEM_LIMIT_BYTES = 56 * 1024 * 1024
MM_VMEM_BUDGET = 40 * 1024 * 1024
MM_MIN_FLOP_PER_STEP = 1e9
BAND_UNITS_PER_STEP = 4
CAUSAL_ROW_CHAIN = 128

D_MODEL = 1024
HEAD_DIM = 64
ROPE_THETA = 10000.0
NORM_EPS = 1e-6
BLOCK = 128
SWA_HEADS = 8
SWA_KV_HEADS = 2
SWA_WINDOW = 128
MLA_HEADS = 8
MLA_Q_RANK = 384
MLA_KV_RANK = 256
MLA_NOPE_DIM = 64
MLA_ROPE_DIM = 32
A_Q = SWA_HEADS * HEAD_DIM
A_KV = SWA_KV_HEADS * HEAD_DIM
EVEN_IN = A_Q + 2 * A_KV + MLA_Q_RANK + MLA_KV_RANK + MLA_ROPE_DIM
EVEN_IN_PAD = 1536
DIL_PATTERNS = ((128, 1), (512, 4), (2048, 16))
X_HEADS = 4
X_HEAD_DIM = 128
FFN_HIDDEN = 2816

ADAM_LR = 0.001
ADAM_B1 = 0.9
ADAM_B2 = 0.999
ADAM_EPS = 1e-08
ADAM_WD = 0.01
ADAM_STEP = 10

N_DEV = 8
GRAD_WIRE_DTYPE = BF16
NEG_MASK = -1e30
NEG_INIT = -1e20

MATRICES = (
    ("l0_w_in", "c", 1024, 1440), ("l0_w_uq", "c", 384, 768), ("l0_w_ukv", "c", 256, 1024),
    ("l0_w_out", "r", 1024, 1024), ("l0_w_xq", "r", 1024, 512), ("l0_w_xkv", "r", 1024, 1024),
    ("l0_w_xo", "c", 512, 1024), ("l0_w_gate", "c", 1024, 2816), ("l0_w_up", "c", 1024, 2816),
    ("l0_w_down", "r", 2816, 1024),
    ("l1_w_qkv", "c", 1024, 3072), ("l1_w_out", "r", 1024, 1024), ("l1_w_xq", "r", 1024, 512),
    ("l1_w_xkv", "r", 1024, 1024), ("l1_w_xo", "c", 512, 1024), ("l1_w_gate", "c", 1024, 2816),
    ("l1_w_up", "c", 1024, 2816), ("l1_w_down", "r", 2816, 1024),
)
VECTORS = (
    ("l0_mix_norm", 1024), ("l0_sinks", 8), ("l0_q_norm", 384), ("l0_kv_norm", 256), ("l0_x_norm", 1024),
    ("l0_mem_norm", 1024), ("l0_ffn_norm", 1024), ("l1_mix_norm", 1024), ("l1_x_norm", 1024),
    ("l1_mem_norm", 1024), ("l1_ffn_norm", 1024), ("final_norm", 1024),
)
WEIGHT_ORDER = (
    "l0_mix_norm", "l0_w_in", "l0_sinks", "l0_q_norm", "l0_w_uq", "l0_kv_norm", "l0_w_ukv", "l0_w_out", "l0_x_norm",
    "l0_mem_norm", "l0_w_xq", "l0_w_xkv", "l0_w_xo", "l0_ffn_norm", "l0_w_gate", "l0_w_up", "l0_w_down",
    "l1_mix_norm", "l1_w_qkv", "l1_w_out", "l1_x_norm", "l1_mem_norm", "l1_w_xq", "l1_w_xkv", "l1_w_xo",
    "l1_ffn_norm", "l1_w_gate", "l1_w_up", "l1_w_down", "final_norm",
)
PACK_COLS = 1024
PART_ROW_ALIGN = 16


def _part_rows(k, n):
    return -(-(k * n // N_DEV // PACK_COLS) // PART_ROW_ALIGN) * PART_ROW_ALIGN


MAT_ROWS_USED = sum(_part_rows(k, n) for _, _, k, n in MATRICES)
MAT_ROWS = -(-MAT_ROWS_USED // 256) * 256
VEC_ROWS = 16


def _pick(n, cands):
    for c in cands:
        if n % c == 0:
            return c
    return n


def _params(*sem):
    return pltpu.CompilerParams(dimension_semantics=sem, vmem_limit_bytes=VMEM_LIMIT_BYTES)


_DIMS = {"nn": (((1,), (0,)), ((), ())), "nt": (((1,), (1,)), ((), ())), "tn": (((0,), (0,)), ((), ()))}


def _div128(n, cap):
    d = (min(n, cap) // LANES) * LANES
    while d >= LANES:
        if n % d == 0:
            return d
        d -= LANES
    return n


def _mm_vmem_bytes(bm, bn, bk, nk, sa, sb, so, has_res):
    est = 2 * (bm * bk * sa + bk * bn * sb + bm * bn * so) + bm * bn * 4
    est += bm * bn * 4 if nk > 1 else 0
    est += 2 * bm * bn * 4 if has_res else 0
    est += bm * bk * 2 if sa == 4 else 0
    est += bk * bn * 2 if sb == 4 else 0
    return est


def _mm_tiles(m, n, k, sa, sb, so, has_res, mode):
    bn = _div128(n, 1536)
    kcap = 2048 if mode == "tn" else k
    for bm_cap in ((1024, 2048) if mode == "tn" else (512, 1024, 2048)):
        bm = _div128(m, bm_cap)
        bk = (min(k, kcap) // LANES) * LANES
        while bk > LANES and (k % bk or _mm_vmem_bytes(bm, bn, bk, k // bk, sa, sb, so, has_res) > MM_VMEM_BUDGET):
            bk -= LANES
        if 2 * bm * bn * bk >= MM_MIN_FLOP_PER_STEP or bm == m:
            break
    return bm, bn, bk


def _mm(a, b, mode, name, out_dtype=F32, res=None):
    if mode == "nn":
        (m, k), (k2, n) = a.shape, b.shape
    elif mode == "nt":
        (m, k), (n, k2) = a.shape, b.shape
    else:
        (k, m), (k2, n) = a.shape, b.shape
    assert k == k2, (name, a.shape, b.shape)
    has_res = res is not None
    bm, bn, bk = _mm_tiles(m, n, k, a.dtype.itemsize, b.dtype.itemsize, jnp.dtype(out_dtype).itemsize, has_res, mode)
    nk = k // bk
    dims = _DIMS[mode]
    a_spec = pl.BlockSpec((bk, bm), lambda i, j, kk: (kk, i)) if mode == "tn" else pl.BlockSpec((bm, bk), lambda i, j, kk: (i, kk))
    b_spec = pl.BlockSpec((bn, bk), lambda i, j, kk: (j, kk)) if mode == "nt" else pl.BlockSpec((bk, bn), lambda i, j, kk: (kk, j))
    o_spec = pl.BlockSpec((bm, bn), lambda i, j, kk: (i, j))

    def body(*refs):
        a_ref, b_ref = refs[0], refs[1]
        r_ref = refs[2] if has_res else None
        o_ref = refs[3] if has_res else refs[2]
        part = lax.dot_general(a_ref[...].astype(BF16), b_ref[...].astype(BF16), dims, preferred_element_type=F32)
        if nk == 1:
            o_ref[...] = (part + r_ref[...] if has_res else part).astype(out_dtype)
            return
        acc = refs[-1]
        kk = pl.program_id(2)

        @pl.when(kk == 0)
        def _():
            acc[...] = part

        @pl.when(jnp.logical_and(kk > 0, kk < nk - 1))
        def _():
            acc[...] += part

        @pl.when(kk == nk - 1)
        def _():
            r = acc[...] + part
            if has_res:
                r = r + r_ref[...]
            o_ref[...] = r.astype(out_dtype)

    args = (a, b, res) if has_res else (a, b)
    in_specs = [a_spec, b_spec] + ([o_spec] if has_res else [])
    return pl.pallas_call(
        body, name=name, grid=(m // bm, n // bn, nk), in_specs=in_specs, out_specs=o_spec,
        out_shape=jax.ShapeDtypeStruct((m, n), out_dtype),
        scratch_shapes=[pltpu.VMEM((bm, bn), F32)] if nk > 1 else [],
        compiler_params=_params("parallel", "parallel", "arbitrary"),
    )(*args)


def _rms_fwd(x, g, name, out_dtype=BF16):
    s, d = x.shape
    bs = _pick(s, (512, 256, 128))

    def body(x_ref, g_ref, o_ref):
        xv = x_ref[...]
        r = lax.rsqrt(jnp.mean(xv * xv, axis=-1, keepdims=True) + NORM_EPS)
        o_ref[...] = ((xv * r) * g_ref[...]).astype(out_dtype)

    return pl.pallas_call(
        body, name=name, grid=(s // bs,),
        in_specs=[pl.BlockSpec((bs, d), lambda i: (i, 0)), pl.BlockSpec((1, d), lambda i: (0, 0))],
        out_specs=pl.BlockSpec((bs, d), lambda i: (i, 0)), out_shape=jax.ShapeDtypeStruct((s, d), out_dtype),
        compiler_params=_params("parallel"),
    )(x, g.reshape(1, d))


def _rms_bwd(x, g, dy, name, dres=None):
    s, d = x.shape
    bs = _pick(s, (512, 256, 128))
    has_res = dres is not None

    def body(*refs):
        if has_res:
            x_ref, g_ref, dy_ref, r_ref, dx_ref, dg_ref = refs
        else:
            x_ref, g_ref, dy_ref, dx_ref, dg_ref = refs
        i = pl.program_id(0)
        xv = x_ref[...]
        dy = dy_ref[...]
        r = lax.rsqrt(jnp.mean(xv * xv, axis=-1, keepdims=True) + NORM_EPS)
        xh = xv * r
        dxh = dy * g_ref[...]
        dx = r * (dxh - xh * jnp.mean(dxh * xh, axis=-1, keepdims=True))
        if has_res:
            dx = dx + r_ref[...]
        dx_ref[...] = dx

        @pl.when(i == 0)
        def _():
            dg_ref[...] = jnp.zeros_like(dg_ref)

        dg_ref[...] += jnp.sum(dy * xh, axis=0, keepdims=True)

    row = pl.BlockSpec((bs, d), lambda i: (i, 0))
    vec = pl.BlockSpec((1, d), lambda i: (0, 0))
    args = (x, g.reshape(1, d), dy) + ((dres,) if has_res else ())
    dx, dg = pl.pallas_call(
        body, name=name, grid=(s // bs,), in_specs=[row, vec, row] + ([row] if has_res else []),
        out_specs=(row, vec), out_shape=(jax.ShapeDtypeStruct((s, d), F32), jax.ShapeDtypeStruct((1, d), F32)),
        compiler_params=_params("arbitrary"),
    )(*args)
    return dx, dg.reshape(d)


def _rope_tables(positions, dh, offset, period):
    role = np.zeros(LANES, np.int32)
    for base in range(0, LANES, period):
        role[base + offset:base + offset + dh // 2] = 1
        role[base + offset + dh // 2:base + offset + dh] = 2
    inv_freq = ROPE_THETA ** (-jnp.arange(0, dh, 2, dtype=F32) / dh)
    one_period = jnp.concatenate([jnp.zeros((offset,), F32), inv_freq, inv_freq,
                                  jnp.zeros((period - offset - dh,), F32)])
    ang = positions.astype(F32)[:, None] * jnp.tile(one_period, LANES // period)[None, :]
    c, s = jnp.cos(ang), jnp.sin(ang)
    role = role[None, :]
    a = jnp.where(role == 0, 1.0, c).astype(F32)
    bm = jnp.where(role == 2, s, 0.0).astype(F32)
    bp = jnp.where(role == 1, -s, 0.0).astype(F32)
    return a, bm, bp


def _rope_apply(x, tabs, half, transpose, name):
    s, w = x.shape
    bs = _pick(s, (512, 256, 128))
    nc = w // LANES
    a, bm, bp = tabs

    def body(x_ref, a_ref, bm_ref, bp_ref, o_ref):
        av, bmv, bpv = a_ref[...], bm_ref[...], bp_ref[...]
        for c in range(nc):
            sl = slice(c * LANES, (c + 1) * LANES)
            xv = x_ref[:, sl]
            if transpose:
                o_ref[:, sl] = xv * av + pltpu.roll(xv * bmv, LANES - half, 1) + pltpu.roll(xv * bpv, half, 1)
            else:
                o_ref[:, sl] = xv * av + pltpu.roll(xv, half, 1) * bmv + pltpu.roll(xv, LANES - half, 1) * bpv

    row = pl.BlockSpec((bs, w), lambda i: (i, 0))
    tab = pl.BlockSpec((bs, LANES), lambda i: (i, 0))
    return pl.pallas_call(
        body, name=name, grid=(s // bs,), in_specs=[row, tab, tab, tab], out_specs=row,
        out_shape=jax.ShapeDtypeStruct((s, w), F32), compiler_params=_params("parallel"),
    )(x, a, bm, bp)


def _make_rope(half, name):
    @jax.custom_vjp
    def rope(x, a, bm, bp):
        return _rope_apply(x, (a, bm, bp), half, False, name + "_fwd")

    def fwd(x, a, bm, bp):
        return rope(x, a, bm, bp), (a, bm, bp)

    def bwd(tabs, dy):
        return _rope_apply(dy, tabs, half, True, name + "_bwd"), None, None, None

    rope.defvjp(fwd, bwd)
    return rope


class AttnCfg:
    def __init__(self, mode, scale, hpb, bq, bk, upb, max_dist=0):
        self.mode, self.scale, self.hpb, self.bq, self.bk, self.upb, self.max_dist = mode, scale, hpb, bq, bk, upb, max_dist


def _kv_of_q(cfg, nq, nk):
    if cfg.mode == "causal":
        return nk, lambda i, j: (jnp.minimum(j, i), j <= i)
    if cfg.mode == "band":
        return 2, lambda i, j: (jnp.maximum(i - 1 + j, 0), i - 1 + j >= 0)
    return nk, lambda i, j: (j, j >= 0)


def _q_of_kv(cfg, nq, nk):
    if cfg.mode == "causal":
        return nq, lambda kb, j: (jnp.maximum(j, kb), j >= kb)
    if cfg.mode == "band":
        return 2, lambda kb, j: (jnp.minimum(kb + j, nq - 1), kb + j <= nq - 1)
    return nq, lambda kb, j: (j, j >= 0)


def _attn_mask(cfg, i, kb):
    if cfg.mode == "full":
        return None
    qpos = i * cfg.bq + lax.broadcasted_iota(jnp.int32, (cfg.bq, cfg.bk), 0)
    kpos = kb * cfg.bk + lax.broadcasted_iota(jnp.int32, (cfg.bq, cfg.bk), 1)
    dist = qpos - kpos
    if cfg.mode == "causal":
        return dist >= 0
    return (dist >= 0) & (dist <= cfg.max_dist)


def _lane_masks():
    lane = lax.broadcasted_iota(jnp.int32, (1, LANES), 1)
    lo = lane < HEAD_DIM
    return [lo, jnp.logical_not(lo)]


def _sel(mask, v):
    return jnp.where(mask, v, jnp.zeros_like(v))


_NT = (((1,), (1,)), ((), ()))
_NN = (((1,), (0,)), ((), ()))
_TN = (((0,), (0,)), ((), ()))


def _dot(a, b, dims):
    return lax.dot_general(a, b, dims, preferred_element_type=F32)


def _attn_fwd(q, k, v, sinkrow, cfg, name):
    sq, w = q.shape
    sk = k.shape[0]
    bq, bk, upb, hpb = cfg.bq, cfg.bk, cfg.upb, cfg.hpb
    nq, nk, nub = sq // bq, sk // bk, w // (LANES * upb)
    nj, sched = _kv_of_q(cfg, nq, nk)
    wb = LANES * upb
    has_sink = sinkrow is not None

    def body(*refs):
        if has_sink:
            q_ref, k_ref, v_ref, s_ref, o_ref, l_ref, m_sc, l_sc, acc_sc = refs
        else:
            q_ref, k_ref, v_ref, o_ref, l_ref, m_sc, l_sc, acc_sc = refs
        i, j = pl.program_id(1), pl.program_id(2)
        kb, active = sched(i, j)
        lms = _lane_masks()

        @pl.when(j == 0)
        def _():
            for u in range(upb):
                for a in range(hpb):
                    if has_sink:
                        srow = s_ref[:, u * LANES:(u + 1) * LANES]
                        sk_a = jnp.max(jnp.where(lms[a], srow, -jnp.inf), axis=-1, keepdims=True)
                        m_sc[u * hpb + a] = jnp.broadcast_to(sk_a, (bq, 1))
                        l_sc[u * hpb + a] = jnp.ones((bq, 1), F32)
                    else:
                        m_sc[u * hpb + a] = jnp.full((bq, 1), NEG_INIT, F32)
                        l_sc[u * hpb + a] = jnp.zeros((bq, 1), F32)
            acc_sc[...] = jnp.zeros_like(acc_sc)

        @pl.when(active)
        def _():
            mask = _attn_mask(cfg, i, kb)
            for u in range(upb):
                sl = slice(u * LANES, (u + 1) * LANES)
                qv = q_ref[:, sl].astype(BF16)
                kv = k_ref[:, sl].astype(BF16)
                vv = v_ref[:, sl].astype(BF16)
                pv_tot, alphas = None, []
                for a in range(hpb):
                    idx = u * hpb + a
                    qa = _sel(lms[a], qv) if hpb == 2 else qv
                    s = _dot(qa, kv, _NT) * cfg.scale
                    if mask is not None:
                        s = jnp.where(mask, s, NEG_MASK)
                    m_prev = m_sc[idx]
                    m_new = jnp.maximum(m_prev, jnp.max(s, axis=-1, keepdims=True))
                    alpha = jnp.exp(m_prev - m_new)
                    p = jnp.exp(s - m_new)
                    l_sc[idx] = alpha * l_sc[idx] + jnp.sum(p, axis=-1, keepdims=True)
                    m_sc[idx] = m_new
                    va = _sel(lms[a], vv) if hpb == 2 else vv
                    pv = _dot(p.astype(BF16), va, _NN)
                    pv_tot = pv if pv_tot is None else pv_tot + pv
                    alphas.append(alpha)
                af = alphas[0] if hpb == 1 else jnp.where(lms[0], alphas[0], alphas[1])
                acc_sc[u] = acc_sc[u] * af + pv_tot

        @pl.when(j == nj - 1)
        def _():
            for u in range(upb):
                sl = slice(u * LANES, (u + 1) * LANES)
                if hpb == 1:
                    lf = jnp.broadcast_to(l_sc[u], (bq, LANES))
                    mf = jnp.broadcast_to(m_sc[u], (bq, LANES))
                else:
                    lf = jnp.where(lms[0], l_sc[2 * u], l_sc[2 * u + 1])
                    mf = jnp.where(lms[0], m_sc[2 * u], m_sc[2 * u + 1])
                o_ref[:, sl] = acc_sc[u] / lf
                l_ref[:, sl] = mf + jnp.log(lf)

    qspec = pl.BlockSpec((bq, wb), lambda ub, i, j: (i, ub))
    kspec = pl.BlockSpec((bk, wb), lambda ub, i, j: (sched(i, j)[0], ub))
    in_specs = [qspec, kspec, kspec] + ([pl.BlockSpec((1, wb), lambda ub, i, j: (0, ub))] if has_sink else [])
    args = (q, k, v) + ((sinkrow,) if has_sink else ())
    return pl.pallas_call(
        body, name=name, grid=(nub, nq, nj), in_specs=in_specs, out_specs=(qspec, qspec),
        out_shape=(jax.ShapeDtypeStruct((sq, w), F32), jax.ShapeDtypeStruct((sq, w), F32)),
        scratch_shapes=[pltpu.VMEM((upb * hpb, bq, 1), F32), pltpu.VMEM((upb * hpb, bq, 1), F32),
                        pltpu.VMEM((upb, bq, LANES), F32)],
        compiler_params=_params("parallel", "parallel", "arbitrary"),
    )(*args)


def _softmax_grad_terms(cfg, lms, a, qv, kv, vv, dob, prod, lv, mask):
    hpb = cfg.hpb
    if hpb == 2:
        t = jnp.sum(_sel(lms[a], prod), axis=-1, keepdims=True)
        lse = jnp.max(jnp.where(lms[a], lv, -jnp.inf), axis=-1, keepdims=True)
        qa, doa = _sel(lms[a], qv), _sel(lms[a], dob)
    else:
        t = jnp.sum(prod, axis=-1, keepdims=True)
        lse = jnp.max(lv, axis=-1, keepdims=True)
        qa, doa = qv, dob
    s = _dot(qa, kv, _NT) * cfg.scale
    if mask is not None:
        s = jnp.where(mask, s, NEG_MASK)
    p = jnp.exp(s - lse)
    dp = _dot(doa, vv, _NT)
    ds = (p * (dp - t)) * cfg.scale
    return p, ds, qa, doa, t


def _attn_dq(q, k, v, o, lse, do, sinkrow, cfg, name):
    sq, w = q.shape
    sk = k.shape[0]
    bq, bk, upb, hpb = cfg.bq, cfg.bk, cfg.upb, cfg.hpb
    nq, nk, nub = sq // bq, sk // bk, w // (LANES * upb)
    nj, sched = _kv_of_q(cfg, nq, nk)
    wb = LANES * upb
    has_sink = sinkrow is not None

    def body(*refs):
        if has_sink:
            q_ref, k_ref, v_ref, o_ref, l_ref, do_ref, s_ref, dq_ref, dsink_ref, acc = refs
        else:
            q_ref, k_ref, v_ref, o_ref, l_ref, do_ref, dq_ref, acc = refs
        i, j = pl.program_id(1), pl.program_id(2)
        kb, active = sched(i, j)
        lms = _lane_masks()

        @pl.when(j == 0)
        def _():
            acc[...] = jnp.zeros_like(acc)

        @pl.when(active)
        def _():
            mask = _attn_mask(cfg, i, kb)
            for u in range(upb):
                sl = slice(u * LANES, (u + 1) * LANES)
                qv = q_ref[:, sl].astype(BF16)
                kv = k_ref[:, sl].astype(BF16)
                vv = v_ref[:, sl].astype(BF16)
                dov = do_ref[:, sl]
                prod = dov * o_ref[:, sl]
                dob = dov.astype(BF16)
                lv = l_ref[:, sl]
                tot = None
                for a in range(hpb):
                    _, ds, _, _, _ = _softmax_grad_terms(cfg, lms, a, qv, kv, vv, dob, prod, lv, mask)
                    ka = _sel(lms[a], kv) if hpb == 2 else kv
                    c = _dot(ds.astype(BF16), ka, _NN)
                    tot = c if tot is None else tot + c
                acc[u] = acc[u] + tot

        @pl.when(j == nj - 1)
        def _():
            for u in range(upb):
                dq_ref[:, u * LANES:(u + 1) * LANES] = acc[u]
            if has_sink:
                @pl.when(i == 0)
                def _():
                    dsink_ref[...] = jnp.zeros_like(dsink_ref)

                for u in range(upb):
                    sl = slice(u * LANES, (u + 1) * LANES)
                    prod = do_ref[:, sl] * o_ref[:, sl]
                    t0 = jnp.sum(_sel(lms[0], prod), axis=-1, keepdims=True)
                    t1 = jnp.sum(_sel(lms[1], prod), axis=-1, keepdims=True)
                    tf = jnp.where(lms[0], t0, t1)
                    rs = -jnp.exp(s_ref[:, sl] - l_ref[:, sl]) * tf
                    dsink_ref[0:1, sl] += jnp.sum(rs, axis=0, keepdims=True)

    qspec = pl.BlockSpec((bq, wb), lambda ub, i, j: (i, ub))
    kspec = pl.BlockSpec((bk, wb), lambda ub, i, j: (sched(i, j)[0], ub))
    in_specs = [qspec, kspec, kspec, qspec, qspec, qspec]
    args = (q, k, v, o, lse, do)
    out_specs = qspec
    out_shape = jax.ShapeDtypeStruct((sq, w), F32)
    sem = ("parallel", "parallel", "arbitrary")
    if has_sink:
        in_specs = in_specs + [pl.BlockSpec((1, wb), lambda ub, i, j: (0, ub))]
        args = args + (sinkrow,)
        out_specs = (qspec, pl.BlockSpec((8, wb), lambda ub, i, j: (0, ub)))
        out_shape = (out_shape, jax.ShapeDtypeStruct((8, w), F32))
        sem = ("parallel", "arbitrary", "arbitrary")
    return pl.pallas_call(
        body, name=name, grid=(nub, nq, nj), in_specs=in_specs, out_specs=out_specs, out_shape=out_shape,
        scratch_shapes=[pltpu.VMEM((upb, bq, LANES), F32)], compiler_params=_params(*sem),
    )(*args)


def _attn_dkv(q, k, v, o, lse, do, cfg, name):
    sq, w = q.shape
    sk = k.shape[0]
    bq, bk, upb, hpb = cfg.bq, cfg.bk, cfg.upb, cfg.hpb
    nq, nk, nub = sq // bq, sk // bk, w // (LANES * upb)
    nj, sched = _q_of_kv(cfg, nq, nk)
    wb = LANES * upb

    def body(q_ref, k_ref, v_ref, o_ref, l_ref, do_ref, dk_ref, dv_ref, dk_acc, dv_acc):
        kb, j = pl.program_id(1), pl.program_id(2)
        i, active = sched(kb, j)
        lms = _lane_masks()

        @pl.when(j == 0)
        def _():
            dk_acc[...] = jnp.zeros_like(dk_acc)
            dv_acc[...] = jnp.zeros_like(dv_acc)

        @pl.when(active)
        def _():
            mask = _attn_mask(cfg, i, kb)
            for u in range(upb):
                sl = slice(u * LANES, (u + 1) * LANES)
                qv = q_ref[:, sl].astype(BF16)
                kv = k_ref[:, sl].astype(BF16)
                vv = v_ref[:, sl].astype(BF16)
                dov = do_ref[:, sl]
                prod = dov * o_ref[:, sl]
                dob = dov.astype(BF16)
                lv = l_ref[:, sl]
                dk_tot, dv_tot = None, None
                for a in range(hpb):
                    p, ds, qa, doa, _ = _softmax_grad_terms(cfg, lms, a, qv, kv, vv, dob, prod, lv, mask)
                    dvc = _dot(p.astype(BF16), doa, _TN)
                    dkc = _dot(ds.astype(BF16), qa, _TN)
                    dv_tot = dvc if dv_tot is None else dv_tot + dvc
                    dk_tot = dkc if dk_tot is None else dk_tot + dkc
                dk_acc[u] = dk_acc[u] + dk_tot
                dv_acc[u] = dv_acc[u] + dv_tot

        @pl.when(j == nj - 1)
        def _():
            for u in range(upb):
                sl = slice(u * LANES, (u + 1) * LANES)
                dk_ref[:, sl] = dk_acc[u]
                dv_ref[:, sl] = dv_acc[u]

    qspec = pl.BlockSpec((bq, wb), lambda ub, kb, j: (sched(kb, j)[0], ub))
    kspec = pl.BlockSpec((bk, wb), lambda ub, kb, j: (kb, ub))
    return pl.pallas_call(
        body, name=name, grid=(nub, nk, nj), in_specs=[qspec, kspec, kspec, qspec, qspec, qspec],
        out_specs=(kspec, kspec),
        out_shape=(jax.ShapeDtypeStruct((sk, w), F32), jax.ShapeDtypeStruct((sk, w), F32)),
        scratch_shapes=[pltpu.VMEM((upb, bk, LANES), F32), pltpu.VMEM((upb, bk, LANES), F32)],
        compiler_params=_params("parallel", "parallel", "arbitrary"),
    )(q, k, v, o, lse, do)


def _band_masks(max_dist):
    assert BLOCK - 1 <= max_dist <= BLOCK
    r = lax.broadcasted_iota(jnp.int32, (BLOCK, BLOCK), 0)
    c = lax.broadcasted_iota(jnp.int32, (BLOCK, BLOCK), 1)
    return (BLOCK + r - c) <= max_dist, r >= c


def _stack_heads(t):
    return jnp.concatenate([t, t], axis=0)


def _head_terms(lms, a, prod, lv):
    t = jnp.sum(_sel(lms[a], prod), axis=-1, keepdims=True)
    lse = jnp.max(jnp.where(lms[a], lv, -jnp.inf), axis=-1, keepdims=True)
    return t, lse


class _Residue:
    def __init__(self, ref, r, dil):
        self.ref, self.rows = ref, pl.ds(r, BLOCK, stride=dil)

    def __getitem__(self, idx):
        return self.ref[self.rows, idx[1]]

    def __setitem__(self, idx, val):
        self.ref[self.rows, idx[1]] = val


def _residues(refs, dil):
    if dil == 1:
        return [tuple(refs)]
    return [tuple(_Residue(x, r, dil) for x in refs) for r in range(dil)]


def _band_fwd(q, k, v, sinkrow, scale, max_dist, upb, dil, name):
    sq, w = q.shape
    rb = BLOCK * dil
    nq, nub, wb = sq // rb, w // (LANES * upb), LANES * upb
    has_sink = sinkrow is not None

    def body(*refs):
        s_ref = refs[5] if has_sink else None
        lms = _lane_masks()
        mprev, mcur = _band_masks(max_dist)
        mprev = jnp.logical_and(mprev, pl.program_id(1) > 0)
        mask2 = _stack_heads(jnp.concatenate([mprev, mcur], axis=1))
        for q_ref, kp_ref, kc_ref, vp_ref, vc_ref, o_ref, l_ref in _residues(refs[:5] + refs[-2:], dil):
          for u in range(upb):
            sl = slice(u * LANES, (u + 1) * LANES)
            qv = (q_ref[:, sl] * scale).astype(BF16)
            kp, kc = kp_ref[:, sl].astype(BF16), kc_ref[:, sl].astype(BF16)
            vp, vc = vp_ref[:, sl].astype(BF16), vc_ref[:, sl].astype(BF16)
            qs = jnp.concatenate([_sel(lms[0], qv), _sel(lms[1], qv)], axis=0)
            kcat, vcat = jnp.concatenate([kp, kc], axis=0), jnp.concatenate([vp, vc], axis=0)
            sc = jnp.where(mask2, _dot(qs, kcat, _NT), NEG_MASK)
            m = jnp.max(sc, axis=-1, keepdims=True)
            if has_sink:
                sk2 = jnp.concatenate(
                    [jnp.broadcast_to(jnp.max(jnp.where(lms[a], s_ref[:, sl], -jnp.inf), axis=-1, keepdims=True),
                                      (BLOCK, 1)) for a in range(2)], axis=0)
                m = jnp.maximum(m, sk2)
            p = jnp.exp(sc - m)
            l = jnp.sum(p, axis=-1, keepdims=True)
            if has_sink:
                l = l + jnp.exp(sk2 - m)
            o2 = _dot(p.astype(BF16), vcat, _NN) / l
            lse2 = m + jnp.log(l)
            o_ref[:, sl] = jnp.where(lms[0], o2[:BLOCK], o2[BLOCK:])
            l_ref[:, sl] = jnp.where(lms[0], lse2[:BLOCK], lse2[BLOCK:])

    cur = pl.BlockSpec((rb, wb), lambda ub, i: (i, ub))
    prev = pl.BlockSpec((rb, wb), lambda ub, i: (jnp.maximum(i - 1, 0), ub))
    in_specs = [cur, prev, cur, prev, cur] + ([pl.BlockSpec((1, wb), lambda ub, i: (0, ub))] if has_sink else [])
    args = (q, k, k, v, v) + ((sinkrow,) if has_sink else ())
    return pl.pallas_call(
        body, name=name, grid=(nub, nq), in_specs=in_specs, out_specs=(cur, cur),
        out_shape=(jax.ShapeDtypeStruct((sq, w), F32), jax.ShapeDtypeStruct((sq, w), F32)),
        compiler_params=_params("parallel", "parallel"),
    )(*args)


def _band_dq(q, k, v, o, lse, do, sinkrow, scale, max_dist, upb, dil, name):
    sq, w = q.shape
    rb = BLOCK * dil
    nq, nub, wb = sq // rb, w // (LANES * upb), LANES * upb
    has_sink = sinkrow is not None

    def body(*refs):
        if has_sink:
            s_ref, dq_block, dsink_ref = refs[8], refs[9], refs[10]
        else:
            dq_block = refs[8]
        i = pl.program_id(1)
        lms = _lane_masks()
        mprev, mcur = _band_masks(max_dist)
        mprev = jnp.logical_and(mprev, i > 0)
        mask2 = _stack_heads(jnp.concatenate([mprev, mcur], axis=1))
        if has_sink:
            @pl.when(i == 0)
            def _():
                dsink_ref[...] = jnp.zeros_like(dsink_ref)

        for q_ref, kp_ref, kc_ref, vp_ref, vc_ref, o_ref, l_ref, do_ref, dq_ref in _residues(refs[:8] + (dq_block,), dil):
          for u in range(upb):
            sl = slice(u * LANES, (u + 1) * LANES)
            qv = (q_ref[:, sl] * scale).astype(BF16)
            kp, kc = kp_ref[:, sl].astype(BF16), kc_ref[:, sl].astype(BF16)
            vp, vc = vp_ref[:, sl].astype(BF16), vc_ref[:, sl].astype(BF16)
            dov = do_ref[:, sl]
            prod = dov * o_ref[:, sl]
            dob = dov.astype(BF16)
            lv = l_ref[:, sl]
            (t0, lse0), (t1, lse1) = _head_terms(lms, 0, prod, lv), _head_terms(lms, 1, prod, lv)
            t2, lse2 = jnp.concatenate([t0, t1], axis=0), jnp.concatenate([lse0, lse1], axis=0)
            qs = jnp.concatenate([_sel(lms[0], qv), _sel(lms[1], qv)], axis=0)
            dos = jnp.concatenate([_sel(lms[0], dob), _sel(lms[1], dob)], axis=0)
            kcat, vcat = jnp.concatenate([kp, kc], axis=0), jnp.concatenate([vp, vc], axis=0)
            p = jnp.exp(jnp.where(mask2, _dot(qs, kcat, _NT), NEG_MASK) - lse2)
            ds = (p * (_dot(dos, vcat, _NT) - t2)).astype(BF16)
            dq2 = _dot(ds, kcat, _NN) * scale
            dq_ref[:, sl] = jnp.where(lms[0], dq2[:BLOCK], dq2[BLOCK:])
            if has_sink:
                rs = -jnp.exp(s_ref[:, sl] - lv) * jnp.where(lms[0], t0, t1)
                dsink_ref[0:1, sl] += jnp.sum(rs, axis=0, keepdims=True)

    cur = pl.BlockSpec((rb, wb), lambda ub, i: (i, ub))
    prev = pl.BlockSpec((rb, wb), lambda ub, i: (jnp.maximum(i - 1, 0), ub))
    in_specs = [cur, prev, cur, prev, cur, cur, cur, cur]
    args = (q, k, k, v, v, o, lse, do)
    out_specs, out_shape = cur, jax.ShapeDtypeStruct((sq, w), F32)
    sem = ("parallel", "parallel")
    if has_sink:
        in_specs = in_specs + [pl.BlockSpec((1, wb), lambda ub, i: (0, ub))]
        args = args + (sinkrow,)
        out_specs = (cur, pl.BlockSpec((8, wb), lambda ub, i: (0, ub)))
        out_shape = (out_shape, jax.ShapeDtypeStruct((8, w), F32))
        sem = ("parallel", "arbitrary")
    return pl.pallas_call(
        body, name=name, grid=(nub, nq), in_specs=in_specs, out_specs=out_specs, out_shape=out_shape,
        compiler_params=_params(*sem),
    )(*args)


def _band_dkv(q, k, v, o, lse, do, scale, max_dist, upb, dil, name):
    sq, w = q.shape
    rb = BLOCK * dil
    nq, nub, wb = sq // rb, w // (LANES * upb), LANES * upb

    def body(*refs):
        kb = pl.program_id(1)
        lms = _lane_masks()
        key = lax.broadcasted_iota(jnp.int32, (BLOCK, BLOCK), 0)
        qry = lax.broadcasted_iota(jnp.int32, (BLOCK, BLOCK), 1)
        msame = qry >= key
        mnext = jnp.logical_and((BLOCK + qry - key) <= max_dist, kb < nq - 1)
        mask4 = jnp.concatenate([msame, msame, mnext, mnext], axis=1)
        for (k_ref, v_ref, qs_ref, qn_ref, os_ref, on_ref, ls_ref, ln_ref, dos_ref, don_ref, dk_ref,
             dv_ref) in _residues(refs, dil):
          for u in range(upb):
            sl = slice(u * LANES, (u + 1) * LANES)
            kv, vv = k_ref[:, sl].astype(BF16), v_ref[:, sl].astype(BF16)
            qparts, doparts, tparts, lparts = [], [], [], []
            for q_ref, o_ref, l_ref, do_ref in ((qs_ref, os_ref, ls_ref, dos_ref), (qn_ref, on_ref, ln_ref, don_ref)):
                qv = (q_ref[:, sl] * scale).astype(BF16)
                dov = do_ref[:, sl]
                prod_t = (dov * o_ref[:, sl]).T
                dob = dov.astype(BF16)
                lse_t = l_ref[:, sl].T
                for a in range(2):
                    lanes = slice(a * HEAD_DIM, (a + 1) * HEAD_DIM)
                    qparts.append(_sel(lms[a], qv))
                    doparts.append(_sel(lms[a], dob))
                    tparts.append(jnp.sum(prod_t[lanes, :], axis=0, keepdims=True))
                    lparts.append(lse_t[a * HEAD_DIM:a * HEAD_DIM + 1, :])
            qs, dos = jnp.concatenate(qparts, axis=0), jnp.concatenate(doparts, axis=0)
            t4, lse4 = jnp.concatenate(tparts, axis=1), jnp.concatenate(lparts, axis=1)
            p = jnp.exp(jnp.where(mask4, _dot(kv, qs, _NT), NEG_MASK) - lse4)
            ds = (p * (_dot(vv, dos, _NT) - t4)).astype(BF16)
            dv_ref[:, sl] = _dot(p.astype(BF16), dos, _NN)
            dk_ref[:, sl] = _dot(ds, qs, _NN)

    same = pl.BlockSpec((rb, wb), lambda ub, kb: (kb, ub))
    nxt = pl.BlockSpec((rb, wb), lambda ub, kb: (jnp.minimum(kb + 1, nq - 1), ub))
    return pl.pallas_call(
        body, name=name, grid=(nub, nq), in_specs=[same, same, same, nxt, same, nxt, same, nxt, same, nxt],
        out_specs=(same, same),
        out_shape=(jax.ShapeDtypeStruct((sq, w), F32), jax.ShapeDtypeStruct((sq, w), F32)),
        compiler_params=_params("parallel", "parallel"),
    )(k, v, q, q, o, o, lse, lse, do, do)


def _make_band_attention(scale, max_dist, upb, name):
    @jax.custom_vjp
    def attn(q, k, v, sinks):
        return _band_fwd(q, k, v, _sink_row(sinks), scale, max_dist, upb, 1, name + "_fwd")[0]

    def fwd(q, k, v, sinks):
        o, lse = _band_fwd(q, k, v, _sink_row(sinks), scale, max_dist, upb, 1, name + "_fwd")
        return o, (q, k, v, o, lse, sinks)

    def bwd(res, do):
        q, k, v, o, lse, sinks = res
        dq, dsink = _band_dq(q, k, v, o, lse, do, _sink_row(sinks), scale, max_dist, upb, 1, name + "_dq")
        dk, dv = _band_dkv(q, k, v, o, lse, do, scale, max_dist, upb, 1, name + "_dkv")
        return dq, dk, dv, dsink[0].reshape(-1, HEAD_DIM)[:, 0]

    attn.defvjp(fwd, bwd)
    return attn


def _triangle(n, by_key):
    if by_key:
        pairs = [(i, kb) for kb in range(n) for i in range(kb, n)]
    else:
        pairs = [(i, j) for i in range(n) for j in range(i + 1)]
    qi = np.asarray([p[0] for p in pairs], np.int32)
    kj = np.asarray([p[1] for p in pairs], np.int32)
    return jnp.asarray(qi), jnp.asarray(kj)


def _causal_fwd(q, k, v, scale, blk, name):
    s, w = q.shape
    nq, nub = s // blk, w // LANES
    qi, kj = _triangle(nq, by_key=False)

    def body(qi_ref, kj_ref, q_ref, k_ref, v_ref, o_ref, l_ref, m_sc, l_sc, acc_sc):
        t = pl.program_id(1)
        i, j = qi_ref[t], kj_ref[t]

        @pl.when(j == 0)
        def _():
            m_sc[...] = jnp.full_like(m_sc, NEG_INIT)
            l_sc[...] = jnp.zeros_like(l_sc)
            acc_sc[...] = jnp.zeros_like(acc_sc)

        def step(diagonal):
            kv, vv = k_ref[...].astype(BF16), v_ref[...].astype(BF16)
            chains = range(0, blk, CAUSAL_ROW_CHAIN)
            scs = [_dot((q_ref[c0:c0 + CAUSAL_ROW_CHAIN, :] * scale).astype(BF16), kv, _NT) for c0 in chains]
            m_all, l_all, acc_all = m_sc[...], l_sc[...], acc_sc[...]
            m_out, l_out, acc_out = [], [], []
            for sc, c0 in zip(scs, chains):
                rows = slice(c0, c0 + CAUSAL_ROW_CHAIN)
                if diagonal:
                    r = c0 + lax.broadcasted_iota(jnp.int32, (CAUSAL_ROW_CHAIN, blk), 0)
                    c = lax.broadcasted_iota(jnp.int32, (CAUSAL_ROW_CHAIN, blk), 1)
                    sc = jnp.where(r >= c, sc, NEG_MASK)
                m_prev = m_all[rows]
                m_new = jnp.maximum(m_prev, jnp.max(sc, axis=-1, keepdims=True))
                alpha = jnp.exp(m_prev - m_new)
                p = jnp.exp(sc - m_new)
                l_out.append(alpha * l_all[rows] + jnp.sum(p, axis=-1, keepdims=True))
                m_out.append(m_new)
                acc_out.append(acc_all[rows] * alpha + _dot(p.astype(BF16), vv, _NN))
            m_sc[...] = jnp.concatenate(m_out, axis=0)
            l_sc[...] = jnp.concatenate(l_out, axis=0)
            acc_sc[...] = jnp.concatenate(acc_out, axis=0)

        @pl.when(j < i)
        def _():
            step(False)

        @pl.when(j == i)
        def _():
            step(True)
            lf = l_sc[...]
            o_ref[...] = acc_sc[...] / lf
            l_ref[...] = jnp.broadcast_to(m_sc[...] + jnp.log(lf), (blk, LANES))

    qspec = pl.BlockSpec((blk, LANES), lambda ub, t, qi_ref, kj_ref: (qi_ref[t], ub))
    kspec = pl.BlockSpec((blk, LANES), lambda ub, t, qi_ref, kj_ref: (kj_ref[t], ub))
    return pl.pallas_call(
        body, name=name,
        grid_spec=pltpu.PrefetchScalarGridSpec(
            num_scalar_prefetch=2, grid=(nub, qi.shape[0]), in_specs=[qspec, kspec, kspec], out_specs=(qspec, qspec),
            scratch_shapes=[pltpu.VMEM((blk, 1), F32), pltpu.VMEM((blk, 1), F32), pltpu.VMEM((blk, LANES), F32)]),
        out_shape=(jax.ShapeDtypeStruct((s, w), F32), jax.ShapeDtypeStruct((s, w), F32)),
        compiler_params=_params("parallel", "arbitrary"),
    )(qi, kj, q, k, v)


def _causal_bwd(q, k, v, o, lse, do, scale, blk, name):
    s, w = q.shape
    nq, nub = s // blk, w // LANES
    qi, kj = _triangle(nq, by_key=True)

    def body(qi_ref, kj_ref, q_ref, k_ref, v_ref, o_ref, l_ref, do_ref, dq_ref, dk_ref, dv_ref, dk_acc, dv_acc):
        t = pl.program_id(1)
        i, kb = qi_ref[t], kj_ref[t]

        @pl.when(t == 0)
        def _():
            dq_ref[...] = jnp.zeros_like(dq_ref)

        @pl.when(i == kb)
        def _():
            dk_acc[...] = jnp.zeros_like(dk_acc)
            dv_acc[...] = jnp.zeros_like(dv_acc)

        def step(diagonal):
            qv = (q_ref[...] * scale).astype(BF16)
            kv, vv = k_ref[...].astype(BF16), v_ref[...].astype(BF16)
            dov = do_ref[...]
            tsum = jnp.sum(dov * o_ref[...], axis=-1, keepdims=True)
            dob = dov.astype(BF16)
            sc = _dot(qv, kv, _NT)
            if diagonal:
                r = lax.broadcasted_iota(jnp.int32, (blk, blk), 0)
                c = lax.broadcasted_iota(jnp.int32, (blk, blk), 1)
                sc = jnp.where(r >= c, sc, NEG_MASK)
            p = jnp.exp(sc - l_ref[:, 0:1])
            ds = (p * (_dot(dob, vv, _NT) - tsum)).astype(BF16)
            dv_acc[...] += _dot(p.astype(BF16), dob, _TN)
            dk_acc[...] += _dot(ds, qv, _TN)
            rows = pl.ds(pl.multiple_of(i * blk, blk), blk)
            dq_ref[rows, :] += _dot(ds, kv, _NN) * scale

        @pl.when(i == kb)
        def _():
            step(True)

        @pl.when(i > kb)
        def _():
            step(False)

        @pl.when(i == nq - 1)
        def _():
            dk_ref[...] = dk_acc[...]
            dv_ref[...] = dv_acc[...]

    qspec = pl.BlockSpec((blk, LANES), lambda ub, t, qi_ref, kj_ref: (qi_ref[t], ub))
    kspec = pl.BlockSpec((blk, LANES), lambda ub, t, qi_ref, kj_ref: (kj_ref[t], ub))
    whole = pl.BlockSpec((s, LANES), lambda ub, t, qi_ref, kj_ref: (0, ub))
    out = jax.ShapeDtypeStruct((s, w), F32)
    return pl.pallas_call(
        body, name=name,
        grid_spec=pltpu.PrefetchScalarGridSpec(
            num_scalar_prefetch=2, grid=(nub, qi.shape[0]), in_specs=[qspec, kspec, kspec, qspec, qspec, qspec],
            out_specs=(whole, kspec, kspec),
            scratch_shapes=[pltpu.VMEM((blk, LANES), F32), pltpu.VMEM((blk, LANES), F32)]),
        out_shape=(out, out, out), compiler_params=_params("parallel", "arbitrary"),
    )(qi, kj, q, k, v, o, lse, do)


def _make_causal_attention(scale, blk, name):
    @jax.custom_vjp
    def attn(q, k, v):
        return _causal_fwd(q, k, v, scale, blk, name + "_fwd")[0]

    def fwd(q, k, v):
        o, lse = _causal_fwd(q, k, v, scale, blk, name + "_fwd")
        return o, (q, k, v, o, lse)

    def bwd(res, do):
        q, k, v, o, lse = res
        return _causal_bwd(q, k, v, o, lse, do, scale, blk, name + "_bwd")

    attn.defvjp(fwd, bwd)
    return attn


def _make_attention(cfg, name, with_sink=False):
    if with_sink:
        @jax.custom_vjp
        def attn(q, k, v, sinks):
            return _attn_fwd(q, k, v, _sink_row(sinks), cfg, name + "_fwd")[0]

        def fwd(q, k, v, sinks):
            o, lse = _attn_fwd(q, k, v, _sink_row(sinks), cfg, name + "_fwd")
            return o, (q, k, v, o, lse, sinks)

        def bwd(res, do):
            q, k, v, o, lse, sinks = res
            dq, dsink = _attn_dq(q, k, v, o, lse, do, _sink_row(sinks), cfg, name + "_dq")
            dk, dv = _attn_dkv(q, k, v, o, lse, do, cfg, name + "_dkv")
            return dq, dk, dv, dsink[0].reshape(-1, HEAD_DIM)[:, 0]
    else:
        @jax.custom_vjp
        def attn(q, k, v):
            return _attn_fwd(q, k, v, None, cfg, name + "_fwd")[0]

        def fwd(q, k, v):
            o, lse = _attn_fwd(q, k, v, None, cfg, name + "_fwd")
            return o, (q, k, v, o, lse)

        def bwd(res, do):
            q, k, v, o, lse = res
            dq = _attn_dq(q, k, v, o, lse, do, None, cfg, name + "_dq")
            dk, dv = _attn_dkv(q, k, v, o, lse, do, cfg, name + "_dkv")
            return dq, dk, dv

    attn.defvjp(fwd, bwd)
    return attn


def _sink_row(sinks):
    return jnp.repeat(sinks.astype(F32), HEAD_DIM).reshape(1, -1)


def _merge3(os_, ls_, name):
    s, w = os_[0].shape
    bs = _pick(s, (256, 128))

    def body(o1, o2, o3, l1, l2, l3, out_ref, lse_ref):
        a1, a2, a3 = l1[...], l2[...], l3[...]
        m = jnp.maximum(jnp.maximum(a1, a2), a3)
        e1, e2, e3 = jnp.exp(a1 - m), jnp.exp(a2 - m), jnp.exp(a3 - m)
        z = e1 + e2 + e3
        out_ref[...] = (e1 * o1[...] + e2 * o2[...] + e3 * o3[...]) / z
        lse_ref[...] = m + jnp.log(z)

    row = pl.BlockSpec((bs, w), lambda i: (i, 0))
    return pl.pallas_call(
        body, name=name, grid=(s // bs,), in_specs=[row] * 6, out_specs=(row, row),
        out_shape=(jax.ShapeDtypeStruct((s, w), F32), jax.ShapeDtypeStruct((s, w), F32)),
        compiler_params=_params("parallel"),
    )(*os_, *ls_)


def _add3(a, b, c, name):
    s, w = a.shape
    bs = _pick(s, (512, 256, 128))

    def body(a_ref, b_ref, c_ref, o_ref):
        o_ref[...] = (a_ref[...] + b_ref[...]) + c_ref[...]

    row = pl.BlockSpec((bs, w), lambda i: (i, 0))
    return pl.pallas_call(
        body, name=name, grid=(s // bs,), in_specs=[row] * 3, out_specs=row,
        out_shape=jax.ShapeDtypeStruct((s, w), F32), compiler_params=_params("parallel"),
    )(a, b, c)


def _make_dilated(name):
    scale, max_dist = HEAD_DIM ** -0.5, BLOCK

    def upb_of(dil):
        return BAND_UNITS_PER_STEP if dil == 1 else 1

    def forward(q, k, v):
        os_, ls_ = [], []
        for n, (_, dil) in enumerate(DIL_PATTERNS):
            o, l = _band_fwd(q, k, v, None, scale, max_dist, upb_of(dil), dil, "%s_b%d_fwd" % (name, n))
            os_.append(o)
            ls_.append(l)
        return _merge3(os_, ls_, name + "_merge")

    @jax.custom_vjp
    def dilated(q, k, v):
        return forward(q, k, v)[0]

    def fwd(q, k, v):
        out, lse = forward(q, k, v)
        return out, (q, k, v, out, lse)

    def bwd(res, do):
        q, k, v, out, lse = res
        dqs, dks, dvs = [], [], []
        for n, (_, dil) in enumerate(DIL_PATTERNS):
            args = (q, k, v, out, lse, do)
            dqs.append(_band_dq(*args, None, scale, max_dist, upb_of(dil), dil, "%s_b%d_dq" % (name, n)))
            dk, dv = _band_dkv(*args, scale, max_dist, upb_of(dil), dil, "%s_b%d_dkv" % (name, n))
            dks.append(dk)
            dvs.append(dv)
        return (_add3(*dqs, name + "_dq_sum"), _add3(*dks, name + "_dk_sum"), _add3(*dvs, name + "_dv_sum"))

    dilated.defvjp(fwd, bwd)
    return dilated


def _make_norm_linear(name):
    @jax.custom_vjp
    def op(x, g, wslot, w):
        return _mm(_rms_fwd(x, g, name + "_norm"), w, "nn", name + "_mm")

    def fwd(x, g, wslot, w):
        h = _rms_fwd(x, g, name + "_norm")
        return _mm(h, w, "nn", name + "_mm"), (x, g, h, w)

    def bwd(res, dz):
        x, g, h, w = res
        dh = _mm(dz, w, "nt", name + "_dh")
        dw = _mm(h, dz, "tn", name + "_dw", out_dtype=GRAD_WIRE_DTYPE)
        dx, dg = _rms_bwd(x, g, dh, name + "_norm_bwd")
        return dx, dg, dw, None

    op.defvjp(fwd, bwd)
    return op


def _make_linear_res(name):
    @jax.custom_vjp
    def op(a, wslot, w, res):
        return _mm(a, w, "nn", name + "_mm", res=res)

    def fwd(a, wslot, w, res):
        return _mm(a, w, "nn", name + "_mm", res=res), (a, w)

    def bwd(saved, dout):
        a, w = saved
        da = _mm(dout, w, "nt", name + "_da")
        dw = _mm(a, dout, "tn", name + "_dw", out_dtype=GRAD_WIRE_DTYPE)
        return da, dw, None, dout

    op.defvjp(fwd, bwd)
    return op


def _swiglu_fwd(gu, name):
    s, w2 = gu.shape
    hdim = w2 // 2
    bs = _pick(s, (256, 128))

    def body(g_ref, u_ref, a_ref):
        g = g_ref[...]
        a_ref[...] = (g / (1.0 + jnp.exp(-g)) * u_ref[...]).astype(BF16)

    return pl.pallas_call(
        body, name=name, grid=(s // bs,),
        in_specs=[pl.BlockSpec((bs, hdim), lambda i: (i, 0)), pl.BlockSpec((bs, hdim), lambda i: (i, 1))],
        out_specs=pl.BlockSpec((bs, hdim), lambda i: (i, 0)), out_shape=jax.ShapeDtypeStruct((s, hdim), BF16),
        compiler_params=_params("parallel"),
    )(gu, gu)


def _swiglu_bwd_joint(gu, da, name):
    s, w2 = gu.shape
    hdim = w2 // 2
    bs = _pick(s, (256, 128))

    def body(g_ref, u_ref, da_ref, dgu_ref):
        g, u, d = g_ref[...], u_ref[...], da_ref[...]
        sig = 1.0 / (1.0 + jnp.exp(-g))
        dgu_ref[:, :hdim] = (d * u * (sig * (1.0 + g * (1.0 - sig)))).astype(BF16)
        dgu_ref[:, hdim:] = (d * (g * sig)).astype(BF16)

    lo = pl.BlockSpec((bs, hdim), lambda i: (i, 0))
    hi = pl.BlockSpec((bs, hdim), lambda i: (i, 1))
    return pl.pallas_call(
        body, name=name, grid=(s // bs,), in_specs=[lo, hi, lo], out_specs=pl.BlockSpec((bs, w2), lambda i: (i, 0)),
        out_shape=jax.ShapeDtypeStruct((s, w2), BF16), compiler_params=_params("parallel"),
    )(gu, gu, da)


def _make_ffn(name):
    def forward(x, g, wgu, wd):
        h = _rms_fwd(x, g, name + "_norm")
        gu = _mm(h, wgu, "nn", name + "_gu")
        a = _swiglu_fwd(gu, name + "_act")
        return _mm(a, wd, "nn", name + "_down", res=x), (x, g, h, gu, a, wgu, wd)

    @jax.custom_vjp
    def op(x, g, wgu_slot, wd_slot, wgu, wd):
        return forward(x, g, wgu, wd)[0]

    def fwd(x, g, wgu_slot, wd_slot, wgu, wd):
        return forward(x, g, wgu, wd)

    def bwd(saved, dout):
        x, g, h, gu, a, wgu, wd = saved
        da = _mm(dout, wd, "nt", name + "_da")
        dwd = _mm(a, dout, "tn", name + "_dwd", out_dtype=GRAD_WIRE_DTYPE)
        dgu = _swiglu_bwd_joint(gu, da, name + "_act_bwd")
        dwgu = _mm(h, dgu, "tn", name + "_dwgu", out_dtype=GRAD_WIRE_DTYPE)
        dh = _mm(dgu, wgu, "nt", name + "_dh")
        dx, dg = _rms_bwd(x, g, dh, name + "_norm_bwd", dres=dout)
        return dx, dg, dwgu, dwd, None, None

    op.defvjp(fwd, bwd)
    return op


def _make_final_loss(name):
    def run(x, g, tgt):
        s, d = x.shape
        bs = _pick(s, (512, 256, 128))

        def body(x_ref, g_ref, t_ref, loss_ref, dx_ref, dg_ref):
            i = pl.program_id(0)
            xv = x_ref[...]
            gv = g_ref[...]
            r = lax.rsqrt(jnp.mean(xv * xv, axis=-1, keepdims=True) + NORM_EPS)
            xh = xv * r
            e = xh * gv - t_ref[...]
            dy = e * (1.0 / d)
            dxh = dy * gv
            dx_ref[...] = r * (dxh - xh * jnp.mean(dxh * xh, axis=-1, keepdims=True))
            part = 0.5 * jnp.sum(jnp.sum(e * e, axis=-1, keepdims=True) * (1.0 / d), axis=0, keepdims=True)

            @pl.when(i == 0)
            def _():
                loss_ref[...] = jnp.zeros_like(loss_ref)
                dg_ref[...] = jnp.zeros_like(dg_ref)

            loss_ref[...] += jnp.broadcast_to(part, loss_ref.shape)
            dg_ref[...] += jnp.sum(dy * xh, axis=0, keepdims=True)

        row = pl.BlockSpec((bs, d), lambda i: (i, 0))
        vec = pl.BlockSpec((1, d), lambda i: (0, 0))
        loss, dx, dg = pl.pallas_call(
            body, name=name, grid=(s // bs,), in_specs=[row, vec, row],
            out_specs=(pl.BlockSpec((8, LANES), lambda i: (0, 0)), row, vec),
            out_shape=(jax.ShapeDtypeStruct((8, LANES), F32), jax.ShapeDtypeStruct((s, d), F32),
                       jax.ShapeDtypeStruct((1, d), F32)),
            compiler_params=_params("arbitrary"),
        )(x, g.reshape(1, d), tgt)
        return loss[0, 0], dx, dg.reshape(d)

    @jax.custom_vjp
    def op(x, g, tgt):
        return run(x, g, tgt)[0]

    def fwd(x, g, tgt):
        loss, dx, dg = run(x, g, tgt)
        return loss, (dx, dg)

    def bwd(saved, ct):
        dx, dg = saved
        return dx * ct, dg * ct, None

    op.defvjp(fwd, bwd)
    return op


def _model_loss(diff, consts):
    x = diff["x"]
    w = consts["w"]
    slot = diff["slots"]
    vec = diff["vec"]
    tab64, tab_mla = consts["tab64"], consts["tab_mla"]
    mem = consts["mem"]
    s = x.shape[0]

    rope64 = lambda t, nm: _make_rope(HEAD_DIM // 2, nm)(t, *tab64)
    rope_mla = lambda t, nm: _make_rope(MLA_ROPE_DIM // 2, nm)(t, *tab_mla)

    def nl(nm, inp, gain, wname):
        return _make_norm_linear(nm)(inp, gain, slot[wname], w[wname])

    def cross(layer, xin):
        p = "l%d_" % layer
        q = nl(p + "xq", xin, vec[p + "x_norm"], p + "w_xq")
        kv = nl(p + "xkv", mem, vec[p + "mem_norm"], p + "w_xkv")
        half = X_HEADS * X_HEAD_DIM
        cfg = AttnCfg("full", X_HEAD_DIM ** -0.5, 1, _pick(s, (512, 256, 128)), kv.shape[0], 4)
        o = _make_attention(cfg, p + "xattn")(q, kv[:, :half], kv[:, half:])
        return _make_linear_res(p + "xo")(o, slot[p + "w_xo"], w[p + "w_xo"], xin)

    def ffn(layer, xin):
        p = "l%d_" % layer
        return _make_ffn(p + "ffn")(xin, vec[p + "ffn_norm"], slot[p + "w_gu"], slot[p + "w_down"], w[p + "w_gu"],
                                    w[p + "w_down"])

    z = nl("l0_in", x, vec["l0_mix_norm"], "l0_w_in")
    qa = rope64(z[:, :A_Q], "l0_rope_qa")
    ka = rope64(z[:, A_Q:A_Q + A_KV], "l0_rope_ka")
    va = z[:, A_Q + A_KV:A_Q + 2 * A_KV]
    rep = SWA_HEADS // SWA_KV_HEADS
    expand = lambda t: jnp.broadcast_to(t.reshape(s, SWA_KV_HEADS, 1, HEAD_DIM),
                                        (s, SWA_KV_HEADS, rep, HEAD_DIM)).reshape(s, A_Q)
    swa = _make_band_attention(HEAD_DIM ** -0.5, SWA_WINDOW - 1, BAND_UNITS_PER_STEP, "l0_swa")
    oa = swa(qa, expand(ka), expand(va), vec["l0_sinks"])

    c0 = A_Q + 2 * A_KV
    cq = z[:, c0:c0 + MLA_Q_RANK]
    ckv = z[:, c0 + MLA_Q_RANK:c0 + MLA_Q_RANK + MLA_KV_RANK]
    kr = z[:, c0 + MLA_Q_RANK + MLA_KV_RANK:EVEN_IN]
    qb = nl("l0_uq", cq, vec["l0_q_norm"], "l0_w_uq").reshape(s, MLA_HEADS, MLA_NOPE_DIM + MLA_ROPE_DIM)
    qfull = jnp.pad(qb, ((0, 0), (0, 0), (0, LANES - MLA_NOPE_DIM - MLA_ROPE_DIM))).reshape(s, MLA_HEADS * LANES)
    qfull = rope_mla(qfull, "l0_rope_q")
    kvb = nl("l0_ukv", ckv, vec["l0_kv_norm"], "l0_w_ukv")
    kvb3 = kvb.reshape(s, MLA_HEADS, LANES)
    kfull = jnp.concatenate(
        [kvb3[:, :, :MLA_NOPE_DIM], jnp.broadcast_to(kr[:, None, :], (s, MLA_HEADS, MLA_ROPE_DIM)),
         jnp.zeros((s, MLA_HEADS, LANES - MLA_NOPE_DIM - MLA_ROPE_DIM), F32)], axis=-1).reshape(s, MLA_HEADS * LANES)
    kfull = rope_mla(kfull, "l0_rope_k")
    mla = _make_causal_attention((MLA_NOPE_DIM + MLA_ROPE_DIM) ** -0.5, _pick(s, (1024, 512, 256, 128)), "l0_mla")
    ob = mla(qfull, kfull, kvb).reshape(s, MLA_HEADS, LANES)[:, :, MLA_NOPE_DIM:]
    o = jnp.concatenate([oa, ob.reshape(s, MLA_HEADS * HEAD_DIM)], axis=-1)
    x = _make_linear_res("l0_out")(o, slot["l0_w_out"], w["l0_w_out"], x)
    x = cross(0, x)
    x = ffn(0, x)

    qkv = nl("l1_qkv", x, vec["l1_mix_norm"], "l1_w_qkv")
    q = rope64(qkv[:, :D_MODEL], "l1_rope_q")
    k = rope64(qkv[:, D_MODEL:2 * D_MODEL], "l1_rope_k")
    o = _make_dilated("l1_dil")(q, k, qkv[:, 2 * D_MODEL:])
    x = _make_linear_res("l1_out")(o, slot["l1_w_out"], w["l1_w_out"], x)
    x = cross(1, x)
    x = ffn(1, x)

    return _make_final_loss("final_loss")(x, vec["final_norm"], consts["target"])


MESH_IDS = pl.DeviceIdType.MESH
HBM_SPEC = pl.BlockSpec(memory_space=pltpu.HBM)


def _my_place():
    return lax.axis_index("x"), lax.axis_index("y"), lax.axis_index("c")


def _flip(v, bit):
    return 1 - v if bit else v


def _all_gather_rows(shard):
    r, c_ = shard.shape

    def body(x_ref, out_ref, send_sems, recv_sems, local_sem):
        x, y, c = _my_place()
        me, sibling = (x, y, c), (x, y, 1 - c)
        chips = [(1 - x, y), (x, 1 - y), (1 - x, 1 - y)]

        def slot(px, py, pc):
            return out_ref.at[4 * px + 2 * py + pc]

        def copy(k, block, to, src=None):
            return pltpu.make_async_remote_copy(
                src_ref=slot(*block) if src is None else src, dst_ref=slot(*block), send_sem=send_sems.at[k],
                recv_sem=recv_sems.at[k], device_id=to, device_id_type=MESH_IDS)

        mine = pltpu.make_async_copy(x_ref, slot(*me), local_sem)
        mine.start()
        first = [copy(0, me, sibling, src=x_ref)]
        first += [copy(1 + j, me, (*chip, c), src=x_ref) for j, chip in enumerate(chips)]
        for cp in first:
            cp.start()
        passed = [copy(4 + j, (*chip, c), sibling) for j, chip in enumerate(chips)]
        for j, chip in enumerate(chips):
            copy(1 + j, (*chip, c), me).wait_recv()
            passed[j].start()
        copy(0, sibling, me).wait_recv()
        for j, chip in enumerate(chips):
            copy(4 + j, (*chip, 1 - c), me).wait_recv()
        for cp in first + passed:
            cp.wait_send()
        mine.wait()

    return pl.pallas_call(
        body, name="weights_all_gather", out_shape=jax.ShapeDtypeStruct((N_DEV, r, c_), shard.dtype),
        in_specs=[HBM_SPEC], out_specs=HBM_SPEC,
        scratch_shapes=[pltpu.SemaphoreType.DMA((7,)), pltpu.SemaphoreType.DMA((7,)), pltpu.SemaphoreType.DMA],
    )(shard)


N_CHIPS = 4


def _exchange_with_sibling(slabs):
    _, nq, r, c_ = slabs.shape

    def body(p_ref, out_ref, send_sem, recv_sem):
        x, y, c = _my_place()
        cp = pltpu.make_async_remote_copy(
            src_ref=p_ref.at[1 - c], dst_ref=out_ref, send_sem=send_sem, recv_sem=recv_sem,
            device_id=(x, y, 1 - c), device_id_type=MESH_IDS)
        cp.start()
        cp.wait_recv()
        cp.wait_send()

    return pl.pallas_call(
        body, name="grad_exchange_sibling", out_shape=jax.ShapeDtypeStruct((nq, r, c_), slabs.dtype),
        in_specs=[HBM_SPEC], out_specs=HBM_SPEC,
        scratch_shapes=[pltpu.SemaphoreType.DMA, pltpu.SemaphoreType.DMA],
    )(slabs)


def _add_pairs(a, b):
    nq, r, c_ = a.shape
    br = _pick(r, (256, 128, 64, 32, 16, 8))

    def body(a_ref, b_ref, o_ref):
        o_ref[...] = (a_ref[...].astype(F32) + b_ref[...].astype(F32)).astype(o_ref.dtype)

    blk = pl.BlockSpec((1, br, c_), lambda q, i: (q, i, 0))
    return pl.pallas_call(
        body, name="grad_chip_sum", grid=(nq, r // br), in_specs=[blk, blk], out_specs=blk,
        out_shape=jax.ShapeDtypeStruct(a.shape, a.dtype), compiler_params=_params("parallel", "parallel"),
    )(a, b)


def _exchange_between_chips(slabs):
    nq, r, c_ = slabs.shape

    def body(t_ref, out_ref, send_sems, recv_sems, local_sem):
        x, y, c = _my_place()
        myq = 2 * x + y
        local = pltpu.make_async_copy(t_ref.at[myq], out_ref.at[myq], local_sem)
        local.start()
        sends, recvs = [], []
        for k in range(1, N_CHIPS):
            px, py = _flip(x, k & 2), _flip(y, k & 1)
            peer = 2 * px + py
            sends.append(pltpu.make_async_remote_copy(
                src_ref=t_ref.at[peer], dst_ref=out_ref.at[myq], send_sem=send_sems.at[k - 1],
                recv_sem=recv_sems.at[k - 1], device_id=(px, py, c), device_id_type=MESH_IDS))
            recvs.append(pltpu.make_async_remote_copy(
                src_ref=t_ref.at[myq], dst_ref=out_ref.at[peer], send_sem=send_sems.at[k - 1],
                recv_sem=recv_sems.at[k - 1], device_id=(px, py, c), device_id_type=MESH_IDS))
        for cp in sends:
            cp.start()
        for cp in recvs:
            cp.wait_recv()
        for cp in sends:
            cp.wait_send()
        local.wait()

    return pl.pallas_call(
        body, name="grad_exchange_chips", out_shape=jax.ShapeDtypeStruct(slabs.shape, slabs.dtype),
        in_specs=[HBM_SPEC], out_specs=HBM_SPEC,
        scratch_shapes=[pltpu.SemaphoreType.DMA((N_CHIPS - 1,)), pltpu.SemaphoreType.DMA((N_CHIPS - 1,)),
                        pltpu.SemaphoreType.DMA],
    )(slabs)


def _all_reduce_small(v):
    r, c_ = v.shape

    def body(v_ref, out_ref, buf, send_sems, recv_sems):
        x, y, c = _my_place()
        me = 4 * x + 2 * y + c
        buf[me] = v_ref[...]
        sends, recvs = [], []
        for k in range(1, N_DEV):
            px, py, pc = _flip(x, k & 4), _flip(y, k & 2), _flip(c, k & 1)
            peer = 4 * px + 2 * py + pc
            sends.append(pltpu.make_async_remote_copy(
                src_ref=v_ref, dst_ref=buf.at[me], send_sem=send_sems.at[k - 1], recv_sem=recv_sems.at[k - 1],
                device_id=(px, py, pc), device_id_type=MESH_IDS))
            recvs.append(pltpu.make_async_remote_copy(
                src_ref=v_ref, dst_ref=buf.at[peer], send_sem=send_sems.at[k - 1], recv_sem=recv_sems.at[k - 1],
                device_id=(px, py, pc), device_id_type=MESH_IDS))
        for cp in sends:
            cp.start()
        for cp in recvs:
            cp.wait_recv()
        for cp in sends:
            cp.wait_send()
        acc = buf[0]
        for d in range(1, N_DEV):
            acc = acc + buf[d]
        out_ref[...] = acc

    vm = pl.BlockSpec(memory_space=pltpu.VMEM)
    return pl.pallas_call(
        body, name="vector_grad_all_reduce", out_shape=jax.ShapeDtypeStruct((r, c_), F32), in_specs=[vm], out_specs=vm,
        scratch_shapes=[pltpu.VMEM((N_DEV, r, c_), F32), pltpu.SemaphoreType.DMA((7,)), pltpu.SemaphoreType.DMA((7,))],
    )(v)


def _adamw_math(w, g, m, v):
    m = ADAM_B1 * m + (1.0 - ADAM_B1) * g
    v = ADAM_B2 * v + (1.0 - ADAM_B2) * (g * g)
    m_hat = m / (1.0 - ADAM_B1 ** ADAM_STEP)
    v_hat = v / (1.0 - ADAM_B2 ** ADAM_STEP)
    delta = -ADAM_LR * (m_hat / (jnp.sqrt(v_hat) + ADAM_EPS) + ADAM_WD * w)
    return delta, m, v


def _sum_and_adamw(parts, w, m, v):
    nparts, r, c_ = parts.shape
    br = _pick(r, (256, 128, 64, 32, 16, 8))

    def body(p_ref, w_ref, m_ref, v_ref, g_ref, d_ref, nm_ref, nv_ref):
        g = p_ref[0].astype(F32)
        for d in range(1, nparts):
            g = g + p_ref[d].astype(F32)
        g_ref[...] = g
        d_ref[...], nm_ref[...], nv_ref[...] = _adamw_math(w_ref[...], g, m_ref[...], v_ref[...])

    row = pl.BlockSpec((br, c_), lambda i: (i, 0))
    return pl.pallas_call(
        body, name="grad_sum_adamw", grid=(r // br,),
        in_specs=[pl.BlockSpec((nparts, br, c_), lambda i: (0, i, 0)), row, row, row], out_specs=(row,) * 4,
        out_shape=(jax.ShapeDtypeStruct((r, c_), F32),) * 4, compiler_params=_params("parallel"),
    )(parts, w, m, v)


def _adamw_small(w, g, m, v):
    vm = pl.BlockSpec(memory_space=pltpu.VMEM)

    def body(w_ref, g_ref, m_ref, v_ref, d_ref, nm_ref, nv_ref):
        d_ref[...], nm_ref[...], nv_ref[...] = _adamw_math(w_ref[...], g_ref[...], m_ref[...], v_ref[...])

    return pl.pallas_call(
        body, name="vector_adamw", in_specs=[vm] * 4, out_specs=(vm,) * 3,
        out_shape=(jax.ShapeDtypeStruct(w.shape, F32),) * 3,
    )(w, g, m, v)


def _pad_rows(t, axis):
    extra = -t.shape[axis] % PART_ROW_ALIGN
    if extra == 0:
        return t
    widths = [(0, 0)] * t.ndim
    widths[axis] = (0, extra)
    return jnp.pad(t, widths)


def _pack_local(named):
    rows = [_pad_rows(named[n].reshape(-1, PACK_COLS), 0) for n, _, _, _ in MATRICES]
    rows.append(jnp.zeros((MAT_ROWS - MAT_ROWS_USED, PACK_COLS), rows[0].dtype))
    return jnp.concatenate(rows, axis=0)


def _unpack_local(packed):
    out, r0 = {}, 0
    for n, kind, k, nn in MATRICES:
        nr = k * nn // N_DEV // PACK_COLS
        shape = (k, nn // N_DEV) if kind == "c" else (k // N_DEV, nn)
        out[n] = packed[r0:r0 + nr].reshape(shape)
        r0 += _part_rows(k, nn)
    return out


def _unpack_gathered(g):
    out, r0 = {}, 0
    for n, kind, k, nn in MATRICES:
        nr = k * nn // N_DEV // PACK_COLS
        blk = g[:, r0:r0 + nr]
        if kind == "c":
            out[n] = blk.reshape(N_DEV, k, nn // N_DEV).transpose(1, 0, 2).reshape(k, nn)
        else:
            out[n] = blk.reshape(k, nn)
        r0 += _part_rows(k, nn)
    return out


def _pack_full_grads(grads):
    rows = []
    for n, kind, k, nn in MATRICES:
        gmat = grads[n]
        if kind == "c":
            gmat = gmat.reshape(k, N_CHIPS, 2, nn // N_DEV).transpose(2, 1, 0, 3)
        else:
            gmat = gmat.reshape(N_CHIPS, 2, k // N_DEV, nn).transpose(1, 0, 2, 3)
        rows.append(_pad_rows(gmat.reshape(2, N_CHIPS, -1, PACK_COLS), 2))
    rows.append(jnp.zeros((2, N_CHIPS, MAT_ROWS - MAT_ROWS_USED, PACK_COLS), rows[0].dtype))
    return jnp.concatenate(rows, axis=2)


def _pack_vectors(named):
    rows = [jnp.pad(named[n].astype(F32), (0, PACK_COLS - d)) for n, d in VECTORS]
    rows += [jnp.zeros((PACK_COLS,), F32)] * (VEC_ROWS - len(VECTORS))
    return jnp.stack(rows, axis=0)


def _unpack_vectors(packed):
    return {n: packed[i, :d] for i, (n, d) in enumerate(VECTORS)}


def _step(inputs):
    x = inputs["x"][0]
    mem = inputs["mem"][0]
    positions = inputs["positions"][0]
    target = inputs["loss_target"][0]

    local_w = _pack_local({n: inputs[n] for n, _, _, _ in MATRICES})
    gathered = _all_gather_rows(local_w.astype(BF16))
    wfull = _unpack_gathered(gathered)
    vec = {n: inputs[n] for n, _ in VECTORS}

    loss_part, grad_x, gfull, gvec = _local_grads(wfull, vec, x, mem, positions, target)
    loss = lax.psum(loss_part, ("x", "y", "c"))

    slabs = _pack_full_grads(gfull)
    from_sibling = _exchange_with_sibling(slabs)
    mine = lax.dynamic_index_in_dim(slabs, lax.axis_index("c"), axis=0, keepdims=False)
    parts = _exchange_between_chips(_add_pairs(mine, from_sibling))
    local_m = _pack_local({n: inputs["m_" + n] for n, _, _, _ in MATRICES})
    local_v = _pack_local({n: inputs["v_" + n] for n, _, _, _ in MATRICES})
    g_pk, d_pk, m_pk, v_pk = _sum_and_adamw(parts, local_w, local_m, local_v)
    g_mat, d_mat, m_mat, v_mat = (_unpack_local(t) for t in (g_pk, d_pk, m_pk, v_pk))

    g_vec_pk = _all_reduce_small(_pack_vectors(gvec))
    d_vec_pk, m_vec_pk, v_vec_pk = _adamw_small(
        _pack_vectors(vec), g_vec_pk, _pack_vectors({n: inputs["m_" + n] for n, _ in VECTORS}),
        _pack_vectors({n: inputs["v_" + n] for n, _ in VECTORS}))
    g_vec, d_vec, m_vec, v_vec = (_unpack_vectors(t) for t in (g_vec_pk, d_vec_pk, m_vec_pk, v_vec_pk))

    def pick(mats, vecs, n):
        return mats[n] if n in mats else vecs[n]

    outs = [loss, grad_x[None]]
    for mats, vecs in ((g_mat, g_vec), (d_mat, d_vec), (m_mat, m_vec), (v_mat, v_vec)):
        outs += [pick(mats, vecs, n) for n in WEIGHT_ORDER]
    return tuple(outs)


def _local_grads(wfull, vec, x, mem, positions, target):
    w = {}
    for n, _, _, _ in MATRICES:
        if n.endswith("w_gate") or n.endswith("w_up"):
            continue
        w[n] = wfull[n]
    w["l0_w_in"] = jnp.pad(wfull["l0_w_in"], ((0, 0), (0, EVEN_IN_PAD - EVEN_IN)))
    for layer in (0, 1):
        p = "l%d_" % layer
        w[p + "w_gu"] = jnp.concatenate([wfull[p + "w_gate"], wfull[p + "w_up"]], axis=1)
    slots = {n: jnp.zeros(t.shape, GRAD_WIRE_DTYPE) for n, t in w.items()}

    tab64 = _rope_tables(positions, HEAD_DIM, 0, HEAD_DIM)
    tab_mla = _rope_tables(positions, MLA_ROPE_DIM, MLA_NOPE_DIM, LANES)
    diff = {"x": x, "slots": slots, "vec": vec}
    consts = {"w": w, "mem": mem, "tab64": tab64, "tab_mla": tab_mla, "target": target}
    loss_part, grads = jax.value_and_grad(_model_loss)(diff, consts)

    gfull = dict(grads["slots"])
    gfull["l0_w_in"] = gfull["l0_w_in"][:, :EVEN_IN]
    for layer in (0, 1):
        p = "l%d_" % layer
        gu = gfull.pop(p + "w_gu")
        gfull[p + "w_gate"], gfull[p + "w_up"] = gu[:, :FFN_HIDDEN], gu[:, FFN_HIDDEN:]
    return loss_part, grads["x"], gfull, grads["vec"]


_INPUT_NAMES = (("x", "mem", "positions") + WEIGHT_ORDER + ("loss_target",)
                + tuple("m_" + n for n in WEIGHT_ORDER) + tuple("v_" + n for n in WEIGHT_ORDER))


def kernel(*args):
    assert len(args) == len(_INPUT_NAMES)
    return _step(dict(zip(_INPUT_NAMES, args)))
```

```python
import functools
import math

import numpy as np
import jax
import jax.numpy as jnp
from jax import lax
from jax.experimental import pallas as pl
from jax.experimental.pallas import tpu as pltpu

F32 = jnp.float32
BF16 = jnp.bfloat16

LANES = 128
VMEM_LIMIT_BYTES = 56 * 1024 * 1024
MM_VMEM_BUDGET = 40 * 1024 * 1024
MM_MIN_FLOP_PER_STEP = 1e9
BAND_UNITS_PER_STEP = 4
CAUSAL_ROW_CHAIN = 128
BAND_CHAINS_PER_BATCH = 4

D_MODEL = 1024
HEAD_DIM = 64
ROPE_THETA = 10000.0
NORM_EPS = 1e-6
BLOCK = 128
SWA_HEADS = 8
SWA_KV_HEADS = 2
SWA_WINDOW = 128
MLA_HEADS = 8
MLA_Q_RANK = 384
MLA_KV_RANK = 256
MLA_NOPE_DIM = 64
MLA_ROPE_DIM = 32
A_Q = SWA_HEADS * HEAD_DIM
A_KV = SWA_KV_HEADS * HEAD_DIM
EVEN_IN = A_Q + 2 * A_KV + MLA_Q_RANK + MLA_KV_RANK + MLA_ROPE_DIM
EVEN_IN_PAD = 1536
DIL_PATTERNS = ((128, 1), (512, 4), (2048, 16))
X_HEADS = 4
X_HEAD_DIM = 128
FFN_HIDDEN = 2816

ADAM_LR = 0.001
ADAM_B1 = 0.9
ADAM_B2 = 0.999
ADAM_EPS = 1e-08
ADAM_WD = 0.01
ADAM_STEP = 10

N_DEV = 8
GRAD_WIRE_DTYPE = BF16
NEG_MASK = -1e30
NEG_INIT = -1e20

MATRICES = (
    ("l0_w_in", "c", 1024, 1440), ("l0_w_uq", "c", 384, 768), ("l0_w_ukv", "c", 256, 1024),
    ("l0_w_out", "r", 1024, 1024), ("l0_w_xq", "r", 1024, 512), ("l0_w_xkv", "r", 1024, 1024),
    ("l0_w_xo", "c", 512, 1024), ("l0_w_gate", "c", 1024, 2816), ("l0_w_up", "c", 1024, 2816),
    ("l0_w_down", "r", 2816, 1024),
    ("l1_w_qkv", "c", 1024, 3072), ("l1_w_out", "r", 1024, 1024), ("l1_w_xq", "r", 1024, 512),
    ("l1_w_xkv", "r", 1024, 1024), ("l1_w_xo", "c", 512, 1024), ("l1_w_gate", "c", 1024, 2816),
    ("l1_w_up", "c", 1024, 2816), ("l1_w_down", "r", 2816, 1024),
)
VECTORS = (
    ("l0_mix_norm", 1024), ("l0_sinks", 8), ("l0_q_norm", 384), ("l0_kv_norm", 256), ("l0_x_norm", 1024),
    ("l0_mem_norm", 1024), ("l0_ffn_norm", 1024), ("l1_mix_norm", 1024), ("l1_x_norm", 1024),
    ("l1_mem_norm", 1024), ("l1_ffn_norm", 1024), ("final_norm", 1024),
)
WEIGHT_ORDER = (
    "l0_mix_norm", "l0_w_in", "l0_sinks", "l0_q_norm", "l0_w_uq", "l0_kv_norm", "l0_w_ukv", "l0_w_out", "l0_x_norm",
    "l0_mem_norm", "l0_w_xq", "l0_w_xkv", "l0_w_xo", "l0_ffn_norm", "l0_w_gate", "l0_w_up", "l0_w_down",
    "l1_mix_norm", "l1_w_qkv", "l1_w_out", "l1_x_norm", "l1_mem_norm", "l1_w_xq", "l1_w_xkv", "l1_w_xo",
    "l1_ffn_norm", "l1_w_gate", "l1_w_up", "l1_w_down", "final_norm",
)
PACK_COLS = 1024
PART_ROW_ALIGN = 16


def _part_rows(k, n):
    return -(-(k * n // N_DEV // PACK_COLS) // PART_ROW_ALIGN) * PART_ROW_ALIGN


MAT_ROWS_USED = sum(_part_rows(k, n) for _, _, k, n in MATRICES)
MAT_ROWS = -(-MAT_ROWS_USED // 256) * 256
VEC_ROWS = 16


def _pick(n, cands):
    for c in cands:
        if n % c == 0:
            return c
    return n


def _params(*sem):
    return pltpu.CompilerParams(dimension_semantics=sem, vmem_limit_bytes=VMEM_LIMIT_BYTES)


_DIMS = {"nn": (((1,), (0,)), ((), ())), "nt": (((1,), (1,)), ((), ())), "tn": (((0,), (0,)), ((), ()))}


def _div128(n, cap):
    d = (min(n, cap) // LANES) * LANES
    while d >= LANES:
        if n % d == 0:
            return d
        d -= LANES
    return n


def _mm_vmem_bytes(bm, bn, bk, nk, sa, sb, so, has_res):
    est = 2 * (bm * bk * sa + bk * bn * sb + bm * bn * so) + bm * bn * 4
    est += bm * bn * 4 if nk > 1 else 0
    est += 2 * bm * bn * 4 if has_res else 0
    est += bm * bk * 2 if sa == 4 else 0
    est += bk * bn * 2 if sb == 4 else 0
    return est


def _mm_tiles(m, n, k, sa, sb, so, has_res, mode):
    bn = _div128(n, 1536)
    kcap = 2048 if mode == "tn" else k
    for bm_cap in ((1024, 2048) if mode == "tn" else (512, 1024, 2048)):
        bm = _div128(m, bm_cap)
        bk = (min(k, kcap) // LANES) * LANES
        while bk > LANES and (k % bk or _mm_vmem_bytes(bm, bn, bk, k // bk, sa, sb, so, has_res) > MM_VMEM_BUDGET):
            bk -= LANES
        if 2 * bm * bn * bk >= MM_MIN_FLOP_PER_STEP or bm == m:
            break
    return bm, bn, bk


def _mm(a, b, mode, name, out_dtype=F32, res=None):
    if mode == "nn":
        (m, k), (k2, n) = a.shape, b.shape
    elif mode == "nt":
        (m, k), (n, k2) = a.shape, b.shape
    else:
        (k, m), (k2, n) = a.shape, b.shape
    assert k == k2, (name, a.shape, b.shape)
    has_res = res is not None
    bm, bn, bk = _mm_tiles(m, n, k, a.dtype.itemsize, b.dtype.itemsize, jnp.dtype(out_dtype).itemsize, has_res, mode)
    nk = k // bk
    dims = _DIMS[mode]
    a_spec = pl.BlockSpec((bk, bm), lambda i, j, kk: (kk, i)) if mode == "tn" else pl.BlockSpec((bm, bk), lambda i, j, kk: (i, kk))
    b_spec = pl.BlockSpec((bn, bk), lambda i, j, kk: (j, kk)) if mode == "nt" else pl.BlockSpec((bk, bn), lambda i, j, kk: (kk, j))
    o_spec = pl.BlockSpec((bm, bn), lambda i, j, kk: (i, j))

    def body(*refs):
        a_ref, b_ref = refs[0], refs[1]
        r_ref = refs[2] if has_res else None
        o_ref = refs[3] if has_res else refs[2]
        part = lax.dot_general(a_ref[...].astype(BF16), b_ref[...].astype(BF16), dims, preferred_element_type=F32)
        if nk == 1:
            o_ref[...] = (part + r_ref[...] if has_res else part).astype(out_dtype)
            return
        acc = refs[-1]
        kk = pl.program_id(2)

        @pl.when(kk == 0)
        def _():
            acc[...] = part

        @pl.when(jnp.logical_and(kk > 0, kk < nk - 1))
        def _():
            acc[...] += part

        @pl.when(kk == nk - 1)
        def _():
            r = acc[...] + part
            if has_res:
                r = r + r_ref[...]
            o_ref[...] = r.astype(out_dtype)

    args = (a, b, res) if has_res else (a, b)
    in_specs = [a_spec, b_spec] + ([o_spec] if has_res else [])
    return pl.pallas_call(
        body, name=name, grid=(m // bm, n // bn, nk), in_specs=in_specs, out_specs=o_spec,
        out_shape=jax.ShapeDtypeStruct((m, n), out_dtype),
        scratch_shapes=[pltpu.VMEM((bm, bn), F32)] if nk > 1 else [],
        compiler_params=_params("parallel", "parallel", "arbitrary"),
    )(*args)


def _rms_fwd(x, g, name, out_dtype=BF16):
    s, d = x.shape
    bs = _pick(s, (512, 256, 128))

    def body(x_ref, g_ref, o_ref):
        xv = x_ref[...]
        r = lax.rsqrt(jnp.mean(xv * xv, axis=-1, keepdims=True) + NORM_EPS)
        o_ref[...] = ((xv * r) * g_ref[...]).astype(out_dtype)

    return pl.pallas_call(
        body, name=name, grid=(s // bs,),
        in_specs=[pl.BlockSpec((bs, d), lambda i: (i, 0)), pl.BlockSpec((1, d), lambda i: (0, 0))],
        out_specs=pl.BlockSpec((bs, d), lambda i: (i, 0)), out_shape=jax.ShapeDtypeStruct((s, d), out_dtype),
        compiler_params=_params("parallel"),
    )(x, g.reshape(1, d))


def _rms_bwd(x, g, dy, name, dres=None):
    s, d = x.shape
    bs = _pick(s, (512, 256, 128))
    has_res = dres is not None

    def body(*refs):
        if has_res:
            x_ref, g_ref, dy_ref, r_ref, dx_ref, dg_ref = refs
        else:
            x_ref, g_ref, dy_ref, dx_ref, dg_ref = refs
        i = pl.program_id(0)
        xv = x_ref[...]
        dy = dy_ref[...]
        r = lax.rsqrt(jnp.mean(xv * xv, axis=-1, keepdims=True) + NORM_EPS)
        xh = xv * r
        dxh = dy * g_ref[...]
        dx = r * (dxh - xh * jnp.mean(dxh * xh, axis=-1, keepdims=True))
        if has_res:
            dx = dx + r_ref[...]
        dx_ref[...] = dx

        @pl.when(i == 0)
        def _():
            dg_ref[...] = jnp.zeros_like(dg_ref)

        dg_ref[...] += jnp.sum(dy * xh, axis=0, keepdims=True)

    row = pl.BlockSpec((bs, d), lambda i: (i, 0))
    vec = pl.BlockSpec((1, d), lambda i: (0, 0))
    args = (x, g.reshape(1, d), dy) + ((dres,) if has_res else ())
    dx, dg = pl.pallas_call(
        body, name=name, grid=(s // bs,), in_specs=[row, vec, row] + ([row] if has_res else []),
        out_specs=(row, vec), out_shape=(jax.ShapeDtypeStruct((s, d), F32), jax.ShapeDtypeStruct((1, d), F32)),
        compiler_params=_params("arbitrary"),
    )(*args)
    return dx, dg.reshape(d)


def _rope_tables(positions, dh, offset, period):
    role = np.zeros(LANES, np.int32)
    for base in range(0, LANES, period):
        role[base + offset:base + offset + dh // 2] = 1
        role[base + offset + dh // 2:base + offset + dh] = 2
    inv_freq = ROPE_THETA ** (-jnp.arange(0, dh, 2, dtype=F32) / dh)
    one_period = jnp.concatenate([jnp.zeros((offset,), F32), inv_freq, inv_freq,
                                  jnp.zeros((period - offset - dh,), F32)])
    ang = positions.astype(F32)[:, None] * jnp.tile(one_period, LANES // period)[None, :]
    c, s = jnp.cos(ang), jnp.sin(ang)
    role = role[None, :]
    a = jnp.where(role == 0, 1.0, c).astype(F32)
    bm = jnp.where(role == 2, s, 0.0).astype(F32)
    bp = jnp.where(role == 1, -s, 0.0).astype(F32)
    return a, bm, bp


def _rope_apply(x, tabs, half, transpose, name):
    s, w = x.shape
    bs = _pick(s, (512, 256, 128))
    nc = w // LANES
    a, bm, bp = tabs

    def body(x_ref, a_ref, bm_ref, bp_ref, o_ref):
        av, bmv, bpv = a_ref[...], bm_ref[...], bp_ref[...]
        for c in range(nc):
            sl = slice(c * LANES, (c + 1) * LANES)
            xv = x_ref[:, sl]
            if transpose:
                o_ref[:, sl] = xv * av + pltpu.roll(xv * bmv, LANES - half, 1) + pltpu.roll(xv * bpv, half, 1)
            else:
                o_ref[:, sl] = xv * av + pltpu.roll(xv, half, 1) * bmv + pltpu.roll(xv, LANES - half, 1) * bpv

    row = pl.BlockSpec((bs, w), lambda i: (i, 0))
    tab = pl.BlockSpec((bs, LANES), lambda i: (i, 0))
    return pl.pallas_call(
        body, name=name, grid=(s // bs,), in_specs=[row, tab, tab, tab], out_specs=row,
        out_shape=jax.ShapeDtypeStruct((s, w), F32), compiler_params=_params("parallel"),
    )(x, a, bm, bp)


def _make_rope(half, name):
    @jax.custom_vjp
    def rope(x, a, bm, bp):
        return _rope_apply(x, (a, bm, bp), half, False, name + "_fwd")

    def fwd(x, a, bm, bp):
        return rope(x, a, bm, bp), (a, bm, bp)

    def bwd(tabs, dy):
        return _rope_apply(dy, tabs, half, True, name + "_bwd"), None, None, None

    rope.defvjp(fwd, bwd)
    return rope


class AttnCfg:
    def __init__(self, mode, scale, hpb, bq, bk, upb, max_dist=0):
        self.mode, self.scale, self.hpb, self.bq, self.bk, self.upb, self.max_dist = mode, scale, hpb, bq, bk, upb, max_dist


def _kv_of_q(cfg, nq, nk):
    if cfg.mode == "causal":
        return nk, lambda i, j: (jnp.minimum(j, i), j <= i)
    if cfg.mode == "band":
        return 2, lambda i, j: (jnp.maximum(i - 1 + j, 0), i - 1 + j >= 0)
    return nk, lambda i, j: (j, j >= 0)


def _q_of_kv(cfg, nq, nk):
    if cfg.mode == "causal":
        return nq, lambda kb, j: (jnp.maximum(j, kb), j >= kb)
    if cfg.mode == "band":
        return 2, lambda kb, j: (jnp.minimum(kb + j, nq - 1), kb + j <= nq - 1)
    return nq, lambda kb, j: (j, j >= 0)


def _attn_mask(cfg, i, kb):
    if cfg.mode == "full":
        return None
    qpos = i * cfg.bq + lax.broadcasted_iota(jnp.int32, (cfg.bq, cfg.bk), 0)
    kpos = kb * cfg.bk + lax.broadcasted_iota(jnp.int32, (cfg.bq, cfg.bk), 1)
    dist = qpos - kpos
    if cfg.mode == "causal":
        return dist >= 0
    return (dist >= 0) & (dist <= cfg.max_dist)


def _lane_masks():
    lane = lax.broadcasted_iota(jnp.int32, (1, LANES), 1)
    lo = lane < HEAD_DIM
    return [lo, jnp.logical_not(lo)]


def _sel(mask, v):
    return jnp.where(mask, v, jnp.zeros_like(v))


_NT = (((1,), (1,)), ((), ()))
_NN = (((1,), (0,)), ((), ()))
_TN = (((0,), (0,)), ((), ()))
_BNT = (((2,), (2,)), ((0,), (0,)))
_BNN = (((2,), (1,)), ((0,), (0,)))


def _dot(a, b, dims):
    return lax.dot_general(a, b, dims, preferred_element_type=F32)


def _attn_fwd(q, k, v, sinkrow, cfg, name):
    sq, w = q.shape
    sk = k.shape[0]
    bq, bk, upb, hpb = cfg.bq, cfg.bk, cfg.upb, cfg.hpb
    nq, nk, nub = sq // bq, sk // bk, w // (LANES * upb)
    nj, sched = _kv_of_q(cfg, nq, nk)
    wb = LANES * upb
    has_sink = sinkrow is not None

    def body(*refs):
        if has_sink:
            q_ref, k_ref, v_ref, s_ref, o_ref, l_ref, m_sc, l_sc, acc_sc = refs
        else:
            q_ref, k_ref, v_ref, o_ref, l_ref, m_sc, l_sc, acc_sc = refs
        i, j = pl.program_id(1), pl.program_id(2)
        kb, active = sched(i, j)
        lms = _lane_masks()

        @pl.when(j == 0)
        def _():
            for u in range(upb):
                for a in range(hpb):
                    if has_sink:
                        srow = s_ref[:, u * LANES:(u + 1) * LANES]
                        sk_a = jnp.max(jnp.where(lms[a], srow, -jnp.inf), axis=-1, keepdims=True)
                        m_sc[u * hpb + a] = jnp.broadcast_to(sk_a, (bq, 1))
                        l_sc[u * hpb + a] = jnp.ones((bq, 1), F32)
                    else:
                        m_sc[u * hpb + a] = jnp.full((bq, 1), NEG_INIT, F32)
                        l_sc[u * hpb + a] = jnp.zeros((bq, 1), F32)
            acc_sc[...] = jnp.zeros_like(acc_sc)

        @pl.when(active)
        def _():
            mask = _attn_mask(cfg, i, kb)
            for u in range(upb):
                sl = slice(u * LANES, (u + 1) * LANES)
                qv = q_ref[:, sl].astype(BF16)
                kv = k_ref[:, sl].astype(BF16)
                vv = v_ref[:, sl].astype(BF16)
                pv_tot, alphas = None, []
                for a in range(hpb):
                    idx = u * hpb + a
                    qa = _sel(lms[a], qv) if hpb == 2 else qv
                    s = _dot(qa, kv, _NT) * cfg.scale
                    if mask is not None:
                        s = jnp.where(mask, s, NEG_MASK)
                    m_prev = m_sc[idx]
                    m_new = jnp.maximum(m_prev, jnp.max(s, axis=-1, keepdims=True))
                    alpha = jnp.exp(m_prev - m_new)
                    p = jnp.exp(s - m_new)
                    l_sc[idx] = alpha * l_sc[idx] + jnp.sum(p, axis=-1, keepdims=True)
                    m_sc[idx] = m_new
                    va = _sel(lms[a], vv) if hpb == 2 else vv
                    pv = _dot(p.astype(BF16), va, _NN)
                    pv_tot = pv if pv_tot is None else pv_tot + pv
                    alphas.append(alpha)
                af = alphas[0] if hpb == 1 else jnp.where(lms[0], alphas[0], alphas[1])
                acc_sc[u] = acc_sc[u] * af + pv_tot

        @pl.when(j == nj - 1)
        def _():
            for u in range(upb):
                sl = slice(u * LANES, (u + 1) * LANES)
                if hpb == 1:
                    lf = jnp.broadcast_to(l_sc[u], (bq, LANES))
                    mf = jnp.broadcast_to(m_sc[u], (bq, LANES))
                else:
                    lf = jnp.where(lms[0], l_sc[2 * u], l_sc[2 * u + 1])
                    mf = jnp.where(lms[0], m_sc[2 * u], m_sc[2 * u + 1])
                o_ref[:, sl] = acc_sc[u] / lf
                l_ref[:, sl] = mf + jnp.log(lf)

    qspec = pl.BlockSpec((bq, wb), lambda ub, i, j: (i, ub))
    kspec = pl.BlockSpec((bk, wb), lambda ub, i, j: (sched(i, j)[0], ub))
    in_specs = [qspec, kspec, kspec] + ([pl.BlockSpec((1, wb), lambda ub, i, j: (0, ub))] if has_sink else [])
    args = (q, k, v) + ((sinkrow,) if has_sink else ())
    return pl.pallas_call(
        body, name=name, grid=(nub, nq, nj), in_specs=in_specs, out_specs=(qspec, qspec),
        out_shape=(jax.ShapeDtypeStruct((sq, w), F32), jax.ShapeDtypeStruct((sq, w), F32)),
        scratch_shapes=[pltpu.VMEM((upb * hpb, bq, 1), F32), pltpu.VMEM((upb * hpb, bq, 1), F32),
                        pltpu.VMEM((upb, bq, LANES), F32)],
        compiler_params=_params("parallel", "parallel", "arbitrary"),
    )(*args)


def _softmax_grad_terms(cfg, lms, a, qv, kv, vv, dob, prod, lv, mask):
    hpb = cfg.hpb
    if hpb == 2:
        t = jnp.sum(_sel(lms[a], prod), axis=-1, keepdims=True)
        lse = jnp.max(jnp.where(lms[a], lv, -jnp.inf), axis=-1, keepdims=True)
        qa, doa = _sel(lms[a], qv), _sel(lms[a], dob)
    else:
        t = jnp.sum(prod, axis=-1, keepdims=True)
        lse = jnp.max(lv, axis=-1, keepdims=True)
        qa, doa = qv, dob
    s = _dot(qa, kv, _NT) * cfg.scale
    if mask is not None:
        s = jnp.where(mask, s, NEG_MASK)
    p = jnp.exp(s - lse)
    dp = _dot(doa, vv, _NT)
    ds = (p * (dp - t)) * cfg.scale
    return p, ds, qa, doa, t


def _attn_dq(q, k, v, o, lse, do, sinkrow, cfg, name):
    sq, w = q.shape
    sk = k.shape[0]
    bq, bk, upb, hpb = cfg.bq, cfg.bk, cfg.upb, cfg.hpb
    nq, nk, nub = sq // bq, sk // bk, w // (LANES * upb)
    nj, sched = _kv_of_q(cfg, nq, nk)
    wb = LANES * upb
    has_sink = sinkrow is not None

    def body(*refs):
        if has_sink:
            q_ref, k_ref, v_ref, o_ref, l_ref, do_ref, s_ref, dq_ref, dsink_ref, acc = refs
        else:
            q_ref, k_ref, v_ref, o_ref, l_ref, do_ref, dq_ref, acc = refs
        i, j = pl.program_id(1), pl.program_id(2)
        kb, active = sched(i, j)
        lms = _lane_masks()

        @pl.when(j == 0)
        def _():
            acc[...] = jnp.zeros_like(acc)

        @pl.when(active)
        def _():
            mask = _attn_mask(cfg, i, kb)
            for u in range(upb):
                sl = slice(u * LANES, (u + 1) * LANES)
                qv = q_ref[:, sl].astype(BF16)
                kv = k_ref[:, sl].astype(BF16)
                vv = v_ref[:, sl].astype(BF16)
                dov = do_ref[:, sl]
                prod = dov * o_ref[:, sl]
                dob = dov.astype(BF16)
                lv = l_ref[:, sl]
                tot = None
                for a in range(hpb):
                    _, ds, _, _, _ = _softmax_grad_terms(cfg, lms, a, qv, kv, vv, dob, prod, lv, mask)
                    ka = _sel(lms[a], kv) if hpb == 2 else kv
                    c = _dot(ds.astype(BF16), ka, _NN)
                    tot = c if tot is None else tot + c
                acc[u] = acc[u] + tot

        @pl.when(j == nj - 1)
        def _():
            for u in range(upb):
                dq_ref[:, u * LANES:(u + 1) * LANES] = acc[u]
            if has_sink:
                @pl.when(i == 0)
                def _():
                    dsink_ref[...] = jnp.zeros_like(dsink_ref)

                for u in range(upb):
                    sl = slice(u * LANES, (u + 1) * LANES)
                    prod = do_ref[:, sl] * o_ref[:, sl]
                    t0 = jnp.sum(_sel(lms[0], prod), axis=-1, keepdims=True)
                    t1 = jnp.sum(_sel(lms[1], prod), axis=-1, keepdims=True)
                    tf = jnp.where(lms[0], t0, t1)
                    rs = -jnp.exp(s_ref[:, sl] - l_ref[:, sl]) * tf
                    dsink_ref[0:1, sl] += jnp.sum(rs, axis=0, keepdims=True)

    qspec = pl.BlockSpec((bq, wb), lambda ub, i, j: (i, ub))
    kspec = pl.BlockSpec((bk, wb), lambda ub, i, j: (sched(i, j)[0], ub))
    in_specs = [qspec, kspec, kspec, qspec, qspec, qspec]
    args = (q, k, v, o, lse, do)
    out_specs = qspec
    out_shape = jax.ShapeDtypeStruct((sq, w), F32)
    sem = ("parallel", "parallel", "arbitrary")
    if has_sink:
        in_specs = in_specs + [pl.BlockSpec((1, wb), lambda ub, i, j: (0, ub))]
        args = args + (sinkrow,)
        out_specs = (qspec, pl.BlockSpec((8, wb), lambda ub, i, j: (0, ub)))
        out_shape = (out_shape, jax.ShapeDtypeStruct((8, w), F32))
        sem = ("parallel", "arbitrary", "arbitrary")
    return pl.pallas_call(
        body, name=name, grid=(nub, nq, nj), in_specs=in_specs, out_specs=out_specs, out_shape=out_shape,
        scratch_shapes=[pltpu.VMEM((upb, bq, LANES), F32)], compiler_params=_params(*sem),
    )(*args)


def _attn_dkv(q, k, v, o, lse, do, cfg, name):
    sq, w = q.shape
    sk = k.shape[0]
    bq, bk, upb, hpb = cfg.bq, cfg.bk, cfg.upb, cfg.hpb
    nq, nk, nub = sq // bq, sk // bk, w // (LANES * upb)
    nj, sched = _q_of_kv(cfg, nq, nk)
    wb = LANES * upb

    def body(q_ref, k_ref, v_ref, o_ref, l_ref, do_ref, dk_ref, dv_ref, dk_acc, dv_acc):
        kb, j = pl.program_id(1), pl.program_id(2)
        i, active = sched(kb, j)
        lms = _lane_masks()

        @pl.when(j == 0)
        def _():
            dk_acc[...] = jnp.zeros_like(dk_acc)
            dv_acc[...] = jnp.zeros_like(dv_acc)

        @pl.when(active)
        def _():
            mask = _attn_mask(cfg, i, kb)
            for u in range(upb):
                sl = slice(u * LANES, (u + 1) * LANES)
                qv = q_ref[:, sl].astype(BF16)
                kv = k_ref[:, sl].astype(BF16)
                vv = v_ref[:, sl].astype(BF16)
                dov = do_ref[:, sl]
                prod = dov * o_ref[:, sl]
                dob = dov.astype(BF16)
                lv = l_ref[:, sl]
                dk_tot, dv_tot = None, None
                for a in range(hpb):
                    p, ds, qa, doa, _ = _softmax_grad_terms(cfg, lms, a, qv, kv, vv, dob, prod, lv, mask)
                    dvc = _dot(p.astype(BF16), doa, _TN)
                    dkc = _dot(ds.astype(BF16), qa, _TN)
                    dv_tot = dvc if dv_tot is None else dv_tot + dvc
                    dk_tot = dkc if dk_tot is None else dk_tot + dkc
                dk_acc[u] = dk_acc[u] + dk_tot
                dv_acc[u] = dv_acc[u] + dv_tot

        @pl.when(j == nj - 1)
        def _():
            for u in range(upb):
                sl = slice(u * LANES, (u + 1) * LANES)
                dk_ref[:, sl] = dk_acc[u]
                dv_ref[:, sl] = dv_acc[u]

    qspec = pl.BlockSpec((bq, wb), lambda ub, kb, j: (sched(kb, j)[0], ub))
    kspec = pl.BlockSpec((bk, wb), lambda ub, kb, j: (kb, ub))
    return pl.pallas_call(
        body, name=name, grid=(nub, nk, nj), in_specs=[qspec, kspec, kspec, qspec, qspec, qspec],
        out_specs=(kspec, kspec),
        out_shape=(jax.ShapeDtypeStruct((sk, w), F32), jax.ShapeDtypeStruct((sk, w), F32)),
        scratch_shapes=[pltpu.VMEM((upb, bk, LANES), F32), pltpu.VMEM((upb, bk, LANES), F32)],
        compiler_params=_params("parallel", "parallel", "arbitrary"),
    )(q, k, v, o, lse, do)


def _band_masks(max_dist):
    assert BLOCK - 1 <= max_dist <= BLOCK
    r = lax.broadcasted_iota(jnp.int32, (BLOCK, BLOCK), 0)
    c = lax.broadcasted_iota(jnp.int32, (BLOCK, BLOCK), 1)
    return (BLOCK + r - c) <= max_dist, r >= c


def _stack_heads(t):
    return jnp.concatenate([t, t], axis=0)


def _head_terms(lms, a, prod, lv):
    t = jnp.sum(_sel(lms[a], prod), axis=-1, keepdims=True)
    lse = jnp.max(jnp.where(lms[a], lv, -jnp.inf), axis=-1, keepdims=True)
    return t, lse


class _Residue:
    def __init__(self, ref, r, dil):
        self.ref, self.rows = ref, pl.ds(r, BLOCK, stride=dil)

    def __getitem__(self, idx):
        return self.ref[self.rows, idx[1]]

    def __setitem__(self, idx, val):
        self.ref[self.rows, idx[1]] = val


def _residues(refs, dil):
    if dil == 1:
        return [tuple(refs)]
    return [tuple(_Residue(x, r, dil) for x in refs) for r in range(dil)]


def _band_fwd(q, k, v, sinkrow, scale, max_dist, upb, dil, name):
    sq, w = q.shape
    rb = BLOCK * dil
    nq, nub, wb = sq // rb, w // (LANES * upb), LANES * upb
    has_sink = sinkrow is not None

    def body(*refs):
        s_ref = refs[5] if has_sink else None
        lms = _lane_masks()
        mprev, mcur = _band_masks(max_dist)
        mprev = jnp.logical_and(mprev, pl.program_id(1) > 0)
        mask2 = _stack_heads(jnp.concatenate([mprev, mcur], axis=1))
        chains = [(rr, slice(u * LANES, (u + 1) * LANES))
                  for rr in _residues(refs[:5] + refs[-2:], dil) for u in range(upb)]
        for g0 in range(0, len(chains), BAND_CHAINS_PER_BATCH):
            group = chains[g0:g0 + BAND_CHAINS_PER_BATCH]
            qs, kcat, vcat, sks = [], [], [], []
            for (q_ref, kp_ref, kc_ref, vp_ref, vc_ref, _, _), sl in group:
                qv = (q_ref[:, sl] * scale).astype(BF16)
                qs.append(jnp.concatenate([_sel(lms[0], qv), _sel(lms[1], qv)], axis=0))
                kcat.append(jnp.concatenate([kp_ref[:, sl].astype(BF16), kc_ref[:, sl].astype(BF16)], axis=0))
                vcat.append(jnp.concatenate([vp_ref[:, sl].astype(BF16), vc_ref[:, sl].astype(BF16)], axis=0))
                if has_sink:
                    sks.append(s_ref[sl.start // LANES])
            qs, kcat, vcat = jnp.stack(qs), jnp.stack(kcat), jnp.stack(vcat)
            sc = jnp.where(mask2[None], _dot(qs, kcat, _BNT), NEG_MASK)
            m = jnp.max(sc, axis=-1, keepdims=True)
            p = jnp.exp(sc - m)
            l = jnp.sum(p, axis=-1, keepdims=True)
            pv = _dot(p.astype(BF16), vcat, _BNN)
            if has_sink:
                sk2 = jnp.stack(sks)
                m_all = jnp.maximum(m, sk2)
                shrink = jnp.exp(m - m_all)
                l = l * shrink + jnp.exp(sk2 - m_all)
                pv, m = pv * shrink, m_all
            o2 = pv / l
            lse2 = m + jnp.log(l)
            for gi, ((_, _, _, _, _, o_ref, l_ref), sl) in enumerate(group):
                o_ref[:, sl] = jnp.where(lms[0], o2[gi, :BLOCK], o2[gi, BLOCK:])
                l_ref[:, sl] = jnp.where(lms[0], lse2[gi, :BLOCK], lse2[gi, BLOCK:])

    cur = pl.BlockSpec((rb, wb), lambda ub, i: (i, ub))
    prev = pl.BlockSpec((rb, wb), lambda ub, i: (jnp.maximum(i - 1, 0), ub))
    in_specs = [cur, prev, cur, prev, cur]
    in_specs += [pl.BlockSpec((upb, 2 * BLOCK, 1), lambda ub, i: (ub, 0, 0))] if has_sink else []
    args = (q, k, k, v, v) + ((sinkrow,) if has_sink else ())
    return pl.pallas_call(
        body, name=name, grid=(nub, nq), in_specs=in_specs, out_specs=(cur, cur),
        out_shape=(jax.ShapeDtypeStruct((sq, w), F32), jax.ShapeDtypeStruct((sq, w), F32)),
        compiler_params=_params("parallel", "parallel"),
    )(*args)


def _band_dq(q, k, v, o, lse, do, sinkrow, scale, max_dist, upb, dil, name):
    sq, w = q.shape
    rb = BLOCK * dil
    nq, nub, wb = sq // rb, w // (LANES * upb), LANES * upb
    has_sink = sinkrow is not None

    def body(*refs):
        if has_sink:
            s_ref, dq_block, dsink_ref = refs[8], refs[9], refs[10]
        else:
            dq_block = refs[8]
        i = pl.program_id(1)
        lms = _lane_masks()
        mprev, mcur = _band_masks(max_dist)
        mprev = jnp.logical_and(mprev, i > 0)
        mask2 = _stack_heads(jnp.concatenate([mprev, mcur], axis=1))
        if has_sink:
            @pl.when(i == 0)
            def _():
                dsink_ref[...] = jnp.zeros_like(dsink_ref)

        chains = [(rr, slice(u * LANES, (u + 1) * LANES))
                  for rr in _residues(refs[:8] + (dq_block,), dil) for u in range(upb)]
        for g0 in range(0, len(chains), BAND_CHAINS_PER_BATCH):
            group = chains[g0:g0 + BAND_CHAINS_PER_BATCH]
            qs, dos, kcat, vcat, t2, lse2 = [], [], [], [], [], []
            for (q_ref, kp_ref, kc_ref, vp_ref, vc_ref, o_ref, l_ref, do_ref, _), sl in group:
                qv = (q_ref[:, sl] * scale).astype(BF16)
                dov = do_ref[:, sl]
                prod = dov * o_ref[:, sl]
                dob = dov.astype(BF16)
                lv = l_ref[:, sl]
                (t0, lse0), (t1, lse1) = _head_terms(lms, 0, prod, lv), _head_terms(lms, 1, prod, lv)
                t2.append(jnp.concatenate([t0, t1], axis=0))
                lse2.append(jnp.concatenate([lse0, lse1], axis=0))
                qs.append(jnp.concatenate([_sel(lms[0], qv), _sel(lms[1], qv)], axis=0))
                dos.append(jnp.concatenate([_sel(lms[0], dob), _sel(lms[1], dob)], axis=0))
                kcat.append(jnp.concatenate([kp_ref[:, sl].astype(BF16), kc_ref[:, sl].astype(BF16)], axis=0))
                vcat.append(jnp.concatenate([vp_ref[:, sl].astype(BF16), vc_ref[:, sl].astype(BF16)], axis=0))
                if has_sink:
                    rs = -jnp.exp(s_ref[:, sl] - lv) * jnp.where(lms[0], t0, t1)
                    dsink_ref[0:1, sl] += jnp.sum(rs, axis=0, keepdims=True)
            qs, dos, kcat, vcat = jnp.stack(qs), jnp.stack(dos), jnp.stack(kcat), jnp.stack(vcat)
            p = jnp.exp(jnp.where(mask2[None], _dot(qs, kcat, _BNT), NEG_MASK) - jnp.stack(lse2))
            ds = (p * (_dot(dos, vcat, _BNT) - jnp.stack(t2))).astype(BF16)
            dq2 = _dot(ds, kcat, _BNN) * scale
            for gi, ((_, _, _, _, _, _, _, _, dq_ref), sl) in enumerate(group):
                dq_ref[:, sl] = jnp.where(lms[0], dq2[gi, :BLOCK], dq2[gi, BLOCK:])

    cur = pl.BlockSpec((rb, wb), lambda ub, i: (i, ub))
    prev = pl.BlockSpec((rb, wb), lambda ub, i: (jnp.maximum(i - 1, 0), ub))
    in_specs = [cur, prev, cur, prev, cur, cur, cur, cur]
    args = (q, k, k, v, v, o, lse, do)
    out_specs, out_shape = cur, jax.ShapeDtypeStruct((sq, w), F32)
    sem = ("parallel", "parallel")
    if has_sink:
        in_specs = in_specs + [pl.BlockSpec((1, wb), lambda ub, i: (0, ub))]
        args = args + (sinkrow,)
        out_specs = (cur, pl.BlockSpec((8, wb), lambda ub, i: (0, ub)))
        out_shape = (out_shape, jax.ShapeDtypeStruct((8, w), F32))
        sem = ("parallel", "arbitrary")
    return pl.pallas_call(
        body, name=name, grid=(nub, nq), in_specs=in_specs, out_specs=out_specs, out_shape=out_shape,
        compiler_params=_params(*sem),
    )(*args)


def _band_dkv(q, k, v, o, lse, do, scale, max_dist, upb, dil, name):
    sq, w = q.shape
    rb = BLOCK * dil
    nq, nub, wb = sq // rb, w // (LANES * upb), LANES * upb

    def body(*refs):
        kb = pl.program_id(1)
        lms = _lane_masks()
        key = lax.broadcasted_iota(jnp.int32, (BLOCK, BLOCK), 0)
        qry = lax.broadcasted_iota(jnp.int32, (BLOCK, BLOCK), 1)
        msame = qry >= key
        mnext = jnp.logical_and((BLOCK + qry - key) <= max_dist, kb < nq - 1)
        mask4 = jnp.concatenate([msame, msame, mnext, mnext], axis=1)
        chains =[(rr, slice(u * LANES, (u + 1) * LANES)) for rr in _residues(refs, dil) for u in range(upb)]
        for g0 in range(0, len(chains), BAND_CHAINS_PER_BATCH):
            group = chains[g0:g0 + BAND_CHAINS_PER_BATCH]
            kvs, vvs, qss, doss, t4s, lse4s = [], [], [], [], [], []
            for (k_ref, v_ref, qs_ref, qn_ref, os_ref, on_ref, ls_ref, ln_ref, dos_ref, don_ref, _, _), sl in group:
                kvs.append(k_ref[:, sl].astype(BF16))
                vvs.append(v_ref[:, sl].astype(BF16))
                qparts, doparts, tparts, lparts = [], [], [], []
                for q_ref, o_ref, l_ref, do_ref in ((qs_ref, os_ref, ls_ref, dos_ref),
                                                    (qn_ref, on_ref, ln_ref, don_ref)):
                    qv = (q_ref[:, sl] * scale).astype(BF16)
                    dov = do_ref[:, sl]
                    prod_t = (dov * o_ref[:, sl]).T
                    dob = dov.astype(BF16)
                    lse_t = l_ref[:, sl].T
                    for a in range(2):
                        lanes = slice(a * HEAD_DIM, (a + 1) * HEAD_DIM)
                        qparts.append(_sel(lms[a], qv))
                        doparts.append(_sel(lms[a], dob))
                        tparts.append(jnp.sum(prod_t[lanes, :], axis=0, keepdims=True))
                        lparts.append(lse_t[a * HEAD_DIM:a * HEAD_DIM + 1, :])
                qss.append(jnp.concatenate(qparts, axis=0))
                doss.append(jnp.concatenate(doparts, axis=0))
                t4s.append(jnp.concatenate(tparts, axis=1))
                lse4s.append(jnp.concatenate(lparts, axis=1))
            kv, vv, qs, dos = jnp.stack(kvs), jnp.stack(vvs), jnp.stack(qss), jnp.stack(doss)
            p = jnp.exp(jnp.where(mask4[None], _dot(kv, qs, _BNT), NEG_MASK) - jnp.stack(lse4s))
            ds = (p * (_dot(vv, dos, _BNT) - jnp.stack(t4s))).astype(BF16)
            dv = _dot(p.astype(BF16), dos, _BNN)
            dk = _dot(ds, qs, _BNN)
            for gi, (rr, sl) in enumerate(group):
                rr[-1][:, sl] = dv[gi]
                rr[-2][:, sl] = dk[gi]

    same = pl.BlockSpec((rb, wb), lambda ub, kb: (kb, ub))
    nxt = pl.BlockSpec((rb, wb), lambda ub, kb: (jnp.minimum(kb + 1, nq - 1), ub))
    return pl.pallas_call(
        body, name=name, grid=(nub, nq), in_specs=[same, same, same, nxt, same, nxt, same, nxt, same, nxt],
        out_specs=(same, same),
        out_shape=(jax.ShapeDtypeStruct((sq, w), F32), jax.ShapeDtypeStruct((sq, w), F32)),
        compiler_params=_params("parallel", "parallel"),
    )(k, v, q, q, o, o, lse, lse, do, do)


def _make_band_attention(scale, max_dist, upb, name):
    @jax.custom_vjp
    def attn(q, k, v, sinks):
        return _band_fwd(q, k, v, _sink_col(sinks), scale, max_dist, upb, 1, name + "_fwd")[0]

    def fwd(q, k, v, sinks):
        o, lse = _band_fwd(q, k, v, _sink_col(sinks), scale, max_dist, upb, 1, name + "_fwd")
        return o, (q, k, v, o, lse, sinks)

    def bwd(res, do):
        q, k, v, o, lse, sinks = res
        dq, dsink = _band_dq(q, k, v, o, lse, do, _sink_row(sinks), scale, max_dist, upb, 1, name + "_dq")
        dk, dv = _band_dkv(q, k, v, o, lse, do, scale, max_dist, upb, 1, name + "_dkv")
        return dq, dk, dv, dsink[0].reshape(-1, HEAD_DIM)[:, 0]

    attn.defvjp(fwd, bwd)
    return attn


def _triangle(n, by_key):
    if by_key:
        pairs = [(i, kb) for kb in range(n) for i in range(kb, n)]
    else:
        pairs = [(i, j) for i in range(n) for j in range(i + 1)]
    qi = np.asarray([p[0] for p in pairs], np.int32)
    kj = np.asarray([p[1] for p in pairs], np.int32)
    return jnp.asarray(qi), jnp.asarray(kj)


def _causal_fwd(q, k, v, scale, blk, name):
    s, w = q.shape
    nq, nub = s // blk, w // LANES
    qi, kj = _triangle(nq, by_key=False)

    def body(qi_ref, kj_ref, q_ref, k_ref, v_ref, o_ref, l_ref, m_sc, l_sc, acc_sc):
        t = pl.program_id(1)
        i, j = qi_ref[t], kj_ref[t]

        @pl.when(j == 0)
        def _():
            m_sc[...] = jnp.full_like(m_sc, NEG_INIT)
            l_sc[...] = jnp.zeros_like(l_sc)
            acc_sc[...] = jnp.zeros_like(acc_sc)

        def step(diagonal):
            kv, vv = k_ref[...].astype(BF16), v_ref[...].astype(BF16)
            chains = range(0, blk, CAUSAL_ROW_CHAIN)
            scs = [_dot((q_ref[c0:c0 + CAUSAL_ROW_CHAIN, :] * scale).astype(BF16), kv, _NT) for c0 in chains]
            m_all, l_all, acc_all = m_sc[...], l_sc[...], acc_sc[...]
            m_out, l_out, acc_out = [], [], []
            for sc, c0 in zip(scs, chains):
                rows = slice(c0, c0 + CAUSAL_ROW_CHAIN)
                if diagonal:
                    r = c0 + lax.broadcasted_iota(jnp.int32, (CAUSAL_ROW_CHAIN, blk), 0)
                    c = lax.broadcasted_iota(jnp.int32, (CAUSAL_ROW_CHAIN, blk), 1)
                    sc = jnp.where(r >= c, sc, NEG_MASK)
                m_prev = m_all[rows]
                m_new = jnp.maximum(m_prev, jnp.max(sc, axis=-1, keepdims=True))
                alpha = jnp.exp(m_prev - m_new)
                p = jnp.exp(sc - m_new)
                l_out.append(alpha * l_all[rows] + jnp.sum(p, axis=-1, keepdims=True))
                m_out.append(m_new)
                acc_out.append(acc_all[rows] * alpha + _dot(p.astype(BF16), vv, _NN))
            m_sc[...] = jnp.concatenate(m_out, axis=0)
            l_sc[...] = jnp.concatenate(l_out, axis=0)
            acc_sc[...] = jnp.concatenate(acc_out, axis=0)

        @pl.when(j < i)
        def _():
            step(False)

        @pl.when(j == i)
        def _():
            step(True)
            lf = l_sc[...]
            o_ref[...] = acc_sc[...] / lf
            l_ref[...] = jnp.broadcast_to(m_sc[...] + jnp.log(lf), (blk, LANES))

    qspec = pl.BlockSpec((blk, LANES), lambda ub, t, qi_ref, kj_ref: (qi_ref[t], ub))
    kspec = pl.BlockSpec((blk, LANES), lambda ub, t, qi_ref, kj_ref: (kj_ref[t], ub))
    return pl.pallas_call(
        body, name=name,
        grid_spec=pltpu.PrefetchScalarGridSpec(
            num_scalar_prefetch=2, grid=(nub, qi.shape[0]), in_specs=[qspec, kspec, kspec], out_specs=(qspec, qspec),
            scratch_shapes=[pltpu.VMEM((blk, 1), F32), pltpu.VMEM((blk, 1), F32), pltpu.VMEM((blk, LANES), F32)]),
        out_shape=(jax.ShapeDtypeStruct((s, w), F32), jax.ShapeDtypeStruct((s, w), F32)),
        compiler_params=_params("parallel", "arbitrary"),
    )(qi, kj, q, k, v)


def _causal_bwd(q, k, v, o, lse, do, scale, blk, name):
    s, w = q.shape
    nq, nub = s // blk, w // LANES
    qi, kj = _triangle(nq, by_key=True)

    def body(qi_ref, kj_ref, q_ref, k_ref, v_ref, o_ref, l_ref, do_ref, dq_ref, dk_ref, dv_ref, dk_acc, dv_acc):
        t = pl.program_id(1)
        i, kb = qi_ref[t], kj_ref[t]

        @pl.when(t == 0)
        def _():
            dq_ref[...] = jnp.zeros_like(dq_ref)

        @pl.when(i == kb)
        def _():
            dk_acc[...] = jnp.zeros_like(dk_acc)
            dv_acc[...] = jnp.zeros_like(dv_acc)

        def step(diagonal):
            qv = (q_ref[...] * scale).astype(BF16)
            kv, vv = k_ref[...].astype(BF16), v_ref[...].astype(BF16)
            dov = do_ref[...]
            tsum = jnp.sum(dov * o_ref[...], axis=-1, keepdims=True)
            dob = dov.astype(BF16)
            sc = _dot(qv, kv, _NT)
            if diagonal:
                r = lax.broadcasted_iota(jnp.int32, (blk, blk), 0)
                c = lax.broadcasted_iota(jnp.int32, (blk, blk), 1)
                sc = jnp.where(r >= c, sc, NEG_MASK)
            p = jnp.exp(sc - l_ref[:, 0:1])
            ds = (p * (_dot(dob, vv, _NT) - tsum)).astype(BF16)
            dv_acc[...] += _dot(p.astype(BF16), dob, _TN)
            dk_acc[...] += _dot(ds, qv, _TN)
            rows = pl.ds(pl.multiple_of(i * blk, blk), blk)
            dq_ref[rows, :] += _dot(ds, kv, _NN) * scale

        @pl.when(i == kb)
        def _():
            step(True)

        @pl.when(i > kb)
        def _():
            step(False)

        @pl.when(i == nq - 1)
        def _():
            dk_ref[...] = dk_acc[...]
            dv_ref[...] = dv_acc[...]

    qspec = pl.BlockSpec((blk, LANES), lambda ub, t, qi_ref, kj_ref: (qi_ref[t], ub))
    kspec = pl.BlockSpec((blk, LANES), lambda ub, t, qi_ref, kj_ref: (kj_ref[t], ub))
    whole = pl.BlockSpec((s, LANES), lambda ub, t, qi_ref, kj_ref: (0, ub))
    out = jax.ShapeDtypeStruct((s, w), F32)
    return pl.pallas_call(
        body, name=name,
        grid_spec=pltpu.PrefetchScalarGridSpec(
            num_scalar_prefetch=2, grid=(nub, qi.shape[0]), in_specs=[qspec, kspec, kspec, qspec, qspec, qspec],
            out_specs=(whole, kspec, kspec),
            scratch_shapes=[pltpu.VMEM((blk, LANES), F32), pltpu.VMEM((blk, LANES), F32)]),
        out_shape=(out, out, out), compiler_params=_params("parallel", "arbitrary"),
    )(qi, kj, q, k, v, o, lse, do)


def _make_causal_attention(scale, blk, name):
    @jax.custom_vjp
    def attn(q, k, v):
        return _causal_fwd(q, k, v, scale, blk, name + "_fwd")[0]

    def fwd(q, k, v):
        o, lse = _causal_fwd(q, k, v, scale, blk, name + "_fwd")
        return o, (q, k, v, o, lse)

    def bwd(res, do):
        q, k, v, o, lse = res
        return _causal_bwd(q, k, v, o, lse, do, scale, blk, name + "_bwd")

    attn.defvjp(fwd, bwd)
    return attn


def _make_attention(cfg, name, with_sink=False):
    if with_sink:
        @jax.custom_vjp
        def attn(q, k, v, sinks):
            return _attn_fwd(q, k, v, _sink_row(sinks), cfg, name + "_fwd")[0]

        def fwd(q, k, v, sinks):
            o, lse = _attn_fwd(q, k, v, _sink_row(sinks), cfg, name + "_fwd")
            return o, (q, k, v, o, lse, sinks)

        def bwd(res, do):
            q, k, v, o, lse, sinks = res
            dq, dsink = _attn_dq(q, k, v, o, lse, do, _sink_row(sinks), cfg, name + "_dq")
            dk, dv = _attn_dkv(q, k, v, o, lse, do, cfg, name + "_dkv")
            return dq, dk, dv, dsink[0].reshape(-1, HEAD_DIM)[:, 0]
    else:
        @jax.custom_vjp
        def attn(q, k, v):
            return _attn_fwd(q, k, v, None, cfg, name + "_fwd")[0]

        def fwd(q, k, v):
            o, lse = _attn_fwd(q, k, v, None, cfg, name + "_fwd")
            return o, (q, k, v, o, lse)

        def bwd(res, do):
            q, k, v, o, lse = res
            dq = _attn_dq(q, k, v, o, lse, do, None, cfg, name + "_dq")
            dk, dv = _attn_dkv(q, k, v, o, lse, do, cfg, name + "_dkv")
            return dq, dk, dv

    attn.defvjp(fwd, bwd)
    return attn


def _sink_row(sinks):
    return jnp.repeat(sinks.astype(F32), HEAD_DIM).reshape(1, -1)


def _sink_col(sinks):
    return jnp.repeat(sinks.astype(F32).reshape(-1, 2, 1), BLOCK, axis=1)


def _merge3(os_, ls_, name):
    s, w = os_[0].shape
    bs = _pick(s, (256, 128))

    def body(o1, o2, o3, l1, l2, l3, out_ref, lse_ref):
        a1, a2, a3 = l1[...], l2[...], l3[...]
        m = jnp.maximum(jnp.maximum(a1, a2), a3)
        e1, e2, e3 = jnp.exp(a1 - m), jnp.exp(a2 - m), jnp.exp(a3 - m)
        z = e1 + e2 + e3
        out_ref[...] = (e1 * o1[...] + e2 * o2[...] + e3 * o3[...]) / z
        lse_ref[...] = m + jnp.log(z)

    row = pl.BlockSpec((bs, w), lambda i: (i, 0))
    return pl.pallas_call(
        body, name=name, grid=(s // bs,), in_specs=[row] * 6, out_specs=(row, row),
        out_shape=(jax.ShapeDtypeStruct((s, w), F32), jax.ShapeDtypeStruct((s, w), F32)),
        compiler_params=_params("parallel"),
    )(*os_, *ls_)


def _add3(a, b, c, name):
    s, w = a.shape
    bs = _pick(s, (512, 256, 128))

    def body(a_ref, b_ref, c_ref, o_ref):
        o_ref[...] = (a_ref[...] + b_ref[...]) + c_ref[...]

    row = pl.BlockSpec((bs, w), lambda i: (i, 0))
    return pl.pallas_call(
        body, name=name, grid=(s // bs,), in_specs=[row] * 3, out_specs=row,
        out_shape=jax.ShapeDtypeStruct((s, w), F32), compiler_params=_params("parallel"),
    )(a, b, c)


def _make_dilated(name):
    scale, max_dist = HEAD_DIM ** -0.5, BLOCK

    def upb_of(dil):
        return BAND_UNITS_PER_STEP if dil == 1 else 1

    def forward(q, k, v):
        os_, ls_ = [], []
        for n, (_, dil) in enumerate(DIL_PATTERNS):
            o, l = _band_fwd(q, k, v, None, scale, max_dist, upb_of(dil), dil, "%s_b%d_fwd" % (name, n))
            os_.append(o)
            ls_.append(l)
        return _merge3(os_, ls_, name + "_merge")

    @jax.custom_vjp
    def dilated(q, k, v):
        return forward(q, k, v)[0]

    def fwd(q, k, v):
        out, lse = forward(q, k, v)
        return out, (q, k, v, out, lse)

    def bwd(res, do):
        q, k, v, out, lse = res
        dqs, dks, dvs = [], [], []
        for n, (_, dil) in enumerate(DIL_PATTERNS):
            args = (q, k, v, out, lse, do)
            dqs.append(_band_dq(*args, None, scale, max_dist, upb_of(dil), dil, "%s_b%d_dq" % (name, n)))
            dk, dv = _band_dkv(*args, scale, max_dist, upb_of(dil), dil, "%s_b%d_dkv" % (name, n))
            dks.append(dk)
            dvs.append(dv)
        return (_add3(*dqs, name + "_dq_sum"), _add3(*dks, name + "_dk_sum"), _add3(*dvs, name + "_dv_sum"))

    dilated.defvjp(fwd, bwd)
    return dilated


def _make_norm_linear(name):
    @jax.custom_vjp
    def op(x, g, wslot, w):
        return _mm(_rms_fwd(x, g, name + "_norm"), w, "nn", name + "_mm")

    def fwd(x, g, wslot, w):
        h = _rms_fwd(x, g, name + "_norm")
        return _mm(h, w, "nn", name + "_mm"), (x, g, h, w)

    def bwd(res, dz):
        x, g, h, w = res
        dh = _mm(dz, w, "nt", name + "_dh")
        dw = _mm(h, dz, "tn", name + "_dw", out_dtype=GRAD_WIRE_DTYPE)
        dx, dg = _rms_bwd(x, g, dh, name + "_norm_bwd")
        return dx, dg, dw, None

    op.defvjp(fwd, bwd)
    return op


def _make_norm_linear_through(name):
    @jax.custom_vjp
    def op(x, g, wslot, w):
        return _mm(_rms_fwd(x, g, name + "_norm"), w, "nn", name + "_mm"), x

    def fwd(x, g, wslot, w):
        h = _rms_fwd(x, g, name + "_norm")
        return (_mm(h, w, "nn", name + "_mm"), x), (x, g, h, w)

    def bwd(res, cts):
        x, g, h, w = res
        dz, dx_through = cts
        dh = _mm(dz, w, "nt", name + "_dh")
        dw = _mm(h, dz, "tn", name + "_dw", out_dtype=GRAD_WIRE_DTYPE)
        dx, dg = _rms_bwd(x, g, dh, name + "_norm_bwd", dres=dx_through)
        return dx, dg, dw, None

    op.defvjp(fwd, bwd)
    return op


def _make_linear_res(name):
    @jax.custom_vjp
    def op(a, wslot, w, res):
        return _mm(a, w, "nn", name + "_mm", res=res)

    def fwd(a, wslot, w, res):
        return _mm(a, w, "nn", name + "_mm", res=res), (a, w)

    def bwd(saved, dout):
        a, w = saved
        da = _mm(dout, w, "nt", name + "_da")
        dw = _mm(a, dout, "tn", name + "_dw", out_dtype=GRAD_WIRE_DTYPE)
        return da, dw, None, dout

    op.defvjp(fwd, bwd)
    return op


FFN_TILE_M, FFN_TILE_N = 512, 1408


def _gate_up_act(h, wg, wu, name):
    m, k = h.shape
    n = wg.shape[1]
    bm, bn = _div128(m, FFN_TILE_M), _div128(n, FFN_TILE_N)

    def body(h_ref, wg_ref, wu_ref, g_ref, u_ref, a_ref):
        hv = h_ref[...]
        g = _dot(hv, wg_ref[...], _NN)
        u = _dot(hv, wu_ref[...], _NN)
        g_ref[...] = g
        u_ref[...] = u
        a_ref[...] = (g / (1.0 + jnp.exp(-g)) * u).astype(BF16)

    wspec = pl.BlockSpec((k, bn), lambda i, j: (0, j))
    ospec = pl.BlockSpec((bm, bn), lambda i, j: (i, j))
    return pl.pallas_call(
        body, name=name, grid=(m // bm, n // bn), in_specs=[pl.BlockSpec((bm, k), lambda i, j: (i, 0)), wspec, wspec],
        out_specs=(ospec, ospec, ospec),
        out_shape=(jax.ShapeDtypeStruct((m, n), F32), jax.ShapeDtypeStruct((m, n), F32),
                   jax.ShapeDtypeStruct((m, n), BF16)),
        compiler_params=_params("parallel", "parallel"),
    )(h, wg, wu)


def _down_bwd_act(dout, wd, gmat, umat, name):
    m, k = dout.shape
    n = wd.shape[0]
    bm, bn = _div128(m, FFN_TILE_M), _div128(n, FFN_TILE_N)

    def body(do_ref, wd_ref, g_ref, u_ref, dg_ref, du_ref):
        d = _dot(do_ref[...].astype(BF16), wd_ref[...], _NT)
        g, u = g_ref[...], u_ref[...]
        sig = 1.0 / (1.0 + jnp.exp(-g))
        dg_ref[...] = (d * u * (sig * (1.0 + g * (1.0 - sig)))).astype(BF16)
        du_ref[...] = (d * (g * sig)).astype(BF16)

    ospec = pl.BlockSpec((bm, bn), lambda i, j: (i, j))
    return pl.pallas_call(
        body, name=name, grid=(m // bm, n // bn),
        in_specs=[pl.BlockSpec((bm, k), lambda i, j: (i, 0)), pl.BlockSpec((bn, k), lambda i, j: (j, 0)), ospec, ospec],
        out_specs=(ospec, ospec), out_shape=(jax.ShapeDtypeStruct((m, n), BF16),) * 2,
        compiler_params=_params("parallel", "parallel"),
    )(dout, wd, gmat, umat)


def _make_ffn(name):
    def forward(x, g, wg, wu, wd):
        h = _rms_fwd(x, g, name + "_norm")
        gmat, umat, a = _gate_up_act(h, wg, wu, name + "_gate_up")
        return _mm(a, wd, "nn", name + "_down", res=x), (x, g, h, gmat, umat, a, wg, wu, wd)

    @jax.custom_vjp
    def op(x, g, wg_slot, wu_slot, wd_slot, wg, wu, wd):
        return forward(x, g, wg, wu, wd)[0]

    def fwd(x, g, wg_slot, wu_slot, wd_slot, wg, wu, wd):
        return forward(x, g, wg, wu, wd)

    def bwd(saved, dout):
        x, g, h, gmat, umat, a, wg, wu, wd = saved
        dgm, dum = _down_bwd_act(dout, wd, gmat, umat, name + "_da_act")
        dwd = _mm(a, dout, "tn", name + "_dwd", out_dtype=GRAD_WIRE_DTYPE)
        dwg = _mm(h, dgm, "tn", name + "_dwg", out_dtype=GRAD_WIRE_DTYPE)
        dwu = _mm(h, dum, "tn", name + "_dwu", out_dtype=GRAD_WIRE_DTYPE)
        dh = _mm(dum, wu, "nt", name + "_dh_u", res=_mm(dgm, wg, "nt", name + "_dh_g"))
        dx, dg = _rms_bwd(x, g, dh, name + "_norm_bwd", dres=dout)
        return dx, dg, dwg, dwu, dwd, None, None, None

    op.defvjp(fwd, bwd)
    return op


def _make_final_loss(name):
    def run(x, g, tgt):
        s, d = x.shape
        bs = _pick(s, (512, 256, 128))

        def body(x_ref, g_ref, t_ref, loss_ref, dx_ref, dg_ref):
            i = pl.program_id(0)
            xv = x_ref[...]
            gv = g_ref[...]
            r = lax.rsqrt(jnp.mean(xv * xv, axis=-1, keepdims=True) + NORM_EPS)
            xh = xv * r
            e = xh * gv - t_ref[...]
            dy = e * (1.0 / d)
            dxh = dy * gv
            dx_ref[...] = r * (dxh - xh * jnp.mean(dxh * xh, axis=-1, keepdims=True))
            part = 0.5 * jnp.sum(jnp.sum(e * e, axis=-1, keepdims=True) * (1.0 / d), axis=0, keepdims=True)

            @pl.when(i == 0)
            def _():
                loss_ref[...] = jnp.zeros_like(loss_ref)
                dg_ref[...] = jnp.zeros_like(dg_ref)

            loss_ref[...] += jnp.broadcast_to(part, loss_ref.shape)
            dg_ref[...] += jnp.sum(dy * xh, axis=0, keepdims=True)

        row = pl.BlockSpec((bs, d), lambda i: (i, 0))
        vec = pl.BlockSpec((1, d), lambda i: (0, 0))
        loss, dx, dg = pl.pallas_call(
            body, name=name, grid=(s // bs,), in_specs=[row, vec, row],
            out_specs=(pl.BlockSpec((8, LANES), lambda i: (0, 0)), row, vec),
            out_shape=(jax.ShapeDtypeStruct((8, LANES), F32), jax.ShapeDtypeStruct((s, d), F32),
                       jax.ShapeDtypeStruct((1, d), F32)),
            compiler_params=_params("arbitrary"),
        )(x, g.reshape(1, d), tgt)
        return loss[0, 0], dx, dg.reshape(d)

    @jax.custom_vjp
    def op(x, g, tgt):
        return run(x, g, tgt)[0]

    def fwd(x, g, tgt):
        loss, dx, dg = run(x, g, tgt)
        return loss, (dx, dg)

    def bwd(saved, ct):
        dx, dg = saved
        return dx * ct, dg * ct, None

    op.defvjp(fwd, bwd)
    return op


def _model_loss(diff, consts):
    x = diff["x"]
    w = consts["w"]
    slot = diff["slots"]
    vec = diff["vec"]
    tab64, tab_mla = consts["tab64"], consts["tab_mla"]
    mem = consts["mem"]
    s = x.shape[0]

    rope64 = lambda t, nm: _make_rope(HEAD_DIM // 2, nm)(t, *tab64)
    rope_mla = lambda t, nm: _make_rope(MLA_ROPE_DIM // 2, nm)(t, *tab_mla)

    def nl(nm, inp, gain, wname):
        return _make_norm_linear(nm)(inp, gain, slot[wname], w[wname])

    def nl_through(nm, inp, gain, wname):
        return _make_norm_linear_through(nm)(inp, gain, slot[wname], w[wname])

    def cross(layer, xin):
        p = "l%d_" % layer
        q, xin = nl_through(p + "xq", xin, vec[p + "x_norm"], p + "w_xq")
        kv = nl(p + "xkv", mem, vec[p + "mem_norm"], p + "w_xkv")
        half = X_HEADS * X_HEAD_DIM
        cfg = AttnCfg("full", X_HEAD_DIM ** -0.5, 1, _pick(s, (512, 256, 128)), kv.shape[0], 4)
        o = _make_attention(cfg, p + "xattn")(q, kv[:, :half], kv[:, half:])
        return _make_linear_res(p + "xo")(o, slot[p + "w_xo"], w[p + "w_xo"], xin)

    def ffn(layer, xin):
        p = "l%d_" % layer
        names = (p + "w_gate", p + "w_up", p + "w_down")
        return _make_ffn(p + "ffn")(xin, vec[p + "ffn_norm"], *(slot[n] for n in names), *(w[n] for n in names))

    z, x = nl_through("l0_in", x, vec["l0_mix_norm"], "l0_w_in")
    qa = rope64(z[:, :A_Q], "l0_rope_qa")
    ka = rope64(z[:, A_Q:A_Q + A_KV], "l0_rope_ka")
    va = z[:, A_Q + A_KV:A_Q + 2 * A_KV]
    rep = SWA_HEADS // SWA_KV_HEADS
    expand = lambda t: jnp.broadcast_to(t.reshape(s, SWA_KV_HEADS, 1, HEAD_DIM),
                                        (s, SWA_KV_HEADS, rep, HEAD_DIM)).reshape(s, A_Q)
    swa = _make_band_attention(HEAD_DIM ** -0.5, SWA_WINDOW - 1, BAND_UNITS_PER_STEP, "l0_swa")
    oa = swa(qa, expand(ka), expand(va), vec["l0_sinks"])

    c0 = A_Q + 2 * A_KV
    cq = z[:, c0:c0 + MLA_Q_RANK]
    ckv = z[:, c0 + MLA_Q_RANK:c0 + MLA_Q_RANK + MLA_KV_RANK]
    kr = z[:, c0 + MLA_Q_RANK + MLA_KV_RANK:EVEN_IN]
    qb = nl("l0_uq", cq, vec["l0_q_norm"], "l0_w_uq").reshape(s, MLA_HEADS, MLA_NOPE_DIM + MLA_ROPE_DIM)
    qfull = jnp.pad(qb, ((0, 0), (0, 0), (0, LANES - MLA_NOPE_DIM - MLA_ROPE_DIM))).reshape(s, MLA_HEADS * LANES)
    qfull = rope_mla(qfull, "l0_rope_q")
    kvb = nl("l0_ukv", ckv, vec["l0_kv_norm"], "l0_w_ukv")
    kvb3 = kvb.reshape(s, MLA_HEADS, LANES)
    kfull = jnp.concatenate(
        [kvb3[:, :, :MLA_NOPE_DIM], jnp.broadcast_to(kr[:, None, :], (s, MLA_HEADS, MLA_ROPE_DIM)),
         jnp.zeros((s, MLA_HEADS, LANES - MLA_NOPE_DIM - MLA_ROPE_DIM), F32)], axis=-1).reshape(s, MLA_HEADS * LANES)
    kfull = rope_mla(kfull, "l0_rope_k")
    mla = _make_causal_attention((MLA_NOPE_DIM + MLA_ROPE_DIM) ** -0.5, _pick(s, (1024, 512, 256, 128)), "l0_mla")
    ob = mla(qfull, kfull, kvb).reshape(s, MLA_HEADS, LANES)[:, :, MLA_NOPE_DIM:]
    o = jnp.concatenate([oa, ob.reshape(s, MLA_HEADS * HEAD_DIM)], axis=-1)
    x = _make_linear_res("l0_out")(o, slot["l0_w_out"], w["l0_w_out"], x)
    x = cross(0, x)
    x = ffn(0, x)

    qkv, x = nl_through("l1_qkv", x, vec["l1_mix_norm"], "l1_w_qkv")
    q = rope64(qkv[:, :D_MODEL], "l1_rope_q")
    k = rope64(qkv[:, D_MODEL:2 * D_MODEL], "l1_rope_k")
    o = _make_dilated("l1_dil")(q, k, qkv[:, 2 * D_MODEL:])
    x = _make_linear_res("l1_out")(o, slot["l1_w_out"], w["l1_w_out"], x)
    x = cross(1, x)
    x = ffn(1, x)

    return _make_final_loss("final_loss")(x, vec["final_norm"], consts["target"])


MESH_IDS = pl.DeviceIdType.MESH
HBM_SPEC = pl.BlockSpec(memory_space=pltpu.HBM)


def _my_place():
    return lax.axis_index("x"), lax.axis_index("y"), lax.axis_index("c")


def _flip(v, bit):
    return 1 - v if bit else v


def _all_gather_rows(shard):
    r, c_ = shard.shape

    def body(x_ref, out_ref, send_sems, recv_sems, local_sem):
        x, y, c = _my_place()
        me, sibling = (x, y, c), (x, y, 1 - c)
        chips = [(1 - x, y), (x, 1 - y), (1 - x, 1 - y)]

        def slot(px, py, pc):
            return out_ref.at[4 * px + 2 * py + pc]

        def copy(k, block, to, src=None):
            return pltpu.make_async_remote_copy(
                src_ref=slot(*block) if src is None else src, dst_ref=slot(*block), send_sem=send_sems.at[k],
                recv_sem=recv_sems.at[k], device_id=to, device_id_type=MESH_IDS)

        mine = pltpu.make_async_copy(x_ref, slot(*me), local_sem)
        mine.start()
        first = [copy(0, me, sibling, src=x_ref)]
        first += [copy(1 + j, me, (*chip, c), src=x_ref) for j, chip in enumerate(chips)]
        for cp in first:
            cp.start()
        passed = [copy(4 + j, (*chip, c), sibling) for j, chip in enumerate(chips)]
        for j, chip in enumerate(chips):
            copy(1 + j, (*chip, c), me).wait_recv()
            passed[j].start()
        copy(0, sibling, me).wait_recv()
        for j, chip in enumerate(chips):
            copy(4 + j, (*chip, 1 - c), me).wait_recv()
        for cp in first + passed:
            cp.wait_send()
        mine.wait()

    return pl.pallas_call(
        body, name="weights_all_gather", out_shape=jax.ShapeDtypeStruct((N_DEV, r, c_), shard.dtype),
        in_specs=[HBM_SPEC], out_specs=HBM_SPEC,
        scratch_shapes=[pltpu.SemaphoreType.DMA((7,)), pltpu.SemaphoreType.DMA((7,)), pltpu.SemaphoreType.DMA],
    )(shard)


N_CHIPS = 4


def _exchange_with_sibling(slabs):
    _, nq, r, c_ = slabs.shape

    def body(p_ref, out_ref, send_sem, recv_sem):
        x, y, c = _my_place()
        cp = pltpu.make_async_remote_copy(
            src_ref=p_ref.at[1 - c], dst_ref=out_ref, send_sem=send_sem, recv_sem=recv_sem,
            device_id=(x, y, 1 - c), device_id_type=MESH_IDS)
        cp.start()
        cp.wait_recv()
        cp.wait_send()

    return pl.pallas_call(
        body, name="grad_exchange_sibling", out_shape=jax.ShapeDtypeStruct((nq, r, c_), slabs.dtype),
        in_specs=[HBM_SPEC], out_specs=HBM_SPEC,
        scratch_shapes=[pltpu.SemaphoreType.DMA, pltpu.SemaphoreType.DMA],
    )(slabs)


def _add_pairs(a, b):
    nq, r, c_ = a.shape
    br = _pick(r, (256, 128, 64, 32, 16, 8))

    def body(a_ref, b_ref, o_ref):
        o_ref[...] = (a_ref[...].astype(F32) + b_ref[...].astype(F32)).astype(o_ref.dtype)

    blk = pl.BlockSpec((1, br, c_), lambda q, i: (q, i, 0))
    return pl.pallas_call(
        body, name="grad_chip_sum", grid=(nq, r // br), in_specs=[blk, blk], out_specs=blk,
        out_shape=jax.ShapeDtypeStruct(a.shape, a.dtype), compiler_params=_params("parallel", "parallel"),
    )(a, b)


def _exchange_between_chips(slabs):
    nq, r, c_ = slabs.shape

    def body(t_ref, out_ref, send_sems, recv_sems, local_sem):
        x, y, c = _my_place()
        myq = 2 * x + y
        local = pltpu.make_async_copy(t_ref.at[myq], out_ref.at[myq], local_sem)
        local.start()
        sends, recvs = [], []
        for k in range(1, N_CHIPS):
            px, py = _flip(x, k & 2), _flip(y, k & 1)
            peer = 2 * px + py
            sends.append(pltpu.make_async_remote_copy(
                src_ref=t_ref.at[peer], dst_ref=out_ref.at[myq], send_sem=send_sems.at[k - 1],
                recv_sem=recv_sems.at[k - 1], device_id=(px, py, c), device_id_type=MESH_IDS))
            recvs.append(pltpu.make_async_remote_copy(
                src_ref=t_ref.at[myq], dst_ref=out_ref.at[peer], send_sem=send_sems.at[k - 1],
                recv_sem=recv_sems.at[k - 1], device_id=(px, py, c), device_id_type=MESH_IDS))
        for cp in sends:
            cp.start()
        for cp in recvs:
            cp.wait_recv()
        for cp in sends:
            cp.wait_send()
        local.wait()

    return pl.pallas_call(
        body, name="grad_exchange_chips", out_shape=jax.ShapeDtypeStruct(slabs.shape, slabs.dtype),
        in_specs=[HBM_SPEC], out_specs=HBM_SPEC,
        scratch_shapes=[pltpu.SemaphoreType.DMA((N_CHIPS - 1,)), pltpu.SemaphoreType.DMA((N_CHIPS - 1,)),
                        pltpu.SemaphoreType.DMA],
    )(slabs)


def _all_reduce_small(v):
    r, c_ = v.shape

    def body(v_ref, out_ref, buf, send_sems, recv_sems):
        x, y, c = _my_place()
        me = 4 * x + 2 * y + c
        buf[me] = v_ref[...]
        sends, recvs = [], []
        for k in range(1, N_DEV):
            px, py, pc = _flip(x, k & 4), _flip(y, k & 2), _flip(c, k & 1)
            peer = 4 * px + 2 * py + pc
            sends.append(pltpu.make_async_remote_copy(
                src_ref=v_ref, dst_ref=buf.at[me], send_sem=send_sems.at[k - 1], recv_sem=recv_sems.at[k - 1],
                device_id=(px, py, pc), device_id_type=MESH_IDS))
            recvs.append(pltpu.make_async_remote_copy(
                src_ref=v_ref, dst_ref=buf.at[peer], send_sem=send_sems.at[k - 1], recv_sem=recv_sems.at[k - 1],
                device_id=(px, py, pc), device_id_type=MESH_IDS))
        for cp in sends:
            cp.start()
        for cp in recvs:
            cp.wait_recv()
        for cp in sends:
            cp.wait_send()
        acc = buf[0]
        for d in range(1, N_DEV):
            acc = acc + buf[d]
        out_ref[...] = acc

    vm = pl.BlockSpec(memory_space=pltpu.VMEM)
    return pl.pallas_call(
        body, name="vector_grad_all_reduce", out_shape=jax.ShapeDtypeStruct((r, c_), F32), in_specs=[vm], out_specs=vm,
        scratch_shapes=[pltpu.VMEM((N_DEV, r, c_), F32), pltpu.SemaphoreType.DMA((7,)), pltpu.SemaphoreType.DMA((7,))],
    )(v)


def _adamw_math(w, g, m, v):
    m = ADAM_B1 * m + (1.0 - ADAM_B1) * g
    v = ADAM_B2 * v + (1.0 - ADAM_B2) * (g * g)
    m_hat = m / (1.0 - ADAM_B1 ** ADAM_STEP)
    v_hat = v / (1.0 - ADAM_B2 ** ADAM_STEP)
    delta = -ADAM_LR * (m_hat / (jnp.sqrt(v_hat) + ADAM_EPS) + ADAM_WD * w)
    return delta, m, v


def _sum_and_adamw(parts, w, m, v):
    nparts, r, c_ = parts.shape
    br = _pick(r, (256, 128, 64, 32, 16, 8))

    def body(p_ref, w_ref, m_ref, v_ref, g_ref, d_ref, nm_ref, nv_ref):
        g = p_ref[0].astype(F32)
        for d in range(1, nparts):
            g = g + p_ref[d].astype(F32)
        g_ref[...] = g
        d_ref[...], nm_ref[...], nv_ref[...] = _adamw_math(w_ref[...], g, m_ref[...], v_ref[...])

    row = pl.BlockSpec((br, c_), lambda i: (i, 0))
    return pl.pallas_call(
        body, name="grad_sum_adamw", grid=(r // br,),
        in_specs=[pl.BlockSpec((nparts, br, c_), lambda i: (0, i, 0)), row, row, row], out_specs=(row,) * 4,
        out_shape=(jax.ShapeDtypeStruct((r, c_), F32),) * 4, compiler_params=_params("parallel"),
    )(parts, w, m, v)


def _adamw_small(w, g, m, v):
    vm = pl.BlockSpec(memory_space=pltpu.VMEM)

    def body(w_ref, g_ref, m_ref, v_ref, d_ref, nm_ref, nv_ref):
        d_ref[...], nm_ref[...], nv_ref[...] = _adamw_math(w_ref[...], g_ref[...], m_ref[...], v_ref[...])

    return pl.pallas_call(
        body, name="vector_adamw", in_specs=[vm] * 4, out_specs=(vm,) * 3,
        out_shape=(jax.ShapeDtypeStruct(w.shape, F32),) * 3,
    )(w, g, m, v)


def _pad_rows(t, axis):
    extra = -t.shape[axis] % PART_ROW_ALIGN
    if extra == 0:
        return t
    widths = [(0, 0)] * t.ndim
    widths[axis] = (0, extra)
    return jnp.pad(t, widths)


def _pack_local(named):
    rows = [_pad_rows(named[n].reshape(-1, PACK_COLS), 0) for n, _, _, _ in MATRICES]
    rows.append(jnp.zeros((MAT_ROWS - MAT_ROWS_USED, PACK_COLS), rows[0].dtype))
    return jnp.concatenate(rows, axis=0)


def _unpack_local(packed):
    out, r0 = {}, 0
    for n, kind, k, nn in MATRICES:
        nr = k * nn // N_DEV // PACK_COLS
        shape = (k, nn // N_DEV) if kind == "c" else (k // N_DEV, nn)
        out[n] = packed[r0:r0 + nr].reshape(shape)
        r0 += _part_rows(k, nn)
    return out


def _unpack_gathered(g):
    out, r0 = {}, 0
    for n, kind, k, nn in MATRICES:
        nr = k * nn // N_DEV // PACK_COLS
        blk = g[:, r0:r0 + nr]
        if kind == "c":
            out[n] = blk.reshape(N_DEV, k, nn // N_DEV).transpose(1, 0, 2).reshape(k, nn)
        else:
            out[n] = blk.reshape(k, nn)
        r0 += _part_rows(k, nn)
    return out


def _pack_full_grads(grads):
    rows = []
    for n, kind, k, nn in MATRICES:
        gmat = grads[n]
        if kind == "c":
            gmat = gmat.reshape(k, N_CHIPS, 2, nn // N_DEV).transpose(2, 1, 0, 3)
        else:
            gmat = gmat.reshape(N_CHIPS, 2, k // N_DEV, nn).transpose(1, 0, 2, 3)
        rows.append(_pad_rows(gmat.reshape(2, N_CHIPS, -1, PACK_COLS), 2))
    rows.append(jnp.zeros((2, N_CHIPS, MAT_ROWS - MAT_ROWS_USED, PACK_COLS), rows[0].dtype))
    return jnp.concatenate(rows, axis=2)


def _pack_vectors(named):
    rows = [jnp.pad(named[n].astype(F32), (0, PACK_COLS - d)) for n, d in VECTORS]
    rows += [jnp.zeros((PACK_COLS,), F32)] * (VEC_ROWS - len(VECTORS))
    return jnp.stack(rows, axis=0)


def _unpack_vectors(packed):
    return {n: packed[i, :d] for i, (n, d) in enumerate(VECTORS)}


def _step(inputs):
    x = inputs["x"][0]
    mem = inputs["mem"][0]
    positions = inputs["positions"][0]
    target = inputs["loss_target"][0]

    local_w = _pack_local({n: inputs[n] for n, _, _, _ in MATRICES})
    gathered = _all_gather_rows(local_w.astype(BF16))
    wfull = _unpack_gathered(gathered)
    vec = {n: inputs[n] for n, _ in VECTORS}

    loss_part, grad_x, gfull, gvec = _local_grads(wfull, vec, x, mem, positions, target)
    loss = lax.psum(loss_part, ("x", "y", "c"))

    slabs = _pack_full_grads(gfull)
    from_sibling = _exchange_with_sibling(slabs)
    mine = lax.dynamic_index_in_dim(slabs, lax.axis_index("c"), axis=0, keepdims=False)
    parts = _exchange_between_chips(_add_pairs(mine, from_sibling))
    local_m = _pack_local({n: inputs["m_" + n] for n, _, _, _ in MATRICES})
    local_v = _pack_local({n: inputs["v_" + n] for n, _, _, _ in MATRICES})
    g_pk, d_pk, m_pk, v_pk = _sum_and_adamw(parts, local_w, local_m, local_v)
    g_mat, d_mat, m_mat, v_mat = (_unpack_local(t) for t in (g_pk, d_pk, m_pk, v_pk))

    g_vec_pk = _all_reduce_small(_pack_vectors(gvec))
    d_vec_pk, m_vec_pk, v_vec_pk = _adamw_small(
        _pack_vectors(vec), g_vec_pk, _pack_vectors({n: inputs["m_" + n] for n, _ in VECTORS}),
        _pack_vectors({n: inputs["v_" + n] for n, _ in VECTORS}))
    g_vec, d_vec, m_vec, v_vec = (_unpack_vectors(t) for t in (g_vec_pk, d_vec_pk, m_vec_pk, v_vec_pk))

    def pick(mats, vecs, n):
        return mats[n] if n in mats else vecs[n]

    outs = [loss, grad_x[None]]
    for mats, vecs in ((g_mat, g_vec), (d_mat, d_vec), (m_mat, m_vec), (v_mat, v_vec)):
        outs += [pick(mats, vecs, n) for n in WEIGHT_ORDER]
    return tuple(outs)


def _local_grads(wfull, vec, x, mem, positions, target):
    w = dict(wfull)
    w["l0_w_in"] = jnp.pad(wfull["l0_w_in"], ((0, 0), (0, EVEN_IN_PAD - EVEN_IN)))
    slots = {n: jnp.zeros(t.shape, GRAD_WIRE_DTYPE) for n, t in w.items()}

    tab64 = _rope_tables(positions, HEAD_DIM, 0, HEAD_DIM)
    tab_mla = _rope_tables(positions, MLA_ROPE_DIM, MLA_NOPE_DIM, LANES)
    diff = {"x": x, "slots": slots, "vec": vec}
    consts = {"w": w, "mem": mem, "tab64": tab64, "tab_mla": tab_mla, "target": target}
    loss_part, grads = jax.value_and_grad(_model_loss)(diff, consts)

    gfull = dict(grads["slots"])
    gfull["l0_w_in"] = gfull["l0_w_in"][:, :EVEN_IN]
    return loss_part, grads["x"], gfull, grads["vec"]


_INPUT_NAMES = (("x", "mem", "positions") + WEIGHT_ORDER + ("loss_target",)
                + tuple("m_" + n for n in WEIGHT_ORDER) + tuple("v_" + n for n in WEIGHT_ORDER))


def kernel(*args):
    assert len(args) == len(_INPUT_NAMES)
    return _step(dict(zip(_INPUT_NAMES, args)))
```

```python
import functools
import math

import numpy as np
import jax
import jax.numpy as jnp
from jax import lax
from jax.experimental import pallas as pl
from jax.experimental.pallas import tpu as pltpu

F32 = jnp.float32
BF16 = jnp.bfloat16

LANES = 128
VMEM_LIMIT_BYTES = 56 * 1024 * 1024
MM_VMEM_BUDGET = 40 * 1024 * 1024
MM_MIN_FLOP_PER_STEP = 1e9
BAND_UNITS_PER_STEP = 4
CAUSAL_ROW_CHAIN = 128
BAND_CHAINS_PER_BATCH = 4

D_MODEL = 1024
HEAD_DIM = 64
ROPE_THETA = 10000.0
NORM_EPS = 1e-6
BLOCK = 128
SWA_HEADS = 8
SWA_KV_HEADS = 2
SWA_WINDOW = 128
MLA_HEADS = 8
MLA_Q_RANK = 384
MLA_KV_RANK = 256
MLA_NOPE_DIM = 64
MLA_ROPE_DIM = 32
A_Q = SWA_HEADS * HEAD_DIM
A_KV = SWA_KV_HEADS * HEAD_DIM
EVEN_IN = A_Q + 2 * A_KV + MLA_Q_RANK + MLA_KV_RANK + MLA_ROPE_DIM
EVEN_IN_PAD = 1536
DIL_PATTERNS = ((128, 1), (512, 4), (2048, 16))
X_HEADS = 4
X_HEAD_DIM = 128
FFN_HIDDEN = 2816

ADAM_LR = 0.001
ADAM_B1 = 0.9
ADAM_B2 = 0.999
ADAM_EPS = 1e-08
ADAM_WD = 0.01
ADAM_STEP = 10

N_DEV = 8
GRAD_WIRE_DTYPE = BF16
NEG_MASK = -1e30
NEG_INIT = -1e20

MATRICES = (
    ("l0_w_in", "c", 1024, 1440), ("l0_w_uq", "c", 384, 768), ("l0_w_ukv", "c", 256, 1024),
    ("l0_w_out", "r", 1024, 1024), ("l0_w_xq", "r", 1024, 512), ("l0_w_xkv", "r", 1024, 1024),
    ("l0_w_xo", "c", 512, 1024), ("l0_w_gate", "c", 1024, 2816), ("l0_w_up", "c", 1024, 2816),
    ("l0_w_down", "r", 2816, 1024),
    ("l1_w_qkv", "c", 1024, 3072), ("l1_w_out", "r", 1024, 1024), ("l1_w_xq", "r", 1024, 512),
    ("l1_w_xkv", "r", 1024, 1024), ("l1_w_xo", "c", 512, 1024), ("l1_w_gate", "c", 1024, 2816),
    ("l1_w_up", "c", 1024, 2816), ("l1_w_down", "r", 2816, 1024),
)
VECTORS = (
    ("l0_mix_norm", 1024), ("l0_sinks", 8), ("l0_q_norm", 384), ("l0_kv_norm", 256), ("l0_x_norm", 1024),
    ("l0_mem_norm", 1024), ("l0_ffn_norm", 1024), ("l1_mix_norm", 1024), ("l1_x_norm", 1024),
    ("l1_mem_norm", 1024), ("l1_ffn_norm", 1024), ("final_norm", 1024),
)
WEIGHT_ORDER = (
    "l0_mix_norm", "l0_w_in", "l0_sinks", "l0_q_norm", "l0_w_uq", "l0_kv_norm", "l0_w_ukv", "l0_w_out", "l0_x_norm",
    "l0_mem_norm", "l0_w_xq", "l0_w_xkv", "l0_w_xo", "l0_ffn_norm", "l0_w_gate", "l0_w_up", "l0_w_down",
    "l1_mix_norm", "l1_w_qkv", "l1_w_out", "l1_x_norm", "l1_mem_norm", "l1_w_xq", "l1_w_xkv", "l1_w_xo",
    "l1_ffn_norm", "l1_w_gate", "l1_w_up", "l1_w_down", "final_norm",
)
PACK_COLS = 1024
PART_ROW_ALIGN = 16


def _part_rows(k, n):
    return -(-(k * n // N_DEV // PACK_COLS) // PART_ROW_ALIGN) * PART_ROW_ALIGN


MAT_ROWS_USED = sum(_part_rows(k, n) for _, _, k, n in MATRICES)
MAT_ROWS = -(-MAT_ROWS_USED // 256) * 256
VEC_ROWS = 16


def _pick(n, cands):
    for c in cands:
        if n % c == 0:
            return c
    return n


def _params(*sem):
    return pltpu.CompilerParams(dimension_semantics=sem, vmem_limit_bytes=VMEM_LIMIT_BYTES)


_DIMS = {"nn": (((1,), (0,)), ((), ())), "nt": (((1,), (1,)), ((), ())), "tn": (((0,), (0,)), ((), ()))}


def _div128(n, cap):
    d = (min(n, cap) // LANES) * LANES
    while d >= LANES:
        if n % d == 0:
            return d
        d -= LANES
    return n


def _mm_vmem_bytes(bm, bn, bk, nk, sa, sb, so, has_res):
    est = 2 * (bm * bk * sa + bk * bn * sb + bm * bn * so) + bm * bn * 4
    est += bm * bn * 4 if nk > 1 else 0
    est += 2 * bm * bn * 4 if has_res else 0
    est += bm * bk * 2 if sa == 4 else 0
    est += bk * bn * 2 if sb == 4 else 0
    return est


def _mm_tiles(m, n, k, sa, sb, so, has_res, mode):
    bn = _div128(n, 1536)
    kcap = 2048 if mode == "tn" else k
    for bm_cap in ((1024, 2048) if mode == "tn" else (512, 1024, 2048)):
        bm = _div128(m, bm_cap)
        bk = (min(k, kcap) // LANES) * LANES
        while bk > LANES and (k % bk or _mm_vmem_bytes(bm, bn, bk, k // bk, sa, sb, so, has_res) > MM_VMEM_BUDGET):
            bk -= LANES
        if 2 * bm * bn * bk >= MM_MIN_FLOP_PER_STEP or bm == m:
            break
    return bm, bn, bk


def _mm(a, b, mode, name, out_dtype=F32, res=None):
    if mode == "nn":
        (m, k), (k2, n) = a.shape, b.shape
    elif mode == "nt":
        (m, k), (n, k2) = a.shape, b.shape
    else:
        (k, m), (k2, n) = a.shape, b.shape
    assert k == k2, (name, a.shape, b.shape)
    has_res = res is not None
    bm, bn, bk = _mm_tiles(m, n, k, a.dtype.itemsize, b.dtype.itemsize, jnp.dtype(out_dtype).itemsize, has_res, mode)
    nk = k // bk
    dims = _DIMS[mode]
    a_spec = pl.BlockSpec((bk, bm), lambda i, j, kk: (kk, i)) if mode == "tn" else pl.BlockSpec((bm, bk), lambda i, j, kk: (i, kk))
    b_spec = pl.BlockSpec((bn, bk), lambda i, j, kk: (j, kk)) if mode == "nt" else pl.BlockSpec((bk, bn), lambda i, j, kk: (kk, j))
    o_spec = pl.BlockSpec((bm, bn), lambda i, j, kk: (i, j))

    def body(*refs):
        a_ref, b_ref = refs[0], refs[1]
        r_ref = refs[2] if has_res else None
        o_ref = refs[3] if has_res else refs[2]
        part = lax.dot_general(a_ref[...].astype(BF16), b_ref[...].astype(BF16), dims, preferred_element_type=F32)
        if nk == 1:
            o_ref[...] = (part + r_ref[...] if has_res else part).astype(out_dtype)
            return
        acc = refs[-1]
        kk = pl.program_id(2)

        @pl.when(kk == 0)
        def _():
            acc[...] = part

        @pl.when(jnp.logical_and(kk > 0, kk < nk - 1))
        def _():
            acc[...] += part

        @pl.when(kk == nk - 1)
        def _():
            r = acc[...] + part
            if has_res:
                r = r + r_ref[...]
            o_ref[...] = r.astype(out_dtype)

    args = (a, b, res) if has_res else (a, b)
    in_specs = [a_spec, b_spec] + ([o_spec] if has_res else [])
    return pl.pallas_call(
        body, name=name, grid=(m // bm, n // bn, nk), in_specs=in_specs, out_specs=o_spec,
        out_shape=jax.ShapeDtypeStruct((m, n), out_dtype),
        scratch_shapes=[pltpu.VMEM((bm, bn), F32)] if nk > 1 else [],
        compiler_params=_params("parallel", "parallel", "arbitrary"),
    )(*args)


def _rms_fwd(x, g, name, out_dtype=BF16):
    s, d = x.shape
    bs = _pick(s, (512, 256, 128))

    def body(x_ref, g_ref, o_ref):
        xv = x_ref[...]
        r = lax.rsqrt(jnp.mean(xv * xv, axis=-1, keepdims=True) + NORM_EPS)
        o_ref[...] = ((xv * r) * g_ref[...]).astype(out_dtype)

    return pl.pallas_call(
        body, name=name, grid=(s // bs,),
        in_specs=[pl.BlockSpec((bs, d), lambda i: (i, 0)), pl.BlockSpec((1, d), lambda i: (0, 0))],
        out_specs=pl.BlockSpec((bs, d), lambda i: (i, 0)), out_shape=jax.ShapeDtypeStruct((s, d), out_dtype),
        compiler_params=_params("parallel"),
    )(x, g.reshape(1, d))


def _rms_bwd(x, g, dy, name, dres=None):
    s, d = x.shape
    bs = _pick(s, (512, 256, 128))
    has_res = dres is not None

    def body(*refs):
        if has_res:
            x_ref, g_ref, dy_ref, r_ref, dx_ref, dg_ref = refs
        else:
            x_ref, g_ref, dy_ref, dx_ref, dg_ref = refs
        i = pl.program_id(0)
        xv = x_ref[...]
        dy = dy_ref[...]
        r = lax.rsqrt(jnp.mean(xv * xv, axis=-1, keepdims=True) + NORM_EPS)
        xh = xv * r
        dxh = dy * g_ref[...]
        dx = r * (dxh - xh * jnp.mean(dxh * xh, axis=-1, keepdims=True))
        if has_res:
            dx = dx + r_ref[...]
        dx_ref[...] = dx

        @pl.when(i == 0)
        def _():
            dg_ref[...] = jnp.zeros_like(dg_ref)

        dg_ref[...] += jnp.sum(dy * xh, axis=0, keepdims=True)

    row = pl.BlockSpec((bs, d), lambda i: (i, 0))
    vec = pl.BlockSpec((1, d), lambda i: (0, 0))
    args = (x, g.reshape(1, d), dy) + ((dres,) if has_res else ())
    dx, dg = pl.pallas_call(
        body, name=name, grid=(s // bs,), in_specs=[row, vec, row] + ([row] if has_res else []),
        out_specs=(row, vec), out_shape=(jax.ShapeDtypeStruct((s, d), F32), jax.ShapeDtypeStruct((1, d), F32)),
        compiler_params=_params("arbitrary"),
    )(*args)
    return dx, dg.reshape(d)


def _rope_tables(positions, dh, offset, period):
    role = np.zeros(LANES, np.int32)
    for base in range(0, LANES, period):
        role[base + offset:base + offset + dh // 2] = 1
        role[base + offset + dh // 2:base + offset + dh] = 2
    inv_freq = ROPE_THETA ** (-jnp.arange(0, dh, 2, dtype=F32) / dh)
    one_period = jnp.concatenate([jnp.zeros((offset,), F32), inv_freq, inv_freq,
                                  jnp.zeros((period - offset - dh,), F32)])
    ang = positions.astype(F32)[:, None] * jnp.tile(one_period, LANES // period)[None, :]
    c, s = jnp.cos(ang), jnp.sin(ang)
    role = role[None, :]
    a = jnp.where(role == 0, 1.0, c).astype(F32)
    bm = jnp.where(role == 2, s, 0.0).astype(F32)
    bp = jnp.where(role == 1, -s, 0.0).astype(F32)
    return a, bm, bp


def _rope_apply(x, tabs, half, transpose, name, shared=None, sum_blocks=False):
    s, w = x.shape
    bs = _pick(s, (512, 256, 128))
    nc = w // LANES
    a, bm, bp = tabs
    has_shared = shared is not None

    def body(*refs):
        x_ref, a_ref, bm_ref, bp_ref = refs[:4]
        o_ref = refs[5] if has_shared else refs[4]
        av, bmv, bpv = a_ref[...], bm_ref[...], bp_ref[...]
        total = None
        for c in range(nc):
            sl = slice(c * LANES, (c + 1) * LANES)
            xv = x_ref[:, sl]
            if has_shared:
                xv = xv + refs[4][...]
            if transpose:
                out = xv * av + pltpu.roll(xv * bmv, LANES - half, 1) + pltpu.roll(xv * bpv, half, 1)
            else:
                out = xv * av + pltpu.roll(xv, half, 1) * bmv + pltpu.roll(xv, LANES - half, 1) * bpv
            o_ref[:, sl] = out
            total = out if total is None else total + out
        if sum_blocks:
            refs[-1][...] = total

    row = pl.BlockSpec((bs, w), lambda i: (i, 0))
    tab = pl.BlockSpec((bs, LANES), lambda i: (i, 0))
    out_shape = jax.ShapeDtypeStruct((s, w), F32)
    return pl.pallas_call(
        body, name=name, grid=(s // bs,), in_specs=[row, tab, tab, tab] + ([tab] if has_shared else []),
        out_specs=(row, tab) if sum_blocks else row,
        out_shape=(out_shape, jax.ShapeDtypeStruct((s, LANES), F32)) if sum_blocks else out_shape,
        compiler_params=_params("parallel"),
    )(x, a, bm, bp, *((shared,) if has_shared else ()))


def _make_rope(half, name):
    @jax.custom_vjp
    def rope(x, a, bm, bp):
        return _rope_apply(x, (a, bm, bp), half, False, name + "_fwd")

    def fwd(x, a, bm, bp):
        return rope(x, a, bm, bp), (a, bm, bp)

    def bwd(tabs, dy):
        return _rope_apply(dy, tabs, half, True, name + "_bwd"), None, None, None

    rope.defvjp(fwd, bwd)
    return rope


def _make_rope_shared(half, name):
    @jax.custom_vjp
    def rope(x, shared, a, bm, bp):
        return _rope_apply(x, (a, bm, bp), half, False, name + "_fwd", shared=shared)

    def fwd(x, shared, a, bm, bp):
        return rope(x, shared, a, bm, bp), (a, bm, bp)

    def bwd(tabs, dy):
        dx, dshared = _rope_apply(dy, tabs, half, True, name + "_bwd", sum_blocks=True)
        return dx, dshared, None, None, None

    rope.defvjp(fwd, bwd)
    return rope


class AttnCfg:
    def __init__(self, mode, scale, hpb, bq, bk, upb, max_dist=0):
        self.mode, self.scale, self.hpb, self.bq, self.bk, self.upb, self.max_dist = mode, scale, hpb, bq, bk, upb, max_dist


def _kv_of_q(cfg, nq, nk):
    if cfg.mode == "causal":
        return nk, lambda i, j: (jnp.minimum(j, i), j <= i)
    if cfg.mode == "band":
        return 2, lambda i, j: (jnp.maximum(i - 1 + j, 0), i - 1 + j >= 0)
    return nk, lambda i, j: (j, j >= 0)


def _q_of_kv(cfg, nq, nk):
    if cfg.mode == "causal":
        return nq, lambda kb, j: (jnp.maximum(j, kb), j >= kb)
    if cfg.mode == "band":
        return 2, lambda kb, j: (jnp.minimum(kb + j, nq - 1), kb + j <= nq - 1)
    return nq, lambda kb, j: (j, j >= 0)


def _attn_mask(cfg, i, kb):
    if cfg.mode == "full":
        return None
    qpos = i * cfg.bq + lax.broadcasted_iota(jnp.int32, (cfg.bq, cfg.bk), 0)
    kpos = kb * cfg.bk + lax.broadcasted_iota(jnp.int32, (cfg.bq, cfg.bk), 1)
    dist = qpos - kpos
    if cfg.mode == "causal":
        return dist >= 0
    return (dist >= 0) & (dist <= cfg.max_dist)


def _lane_masks():
    lane = lax.broadcasted_iota(jnp.int32, (1, LANES), 1)
    lo = lane < HEAD_DIM
    return [lo, jnp.logical_not(lo)]


def _sel(mask, v):
    return jnp.where(mask, v, jnp.zeros_like(v))


_NT = (((1,), (1,)), ((), ()))
_NN = (((1,), (0,)), ((), ()))
_TN = (((0,), (0,)), ((), ()))
_BNT = (((2,), (2,)), ((0,), (0,)))
_BNN = (((2,), (1,)), ((0,), (0,)))


def _dot(a, b, dims):
    return lax.dot_general(a, b, dims, preferred_element_type=F32)


def _attn_fwd(q, k, v, sinkrow, cfg, name):
    sq, w = q.shape
    sk = k.shape[0]
    bq, bk, upb, hpb = cfg.bq, cfg.bk, cfg.upb, cfg.hpb
    nq, nk, nub = sq // bq, sk // bk, w // (LANES * upb)
    nj, sched = _kv_of_q(cfg, nq, nk)
    wb = LANES * upb
    has_sink = sinkrow is not None

    def body(*refs):
        if has_sink:
            q_ref, k_ref, v_ref, s_ref, o_ref, l_ref, m_sc, l_sc, acc_sc = refs
        else:
            q_ref, k_ref, v_ref, o_ref, l_ref, m_sc, l_sc, acc_sc = refs
        i, j = pl.program_id(1), pl.program_id(2)
        kb, active = sched(i, j)
        lms = _lane_masks()

        @pl.when(j == 0)
        def _():
            for u in range(upb):
                for a in range(hpb):
                    if has_sink:
                        srow = s_ref[:, u * LANES:(u + 1) * LANES]
                        sk_a = jnp.max(jnp.where(lms[a], srow, -jnp.inf), axis=-1, keepdims=True)
                        m_sc[u * hpb + a] = jnp.broadcast_to(sk_a, (bq, 1))
                        l_sc[u * hpb + a] = jnp.ones((bq, 1), F32)
                    else:
                        m_sc[u * hpb + a] = jnp.full((bq, 1), NEG_INIT, F32)
                        l_sc[u * hpb + a] = jnp.zeros((bq, 1), F32)
            acc_sc[...] = jnp.zeros_like(acc_sc)

        @pl.when(active)
        def _():
            mask = _attn_mask(cfg, i, kb)
            for u in range(upb):
                sl = slice(u * LANES, (u + 1) * LANES)
                qv = q_ref[:, sl].astype(BF16)
                kv = k_ref[:, sl].astype(BF16)
                vv = v_ref[:, sl].astype(BF16)
                pv_tot, alphas = None, []
                for a in range(hpb):
                    idx = u * hpb + a
                    qa = _sel(lms[a], qv) if hpb == 2 else qv
                    s = _dot(qa, kv, _NT) * cfg.scale
                    if mask is not None:
                        s = jnp.where(mask, s, NEG_MASK)
                    m_prev = m_sc[idx]
                    m_new = jnp.maximum(m_prev, jnp.max(s, axis=-1, keepdims=True))
                    alpha = jnp.exp(m_prev - m_new)
                    p = jnp.exp(s - m_new)
                    l_sc[idx] = alpha * l_sc[idx] + jnp.sum(p, axis=-1, keepdims=True)
                    m_sc[idx] = m_new
                    va = _sel(lms[a], vv) if hpb == 2 else vv
                    pv = _dot(p.astype(BF16), va, _NN)
                    pv_tot = pv if pv_tot is None else pv_tot + pv
                    alphas.append(alpha)
                af = alphas[0] if hpb == 1 else jnp.where(lms[0], alphas[0], alphas[1])
                acc_sc[u] = acc_sc[u] * af + pv_tot

        @pl.when(j == nj - 1)
        def _():
            for u in range(upb):
                sl = slice(u * LANES, (u + 1) * LANES)
                if hpb == 1:
                    lf = jnp.broadcast_to(l_sc[u], (bq, LANES))
                    mf = jnp.broadcast_to(m_sc[u], (bq, LANES))
                else:
                    lf = jnp.where(lms[0], l_sc[2 * u], l_sc[2 * u + 1])
                    mf = jnp.where(lms[0], m_sc[2 * u], m_sc[2 * u + 1])
                o_ref[:, sl] = acc_sc[u] / lf
                l_ref[:, sl] = mf + jnp.log(lf)

    qspec = pl.BlockSpec((bq, wb), lambda ub, i, j: (i, ub))
    kspec = pl.BlockSpec((bk, wb), lambda ub, i, j: (sched(i, j)[0], ub))
    in_specs = [qspec, kspec, kspec] + ([pl.BlockSpec((1, wb), lambda ub, i, j: (0, ub))] if has_sink else [])
    args = (q, k, v) + ((sinkrow,) if has_sink else ())
    return pl.pallas_call(
        body, name=name, grid=(nub, nq, nj), in_specs=in_specs, out_specs=(qspec, qspec),
        out_shape=(jax.ShapeDtypeStruct((sq, w), F32), jax.ShapeDtypeStruct((sq, w), F32)),
        scratch_shapes=[pltpu.VMEM((upb * hpb, bq, 1), F32), pltpu.VMEM((upb * hpb, bq, 1), F32),
                        pltpu.VMEM((upb, bq, LANES), F32)],
        compiler_params=_params("parallel", "parallel", "arbitrary"),
    )(*args)


def _softmax_grad_terms(cfg, lms, a, qv, kv, vv, dob, prod, lv, mask):
    hpb = cfg.hpb
    if hpb == 2:
        t = jnp.sum(_sel(lms[a], prod), axis=-1, keepdims=True)
        lse = jnp.max(jnp.where(lms[a], lv, -jnp.inf), axis=-1, keepdims=True)
        qa, doa = _sel(lms[a], qv), _sel(lms[a], dob)
    else:
        t = jnp.sum(prod, axis=-1, keepdims=True)
        lse = jnp.max(lv, axis=-1, keepdims=True)
        qa, doa = qv, dob
    s = _dot(qa, kv, _NT) * cfg.scale
    if mask is not None:
        s = jnp.where(mask, s, NEG_MASK)
    p = jnp.exp(s - lse)
    dp = _dot(doa, vv, _NT)
    ds = (p * (dp - t)) * cfg.scale
    return p, ds, qa, doa, t


def _attn_dq(q, k, v, o, lse, do, sinkrow, cfg, name):
    sq, w = q.shape
    sk = k.shape[0]
    bq, bk, upb, hpb = cfg.bq, cfg.bk, cfg.upb, cfg.hpb
    nq, nk, nub = sq // bq, sk // bk, w // (LANES * upb)
    nj, sched = _kv_of_q(cfg, nq, nk)
    wb = LANES * upb
    has_sink = sinkrow is not None

    def body(*refs):
        if has_sink:
            q_ref, k_ref, v_ref, o_ref, l_ref, do_ref, s_ref, dq_ref, dsink_ref, acc = refs
        else:
            q_ref, k_ref, v_ref, o_ref, l_ref, do_ref, dq_ref, acc = refs
        i, j = pl.program_id(1), pl.program_id(2)
        kb, active = sched(i, j)
        lms = _lane_masks()

        @pl.when(j == 0)
        def _():
            acc[...] = jnp.zeros_like(acc)

        @pl.when(active)
        def _():
            mask = _attn_mask(cfg, i, kb)
            for u in range(upb):
                sl = slice(u * LANES, (u + 1) * LANES)
                qv = q_ref[:, sl].astype(BF16)
                kv = k_ref[:, sl].astype(BF16)
                vv = v_ref[:, sl].astype(BF16)
                dov = do_ref[:, sl]
                prod = dov * o_ref[:, sl]
                dob = dov.astype(BF16)
                lv = l_ref[:, sl]
                tot = None
                for a in range(hpb):
                    _, ds, _, _, _ = _softmax_grad_terms(cfg, lms, a, qv, kv, vv, dob, prod, lv, mask)
                    ka = _sel(lms[a], kv) if hpb == 2 else kv
                    c = _dot(ds.astype(BF16), ka, _NN)
                    tot = c if tot is None else tot + c
                acc[u] = acc[u] + tot

        @pl.when(j == nj - 1)
        def _():
            for u in range(upb):
                dq_ref[:, u * LANES:(u + 1) * LANES] = acc[u]
            if has_sink:
                @pl.when(i == 0)
                def _():
                    dsink_ref[...] = jnp.zeros_like(dsink_ref)

                for u in range(upb):
                    sl = slice(u * LANES, (u + 1) * LANES)
                    prod = do_ref[:, sl] * o_ref[:, sl]
                    t0 = jnp.sum(_sel(lms[0], prod), axis=-1, keepdims=True)
                    t1 = jnp.sum(_sel(lms[1], prod), axis=-1, keepdims=True)
                    tf = jnp.where(lms[0], t0, t1)
                    rs = -jnp.exp(s_ref[:, sl] - l_ref[:, sl]) * tf
                    dsink_ref[0:1, sl] += jnp.sum(rs, axis=0, keepdims=True)

    qspec = pl.BlockSpec((bq, wb), lambda ub, i, j: (i, ub))
    kspec = pl.BlockSpec((bk, wb), lambda ub, i, j: (sched(i, j)[0], ub))
    in_specs = [qspec, kspec, kspec, qspec, qspec, qspec]
    args = (q, k, v, o, lse, do)
    out_specs = qspec
    out_shape = jax.ShapeDtypeStruct((sq, w), F32)
    sem = ("parallel", "parallel", "arbitrary")
    if has_sink:
        in_specs = in_specs + [pl.BlockSpec((1, wb), lambda ub, i, j: (0, ub))]
        args = args + (sinkrow,)
        out_specs = (qspec, pl.BlockSpec((8, wb), lambda ub, i, j: (0, ub)))
        out_shape = (out_shape, jax.ShapeDtypeStruct((8, w), F32))
        sem = ("parallel", "arbitrary", "arbitrary")
    return pl.pallas_call(
        body, name=name, grid=(nub, nq, nj), in_specs=in_specs, out_specs=out_specs, out_shape=out_shape,
        scratch_shapes=[pltpu.VMEM((upb, bq, LANES), F32)], compiler_params=_params(*sem),
    )(*args)


def _attn_dkv(q, k, v, o, lse, do, cfg, name):
    sq, w = q.shape
    sk = k.shape[0]
    bq, bk, upb, hpb = cfg.bq, cfg.bk, cfg.upb, cfg.hpb
    nq, nk, nub = sq // bq, sk // bk, w // (LANES * upb)
    nj, sched = _q_of_kv(cfg, nq, nk)
    wb = LANES * upb

    def body(q_ref, k_ref, v_ref, o_ref, l_ref, do_ref, dk_ref, dv_ref, dk_acc, dv_acc):
        kb, j = pl.program_id(1), pl.program_id(2)
        i, active = sched(kb, j)
        lms = _lane_masks()

        @pl.when(j == 0)
        def _():
            dk_acc[...] = jnp.zeros_like(dk_acc)
            dv_acc[...] = jnp.zeros_like(dv_acc)

        @pl.when(active)
        def _():
            mask = _attn_mask(cfg, i, kb)
            for u in range(upb):
                sl = slice(u * LANES, (u + 1) * LANES)
                qv = q_ref[:, sl].astype(BF16)
                kv = k_ref[:, sl].astype(BF16)
                vv = v_ref[:, sl].astype(BF16)
                dov = do_ref[:, sl]
                prod = dov * o_ref[:, sl]
                dob = dov.astype(BF16)
                lv = l_ref[:, sl]
                dk_tot, dv_tot = None, None
                for a in range(hpb):
                    p, ds, qa, doa, _ = _softmax_grad_terms(cfg, lms, a, qv, kv, vv, dob, prod, lv, mask)
                    dvc = _dot(p.astype(BF16), doa, _TN)
                    dkc = _dot(ds.astype(BF16), qa, _TN)
                    dv_tot = dvc if dv_tot is None else dv_tot + dvc
                    dk_tot = dkc if dk_tot is None else dk_tot + dkc
                dk_acc[u] = dk_acc[u] + dk_tot
                dv_acc[u] = dv_acc[u] + dv_tot

        @pl.when(j == nj - 1)
        def _():
            for u in range(upb):
                sl = slice(u * LANES, (u + 1) * LANES)
                dk_ref[:, sl] = dk_acc[u]
                dv_ref[:, sl] = dv_acc[u]

    qspec = pl.BlockSpec((bq, wb), lambda ub, kb, j: (sched(kb, j)[0], ub))
    kspec = pl.BlockSpec((bk, wb), lambda ub, kb, j: (kb, ub))
    return pl.pallas_call(
        body, name=name, grid=(nub, nk, nj), in_specs=[qspec, kspec, kspec, qspec, qspec, qspec],
        out_specs=(kspec, kspec),
        out_shape=(jax.ShapeDtypeStruct((sk, w), F32), jax.ShapeDtypeStruct((sk, w), F32)),
        scratch_shapes=[pltpu.VMEM((upb, bk, LANES), F32), pltpu.VMEM((upb, bk, LANES), F32)],
        compiler_params=_params("parallel", "parallel", "arbitrary"),
    )(q, k, v, o, lse, do)


def _band_masks(max_dist):
    assert BLOCK - 1 <= max_dist <= BLOCK
    r = lax.broadcasted_iota(jnp.int32, (BLOCK, BLOCK), 0)
    c = lax.broadcasted_iota(jnp.int32, (BLOCK, BLOCK), 1)
    return (BLOCK + r - c) <= max_dist, r >= c


def _stack_heads(t):
    return jnp.concatenate([t, t], axis=0)


def _head_terms(lms, a, prod, lv):
    t = jnp.sum(_sel(lms[a], prod), axis=-1, keepdims=True)
    lse = jnp.max(jnp.where(lms[a], lv, -jnp.inf), axis=-1, keepdims=True)
    return t, lse


class _Residue:
    def __init__(self, ref, r, dil):
        self.ref, self.rows = ref, pl.ds(r, BLOCK, stride=dil)

    def __getitem__(self, idx):
        return self.ref[self.rows, idx[1]]

    def __setitem__(self, idx, val):
        self.ref[self.rows, idx[1]] = val


def _residues(refs, dil):
    if dil == 1:
        return [tuple(refs)]
    return [tuple(_Residue(x, r, dil) for x in refs) for r in range(dil)]


def _band_fwd(q, k, v, sinkrow, scale, max_dist, upb, dil, name):
    sq, w = q.shape
    rb = BLOCK * dil
    nq, nub, wb = sq // rb, w // (LANES * upb), LANES * upb
    has_sink = sinkrow is not None

    def body(*refs):
        s_ref = refs[5] if has_sink else None
        lms = _lane_masks()
        mprev, mcur = _band_masks(max_dist)
        mprev = jnp.logical_and(mprev, pl.program_id(1) > 0)
        mask2 = _stack_heads(jnp.concatenate([mprev, mcur], axis=1))
        chains = [(rr, slice(u * LANES, (u + 1) * LANES))
                  for rr in _residues(refs[:5] + refs[-2:], dil) for u in range(upb)]
        for g0 in range(0, len(chains), BAND_CHAINS_PER_BATCH):
            group = chains[g0:g0 + BAND_CHAINS_PER_BATCH]
            qs, kcat, vcat, sks = [], [], [], []
            for (q_ref, kp_ref, kc_ref, vp_ref, vc_ref, _, _), sl in group:
                qv = (q_ref[:, sl] * scale).astype(BF16)
                qs.append(jnp.concatenate([_sel(lms[0], qv), _sel(lms[1], qv)], axis=0))
                kcat.append(jnp.concatenate([kp_ref[:, sl].astype(BF16), kc_ref[:, sl].astype(BF16)], axis=0))
                vcat.append(jnp.concatenate([vp_ref[:, sl].astype(BF16), vc_ref[:, sl].astype(BF16)], axis=0))
                if has_sink:
                    sks.append(s_ref[sl.start // LANES])
            qs, kcat, vcat = jnp.stack(qs), jnp.stack(kcat), jnp.stack(vcat)
            sc = jnp.where(mask2[None], _dot(qs, kcat, _BNT), NEG_MASK)
            m = jnp.max(sc, axis=-1, keepdims=True)
            p = jnp.exp(sc - m)
            l = jnp.sum(p, axis=-1, keepdims=True)
            pv = _dot(p.astype(BF16), vcat, _BNN)
            if has_sink:
                sk2 = jnp.stack(sks)
                m_all = jnp.maximum(m, sk2)
                shrink = jnp.exp(m - m_all)
                l = l * shrink + jnp.exp(sk2 - m_all)
                pv, m = pv * shrink, m_all
            o2 = pv / l
            lse2 = m + jnp.log(l)
            for gi, ((_, _, _, _, _, o_ref, l_ref), sl) in enumerate(group):
                o_ref[:, sl] = jnp.where(lms[0], o2[gi, :BLOCK], o2[gi, BLOCK:])
                l_ref[:, sl] = jnp.where(lms[0], lse2[gi, :BLOCK], lse2[gi, BLOCK:])

    cur = pl.BlockSpec((rb, wb), lambda ub, i: (i, ub))
    prev = pl.BlockSpec((rb, wb), lambda ub, i: (jnp.maximum(i - 1, 0), ub))
    in_specs = [cur, prev, cur, prev, cur]
    in_specs += [pl.BlockSpec((upb, 2 * BLOCK, 1), lambda ub, i: (ub, 0, 0))] if has_sink else []
    args = (q, k, k, v, v) + ((sinkrow,) if has_sink else ())
    return pl.pallas_call(
        body, name=name, grid=(nub, nq), in_specs=in_specs, out_specs=(cur, cur),
        out_shape=(jax.ShapeDtypeStruct((sq, w), F32), jax.ShapeDtypeStruct((sq, w), F32)),
        compiler_params=_params("parallel", "parallel"),
    )(*args)


def _band_dq(q, k, v, o, lse, do, sinkrow, scale, max_dist, upb, dil, name):
    sq, w = q.shape
    rb = BLOCK * dil
    nq, nub, wb = sq // rb, w // (LANES * upb), LANES * upb
    has_sink = sinkrow is not None

    def body(*refs):
        if has_sink:
            s_ref, dq_block, dsink_ref = refs[8], refs[9], refs[10]
        else:
            dq_block = refs[8]
        i = pl.program_id(1)
        lms = _lane_masks()
        mprev, mcur = _band_masks(max_dist)
        mprev = jnp.logical_and(mprev, i > 0)
        mask2 = _stack_heads(jnp.concatenate([mprev, mcur], axis=1))
        if has_sink:
            @pl.when(i == 0)
            def _():
                dsink_ref[...] = jnp.zeros_like(dsink_ref)

        chains = [(rr, slice(u * LANES, (u + 1) * LANES))
                  for rr in _residues(refs[:8] + (dq_block,), dil) for u in range(upb)]
        for g0 in range(0, len(chains), BAND_CHAINS_PER_BATCH):
            group = chains[g0:g0 + BAND_CHAINS_PER_BATCH]
            qs, dos, kcat, vcat, t2, lse2 = [], [], [], [], [], []
            for (q_ref, kp_ref, kc_ref, vp_ref, vc_ref, o_ref, l_ref, do_ref, _), sl in group:
                qv = (q_ref[:, sl] * scale).astype(BF16)
                dov = do_ref[:, sl]
                prod = dov * o_ref[:, sl]
                dob = dov.astype(BF16)
                lv = l_ref[:, sl]
                (t0, lse0), (t1, lse1) = _head_terms(lms, 0, prod, lv), _head_terms(lms, 1, prod, lv)
                t2.append(jnp.concatenate([t0, t1], axis=0))
                lse2.append(jnp.concatenate([lse0, lse1], axis=0))
                qs.append(jnp.concatenate([_sel(lms[0], qv), _sel(lms[1], qv)], axis=0))
                dos.append(jnp.concatenate([_sel(lms[0], dob), _sel(lms[1], dob)], axis=0))
                kcat.append(jnp.concatenate([kp_ref[:, sl].astype(BF16), kc_ref[:, sl].astype(BF16)], axis=0))
                vcat.append(jnp.concatenate([vp_ref[:, sl].astype(BF16), vc_ref[:, sl].astype(BF16)], axis=0))
                if has_sink:
                    rs = -jnp.exp(s_ref[:, sl] - lv) * jnp.where(lms[0], t0, t1)
                    dsink_ref[0:1, sl] += jnp.sum(rs, axis=0, keepdims=True)
            qs, dos, kcat, vcat = jnp.stack(qs), jnp.stack(dos), jnp.stack(kcat), jnp.stack(vcat)
            p = jnp.exp(jnp.where(mask2[None], _dot(qs, kcat, _BNT), NEG_MASK) - jnp.stack(lse2))
            ds = (p * (_dot(dos, vcat, _BNT) - jnp.stack(t2))).astype(BF16)
            dq2 = _dot(ds, kcat, _BNN) * scale
            for gi, ((_, _, _, _, _, _, _, _, dq_ref), sl) in enumerate(group):
                dq_ref[:, sl] = jnp.where(lms[0], dq2[gi, :BLOCK], dq2[gi, BLOCK:])

    cur = pl.BlockSpec((rb, wb), lambda ub, i: (i, ub))
    prev = pl.BlockSpec((rb, wb), lambda ub, i: (jnp.maximum(i - 1, 0), ub))
    in_specs = [cur, prev, cur, prev, cur, cur, cur, cur]
    args = (q, k, k, v, v, o, lse, do)
    out_specs, out_shape = cur, jax.ShapeDtypeStruct((sq, w), F32)
    sem = ("parallel", "parallel")
    if has_sink:
        in_specs = in_specs + [pl.BlockSpec((1, wb), lambda ub, i: (0, ub))]
        args = args + (sinkrow,)
        out_specs = (cur, pl.BlockSpec((8, wb), lambda ub, i: (0, ub)))
        out_shape = (out_shape, jax.ShapeDtypeStruct((8, w), F32))
        sem = ("parallel", "arbitrary")
    return pl.pallas_call(
        body, name=name, grid=(nub, nq), in_specs=in_specs, out_specs=out_specs, out_shape=out_shape,
        compiler_params=_params(*sem),
    )(*args)


def _band_dkv(q, k, v, o, lse, do, scale, max_dist, upb, dil, name):
    sq, w = q.shape
    rb = BLOCK * dil
    nq, nub, wb = sq // rb, w // (LANES * upb), LANES * upb

    def body(*refs):
        kb = pl.program_id(1)
        lms = _lane_masks()
        key = lax.broadcasted_iota(jnp.int32, (BLOCK, BLOCK), 0)
        qry = lax.broadcasted_iota(jnp.int32, (BLOCK, BLOCK), 1)
        msame = qry >= key
        mnext = jnp.logical_and((BLOCK + qry - key) <= max_dist, kb < nq - 1)
        mask4 = jnp.concatenate([msame, msame, mnext, mnext], axis=1)
        chains =[(rr, slice(u * LANES, (u + 1) * LANES)) for rr in _residues(refs, dil) for u in range(upb)]
        for g0 in range(0, len(chains), BAND_CHAINS_PER_BATCH):
            group = chains[g0:g0 + BAND_CHAINS_PER_BATCH]
            kvs, vvs, qss, doss, t4s, lse4s = [], [], [], [], [], []
            for (k_ref, v_ref, qs_ref, qn_ref, os_ref, on_ref, ls_ref, ln_ref, dos_ref, don_ref, _, _), sl in group:
                kvs.append(k_ref[:, sl].astype(BF16))
                vvs.append(v_ref[:, sl].astype(BF16))
                qparts, doparts, tparts, lparts = [], [], [], []
                for q_ref, o_ref, l_ref, do_ref in ((qs_ref, os_ref, ls_ref, dos_ref),
                                                    (qn_ref, on_ref, ln_ref, don_ref)):
                    qv = (q_ref[:, sl] * scale).astype(BF16)
                    dov = do_ref[:, sl]
                    prod_t = (dov * o_ref[:, sl]).T
                    dob = dov.astype(BF16)
                    lse_t = l_ref[:, sl].T
                    for a in range(2):
                        lanes = slice(a * HEAD_DIM, (a + 1) * HEAD_DIM)
                        qparts.append(_sel(lms[a], qv))
                        doparts.append(_sel(lms[a], dob))
                        tparts.append(jnp.sum(prod_t[lanes, :], axis=0, keepdims=True))
                        lparts.append(lse_t[a * HEAD_DIM:a * HEAD_DIM + 1, :])
                qss.append(jnp.concatenate(qparts, axis=0))
                doss.append(jnp.concatenate(doparts, axis=0))
                t4s.append(jnp.concatenate(tparts, axis=1))
                lse4s.append(jnp.concatenate(lparts, axis=1))
            kv, vv, qs, dos = jnp.stack(kvs), jnp.stack(vvs), jnp.stack(qss), jnp.stack(doss)
            p = jnp.exp(jnp.where(mask4[None], _dot(kv, qs, _BNT), NEG_MASK) - jnp.stack(lse4s))
            ds = (p * (_dot(vv, dos, _BNT) - jnp.stack(t4s))).astype(BF16)
            dv = _dot(p.astype(BF16), dos, _BNN)
            dk = _dot(ds, qs, _BNN)
            for gi, (rr, sl) in enumerate(group):
                rr[-1][:, sl] = dv[gi]
                rr[-2][:, sl] = dk[gi]

    same = pl.BlockSpec((rb, wb), lambda ub, kb: (kb, ub))
    nxt = pl.BlockSpec((rb, wb), lambda ub, kb: (jnp.minimum(kb + 1, nq - 1), ub))
    return pl.pallas_call(
        body, name=name, grid=(nub, nq), in_specs=[same, same, same, nxt, same, nxt, same, nxt, same, nxt],
        out_specs=(same, same),
        out_shape=(jax.ShapeDtypeStruct((sq, w), F32), jax.ShapeDtypeStruct((sq, w), F32)),
        compiler_params=_params("parallel", "parallel"),
    )(k, v, q, q, o, o, lse, lse, do, do)


def _make_band_attention(scale, max_dist, upb, name):
    @jax.custom_vjp
    def attn(q, k, v, sinks):
        return _band_fwd(q, k, v, _sink_col(sinks), scale, max_dist, upb, 1, name + "_fwd")[0]

    def fwd(q, k, v, sinks):
        o, lse = _band_fwd(q, k, v, _sink_col(sinks), scale, max_dist, upb, 1, name + "_fwd")
        return o, (q, k, v, o, lse, sinks)

    def bwd(res, do):
        q, k, v, o, lse, sinks = res
        dq, dsink = _band_dq(q, k, v, o, lse, do, _sink_row(sinks), scale, max_dist, upb, 1, name + "_dq")
        dk, dv = _band_dkv(q, k, v, o, lse, do, scale, max_dist, upb, 1, name + "_dkv")
        return dq, dk, dv, dsink[0].reshape(-1, HEAD_DIM)[:, 0]

    attn.defvjp(fwd, bwd)
    return attn


def _triangle(n, by_key):
    if by_key:
        pairs = [(i, kb) for kb in range(n) for i in range(kb, n)]
    else:
        pairs = [(i, j) for i in range(n) for j in range(i + 1)]
    qi = np.asarray([p[0] for p in pairs], np.int32)
    kj = np.asarray([p[1] for p in pairs], np.int32)
    return jnp.asarray(qi), jnp.asarray(kj)


def _causal_fwd(q, k, v, scale, blk, name):
    s, w = q.shape
    nq, nub = s // blk, w // LANES
    qi, kj = _triangle(nq, by_key=False)

    def body(qi_ref, kj_ref, q_ref, k_ref, v_ref, o_ref, l_ref, m_sc, l_sc, acc_sc):
        t = pl.program_id(1)
        i, j = qi_ref[t], kj_ref[t]

        @pl.when(j == 0)
        def _():
            m_sc[...] = jnp.full_like(m_sc, NEG_INIT)
            l_sc[...] = jnp.zeros_like(l_sc)
            acc_sc[...] = jnp.zeros_like(acc_sc)

        def step(diagonal):
            kv, vv = k_ref[...].astype(BF16), v_ref[...].astype(BF16)
            chains = range(0, blk, CAUSAL_ROW_CHAIN)
            scs = [_dot((q_ref[c0:c0 + CAUSAL_ROW_CHAIN, :] * scale).astype(BF16), kv, _NT) for c0 in chains]
            m_all, l_all, acc_all = m_sc[...], l_sc[...], acc_sc[...]
            m_out, l_out, acc_out = [], [], []
            for sc, c0 in zip(scs, chains):
                rows = slice(c0, c0 + CAUSAL_ROW_CHAIN)
                if diagonal:
                    r = c0 + lax.broadcasted_iota(jnp.int32, (CAUSAL_ROW_CHAIN, blk), 0)
                    c = lax.broadcasted_iota(jnp.int32, (CAUSAL_ROW_CHAIN, blk), 1)
                    sc = jnp.where(r >= c, sc, NEG_MASK)
                m_prev = m_all[rows]
                m_new = jnp.maximum(m_prev, jnp.max(sc, axis=-1, keepdims=True))
                alpha = jnp.exp(m_prev - m_new)
                p = jnp.exp(sc - m_new)
                l_out.append(alpha * l_all[rows] + jnp.sum(p, axis=-1, keepdims=True))
                m_out.append(m_new)
                acc_out.append(acc_all[rows] * alpha + _dot(p.astype(BF16), vv, _NN))
            m_sc[...] = jnp.concatenate(m_out, axis=0)
            l_sc[...] = jnp.concatenate(l_out, axis=0)
            acc_sc[...] = jnp.concatenate(acc_out, axis=0)

        @pl.when(j < i)
        def _():
            step(False)

        @pl.when(j == i)
        def _():
            step(True)
            lf = l_sc[...]
            o_ref[...] = acc_sc[...] / lf
            l_ref[...] = jnp.broadcast_to(m_sc[...] + jnp.log(lf), (blk, LANES))

    qspec = pl.BlockSpec((blk, LANES), lambda ub, t, qi_ref, kj_ref: (qi_ref[t], ub))
    kspec = pl.BlockSpec((blk, LANES), lambda ub, t, qi_ref, kj_ref: (kj_ref[t], ub))
    return pl.pallas_call(
        body, name=name,
        grid_spec=pltpu.PrefetchScalarGridSpec(
            num_scalar_prefetch=2, grid=(nub, qi.shape[0]), in_specs=[qspec, kspec, kspec], out_specs=(qspec, qspec),
            scratch_shapes=[pltpu.VMEM((blk, 1), F32), pltpu.VMEM((blk, 1), F32), pltpu.VMEM((blk, LANES), F32)]),
        out_shape=(jax.ShapeDtypeStruct((s, w), F32), jax.ShapeDtypeStruct((s, w), F32)),
        compiler_params=_params("parallel", "arbitrary"),
    )(qi, kj, q, k, v)


def _causal_bwd(q, k, v, o, lse, do, scale, blk, name):
    s, w = q.shape
    nq, nub = s // blk, w // LANES
    qi, kj = _triangle(nq, by_key=True)

    def body(qi_ref, kj_ref, q_ref, k_ref, v_ref, o_ref, l_ref, do_ref, dq_ref, dk_ref, dv_ref, dk_acc, dv_acc):
        t = pl.program_id(1)
        i, kb = qi_ref[t], kj_ref[t]

        @pl.when(t == 0)
        def _():
            dq_ref[...] = jnp.zeros_like(dq_ref)

        @pl.when(i == kb)
        def _():
            dk_acc[...] = jnp.zeros_like(dk_acc)
            dv_acc[...] = jnp.zeros_like(dv_acc)

        def step(diagonal):
            qv = (q_ref[...] * scale).astype(BF16)
            kv, vv = k_ref[...].astype(BF16), v_ref[...].astype(BF16)
            dov = do_ref[...]
            tsum = jnp.sum(dov * o_ref[...], axis=-1, keepdims=True)
            dob = dov.astype(BF16)
            sc = _dot(qv, kv, _NT)
            if diagonal:
                r = lax.broadcasted_iota(jnp.int32, (blk, blk), 0)
                c = lax.broadcasted_iota(jnp.int32, (blk, blk), 1)
                sc = jnp.where(r >= c, sc, NEG_MASK)
            p = jnp.exp(sc - l_ref[:, 0:1])
            ds = (p * (_dot(dob, vv, _NT) - tsum)).astype(BF16)
            dv_acc[...] += _dot(p.astype(BF16), dob, _TN)
            dk_acc[...] += _dot(ds, qv, _TN)
            rows = pl.ds(pl.multiple_of(i * blk, blk), blk)
            dq_ref[rows, :] += _dot(ds, kv, _NN) * scale

        @pl.when(i == kb)
        def _():
            step(True)

        @pl.when(i > kb)
        def _():
            step(False)

        @pl.when(i == nq - 1)
        def _():
            dk_ref[...] = dk_acc[...]
            dv_ref[...] = dv_acc[...]

    qspec = pl.BlockSpec((blk, LANES), lambda ub, t, qi_ref, kj_ref: (qi_ref[t], ub))
    kspec = pl.BlockSpec((blk, LANES), lambda ub, t, qi_ref, kj_ref: (kj_ref[t], ub))
    whole = pl.BlockSpec((s, LANES), lambda ub, t, qi_ref, kj_ref: (0, ub))
    out = jax.ShapeDtypeStruct((s, w), F32)
    return pl.pallas_call(
        body, name=name,
        grid_spec=pltpu.PrefetchScalarGridSpec(
            num_scalar_prefetch=2, grid=(nub, qi.shape[0]), in_specs=[qspec, kspec, kspec, qspec, qspec, qspec],
            out_specs=(whole, kspec, kspec),
            scratch_shapes=[pltpu.VMEM((blk, LANES), F32), pltpu.VMEM((blk, LANES), F32)]),
        out_shape=(out, out, out), compiler_params=_params("parallel", "arbitrary"),
    )(qi, kj, q, k, v, o, lse, do)


def _make_causal_attention(scale, blk, name):
    @jax.custom_vjp
    def attn(q, k, v):
        return _causal_fwd(q, k, v, scale, blk, name + "_fwd")[0]

    def fwd(q, k, v):
        o, lse = _causal_fwd(q, k, v, scale, blk, name + "_fwd")
        return o, (q, k, v, o, lse)

    def bwd(res, do):
        q, k, v, o, lse = res
        return _causal_bwd(q, k, v, o, lse, do, scale, blk, name + "_bwd")

    attn.defvjp(fwd, bwd)
    return attn


def _make_attention(cfg, name, with_sink=False):
    if with_sink:
        @jax.custom_vjp
        def attn(q, k, v, sinks):
            return _attn_fwd(q, k, v, _sink_row(sinks), cfg, name + "_fwd")[0]

        def fwd(q, k, v, sinks):
            o, lse = _attn_fwd(q, k, v, _sink_row(sinks), cfg, name + "_fwd")
            return o, (q, k, v, o, lse, sinks)

        def bwd(res, do):
            q, k, v, o, lse, sinks = res
            dq, dsink = _attn_dq(q, k, v, o, lse, do, _sink_row(sinks), cfg, name + "_dq")
            dk, dv = _attn_dkv(q, k, v, o, lse, do, cfg, name + "_dkv")
            return dq, dk, dv, dsink[0].reshape(-1, HEAD_DIM)[:, 0]
    else:
        @jax.custom_vjp
        def attn(q, k, v):
            return _attn_fwd(q, k, v, None, cfg, name + "_fwd")[0]

        def fwd(q, k, v):
            o, lse = _attn_fwd(q, k, v, None, cfg, name + "_fwd")
            return o, (q, k, v, o, lse)

        def bwd(res, do):
            q, k, v, o, lse = res
            dq = _attn_dq(q, k, v, o, lse, do, None, cfg, name + "_dq")
            dk, dv = _attn_dkv(q, k, v, o, lse, do, cfg, name + "_dkv")
            return dq, dk, dv

    attn.defvjp(fwd, bwd)
    return attn


def _sink_row(sinks):
    return jnp.repeat(sinks.astype(F32), HEAD_DIM).reshape(1, -1)


def _sink_col(sinks):
    return jnp.repeat(sinks.astype(F32).reshape(-1, 2, 1), BLOCK, axis=1)


def _merge3(os_, ls_, name):
    s, w = os_[0].shape
    bs = _pick(s, (256, 128))

    def body(o1, o2, o3, l1, l2, l3, out_ref, lse_ref):
        a1, a2, a3 = l1[...], l2[...], l3[...]
        m = jnp.maximum(jnp.maximum(a1, a2), a3)
        e1, e2, e3 = jnp.exp(a1 - m), jnp.exp(a2 - m), jnp.exp(a3 - m)
        z = e1 + e2 + e3
        out_ref[...] = (e1 * o1[...] + e2 * o2[...] + e3 * o3[...]) / z
        lse_ref[...] = m + jnp.log(z)

    row = pl.BlockSpec((bs, w), lambda i: (i, 0))
    return pl.pallas_call(
        body, name=name, grid=(s // bs,), in_specs=[row] * 6, out_specs=(row, row),
        out_shape=(jax.ShapeDtypeStruct((s, w), F32), jax.ShapeDtypeStruct((s, w), F32)),
        compiler_params=_params("parallel"),
    )(*os_, *ls_)


def _add3(a, b, c, name):
    s, w = a.shape
    bs = _pick(s, (512, 256, 128))

    def body(a_ref, b_ref, c_ref, o_ref):
        o_ref[...] = (a_ref[...] + b_ref[...]) + c_ref[...]

    row = pl.BlockSpec((bs, w), lambda i: (i, 0))
    return pl.pallas_call(
        body, name=name, grid=(s // bs,), in_specs=[row] * 3, out_specs=row,
        out_shape=jax.ShapeDtypeStruct((s, w), F32), compiler_params=_params("parallel"),
    )(a, b, c)


def _make_dilated(name):
    scale, max_dist = HEAD_DIM ** -0.5, BLOCK

    def upb_of(dil):
        return BAND_UNITS_PER_STEP if dil == 1 else 1

    def forward(q, k, v):
        os_, ls_ = [], []
        for n, (_, dil) in enumerate(DIL_PATTERNS):
            o, l = _band_fwd(q, k, v, None, scale, max_dist, upb_of(dil), dil, "%s_b%d_fwd" % (name, n))
            os_.append(o)
            ls_.append(l)
        return _merge3(os_, ls_, name + "_merge")

    @jax.custom_vjp
    def dilated(q, k, v):
        return forward(q, k, v)[0]

    def fwd(q, k, v):
        out, lse = forward(q, k, v)
        return out, (q, k, v, out, lse)

    def bwd(res, do):
        q, k, v, out, lse = res
        dqs, dks, dvs = [], [], []
        for n, (_, dil) in enumerate(DIL_PATTERNS):
            args = (q, k, v, out, lse, do)
            dqs.append(_band_dq(*args, None, scale, max_dist, upb_of(dil), dil, "%s_b%d_dq" % (name, n)))
            dk, dv = _band_dkv(*args, scale, max_dist, upb_of(dil), dil, "%s_b%d_dkv" % (name, n))
            dks.append(dk)
            dvs.append(dv)
        return (_add3(*dqs, name + "_dq_sum"), _add3(*dks, name + "_dk_sum"), _add3(*dvs, name + "_dv_sum"))

    dilated.defvjp(fwd, bwd)
    return dilated


def _times_w(a, w, orient, name, res=None):
    return _mm(a, w, "nn" if orient == "n" else "nt", name, res=res)


def _times_wt(dz, w, orient, name, res=None):
    return _mm(dz, w, "nt" if orient == "n" else "nn", name, res=res)


def _grad_w(a, dz, orient, name):
    if orient == "n":
        return _mm(a, dz, "tn", name, out_dtype=GRAD_WIRE_DTYPE)
    return _mm(dz, a, "tn", name, out_dtype=GRAD_WIRE_DTYPE)


def _make_norm_linear(name, orients, through=False):
    nw = len(orients)

    def forward(x, g, ws):
        h = _rms_fwd(x, g, name + "_norm")
        zs = tuple(_times_w(h, w, o, "%s_mm%d" % (name, i)) for i, (w, o) in enumerate(zip(ws, orients)))
        return zs + ((x,) if through else ()), h

    @jax.custom_vjp
    def op(x, g, slots, ws):
        return forward(x, g, ws)[0]

    def fwd(x, g, slots, ws):
        outs, h = forward(x, g, ws)
        return outs, (x, g, h, ws)

    def bwd(res, cts):
        x, g, h, ws = res
        dh = None
        for i, (w, o) in enumerate(zip(ws, orients)):
            dh = _times_wt(cts[i], w, o, "%s_dh%d" % (name, i), res=dh)
        dws = tuple(_grad_w(h, cts[i], o, "%s_dw%d" % (name, i)) for i, o in enumerate(orients))
        dx, dg = _rms_bwd(x, g, dh, name + "_norm_bwd", dres=cts[nw] if through else None)
        return dx, dg, dws, (None,) * nw

    op.defvjp(fwd, bwd)
    return op


def _make_linear_res(name, orient):
    @jax.custom_vjp
    def op(a, wslot, w, res):
        return _times_w(a, w, orient, name + "_mm", res=res)

    def fwd(a, wslot, w, res):
        return _times_w(a, w, orient, name + "_mm", res=res), (a, w)

    def bwd(saved, dout):
        a, w = saved
        return _times_wt(dout, w, orient, name + "_da"), _grad_w(a, dout, orient, name + "_dw"), None, dout

    op.defvjp(fwd, bwd)
    return op


FFN_TILE_M, FFN_TILE_N = 512, 1408


def _gate_up_act(h, wg, wu, name):
    m, k = h.shape
    n = wg.shape[0]
    bm, bn = _div128(m, FFN_TILE_M), _div128(n, FFN_TILE_N)

    def body(h_ref, wg_ref, wu_ref, g_ref, u_ref, a_ref):
        hv = h_ref[...]
        g = _dot(hv, wg_ref[...], _NT)
        u = _dot(hv, wu_ref[...], _NT)
        g_ref[...] = g
        u_ref[...] = u
        a_ref[...] = (g / (1.0 + jnp.exp(-g)) * u).astype(BF16)

    wspec = pl.BlockSpec((bn, k), lambda i, j: (j, 0))
    ospec = pl.BlockSpec((bm, bn), lambda i, j: (i, j))
    return pl.pallas_call(
        body, name=name, grid=(m // bm, n // bn), in_specs=[pl.BlockSpec((bm, k), lambda i, j: (i, 0)), wspec, wspec],
        out_specs=(ospec, ospec, ospec),
        out_shape=(jax.ShapeDtypeStruct((m, n), F32), jax.ShapeDtypeStruct((m, n), F32),
                   jax.ShapeDtypeStruct((m, n), BF16)),
        compiler_params=_params("parallel", "parallel"),
    )(h, wg, wu)


def _down_bwd_act(dout, wd, gmat, umat, name):
    m, k = dout.shape
    n = wd.shape[0]
    bm, bn = _div128(m, FFN_TILE_M), _div128(n, FFN_TILE_N)

    def body(do_ref, wd_ref, g_ref, u_ref, dg_ref, du_ref):
        d = _dot(do_ref[...].astype(BF16), wd_ref[...], _NT)
        g, u = g_ref[...], u_ref[...]
        sig = 1.0 / (1.0 + jnp.exp(-g))
        dg_ref[...] = (d * u * (sig * (1.0 + g * (1.0 - sig)))).astype(BF16)
        du_ref[...] = (d * (g * sig)).astype(BF16)

    ospec = pl.BlockSpec((bm, bn), lambda i, j: (i, j))
    return pl.pallas_call(
        body, name=name, grid=(m // bm, n // bn),
        in_specs=[pl.BlockSpec((bm, k), lambda i, j: (i, 0)), pl.BlockSpec((bn, k), lambda i, j: (j, 0)), ospec, ospec],
        out_specs=(ospec, ospec), out_shape=(jax.ShapeDtypeStruct((m, n), BF16),) * 2,
        compiler_params=_params("parallel", "parallel"),
    )(dout, wd, gmat, umat)


def _make_ffn(name):
    def forward(x, g, wg, wu, wd):
        h = _rms_fwd(x, g, name + "_norm")
        gmat, umat, a = _gate_up_act(h, wg, wu, name + "_gate_up")
        return _mm(a, wd, "nn", name + "_down", res=x), (x, g, h, gmat, umat, a, wg, wu, wd)

    @jax.custom_vjp
    def op(x, g, wg_slot, wu_slot, wd_slot, wg, wu, wd):
        return forward(x, g, wg, wu, wd)[0]

    def fwd(x, g, wg_slot, wu_slot, wd_slot, wg, wu, wd):
        return forward(x, g, wg, wu, wd)

    def bwd(saved, dout):
        x, g, h, gmat, umat, a, wg, wu, wd = saved
        dgm, dum = _down_bwd_act(dout, wd, gmat, umat, name + "_da_act")
        dwd = _mm(a, dout, "tn", name + "_dwd", out_dtype=GRAD_WIRE_DTYPE)
        dwg = _grad_w(h, dgm, "t", name + "_dwg")
        dwu = _grad_w(h, dum, "t", name + "_dwu")
        dh = _times_wt(dum, wu, "t", name + "_dh_u", res=_times_wt(dgm, wg, "t", name + "_dh_g"))
        dx, dg = _rms_bwd(x, g, dh, name + "_norm_bwd", dres=dout)
        return dx, dg, dwg, dwu, dwd, None, None, None

    op.defvjp(fwd, bwd)
    return op


def _make_final_loss(name):
    def run(x, g, tgt):
        s, d = x.shape
        bs = _pick(s, (512, 256, 128))

        def body(x_ref, g_ref, t_ref, loss_ref, dx_ref, dg_ref):
            i = pl.program_id(0)
            xv = x_ref[...]
            gv = g_ref[...]
            r = lax.rsqrt(jnp.mean(xv * xv, axis=-1, keepdims=True) + NORM_EPS)
            xh = xv * r
            e = xh * gv - t_ref[...]
            dy = e * (1.0 / d)
            dxh = dy * gv
            dx_ref[...] = r * (dxh - xh * jnp.mean(dxh * xh, axis=-1, keepdims=True))
            part = 0.5 * jnp.sum(jnp.sum(e * e, axis=-1, keepdims=True) * (1.0 / d), axis=0, keepdims=True)

            @pl.when(i == 0)
            def _():
                loss_ref[...] = jnp.zeros_like(loss_ref)
                dg_ref[...] = jnp.zeros_like(dg_ref)

            loss_ref[...] += jnp.broadcast_to(part, loss_ref.shape)
            dg_ref[...] += jnp.sum(dy * xh, axis=0, keepdims=True)

        row = pl.BlockSpec((bs, d), lambda i: (i, 0))
        vec = pl.BlockSpec((1, d), lambda i: (0, 0))
        loss, dx, dg = pl.pallas_call(
            body, name=name, grid=(s // bs,), in_specs=[row, vec, row],
            out_specs=(pl.BlockSpec((8, LANES), lambda i: (0, 0)), row, vec),
            out_shape=(jax.ShapeDtypeStruct((8, LANES), F32), jax.ShapeDtypeStruct((s, d), F32),
                       jax.ShapeDtypeStruct((1, d), F32)),
            compiler_params=_params("arbitrary"),
        )(x, g.reshape(1, d), tgt)
        return loss[0, 0], dx, dg.reshape(d)

    @jax.custom_vjp
    def op(x, g, tgt):
        return run(x, g, tgt)[0]

    def fwd(x, g, tgt):
        loss, dx, dg = run(x, g, tgt)
        return loss, (dx, dg)

    def bwd(saved, ct):
        dx, dg = saved
        return dx * ct, dg * ct, None

    op.defvjp(fwd, bwd)
    return op


def _model_loss(diff, consts):
    x = diff["x"]
    w = consts["w"]
    slot = diff["slots"]
    vec = diff["vec"]
    tab64, tab_mla = consts["tab64"], consts["tab_mla"]
    mem = consts["mem"]
    s = x.shape[0]

    rope64 = lambda t, nm: _make_rope(HEAD_DIM // 2, nm)(t, *tab64)
    rope_mla = lambda t, nm: _make_rope(MLA_ROPE_DIM // 2, nm)(t, *tab_mla)

    def nl(nm, inp, gain, wnames, through=False):
        orients = tuple(_orient(n) for n in wnames)
        op = _make_norm_linear(nm, orients, through)
        return op(inp, gain, tuple(slot[n] for n in wnames), tuple(w[n] for n in wnames))

    def lin_res(nm, a, wname, res):
        return _make_linear_res(nm, _orient(wname))(a, slot[wname], w[wname], res)

    def cross(layer, xin):
        p = "l%d_" % layer
        q, xin = nl(p + "xq", xin, vec[p + "x_norm"], (p + "w_xq",), through=True)
        kv, = nl(p + "xkv", mem, vec[p + "mem_norm"], (p + "w_xkv",))
        half = X_HEADS * X_HEAD_DIM
        cfg = AttnCfg("full", X_HEAD_DIM ** -0.5, 1, _pick(s, (512, 256, 128)), kv.shape[0], 4)
        o = _make_attention(cfg, p + "xattn")(q, kv[:, :half], kv[:, half:])
        return lin_res(p + "xo", o, p + "w_xo", xin)

    def ffn(layer, xin):
        p = "l%d_" % layer
        names = (p + "w_gate", p + "w_up", p + "w_down")
        return _make_ffn(p + "ffn")(xin, vec[p + "ffn_norm"], *(slot[n] for n in names), *(w[n] for n in names))

    z, x = nl("l0_in", x, vec["l0_mix_norm"], ("l0_w_in",), through=True)
    qa = rope64(z[:, :A_Q], "l0_rope_qa")
    ka = rope64(z[:, A_Q:A_Q + A_KV], "l0_rope_ka")
    va = z[:, A_Q + A_KV:A_Q + 2 * A_KV]
    rep = SWA_HEADS // SWA_KV_HEADS
    expand = lambda t: jnp.broadcast_to(t.reshape(s, SWA_KV_HEADS, 1, HEAD_DIM),
                                        (s, SWA_KV_HEADS, rep, HEAD_DIM)).reshape(s, A_Q)
    swa = _make_band_attention(HEAD_DIM ** -0.5, SWA_WINDOW - 1, BAND_UNITS_PER_STEP, "l0_swa")
    oa = swa(qa, expand(ka), expand(va), vec["l0_sinks"])

    c0 = A_Q + 2 * A_KV
    cq = z[:, c0:c0 + MLA_Q_RANK]
    ckv = z[:, c0 + MLA_Q_RANK:c0 + MLA_Q_RANK + MLA_KV_RANK]
    kr = z[:, c0 + MLA_Q_RANK + MLA_KV_RANK:EVEN_IN]
    qfull, = nl("l0_uq", cq, vec["l0_q_norm"], ("l0_w_uq_heads",))
    qfull = rope_mla(qfull, "l0_rope_q")
    kvb, knope = nl("l0_ukv", ckv, vec["l0_kv_norm"], ("l0_w_ukv", "l0_w_uk_heads"))
    kr_lanes = jnp.pad(kr, ((0, 0), (MLA_NOPE_DIM, LANES - MLA_NOPE_DIM - MLA_ROPE_DIM)))
    kfull = _make_rope_shared(MLA_ROPE_DIM // 2, "l0_rope_k")(knope, kr_lanes, *tab_mla)
    mla = _make_causal_attention((MLA_NOPE_DIM + MLA_ROPE_DIM) ** -0.5, _pick(s, (1024, 512, 256, 128)), "l0_mla")
    ob = mla(qfull, kfull, kvb)
    x = lin_res("l0_out_a", oa, "l0_w_out_swa", x)
    x = lin_res("l0_out_b", ob, "l0_w_out_mla", x)
    x = cross(0, x)
    x = ffn(0, x)

    qkv, x = nl("l1_qkv", x, vec["l1_mix_norm"], ("l1_w_qkv",), through=True)
    q = rope64(qkv[:, :D_MODEL], "l1_rope_q")
    k = rope64(qkv[:, D_MODEL:2 * D_MODEL], "l1_rope_k")
    o = _make_dilated("l1_dil")(q, k, qkv[:, 2 * D_MODEL:])
    x = lin_res("l1_out", o, "l1_w_out", x)
    x = cross(1, x)
    x = ffn(1, x)

    return _make_final_loss("final_loss")(x, vec["final_norm"], consts["target"])


MESH_IDS = pl.DeviceIdType.MESH
HBM_SPEC = pl.BlockSpec(memory_space=pltpu.HBM)


def _my_place():
    return lax.axis_index("x"), lax.axis_index("y"), lax.axis_index("c")


def _flip(v, bit):
    return 1 - v if bit else v


def _all_gather_rows(shard):
    r, c_ = shard.shape

    def body(x_ref, out_ref, send_sems, recv_sems, local_sem):
        x, y, c = _my_place()
        me, sibling = (x, y, c), (x, y, 1 - c)
        chips = [(1 - x, y), (x, 1 - y), (1 - x, 1 - y)]

        def slot(px, py, pc):
            return out_ref.at[4 * px + 2 * py + pc]

        def copy(k, block, to, src=None):
            return pltpu.make_async_remote_copy(
                src_ref=slot(*block) if src is None else src, dst_ref=slot(*block), send_sem=send_sems.at[k],
                recv_sem=recv_sems.at[k], device_id=to, device_id_type=MESH_IDS)

        mine = pltpu.make_async_copy(x_ref, slot(*me), local_sem)
        mine.start()
        first = [copy(0, me, sibling, src=x_ref)]
        first += [copy(1 + j, me, (*chip, c), src=x_ref) for j, chip in enumerate(chips)]
        for cp in first:
            cp.start()
        passed = [copy(4 + j, (*chip, c), sibling) for j, chip in enumerate(chips)]
        for j, chip in enumerate(chips):
            copy(1 + j, (*chip, c), me).wait_recv()
            passed[j].start()
        copy(0, sibling, me).wait_recv()
        for j, chip in enumerate(chips):
            copy(4 + j, (*chip, 1 - c), me).wait_recv()
        for cp in first + passed:
            cp.wait_send()
        mine.wait()

    return pl.pallas_call(
        body, name="weights_all_gather", out_shape=jax.ShapeDtypeStruct((N_DEV, r, c_), shard.dtype),
        in_specs=[HBM_SPEC], out_specs=HBM_SPEC,
        scratch_shapes=[pltpu.SemaphoreType.DMA((7,)), pltpu.SemaphoreType.DMA((7,)), pltpu.SemaphoreType.DMA],
    )(shard)


N_CHIPS = 4


def _exchange_with_sibling(slabs):
    _, nq, r, c_ = slabs.shape

    def body(p_ref, out_ref, send_sem, recv_sem):
        x, y, c = _my_place()
        cp = pltpu.make_async_remote_copy(
            src_ref=p_ref.at[1 - c], dst_ref=out_ref, send_sem=send_sem, recv_sem=recv_sem,
            device_id=(x, y, 1 - c), device_id_type=MESH_IDS)
        cp.start()
        cp.wait_recv()
        cp.wait_send()

    return pl.pallas_call(
        body, name="grad_exchange_sibling", out_shape=jax.ShapeDtypeStruct((nq, r, c_), slabs.dtype),
        in_specs=[HBM_SPEC], out_specs=HBM_SPEC,
        scratch_shapes=[pltpu.SemaphoreType.DMA, pltpu.SemaphoreType.DMA],
    )(slabs)


def _add_pairs(a, b):
    nq, r, c_ = a.shape
    br = _pick(r, (256, 128, 64, 32, 16, 8))

    def body(a_ref, b_ref, o_ref):
        o_ref[...] = (a_ref[...].astype(F32) + b_ref[...].astype(F32)).astype(o_ref.dtype)

    blk = pl.BlockSpec((1, br, c_), lambda q, i: (q, i, 0))
    return pl.pallas_call(
        body, name="grad_chip_sum", grid=(nq, r // br), in_specs=[blk, blk], out_specs=blk,
        out_shape=jax.ShapeDtypeStruct(a.shape, a.dtype), compiler_params=_params("parallel", "parallel"),
    )(a, b)


def _exchange_between_chips(slabs):
    nq, r, c_ = slabs.shape

    def body(t_ref, out_ref, send_sems, recv_sems, local_sem):
        x, y, c = _my_place()
        myq = 2 * x + y
        local = pltpu.make_async_copy(t_ref.at[myq], out_ref.at[myq], local_sem)
        local.start()
        sends, recvs = [], []
        for k in range(1, N_CHIPS):
            px, py = _flip(x, k & 2), _flip(y, k & 1)
            peer = 2 * px + py
            sends.append(pltpu.make_async_remote_copy(
                src_ref=t_ref.at[peer], dst_ref=out_ref.at[myq], send_sem=send_sems.at[k - 1],
                recv_sem=recv_sems.at[k - 1], device_id=(px, py, c), device_id_type=MESH_IDS))
            recvs.append(pltpu.make_async_remote_copy(
                src_ref=t_ref.at[myq], dst_ref=out_ref.at[peer], send_sem=send_sems.at[k - 1],
                recv_sem=recv_sems.at[k - 1], device_id=(px, py, c), device_id_type=MESH_IDS))
        for cp in sends:
            cp.start()
        for cp in recvs:
            cp.wait_recv()
        for cp in sends:
            cp.wait_send()
        local.wait()

    return pl.pallas_call(
        body, name="grad_exchange_chips", out_shape=jax.ShapeDtypeStruct(slabs.shape, slabs.dtype),
        in_specs=[HBM_SPEC], out_specs=HBM_SPEC,
        scratch_shapes=[pltpu.SemaphoreType.DMA((N_CHIPS - 1,)), pltpu.SemaphoreType.DMA((N_CHIPS - 1,)),
                        pltpu.SemaphoreType.DMA],
    )(slabs)


def _all_reduce_small(v):
    r, c_ = v.shape

    def body(v_ref, out_ref, buf, send_sems, recv_sems):
        x, y, c = _my_place()
        me = 4 * x + 2 * y + c
        buf[me] = v_ref[...]
        sends, recvs = [], []
        for k in range(1, N_DEV):
            px, py, pc = _flip(x, k & 4), _flip(y, k & 2), _flip(c, k & 1)
            peer = 4 * px + 2 * py + pc
            sends.append(pltpu.make_async_remote_copy(
                src_ref=v_ref, dst_ref=buf.at[me], send_sem=send_sems.at[k - 1], recv_sem=recv_sems.at[k - 1],
                device_id=(px, py, pc), device_id_type=MESH_IDS))
            recvs.append(pltpu.make_async_remote_copy(
                src_ref=v_ref, dst_ref=buf.at[peer], send_sem=send_sems.at[k - 1], recv_sem=recv_sems.at[k - 1],
                device_id=(px, py, pc), device_id_type=MESH_IDS))
        for cp in sends:
            cp.start()
        for cp in recvs:
            cp.wait_recv()
        for cp in sends:
            cp.wait_send()
        acc = buf[0]
        for d in range(1, N_DEV):
            acc = acc + buf[d]
        out_ref[...] = acc

    vm = pl.BlockSpec(memory_space=pltpu.VMEM)
    return pl.pallas_call(
        body, name="vector_grad_all_reduce", out_shape=jax.ShapeDtypeStruct((r, c_), F32), in_specs=[vm], out_specs=vm,
        scratch_shapes=[pltpu.VMEM((N_DEV, r, c_), F32), pltpu.SemaphoreType.DMA((7,)), pltpu.SemaphoreType.DMA((7,))],
    )(v)


def _adamw_math(w, g, m, v):
    m = ADAM_B1 * m + (1.0 - ADAM_B1) * g
    v = ADAM_B2 * v + (1.0 - ADAM_B2) * (g * g)
    m_hat = m / (1.0 - ADAM_B1 ** ADAM_STEP)
    v_hat = v / (1.0 - ADAM_B2 ** ADAM_STEP)
    delta = -ADAM_LR * (m_hat / (jnp.sqrt(v_hat) + ADAM_EPS) + ADAM_WD * w)
    return delta, m, v


def _sum_and_adamw(parts, w, m, v):
    nparts, r, c_ = parts.shape
    br = _pick(r, (256, 128, 64, 32, 16, 8))

    def body(p_ref, w_ref, m_ref, v_ref, g_ref, d_ref, nm_ref, nv_ref):
        g = p_ref[0].astype(F32)
        for d in range(1, nparts):
            g = g + p_ref[d].astype(F32)
        g_ref[...] = g
        d_ref[...], nm_ref[...], nv_ref[...] = _adamw_math(w_ref[...], g, m_ref[...], v_ref[...])

    row = pl.BlockSpec((br, c_), lambda i: (i, 0))
    return pl.pallas_call(
        body, name="grad_sum_adamw", grid=(r // br,),
        in_specs=[pl.BlockSpec((nparts, br, c_), lambda i: (0, i, 0)), row, row, row], out_specs=(row,) * 4,
        out_shape=(jax.ShapeDtypeStruct((r, c_), F32),) * 4, compiler_params=_params("parallel"),
    )(parts, w, m, v)


def _adamw_small(w, g, m, v):
    vm = pl.BlockSpec(memory_space=pltpu.VMEM)

    def body(w_ref, g_ref, m_ref, v_ref, d_ref, nm_ref, nv_ref):
        d_ref[...], nm_ref[...], nv_ref[...] = _adamw_math(w_ref[...], g_ref[...], m_ref[...], v_ref[...])

    return pl.pallas_call(
        body, name="vector_adamw", in_specs=[vm] * 4, out_specs=(vm,) * 3,
        out_shape=(jax.ShapeDtypeStruct(w.shape, F32),) * 3,
    )(w, g, m, v)


def _pad_rows(t, axis):
    extra = -t.shape[axis] % PART_ROW_ALIGN
    if extra == 0:
        return t
    widths = [(0, 0)] * t.ndim
    widths[axis] = (0, extra)
    return jnp.pad(t, widths)


def _pack_local(named):
    rows = [_pad_rows((named[n].T if kind == "c" else named[n]).reshape(-1, PACK_COLS), 0)
            for n, kind, _, _ in MATRICES]
    rows.append(jnp.zeros((MAT_ROWS - MAT_ROWS_USED, PACK_COLS), rows[0].dtype))
    return jnp.concatenate(rows, axis=0)


def _unpack_local(packed):
    out, r0 = {}, 0
    for n, kind, k, nn in MATRICES:
        nr = k * nn // N_DEV // PACK_COLS
        part = packed[r0:r0 + nr]
        out[n] = part.reshape(nn // N_DEV, k).T if kind == "c" else part.reshape(k // N_DEV, nn)
        r0 += _part_rows(k, nn)
    return out


def _unpack_gathered(g):
    out, r0 = {}, 0
    for n, kind, k, nn in MATRICES:
        nr = k * nn // N_DEV // PACK_COLS
        out[n] = g[:, r0:r0 + nr].reshape((nn, k) if kind == "c" else (k, nn))
        r0 += _part_rows(k, nn)
    return out


def _pack_full_grads(grads):
    rows = []
    for n, _, k, nn in MATRICES:
        gmat = grads[n].reshape(N_CHIPS, 2, -1, PACK_COLS).transpose(1, 0, 2, 3)
        rows.append(_pad_rows(gmat, 2))
    rows.append(jnp.zeros((2, N_CHIPS, MAT_ROWS - MAT_ROWS_USED, PACK_COLS), rows[0].dtype))
    return jnp.concatenate(rows, axis=2)


def _pack_vectors(named):
    rows = [jnp.pad(named[n].astype(F32), (0, PACK_COLS - d)) for n, d in VECTORS]
    rows += [jnp.zeros((PACK_COLS,), F32)] * (VEC_ROWS - len(VECTORS))
    return jnp.stack(rows, axis=0)


def _unpack_vectors(packed):
    return {n: packed[i, :d] for i, (n, d) in enumerate(VECTORS)}


def _step(inputs):
    x = inputs["x"][0]
    mem = inputs["mem"][0]
    positions = inputs["positions"][0]
    target = inputs["loss_target"][0]

    local_w = _pack_local({n: inputs[n] for n, _, _, _ in MATRICES})
    gathered = _all_gather_rows(local_w.astype(BF16))
    wfull = _unpack_gathered(gathered)
    vec = {n: inputs[n] for n, _ in VECTORS}

    loss_part, grad_x, gfull, gvec = _local_grads(wfull, vec, x, mem, positions, target)
    loss = lax.psum(loss_part, ("x", "y", "c"))

    slabs = _pack_full_grads(gfull)
    from_sibling = _exchange_with_sibling(slabs)
    mine = lax.dynamic_index_in_dim(slabs, lax.axis_index("c"), axis=0, keepdims=False)
    parts = _exchange_between_chips(_add_pairs(mine, from_sibling))
    local_m = _pack_local({n: inputs["m_" + n] for n, _, _, _ in MATRICES})
    local_v = _pack_local({n: inputs["v_" + n] for n, _, _, _ in MATRICES})
    g_pk, d_pk, m_pk, v_pk = _sum_and_adamw(parts, local_w, local_m, local_v)
    g_mat, d_mat, m_mat, v_mat = (_unpack_local(t) for t in (g_pk, d_pk, m_pk, v_pk))

    g_vec_pk = _all_reduce_small(_pack_vectors(gvec))
    d_vec_pk, m_vec_pk, v_vec_pk = _adamw_small(
        _pack_vectors(vec), g_vec_pk, _pack_vectors({n: inputs["m_" + n] for n, _ in VECTORS}),
        _pack_vectors({n: inputs["v_" + n] for n, _ in VECTORS}))
    g_vec, d_vec, m_vec, v_vec = (_unpack_vectors(t) for t in (g_vec_pk, d_vec_pk, m_vec_pk, v_vec_pk))

    def pick(mats, vecs, n):
        return mats[n] if n in mats else vecs[n]

    outs = [loss, grad_x[None]]
    for mats, vecs in ((g_mat, g_vec), (d_mat, d_vec), (m_mat, m_vec), (v_mat, v_vec)):
        outs += [pick(mats, vecs, n) for n in WEIGHT_ORDER]
    return tuple(outs)


_KIND = {n: kind for n, kind, _, _ in MATRICES}
_VIEW_OF = {"l0_w_uq_heads": "l0_w_uq", "l0_w_uk_heads": "l0_w_ukv", "l0_w_out_swa": "l0_w_out",
            "l0_w_out_mla": "l0_w_out"}
_MLA_QK = MLA_NOPE_DIM + MLA_ROPE_DIM


def _orient(name):
    return "t" if _KIND[_VIEW_OF.get(name, name)] == "c" else "n"


def _nope_rows():
    return (np.arange(MLA_HEADS * LANES) % LANES < MLA_NOPE_DIM)[:, None]


def _model_weights(wfull):
    w = dict(wfull)
    w["l0_w_in"] = jnp.pad(wfull["l0_w_in"], ((0, EVEN_IN_PAD - EVEN_IN), (0, 0)))
    uq = w.pop("l0_w_uq").reshape(MLA_HEADS, _MLA_QK, MLA_Q_RANK)
    w["l0_w_uq_heads"] = jnp.pad(uq, ((0, 0), (0, LANES - _MLA_QK), (0, 0))).reshape(MLA_HEADS * LANES, MLA_Q_RANK)
    w["l0_w_uk_heads"] = jnp.where(_nope_rows(), wfull["l0_w_ukv"], jnp.zeros_like(wfull["l0_w_ukv"]))
    wo = w.pop("l0_w_out")
    w["l0_w_out_swa"] = wo[:A_Q]
    w["l0_w_out_mla"] = jnp.pad(wo[A_Q:].reshape(MLA_HEADS, HEAD_DIM, D_MODEL),
                                ((0, 0), (LANES - HEAD_DIM, 0), (0, 0))).reshape(MLA_HEADS * LANES, D_MODEL)
    return w


def _matrix_grads(g):
    out = {n: g[n] for n, _, _, _ in MATRICES if n in g}
    out["l0_w_in"] = g["l0_w_in"][:EVEN_IN]
    out["l0_w_uq"] = g["l0_w_uq_heads"].reshape(MLA_HEADS, LANES, MLA_Q_RANK)[:, :_MLA_QK].reshape(-1, MLA_Q_RANK)
    uk = jnp.where(_nope_rows(), g["l0_w_uk_heads"], jnp.zeros_like(g["l0_w_uk_heads"]))
    out["l0_w_ukv"] = (g["l0_w_ukv"].astype(F32) + uk.astype(F32)).astype(g["l0_w_ukv"].dtype)
    out["l0_w_out"] = jnp.concatenate(
        [g["l0_w_out_swa"],
         g["l0_w_out_mla"].reshape(MLA_HEADS, LANES, D_MODEL)[:, LANES - HEAD_DIM:].reshape(-1, D_MODEL)], axis=0)
    return out


def _local_grads(wfull, vec, x, mem, positions, target):
    w = _model_weights(wfull)
    slots = {n: jnp.zeros(t.shape, GRAD_WIRE_DTYPE) for n, t in w.items()}
    tab64 = _rope_tables(positions, HEAD_DIM, 0, HEAD_DIM)
    tab_mla = _rope_tables(positions, MLA_ROPE_DIM, MLA_NOPE_DIM, LANES)
    diff = {"x": x, "slots": slots, "vec": vec}
    consts = {"w": w, "mem": mem, "tab64": tab64, "tab_mla": tab_mla, "target": target}
    loss_part, grads = jax.value_and_grad(_model_loss)(diff, consts)
    return loss_part, grads["x"], _matrix_grads(grads["slots"]), grads["vec"]


_INPUT_NAMES = (("x", "mem", "positions") + WEIGHT_ORDER + ("loss_target",)
                + tuple("m_" + n for n in WEIGHT_ORDER) + tuple("v_" + n for n in WEIGHT_ORDER))


def kernel(*args):
    assert len(args) == len(_INPUT_NAMES)
    return _step(dict(zip(_INPUT_NAMES, args)))
```

```python
import functools
import math

import numpy as np
import jax
import jax.numpy as jnp
from jax import lax
from jax.experimental import pallas as pl
from jax.experimental.pallas import tpu as pltpu

F32 = jnp.float32
BF16 = jnp.bfloat16

LANES = 128
VMEM_LIMIT_BYTES = 56 * 1024 * 1024
MM_VMEM_BUDGET = 40 * 1024 * 1024
MM_MIN_FLOP_PER_STEP = 1e9
BAND_UNITS_PER_STEP = 4
CAUSAL_ROW_CHAIN = 128
BAND_CHAINS_PER_BATCH = 4

D_MODEL = 1024
HEAD_DIM = 64
ROPE_THETA = 10000.0
NORM_EPS = 1e-6
BLOCK = 128
SWA_HEADS = 8
SWA_KV_HEADS = 2
SWA_WINDOW = 128
MLA_HEADS = 8
MLA_Q_RANK = 384
MLA_KV_RANK = 256
MLA_NOPE_DIM = 64
MLA_ROPE_DIM = 32
A_Q = SWA_HEADS * HEAD_DIM
A_KV = SWA_KV_HEADS * HEAD_DIM
EVEN_IN = A_Q + 2 * A_KV + MLA_Q_RANK + MLA_KV_RANK + MLA_ROPE_DIM
EVEN_IN_PAD = 1536
DIL_PATTERNS = ((128, 1), (512, 4), (2048, 16))
X_HEADS = 4
X_HEAD_DIM = 128
FFN_HIDDEN = 2816

ADAM_LR = 0.001
ADAM_B1 = 0.9
ADAM_B2 = 0.999
ADAM_EPS = 1e-08
ADAM_WD = 0.01
ADAM_STEP = 10

N_DEV = 8
GRAD_WIRE_DTYPE = BF16
NEG_MASK = -1e30
NEG_INIT = -1e20

MATRICES = (
    ("l0_w_in", "c", 1024, 1440), ("l0_w_uq", "c", 384, 768), ("l0_w_ukv", "c", 256, 1024),
    ("l0_w_out", "r", 1024, 1024), ("l0_w_xq", "r", 1024, 512), ("l0_w_xkv", "r", 1024, 1024),
    ("l0_w_xo", "c", 512, 1024), ("l0_w_gate", "c", 1024, 2816), ("l0_w_up", "c", 1024, 2816),
    ("l0_w_down", "r", 2816, 1024),
    ("l1_w_qkv", "c", 1024, 3072), ("l1_w_out", "r", 1024, 1024), ("l1_w_xq", "r", 1024, 512),
    ("l1_w_xkv", "r", 1024, 1024), ("l1_w_xo", "c", 512, 1024), ("l1_w_gate", "c", 1024, 2816),
    ("l1_w_up", "c", 1024, 2816), ("l1_w_down", "r", 2816, 1024),
)
VECTORS = (
    ("l0_mix_norm", 1024), ("l0_sinks", 8), ("l0_q_norm", 384), ("l0_kv_norm", 256), ("l0_x_norm", 1024),
    ("l0_mem_norm", 1024), ("l0_ffn_norm", 1024), ("l1_mix_norm", 1024), ("l1_x_norm", 1024),
    ("l1_mem_norm", 1024), ("l1_ffn_norm", 1024), ("final_norm", 1024),
)
WEIGHT_ORDER = (
    "l0_mix_norm", "l0_w_in", "l0_sinks", "l0_q_norm", "l0_w_uq", "l0_kv_norm", "l0_w_ukv", "l0_w_out", "l0_x_norm",
    "l0_mem_norm", "l0_w_xq", "l0_w_xkv", "l0_w_xo", "l0_ffn_norm", "l0_w_gate", "l0_w_up", "l0_w_down",
    "l1_mix_norm", "l1_w_qkv", "l1_w_out", "l1_x_norm", "l1_mem_norm", "l1_w_xq", "l1_w_xkv", "l1_w_xo",
    "l1_ffn_norm", "l1_w_gate", "l1_w_up", "l1_w_down", "final_norm",
)
PACK_COLS = 1024
PART_ROW_ALIGN = 16


def _part_rows(k, n):
    return -(-(k * n // N_DEV // PACK_COLS) // PART_ROW_ALIGN) * PART_ROW_ALIGN


MAT_ROWS_USED = sum(_part_rows(k, n) for _, _, k, n in MATRICES)
MAT_ROWS = -(-MAT_ROWS_USED // 256) * 256
VEC_ROWS = 16


def _pick(n, cands):
    for c in cands:
        if n % c == 0:
            return c
    return n


def _params(*sem):
    return pltpu.CompilerParams(dimension_semantics=sem, vmem_limit_bytes=VMEM_LIMIT_BYTES)


_DIMS = {"nn": (((1,), (0,)), ((), ())), "nt": (((1,), (1,)), ((), ())), "tn": (((0,), (0,)), ((), ()))}


def _rotate_block(xv, av, bmv, bpv, half, transpose):
    if transpose:
        return xv * av + pltpu.roll(xv * bmv, LANES - half, 1) + pltpu.roll(xv * bpv, half, 1)
    return xv * av + pltpu.roll(xv, half, 1) * bmv + pltpu.roll(xv, LANES - half, 1) * bpv


def _rotate_tile(t, tabs, half, transpose):
    av, bmv, bpv = tabs
    blocks = [_rotate_block(t[:, c:c + LANES], av, bmv, bpv, half, transpose) for c in range(0, t.shape[1], LANES)]
    return blocks[0] if len(blocks) == 1 else jnp.concatenate(blocks, axis=1)


def _div128(n, cap):
    d = (min(n, cap) // LANES) * LANES
    while d >= LANES:
        if n % d == 0:
            return d
        d -= LANES
    return n


def _mm_vmem_bytes(bm, bn, bk, nk, sa, sb, so, has_res):
    est = 2 * (bm * bk * sa + bk * bn * sb + bm * bn * so) + bm * bn * 4
    est += bm * bn * 4 if nk > 1 else 0
    est += 2 * bm * bn * 4 if has_res else 0
    est += bm * bk * 2 if sa == 4 else 0
    est += bk * bn * 2 if sb == 4 else 0
    return est


def _mm_tiles(m, n, k, sa, sb, so, has_res, mode):
    bn = _div128(n, 1536)
    kcap = 2048 if mode == "tn" else k
    for bm_cap in ((1408, 2816) if mode == "tn" else (512, 1024, 2048)):
        bm = _div128(m, bm_cap)
        bk = (min(k, kcap) // LANES) * LANES
        while bk > LANES and (k % bk or _mm_vmem_bytes(bm, bn, bk, k // bk, sa, sb, so, has_res) > MM_VMEM_BUDGET):
            bk -= LANES
        if 2 * bm * bn * bk >= MM_MIN_FLOP_PER_STEP or bm == m:
            break
    return bm, bn, bk


def _mm(a, b, mode, name, out_dtype=F32, res=None, rope=None):
    if mode == "nn":
        (m, k), (k2, n) = a.shape, b.shape
    elif mode == "nt":
        (m, k), (n, k2) = a.shape, b.shape
    else:
        (k, m), (k2, n) = a.shape, b.shape
    assert k == k2, (name, a.shape, b.shape)
    has_res = res is not None
    bm, bn, bk = _mm_tiles(m, n, k, a.dtype.itemsize, b.dtype.itemsize, jnp.dtype(out_dtype).itemsize, has_res, mode)
    nk = k // bk
    dims = _DIMS[mode]
    a_spec = pl.BlockSpec((bk, bm), lambda i, j, kk: (kk, i)) if mode == "tn" else pl.BlockSpec((bm, bk), lambda i, j, kk: (i, kk))
    b_spec = pl.BlockSpec((bn, bk), lambda i, j, kk: (j, kk)) if mode == "nt" else pl.BlockSpec((bk, bn), lambda i, j, kk: (kk, j))
    o_spec = pl.BlockSpec((bm, bn), lambda i, j, kk: (i, j))

    n_in = 2 + (1 if has_res else 0) + (3 if rope is not None else 0)

    def body(*refs):
        a_ref, b_ref = refs[0], refs[1]
        r_ref = refs[2] if has_res else None
        o_ref = refs[n_in]
        part = lax.dot_general(a_ref[...].astype(BF16), b_ref[...].astype(BF16), dims, preferred_element_type=F32)

        def finish(r):
            if has_res:
                r = r + r_ref[...]
            if rope is not None:
                r = _rotate_tile(r, tuple(t[...] for t in refs[n_in - 3:n_in]), rope[1], False)
            o_ref[...] = r.astype(out_dtype)

        if nk == 1:
            finish(part)
            return
        acc = refs[-1]
        kk = pl.program_id(2)

        @pl.when(kk == 0)
        def _():
            acc[...] = part

        @pl.when(jnp.logical_and(kk > 0, kk < nk - 1))
        def _():
            acc[...] += part

        @pl.when(kk == nk - 1)
        def _():
            finish(acc[...] + part)

    args = (a, b, res) if has_res else (a, b)
    in_specs = [a_spec, b_spec] + ([o_spec] if has_res else [])
    if rope is not None:
        args = args + tuple(rope[0])
        in_specs = in_specs + [pl.BlockSpec((bm, LANES), lambda i, j, kk: (i, 0))] * 3
    return pl.pallas_call(
        body, name=name, grid=(m // bm, n // bn, nk), in_specs=in_specs, out_specs=o_spec,
        out_shape=jax.ShapeDtypeStruct((m, n), out_dtype),
        scratch_shapes=[pltpu.VMEM((bm, bn), F32)] if nk > 1 else [],
        compiler_params=_params("parallel", "parallel", "arbitrary"),
    )(*args)


def _rms_fwd(x, g, name, out_dtype=BF16):
    s, d = x.shape
    bs = _pick(s, (512, 256, 128))

    def body(x_ref, g_ref, o_ref):
        xv = x_ref[...]
        r = lax.rsqrt(jnp.mean(xv * xv, axis=-1, keepdims=True) + NORM_EPS)
        o_ref[...] = ((xv * r) * g_ref[...]).astype(out_dtype)

    return pl.pallas_call(
        body, name=name, grid=(s // bs,),
        in_specs=[pl.BlockSpec((bs, d), lambda i: (i, 0)), pl.BlockSpec((1, d), lambda i: (0, 0))],
        out_specs=pl.BlockSpec((bs, d), lambda i: (i, 0)), out_shape=jax.ShapeDtypeStruct((s, d), out_dtype),
        compiler_params=_params("parallel"),
    )(x, g.reshape(1, d))


def _rms_bwd(x, g, dy, name, dres=None):
    s, d = x.shape
    bs = _pick(s, (512, 256, 128))
    has_res = dres is not None

    def body(*refs):
        if has_res:
            x_ref, g_ref, dy_ref, r_ref, dx_ref, dg_ref = refs
        else:
            x_ref, g_ref, dy_ref, dx_ref, dg_ref = refs
        i = pl.program_id(0)
        xv = x_ref[...]
        dy = dy_ref[...]
        r = lax.rsqrt(jnp.mean(xv * xv, axis=-1, keepdims=True) + NORM_EPS)
        xh = xv * r
        dxh = dy * g_ref[...]
        dx = r * (dxh - xh * jnp.mean(dxh * xh, axis=-1, keepdims=True))
        if has_res:
            dx = dx + r_ref[...]
        dx_ref[...] = dx

        @pl.when(i == 0)
        def _():
            dg_ref[...] = jnp.zeros_like(dg_ref)

        dg_ref[...] += jnp.sum(dy * xh, axis=0, keepdims=True)

    row = pl.BlockSpec((bs, d), lambda i: (i, 0))
    vec = pl.BlockSpec((1, d), lambda i: (0, 0))
    args = (x, g.reshape(1, d), dy) + ((dres,) if has_res else ())
    dx, dg = pl.pallas_call(
        body, name=name, grid=(s // bs,), in_specs=[row, vec, row] + ([row] if has_res else []),
        out_specs=(row, vec), out_shape=(jax.ShapeDtypeStruct((s, d), F32), jax.ShapeDtypeStruct((1, d), F32)),
        compiler_params=_params("arbitrary"),
    )(*args)
    return dx, dg.reshape(d)


def _rope_tables(positions, dh, offset, period):
    role = np.zeros(LANES, np.int32)
    for base in range(0, LANES, period):
        role[base + offset:base + offset + dh // 2] = 1
        role[base + offset + dh // 2:base + offset + dh] = 2
    inv_freq = ROPE_THETA ** (-jnp.arange(0, dh, 2, dtype=F32) / dh)
    one_period = jnp.concatenate([jnp.zeros((offset,), F32), inv_freq, inv_freq,
                                  jnp.zeros((period - offset - dh,), F32)])
    ang = positions.astype(F32)[:, None] * jnp.tile(one_period, LANES // period)[None, :]
    c, s = jnp.cos(ang), jnp.sin(ang)
    role = role[None, :]
    a = jnp.where(role == 0, 1.0, c).astype(F32)
    bm = jnp.where(role == 2, s, 0.0).astype(F32)
    bp = jnp.where(role == 1, -s, 0.0).astype(F32)
    return a, bm, bp


def _rope_apply(x, tabs, half, transpose, name, shared=None, sum_blocks=False):
    s, w = x.shape
    bs = _pick(s, (512, 256, 128))
    nc = w // LANES
    a, bm, bp = tabs
    has_shared = shared is not None

    def body(*refs):
        x_ref, a_ref, bm_ref, bp_ref = refs[:4]
        o_ref = refs[5] if has_shared else refs[4]
        av, bmv, bpv = a_ref[...], bm_ref[...], bp_ref[...]
        total = None
        for c in range(nc):
            sl = slice(c * LANES, (c + 1) * LANES)
            xv = x_ref[:, sl]
            if has_shared:
                xv = xv + refs[4][...]
            out = _rotate_block(xv, av, bmv, bpv, half, transpose)
            o_ref[:, sl] = out
            total = out if total is None else total + out
        if sum_blocks:
            refs[-1][...] = total

    row = pl.BlockSpec((bs, w), lambda i: (i, 0))
    tab = pl.BlockSpec((bs, LANES), lambda i: (i, 0))
    out_shape = jax.ShapeDtypeStruct((s, w), F32)
    return pl.pallas_call(
        body, name=name, grid=(s // bs,), in_specs=[row, tab, tab, tab] + ([tab] if has_shared else []),
        out_specs=(row, tab) if sum_blocks else row,
        out_shape=(out_shape, jax.ShapeDtypeStruct((s, LANES), F32)) if sum_blocks else out_shape,
        compiler_params=_params("parallel"),
    )(x, a, bm, bp, *((shared,) if has_shared else ()))


def _make_rope(half, name):
    @jax.custom_vjp
    def rope(x, a, bm, bp):
        return _rope_apply(x, (a, bm, bp), half, False, name + "_fwd")

    def fwd(x, a, bm, bp):
        return rope(x, a, bm, bp), (a, bm, bp)

    def bwd(tabs, dy):
        return _rope_apply(dy, tabs, half, True, name + "_bwd"), None, None, None

    rope.defvjp(fwd, bwd)
    return rope


def _make_rope_shared(half, name):
    @jax.custom_vjp
    def rope(x, shared, a, bm, bp):
        return _rope_apply(x, (a, bm, bp), half, False, name + "_fwd", shared=shared)

    def fwd(x, shared, a, bm, bp):
        return rope(x, shared, a, bm, bp), (a, bm, bp)

    def bwd(tabs, dy):
        dx, dshared = _rope_apply(dy, tabs, half, True, name + "_bwd", sum_blocks=True)
        return dx, dshared, None, None, None

    rope.defvjp(fwd, bwd)
    return rope


class AttnCfg:
    def __init__(self, mode, scale, hpb, bq, bk, upb, max_dist=0):
        self.mode, self.scale, self.hpb, self.bq, self.bk, self.upb, self.max_dist = mode, scale, hpb, bq, bk, upb, max_dist


def _kv_of_q(cfg, nq, nk):
    if cfg.mode == "causal":
        return nk, lambda i, j: (jnp.minimum(j, i), j <= i)
    if cfg.mode == "band":
        return 2, lambda i, j: (jnp.maximum(i - 1 + j, 0), i - 1 + j >= 0)
    return nk, lambda i, j: (j, j >= 0)


def _q_of_kv(cfg, nq, nk):
    if cfg.mode == "causal":
        return nq, lambda kb, j: (jnp.maximum(j, kb), j >= kb)
    if cfg.mode == "band":
        return 2, lambda kb, j: (jnp.minimum(kb + j, nq - 1), kb + j <= nq - 1)
    return nq, lambda kb, j: (j, j >= 0)


def _attn_mask(cfg, i, kb):
    if cfg.mode == "full":
        return None
    qpos = i * cfg.bq + lax.broadcasted_iota(jnp.int32, (cfg.bq, cfg.bk), 0)
    kpos = kb * cfg.bk + lax.broadcasted_iota(jnp.int32, (cfg.bq, cfg.bk), 1)
    dist = qpos - kpos
    if cfg.mode == "causal":
        return dist >= 0
    return (dist >= 0) & (dist <= cfg.max_dist)


def _lane_masks():
    lane = lax.broadcasted_iota(jnp.int32, (1, LANES), 1)
    lo = lane < HEAD_DIM
    return [lo, jnp.logical_not(lo)]


def _sel(mask, v):
    return jnp.where(mask, v, jnp.zeros_like(v))


_NT = (((1,), (1,)), ((), ()))
_NN = (((1,), (0,)), ((), ()))
_TN = (((0,), (0,)), ((), ()))
_BNT = (((2,), (2,)), ((0,), (0,)))
_BNN = (((2,), (1,)), ((0,), (0,)))


def _dot(a, b, dims):
    return lax.dot_general(a, b, dims, preferred_element_type=F32)


def _attn_fwd(q, k, v, sinkrow, cfg, name):
    sq, w = q.shape
    sk = k.shape[0]
    bq, bk, upb, hpb = cfg.bq, cfg.bk, cfg.upb, cfg.hpb
    nq, nk, nub = sq // bq, sk // bk, w // (LANES * upb)
    nj, sched = _kv_of_q(cfg, nq, nk)
    wb = LANES * upb
    has_sink = sinkrow is not None

    def body(*refs):
        if has_sink:
            q_ref, k_ref, v_ref, s_ref, o_ref, l_ref, m_sc, l_sc, acc_sc = refs
        else:
            q_ref, k_ref, v_ref, o_ref, l_ref, m_sc, l_sc, acc_sc = refs
        i, j = pl.program_id(1), pl.program_id(2)
        kb, active = sched(i, j)
        lms = _lane_masks()

        @pl.when(j == 0)
        def _():
            for u in range(upb):
                for a in range(hpb):
                    if has_sink:
                        srow = s_ref[:, u * LANES:(u + 1) * LANES]
                        sk_a = jnp.max(jnp.where(lms[a], srow, -jnp.inf), axis=-1, keepdims=True)
                        m_sc[u * hpb + a] = jnp.broadcast_to(sk_a, (bq, 1))
                        l_sc[u * hpb + a] = jnp.ones((bq, 1), F32)
                    else:
                        m_sc[u * hpb + a] = jnp.full((bq, 1), NEG_INIT, F32)
                        l_sc[u * hpb + a] = jnp.zeros((bq, 1), F32)
            acc_sc[...] = jnp.zeros_like(acc_sc)

        @pl.when(active)
        def _():
            mask = _attn_mask(cfg, i, kb)
            for u in range(upb):
                sl = slice(u * LANES, (u + 1) * LANES)
                qv = q_ref[:, sl].astype(BF16)
                kv = k_ref[:, sl].astype(BF16)
                vv = v_ref[:, sl].astype(BF16)
                pv_tot, alphas = None, []
                for a in range(hpb):
                    idx = u * hpb + a
                    qa = _sel(lms[a], qv) if hpb == 2 else qv
                    s = _dot(qa, kv, _NT) * cfg.scale
                    if mask is not None:
                        s = jnp.where(mask, s, NEG_MASK)
                    m_prev = m_sc[idx]
                    m_new = jnp.maximum(m_prev, jnp.max(s, axis=-1, keepdims=True))
                    alpha = jnp.exp(m_prev - m_new)
                    p = jnp.exp(s - m_new)
                    l_sc[idx] = alpha * l_sc[idx] + jnp.sum(p, axis=-1, keepdims=True)
                    m_sc[idx] = m_new
                    va = _sel(lms[a], vv) if hpb == 2 else vv
                    pv = _dot(p.astype(BF16), va, _NN)
                    pv_tot = pv if pv_tot is None else pv_tot + pv
                    alphas.append(alpha)
                af = alphas[0] if hpb == 1 else jnp.where(lms[0], alphas[0], alphas[1])
                acc_sc[u] = acc_sc[u] * af + pv_tot

        @pl.when(j == nj - 1)
        def _():
            for u in range(upb):
                sl = slice(u * LANES, (u + 1) * LANES)
                if hpb == 1:
                    lf = jnp.broadcast_to(l_sc[u], (bq, LANES))
                    mf = jnp.broadcast_to(m_sc[u], (bq, LANES))
                else:
                    lf = jnp.where(lms[0], l_sc[2 * u], l_sc[2 * u + 1])
                    mf = jnp.where(lms[0], m_sc[2 * u], m_sc[2 * u + 1])
                o_ref[:, sl] = acc_sc[u] / lf
                l_ref[:, sl] = mf + jnp.log(lf)

    qspec = pl.BlockSpec((bq, wb), lambda ub, i, j: (i, ub))
    kspec = pl.BlockSpec((bk, wb), lambda ub, i, j: (sched(i, j)[0], ub))
    in_specs = [qspec, kspec, kspec] + ([pl.BlockSpec((1, wb), lambda ub, i, j: (0, ub))] if has_sink else [])
    args = (q, k, v) + ((sinkrow,) if has_sink else ())
    return pl.pallas_call(
        body, name=name, grid=(nub, nq, nj), in_specs=in_specs, out_specs=(qspec, qspec),
        out_shape=(jax.ShapeDtypeStruct((sq, w), F32), jax.ShapeDtypeStruct((sq, w), F32)),
        scratch_shapes=[pltpu.VMEM((upb * hpb, bq, 1), F32), pltpu.VMEM((upb * hpb, bq, 1), F32),
                        pltpu.VMEM((upb, bq, LANES), F32)],
        compiler_params=_params("parallel", "parallel", "arbitrary"),
    )(*args)


def _softmax_grad_terms(cfg, lms, a, qv, kv, vv, dob, prod, lv, mask):
    hpb = cfg.hpb
    if hpb == 2:
        t = jnp.sum(_sel(lms[a], prod), axis=-1, keepdims=True)
        lse = jnp.max(jnp.where(lms[a], lv, -jnp.inf), axis=-1, keepdims=True)
        qa, doa = _sel(lms[a], qv), _sel(lms[a], dob)
    else:
        t = jnp.sum(prod, axis=-1, keepdims=True)
        lse = jnp.max(lv, axis=-1, keepdims=True)
        qa, doa = qv, dob
    s = _dot(qa, kv, _NT) * cfg.scale
    if mask is not None:
        s = jnp.where(mask, s, NEG_MASK)
    p = jnp.exp(s - lse)
    dp = _dot(doa, vv, _NT)
    ds = (p * (dp - t)) * cfg.scale
    return p, ds, qa, doa, t


def _attn_dq(q, k, v, o, lse, do, sinkrow, cfg, name):
    sq, w = q.shape
    sk = k.shape[0]
    bq, bk, upb, hpb = cfg.bq, cfg.bk, cfg.upb, cfg.hpb
    nq, nk, nub = sq // bq, sk // bk, w // (LANES * upb)
    nj, sched = _kv_of_q(cfg, nq, nk)
    wb = LANES * upb
    has_sink = sinkrow is not None

    def body(*refs):
        if has_sink:
            q_ref, k_ref, v_ref, o_ref, l_ref, do_ref, s_ref, dq_ref, dsink_ref, acc = refs
        else:
            q_ref, k_ref, v_ref, o_ref, l_ref, do_ref, dq_ref, acc = refs
        i, j = pl.program_id(1), pl.program_id(2)
        kb, active = sched(i, j)
        lms = _lane_masks()

        @pl.when(j == 0)
        def _():
            acc[...] = jnp.zeros_like(acc)

        @pl.when(active)
        def _():
            mask = _attn_mask(cfg, i, kb)
            for u in range(upb):
                sl = slice(u * LANES, (u + 1) * LANES)
                qv = q_ref[:, sl].astype(BF16)
                kv = k_ref[:, sl].astype(BF16)
                vv = v_ref[:, sl].astype(BF16)
                dov = do_ref[:, sl]
                prod = dov * o_ref[:, sl]
                dob = dov.astype(BF16)
                lv = l_ref[:, sl]
                tot = None
                for a in range(hpb):
                    _, ds, _, _, _ = _softmax_grad_terms(cfg, lms, a, qv, kv, vv, dob, prod, lv, mask)
                    ka = _sel(lms[a], kv) if hpb == 2 else kv
                    c = _dot(ds.astype(BF16), ka, _NN)
                    tot = c if tot is None else tot + c
                acc[u] = acc[u] + tot

        @pl.when(j == nj - 1)
        def _():
            for u in range(upb):
                dq_ref[:, u * LANES:(u + 1) * LANES] = acc[u]
            if has_sink:
                @pl.when(i == 0)
                def _():
                    dsink_ref[...] = jnp.zeros_like(dsink_ref)

                for u in range(upb):
                    sl = slice(u * LANES, (u + 1) * LANES)
                    prod = do_ref[:, sl] * o_ref[:, sl]
                    t0 = jnp.sum(_sel(lms[0], prod), axis=-1, keepdims=True)
                    t1 = jnp.sum(_sel(lms[1], prod), axis=-1, keepdims=True)
                    tf = jnp.where(lms[0], t0, t1)
                    rs = -jnp.exp(s_ref[:, sl] - l_ref[:, sl]) * tf
                    dsink_ref[0:1, sl] += jnp.sum(rs, axis=0, keepdims=True)

    qspec = pl.BlockSpec((bq, wb), lambda ub, i, j: (i, ub))
    kspec = pl.BlockSpec((bk, wb), lambda ub, i, j: (sched(i, j)[0], ub))
    in_specs = [qspec, kspec, kspec, qspec, qspec, qspec]
    args = (q, k, v, o, lse, do)
    out_specs = qspec
    out_shape = jax.ShapeDtypeStruct((sq, w), F32)
    sem = ("parallel", "parallel", "arbitrary")
    if has_sink:
        in_specs = in_specs + [pl.BlockSpec((1, wb), lambda ub, i, j: (0, ub))]
        args = args + (sinkrow,)
        out_specs = (qspec, pl.BlockSpec((8, wb), lambda ub, i, j: (0, ub)))
        out_shape = (out_shape, jax.ShapeDtypeStruct((8, w), F32))
        sem = ("parallel", "arbitrary", "arbitrary")
    return pl.pallas_call(
        body, name=name, grid=(nub, nq, nj), in_specs=in_specs, out_specs=out_specs, out_shape=out_shape,
        scratch_shapes=[pltpu.VMEM((upb, bq, LANES), F32)], compiler_params=_params(*sem),
    )(*args)


def _attn_dkv(q, k, v, o, lse, do, cfg, name):
    sq, w = q.shape
    sk = k.shape[0]
    bq, bk, upb, hpb = cfg.bq, cfg.bk, cfg.upb, cfg.hpb
    nq, nk, nub = sq // bq, sk // bk, w // (LANES * upb)
    nj, sched = _q_of_kv(cfg, nq, nk)
    wb = LANES * upb

    def body(q_ref, k_ref, v_ref, o_ref, l_ref, do_ref, dk_ref, dv_ref, dk_acc, dv_acc):
        kb, j = pl.program_id(1), pl.program_id(2)
        i, active = sched(kb, j)
        lms = _lane_masks()

        @pl.when(j == 0)
        def _():
            dk_acc[...] = jnp.zeros_like(dk_acc)
            dv_acc[...] = jnp.zeros_like(dv_acc)

        @pl.when(active)
        def _():
            mask = _attn_mask(cfg, i, kb)
            for u in range(upb):
                sl = slice(u * LANES, (u + 1) * LANES)
                qv = q_ref[:, sl].astype(BF16)
                kv = k_ref[:, sl].astype(BF16)
                vv = v_ref[:, sl].astype(BF16)
                dov = do_ref[:, sl]
                prod = dov * o_ref[:, sl]
                dob = dov.astype(BF16)
                lv = l_ref[:, sl]
                dk_tot, dv_tot = None, None
                for a in range(hpb):
                    p, ds, qa, doa, _ = _softmax_grad_terms(cfg, lms, a, qv, kv, vv, dob, prod, lv, mask)
                    dvc = _dot(p.astype(BF16), doa, _TN)
                    dkc = _dot(ds.astype(BF16), qa, _TN)
                    dv_tot = dvc if dv_tot is None else dv_tot + dvc
                    dk_tot = dkc if dk_tot is None else dk_tot + dkc
                dk_acc[u] = dk_acc[u] + dk_tot
                dv_acc[u] = dv_acc[u] + dv_tot

        @pl.when(j == nj - 1)
        def _():
            for u in range(upb):
                sl = slice(u * LANES, (u + 1) * LANES)
                dk_ref[:, sl] = dk_acc[u]
                dv_ref[:, sl] = dv_acc[u]

    qspec = pl.BlockSpec((bq, wb), lambda ub, kb, j: (sched(kb, j)[0], ub))
    kspec = pl.BlockSpec((bk, wb), lambda ub, kb, j: (kb, ub))
    return pl.pallas_call(
        body, name=name, grid=(nub, nk, nj), in_specs=[qspec, kspec, kspec, qspec, qspec, qspec],
        out_specs=(kspec, kspec),
        out_shape=(jax.ShapeDtypeStruct((sk, w), F32), jax.ShapeDtypeStruct((sk, w), F32)),
        scratch_shapes=[pltpu.VMEM((upb, bk, LANES), F32), pltpu.VMEM((upb, bk, LANES), F32)],
        compiler_params=_params("parallel", "parallel", "arbitrary"),
    )(q, k, v, o, lse, do)


def _band_masks(max_dist):
    assert BLOCK - 1 <= max_dist <= BLOCK
    r = lax.broadcasted_iota(jnp.int32, (BLOCK, BLOCK), 0)
    c = lax.broadcasted_iota(jnp.int32, (BLOCK, BLOCK), 1)
    return (BLOCK + r - c) <= max_dist, r >= c


def _stack_heads(t):
    return jnp.concatenate([t, t], axis=0)


def _head_terms(lms, a, prod, lv):
    t = jnp.sum(_sel(lms[a], prod), axis=-1, keepdims=True)
    lse = jnp.max(jnp.where(lms[a], lv, -jnp.inf), axis=-1, keepdims=True)
    return t, lse


class _Residue:
    def __init__(self, ref, r, dil):
        self.ref, self.rows = ref, pl.ds(r, BLOCK, stride=dil)

    def __getitem__(self, idx):
        return self.ref[self.rows, idx[1]]

    def __setitem__(self, idx, val):
        self.ref[self.rows, idx[1]] = val


def _residues(refs, dil):
    if dil == 1:
        return [tuple(refs)]
    return [tuple(_Residue(x, r, dil) for x in refs) for r in range(dil)]


def _band_fwd(q, k, v, sinkrow, scale, max_dist, upb, dil, name):
    sq, w = q.shape
    rb = BLOCK * dil
    nq, nub, wb = sq // rb, w // (LANES * upb), LANES * upb
    has_sink = sinkrow is not None

    def body(*refs):
        s_ref = refs[5] if has_sink else None
        lms = _lane_masks()
        mprev, mcur = _band_masks(max_dist)
        mprev = jnp.logical_and(mprev, pl.program_id(1) > 0)
        mask2 = _stack_heads(jnp.concatenate([mprev, mcur], axis=1))
        chains = [(rr, slice(u * LANES, (u + 1) * LANES))
                  for rr in _residues(refs[:5] + refs[-2:], dil) for u in range(upb)]
        for g0 in range(0, len(chains), BAND_CHAINS_PER_BATCH):
            group = chains[g0:g0 + BAND_CHAINS_PER_BATCH]
            qs, kcat, vcat, sks = [], [], [], []
            for (q_ref, kp_ref, kc_ref, vp_ref, vc_ref, _, _), sl in group:
                qv = (q_ref[:, sl] * scale).astype(BF16)
                qs.append(jnp.concatenate([_sel(lms[0], qv), _sel(lms[1], qv)], axis=0))
                kcat.append(jnp.concatenate([kp_ref[:, sl].astype(BF16), kc_ref[:, sl].astype(BF16)], axis=0))
                vcat.append(jnp.concatenate([vp_ref[:, sl].astype(BF16), vc_ref[:, sl].astype(BF16)], axis=0))
                if has_sink:
                    sks.append(s_ref[sl.start // LANES])
            qs, kcat, vcat = jnp.stack(qs), jnp.stack(kcat), jnp.stack(vcat)
            sc = jnp.where(mask2[None], _dot(qs, kcat, _BNT), NEG_MASK)
            m = jnp.max(sc, axis=-1, keepdims=True)
            p = jnp.exp(sc - m)
            l = jnp.sum(p, axis=-1, keepdims=True)
            pv = _dot(p.astype(BF16), vcat, _BNN)
            if has_sink:
                sk2 = jnp.stack(sks)
                m_all = jnp.maximum(m, sk2)
                shrink = jnp.exp(m - m_all)
                l = l * shrink + jnp.exp(sk2 - m_all)
                pv, m = pv * shrink, m_all
            o2 = pv / l
            lse2 = m + jnp.log(l)
            for gi, ((_, _, _, _, _, o_ref, l_ref), sl) in enumerate(group):
                o_ref[:, sl] = jnp.where(lms[0], o2[gi, :BLOCK], o2[gi, BLOCK:])
                l_ref[:, sl] = jnp.where(lms[0], lse2[gi, :BLOCK], lse2[gi, BLOCK:])

    cur = pl.BlockSpec((rb, wb), lambda ub, i: (i, ub))
    prev = pl.BlockSpec((rb, wb), lambda ub, i: (jnp.maximum(i - 1, 0), ub))
    in_specs = [cur, prev, cur, prev, cur]
    in_specs += [pl.BlockSpec((upb, 2 * BLOCK, 1), lambda ub, i: (ub, 0, 0))] if has_sink else []
    args = (q, k, k, v, v) + ((sinkrow,) if has_sink else ())
    return pl.pallas_call(
        body, name=name, grid=(nub, nq), in_specs=in_specs, out_specs=(cur, cur),
        out_shape=(jax.ShapeDtypeStruct((sq, w), F32), jax.ShapeDtypeStruct((sq, w), F32)),
        compiler_params=_params("parallel", "parallel"),
    )(*args)


def _band_dq(q, k, v, o, lse, do, sinkrow, scale, max_dist, upb, dil, name):
    sq, w = q.shape
    rb = BLOCK * dil
    nq, nub, wb = sq // rb, w // (LANES * upb), LANES * upb
    has_sink = sinkrow is not None

    def body(*refs):
        if has_sink:
            s_ref, dq_block, dsink_ref = refs[8], refs[9], refs[10]
        else:
            dq_block = refs[8]
        i = pl.program_id(1)
        lms = _lane_masks()
        mprev, mcur = _band_masks(max_dist)
        mprev = jnp.logical_and(mprev, i > 0)
        mask2 = _stack_heads(jnp.concatenate([mprev, mcur], axis=1))
        if has_sink:
            @pl.when(i == 0)
            def _():
                dsink_ref[...] = jnp.zeros_like(dsink_ref)

        chains = [(rr, slice(u * LANES, (u + 1) * LANES))
                  for rr in _residues(refs[:8] + (dq_block,), dil) for u in range(upb)]
        for g0 in range(0, len(chains), BAND_CHAINS_PER_BATCH):
            group = chains[g0:g0 + BAND_CHAINS_PER_BATCH]
            qs, dos, kcat, vcat, t2, lse2 = [], [], [], [], [], []
            for (q_ref, kp_ref, kc_ref, vp_ref, vc_ref, o_ref, l_ref, do_ref, _), sl in group:
                qv = (q_ref[:, sl] * scale).astype(BF16)
                dov = do_ref[:, sl]
                prod = dov * o_ref[:, sl]
                dob = dov.astype(BF16)
                lv = l_ref[:, sl]
                (t0, lse0), (t1, lse1) = _head_terms(lms, 0, prod, lv), _head_terms(lms, 1, prod, lv)
                t2.append(jnp.concatenate([t0, t1], axis=0))
                lse2.append(jnp.concatenate([lse0, lse1], axis=0))
                qs.append(jnp.concatenate([_sel(lms[0], qv), _sel(lms[1], qv)], axis=0))
                dos.append(jnp.concatenate([_sel(lms[0], dob), _sel(lms[1], dob)], axis=0))
                kcat.append(jnp.concatenate([kp_ref[:, sl].astype(BF16), kc_ref[:, sl].astype(BF16)], axis=0))
                vcat.append(jnp.concatenate([vp_ref[:, sl].astype(BF16), vc_ref[:, sl].astype(BF16)], axis=0))
                if has_sink:
                    rs = -jnp.exp(s_ref[:, sl] - lv) * jnp.where(lms[0], t0, t1)
                    dsink_ref[0:1, sl] += jnp.sum(rs, axis=0, keepdims=True)
            qs, dos, kcat, vcat = jnp.stack(qs), jnp.stack(dos), jnp.stack(kcat), jnp.stack(vcat)
            p = jnp.exp(jnp.where(mask2[None], _dot(qs, kcat, _BNT), NEG_MASK) - jnp.stack(lse2))
            ds = (p * (_dot(dos, vcat, _BNT) - jnp.stack(t2))).astype(BF16)
            dq2 = _dot(ds, kcat, _BNN) * scale
            for gi, ((_, _, _, _, _, _, _, _, dq_ref), sl) in enumerate(group):
                dq_ref[:, sl] = jnp.where(lms[0], dq2[gi, :BLOCK], dq2[gi, BLOCK:])

    cur = pl.BlockSpec((rb, wb), lambda ub, i: (i, ub))
    prev = pl.BlockSpec((rb, wb), lambda ub, i: (jnp.maximum(i - 1, 0), ub))
    in_specs = [cur, prev, cur, prev, cur, cur, cur, cur]
    args = (q, k, k, v, v, o, lse, do)
    out_specs, out_shape = cur, jax.ShapeDtypeStruct((sq, w), F32)
    sem = ("parallel", "parallel")
    if has_sink:
        in_specs = in_specs + [pl.BlockSpec((1, wb), lambda ub, i: (0, ub))]
        args = args + (sinkrow,)
        out_specs = (cur, pl.BlockSpec((8, wb), lambda ub, i: (0, ub)))
        out_shape = (out_shape, jax.ShapeDtypeStruct((8, w), F32))
        sem = ("parallel", "arbitrary")
    return pl.pallas_call(
        body, name=name, grid=(nub, nq), in_specs=in_specs, out_specs=out_specs, out_shape=out_shape,
        compiler_params=_params(*sem),
    )(*args)


def _band_dkv(q, k, v, o, lse, do, scale, max_dist, upb, dil, name):
    sq, w = q.shape
    rb = BLOCK * dil
    nq, nub, wb = sq // rb, w // (LANES * upb), LANES * upb

    def body(*refs):
        kb = pl.program_id(1)
        lms = _lane_masks()
        key = lax.broadcasted_iota(jnp.int32, (BLOCK, BLOCK), 0)
        qry = lax.broadcasted_iota(jnp.int32, (BLOCK, BLOCK), 1)
        msame = qry >= key
        mnext = jnp.logical_and((BLOCK + qry - key) <= max_dist, kb < nq - 1)
        mask4 = jnp.concatenate([msame, msame, mnext, mnext], axis=1)
        chains =[(rr, slice(u * LANES, (u + 1) * LANES)) for rr in _residues(refs, dil) for u in range(upb)]
        for g0 in range(0, len(chains), BAND_CHAINS_PER_BATCH):
            group = chains[g0:g0 + BAND_CHAINS_PER_BATCH]
            kvs, vvs, qss, doss, t4s, lse4s = [], [], [], [], [], []
            for (k_ref, v_ref, qs_ref, qn_ref, os_ref, on_ref, ls_ref, ln_ref, dos_ref, don_ref, _, _), sl in group:
                kvs.append(k_ref[:, sl].astype(BF16))
                vvs.append(v_ref[:, sl].astype(BF16))
                qparts, doparts, tparts, lparts = [], [], [], []
                for q_ref, o_ref, l_ref, do_ref in ((qs_ref, os_ref, ls_ref, dos_ref),
                                                    (qn_ref, on_ref, ln_ref, don_ref)):
                    qv = (q_ref[:, sl] * scale).astype(BF16)
                    dov = do_ref[:, sl]
                    prod_t = (dov * o_ref[:, sl]).T
                    dob = dov.astype(BF16)
                    lse_t = l_ref[:, sl].T
                    for a in range(2):
                        lanes = slice(a * HEAD_DIM, (a + 1) * HEAD_DIM)
                        qparts.append(_sel(lms[a], qv))
                        doparts.append(_sel(lms[a], dob))
                        tparts.append(jnp.sum(prod_t[lanes, :], axis=0, keepdims=True))
                        lparts.append(lse_t[a * HEAD_DIM:a * HEAD_DIM + 1, :])
                qss.append(jnp.concatenate(qparts, axis=0))
                doss.append(jnp.concatenate(doparts, axis=0))
                t4s.append(jnp.concatenate(tparts, axis=1))
                lse4s.append(jnp.concatenate(lparts, axis=1))
            kv, vv, qs, dos = jnp.stack(kvs), jnp.stack(vvs), jnp.stack(qss), jnp.stack(doss)
            p = jnp.exp(jnp.where(mask4[None], _dot(kv, qs, _BNT), NEG_MASK) - jnp.stack(lse4s))
            ds = (p * (_dot(vv, dos, _BNT) - jnp.stack(t4s))).astype(BF16)
            dv = _dot(p.astype(BF16), dos, _BNN)
            dk = _dot(ds, qs, _BNN)
            for gi, (rr, sl) in enumerate(group):
                rr[-1][:, sl] = dv[gi]
                rr[-2][:, sl] = dk[gi]

    same = pl.BlockSpec((rb, wb), lambda ub, kb: (kb, ub))
    nxt = pl.BlockSpec((rb, wb), lambda ub, kb: (jnp.minimum(kb + 1, nq - 1), ub))
    return pl.pallas_call(
        body, name=name, grid=(nub, nq), in_specs=[same, same, same, nxt, same, nxt, same, nxt, same, nxt],
        out_specs=(same, same),
        out_shape=(jax.ShapeDtypeStruct((sq, w), F32), jax.ShapeDtypeStruct((sq, w), F32)),
        compiler_params=_params("parallel", "parallel"),
    )(k, v, q, q, o, o, lse, lse, do, do)


def _make_band_attention(scale, max_dist, upb, name):
    @jax.custom_vjp
    def attn(q, k, v, sinks):
        return _band_fwd(q, k, v, _sink_col(sinks), scale, max_dist, upb, 1, name + "_fwd")[0]

    def fwd(q, k, v, sinks):
        o, lse = _band_fwd(q, k, v, _sink_col(sinks), scale, max_dist, upb, 1, name + "_fwd")
        return o, (q, k, v, o, lse, sinks)

    def bwd(res, do):
        q, k, v, o, lse, sinks = res
        dq, dsink = _band_dq(q, k, v, o, lse, do, _sink_row(sinks), scale, max_dist, upb, 1, name + "_dq")
        dk, dv = _band_dkv(q, k, v, o, lse, do, scale, max_dist, upb, 1, name + "_dkv")
        return dq, dk, dv, dsink[0].reshape(-1, HEAD_DIM)[:, 0]

    attn.defvjp(fwd, bwd)
    return attn


def _triangle(n, by_key):
    if by_key:
        pairs = [(i, kb) for kb in range(n) for i in range(kb, n)]
    else:
        pairs = [(i, j) for i in range(n) for j in range(i + 1)]
    qi = np.asarray([p[0] for p in pairs], np.int32)
    kj = np.asarray([p[1] for p in pairs], np.int32)
    return jnp.asarray(qi), jnp.asarray(kj)


def _causal_fwd(q, k, v, scale, blk, name):
    s, w = q.shape
    nq, nub = s // blk, w // LANES
    qi, kj = _triangle(nq, by_key=False)

    def body(qi_ref, kj_ref, q_ref, k_ref, v_ref, o_ref, l_ref, m_sc, l_sc, acc_sc):
        t = pl.program_id(1)
        i, j = qi_ref[t], kj_ref[t]

        @pl.when(j == 0)
        def _():
            m_sc[...] = jnp.full_like(m_sc, NEG_INIT)
            l_sc[...] = jnp.zeros_like(l_sc)
            acc_sc[...] = jnp.zeros_like(acc_sc)

        def step(diagonal):
            kv, vv = k_ref[...].astype(BF16), v_ref[...].astype(BF16)
            chains = range(0, blk, CAUSAL_ROW_CHAIN)
            scs = [_dot((q_ref[c0:c0 + CAUSAL_ROW_CHAIN, :] * scale).astype(BF16), kv, _NT) for c0 in chains]
            m_all, l_all, acc_all = m_sc[...], l_sc[...], acc_sc[...]
            m_out, l_out, acc_out = [], [], []
            for sc, c0 in zip(scs, chains):
                rows = slice(c0, c0 + CAUSAL_ROW_CHAIN)
                if diagonal:
                    r = c0 + lax.broadcasted_iota(jnp.int32, (CAUSAL_ROW_CHAIN, blk), 0)
                    c = lax.broadcasted_iota(jnp.int32, (CAUSAL_ROW_CHAIN, blk), 1)
                    sc = jnp.where(r >= c, sc, NEG_MASK)
                m_prev = m_all[rows]
                m_new = jnp.maximum(m_prev, jnp.max(sc, axis=-1, keepdims=True))
                alpha = jnp.exp(m_prev - m_new)
                p = jnp.exp(sc - m_new)
                l_out.append(alpha * l_all[rows] + jnp.sum(p, axis=-1, keepdims=True))
                m_out.append(m_new)
                acc_out.append(acc_all[rows] * alpha + _dot(p.astype(BF16), vv, _NN))
            m_sc[...] = jnp.concatenate(m_out, axis=0)
            l_sc[...] = jnp.concatenate(l_out, axis=0)
            acc_sc[...] = jnp.concatenate(acc_out, axis=0)

        @pl.when(j < i)
        def _():
            step(False)

        @pl.when(j == i)
        def _():
            step(True)
            lf = l_sc[...]
            o_ref[...] = acc_sc[...] / lf
            l_ref[...] = jnp.broadcast_to(m_sc[...] + jnp.log(lf), (blk, LANES))

    qspec = pl.BlockSpec((blk, LANES), lambda ub, t, qi_ref, kj_ref: (qi_ref[t], ub))
    kspec = pl.BlockSpec((blk, LANES), lambda ub, t, qi_ref, kj_ref: (kj_ref[t], ub))
    return pl.pallas_call(
        body, name=name,
        grid_spec=pltpu.PrefetchScalarGridSpec(
            num_scalar_prefetch=2, grid=(nub, qi.shape[0]), in_specs=[qspec, kspec, kspec], out_specs=(qspec, qspec),
            scratch_shapes=[pltpu.VMEM((blk, 1), F32), pltpu.VMEM((blk, 1), F32), pltpu.VMEM((blk, LANES), F32)]),
        out_shape=(jax.ShapeDtypeStruct((s, w), F32), jax.ShapeDtypeStruct((s, w), F32)),
        compiler_params=_params("parallel", "arbitrary"),
    )(qi, kj, q, k, v)


def _causal_bwd(q, k, v, o, lse, do, scale, blk, name):
    s, w = q.shape
    nq, nub = s // blk, w // LANES
    qi, kj = _triangle(nq, by_key=True)

    def body(qi_ref, kj_ref, q_ref, k_ref, v_ref, o_ref, l_ref, do_ref, dq_ref, dk_ref, dv_ref, dk_acc, dv_acc):
        t = pl.program_id(1)
        i, kb = qi_ref[t], kj_ref[t]

        @pl.when(t == 0)
        def _():
            dq_ref[...] = jnp.zeros_like(dq_ref)

        @pl.when(i == kb)
        def _():
            dk_acc[...] = jnp.zeros_like(dk_acc)
            dv_acc[...] = jnp.zeros_like(dv_acc)

        def step(diagonal):
            qv = (q_ref[...] * scale).astype(BF16)
            kv, vv = k_ref[...].astype(BF16), v_ref[...].astype(BF16)
            dov = do_ref[...]
            tsum = jnp.sum(dov * o_ref[...], axis=-1, keepdims=True)
            dob = dov.astype(BF16)
            sc = _dot(qv, kv, _NT)
            if diagonal:
                r = lax.broadcasted_iota(jnp.int32, (blk, blk), 0)
                c = lax.broadcasted_iota(jnp.int32, (blk, blk), 1)
                sc = jnp.where(r >= c, sc, NEG_MASK)
            p = jnp.exp(sc - l_ref[:, 0:1])
            ds = (p * (_dot(dob, vv, _NT) - tsum)).astype(BF16)
            dv_acc[...] += _dot(p.astype(BF16), dob, _TN)
            dk_acc[...] += _dot(ds, qv, _TN)
            rows = pl.ds(pl.multiple_of(i * blk, blk), blk)
            dq_ref[rows, :] += _dot(ds, kv, _NN) * scale

        @pl.when(i == kb)
        def _():
            step(True)

        @pl.when(i > kb)
        def _():
            step(False)

        @pl.when(i == nq - 1)
        def _():
            dk_ref[...] = dk_acc[...]
            dv_ref[...] = dv_acc[...]

    qspec = pl.BlockSpec((blk, LANES), lambda ub, t, qi_ref, kj_ref: (qi_ref[t], ub))
    kspec = pl.BlockSpec((blk, LANES), lambda ub, t, qi_ref, kj_ref: (kj_ref[t], ub))
    whole = pl.BlockSpec((s, LANES), lambda ub, t, qi_ref, kj_ref: (0, ub))
    out = jax.ShapeDtypeStruct((s, w), F32)
    return pl.pallas_call(
        body, name=name,
        grid_spec=pltpu.PrefetchScalarGridSpec(
            num_scalar_prefetch=2, grid=(nub, qi.shape[0]), in_specs=[qspec, kspec, kspec, qspec, qspec, qspec],
            out_specs=(whole, kspec, kspec),
            scratch_shapes=[pltpu.VMEM((blk, LANES), F32), pltpu.VMEM((blk, LANES), F32)]),
        out_shape=(out, out, out), compiler_params=_params("parallel", "arbitrary"),
    )(qi, kj, q, k, v, o, lse, do)


def _make_causal_attention(scale, blk, name):
    @jax.custom_vjp
    def attn(q, k, v):
        return _causal_fwd(q, k, v, scale, blk, name + "_fwd")[0]

    def fwd(q, k, v):
        o, lse = _causal_fwd(q, k, v, scale, blk, name + "_fwd")
        return o, (q, k, v, o, lse)

    def bwd(res, do):
        q, k, v, o, lse = res
        return _causal_bwd(q, k, v, o, lse, do, scale, blk, name + "_bwd")

    attn.defvjp(fwd, bwd)
    return attn


def _make_attention(cfg, name, with_sink=False):
    if with_sink:
        @jax.custom_vjp
        def attn(q, k, v, sinks):
            return _attn_fwd(q, k, v, _sink_row(sinks), cfg, name + "_fwd")[0]

        def fwd(q, k, v, sinks):
            o, lse = _attn_fwd(q, k, v, _sink_row(sinks), cfg, name + "_fwd")
            return o, (q, k, v, o, lse, sinks)

        def bwd(res, do):
            q, k, v, o, lse, sinks = res
            dq, dsink = _attn_dq(q, k, v, o, lse, do, _sink_row(sinks), cfg, name + "_dq")
            dk, dv = _attn_dkv(q, k, v, o, lse, do, cfg, name + "_dkv")
            return dq, dk, dv, dsink[0].reshape(-1, HEAD_DIM)[:, 0]
    else:
        @jax.custom_vjp
        def attn(q, k, v):
            return _attn_fwd(q, k, v, None, cfg, name + "_fwd")[0]

        def fwd(q, k, v):
            o, lse = _attn_fwd(q, k, v, None, cfg, name + "_fwd")
            return o, (q, k, v, o, lse)

        def bwd(res, do):
            q, k, v, o, lse = res
            dq = _attn_dq(q, k, v, o, lse, do, None, cfg, name + "_dq")
            dk, dv = _attn_dkv(q, k, v, o, lse, do, cfg, name + "_dkv")
            return dq, dk, dv

    attn.defvjp(fwd, bwd)
    return attn


def _sink_row(sinks):
    return jnp.repeat(sinks.astype(F32), HEAD_DIM).reshape(1, -1)


def _sink_col(sinks):
    return jnp.repeat(sinks.astype(F32).reshape(-1, 2, 1), BLOCK, axis=1)


def _merge3(os_, ls_, name):
    s, w = os_[0].shape
    bs = _pick(s, (256, 128))

    def body(o1, o2, o3, l1, l2, l3, out_ref, lse_ref):
        a1, a2, a3 = l1[...], l2[...], l3[...]
        m = jnp.maximum(jnp.maximum(a1, a2), a3)
        e1, e2, e3 = jnp.exp(a1 - m), jnp.exp(a2 - m), jnp.exp(a3 - m)
        z = e1 + e2 + e3
        out_ref[...] = (e1 * o1[...] + e2 * o2[...] + e3 * o3[...]) / z
        lse_ref[...] = m + jnp.log(z)

    row = pl.BlockSpec((bs, w), lambda i: (i, 0))
    return pl.pallas_call(
        body, name=name, grid=(s // bs,), in_specs=[row] * 6, out_specs=(row, row),
        out_shape=(jax.ShapeDtypeStruct((s, w), F32), jax.ShapeDtypeStruct((s, w), F32)),
        compiler_params=_params("parallel"),
    )(*os_, *ls_)


def _add3(a, b, c, name):
    s, w = a.shape
    bs = _pick(s, (512, 256, 128))

    def body(a_ref, b_ref, c_ref, o_ref):
        o_ref[...] = (a_ref[...] + b_ref[...]) + c_ref[...]

    row = pl.BlockSpec((bs, w), lambda i: (i, 0))
    return pl.pallas_call(
        body, name=name, grid=(s // bs,), in_specs=[row] * 3, out_specs=row,
        out_shape=jax.ShapeDtypeStruct((s, w), F32), compiler_params=_params("parallel"),
    )(a, b, c)


def _make_dilated(name):
    scale, max_dist = HEAD_DIM ** -0.5, BLOCK

    def upb_of(dil):
        return BAND_UNITS_PER_STEP if dil == 1 else 1

    def forward(q, k, v):
        os_, ls_ = [], []
        for n, (_, dil) in enumerate(DIL_PATTERNS):
            o, l = _band_fwd(q, k, v, None, scale, max_dist, upb_of(dil), dil, "%s_b%d_fwd" % (name, n))
            os_.append(o)
            ls_.append(l)
        return _merge3(os_, ls_, name + "_merge")

    @jax.custom_vjp
    def dilated(q, k, v):
        return forward(q, k, v)[0]

    def fwd(q, k, v):
        out, lse = forward(q, k, v)
        return out, (q, k, v, out, lse)

    def bwd(res, do):
        q, k, v, out, lse = res
        dqs, dks, dvs = [], [], []
        for n, (_, dil) in enumerate(DIL_PATTERNS):
            args = (q, k, v, out, lse, do)
            dqs.append(_band_dq(*args, None, scale, max_dist, upb_of(dil), dil, "%s_b%d_dq" % (name, n)))
            dk, dv = _band_dkv(*args, scale, max_dist, upb_of(dil), dil, "%s_b%d_dkv" % (name, n))
            dks.append(dk)
            dvs.append(dv)
        return (_add3(*dqs, name + "_dq_sum"), _add3(*dks, name + "_dk_sum"), _add3(*dvs, name + "_dv_sum"))

    dilated.defvjp(fwd, bwd)
    return dilated


def _times_w(a, w, orient, name, res=None, rope=None):
    return _mm(a, w, "nn" if orient == "n" else "nt", name, res=res, rope=rope)


def _times_wt(dz, w, orient, name, res=None):
    return _mm(dz, w, "nt" if orient == "n" else "nn", name, res=res)


def _grad_w(a, dz, orient, name):
    if orient == "n":
        return _mm(a, dz, "tn", name, out_dtype=GRAD_WIRE_DTYPE)
    return _mm(dz, a, "tn", name, out_dtype=GRAD_WIRE_DTYPE)


def _make_norm_linear(name, orients, through=False, rope_halves=None):
    nw = len(orients)
    halves = rope_halves or (None,) * nw

    def rope_of(i, ropes):
        return None if halves[i] is None else (ropes[i], halves[i])

    def forward(x, g, ws, ropes):
        h = _rms_fwd(x, g, name + "_norm")
        zs = tuple(_times_w(h, w, o, "%s_mm%d" % (name, i), rope=rope_of(i, ropes))
                   for i, (w, o) in enumerate(zip(ws, orients)))
        return zs + ((x,) if through else ()), h

    @jax.custom_vjp
    def op(x, g, slots, ws, ropes):
        return forward(x, g, ws, ropes)[0]

    def fwd(x, g, slots, ws, ropes):
        outs, h = forward(x, g, ws, ropes)
        return outs, (x, g, h, ws, ropes)

    def bwd(res, cts):
        x, g, h, ws, ropes = res
        dzs = [cts[i] if halves[i] is None else
               _rope_apply(cts[i], ropes[i], halves[i], True, "%s_unrope%d" % (name, i)) for i in range(nw)]
        dh = None
        for i, (w, o) in enumerate(zip(ws, orients)):
            dh = _times_wt(dzs[i], w, o, "%s_dh%d" % (name, i), res=dh)
        dws = tuple(_grad_w(h, dzs[i], o, "%s_dw%d" % (name, i)) for i, o in enumerate(orients))
        dx, dg = _rms_bwd(x, g, dh, name + "_norm_bwd", dres=cts[nw] if through else None)
        return dx, dg, dws, (None,) * nw, tuple(None if r is None else (None,) * len(r) for r in ropes)

    op.defvjp(fwd, bwd)
    return op


def _make_linear_res(name, orient):
    @jax.custom_vjp
    def op(a, wslot, w, res):
        return _times_w(a, w, orient, name + "_mm", res=res)

    def fwd(a, wslot, w, res):
        return _times_w(a, w, orient, name + "_mm", res=res), (a, w)

    def bwd(saved, dout):
        a, w = saved
        return _times_wt(dout, w, orient, name + "_da"), _grad_w(a, dout, orient, name + "_dw"), None, dout

    op.defvjp(fwd, bwd)
    return op


FFN_TILE_M, FFN_TILE_N = 512, 1408


def _gate_up_act(h, wg, wu, name):
    m, k = h.shape
    n = wg.shape[0]
    bm, bn = _div128(m, FFN_TILE_M), _div128(n, FFN_TILE_N)

    def body(h_ref, wg_ref, wu_ref, g_ref, u_ref, a_ref):
        hv = h_ref[...]
        g = _dot(hv, wg_ref[...], _NT)
        u = _dot(hv, wu_ref[...], _NT)
        g_ref[...] = g
        u_ref[...] = u
        a_ref[...] = (g / (1.0 + jnp.exp(-g)) * u).astype(BF16)

    wspec = pl.BlockSpec((bn, k), lambda i, j: (j, 0))
    ospec = pl.BlockSpec((bm, bn), lambda i, j: (i, j))
    return pl.pallas_call(
        body, name=name, grid=(m // bm, n // bn), in_specs=[pl.BlockSpec((bm, k), lambda i, j: (i, 0)), wspec, wspec],
        out_specs=(ospec, ospec, ospec),
        out_shape=(jax.ShapeDtypeStruct((m, n), F32), jax.ShapeDtypeStruct((m, n), F32),
                   jax.ShapeDtypeStruct((m, n), BF16)),
        compiler_params=_params("parallel", "parallel"),
    )(h, wg, wu)


def _down_bwd_act(dout, wd, gmat, umat, name):
    m, k = dout.shape
    n = wd.shape[0]
    bm, bn = _div128(m, FFN_TILE_M), _div128(n, FFN_TILE_N)

    def body(do_ref, wd_ref, g_ref, u_ref, dg_ref, du_ref):
        d = _dot(do_ref[...].astype(BF16), wd_ref[...], _NT)
        g, u = g_ref[...], u_ref[...]
        sig = 1.0 / (1.0 + jnp.exp(-g))
        dg_ref[...] = (d * u * (sig * (1.0 + g * (1.0 - sig)))).astype(BF16)
        du_ref[...] = (d * (g * sig)).astype(BF16)

    ospec = pl.BlockSpec((bm, bn), lambda i, j: (i, j))
    return pl.pallas_call(
        body, name=name, grid=(m // bm, n // bn),
        in_specs=[pl.BlockSpec((bm, k), lambda i, j: (i, 0)), pl.BlockSpec((bn, k), lambda i, j: (j, 0)), ospec, ospec],
        out_specs=(ospec, ospec), out_shape=(jax.ShapeDtypeStruct((m, n), BF16),) * 2,
        compiler_params=_params("parallel", "parallel"),
    )(dout, wd, gmat, umat)


def _make_ffn(name):
    def forward(x, g, wg, wu, wd):
        h = _rms_fwd(x, g, name + "_norm")
        gmat, umat, a = _gate_up_act(h, wg, wu, name + "_gate_up")
        return _mm(a, wd, "nn", name + "_down", res=x), (x, g, h, gmat, umat, a, wg, wu, wd)

    @jax.custom_vjp
    def op(x, g, wg_slot, wu_slot, wd_slot, wg, wu, wd):
        return forward(x, g, wg, wu, wd)[0]

    def fwd(x, g, wg_slot, wu_slot, wd_slot, wg, wu, wd):
        return forward(x, g, wg, wu, wd)

    def bwd(saved, dout):
        x, g, h, gmat, umat, a, wg, wu, wd = saved
        dgm, dum = _down_bwd_act(dout, wd, gmat, umat, name + "_da_act")
        dwd = _mm(a, dout, "tn", name + "_dwd", out_dtype=GRAD_WIRE_DTYPE)
        dwg = _grad_w(h, dgm, "t", name + "_dwg")
        dwu = _grad_w(h, dum, "t", name + "_dwu")
        dh = _times_wt(dum, wu, "t", name + "_dh_u", res=_times_wt(dgm, wg, "t", name + "_dh_g"))
        dx, dg = _rms_bwd(x, g, dh, name + "_norm_bwd", dres=dout)
        return dx, dg, dwg, dwu, dwd, None, None, None

    op.defvjp(fwd, bwd)
    return op


def _make_final_loss(name):
    def run(x, g, tgt):
        s, d = x.shape
        bs = _pick(s, (512, 256, 128))

        def body(x_ref, g_ref, t_ref, loss_ref, dx_ref, dg_ref):
            i = pl.program_id(0)
            xv = x_ref[...]
            gv = g_ref[...]
            r = lax.rsqrt(jnp.mean(xv * xv, axis=-1, keepdims=True) + NORM_EPS)
            xh = xv * r
            e = xh * gv - t_ref[...]
            dy = e * (1.0 / d)
            dxh = dy * gv
            dx_ref[...] = r * (dxh - xh * jnp.mean(dxh * xh, axis=-1, keepdims=True))
            part = 0.5 * jnp.sum(jnp.sum(e * e, axis=-1, keepdims=True) * (1.0 / d), axis=0, keepdims=True)

            @pl.when(i == 0)
            def _():
                loss_ref[...] = jnp.zeros_like(loss_ref)
                dg_ref[...] = jnp.zeros_like(dg_ref)

            loss_ref[...] += jnp.broadcast_to(part, loss_ref.shape)
            dg_ref[...] += jnp.sum(dy * xh, axis=0, keepdims=True)

        row = pl.BlockSpec((bs, d), lambda i: (i, 0))
        vec = pl.BlockSpec((1, d), lambda i: (0, 0))
        loss, dx, dg = pl.pallas_call(
            body, name=name, grid=(s // bs,), in_specs=[row, vec, row],
            out_specs=(pl.BlockSpec((8, LANES), lambda i: (0, 0)), row, vec),
            out_shape=(jax.ShapeDtypeStruct((8, LANES), F32), jax.ShapeDtypeStruct((s, d), F32),
                       jax.ShapeDtypeStruct((1, d), F32)),
            compiler_params=_params("arbitrary"),
        )(x, g.reshape(1, d), tgt)
        return loss[0, 0], dx, dg.reshape(d)

    @jax.custom_vjp
    def op(x, g, tgt):
        return run(x, g, tgt)[0]

    def fwd(x, g, tgt):
        loss, dx, dg = run(x, g, tgt)
        return loss, (dx, dg)

    def bwd(saved, ct):
        dx, dg = saved
        return dx * ct, dg * ct, None

    op.defvjp(fwd, bwd)
    return op


def _model_loss(diff, consts):
    x = diff["x"]
    w = consts["w"]
    slot = diff["slots"]
    vec = diff["vec"]
    tab64, tab_mla = consts["tab64"], consts["tab_mla"]
    mem = consts["mem"]
    s = x.shape[0]

    rope64 = lambda t, nm: _make_rope(HEAD_DIM // 2, nm)(t, *tab64)

    def nl(nm, inp, gain, wnames, through=False, ropes=None):
        orients = tuple(_orient(n) for n in wnames)
        kinds = ropes or (None,) * len(wnames)
        halves = tuple({None: None, "64": HEAD_DIM // 2, "mla": MLA_ROPE_DIM // 2}[r] for r in kinds)
        tabs = tuple({None: None, "64": tab64, "mla": tab_mla}[r] for r in kinds)
        op = _make_norm_linear(nm, orients, through, halves)
        return op(inp, gain, tuple(slot[n] for n in wnames), tuple(w[n] for n in wnames), tabs)

    def lin_res(nm, a, wname, res):
        return _make_linear_res(nm, _orient(wname))(a, slot[wname], w[wname], res)

    def cross(layer, xin):
        p = "l%d_" % layer
        q, xin = nl(p + "xq", xin, vec[p + "x_norm"], (p + "w_xq",), through=True)
        kv, = nl(p + "xkv", mem, vec[p + "mem_norm"], (p + "w_xkv",))
        half = X_HEADS * X_HEAD_DIM
        cfg = AttnCfg("full", X_HEAD_DIM ** -0.5, 1, _pick(s, (512, 256, 128)), kv.shape[0], 4)
        o = _make_attention(cfg, p + "xattn")(q, kv[:, :half], kv[:, half:])
        return lin_res(p + "xo", o, p + "w_xo", xin)

    def ffn(layer, xin):
        p = "l%d_" % layer
        names = (p + "w_gate", p + "w_up", p + "w_down")
        return _make_ffn(p + "ffn")(xin, vec[p + "ffn_norm"], *(slot[n] for n in names), *(w[n] for n in names))

    in_parts = tuple(name for name, _, _ in _IN_PARTS) + ("l0_w_in_kr",)
    qa, kva, cq, ckv, kr_lanes, x = nl("l0_in", x, vec["l0_mix_norm"], in_parts, through=True,
                                       ropes=("64", None, None, None, None))
    ka = rope64(kva[:, :A_KV], "l0_rope_ka")
    va = kva[:, A_KV:]
    rep = SWA_HEADS // SWA_KV_HEADS
    expand = lambda t: jnp.broadcast_to(t.reshape(s, SWA_KV_HEADS, 1, HEAD_DIM),
                                        (s, SWA_KV_HEADS, rep, HEAD_DIM)).reshape(s, A_Q)
    swa = _make_band_attention(HEAD_DIM ** -0.5, SWA_WINDOW - 1, BAND_UNITS_PER_STEP, "l0_swa")
    oa = swa(qa, expand(ka), expand(va), vec["l0_sinks"])

    qfull, = nl("l0_uq", cq, vec["l0_q_norm"], ("l0_w_uq_heads",), ropes=("mla",))
    kvb, knope = nl("l0_ukv", ckv, vec["l0_kv_norm"], ("l0_w_ukv", "l0_w_uk_heads"))
    kfull = _make_rope_shared(MLA_ROPE_DIM // 2, "l0_rope_k")(knope, kr_lanes, *tab_mla)
    mla = _make_causal_attention((MLA_NOPE_DIM + MLA_ROPE_DIM) ** -0.5, _pick(s, (1024, 512, 256, 128)), "l0_mla")
    ob = mla(qfull, kfull, kvb)
    x = lin_res("l0_out_a", oa, "l0_w_out_swa", x)
    x = lin_res("l0_out_b", ob, "l0_w_out_mla", x)
    x = cross(0, x)
    x = ffn(0, x)

    q, k, v, x = nl("l1_qkv", x, vec["l1_mix_norm"], ("l1_w_q", "l1_w_k", "l1_w_v"), through=True,
                    ropes=("64", "64", None))
    o = _make_dilated("l1_dil")(q, k, v)
    x = lin_res("l1_out", o, "l1_w_out", x)
    x = cross(1, x)
    x = ffn(1, x)

    return _make_final_loss("final_loss")(x, vec["final_norm"], consts["target"])


MESH_IDS = pl.DeviceIdType.MESH
HBM_SPEC = pl.BlockSpec(memory_space=pltpu.HBM)


def _my_place():
    return lax.axis_index("x"), lax.axis_index("y"), lax.axis_index("c")


def _flip(v, bit):
    return 1 - v if bit else v


def _all_gather_rows(shard):
    r, c_ = shard.shape

    def body(x_ref, out_ref, send_sems, recv_sems, local_sem):
        x, y, c = _my_place()
        me, sibling = (x, y, c), (x, y, 1 - c)
        chips = [(1 - x, y), (x, 1 - y), (1 - x, 1 - y)]

        def slot(px, py, pc):
            return out_ref.at[4 * px + 2 * py + pc]

        def copy(k, block, to, src=None):
            return pltpu.make_async_remote_copy(
                src_ref=slot(*block) if src is None else src, dst_ref=slot(*block), send_sem=send_sems.at[k],
                recv_sem=recv_sems.at[k], device_id=to, device_id_type=MESH_IDS)

        mine = pltpu.make_async_copy(x_ref, slot(*me), local_sem)
        mine.start()
        first = [copy(0, me, sibling, src=x_ref)]
        first += [copy(1 + j, me, (*chip, c), src=x_ref) for j, chip in enumerate(chips)]
        for cp in first:
            cp.start()
        passed = [copy(4 + j, (*chip, c), sibling) for j, chip in enumerate(chips)]
        for j, chip in enumerate(chips):
            copy(1 + j, (*chip, c), me).wait_recv()
            passed[j].start()
        copy(0, sibling, me).wait_recv()
        for j, chip in enumerate(chips):
            copy(4 + j, (*chip, 1 - c), me).wait_recv()
        for cp in first + passed:
            cp.wait_send()
        mine.wait()

    return pl.pallas_call(
        body, name="weights_all_gather", out_shape=jax.ShapeDtypeStruct((N_DEV, r, c_), shard.dtype),
        in_specs=[HBM_SPEC], out_specs=HBM_SPEC,
        scratch_shapes=[pltpu.SemaphoreType.DMA((7,)), pltpu.SemaphoreType.DMA((7,)), pltpu.SemaphoreType.DMA],
    )(shard)


N_CHIPS = 4


def _exchange_with_sibling(slabs):
    _, nq, r, c_ = slabs.shape

    def body(p_ref, out_ref, send_sem, recv_sem):
        x, y, c = _my_place()
        cp = pltpu.make_async_remote_copy(
            src_ref=p_ref.at[1 - c], dst_ref=out_ref, send_sem=send_sem, recv_sem=recv_sem,
            device_id=(x, y, 1 - c), device_id_type=MESH_IDS)
        cp.start()
        cp.wait_recv()
        cp.wait_send()

    return pl.pallas_call(
        body, name="grad_exchange_sibling", out_shape=jax.ShapeDtypeStruct((nq, r, c_), slabs.dtype),
        in_specs=[HBM_SPEC], out_specs=HBM_SPEC,
        scratch_shapes=[pltpu.SemaphoreType.DMA, pltpu.SemaphoreType.DMA],
    )(slabs)


def _add_pairs(a, b):
    nq, r, c_ = a.shape
    br = _pick(r, (256, 128, 64, 32, 16, 8))

    def body(a_ref, b_ref, o_ref):
        o_ref[...] = (a_ref[...].astype(F32) + b_ref[...].astype(F32)).astype(o_ref.dtype)

    blk = pl.BlockSpec((1, br, c_), lambda q, i: (q, i, 0))
    return pl.pallas_call(
        body, name="grad_chip_sum", grid=(nq, r // br), in_specs=[blk, blk], out_specs=blk,
        out_shape=jax.ShapeDtypeStruct(a.shape, a.dtype), compiler_params=_params("parallel", "parallel"),
    )(a, b)


def _exchange_between_chips(slabs):
    nq, r, c_ = slabs.shape

    def body(t_ref, out_ref, send_sems, recv_sems, local_sem):
        x, y, c = _my_place()
        myq = 2 * x + y
        local = pltpu.make_async_copy(t_ref.at[myq], out_ref.at[myq], local_sem)
        local.start()
        sends, recvs = [], []
        for k in range(1, N_CHIPS):
            px, py = _flip(x, k & 2), _flip(y, k & 1)
            peer = 2 * px + py
            sends.append(pltpu.make_async_remote_copy(
                src_ref=t_ref.at[peer], dst_ref=out_ref.at[myq], send_sem=send_sems.at[k - 1],
                recv_sem=recv_sems.at[k - 1], device_id=(px, py, c), device_id_type=MESH_IDS))
            recvs.append(pltpu.make_async_remote_copy(
                src_ref=t_ref.at[myq], dst_ref=out_ref.at[peer], send_sem=send_sems.at[k - 1],
                recv_sem=recv_sems.at[k - 1], device_id=(px, py, c), device_id_type=MESH_IDS))
        for cp in sends:
            cp.start()
        for cp in recvs:
            cp.wait_recv()
        for cp in sends:
            cp.wait_send()
        local.wait()

    return pl.pallas_call(
        body, name="grad_exchange_chips", out_shape=jax.ShapeDtypeStruct(slabs.shape, slabs.dtype),
        in_specs=[HBM_SPEC], out_specs=HBM_SPEC,
        scratch_shapes=[pltpu.SemaphoreType.DMA((N_CHIPS - 1,)), pltpu.SemaphoreType.DMA((N_CHIPS - 1,)),
                        pltpu.SemaphoreType.DMA],
    )(slabs)


def _all_reduce_small(v):
    r, c_ = v.shape

    def body(v_ref, out_ref, buf, send_sems, recv_sems):
        x, y, c = _my_place()
        me = 4 * x + 2 * y + c
        buf[me] = v_ref[...]
        sends, recvs = [], []
        for k in range(1, N_DEV):
            px, py, pc = _flip(x, k & 4), _flip(y, k & 2), _flip(c, k & 1)
            peer = 4 * px + 2 * py + pc
            sends.append(pltpu.make_async_remote_copy(
                src_ref=v_ref, dst_ref=buf.at[me], send_sem=send_sems.at[k - 1], recv_sem=recv_sems.at[k - 1],
                device_id=(px, py, pc), device_id_type=MESH_IDS))
            recvs.append(pltpu.make_async_remote_copy(
                src_ref=v_ref, dst_ref=buf.at[peer], send_sem=send_sems.at[k - 1], recv_sem=recv_sems.at[k - 1],
                device_id=(px, py, pc), device_id_type=MESH_IDS))
        for cp in sends:
            cp.start()
        for cp in recvs:
            cp.wait_recv()
        for cp in sends:
            cp.wait_send()
        acc = buf[0]
        for d in range(1, N_DEV):
            acc = acc + buf[d]
        out_ref[...] = acc

    vm = pl.BlockSpec(memory_space=pltpu.VMEM)
    return pl.pallas_call(
        body, name="vector_grad_all_reduce", out_shape=jax.ShapeDtypeStruct((r, c_), F32), in_specs=[vm], out_specs=vm,
        scratch_shapes=[pltpu.VMEM((N_DEV, r, c_), F32), pltpu.SemaphoreType.DMA((7,)), pltpu.SemaphoreType.DMA((7,))],
    )(v)


def _adamw_math(w, g, m, v):
    m = ADAM_B1 * m + (1.0 - ADAM_B1) * g
    v = ADAM_B2 * v + (1.0 - ADAM_B2) * (g * g)
    m_hat = m / (1.0 - ADAM_B1 ** ADAM_STEP)
    v_hat = v / (1.0 - ADAM_B2 ** ADAM_STEP)
    delta = -ADAM_LR * (m_hat / (jnp.sqrt(v_hat) + ADAM_EPS) + ADAM_WD * w)
    return delta, m, v


def _sum_and_adamw(parts, w, m, v):
    nparts, r, c_ = parts.shape
    br = _pick(r, (256, 128, 64, 32, 16, 8))

    def body(p_ref, w_ref, m_ref, v_ref, g_ref, d_ref, nm_ref, nv_ref):
        g = p_ref[0].astype(F32)
        for d in range(1, nparts):
            g = g + p_ref[d].astype(F32)
        g_ref[...] = g
        d_ref[...], nm_ref[...], nv_ref[...] = _adamw_math(w_ref[...], g, m_ref[...], v_ref[...])

    row = pl.BlockSpec((br, c_), lambda i: (i, 0))
    return pl.pallas_call(
        body, name="grad_sum_adamw", grid=(r // br,),
        in_specs=[pl.BlockSpec((nparts, br, c_), lambda i: (0, i, 0)), row, row, row], out_specs=(row,) * 4,
        out_shape=(jax.ShapeDtypeStruct((r, c_), F32),) * 4, compiler_params=_params("parallel"),
    )(parts, w, m, v)


def _adamw_small(w, g, m, v):
    vm = pl.BlockSpec(memory_space=pltpu.VMEM)

    def body(w_ref, g_ref, m_ref, v_ref, d_ref, nm_ref, nv_ref):
        d_ref[...], nm_ref[...], nv_ref[...] = _adamw_math(w_ref[...], g_ref[...], m_ref[...], v_ref[...])

    return pl.pallas_call(
        body, name="vector_adamw", in_specs=[vm] * 4, out_specs=(vm,) * 3,
        out_shape=(jax.ShapeDtypeStruct(w.shape, F32),) * 3,
    )(w, g, m, v)


def _pad_rows(t, axis):
    extra = -t.shape[axis] % PART_ROW_ALIGN
    if extra == 0:
        return t
    widths = [(0, 0)] * t.ndim
    widths[axis] = (0, extra)
    return jnp.pad(t, widths)


def _pack_local(named):
    rows = [_pad_rows((named[n].T if kind == "c" else named[n]).reshape(-1, PACK_COLS), 0)
            for n, kind, _, _ in MATRICES]
    rows.append(jnp.zeros((MAT_ROWS - MAT_ROWS_USED, PACK_COLS), rows[0].dtype))
    return jnp.concatenate(rows, axis=0)


def _unpack_local(packed):
    out, r0 = {}, 0
    for n, kind, k, nn in MATRICES:
        nr = k * nn // N_DEV // PACK_COLS
        part = packed[r0:r0 + nr]
        out[n] = part.reshape(nn // N_DEV, k).T if kind == "c" else part.reshape(k // N_DEV, nn)
        r0 += _part_rows(k, nn)
    return out


def _unpack_gathered(g):
    out, r0 = {}, 0
    for n, kind, k, nn in MATRICES:
        nr = k * nn // N_DEV // PACK_COLS
        out[n] = g[:, r0:r0 + nr].reshape((nn, k) if kind == "c" else (k, nn))
        r0 += _part_rows(k, nn)
    return out


def _pack_full_grads(grads):
    rows = []
    for n, _, k, nn in MATRICES:
        gmat = grads[n].reshape(N_CHIPS, 2, -1, PACK_COLS).transpose(1, 0, 2, 3)
        rows.append(_pad_rows(gmat, 2))
    rows.append(jnp.zeros((2, N_CHIPS, MAT_ROWS - MAT_ROWS_USED, PACK_COLS), rows[0].dtype))
    return jnp.concatenate(rows, axis=2)


def _pack_vectors(named):
    rows = [jnp.pad(named[n].astype(F32), (0, PACK_COLS - d)) for n, d in VECTORS]
    rows += [jnp.zeros((PACK_COLS,), F32)] * (VEC_ROWS - len(VECTORS))
    return jnp.stack(rows, axis=0)


def _unpack_vectors(packed):
    return {n: packed[i, :d] for i, (n, d) in enumerate(VECTORS)}


def _step(inputs):
    x = inputs["x"][0]
    mem = inputs["mem"][0]
    positions = inputs["positions"][0]
    target = inputs["loss_target"][0]

    local_w = _pack_local({n: inputs[n] for n, _, _, _ in MATRICES})
    gathered = _all_gather_rows(local_w.astype(BF16))
    wfull = _unpack_gathered(gathered)
    vec = {n: inputs[n] for n, _ in VECTORS}

    loss_part, grad_x, gfull, gvec = _local_grads(wfull, vec, x, mem, positions, target)
    loss = lax.psum(loss_part, ("x", "y", "c"))

    slabs = _pack_full_grads(gfull)
    from_sibling = _exchange_with_sibling(slabs)
    mine = lax.dynamic_index_in_dim(slabs, lax.axis_index("c"), axis=0, keepdims=False)
    parts = _exchange_between_chips(_add_pairs(mine, from_sibling))
    local_m = _pack_local({n: inputs["m_" + n] for n, _, _, _ in MATRICES})
    local_v = _pack_local({n: inputs["v_" + n] for n, _, _, _ in MATRICES})
    g_pk, d_pk, m_pk, v_pk = _sum_and_adamw(parts, local_w, local_m, local_v)
    g_mat, d_mat, m_mat, v_mat = (_unpack_local(t) for t in (g_pk, d_pk, m_pk, v_pk))

    g_vec_pk = _all_reduce_small(_pack_vectors(gvec))
    d_vec_pk, m_vec_pk, v_vec_pk = _adamw_small(
        _pack_vectors(vec), g_vec_pk, _pack_vectors({n: inputs["m_" + n] for n, _ in VECTORS}),
        _pack_vectors({n: inputs["v_" + n] for n, _ in VECTORS}))
    g_vec, d_vec, m_vec, v_vec = (_unpack_vectors(t) for t in (g_vec_pk, d_vec_pk, m_vec_pk, v_vec_pk))

    def pick(mats, vecs, n):
        return mats[n] if n in mats else vecs[n]

    outs = [loss, grad_x[None]]
    for mats, vecs in ((g_mat, g_vec), (d_mat, d_vec), (m_mat, m_vec), (v_mat, v_vec)):
        outs += [pick(mats, vecs, n) for n in WEIGHT_ORDER]
    return tuple(outs)


_KIND = {n: kind for n, kind, _, _ in MATRICES}
_VIEW_OF = {"l0_w_uq_heads": "l0_w_uq", "l0_w_uk_heads": "l0_w_ukv", "l0_w_out_swa": "l0_w_out",
            "l0_w_out_mla": "l0_w_out", "l0_w_in_qa": "l0_w_in", "l0_w_in_kva": "l0_w_in", "l0_w_in_cq": "l0_w_in",
            "l0_w_in_ckv": "l0_w_in", "l0_w_in_kr": "l0_w_in", "l1_w_q": "l1_w_qkv", "l1_w_k": "l1_w_qkv",
            "l1_w_v": "l1_w_qkv"}
_IN_PARTS = (("l0_w_in_qa", 0, A_Q), ("l0_w_in_kva", A_Q, A_Q + 2 * A_KV),
             ("l0_w_in_cq", A_Q + 2 * A_KV, A_Q + 2 * A_KV + MLA_Q_RANK),
             ("l0_w_in_ckv", A_Q + 2 * A_KV + MLA_Q_RANK, EVEN_IN - MLA_ROPE_DIM))
_KR_PAD = (MLA_NOPE_DIM, LANES - MLA_NOPE_DIM - MLA_ROPE_DIM)
_MLA_QK = MLA_NOPE_DIM + MLA_ROPE_DIM


def _orient(name):
    return "t" if _KIND[_VIEW_OF.get(name, name)] == "c" else "n"


def _nope_rows():
    return (np.arange(MLA_HEADS * LANES) % LANES < MLA_NOPE_DIM)[:, None]


def _model_weights(wfull):
    w = dict(wfull)
    w_in = w.pop("l0_w_in")
    for name, r0, r1 in _IN_PARTS:
        w[name] = w_in[r0:r1]
    w["l0_w_in_kr"] = jnp.pad(w_in[EVEN_IN - MLA_ROPE_DIM:], (_KR_PAD, (0, 0)))
    w_qkv = w.pop("l1_w_qkv")
    for i, name in enumerate(("l1_w_q", "l1_w_k", "l1_w_v")):
        w[name] = w_qkv[i * D_MODEL:(i + 1) * D_MODEL]
    uq = w.pop("l0_w_uq").reshape(MLA_HEADS, _MLA_QK, MLA_Q_RANK)
    w["l0_w_uq_heads"] = jnp.pad(uq, ((0, 0), (0, LANES - _MLA_QK), (0, 0))).reshape(MLA_HEADS * LANES, MLA_Q_RANK)
    w["l0_w_uk_heads"] = jnp.where(_nope_rows(), wfull["l0_w_ukv"], jnp.zeros_like(wfull["l0_w_ukv"]))
    wo = w.pop("l0_w_out")
    w["l0_w_out_swa"] = wo[:A_Q]
    w["l0_w_out_mla"] = jnp.pad(wo[A_Q:].reshape(MLA_HEADS, HEAD_DIM, D_MODEL),
                                ((0, 0), (LANES - HEAD_DIM, 0), (0, 0))).reshape(MLA_HEADS * LANES, D_MODEL)
    return w


def _matrix_grads(g):
    out = {n: g[n] for n, _, _, _ in MATRICES if n in g}
    out["l0_w_in"] = jnp.concatenate([g[name] for name, _, _ in _IN_PARTS]
                                     + [g["l0_w_in_kr"][_KR_PAD[0]:_KR_PAD[0] + MLA_ROPE_DIM]], axis=0)
    out["l1_w_qkv"] = jnp.concatenate([g["l1_w_q"], g["l1_w_k"], g["l1_w_v"]], axis=0)
    out["l0_w_uq"] = g["l0_w_uq_heads"].reshape(MLA_HEADS, LANES, MLA_Q_RANK)[:, :_MLA_QK].reshape(-1, MLA_Q_RANK)
    uk = jnp.where(_nope_rows(), g["l0_w_uk_heads"], jnp.zeros_like(g["l0_w_uk_heads"]))
    out["l0_w_ukv"] = (g["l0_w_ukv"].astype(F32) + uk.astype(F32)).astype(g["l0_w_ukv"].dtype)
    out["l0_w_out"] = jnp.concatenate(
        [g["l0_w_out_swa"],
         g["l0_w_out_mla"].reshape(MLA_HEADS, LANES, D_MODEL)[:, LANES - HEAD_DIM:].reshape(-1, D_MODEL)], axis=0)
    return out


def _local_grads(wfull, vec, x, mem, positions, target):
    w = _model_weights(wfull)
    slots = {n: jnp.zeros(t.shape, GRAD_WIRE_DTYPE) for n, t in w.items()}
    tab64 = _rope_tables(positions, HEAD_DIM, 0, HEAD_DIM)
    tab_mla = _rope_tables(positions, MLA_ROPE_DIM, MLA_NOPE_DIM, LANES)
    diff = {"x": x, "slots": slots, "vec": vec}
    consts = {"w": w, "mem": mem, "tab64": tab64, "tab_mla": tab_mla, "target": target}
    loss_part, grads = jax.value_and_grad(_model_loss)(diff, consts)
    return loss_part, grads["x"], _matrix_grads(grads["slots"]), grads["vec"]


_INPUT_NAMES = (("x", "mem", "positions") + WEIGHT_ORDER + ("loss_target",)
                + tuple("m_" + n for n in WEIGHT_ORDER) + tuple("v_" + n for n in WEIGHT_ORDER))


def kernel(*args):
    assert len(args) == len(_INPUT_NAMES)
    return _step(dict(zip(_INPUT_NAMES, args)))
```

```python
import numpy as np
import jax
import jax.numpy as jnp
from jax import lax
from jax.experimental import pallas as pl
from jax.experimental.pallas import tpu as pltpu

F32 = jnp.float32
BF16 = jnp.bfloat16

LANES = 128
VMEM_LIMIT_BYTES = 56 * 1024 * 1024
MM_VMEM_BUDGET = 40 * 1024 * 1024
MM_MIN_FLOP_PER_STEP = 1e9
BAND_UNITS_PER_STEP = 4
CAUSAL_ROW_CHAIN = 128
BAND_CHAINS_PER_BATCH = 4

D_MODEL = 1024
HEAD_DIM = 64
ROPE_THETA = 10000.0
NORM_EPS = 1e-6
BLOCK = 128
SWA_HEADS = 8
SWA_KV_HEADS = 2
SWA_WINDOW = 128
MLA_HEADS = 8
MLA_Q_RANK = 384
MLA_KV_RANK = 256
MLA_NOPE_DIM = 64
MLA_ROPE_DIM = 32
A_Q = SWA_HEADS * HEAD_DIM
A_KV = SWA_KV_HEADS * HEAD_DIM
EVEN_IN = A_Q + 2 * A_KV + MLA_Q_RANK + MLA_KV_RANK + MLA_ROPE_DIM
DIL_PATTERNS = ((128, 1), (512, 4), (2048, 16))
X_HEADS = 4
X_HEAD_DIM = 128

ADAM_LR = 0.001
ADAM_B1 = 0.9
ADAM_B2 = 0.999
ADAM_EPS = 1e-08
ADAM_WD = 0.01
ADAM_STEP = 10

N_DEV = 8
GRAD_WIRE_DTYPE = BF16
NEG_MASK = -1e30
NEG_INIT = -1e20

MATRICES = (
    ("l0_w_in", "c", 1024, 1440), ("l0_w_uq", "c", 384, 768), ("l0_w_ukv", "c", 256, 1024),
    ("l0_w_out", "r", 1024, 1024), ("l0_w_xq", "r", 1024, 512), ("l0_w_xkv", "r", 1024, 1024),
    ("l0_w_xo", "c", 512, 1024), ("l0_w_gate", "c", 1024, 2816), ("l0_w_up", "c", 1024, 2816),
    ("l0_w_down", "r", 2816, 1024),
    ("l1_w_qkv", "c", 1024, 3072), ("l1_w_out", "r", 1024, 1024), ("l1_w_xq", "r", 1024, 512),
    ("l1_w_xkv", "r", 1024, 1024), ("l1_w_xo", "c", 512, 1024), ("l1_w_gate", "c", 1024, 2816),
    ("l1_w_up", "c", 1024, 2816), ("l1_w_down", "r", 2816, 1024),
)
VECTORS = (
    ("l0_mix_norm", 1024), ("l0_sinks", 8), ("l0_q_norm", 384), ("l0_kv_norm", 256), ("l0_x_norm", 1024),
    ("l0_mem_norm", 1024), ("l0_ffn_norm", 1024), ("l1_mix_norm", 1024), ("l1_x_norm", 1024),
    ("l1_mem_norm", 1024), ("l1_ffn_norm", 1024), ("final_norm", 1024),
)
WEIGHT_ORDER = (
    "l0_mix_norm", "l0_w_in", "l0_sinks", "l0_q_norm", "l0_w_uq", "l0_kv_norm", "l0_w_ukv", "l0_w_out", "l0_x_norm",
    "l0_mem_norm", "l0_w_xq", "l0_w_xkv", "l0_w_xo", "l0_ffn_norm", "l0_w_gate", "l0_w_up", "l0_w_down",
    "l1_mix_norm", "l1_w_qkv", "l1_w_out", "l1_x_norm", "l1_mem_norm", "l1_w_xq", "l1_w_xkv", "l1_w_xo",
    "l1_ffn_norm", "l1_w_gate", "l1_w_up", "l1_w_down", "final_norm",
)
PACK_COLS = 1024
PART_ROW_ALIGN = 16


def _part_rows(k, n):
    return -(-(k * n // N_DEV // PACK_COLS) // PART_ROW_ALIGN) * PART_ROW_ALIGN


MAT_ROWS_USED = sum(_part_rows(k, n) for _, _, k, n in MATRICES)
MAT_ROWS = -(-MAT_ROWS_USED // 256) * 256
VEC_ROWS = 16


def _pick(n, cands):
    for c in cands:
        if n % c == 0:
            return c
    return n


def _params(*sem):
    return pltpu.CompilerParams(dimension_semantics=sem, vmem_limit_bytes=VMEM_LIMIT_BYTES)


_DIMS = {"nn": (((1,), (0,)), ((), ())), "nt": (((1,), (1,)), ((), ())), "tn": (((0,), (0,)), ((), ()))}


def _rotate_block(xv, av, bmv, bpv, half, transpose):
    if transpose:
        return xv * av + pltpu.roll(xv * bmv, LANES - half, 1) + pltpu.roll(xv * bpv, half, 1)
    return xv * av + pltpu.roll(xv, half, 1) * bmv + pltpu.roll(xv, LANES - half, 1) * bpv


def _rotate_tile(t, tabs, half, transpose):
    av, bmv, bpv = tabs
    blocks = [_rotate_block(t[:, c:c + LANES], av, bmv, bpv, half, transpose) for c in range(0, t.shape[1], LANES)]
    return blocks[0] if len(blocks) == 1 else jnp.concatenate(blocks, axis=1)


def _div128(n, cap):
    d = (min(n, cap) // LANES) * LANES
    while d >= LANES:
        if n % d == 0:
            return d
        d -= LANES
    return n


def _mm_vmem_bytes(bm, bn, bk, nk, sa, sb, so, has_res):
    est = 2 * (bm * bk * sa + bk * bn * sb + bm * bn * so) + bm * bn * 4
    est += bm * bn * 4 if nk > 1 else 0
    est += 2 * bm * bn * 4 if has_res else 0
    est += bm * bk * 2 if sa == 4 else 0
    est += bk * bn * 2 if sb == 4 else 0
    return est


def _mm_tiles(m, n, k, sa, sb, so, has_res, mode):
    bn = _div128(n, 1536)
    kcap = 2048 if mode == "tn" else k
    for bm_cap in ((1408, 2816) if mode == "tn" else (512, 1024, 2048)):
        bm = _div128(m, bm_cap)
        bk = (min(k, kcap) // LANES) * LANES
        while bk > LANES and (k % bk or _mm_vmem_bytes(bm, bn, bk, k // bk, sa, sb, so, has_res) > MM_VMEM_BUDGET):
            bk -= LANES
        if 2 * bm * bn * bk >= MM_MIN_FLOP_PER_STEP or bm == m:
            break
    return bm, bn, bk


def _mm(a, b, mode, name, out_dtype=F32, res=None, rope=None):
    if mode == "nn":
        (m, k), (k2, n) = a.shape, b.shape
    elif mode == "nt":
        (m, k), (n, k2) = a.shape, b.shape
    else:
        (k, m), (k2, n) = a.shape, b.shape
    assert k == k2, (name, a.shape, b.shape)
    has_res = res is not None
    bm, bn, bk = _mm_tiles(m, n, k, a.dtype.itemsize, b.dtype.itemsize, jnp.dtype(out_dtype).itemsize, has_res, mode)
    nk = k // bk
    dims = _DIMS[mode]
    a_spec = pl.BlockSpec((bk, bm), lambda i, j, kk: (kk, i)) if mode == "tn" else pl.BlockSpec((bm, bk), lambda i, j, kk: (i, kk))
    b_spec = pl.BlockSpec((bn, bk), lambda i, j, kk: (j, kk)) if mode == "nt" else pl.BlockSpec((bk, bn), lambda i, j, kk: (kk, j))
    o_spec = pl.BlockSpec((bm, bn), lambda i, j, kk: (i, j))

    n_in = 2 + (1 if has_res else 0) + (3 if rope is not None else 0)

    def body(*refs):
        a_ref, b_ref = refs[0], refs[1]
        r_ref = refs[2] if has_res else None
        o_ref = refs[n_in]
        part = lax.dot_general(a_ref[...].astype(BF16), b_ref[...].astype(BF16), dims, preferred_element_type=F32)

        def finish(r):
            if has_res:
                r = r + r_ref[...]
            if rope is not None:
                r = _rotate_tile(r, tuple(t[...] for t in refs[n_in - 3:n_in]), rope[1], False)
            o_ref[...] = r.astype(out_dtype)

        if nk == 1:
            finish(part)
            return
        acc = refs[-1]
        kk = pl.program_id(2)

        @pl.when(kk == 0)
        def _():
            acc[...] = part

        @pl.when(jnp.logical_and(kk > 0, kk < nk - 1))
        def _():
            acc[...] += part

        @pl.when(kk == nk - 1)
        def _():
            finish(acc[...] + part)

    args = (a, b, res) if has_res else (a, b)
    in_specs = [a_spec, b_spec] + ([o_spec] if has_res else [])
    if rope is not None:
        args = args + tuple(rope[0])
        in_specs = in_specs + [pl.BlockSpec((bm, LANES), lambda i, j, kk: (i, 0))] * 3
    return pl.pallas_call(
        body, name=name, grid=(m // bm, n // bn, nk), in_specs=in_specs, out_specs=o_spec,
        out_shape=jax.ShapeDtypeStruct((m, n), out_dtype),
        scratch_shapes=[pltpu.VMEM((bm, bn), F32)] if nk > 1 else [],
        compiler_params=_params("parallel", "parallel", "arbitrary"),
    )(*args)


def _rms_fwd(x, g, name, out_dtype=BF16):
    s, d = x.shape
    bs = _pick(s, (512, 256, 128))

    def body(x_ref, g_ref, o_ref):
        xv = x_ref[...]
        r = lax.rsqrt(jnp.mean(xv * xv, axis=-1, keepdims=True) + NORM_EPS)
        o_ref[...] = ((xv * r) * g_ref[...]).astype(out_dtype)

    return pl.pallas_call(
        body, name=name, grid=(s // bs,),
        in_specs=[pl.BlockSpec((bs, d), lambda i: (i, 0)), pl.BlockSpec((1, d), lambda i: (0, 0))],
        out_specs=pl.BlockSpec((bs, d), lambda i: (i, 0)), out_shape=jax.ShapeDtypeStruct((s, d), out_dtype),
        compiler_params=_params("parallel"),
    )(x, g.reshape(1, d))


def _rms_bwd(x, g, dy, name, dres=None):
    s, d = x.shape
    bs = _pick(s, (512, 256, 128))
    has_res = dres is not None

    def body(*refs):
        if has_res:
            x_ref, g_ref, dy_ref, r_ref, dx_ref, dg_ref = refs
        else:
            x_ref, g_ref, dy_ref, dx_ref, dg_ref = refs
        i = pl.program_id(0)
        xv = x_ref[...]
        dy = dy_ref[...]
        r = lax.rsqrt(jnp.mean(xv * xv, axis=-1, keepdims=True) + NORM_EPS)
        xh = xv * r
        dxh = dy * g_ref[...]
        dx = r * (dxh - xh * jnp.mean(dxh * xh, axis=-1, keepdims=True))
        if has_res:
            dx = dx + r_ref[...]
        dx_ref[...] = dx

        @pl.when(i == 0)
        def _():
            dg_ref[...] = jnp.zeros_like(dg_ref)

        dg_ref[...] += jnp.sum(dy * xh, axis=0, keepdims=True)

    row = pl.BlockSpec((bs, d), lambda i: (i, 0))
    vec = pl.BlockSpec((1, d), lambda i: (0, 0))
    args = (x, g.reshape(1, d), dy) + ((dres,) if has_res else ())
    dx, dg = pl.pallas_call(
        body, name=name, grid=(s // bs,), in_specs=[row, vec, row] + ([row] if has_res else []),
        out_specs=(row, vec), out_shape=(jax.ShapeDtypeStruct((s, d), F32), jax.ShapeDtypeStruct((1, d), F32)),
        compiler_params=_params("arbitrary"),
    )(*args)
    return dx, dg.reshape(d)


def _rope_tables(positions, dh, offset, period):
    role = np.zeros(LANES, np.int32)
    for base in range(0, LANES, period):
        role[base + offset:base + offset + dh // 2] = 1
        role[base + offset + dh // 2:base + offset + dh] = 2
    inv_freq = ROPE_THETA ** (-jnp.arange(0, dh, 2, dtype=F32) / dh)
    one_period = jnp.concatenate([jnp.zeros((offset,), F32), inv_freq, inv_freq,
                                  jnp.zeros((period - offset - dh,), F32)])
    ang = positions.astype(F32)[:, None] * jnp.tile(one_period, LANES // period)[None, :]
    c, s = jnp.cos(ang), jnp.sin(ang)
    role = role[None, :]
    a = jnp.where(role == 0, 1.0, c).astype(F32)
    bm = jnp.where(role == 2, s, 0.0).astype(F32)
    bp = jnp.where(role == 1, -s, 0.0).astype(F32)
    return a, bm, bp


def _rope_apply(x, tabs, half, transpose, name, shared=None, sum_blocks=False):
    s, w = x.shape
    bs = _pick(s, (512, 256, 128))
    nc = w // LANES
    a, bm, bp = tabs
    has_shared = shared is not None

    def body(*refs):
        x_ref, a_ref, bm_ref, bp_ref = refs[:4]
        o_ref = refs[5] if has_shared else refs[4]
        av, bmv, bpv = a_ref[...], bm_ref[...], bp_ref[...]
        total = None
        for c in range(nc):
            sl = slice(c * LANES, (c + 1) * LANES)
            xv = x_ref[:, sl]
            if has_shared:
                xv = xv + refs[4][...]
            out = _rotate_block(xv, av, bmv, bpv, half, transpose)
            o_ref[:, sl] = out
            total = out if total is None else total + out
        if sum_blocks:
            refs[-1][...] = total

    row = pl.BlockSpec((bs, w), lambda i: (i, 0))
    tab = pl.BlockSpec((bs, LANES), lambda i: (i, 0))
    out_shape = jax.ShapeDtypeStruct((s, w), F32)
    return pl.pallas_call(
        body, name=name, grid=(s // bs,), in_specs=[row, tab, tab, tab] + ([tab] if has_shared else []),
        out_specs=(row, tab) if sum_blocks else row,
        out_shape=(out_shape, jax.ShapeDtypeStruct((s, LANES), F32)) if sum_blocks else out_shape,
        compiler_params=_params("parallel"),
    )(x, a, bm, bp, *((shared,) if has_shared else ()))


def _make_rope(half, name):
    @jax.custom_vjp
    def rope(x, a, bm, bp):
        return _rope_apply(x, (a, bm, bp), half, False, name + "_fwd")

    def fwd(x, a, bm, bp):
        return rope(x, a, bm, bp), (a, bm, bp)

    def bwd(tabs, dy):
        return _rope_apply(dy, tabs, half, True, name + "_bwd"), None, None, None

    rope.defvjp(fwd, bwd)
    return rope


def _make_rope_shared(half, name):
    @jax.custom_vjp
    def rope(x, shared, a, bm, bp):
        return _rope_apply(x, (a, bm, bp), half, False, name + "_fwd", shared=shared)

    def fwd(x, shared, a, bm, bp):
        return rope(x, shared, a, bm, bp), (a, bm, bp)

    def bwd(tabs, dy):
        dx, dshared = _rope_apply(dy, tabs, half, True, name + "_bwd", sum_blocks=True)
        return dx, dshared, None, None, None

    rope.defvjp(fwd, bwd)
    return rope


def _lane_masks():
    lane = lax.broadcasted_iota(jnp.int32, (1, LANES), 1)
    lo = lane < HEAD_DIM
    return [lo, jnp.logical_not(lo)]


def _sel(mask, v):
    return jnp.where(mask, v, jnp.zeros_like(v))


_NT = (((1,), (1,)), ((), ()))
_NN = (((1,), (0,)), ((), ()))
_TN = (((0,), (0,)), ((), ()))
_BNT = (((2,), (2,)), ((0,), (0,)))
_BNN = (((2,), (1,)), ((0,), (0,)))


def _dot(a, b, dims):
    return lax.dot_general(a, b, dims, preferred_element_type=F32)


def _band_masks(max_dist):
    assert BLOCK - 1 <= max_dist <= BLOCK
    r = lax.broadcasted_iota(jnp.int32, (BLOCK, BLOCK), 0)
    c = lax.broadcasted_iota(jnp.int32, (BLOCK, BLOCK), 1)
    return (BLOCK + r - c) <= max_dist, r >= c


def _stack_heads(t):
    return jnp.concatenate([t, t], axis=0)


def _head_terms(lms, a, prod, lv):
    t = jnp.sum(_sel(lms[a], prod), axis=-1, keepdims=True)
    lse = jnp.max(jnp.where(lms[a], lv, -jnp.inf), axis=-1, keepdims=True)
    return t, lse


class _Residue:
    def __init__(self, ref, r, dil):
        self.ref, self.rows = ref, pl.ds(r, BLOCK, stride=dil)

    def __getitem__(self, idx):
        return self.ref[self.rows, idx[1]]

    def __setitem__(self, idx, val):
        self.ref[self.rows, idx[1]] = val


def _residues(refs, dil):
    if dil == 1:
        return [tuple(refs)]
    return [tuple(_Residue(x, r, dil) for x in refs) for r in range(dil)]


def _band_fwd(q, k, v, sinkrow, scale, max_dist, upb, dil, name):
    sq, w = q.shape
    rb = BLOCK * dil
    nq, nub, wb = sq // rb, w // (LANES * upb), LANES * upb
    has_sink = sinkrow is not None

    def body(*refs):
        s_ref = refs[5] if has_sink else None
        lms = _lane_masks()
        mprev, mcur = _band_masks(max_dist)
        mprev = jnp.logical_and(mprev, pl.program_id(1) > 0)
        mask2 = _stack_heads(jnp.concatenate([mprev, mcur], axis=1))
        chains = [(rr, slice(u * LANES, (u + 1) * LANES))
                  for rr in _residues(refs[:5] + refs[-2:], dil) for u in range(upb)]
        for g0 in range(0, len(chains), BAND_CHAINS_PER_BATCH):
            group = chains[g0:g0 + BAND_CHAINS_PER_BATCH]
            qs, kcat, vcat, sks = [], [], [], []
            for (q_ref, kp_ref, kc_ref, vp_ref, vc_ref, _, _), sl in group:
                qv = (q_ref[:, sl] * scale).astype(BF16)
                qs.append(jnp.concatenate([_sel(lms[0], qv), _sel(lms[1], qv)], axis=0))
                kcat.append(jnp.concatenate([kp_ref[:, sl].astype(BF16), kc_ref[:, sl].astype(BF16)], axis=0))
                vcat.append(jnp.concatenate([vp_ref[:, sl].astype(BF16), vc_ref[:, sl].astype(BF16)], axis=0))
                if has_sink:
                    sks.append(s_ref[sl.start // LANES])
            qs, kcat, vcat = jnp.stack(qs), jnp.stack(kcat), jnp.stack(vcat)
            sc = jnp.where(mask2[None], _dot(qs, kcat, _BNT), NEG_MASK)
            m = jnp.max(sc, axis=-1, keepdims=True)
            p = jnp.exp(sc - m)
            l = jnp.sum(p, axis=-1, keepdims=True)
            pv = _dot(p.astype(BF16), vcat, _BNN)
            if has_sink:
                sk2 = jnp.stack(sks)
                m_all = jnp.maximum(m, sk2)
                shrink = jnp.exp(m - m_all)
                l = l * shrink + jnp.exp(sk2 - m_all)
                pv, m = pv * shrink, m_all
            o2 = pv / l
            lse2 = m + jnp.log(l)
            for gi, ((_, _, _, _, _, o_ref, l_ref), sl) in enumerate(group):
                o_ref[:, sl] = jnp.where(lms[0], o2[gi, :BLOCK], o2[gi, BLOCK:])
                l_ref[:, sl] = jnp.where(lms[0], lse2[gi, :BLOCK], lse2[gi, BLOCK:])

    cur = pl.BlockSpec((rb, wb), lambda ub, i: (i, ub))
    prev = pl.BlockSpec((rb, wb), lambda ub, i: (jnp.maximum(i - 1, 0), ub))
    in_specs = [cur, prev, cur, prev, cur]
    in_specs += [pl.BlockSpec((upb, 2 * BLOCK, 1), lambda ub, i: (ub, 0, 0))] if has_sink else []
    args = (q, k, k, v, v) + ((sinkrow,) if has_sink else ())
    return pl.pallas_call(
        body, name=name, grid=(nub, nq), in_specs=in_specs, out_specs=(cur, cur),
        out_shape=(jax.ShapeDtypeStruct((sq, w), F32), jax.ShapeDtypeStruct((sq, w), F32)),
        compiler_params=_params("parallel", "parallel"),
    )(*args)


def _band_dq(q, k, v, o, lse, do, sinkrow, scale, max_dist, upb, dil, name):
    sq, w = q.shape
    rb = BLOCK * dil
    nq, nub, wb = sq // rb, w // (LANES * upb), LANES * upb
    has_sink = sinkrow is not None

    def body(*refs):
        if has_sink:
            s_ref, dq_block, dsink_ref = refs[8], refs[9], refs[10]
        else:
            dq_block = refs[8]
        i = pl.program_id(1)
        lms = _lane_masks()
        mprev, mcur = _band_masks(max_dist)
        mprev = jnp.logical_and(mprev, i > 0)
        mask2 = _stack_heads(jnp.concatenate([mprev, mcur], axis=1))
        if has_sink:
            @pl.when(i == 0)
            def _():
                dsink_ref[...] = jnp.zeros_like(dsink_ref)

        chains = [(rr, slice(u * LANES, (u + 1) * LANES))
                  for rr in _residues(refs[:8] + (dq_block,), dil) for u in range(upb)]
        for g0 in range(0, len(chains), BAND_CHAINS_PER_BATCH):
            group = chains[g0:g0 + BAND_CHAINS_PER_BATCH]
            qs, dos, kcat, vcat, t2, lse2 = [], [], [], [], [], []
            for (q_ref, kp_ref, kc_ref, vp_ref, vc_ref, o_ref, l_ref, do_ref, _), sl in group:
                qv = (q_ref[:, sl] * scale).astype(BF16)
                dov = do_ref[:, sl]
                prod = dov * o_ref[:, sl]
                dob = dov.astype(BF16)
                lv = l_ref[:, sl]
                (t0, lse0), (t1, lse1) = _head_terms(lms, 0, prod, lv), _head_terms(lms, 1, prod, lv)
                t2.append(jnp.concatenate([t0, t1], axis=0))
                lse2.append(jnp.concatenate([lse0, lse1], axis=0))
                qs.append(jnp.concatenate([_sel(lms[0], qv), _sel(lms[1], qv)], axis=0))
                dos.append(jnp.concatenate([_sel(lms[0], dob), _sel(lms[1], dob)], axis=0))
                kcat.append(jnp.concatenate([kp_ref[:, sl].astype(BF16), kc_ref[:, sl].astype(BF16)], axis=0))
                vcat.append(jnp.concatenate([vp_ref[:, sl].astype(BF16), vc_ref[:, sl].astype(BF16)], axis=0))
                if has_sink:
                    rs = -jnp.exp(s_ref[:, sl] - lv) * jnp.where(lms[0], t0, t1)
                    dsink_ref[0:1, sl] += jnp.sum(rs, axis=0, keepdims=True)
            qs, dos, kcat, vcat = jnp.stack(qs), jnp.stack(dos), jnp.stack(kcat), jnp.stack(vcat)
            p = jnp.exp(jnp.where(mask2[None], _dot(qs, kcat, _BNT), NEG_MASK) - jnp.stack(lse2))
            ds = (p * (_dot(dos, vcat, _BNT) - jnp.stack(t2))).astype(BF16)
            dq2 = _dot(ds, kcat, _BNN) * scale
            for gi, ((_, _, _, _, _, _, _, _, dq_ref), sl) in enumerate(group):
                dq_ref[:, sl] = jnp.where(lms[0], dq2[gi, :BLOCK], dq2[gi, BLOCK:])

    cur = pl.BlockSpec((rb, wb), lambda ub, i: (i, ub))
    prev = pl.BlockSpec((rb, wb), lambda ub, i: (jnp.maximum(i - 1, 0), ub))
    in_specs = [cur, prev, cur, prev, cur, cur, cur, cur]
    args = (q, k, k, v, v, o, lse, do)
    out_specs, out_shape = cur, jax.ShapeDtypeStruct((sq, w), F32)
    sem = ("parallel", "parallel")
    if has_sink:
        in_specs = in_specs + [pl.BlockSpec((1, wb), lambda ub, i: (0, ub))]
        args = args + (sinkrow,)
        out_specs = (cur, pl.BlockSpec((8, wb), lambda ub, i: (0, ub)))
        out_shape = (out_shape, jax.ShapeDtypeStruct((8, w), F32))
        sem = ("parallel", "arbitrary")
    return pl.pallas_call(
        body, name=name, grid=(nub, nq), in_specs=in_specs, out_specs=out_specs, out_shape=out_shape,
        compiler_params=_params(*sem),
    )(*args)


def _band_dkv(q, k, v, o, lse, do, scale, max_dist, upb, dil, name):
    sq, w = q.shape
    rb = BLOCK * dil
    nq, nub, wb = sq // rb, w // (LANES * upb), LANES * upb

    def body(*refs):
        kb = pl.program_id(1)
        lms = _lane_masks()
        key = lax.broadcasted_iota(jnp.int32, (BLOCK, BLOCK), 0)
        qry = lax.broadcasted_iota(jnp.int32, (BLOCK, BLOCK), 1)
        msame = qry >= key
        mnext = jnp.logical_and((BLOCK + qry - key) <= max_dist, kb < nq - 1)
        mask4 = jnp.concatenate([msame, msame, mnext, mnext], axis=1)
        chains =[(rr, slice(u * LANES, (u + 1) * LANES)) for rr in _residues(refs, dil) for u in range(upb)]
        for g0 in range(0, len(chains), BAND_CHAINS_PER_BATCH):
            group = chains[g0:g0 + BAND_CHAINS_PER_BATCH]
            kvs, vvs, qss, doss, t4s, lse4s = [], [], [], [], [], []
            for (k_ref, v_ref, qs_ref, qn_ref, os_ref, on_ref, ls_ref, ln_ref, dos_ref, don_ref, _, _), sl in group:
                kvs.append(k_ref[:, sl].astype(BF16))
                vvs.append(v_ref[:, sl].astype(BF16))
                qparts, doparts, tparts, lparts = [], [], [], []
                for q_ref, o_ref, l_ref, do_ref in ((qs_ref, os_ref, ls_ref, dos_ref),
                                                    (qn_ref, on_ref, ln_ref, don_ref)):
                    qv = (q_ref[:, sl] * scale).astype(BF16)
                    dov = do_ref[:, sl]
                    prod_t = (dov * o_ref[:, sl]).T
                    dob = dov.astype(BF16)
                    lse_t = l_ref[:, sl].T
                    for a in range(2):
                        lanes = slice(a * HEAD_DIM, (a + 1) * HEAD_DIM)
                        qparts.append(_sel(lms[a], qv))
                        doparts.append(_sel(lms[a], dob))
                        tparts.append(jnp.sum(prod_t[lanes, :], axis=0, keepdims=True))
                        lparts.append(lse_t[a * HEAD_DIM:a * HEAD_DIM + 1, :])
                qss.append(jnp.concatenate(qparts, axis=0))
                doss.append(jnp.concatenate(doparts, axis=0))
                t4s.append(jnp.concatenate(tparts, axis=1))
                lse4s.append(jnp.concatenate(lparts, axis=1))
            kv, vv, qs, dos = jnp.stack(kvs), jnp.stack(vvs), jnp.stack(qss), jnp.stack(doss)
            p = jnp.exp(jnp.where(mask4[None], _dot(kv, qs, _BNT), NEG_MASK) - jnp.stack(lse4s))
            ds = (p * (_dot(vv, dos, _BNT) - jnp.stack(t4s))).astype(BF16)
            dv = _dot(p.astype(BF16), dos, _BNN)
            dk = _dot(ds, qs, _BNN)
            for gi, (rr, sl) in enumerate(group):
                rr[-1][:, sl] = dv[gi]
                rr[-2][:, sl] = dk[gi]

    same = pl.BlockSpec((rb, wb), lambda ub, kb: (kb, ub))
    nxt = pl.BlockSpec((rb, wb), lambda ub, kb: (jnp.minimum(kb + 1, nq - 1), ub))
    return pl.pallas_call(
        body, name=name, grid=(nub, nq), in_specs=[same, same, same, nxt, same, nxt, same, nxt, same, nxt],
        out_specs=(same, same),
        out_shape=(jax.ShapeDtypeStruct((sq, w), F32), jax.ShapeDtypeStruct((sq, w), F32)),
        compiler_params=_params("parallel", "parallel"),
    )(k, v, q, q, o, o, lse, lse, do, do)


def _make_band_attention(scale, max_dist, upb, name):
    @jax.custom_vjp
    def attn(q, k, v, sinks):
        return _band_fwd(q, k, v, _sink_col(sinks), scale, max_dist, upb, 1, name + "_fwd")[0]

    def fwd(q, k, v, sinks):
        o, lse = _band_fwd(q, k, v, _sink_col(sinks), scale, max_dist, upb, 1, name + "_fwd")
        return o, (q, k, v, o, lse, sinks)

    def bwd(res, do):
        q, k, v, o, lse, sinks = res
        dq, dsink = _band_dq(q, k, v, o, lse, do, _sink_row(sinks), scale, max_dist, upb, 1, name + "_dq")
        dk, dv = _band_dkv(q, k, v, o, lse, do, scale, max_dist, upb, 1, name + "_dkv")
        return dq, dk, dv, dsink[0].reshape(-1, HEAD_DIM)[:, 0]

    attn.defvjp(fwd, bwd)
    return attn


def _triangle(n, by_key):
    if by_key:
        pairs = [(i, kb) for kb in range(n) for i in range(kb, n)]
    else:
        pairs = [(i, j) for i in range(n) for j in range(i + 1)]
    qi = np.asarray([p[0] for p in pairs], np.int32)
    kj = np.asarray([p[1] for p in pairs], np.int32)
    return jnp.asarray(qi), jnp.asarray(kj)


def _causal_fwd(q, k, v, scale, blk, name):
    s, w = q.shape
    nq, nub = s // blk, w // LANES
    qi, kj = _triangle(nq, by_key=False)

    def body(qi_ref, kj_ref, q_ref, k_ref, v_ref, o_ref, l_ref, m_sc, l_sc, acc_sc):
        t = pl.program_id(1)
        i, j = qi_ref[t], kj_ref[t]

        @pl.when(j == 0)
        def _():
            m_sc[...] = jnp.full_like(m_sc, NEG_INIT)
            l_sc[...] = jnp.zeros_like(l_sc)
            acc_sc[...] = jnp.zeros_like(acc_sc)

        def step(diagonal):
            kv, vv = k_ref[...].astype(BF16), v_ref[...].astype(BF16)
            chains = range(0, blk, CAUSAL_ROW_CHAIN)
            scs = [_dot((q_ref[c0:c0 + CAUSAL_ROW_CHAIN, :] * scale).astype(BF16), kv, _NT) for c0 in chains]
            m_all, l_all, acc_all = m_sc[...], l_sc[...], acc_sc[...]
            m_out, l_out, acc_out = [], [], []
            for sc, c0 in zip(scs, chains):
                rows = slice(c0, c0 + CAUSAL_ROW_CHAIN)
                if diagonal:
                    r = c0 + lax.broadcasted_iota(jnp.int32, (CAUSAL_ROW_CHAIN, blk), 0)
                    c = lax.broadcasted_iota(jnp.int32, (CAUSAL_ROW_CHAIN, blk), 1)
                    sc = jnp.where(r >= c, sc, NEG_MASK)
                m_prev = m_all[rows]
                m_new = jnp.maximum(m_prev, jnp.max(sc, axis=-1, keepdims=True))
                alpha = jnp.exp(m_prev - m_new)
                p = jnp.exp(sc - m_new)
                l_out.append(alpha * l_all[rows] + jnp.sum(p, axis=-1, keepdims=True))
                m_out.append(m_new)
                acc_out.append(acc_all[rows] * alpha + _dot(p.astype(BF16), vv, _NN))
            m_sc[...] = jnp.concatenate(m_out, axis=0)
            l_sc[...] = jnp.concatenate(l_out, axis=0)
            acc_sc[...] = jnp.concatenate(acc_out, axis=0)

        @pl.when(j < i)
        def _():
            step(False)

        @pl.when(j == i)
        def _():
            step(True)
            lf = l_sc[...]
            o_ref[...] = acc_sc[...] / lf
            l_ref[...] = jnp.broadcast_to(m_sc[...] + jnp.log(lf), (blk, LANES))

    qspec = pl.BlockSpec((blk, LANES), lambda ub, t, qi_ref, kj_ref: (qi_ref[t], ub))
    kspec = pl.BlockSpec((blk, LANES), lambda ub, t, qi_ref, kj_ref: (kj_ref[t], ub))
    return pl.pallas_call(
        body, name=name,
        grid_spec=pltpu.PrefetchScalarGridSpec(
            num_scalar_prefetch=2, grid=(nub, qi.shape[0]), in_specs=[qspec, kspec, kspec], out_specs=(qspec, qspec),
            scratch_shapes=[pltpu.VMEM((blk, 1), F32), pltpu.VMEM((blk, 1), F32), pltpu.VMEM((blk, LANES), F32)]),
        out_shape=(jax.ShapeDtypeStruct((s, w), F32), jax.ShapeDtypeStruct((s, w), F32)),
        compiler_params=_params("parallel", "arbitrary"),
    )(qi, kj, q, k, v)


def _causal_bwd(q, k, v, o, lse, do, scale, blk, name):
    s, w = q.shape
    nq, nub = s // blk, w // LANES
    qi, kj = _triangle(nq, by_key=True)

    def body(qi_ref, kj_ref, q_ref, k_ref, v_ref, o_ref, l_ref, do_ref, dq_ref, dk_ref, dv_ref, dk_acc, dv_acc):
        t = pl.program_id(1)
        i, kb = qi_ref[t], kj_ref[t]

        @pl.when(t == 0)
        def _():
            dq_ref[...] = jnp.zeros_like(dq_ref)

        @pl.when(i == kb)
        def _():
            dk_acc[...] = jnp.zeros_like(dk_acc)
            dv_acc[...] = jnp.zeros_like(dv_acc)

        def step(diagonal):
            qv = (q_ref[...] * scale).astype(BF16)
            kv, vv = k_ref[...].astype(BF16), v_ref[...].astype(BF16)
            dov = do_ref[...]
            tsum = jnp.sum(dov * o_ref[...], axis=-1, keepdims=True)
            dob = dov.astype(BF16)
            sc = _dot(qv, kv, _NT)
            if diagonal:
                r = lax.broadcasted_iota(jnp.int32, (blk, blk), 0)
                c = lax.broadcasted_iota(jnp.int32, (blk, blk), 1)
                sc = jnp.where(r >= c, sc, NEG_MASK)
            p = jnp.exp(sc - l_ref[:, 0:1])
            ds = (p * (_dot(dob, vv, _NT) - tsum)).astype(BF16)
            dv_acc[...] += _dot(p.astype(BF16), dob, _TN)
            dk_acc[...] += _dot(ds, qv, _TN)
            rows = pl.ds(pl.multiple_of(i * blk, blk), blk)
            dq_ref[rows, :] += _dot(ds, kv, _NN) * scale

        @pl.when(i == kb)
        def _():
            step(True)

        @pl.when(i > kb)
        def _():
            step(False)

        @pl.when(i == nq - 1)
        def _():
            dk_ref[...] = dk_acc[...]
            dv_ref[...] = dv_acc[...]

    qspec = pl.BlockSpec((blk, LANES), lambda ub, t, qi_ref, kj_ref: (qi_ref[t], ub))
    kspec = pl.BlockSpec((blk, LANES), lambda ub, t, qi_ref, kj_ref: (kj_ref[t], ub))
    whole = pl.BlockSpec((s, LANES), lambda ub, t, qi_ref, kj_ref: (0, ub))
    out = jax.ShapeDtypeStruct((s, w), F32)
    return pl.pallas_call(
        body, name=name,
        grid_spec=pltpu.PrefetchScalarGridSpec(
            num_scalar_prefetch=2, grid=(nub, qi.shape[0]), in_specs=[qspec, kspec, kspec, qspec, qspec, qspec],
            out_specs=(whole, kspec, kspec),
            scratch_shapes=[pltpu.VMEM((blk, LANES), F32), pltpu.VMEM((blk, LANES), F32)]),
        out_shape=(out, out, out), compiler_params=_params("parallel", "arbitrary"),
    )(qi, kj, q, k, v, o, lse, do)


def _make_causal_attention(scale, blk, name):
    @jax.custom_vjp
    def attn(q, k, v):
        return _causal_fwd(q, k, v, scale, blk, name + "_fwd")[0]

    def fwd(q, k, v):
        o, lse = _causal_fwd(q, k, v, scale, blk, name + "_fwd")
        return o, (q, k, v, o, lse)

    def bwd(res, do):
        q, k, v, o, lse = res
        return _causal_bwd(q, k, v, o, lse, do, scale, blk, name + "_bwd")

    attn.defvjp(fwd, bwd)
    return attn


_BTN = (((1,), (1,)), ((0,), (0,)))


def _heads(ref, scale=None):
    blocks = []
    for c in range(0, ref.shape[1], LANES):
        t = ref[:, c:c + LANES]
        blocks.append((t if scale is None else t * scale).astype(BF16))
    return jnp.stack(blocks)


def _memory_fwd(q, k, v, scale, name):
    s, w = q.shape
    m = k.shape[0]
    bq = _pick(s, (512, 256, 128))

    def body(q_ref, k_ref, v_ref, o_ref, l_ref):
        sc = _dot(_heads(q_ref, scale), _heads(k_ref), _BNT)
        mx = jnp.max(sc, axis=-1, keepdims=True)
        p = jnp.exp(sc - mx)
        l = jnp.sum(p, axis=-1, keepdims=True)
        o = _dot(p.astype(BF16), _heads(v_ref), _BNN) / l
        lse = mx + jnp.log(l)
        for h in range(w // LANES):
            o_ref[:, h * LANES:(h + 1) * LANES] = o[h]
            l_ref[:, h * LANES:(h + 1) * LANES] = jnp.broadcast_to(lse[h], (bq, LANES))

    row = pl.BlockSpec((bq, w), lambda i: (i, 0))
    mem = pl.BlockSpec((m, w), lambda i: (0, 0))
    return pl.pallas_call(
        body, name=name, grid=(s // bq,), in_specs=[row, mem, mem], out_specs=(row, row),
        out_shape=(jax.ShapeDtypeStruct((s, w), F32), jax.ShapeDtypeStruct((s, w), F32)),
        compiler_params=_params("parallel"),
    )(q, k, v)


def _memory_bwd(q, k, v, o, lse, do, scale, name):
    s, w = q.shape
    m = k.shape[0]
    nh = w // LANES
    bq = _pick(s, (512, 256, 128))

    def body(q_ref, k_ref, v_ref, o_ref, l_ref, do_ref, dq_ref, dk_ref, dv_ref):
        qs, ks, vs = _heads(q_ref, scale), _heads(k_ref), _heads(v_ref)
        dos = _heads(do_ref)
        t = jnp.stack([jnp.sum(do_ref[:, h * LANES:(h + 1) * LANES] * o_ref[:, h * LANES:(h + 1) * LANES],
                               axis=-1, keepdims=True) for h in range(nh)])
        lse = jnp.stack([l_ref[:, h * LANES:h * LANES + 1] for h in range(nh)])
        p = jnp.exp(_dot(qs, ks, _BNT) - lse)
        ds = (p * (_dot(dos, vs, _BNT) - t)).astype(BF16)
        dq = _dot(ds, ks, _BNN) * scale
        dk = _dot(ds, qs, _BTN)
        dv = _dot(p.astype(BF16), dos, _BTN)

        @pl.when(pl.program_id(0) == 0)
        def _():
            dk_ref[...] = jnp.zeros_like(dk_ref)
            dv_ref[...] = jnp.zeros_like(dv_ref)

        for h in range(nh):
            sl = slice(h * LANES, (h + 1) * LANES)
            dq_ref[:, sl] = dq[h]
            dk_ref[:, sl] += dk[h]
            dv_ref[:, sl] += dv[h]

    row = pl.BlockSpec((bq, w), lambda i: (i, 0))
    mem = pl.BlockSpec((m, w), lambda i: (0, 0))
    return pl.pallas_call(
        body, name=name, grid=(s // bq,), in_specs=[row, mem, mem, row, row, row], out_specs=(row, mem, mem),
        out_shape=(jax.ShapeDtypeStruct((s, w), F32), jax.ShapeDtypeStruct((m, w), F32),
                   jax.ShapeDtypeStruct((m, w), F32)),
        compiler_params=_params("arbitrary"),
    )(q, k, v, o, lse, do)


def _make_memory_attention(scale, name):
    @jax.custom_vjp
    def attn(q, k, v):
        return _memory_fwd(q, k, v, scale, name + "_fwd")[0]

    def fwd(q, k, v):
        o, lse = _memory_fwd(q, k, v, scale, name + "_fwd")
        return o, (q, k, v, o, lse)

    def bwd(res, do):
        q, k, v, o, lse = res
        return _memory_bwd(q, k, v, o, lse, do, scale, name + "_bwd")

    attn.defvjp(fwd, bwd)
    return attn


def _sink_row(sinks):
    return jnp.repeat(sinks.astype(F32), HEAD_DIM).reshape(1, -1)


def _sink_col(sinks):
    return jnp.repeat(sinks.astype(F32).reshape(-1, 2, 1), BLOCK, axis=1)


def _merge3(os_, ls_, name):
    s, w = os_[0].shape
    bs = _pick(s, (256, 128))

    def body(o1, o2, o3, l1, l2, l3, out_ref, lse_ref):
        a1, a2, a3 = l1[...], l2[...], l3[...]
        m = jnp.maximum(jnp.maximum(a1, a2), a3)
        e1, e2, e3 = jnp.exp(a1 - m), jnp.exp(a2 - m), jnp.exp(a3 - m)
        z = e1 + e2 + e3
        out_ref[...] = (e1 * o1[...] + e2 * o2[...] + e3 * o3[...]) / z
        lse_ref[...] = m + jnp.log(z)

    row = pl.BlockSpec((bs, w), lambda i: (i, 0))
    return pl.pallas_call(
        body, name=name, grid=(s // bs,), in_specs=[row] * 6, out_specs=(row, row),
        out_shape=(jax.ShapeDtypeStruct((s, w), F32), jax.ShapeDtypeStruct((s, w), F32)),
        compiler_params=_params("parallel"),
    )(*os_, *ls_)


def _add3(a, b, c, name):
    s, w = a.shape
    bs = _pick(s, (512, 256, 128))

    def body(a_ref, b_ref, c_ref, o_ref):
        o_ref[...] = (a_ref[...] + b_ref[...]) + c_ref[...]

    row = pl.BlockSpec((bs, w), lambda i: (i, 0))
    return pl.pallas_call(
        body, name=name, grid=(s // bs,), in_specs=[row] * 3, out_specs=row,
        out_shape=jax.ShapeDtypeStruct((s, w), F32), compiler_params=_params("parallel"),
    )(a, b, c)


def _make_dilated(name):
    scale, max_dist = HEAD_DIM ** -0.5, BLOCK

    def upb_of(dil):
        return 2 * BAND_UNITS_PER_STEP if dil == 1 else 1

    def forward(q, k, v):
        os_, ls_ = [], []
        for n, (_, dil) in enumerate(DIL_PATTERNS):
            o, l = _band_fwd(q, k, v, None, scale, max_dist, upb_of(dil), dil, "%s_b%d_fwd" % (name, n))
            os_.append(o)
            ls_.append(l)
        return _merge3(os_, ls_, name + "_merge")

    @jax.custom_vjp
    def dilated(q, k, v):
        return forward(q, k, v)[0]

    def fwd(q, k, v):
        out, lse = forward(q, k, v)
        return out, (q, k, v, out, lse)

    def bwd(res, do):
        q, k, v, out, lse = res
        dqs, dks, dvs = [], [], []
        for n, (_, dil) in enumerate(DIL_PATTERNS):
            args = (q, k, v, out, lse, do)
            dqs.append(_band_dq(*args, None, scale, max_dist, upb_of(dil), dil, "%s_b%d_dq" % (name, n)))
            dk, dv = _band_dkv(*args, scale, max_dist, upb_of(dil), dil, "%s_b%d_dkv" % (name, n))
            dks.append(dk)
            dvs.append(dv)
        return (_add3(*dqs, name + "_dq_sum"), _add3(*dks, name + "_dk_sum"), _add3(*dvs, name + "_dv_sum"))

    dilated.defvjp(fwd, bwd)
    return dilated


def _times_w(a, w, orient, name, res=None, rope=None):
    return _mm(a, w, "nn" if orient == "n" else "nt", name, res=res, rope=rope)


def _times_wt(dz, w, orient, name, res=None):
    return _mm(dz, w, "nt" if orient == "n" else "nn", name, res=res)


def _grad_w(a, dz, orient, name):
    if orient == "n":
        return _mm(a, dz, "tn", name, out_dtype=GRAD_WIRE_DTYPE)
    return _mm(dz, a, "tn", name, out_dtype=GRAD_WIRE_DTYPE)


def _make_norm_linear(name, orients, through=False, rope_halves=None):
    nw = len(orients)
    halves = rope_halves or (None,) * nw

    def rope_of(i, ropes):
        return None if halves[i] is None else (ropes[i], halves[i])

    def forward(x, g, ws, ropes):
        h = _rms_fwd(x, g, name + "_norm")
        zs = tuple(_times_w(h, w, o, "%s_mm%d" % (name, i), rope=rope_of(i, ropes))
                   for i, (w, o) in enumerate(zip(ws, orients)))
        return zs + ((x,) if through else ()), h

    @jax.custom_vjp
    def op(x, g, slots, ws, ropes):
        return forward(x, g, ws, ropes)[0]

    def fwd(x, g, slots, ws, ropes):
        outs, h = forward(x, g, ws, ropes)
        return outs, (x, g, h, ws, ropes)

    def bwd(res, cts):
        x, g, h, ws, ropes = res
        dzs = [cts[i] if halves[i] is None else
               _rope_apply(cts[i], ropes[i], halves[i], True, "%s_unrope%d" % (name, i)) for i in range(nw)]
        dh = None
        for i, (w, o) in enumerate(zip(ws, orients)):
            dh = _times_wt(dzs[i], w, o, "%s_dh%d" % (name, i), res=dh)
        dws = tuple(_grad_w(h, dzs[i], o, "%s_dw%d" % (name, i)) for i, o in enumerate(orients))
        dx, dg = _rms_bwd(x, g, dh, name + "_norm_bwd", dres=cts[nw] if through else None)
        return dx, dg, dws, (None,) * nw, tuple(None if r is None else (None,) * len(r) for r in ropes)

    op.defvjp(fwd, bwd)
    return op


def _make_linear_res(name, orient):
    @jax.custom_vjp
    def op(a, wslot, w, res):
        return _times_w(a, w, orient, name + "_mm", res=res)

    def fwd(a, wslot, w, res):
        return _times_w(a, w, orient, name + "_mm", res=res), (a, w)

    def bwd(saved, dout):
        a, w = saved
        return _times_wt(dout, w, orient, name + "_da"), _grad_w(a, dout, orient, name + "_dw"), None, dout

    op.defvjp(fwd, bwd)
    return op


FFN_TILE_M, FFN_TILE_N = 512, 1408


def _gate_up_act(h, wg, wu, name):
    m, k = h.shape
    n = wg.shape[0]
    bm, bn = _div128(m, FFN_TILE_M), _div128(n, FFN_TILE_N)

    def body(h_ref, wg_ref, wu_ref, g_ref, u_ref, a_ref):
        hv = h_ref[...]
        g = _dot(hv, wg_ref[...], _NT)
        u = _dot(hv, wu_ref[...], _NT)
        g_ref[...] = g
        u_ref[...] = u
        a_ref[...] = (g / (1.0 + jnp.exp(-g)) * u).astype(BF16)

    wspec = pl.BlockSpec((bn, k), lambda i, j: (j, 0))
    ospec = pl.BlockSpec((bm, bn), lambda i, j: (i, j))
    return pl.pallas_call(
        body, name=name, grid=(m // bm, n // bn), in_specs=[pl.BlockSpec((bm, k), lambda i, j: (i, 0)), wspec, wspec],
        out_specs=(ospec, ospec, ospec),
        out_shape=(jax.ShapeDtypeStruct((m, n), F32), jax.ShapeDtypeStruct((m, n), F32),
                   jax.ShapeDtypeStruct((m, n), BF16)),
        compiler_params=_params("parallel", "parallel"),
    )(h, wg, wu)


def _down_bwd_act(dout, wd, gmat, umat, name):
    m, k = dout.shape
    n = wd.shape[0]
    bm, bn = _div128(m, FFN_TILE_M), _div128(n, FFN_TILE_N)

    def body(do_ref, wd_ref, g_ref, u_ref, dg_ref, du_ref):
        d = _dot(do_ref[...].astype(BF16), wd_ref[...], _NT)
        g, u = g_ref[...], u_ref[...]
        sig = 1.0 / (1.0 + jnp.exp(-g))
        dg_ref[...] = (d * u * (sig * (1.0 + g * (1.0 - sig)))).astype(BF16)
        du_ref[...] = (d * (g * sig)).astype(BF16)

    ospec = pl.BlockSpec((bm, bn), lambda i, j: (i, j))
    return pl.pallas_call(
        body, name=name, grid=(m // bm, n // bn),
        in_specs=[pl.BlockSpec((bm, k), lambda i, j: (i, 0)), pl.BlockSpec((bn, k), lambda i, j: (j, 0)), ospec, ospec],
        out_specs=(ospec, ospec), out_shape=(jax.ShapeDtypeStruct((m, n), BF16),) * 2,
        compiler_params=_params("parallel", "parallel"),
    )(dout, wd, gmat, umat)


def _make_ffn(name):
    def forward(x, g, wg, wu, wd):
        h = _rms_fwd(x, g, name + "_norm")
        gmat, umat, a = _gate_up_act(h, wg, wu, name + "_gate_up")
        return _mm(a, wd, "nn", name + "_down", res=x), (x, g, h, gmat, umat, a, wg, wu, wd)

    @jax.custom_vjp
    def op(x, g, wg_slot, wu_slot, wd_slot, wg, wu, wd):
        return forward(x, g, wg, wu, wd)[0]

    def fwd(x, g, wg_slot, wu_slot, wd_slot, wg, wu, wd):
        return forward(x, g, wg, wu, wd)

    def bwd(saved, dout):
        x, g, h, gmat, umat, a, wg, wu, wd = saved
        dgm, dum = _down_bwd_act(dout, wd, gmat, umat, name + "_da_act")
        dwd = _mm(a, dout, "tn", name + "_dwd", out_dtype=GRAD_WIRE_DTYPE)
        dwg = _grad_w(h, dgm, "t", name + "_dwg")
        dwu = _grad_w(h, dum, "t", name + "_dwu")
        dh = _times_wt(dum, wu, "t", name + "_dh_u", res=_times_wt(dgm, wg, "t", name + "_dh_g"))
        dx, dg = _rms_bwd(x, g, dh, name + "_norm_bwd", dres=dout)
        return dx, dg, dwg, dwu, dwd, None, None, None

    op.defvjp(fwd, bwd)
    return op


def _make_final_loss(name):
    def run(x, g, tgt):
        s, d = x.shape
        bs = _pick(s, (512, 256, 128))

        def body(x_ref, g_ref, t_ref, loss_ref, dx_ref, dg_ref):
            i = pl.program_id(0)
            xv = x_ref[...]
            gv = g_ref[...]
            r = lax.rsqrt(jnp.mean(xv * xv, axis=-1, keepdims=True) + NORM_EPS)
            xh = xv * r
            e = xh * gv - t_ref[...]
            dy = e * (1.0 / d)
            dxh = dy * gv
            dx_ref[...] = r * (dxh - xh * jnp.mean(dxh * xh, axis=-1, keepdims=True))
            part = 0.5 * jnp.sum(jnp.sum(e * e, axis=-1, keepdims=True) * (1.0 / d), axis=0, keepdims=True)

            @pl.when(i == 0)
            def _():
                loss_ref[...] = jnp.zeros_like(loss_ref)
                dg_ref[...] = jnp.zeros_like(dg_ref)

            loss_ref[...] += jnp.broadcast_to(part, loss_ref.shape)
            dg_ref[...] += jnp.sum(dy * xh, axis=0, keepdims=True)

        row = pl.BlockSpec((bs, d), lambda i: (i, 0))
        vec = pl.BlockSpec((1, d), lambda i: (0, 0))
        loss, dx, dg = pl.pallas_call(
            body, name=name, grid=(s // bs,), in_specs=[row, vec, row],
            out_specs=(pl.BlockSpec((8, LANES), lambda i: (0, 0)), row, vec),
            out_shape=(jax.ShapeDtypeStruct((8, LANES), F32), jax.ShapeDtypeStruct((s, d), F32),
                       jax.ShapeDtypeStruct((1, d), F32)),
            compiler_params=_params("arbitrary"),
        )(x, g.reshape(1, d), tgt)
        return loss[0, 0], dx, dg.reshape(d)

    @jax.custom_vjp
    def op(x, g, tgt):
        return run(x, g, tgt)[0]

    def fwd(x, g, tgt):
        loss, dx, dg = run(x, g, tgt)
        return loss, (dx, dg)

    def bwd(saved, ct):
        dx, dg = saved
        return dx * ct, dg * ct, None

    op.defvjp(fwd, bwd)
    return op


def _model_loss(diff, consts):
    x = diff["x"]
    w = consts["w"]
    slot = diff["slots"]
    vec = diff["vec"]
    tab64, tab_mla = consts["tab64"], consts["tab_mla"]
    mem = consts["mem"]
    s = x.shape[0]

    rope64 = lambda t, nm: _make_rope(HEAD_DIM // 2, nm)(t, *tab64)

    def nl(nm, inp, gain, wnames, through=False, ropes=None):
        orients = tuple(_orient(n) for n in wnames)
        kinds = ropes or (None,) * len(wnames)
        halves = tuple({None: None, "64": HEAD_DIM // 2, "mla": MLA_ROPE_DIM // 2}[r] for r in kinds)
        tabs = tuple({None: None, "64": tab64, "mla": tab_mla}[r] for r in kinds)
        op = _make_norm_linear(nm, orients, through, halves)
        return op(inp, gain, tuple(slot[n] for n in wnames), tuple(w[n] for n in wnames), tabs)

    def lin_res(nm, a, wname, res):
        return _make_linear_res(nm, _orient(wname))(a, slot[wname], w[wname], res)

    def cross(layer, xin):
        p = "l%d_" % layer
        q, xin = nl(p + "xq", xin, vec[p + "x_norm"], (p + "w_xq",), through=True)
        kv, = nl(p + "xkv", mem, vec[p + "mem_norm"], (p + "w_xkv",))
        half = X_HEADS * X_HEAD_DIM
        o = _make_memory_attention(X_HEAD_DIM ** -0.5, p + "xattn")(q, kv[:, :half], kv[:, half:])
        return lin_res(p + "xo", o, p + "w_xo", xin)

    def ffn(layer, xin):
        p = "l%d_" % layer
        names = (p + "w_gate", p + "w_up", p + "w_down")
        return _make_ffn(p + "ffn")(xin, vec[p + "ffn_norm"], *(slot[n] for n in names), *(w[n] for n in names))

    in_parts = tuple(name for name, _, _ in _IN_PARTS) + ("l0_w_in_kr",)
    qa, kva, cq, ckv, kr_lanes, x = nl("l0_in", x, vec["l0_mix_norm"], in_parts, through=True,
                                       ropes=("64", None, None, None, None))
    ka = rope64(kva[:, :A_KV], "l0_rope_ka")
    va = kva[:, A_KV:]
    rep = SWA_HEADS // SWA_KV_HEADS
    expand = lambda t: jnp.broadcast_to(t.reshape(s, SWA_KV_HEADS, 1, HEAD_DIM),
                                        (s, SWA_KV_HEADS, rep, HEAD_DIM)).reshape(s, A_Q)
    swa = _make_band_attention(HEAD_DIM ** -0.5, SWA_WINDOW - 1, BAND_UNITS_PER_STEP, "l0_swa")
    oa = swa(qa, expand(ka), expand(va), vec["l0_sinks"])

    qfull, = nl("l0_uq", cq, vec["l0_q_norm"], ("l0_w_uq_heads",), ropes=("mla",))
    kvb, knope = nl("l0_ukv", ckv, vec["l0_kv_norm"], ("l0_w_ukv", "l0_w_uk_heads"))
    kfull = _make_rope_shared(MLA_ROPE_DIM // 2, "l0_rope_k")(knope, kr_lanes, *tab_mla)
    mla = _make_causal_attention((MLA_NOPE_DIM + MLA_ROPE_DIM) ** -0.5, _pick(s, (1024, 512, 256, 128)), "l0_mla")
    ob = mla(qfull, kfull, kvb)
    x = lin_res("l0_out_a", oa, "l0_w_out_swa", x)
    x = lin_res("l0_out_b", ob, "l0_w_out_mla", x)
    x = cross(0, x)
    x = ffn(0, x)

    q, k, v, x = nl("l1_qkv", x, vec["l1_mix_norm"], ("l1_w_q", "l1_w_k", "l1_w_v"), through=True,
                    ropes=("64", "64", None))
    o = _make_dilated("l1_dil")(q, k, v)
    x = lin_res("l1_out", o, "l1_w_out", x)
    x = cross(1, x)
    x = ffn(1, x)

    return _make_final_loss("final_loss")(x, vec["final_norm"], consts["target"])


MESH_IDS = pl.DeviceIdType.MESH
HBM_SPEC = pl.BlockSpec(memory_space=pltpu.HBM)


def _my_place():
    return lax.axis_index("x"), lax.axis_index("y"), lax.axis_index("c")


def _flip(v, bit):
    return 1 - v if bit else v


def _all_gather_rows(shard):
    r, c_ = shard.shape

    def body(x_ref, out_ref, send_sems, recv_sems, local_sem):
        x, y, c = _my_place()
        me, sibling = (x, y, c), (x, y, 1 - c)
        chips = [(1 - x, y), (x, 1 - y), (1 - x, 1 - y)]

        def slot(px, py, pc):
            return out_ref.at[4 * px + 2 * py + pc]

        def copy(k, block, to, src=None):
            return pltpu.make_async_remote_copy(
                src_ref=slot(*block) if src is None else src, dst_ref=slot(*block), send_sem=send_sems.at[k],
                recv_sem=recv_sems.at[k], device_id=to, device_id_type=MESH_IDS)

        mine = pltpu.make_async_copy(x_ref, slot(*me), local_sem)
        mine.start()
        first = [copy(0, me, sibling, src=x_ref)]
        first += [copy(1 + j, me, (*chip, c), src=x_ref) for j, chip in enumerate(chips)]
        for cp in first:
            cp.start()
        passed = [copy(4 + j, (*chip, c), sibling) for j, chip in enumerate(chips)]
        for j, chip in enumerate(chips):
            copy(1 + j, (*chip, c), me).wait_recv()
            passed[j].start()
        copy(0, sibling, me).wait_recv()
        for j, chip in enumerate(chips):
            copy(4 + j, (*chip, 1 - c), me).wait_recv()
        for cp in first + passed:
            cp.wait_send()
        mine.wait()

    return pl.pallas_call(
        body, name="weights_all_gather", out_shape=jax.ShapeDtypeStruct((N_DEV, r, c_), shard.dtype),
        in_specs=[HBM_SPEC], out_specs=HBM_SPEC,
        scratch_shapes=[pltpu.SemaphoreType.DMA((7,)), pltpu.SemaphoreType.DMA((7,)), pltpu.SemaphoreType.DMA],
    )(shard)


N_CHIPS = 4


def _exchange_with_sibling(slabs):
    _, nq, r, c_ = slabs.shape

    def body(p_ref, out_ref, send_sem, recv_sem):
        x, y, c = _my_place()
        cp = pltpu.make_async_remote_copy(
            src_ref=p_ref.at[1 - c], dst_ref=out_ref, send_sem=send_sem, recv_sem=recv_sem,
            device_id=(x, y, 1 - c), device_id_type=MESH_IDS)
        cp.start()
        cp.wait_recv()
        cp.wait_send()

    return pl.pallas_call(
        body, name="grad_exchange_sibling", out_shape=jax.ShapeDtypeStruct((nq, r, c_), slabs.dtype),
        in_specs=[HBM_SPEC], out_specs=HBM_SPEC,
        scratch_shapes=[pltpu.SemaphoreType.DMA, pltpu.SemaphoreType.DMA],
    )(slabs)


def _add_pairs(a, b):
    nq, r, c_ = a.shape
    br = _pick(r, (256, 128, 64, 32, 16, 8))

    def body(a_ref, b_ref, o_ref):
        o_ref[...] = (a_ref[...].astype(F32) + b_ref[...].astype(F32)).astype(o_ref.dtype)

    blk = pl.BlockSpec((1, br, c_), lambda q, i: (q, i, 0))
    return pl.pallas_call(
        body, name="grad_chip_sum", grid=(nq, r // br), in_specs=[blk, blk], out_specs=blk,
        out_shape=jax.ShapeDtypeStruct(a.shape, a.dtype), compiler_params=_params("parallel", "parallel"),
    )(a, b)


def _exchange_between_chips(slabs):
    nq, r, c_ = slabs.shape

    def body(t_ref, out_ref, send_sems, recv_sems, local_sem):
        x, y, c = _my_place()
        myq = 2 * x + y
        local = pltpu.make_async_copy(t_ref.at[myq], out_ref.at[myq], local_sem)
        local.start()
        sends, recvs = [], []
        for k in range(1, N_CHIPS):
            px, py = _flip(x, k & 2), _flip(y, k & 1)
            peer = 2 * px + py
            sends.append(pltpu.make_async_remote_copy(
                src_ref=t_ref.at[peer], dst_ref=out_ref.at[myq], send_sem=send_sems.at[k - 1],
                recv_sem=recv_sems.at[k - 1], device_id=(px, py, c), device_id_type=MESH_IDS))
            recvs.append(pltpu.make_async_remote_copy(
                src_ref=t_ref.at[myq], dst_ref=out_ref.at[peer], send_sem=send_sems.at[k - 1],
                recv_sem=recv_sems.at[k - 1], device_id=(px, py, c), device_id_type=MESH_IDS))
        for cp in sends:
            cp.start()
        for cp in recvs:
            cp.wait_recv()
        for cp in sends:
            cp.wait_send()
        local.wait()

    return pl.pallas_call(
        body, name="grad_exchange_chips", out_shape=jax.ShapeDtypeStruct(slabs.shape, slabs.dtype),
        in_specs=[HBM_SPEC], out_specs=HBM_SPEC,
        scratch_shapes=[pltpu.SemaphoreType.DMA((N_CHIPS - 1,)), pltpu.SemaphoreType.DMA((N_CHIPS - 1,)),
                        pltpu.SemaphoreType.DMA],
    )(slabs)


def _all_reduce_small(v):
    r, c_ = v.shape

    def body(v_ref, out_ref, buf, send_sems, recv_sems):
        x, y, c = _my_place()
        me = 4 * x + 2 * y + c
        buf[me] = v_ref[...]
        sends, recvs = [], []
        for k in range(1, N_DEV):
            px, py, pc = _flip(x, k & 4), _flip(y, k & 2), _flip(c, k & 1)
            peer = 4 * px + 2 * py + pc
            sends.append(pltpu.make_async_remote_copy(
                src_ref=v_ref, dst_ref=buf.at[me], send_sem=send_sems.at[k - 1], recv_sem=recv_sems.at[k - 1],
                device_id=(px, py, pc), device_id_type=MESH_IDS))
            recvs.append(pltpu.make_async_remote_copy(
                src_ref=v_ref, dst_ref=buf.at[peer], send_sem=send_sems.at[k - 1], recv_sem=recv_sems.at[k - 1],
                device_id=(px, py, pc), device_id_type=MESH_IDS))
        for cp in sends:
            cp.start()
        for cp in recvs:
            cp.wait_recv()
        for cp in sends:
            cp.wait_send()
        acc = buf[0]
        for d in range(1, N_DEV):
            acc = acc + buf[d]
        out_ref[...] = acc

    vm = pl.BlockSpec(memory_space=pltpu.VMEM)
    return pl.pallas_call(
        body, name="vector_grad_all_reduce", out_shape=jax.ShapeDtypeStruct((r, c_), F32), in_specs=[vm], out_specs=vm,
        scratch_shapes=[pltpu.VMEM((N_DEV, r, c_), F32), pltpu.SemaphoreType.DMA((7,)), pltpu.SemaphoreType.DMA((7,))],
    )(v)


def _adamw_math(w, g, m, v):
    m = ADAM_B1 * m + (1.0 - ADAM_B1) * g
    v = ADAM_B2 * v + (1.0 - ADAM_B2) * (g * g)
    m_hat = m / (1.0 - ADAM_B1 ** ADAM_STEP)
    v_hat = v / (1.0 - ADAM_B2 ** ADAM_STEP)
    delta = -ADAM_LR * (m_hat / (jnp.sqrt(v_hat) + ADAM_EPS) + ADAM_WD * w)
    return delta, m, v


def _sum_and_adamw(parts, w, m, v):
    nparts, r, c_ = parts.shape
    br = _pick(r, (256, 128, 64, 32, 16, 8))

    def body(p_ref, w_ref, m_ref, v_ref, g_ref, d_ref, nm_ref, nv_ref):
        g = p_ref[0].astype(F32)
        for d in range(1, nparts):
            g = g + p_ref[d].astype(F32)
        g_ref[...] = g
        d_ref[...], nm_ref[...], nv_ref[...] = _adamw_math(w_ref[...], g, m_ref[...], v_ref[...])

    row = pl.BlockSpec((br, c_), lambda i: (i, 0))
    return pl.pallas_call(
        body, name="grad_sum_adamw", grid=(r // br,),
        in_specs=[pl.BlockSpec((nparts, br, c_), lambda i: (0, i, 0)), row, row, row], out_specs=(row,) * 4,
        out_shape=(jax.ShapeDtypeStruct((r, c_), F32),) * 4, compiler_params=_params("parallel"),
    )(parts, w, m, v)


def _adamw_small(w, g, m, v):
    vm = pl.BlockSpec(memory_space=pltpu.VMEM)

    def body(w_ref, g_ref, m_ref, v_ref, d_ref, nm_ref, nv_ref):
        d_ref[...], nm_ref[...], nv_ref[...] = _adamw_math(w_ref[...], g_ref[...], m_ref[...], v_ref[...])

    return pl.pallas_call(
        body, name="vector_adamw", in_specs=[vm] * 4, out_specs=(vm,) * 3,
        out_shape=(jax.ShapeDtypeStruct(w.shape, F32),) * 3,
    )(w, g, m, v)


def _pad_rows(t, axis):
    extra = -t.shape[axis] % PART_ROW_ALIGN
    if extra == 0:
        return t
    widths = [(0, 0)] * t.ndim
    widths[axis] = (0, extra)
    return jnp.pad(t, widths)


def _pack_local(named):
    rows = [_pad_rows((named[n].T if kind == "c" else named[n]).reshape(-1, PACK_COLS), 0)
            for n, kind, _, _ in MATRICES]
    rows.append(jnp.zeros((MAT_ROWS - MAT_ROWS_USED, PACK_COLS), rows[0].dtype))
    return jnp.concatenate(rows, axis=0)


def _unpack_local(packed):
    out, r0 = {}, 0
    for n, kind, k, nn in MATRICES:
        nr = k * nn // N_DEV // PACK_COLS
        part = packed[r0:r0 + nr]
        out[n] = part.reshape(nn // N_DEV, k).T if kind == "c" else part.reshape(k // N_DEV, nn)
        r0 += _part_rows(k, nn)
    return out


def _unpack_gathered(g):
    out, r0 = {}, 0
    for n, kind, k, nn in MATRICES:
        nr = k * nn // N_DEV // PACK_COLS
        out[n] = g[:, r0:r0 + nr].reshape((nn, k) if kind == "c" else (k, nn))
        r0 += _part_rows(k, nn)
    return out


def _pack_full_grads(grads):
    rows = []
    for n, _, k, nn in MATRICES:
        gmat = grads[n].reshape(N_CHIPS, 2, -1, PACK_COLS).transpose(1, 0, 2, 3)
        rows.append(_pad_rows(gmat, 2))
    rows.append(jnp.zeros((2, N_CHIPS, MAT_ROWS - MAT_ROWS_USED, PACK_COLS), rows[0].dtype))
    return jnp.concatenate(rows, axis=2)


def _pack_vectors(named):
    rows = [jnp.pad(named[n].astype(F32), (0, PACK_COLS - d)) for n, d in VECTORS]
    rows += [jnp.zeros((PACK_COLS,), F32)] * (VEC_ROWS - len(VECTORS))
    return jnp.stack(rows, axis=0)


def _unpack_vectors(packed):
    return {n: packed[i, :d] for i, (n, d) in enumerate(VECTORS)}


def _step(inputs):
    x = inputs["x"][0]
    mem = inputs["mem"][0]
    positions = inputs["positions"][0]
    target = inputs["loss_target"][0]

    local_w = _pack_local({n: inputs[n] for n, _, _, _ in MATRICES})
    gathered = _all_gather_rows(local_w.astype(BF16))
    wfull = _unpack_gathered(gathered)
    vec = {n: inputs[n] for n, _ in VECTORS}

    loss_part, grad_x, gfull, gvec = _local_grads(wfull, vec, x, mem, positions, target)
    loss = lax.psum(loss_part, ("x", "y", "c"))

    slabs = _pack_full_grads(gfull)
    from_sibling = _exchange_with_sibling(slabs)
    mine = lax.dynamic_index_in_dim(slabs, lax.axis_index("c"), axis=0, keepdims=False)
    parts = _exchange_between_chips(_add_pairs(mine, from_sibling))
    local_m = _pack_local({n: inputs["m_" + n] for n, _, _, _ in MATRICES})
    local_v = _pack_local({n: inputs["v_" + n] for n, _, _, _ in MATRICES})
    g_pk, d_pk, m_pk, v_pk = _sum_and_adamw(parts, local_w, local_m, local_v)
    g_mat, d_mat, m_mat, v_mat = (_unpack_local(t) for t in (g_pk, d_pk, m_pk, v_pk))

    g_vec_pk = _all_reduce_small(_pack_vectors(gvec))
    d_vec_pk, m_vec_pk, v_vec_pk = _adamw_small(
        _pack_vectors(vec), g_vec_pk, _pack_vectors({n: inputs["m_" + n] for n, _ in VECTORS}),
        _pack_vectors({n: inputs["v_" + n] for n, _ in VECTORS}))
    g_vec, d_vec, m_vec, v_vec = (_unpack_vectors(t) for t in (g_vec_pk, d_vec_pk, m_vec_pk, v_vec_pk))

    def pick(mats, vecs, n):
        return mats[n] if n in mats else vecs[n]

    outs = [loss, grad_x[None]]
    for mats, vecs in ((g_mat, g_vec), (d_mat, d_vec), (m_mat, m_vec), (v_mat, v_vec)):
        outs += [pick(mats, vecs, n) for n in WEIGHT_ORDER]
    return tuple(outs)


_KIND = {n: kind for n, kind, _, _ in MATRICES}
_VIEW_OF = {"l0_w_uq_heads": "l0_w_uq", "l0_w_uk_heads": "l0_w_ukv", "l0_w_out_swa": "l0_w_out",
            "l0_w_out_mla": "l0_w_out", "l0_w_in_qa": "l0_w_in", "l0_w_in_kva": "l0_w_in", "l0_w_in_cq": "l0_w_in",
            "l0_w_in_ckv": "l0_w_in", "l0_w_in_kr": "l0_w_in", "l1_w_q": "l1_w_qkv", "l1_w_k": "l1_w_qkv",
            "l1_w_v": "l1_w_qkv"}
_IN_PARTS = (("l0_w_in_qa", 0, A_Q), ("l0_w_in_kva", A_Q, A_Q + 2 * A_KV),
             ("l0_w_in_cq", A_Q + 2 * A_KV, A_Q + 2 * A_KV + MLA_Q_RANK),
             ("l0_w_in_ckv", A_Q + 2 * A_KV + MLA_Q_RANK, EVEN_IN - MLA_ROPE_DIM))
_KR_PAD = (MLA_NOPE_DIM, LANES - MLA_NOPE_DIM - MLA_ROPE_DIM)
_MLA_QK = MLA_NOPE_DIM + MLA_ROPE_DIM


def _orient(name):
    return "t" if _KIND[_VIEW_OF.get(name, name)] == "c" else "n"


def _nope_rows():
    return (np.arange(MLA_HEADS * LANES) % LANES < MLA_NOPE_DIM)[:, None]


def _model_weights(wfull):
    w = dict(wfull)
    w_in = w.pop("l0_w_in")
    for name, r0, r1 in _IN_PARTS:
        w[name] = w_in[r0:r1]
    w["l0_w_in_kr"] = jnp.pad(w_in[EVEN_IN - MLA_ROPE_DIM:], (_KR_PAD, (0, 0)))
    w_qkv = w.pop("l1_w_qkv")
    for i, name in enumerate(("l1_w_q", "l1_w_k", "l1_w_v")):
        w[name] = w_qkv[i * D_MODEL:(i + 1) * D_MODEL]
    uq = w.pop("l0_w_uq").reshape(MLA_HEADS, _MLA_QK, MLA_Q_RANK)
    w["l0_w_uq_heads"] = jnp.pad(uq, ((0, 0), (0, LANES - _MLA_QK), (0, 0))).reshape(MLA_HEADS * LANES, MLA_Q_RANK)
    w["l0_w_uk_heads"] = jnp.where(_nope_rows(), wfull["l0_w_ukv"], jnp.zeros_like(wfull["l0_w_ukv"]))
    wo = w.pop("l0_w_out")
    w["l0_w_out_swa"] = wo[:A_Q]
    w["l0_w_out_mla"] = jnp.pad(wo[A_Q:].reshape(MLA_HEADS, HEAD_DIM, D_MODEL),
                                ((0, 0), (LANES - HEAD_DIM, 0), (0, 0))).reshape(MLA_HEADS * LANES, D_MODEL)
    return w


def _matrix_grads(g):
    out = {n: g[n] for n, _, _, _ in MATRICES if n in g}
    out["l0_w_in"] = jnp.concatenate([g[name] for name, _, _ in _IN_PARTS]
                                     + [g["l0_w_in_kr"][_KR_PAD[0]:_KR_PAD[0] + MLA_ROPE_DIM]], axis=0)
    out["l1_w_qkv"] = jnp.concatenate([g["l1_w_q"], g["l1_w_k"], g["l1_w_v"]], axis=0)
    out["l0_w_uq"] = g["l0_w_uq_heads"].reshape(MLA_HEADS, LANES, MLA_Q_RANK)[:, :_MLA_QK].reshape(-1, MLA_Q_RANK)
    uk = jnp.where(_nope_rows(), g["l0_w_uk_heads"], jnp.zeros_like(g["l0_w_uk_heads"]))
    out["l0_w_ukv"] = (g["l0_w_ukv"].astype(F32) + uk.astype(F32)).astype(g["l0_w_ukv"].dtype)
    out["l0_w_out"] = jnp.concatenate(
        [g["l0_w_out_swa"],
         g["l0_w_out_mla"].reshape(MLA_HEADS, LANES, D_MODEL)[:, LANES - HEAD_DIM:].reshape(-1, D_MODEL)], axis=0)
    return out


def _local_grads(wfull, vec, x, mem, positions, target):
    w = _model_weights(wfull)
    slots = {n: jnp.zeros(t.shape, GRAD_WIRE_DTYPE) for n, t in w.items()}
    tab64 = _rope_tables(positions, HEAD_DIM, 0, HEAD_DIM)
    tab_mla = _rope_tables(positions, MLA_ROPE_DIM, MLA_NOPE_DIM, LANES)
    diff = {"x": x, "slots": slots, "vec": vec}
    consts = {"w": w, "mem": mem, "tab64": tab64, "tab_mla": tab_mla, "target": target}
    loss_part, grads = jax.value_and_grad(_model_loss)(diff, consts)
    return loss_part, grads["x"], _matrix_grads(grads["slots"]), grads["vec"]


_INPUT_NAMES = (("x", "mem", "positions") + WEIGHT_ORDER + ("loss_target",)
                + tuple("m_" + n for n in WEIGHT_ORDER) + tuple("v_" + n for n in WEIGHT_ORDER))


def kernel(*args):
    assert len(args) == len(_INPUT_NAMES)
    return _step(dict(zip(_INPUT_NAMES, args)))
```

```python
import numpy as np
import jax
import jax.numpy as jnp
from jax import lax
from jax.experimental import pallas as pl
from jax.experimental.pallas import tpu as pltpu

F32 = jnp.float32
BF16 = jnp.bfloat16

LANES = 128
VMEM_LIMIT_BYTES = 56 * 1024 * 1024
MM_VMEM_BUDGET = 40 * 1024 * 1024
MM_MIN_FLOP_PER_STEP = 1e9
BAND_UNITS_PER_STEP = 4
CAUSAL_ROW_CHAIN = 128
BAND_CHAINS_PER_BATCH = 4

D_MODEL = 1024
HEAD_DIM = 64
ROPE_THETA = 10000.0
NORM_EPS = 1e-6
BLOCK = 128
SWA_HEADS = 8
SWA_KV_HEADS = 2
SWA_WINDOW = 128
MLA_HEADS = 8
MLA_Q_RANK = 384
MLA_KV_RANK = 256
MLA_NOPE_DIM = 64
MLA_ROPE_DIM = 32
A_Q = SWA_HEADS * HEAD_DIM
A_KV = SWA_KV_HEADS * HEAD_DIM
EVEN_IN = A_Q + 2 * A_KV + MLA_Q_RANK + MLA_KV_RANK + MLA_ROPE_DIM
DIL_PATTERNS = ((128, 1), (512, 4), (2048, 16))
X_HEADS = 4
X_HEAD_DIM = 128

ADAM_LR = 0.001
ADAM_B1 = 0.9
ADAM_B2 = 0.999
ADAM_EPS = 1e-08
ADAM_WD = 0.01
ADAM_STEP = 10

N_DEV = 8
GRAD_WIRE_DTYPE = BF16
NEG_MASK = -1e30
NEG_INIT = -1e20

MATRICES = (
    ("l0_w_in", "c", 1024, 1440), ("l0_w_uq", "c", 384, 768), ("l0_w_ukv", "c", 256, 1024),
    ("l0_w_out", "r", 1024, 1024), ("l0_w_xq", "r", 1024, 512), ("l0_w_xkv", "r", 1024, 1024),
    ("l0_w_xo", "c", 512, 1024), ("l0_w_gate", "c", 1024, 2816), ("l0_w_up", "c", 1024, 2816),
    ("l0_w_down", "r", 2816, 1024),
    ("l1_w_qkv", "c", 1024, 3072), ("l1_w_out", "r", 1024, 1024), ("l1_w_xq", "r", 1024, 512),
    ("l1_w_xkv", "r", 1024, 1024), ("l1_w_xo", "c", 512, 1024), ("l1_w_gate", "c", 1024, 2816),
    ("l1_w_up", "c", 1024, 2816), ("l1_w_down", "r", 2816, 1024),
)
VECTORS = (
    ("l0_mix_norm", 1024), ("l0_sinks", 8), ("l0_q_norm", 384), ("l0_kv_norm", 256), ("l0_x_norm", 1024),
    ("l0_mem_norm", 1024), ("l0_ffn_norm", 1024), ("l1_mix_norm", 1024), ("l1_x_norm", 1024),
    ("l1_mem_norm", 1024), ("l1_ffn_norm", 1024), ("final_norm", 1024),
)
WEIGHT_ORDER = (
    "l0_mix_norm", "l0_w_in", "l0_sinks", "l0_q_norm", "l0_w_uq", "l0_kv_norm", "l0_w_ukv", "l0_w_out", "l0_x_norm",
    "l0_mem_norm", "l0_w_xq", "l0_w_xkv", "l0_w_xo", "l0_ffn_norm", "l0_w_gate", "l0_w_up", "l0_w_down",
    "l1_mix_norm", "l1_w_qkv", "l1_w_out", "l1_x_norm", "l1_mem_norm", "l1_w_xq", "l1_w_xkv", "l1_w_xo",
    "l1_ffn_norm", "l1_w_gate", "l1_w_up", "l1_w_down", "final_norm",
)
PACK_COLS = 1024
PART_ROW_ALIGN = 16


def _part_rows(k, n):
    return -(-(k * n // N_DEV // PACK_COLS) // PART_ROW_ALIGN) * PART_ROW_ALIGN


LAYER0 = tuple(mat for mat in MATRICES if mat[0].startswith("l0_"))
LAYER1 = tuple(mat for mat in MATRICES if mat[0].startswith("l1_"))
assert MATRICES == LAYER0 + LAYER1
LAYER0_ROWS = sum(_part_rows(k, n) for _, _, k, n in LAYER0)
MAT_ROWS_USED = sum(_part_rows(k, n) for _, _, k, n in MATRICES)
MAT_ROWS = -(-MAT_ROWS_USED // 256) * 256
VEC_ROWS = 16


def _pick(n, cands):
    for c in cands:
        if n % c == 0:
            return c
    return n


def _params(*sem):
    return pltpu.CompilerParams(dimension_semantics=sem, vmem_limit_bytes=VMEM_LIMIT_BYTES)


_DIMS = {"nn": (((1,), (0,)), ((), ())), "nt": (((1,), (1,)), ((), ())), "tn": (((0,), (0,)), ((), ()))}


def _rotate_block(xv, av, bmv, bpv, half, transpose):
    if transpose:
        return xv * av + pltpu.roll(xv * bmv, LANES - half, 1) + pltpu.roll(xv * bpv, half, 1)
    return xv * av + pltpu.roll(xv, half, 1) * bmv + pltpu.roll(xv, LANES - half, 1) * bpv


def _rotate_tile(t, tabs, half, transpose):
    av, bmv, bpv = tabs
    blocks = [_rotate_block(t[:, c:c + LANES], av, bmv, bpv, half, transpose) for c in range(0, t.shape[1], LANES)]
    return blocks[0] if len(blocks) == 1 else jnp.concatenate(blocks, axis=1)


def _div128(n, cap):
    d = (min(n, cap) // LANES) * LANES
    while d >= LANES:
        if n % d == 0:
            return d
        d -= LANES
    return n


def _mm_vmem_bytes(bm, bn, bk, nk, sa, sb, so, has_res):
    est = 2 * (bm * bk * sa + bk * bn * sb + bm * bn * so) + bm * bn * 4
    est += bm * bn * 4 if nk > 1 else 0
    est += 2 * bm * bn * 4 if has_res else 0
    est += bm * bk * 2 if sa == 4 else 0
    est += bk * bn * 2 if sb == 4 else 0
    return est


def _mm_tiles(m, n, k, sa, sb, so, has_res, mode):
    bn = _div128(n, 1536)
    kcap = 2048 if mode == "tn" else k
    for bm_cap in ((1408, 2816) if mode == "tn" else (512, 1024, 2048)):
        bm = _div128(m, bm_cap)
        bk = (min(k, kcap) // LANES) * LANES
        while bk > LANES and (k % bk or _mm_vmem_bytes(bm, bn, bk, k // bk, sa, sb, so, has_res) > MM_VMEM_BUDGET):
            bk -= LANES
        if 2 * bm * bn * bk >= MM_MIN_FLOP_PER_STEP or bm == m:
            break
    return bm, bn, bk


def _mm(a, b, mode, name, out_dtype=F32, res=None, rope=None):
    if mode == "nn":
        (m, k), (k2, n) = a.shape, b.shape
    elif mode == "nt":
        (m, k), (n, k2) = a.shape, b.shape
    else:
        (k, m), (k2, n) = a.shape, b.shape
    assert k == k2, (name, a.shape, b.shape)
    has_res = res is not None
    bm, bn, bk = _mm_tiles(m, n, k, a.dtype.itemsize, b.dtype.itemsize, jnp.dtype(out_dtype).itemsize, has_res, mode)
    nk = k // bk
    dims = _DIMS[mode]
    a_spec = pl.BlockSpec((bk, bm), lambda i, j, kk: (kk, i)) if mode == "tn" else pl.BlockSpec((bm, bk), lambda i, j, kk: (i, kk))
    b_spec = pl.BlockSpec((bn, bk), lambda i, j, kk: (j, kk)) if mode == "nt" else pl.BlockSpec((bk, bn), lambda i, j, kk: (kk, j))
    o_spec = pl.BlockSpec((bm, bn), lambda i, j, kk: (i, j))

    n_in = 2 + (1 if has_res else 0) + (3 if rope is not None else 0)

    def body(*refs):
        a_ref, b_ref = refs[0], refs[1]
        r_ref = refs[2] if has_res else None
        o_ref = refs[n_in]
        part = lax.dot_general(a_ref[...].astype(BF16), b_ref[...].astype(BF16), dims, preferred_element_type=F32)

        def finish(r):
            if has_res:
                r = r + r_ref[...]
            if rope is not None:
                r = _rotate_tile(r, tuple(t[...] for t in refs[n_in - 3:n_in]), rope[1], False)
            o_ref[...] = r.astype(out_dtype)

        if nk == 1:
            finish(part)
            return
        acc = refs[-1]
        kk = pl.program_id(2)

        @pl.when(kk == 0)
        def _():
            acc[...] = part

        @pl.when(jnp.logical_and(kk > 0, kk < nk - 1))
        def _():
            acc[...] += part

        @pl.when(kk == nk - 1)
        def _():
            finish(acc[...] + part)

    args = (a, b, res) if has_res else (a, b)
    in_specs = [a_spec, b_spec] + ([o_spec] if has_res else [])
    if rope is not None:
        args = args + tuple(rope[0])
        in_specs = in_specs + [pl.BlockSpec((bm, LANES), lambda i, j, kk: (i, 0))] * 3
    return pl.pallas_call(
        body, name=name, grid=(m // bm, n // bn, nk), in_specs=in_specs, out_specs=o_spec,
        out_shape=jax.ShapeDtypeStruct((m, n), out_dtype),
        scratch_shapes=[pltpu.VMEM((bm, bn), F32)] if nk > 1 else [],
        compiler_params=_params("parallel", "parallel", "arbitrary"),
    )(*args)


def _rms_fwd(x, g, name, out_dtype=BF16):
    s, d = x.shape
    bs = _pick(s, (512, 256, 128))

    def body(x_ref, g_ref, o_ref):
        xv = x_ref[...]
        r = lax.rsqrt(jnp.mean(xv * xv, axis=-1, keepdims=True) + NORM_EPS)
        o_ref[...] = ((xv * r) * g_ref[...]).astype(out_dtype)

    return pl.pallas_call(
        body, name=name, grid=(s // bs,),
        in_specs=[pl.BlockSpec((bs, d), lambda i: (i, 0)), pl.BlockSpec((1, d), lambda i: (0, 0))],
        out_specs=pl.BlockSpec((bs, d), lambda i: (i, 0)), out_shape=jax.ShapeDtypeStruct((s, d), out_dtype),
        compiler_params=_params("parallel"),
    )(x, g.reshape(1, d))


def _rms_bwd(x, g, dy, name, dres=None):
    s, d = x.shape
    bs = _pick(s, (512, 256, 128))
    has_res = dres is not None

    def body(*refs):
        if has_res:
            x_ref, g_ref, dy_ref, r_ref, dx_ref, dg_ref = refs
        else:
            x_ref, g_ref, dy_ref, dx_ref, dg_ref = refs
        i = pl.program_id(0)
        xv = x_ref[...]
        dy = dy_ref[...]
        r = lax.rsqrt(jnp.mean(xv * xv, axis=-1, keepdims=True) + NORM_EPS)
        xh = xv * r
        dxh = dy * g_ref[...]
        dx = r * (dxh - xh * jnp.mean(dxh * xh, axis=-1, keepdims=True))
        if has_res:
            dx = dx + r_ref[...]
        dx_ref[...] = dx

        @pl.when(i == 0)
        def _():
            dg_ref[...] = jnp.zeros_like(dg_ref)

        dg_ref[...] += jnp.sum(dy * xh, axis=0, keepdims=True)

    row = pl.BlockSpec((bs, d), lambda i: (i, 0))
    vec = pl.BlockSpec((1, d), lambda i: (0, 0))
    args = (x, g.reshape(1, d), dy) + ((dres,) if has_res else ())
    dx, dg = pl.pallas_call(
        body, name=name, grid=(s // bs,), in_specs=[row, vec, row] + ([row] if has_res else []),
        out_specs=(row, vec), out_shape=(jax.ShapeDtypeStruct((s, d), F32), jax.ShapeDtypeStruct((1, d), F32)),
        compiler_params=_params("arbitrary"),
    )(*args)
    return dx, dg.reshape(d)


def _rope_tables(positions, dh, offset, period):
    role = np.zeros(LANES, np.int32)
    for base in range(0, LANES, period):
        role[base + offset:base + offset + dh // 2] = 1
        role[base + offset + dh // 2:base + offset + dh] = 2
    inv_freq = ROPE_THETA ** (-jnp.arange(0, dh, 2, dtype=F32) / dh)
    one_period = jnp.concatenate([jnp.zeros((offset,), F32), inv_freq, inv_freq,
                                  jnp.zeros((period - offset - dh,), F32)])
    ang = positions.astype(F32)[:, None] * jnp.tile(one_period, LANES // period)[None, :]
    c, s = jnp.cos(ang), jnp.sin(ang)
    role = role[None, :]
    a = jnp.where(role == 0, 1.0, c).astype(F32)
    bm = jnp.where(role == 2, s, 0.0).astype(F32)
    bp = jnp.where(role == 1, -s, 0.0).astype(F32)
    return a, bm, bp


def _rope_apply(x, tabs, half, transpose, name, shared=None, sum_blocks=False):
    s, w = x.shape
    bs = _pick(s, (512, 256, 128))
    nc = w // LANES
    a, bm, bp = tabs
    has_shared = shared is not None

    def body(*refs):
        x_ref, a_ref, bm_ref, bp_ref = refs[:4]
        o_ref = refs[5] if has_shared else refs[4]
        av, bmv, bpv = a_ref[...], bm_ref[...], bp_ref[...]
        total = None
        for c in range(nc):
            sl = slice(c * LANES, (c + 1) * LANES)
            xv = x_ref[:, sl]
            if has_shared:
                xv = xv + refs[4][...]
            out = _rotate_block(xv, av, bmv, bpv, half, transpose)
            o_ref[:, sl] = out
            total = out if total is None else total + out
        if sum_blocks:
            refs[-1][...] = total

    row = pl.BlockSpec((bs, w), lambda i: (i, 0))
    tab = pl.BlockSpec((bs, LANES), lambda i: (i, 0))
    out_shape = jax.ShapeDtypeStruct((s, w), F32)
    return pl.pallas_call(
        body, name=name, grid=(s // bs,), in_specs=[row, tab, tab, tab] + ([tab] if has_shared else []),
        out_specs=(row, tab) if sum_blocks else row,
        out_shape=(out_shape, jax.ShapeDtypeStruct((s, LANES), F32)) if sum_blocks else out_shape,
        compiler_params=_params("parallel"),
    )(x, a, bm, bp, *((shared,) if has_shared else ()))


def _make_rope(half, name):
    @jax.custom_vjp
    def rope(x, a, bm, bp):
        return _rope_apply(x, (a, bm, bp), half, False, name + "_fwd")

    def fwd(x, a, bm, bp):
        return rope(x, a, bm, bp), (a, bm, bp)

    def bwd(tabs, dy):
        return _rope_apply(dy, tabs, half, True, name + "_bwd"), None, None, None

    rope.defvjp(fwd, bwd)
    return rope


def _make_rope_shared(half, name):
    @jax.custom_vjp
    def rope(x, shared, a, bm, bp):
        return _rope_apply(x, (a, bm, bp), half, False, name + "_fwd", shared=shared)

    def fwd(x, shared, a, bm, bp):
        return rope(x, shared, a, bm, bp), (a, bm, bp)

    def bwd(tabs, dy):
        dx, dshared = _rope_apply(dy, tabs, half, True, name + "_bwd", sum_blocks=True)
        return dx, dshared, None, None, None

    rope.defvjp(fwd, bwd)
    return rope


def _lane_masks():
    lane = lax.broadcasted_iota(jnp.int32, (1, LANES), 1)
    lo = lane < HEAD_DIM
    return [lo, jnp.logical_not(lo)]


def _sel(mask, v):
    return jnp.where(mask, v, jnp.zeros_like(v))


_NT = (((1,), (1,)), ((), ()))
_NN = (((1,), (0,)), ((), ()))
_TN = (((0,), (0,)), ((), ()))
_BNT = (((2,), (2,)), ((0,), (0,)))
_BNN = (((2,), (1,)), ((0,), (0,)))


def _dot(a, b, dims):
    return lax.dot_general(a, b, dims, preferred_element_type=F32)


def _band_masks(max_dist):
    assert BLOCK - 1 <= max_dist <= BLOCK
    r = lax.broadcasted_iota(jnp.int32, (BLOCK, BLOCK), 0)
    c = lax.broadcasted_iota(jnp.int32, (BLOCK, BLOCK), 1)
    return (BLOCK + r - c) <= max_dist, r >= c


def _stack_heads(t):
    return jnp.concatenate([t, t], axis=0)


def _head_terms(lms, a, prod, lv):
    t = jnp.sum(_sel(lms[a], prod), axis=-1, keepdims=True)
    lse = jnp.max(jnp.where(lms[a], lv, -jnp.inf), axis=-1, keepdims=True)
    return t, lse


class _Residue:
    def __init__(self, ref, r, dil):
        self.ref, self.rows = ref, pl.ds(r, BLOCK, stride=dil)

    def __getitem__(self, idx):
        return self.ref[self.rows, idx[1]]

    def __setitem__(self, idx, val):
        self.ref[self.rows, idx[1]] = val


def _residues(refs, dil):
    if dil == 1:
        return [tuple(refs)]
    return [tuple(_Residue(x, r, dil) for x in refs) for r in range(dil)]


def _band_fwd(q, k, v, sinkrow, scale, max_dist, upb, dil, name):
    sq, w = q.shape
    rb = BLOCK * dil
    nq, nub, wb = sq // rb, w // (LANES * upb), LANES * upb
    has_sink = sinkrow is not None

    def body(*refs):
        s_ref = refs[5] if has_sink else None
        lms = _lane_masks()
        mprev, mcur = _band_masks(max_dist)
        mprev = jnp.logical_and(mprev, pl.program_id(1) > 0)
        mask2 = _stack_heads(jnp.concatenate([mprev, mcur], axis=1))
        chains = [(rr, slice(u * LANES, (u + 1) * LANES))
                  for rr in _residues(refs[:5] + refs[-2:], dil) for u in range(upb)]
        for g0 in range(0, len(chains), BAND_CHAINS_PER_BATCH):
            group = chains[g0:g0 + BAND_CHAINS_PER_BATCH]
            qs, kcat, vcat, sks = [], [], [], []
            for (q_ref, kp_ref, kc_ref, vp_ref, vc_ref, _, _), sl in group:
                qv = (q_ref[:, sl] * scale).astype(BF16)
                qs.append(jnp.concatenate([_sel(lms[0], qv), _sel(lms[1], qv)], axis=0))
                kcat.append(jnp.concatenate([kp_ref[:, sl].astype(BF16), kc_ref[:, sl].astype(BF16)], axis=0))
                vcat.append(jnp.concatenate([vp_ref[:, sl].astype(BF16), vc_ref[:, sl].astype(BF16)], axis=0))
                if has_sink:
                    sks.append(s_ref[sl.start // LANES])
            qs, kcat, vcat = jnp.stack(qs), jnp.stack(kcat), jnp.stack(vcat)
            sc = jnp.where(mask2[None], _dot(qs, kcat, _BNT), NEG_MASK)
            m = jnp.max(sc, axis=-1, keepdims=True)
            p = jnp.exp(sc - m)
            l = jnp.sum(p, axis=-1, keepdims=True)
            pv = _dot(p.astype(BF16), vcat, _BNN)
            if has_sink:
                sk2 = jnp.stack(sks)
                m_all = jnp.maximum(m, sk2)
                shrink = jnp.exp(m - m_all)
                l = l * shrink + jnp.exp(sk2 - m_all)
                pv, m = pv * shrink, m_all
            o2 = pv / l
            lse2 = m + jnp.log(l)
            for gi, ((_, _, _, _, _, o_ref, l_ref), sl) in enumerate(group):
                o_ref[:, sl] = jnp.where(lms[0], o2[gi, :BLOCK], o2[gi, BLOCK:])
                l_ref[:, sl] = jnp.where(lms[0], lse2[gi, :BLOCK], lse2[gi, BLOCK:])

    cur = pl.BlockSpec((rb, wb), lambda ub, i: (i, ub))
    prev = pl.BlockSpec((rb, wb), lambda ub, i: (jnp.maximum(i - 1, 0), ub))
    in_specs = [cur, prev, cur, prev, cur]
    in_specs += [pl.BlockSpec((upb, 2 * BLOCK, 1), lambda ub, i: (ub, 0, 0))] if has_sink else []
    args = (q, k, k, v, v) + ((sinkrow,) if has_sink else ())
    return pl.pallas_call(
        body, name=name, grid=(nub, nq), in_specs=in_specs, out_specs=(cur, cur),
        out_shape=(jax.ShapeDtypeStruct((sq, w), F32), jax.ShapeDtypeStruct((sq, w), F32)),
        compiler_params=_params("parallel", "parallel"),
    )(*args)


def _band_dq(q, k, v, o, lse, do, sinkrow, scale, max_dist, upb, dil, name):
    sq, w = q.shape
    rb = BLOCK * dil
    nq, nub, wb = sq // rb, w // (LANES * upb), LANES * upb
    has_sink = sinkrow is not None

    def body(*refs):
        if has_sink:
            s_ref, dq_block, dsink_ref = refs[8], refs[9], refs[10]
        else:
            dq_block = refs[8]
        i = pl.program_id(1)
        lms = _lane_masks()
        mprev, mcur = _band_masks(max_dist)
        mprev = jnp.logical_and(mprev, i > 0)
        mask2 = _stack_heads(jnp.concatenate([mprev, mcur], axis=1))
        if has_sink:
            @pl.when(i == 0)
            def _():
                dsink_ref[...] = jnp.zeros_like(dsink_ref)

        chains = [(rr, slice(u * LANES, (u + 1) * LANES))
                  for rr in _residues(refs[:8] + (dq_block,), dil) for u in range(upb)]
        for g0 in range(0, len(chains), BAND_CHAINS_PER_BATCH):
            group = chains[g0:g0 + BAND_CHAINS_PER_BATCH]
            qs, dos, kcat, vcat, t2, lse2 = [], [], [], [], [], []
            for (q_ref, kp_ref, kc_ref, vp_ref, vc_ref, o_ref, l_ref, do_ref, _), sl in group:
                qv = (q_ref[:, sl] * scale).astype(BF16)
                dov = do_ref[:, sl]
                prod = dov * o_ref[:, sl]
                dob = dov.astype(BF16)
                lv = l_ref[:, sl]
                (t0, lse0), (t1, lse1) = _head_terms(lms, 0, prod, lv), _head_terms(lms, 1, prod, lv)
                t2.append(jnp.concatenate([t0, t1], axis=0))
                lse2.append(jnp.concatenate([lse0, lse1], axis=0))
                qs.append(jnp.concatenate([_sel(lms[0], qv), _sel(lms[1], qv)], axis=0))
                dos.append(jnp.concatenate([_sel(lms[0], dob), _sel(lms[1], dob)], axis=0))
                kcat.append(jnp.concatenate([kp_ref[:, sl].astype(BF16), kc_ref[:, sl].astype(BF16)], axis=0))
                vcat.append(jnp.concatenate([vp_ref[:, sl].astype(BF16), vc_ref[:, sl].astype(BF16)], axis=0))
                if has_sink:
                    rs = -jnp.exp(s_ref[:, sl] - lv) * jnp.where(lms[0], t0, t1)
                    dsink_ref[0:1, sl] += jnp.sum(rs, axis=0, keepdims=True)
            qs, dos, kcat, vcat = jnp.stack(qs), jnp.stack(dos), jnp.stack(kcat), jnp.stack(vcat)
            p = jnp.exp(jnp.where(mask2[None], _dot(qs, kcat, _BNT), NEG_MASK) - jnp.stack(lse2))
            ds = (p * (_dot(dos, vcat, _BNT) - jnp.stack(t2))).astype(BF16)
            dq2 = _dot(ds, kcat, _BNN) * scale
            for gi, ((_, _, _, _, _, _, _, _, dq_ref), sl) in enumerate(group):
                dq_ref[:, sl] = jnp.where(lms[0], dq2[gi, :BLOCK], dq2[gi, BLOCK:])

    cur = pl.BlockSpec((rb, wb), lambda ub, i: (i, ub))
    prev = pl.BlockSpec((rb, wb), lambda ub, i: (jnp.maximum(i - 1, 0), ub))
    in_specs = [cur, prev, cur, prev, cur, cur, cur, cur]
    args = (q, k, k, v, v, o, lse, do)
    out_specs, out_shape = cur, jax.ShapeDtypeStruct((sq, w), F32)
    sem = ("parallel", "parallel")
    if has_sink:
        in_specs = in_specs + [pl.BlockSpec((1, wb), lambda ub, i: (0, ub))]
        args = args + (sinkrow,)
        out_specs = (cur, pl.BlockSpec((8, wb), lambda ub, i: (0, ub)))
        out_shape = (out_shape, jax.ShapeDtypeStruct((8, w), F32))
        sem = ("parallel", "arbitrary")
    return pl.pallas_call(
        body, name=name, grid=(nub, nq), in_specs=in_specs, out_specs=out_specs, out_shape=out_shape,
        compiler_params=_params(*sem),
    )(*args)


def _band_dkv(q, k, v, o, lse, do, scale, max_dist, upb, dil, name):
    sq, w = q.shape
    rb = BLOCK * dil
    nq, nub, wb = sq // rb, w // (LANES * upb), LANES * upb

    def body(*refs):
        kb = pl.program_id(1)
        lms = _lane_masks()
        key = lax.broadcasted_iota(jnp.int32, (BLOCK, BLOCK), 0)
        qry = lax.broadcasted_iota(jnp.int32, (BLOCK, BLOCK), 1)
        msame = qry >= key
        mnext = jnp.logical_and((BLOCK + qry - key) <= max_dist, kb < nq - 1)
        mask4 = jnp.concatenate([msame, msame, mnext, mnext], axis=1)
        chains =[(rr, slice(u * LANES, (u + 1) * LANES)) for rr in _residues(refs, dil) for u in range(upb)]
        for g0 in range(0, len(chains), BAND_CHAINS_PER_BATCH):
            group = chains[g0:g0 + BAND_CHAINS_PER_BATCH]
            kvs, vvs, qss, doss, t4s, lse4s = [], [], [], [], [], []
            for (k_ref, v_ref, qs_ref, qn_ref, os_ref, on_ref, ls_ref, ln_ref, dos_ref, don_ref, _, _), sl in group:
                kvs.append(k_ref[:, sl].astype(BF16))
                vvs.append(v_ref[:, sl].astype(BF16))
                qparts, doparts, tparts, lparts = [], [], [], []
                for q_ref, o_ref, l_ref, do_ref in ((qs_ref, os_ref, ls_ref, dos_ref),
                                                    (qn_ref, on_ref, ln_ref, don_ref)):
                    qv = (q_ref[:, sl] * scale).astype(BF16)
                    dov = do_ref[:, sl]
                    prod_t = (dov * o_ref[:, sl]).T
                    dob = dov.astype(BF16)
                    lse_t = l_ref[:, sl].T
                    for a in range(2):
                        lanes = slice(a * HEAD_DIM, (a + 1) * HEAD_DIM)
                        qparts.append(_sel(lms[a], qv))
                        doparts.append(_sel(lms[a], dob))
                        tparts.append(jnp.sum(prod_t[lanes, :], axis=0, keepdims=True))
                        lparts.append(lse_t[a * HEAD_DIM:a * HEAD_DIM + 1, :])
                qss.append(jnp.concatenate(qparts, axis=0))
                doss.append(jnp.concatenate(doparts, axis=0))
                t4s.append(jnp.concatenate(tparts, axis=1))
                lse4s.append(jnp.concatenate(lparts, axis=1))
            kv, vv, qs, dos = jnp.stack(kvs), jnp.stack(vvs), jnp.stack(qss), jnp.stack(doss)
            p = jnp.exp(jnp.where(mask4[None], _dot(kv, qs, _BNT), NEG_MASK) - jnp.stack(lse4s))
            ds = (p * (_dot(vv, dos, _BNT) - jnp.stack(t4s))).astype(BF16)
            dv = _dot(p.astype(BF16), dos, _BNN)
            dk = _dot(ds, qs, _BNN)
            for gi, (rr, sl) in enumerate(group):
                rr[-1][:, sl] = dv[gi]
                rr[-2][:, sl] = dk[gi]

    same = pl.BlockSpec((rb, wb), lambda ub, kb: (kb, ub))
    nxt = pl.BlockSpec((rb, wb), lambda ub, kb: (jnp.minimum(kb + 1, nq - 1), ub))
    return pl.pallas_call(
        body, name=name, grid=(nub, nq), in_specs=[same, same, same, nxt, same, nxt, same, nxt, same, nxt],
        out_specs=(same, same),
        out_shape=(jax.ShapeDtypeStruct((sq, w), F32), jax.ShapeDtypeStruct((sq, w), F32)),
        compiler_params=_params("parallel", "parallel"),
    )(k, v, q, q, o, o, lse, lse, do, do)


def _make_band_attention(scale, max_dist, upb, name):
    @jax.custom_vjp
    def attn(q, k, v, sinks):
        return _band_fwd(q, k, v, _sink_col(sinks), scale, max_dist, upb, 1, name + "_fwd")[0]

    def fwd(q, k, v, sinks):
        o, lse = _band_fwd(q, k, v, _sink_col(sinks), scale, max_dist, upb, 1, name + "_fwd")
        return o, (q, k, v, o, lse, sinks)

    def bwd(res, do):
        q, k, v, o, lse, sinks = res
        dq, dsink = _band_dq(q, k, v, o, lse, do, _sink_row(sinks), scale, max_dist, upb, 1, name + "_dq")
        dk, dv = _band_dkv(q, k, v, o, lse, do, scale, max_dist, upb, 1, name + "_dkv")
        return dq, dk, dv, dsink[0].reshape(-1, HEAD_DIM)[:, 0]

    attn.defvjp(fwd, bwd)
    return attn


def _triangle(n, by_key):
    if by_key:
        pairs = [(i, kb) for kb in range(n) for i in range(kb, n)]
    else:
        pairs = [(i, j) for i in range(n) for j in range(i + 1)]
    qi = np.asarray([p[0] for p in pairs], np.int32)
    kj = np.asarray([p[1] for p in pairs], np.int32)
    return jnp.asarray(qi), jnp.asarray(kj)


def _gather_copies(shard_ref, out_ref, send_sems, recv_sems, arrivals):
    x, y, c = _my_place()
    me = 4 * x + 2 * y + c
    sends, recvs = [], []
    for k in range(1, N_DEV):
        px, py, pc = _flip(x, k & 4), _flip(y, k & 2), _flip(c, k & 1)
        peer = 4 * px + 2 * py + pc
        for slot, into in ((me, sends),) + (((peer, recvs),) if arrivals else ()):
            into.append(pltpu.make_async_remote_copy(
                src_ref=shard_ref, dst_ref=out_ref.at[slot], send_sem=send_sems.at[k - 1],
                recv_sem=recv_sems.at[k - 1], device_id=(px, py, pc), device_id_type=MESH_IDS))
    return me, sends, recvs


def _causal_fwd(q, k, v, scale, blk, name, shard=None):
    s, w = q.shape
    nq, nub = s // blk, w // LANES
    qi, kj = _triangle(nq, by_key=False)
    nsteps = qi.shape[0]
    gathers = shard is not None

    def body(qi_ref, kj_ref, q_ref, k_ref, v_ref, *rest):
        if gathers:
            shard_ref, o_ref, l_ref, gath_ref, m_sc, l_sc, acc_sc, send_sems, recv_sems, local_sem = rest
        else:
            o_ref, l_ref, m_sc, l_sc, acc_sc = rest
        t = pl.program_id(1)
        i, j = qi_ref[t], kj_ref[t]

        if gathers:
            ub = pl.program_id(0)

            @pl.when(jnp.logical_and(ub == 0, t == 0))
            def _():
                me, sends, _ = _gather_copies(shard_ref, gath_ref, send_sems, recv_sems, arrivals=False)
                pltpu.make_async_copy(shard_ref, gath_ref.at[me], local_sem).start()
                for cp in sends:
                    cp.start()

        @pl.when(j == 0)
        def _():
            m_sc[...] = jnp.full_like(m_sc, NEG_INIT)
            l_sc[...] = jnp.zeros_like(l_sc)
            acc_sc[...] = jnp.zeros_like(acc_sc)

        def step(diagonal):
            kv, vv = k_ref[...].astype(BF16), v_ref[...].astype(BF16)
            chains = range(0, blk, CAUSAL_ROW_CHAIN)
            scs = [_dot((q_ref[c0:c0 + CAUSAL_ROW_CHAIN, :] * scale).astype(BF16), kv, _NT) for c0 in chains]
            m_all, l_all, acc_all = m_sc[...], l_sc[...], acc_sc[...]
            m_out, l_out, acc_out = [], [], []
            for sc, c0 in zip(scs, chains):
                rows = slice(c0, c0 + CAUSAL_ROW_CHAIN)
                if diagonal:
                    r = c0 + lax.broadcasted_iota(jnp.int32, (CAUSAL_ROW_CHAIN, blk), 0)
                    c = lax.broadcasted_iota(jnp.int32, (CAUSAL_ROW_CHAIN, blk), 1)
                    sc = jnp.where(r >= c, sc, NEG_MASK)
                m_prev = m_all[rows]
                m_new = jnp.maximum(m_prev, jnp.max(sc, axis=-1, keepdims=True))
                alpha = jnp.exp(m_prev - m_new)
                p = jnp.exp(sc - m_new)
                l_out.append(alpha * l_all[rows] + jnp.sum(p, axis=-1, keepdims=True))
                m_out.append(m_new)
                acc_out.append(acc_all[rows] * alpha + _dot(p.astype(BF16), vv, _NN))
            m_sc[...] = jnp.concatenate(m_out, axis=0)
            l_sc[...] = jnp.concatenate(l_out, axis=0)
            acc_sc[...] = jnp.concatenate(acc_out, axis=0)

        @pl.when(j < i)
        def _():
            step(False)

        @pl.when(j == i)
        def _():
            step(True)
            lf = l_sc[...]
            o_ref[...] = acc_sc[...] / lf
            l_ref[...] = jnp.broadcast_to(m_sc[...] + jnp.log(lf), (blk, LANES))

        if gathers:
            @pl.when(jnp.logical_and(pl.program_id(0) == nub - 1, t == nsteps - 1))
            def _():
                me, sends, recvs = _gather_copies(shard_ref, gath_ref, send_sems, recv_sems, arrivals=True)
                for cp in recvs:
                    cp.wait_recv()
                for cp in sends:
                    cp.wait_send()
                pltpu.make_async_copy(shard_ref, gath_ref.at[me], local_sem).wait()

    qspec = pl.BlockSpec((blk, LANES), lambda ub, t, qi_ref, kj_ref: (qi_ref[t], ub))
    kspec = pl.BlockSpec((blk, LANES), lambda ub, t, qi_ref, kj_ref: (kj_ref[t], ub))
    in_specs, out_specs = [qspec, kspec, kspec], (qspec, qspec)
    out_shape = (jax.ShapeDtypeStruct((s, w), F32), jax.ShapeDtypeStruct((s, w), F32))
    scratch = [pltpu.VMEM((blk, 1), F32), pltpu.VMEM((blk, 1), F32), pltpu.VMEM((blk, LANES), F32)]
    args = (qi, kj, q, k, v)
    if gathers:
        in_specs, out_specs = in_specs + [HBM_SPEC], out_specs + (HBM_SPEC,)
        out_shape = out_shape + (jax.ShapeDtypeStruct((N_DEV,) + shard.shape, shard.dtype),)
        scratch = scratch + [pltpu.SemaphoreType.DMA((N_DEV - 1,)), pltpu.SemaphoreType.DMA((N_DEV - 1,)),
                             pltpu.SemaphoreType.DMA]
        args = args + (shard,)
    return pl.pallas_call(
        body, name=name,
        grid_spec=pltpu.PrefetchScalarGridSpec(
            num_scalar_prefetch=2, grid=(nub, nsteps), in_specs=in_specs, out_specs=out_specs, scratch_shapes=scratch),
        out_shape=out_shape, compiler_params=_params("arbitrary", "arbitrary"),
    )(*args)


def _causal_bwd(q, k, v, o, lse, do, scale, blk, name):
    s, w = q.shape
    nq, nub = s // blk, w // LANES
    qi, kj = _triangle(nq, by_key=True)

    def body(qi_ref, kj_ref, q_ref, k_ref, v_ref, o_ref, l_ref, do_ref, dq_ref, dk_ref, dv_ref, dk_acc, dv_acc):
        t = pl.program_id(1)
        i, kb = qi_ref[t], kj_ref[t]

        @pl.when(t == 0)
        def _():
            dq_ref[...] = jnp.zeros_like(dq_ref)

        @pl.when(i == kb)
        def _():
            dk_acc[...] = jnp.zeros_like(dk_acc)
            dv_acc[...] = jnp.zeros_like(dv_acc)

        def step(diagonal):
            qv = (q_ref[...] * scale).astype(BF16)
            kv, vv = k_ref[...].astype(BF16), v_ref[...].astype(BF16)
            dov = do_ref[...]
            tsum = jnp.sum(dov * o_ref[...], axis=-1, keepdims=True)
            dob = dov.astype(BF16)
            sc = _dot(qv, kv, _NT)
            if diagonal:
                r = lax.broadcasted_iota(jnp.int32, (blk, blk), 0)
                c = lax.broadcasted_iota(jnp.int32, (blk, blk), 1)
                sc = jnp.where(r >= c, sc, NEG_MASK)
            p = jnp.exp(sc - l_ref[:, 0:1])
            ds = (p * (_dot(dob, vv, _NT) - tsum)).astype(BF16)
            dv_acc[...] += _dot(p.astype(BF16), dob, _TN)
            dk_acc[...] += _dot(ds, qv, _TN)
            rows = pl.ds(pl.multiple_of(i * blk, blk), blk)
            dq_ref[rows, :] += _dot(ds, kv, _NN) * scale

        @pl.when(i == kb)
        def _():
            step(True)

        @pl.when(i > kb)
        def _():
            step(False)

        @pl.when(i == nq - 1)
        def _():
            dk_ref[...] = dk_acc[...]
            dv_ref[...] = dv_acc[...]

    qspec = pl.BlockSpec((blk, LANES), lambda ub, t, qi_ref, kj_ref: (qi_ref[t], ub))
    kspec = pl.BlockSpec((blk, LANES), lambda ub, t, qi_ref, kj_ref: (kj_ref[t], ub))
    whole = pl.BlockSpec((s, LANES), lambda ub, t, qi_ref, kj_ref: (0, ub))
    out = jax.ShapeDtypeStruct((s, w), F32)
    return pl.pallas_call(
        body, name=name,
        grid_spec=pltpu.PrefetchScalarGridSpec(
            num_scalar_prefetch=2, grid=(nub, qi.shape[0]), in_specs=[qspec, kspec, kspec, qspec, qspec, qspec],
            out_specs=(whole, kspec, kspec),
            scratch_shapes=[pltpu.VMEM((blk, LANES), F32), pltpu.VMEM((blk, LANES), F32)]),
        out_shape=(out, out, out), compiler_params=_params("parallel", "arbitrary"),
    )(qi, kj, q, k, v, o, lse, do)


def _make_causal_attention(scale, blk, name):
    @jax.custom_vjp
    def attn(q, k, v, shard):
        o, _, gathered = _causal_fwd(q, k, v, scale, blk, name + "_fwd", shard=shard)
        return o, gathered

    def fwd(q, k, v, shard):
        o, lse, gathered = _causal_fwd(q, k, v, scale, blk, name + "_fwd", shard=shard)
        return (o, gathered), (q, k, v, o, lse)

    def bwd(res, cts):
        q, k, v, o, lse = res
        return _causal_bwd(q, k, v, o, lse, cts[0], scale, blk, name + "_bwd") + (None,)

    attn.defvjp(fwd, bwd)
    return attn


_BTN = (((1,), (1,)), ((0,), (0,)))


def _heads(ref, scale=None):
    blocks = []
    for c in range(0, ref.shape[1], LANES):
        t = ref[:, c:c + LANES]
        blocks.append((t if scale is None else t * scale).astype(BF16))
    return jnp.stack(blocks)


def _memory_fwd(q, k, v, scale, name):
    s, w = q.shape
    m = k.shape[0]
    bq = _pick(s, (512, 256, 128))

    def body(q_ref, k_ref, v_ref, o_ref, l_ref):
        sc = _dot(_heads(q_ref, scale), _heads(k_ref), _BNT)
        mx = jnp.max(sc, axis=-1, keepdims=True)
        p = jnp.exp(sc - mx)
        l = jnp.sum(p, axis=-1, keepdims=True)
        o = _dot(p.astype(BF16), _heads(v_ref), _BNN) / l
        lse = mx + jnp.log(l)
        for h in range(w // LANES):
            o_ref[:, h * LANES:(h + 1) * LANES] = o[h]
            l_ref[:, h * LANES:(h + 1) * LANES] = jnp.broadcast_to(lse[h], (bq, LANES))

    row = pl.BlockSpec((bq, w), lambda i: (i, 0))
    mem = pl.BlockSpec((m, w), lambda i: (0, 0))
    return pl.pallas_call(
        body, name=name, grid=(s // bq,), in_specs=[row, mem, mem], out_specs=(row, row),
        out_shape=(jax.ShapeDtypeStruct((s, w), F32), jax.ShapeDtypeStruct((s, w), F32)),
        compiler_params=_params("parallel"),
    )(q, k, v)


def _memory_bwd(q, k, v, o, lse, do, scale, name):
    s, w = q.shape
    m = k.shape[0]
    nh = w // LANES
    bq = _pick(s, (512, 256, 128))

    def body(q_ref, k_ref, v_ref, o_ref, l_ref, do_ref, dq_ref, dk_ref, dv_ref):
        qs, ks, vs = _heads(q_ref, scale), _heads(k_ref), _heads(v_ref)
        dos = _heads(do_ref)
        t = jnp.stack([jnp.sum(do_ref[:, h * LANES:(h + 1) * LANES] * o_ref[:, h * LANES:(h + 1) * LANES],
                               axis=-1, keepdims=True) for h in range(nh)])
        lse = jnp.stack([l_ref[:, h * LANES:h * LANES + 1] for h in range(nh)])
        p = jnp.exp(_dot(qs, ks, _BNT) - lse)
        ds = (p * (_dot(dos, vs, _BNT) - t)).astype(BF16)
        dq = _dot(ds, ks, _BNN) * scale
        dk = _dot(ds, qs, _BTN)
        dv = _dot(p.astype(BF16), dos, _BTN)

        @pl.when(pl.program_id(0) == 0)
        def _():
            dk_ref[...] = jnp.zeros_like(dk_ref)
            dv_ref[...] = jnp.zeros_like(dv_ref)

        for h in range(nh):
            sl = slice(h * LANES, (h + 1) * LANES)
            dq_ref[:, sl] = dq[h]
            dk_ref[:, sl] += dk[h]
            dv_ref[:, sl] += dv[h]

    row = pl.BlockSpec((bq, w), lambda i: (i, 0))
    mem = pl.BlockSpec((m, w), lambda i: (0, 0))
    return pl.pallas_call(
        body, name=name, grid=(s // bq,), in_specs=[row, mem, mem, row, row, row], out_specs=(row, mem, mem),
        out_shape=(jax.ShapeDtypeStruct((s, w), F32), jax.ShapeDtypeStruct((m, w), F32),
                   jax.ShapeDtypeStruct((m, w), F32)),
        compiler_params=_params("arbitrary"),
    )(q, k, v, o, lse, do)


def _make_memory_attention(scale, name):
    @jax.custom_vjp
    def attn(q, k, v):
        return _memory_fwd(q, k, v, scale, name + "_fwd")[0]

    def fwd(q, k, v):
        o, lse = _memory_fwd(q, k, v, scale, name + "_fwd")
        return o, (q, k, v, o, lse)

    def bwd(res, do):
        q, k, v, o, lse = res
        return _memory_bwd(q, k, v, o, lse, do, scale, name + "_bwd")

    attn.defvjp(fwd, bwd)
    return attn


def _sink_row(sinks):
    return jnp.repeat(sinks.astype(F32), HEAD_DIM).reshape(1, -1)


def _sink_col(sinks):
    return jnp.repeat(sinks.astype(F32).reshape(-1, 2, 1), BLOCK, axis=1)


def _merge3(os_, ls_, name):
    s, w = os_[0].shape
    bs = _pick(s, (256, 128))

    def body(o1, o2, o3, l1, l2, l3, out_ref, lse_ref):
        a1, a2, a3 = l1[...], l2[...], l3[...]
        m = jnp.maximum(jnp.maximum(a1, a2), a3)
        e1, e2, e3 = jnp.exp(a1 - m), jnp.exp(a2 - m), jnp.exp(a3 - m)
        z = e1 + e2 + e3
        out_ref[...] = (e1 * o1[...] + e2 * o2[...] + e3 * o3[...]) / z
        lse_ref[...] = m + jnp.log(z)

    row = pl.BlockSpec((bs, w), lambda i: (i, 0))
    return pl.pallas_call(
        body, name=name, grid=(s // bs,), in_specs=[row] * 6, out_specs=(row, row),
        out_shape=(jax.ShapeDtypeStruct((s, w), F32), jax.ShapeDtypeStruct((s, w), F32)),
        compiler_params=_params("parallel"),
    )(*os_, *ls_)


def _add3(a, b, c, name):
    s, w = a.shape
    bs = _pick(s, (512, 256, 128))

    def body(a_ref, b_ref, c_ref, o_ref):
        o_ref[...] = (a_ref[...] + b_ref[...]) + c_ref[...]

    row = pl.BlockSpec((bs, w), lambda i: (i, 0))
    return pl.pallas_call(
        body, name=name, grid=(s // bs,), in_specs=[row] * 3, out_specs=row,
        out_shape=jax.ShapeDtypeStruct((s, w), F32), compiler_params=_params("parallel"),
    )(a, b, c)


def _make_dilated(name):
    scale, max_dist = HEAD_DIM ** -0.5, BLOCK

    def upb_of(dil):
        return 2 * BAND_UNITS_PER_STEP if dil == 1 else 1

    def forward(q, k, v):
        os_, ls_ = [], []
        for n, (_, dil) in enumerate(DIL_PATTERNS):
            o, l = _band_fwd(q, k, v, None, scale, max_dist, upb_of(dil), dil, "%s_b%d_fwd" % (name, n))
            os_.append(o)
            ls_.append(l)
        return _merge3(os_, ls_, name + "_merge")

    @jax.custom_vjp
    def dilated(q, k, v):
        return forward(q, k, v)[0]

    def fwd(q, k, v):
        out, lse = forward(q, k, v)
        return out, (q, k, v, out, lse)

    def bwd(res, do):
        q, k, v, out, lse = res
        dqs, dks, dvs = [], [], []
        for n, (_, dil) in enumerate(DIL_PATTERNS):
            args = (q, k, v, out, lse, do)
            dqs.append(_band_dq(*args, None, scale, max_dist, upb_of(dil), dil, "%s_b%d_dq" % (name, n)))
            dk, dv = _band_dkv(*args, scale, max_dist, upb_of(dil), dil, "%s_b%d_dkv" % (name, n))
            dks.append(dk)
            dvs.append(dv)
        return (_add3(*dqs, name + "_dq_sum"), _add3(*dks, name + "_dk_sum"), _add3(*dvs, name + "_dv_sum"))

    dilated.defvjp(fwd, bwd)
    return dilated


def _times_w(a, w, orient, name, res=None, rope=None):
    return _mm(a, w, "nn" if orient == "n" else "nt", name, res=res, rope=rope)


def _times_wt(dz, w, orient, name, res=None):
    return _mm(dz, w, "nt" if orient == "n" else "nn", name, res=res)


def _grad_w(a, dz, orient, name):
    if orient == "n":
        return _mm(a, dz, "tn", name, out_dtype=GRAD_WIRE_DTYPE)
    return _mm(dz, a, "tn", name, out_dtype=GRAD_WIRE_DTYPE)


def _make_norm_linear(name, orients, through=False, rope_halves=None):
    nw = len(orients)
    halves = rope_halves or (None,) * nw

    def rope_of(i, ropes):
        return None if halves[i] is None else (ropes[i], halves[i])

    def forward(x, g, ws, ropes):
        h = _rms_fwd(x, g, name + "_norm")
        zs = tuple(_times_w(h, w, o, "%s_mm%d" % (name, i), rope=rope_of(i, ropes))
                   for i, (w, o) in enumerate(zip(ws, orients)))
        return zs + ((x,) if through else ()), h

    @jax.custom_vjp
    def op(x, g, slots, ws, ropes):
        return forward(x, g, ws, ropes)[0]

    def fwd(x, g, slots, ws, ropes):
        outs, h = forward(x, g, ws, ropes)
        return outs, (x, g, h, ws, ropes)

    def bwd(res, cts):
        x, g, h, ws, ropes = res
        dzs = [cts[i] if halves[i] is None else
               _rope_apply(cts[i], ropes[i], halves[i], True, "%s_unrope%d" % (name, i)) for i in range(nw)]
        dh = None
        for i, (w, o) in enumerate(zip(ws, orients)):
            dh = _times_wt(dzs[i], w, o, "%s_dh%d" % (name, i), res=dh)
        dws = tuple(_grad_w(h, dzs[i], o, "%s_dw%d" % (name, i)) for i, o in enumerate(orients))
        dx, dg = _rms_bwd(x, g, dh, name + "_norm_bwd", dres=cts[nw] if through else None)
        return dx, dg, dws, (None,) * nw, tuple(None if r is None else (None,) * len(r) for r in ropes)

    op.defvjp(fwd, bwd)
    return op


def _make_linear_res(name, orient):
    @jax.custom_vjp
    def op(a, wslot, w, res):
        return _times_w(a, w, orient, name + "_mm", res=res)

    def fwd(a, wslot, w, res):
        return _times_w(a, w, orient, name + "_mm", res=res), (a, w)

    def bwd(saved, dout):
        a, w = saved
        return _times_wt(dout, w, orient, name + "_da"), _grad_w(a, dout, orient, name + "_dw"), None, dout

    op.defvjp(fwd, bwd)
    return op


FFN_TILE_M, FFN_TILE_N = 512, 1408


def _gate_up_act(h, wg, wu, name):
    m, k = h.shape
    n = wg.shape[0]
    bm, bn = _div128(m, FFN_TILE_M), _div128(n, FFN_TILE_N)

    def body(h_ref, wg_ref, wu_ref, g_ref, u_ref, a_ref):
        hv = h_ref[...]
        g = _dot(hv, wg_ref[...], _NT)
        u = _dot(hv, wu_ref[...], _NT)
        g_ref[...] = g
        u_ref[...] = u
        a_ref[...] = (g / (1.0 + jnp.exp(-g)) * u).astype(BF16)

    wspec = pl.BlockSpec((bn, k), lambda i, j: (j, 0))
    ospec = pl.BlockSpec((bm, bn), lambda i, j: (i, j))
    return pl.pallas_call(
        body, name=name, grid=(m // bm, n // bn), in_specs=[pl.BlockSpec((bm, k), lambda i, j: (i, 0)), wspec, wspec],
        out_specs=(ospec, ospec, ospec),
        out_shape=(jax.ShapeDtypeStruct((m, n), F32), jax.ShapeDtypeStruct((m, n), F32),
                   jax.ShapeDtypeStruct((m, n), BF16)),
        compiler_params=_params("parallel", "parallel"),
    )(h, wg, wu)


def _down_bwd_act(dout, wd, gmat, umat, name):
    m, k = dout.shape
    n = wd.shape[0]
    bm, bn = _div128(m, FFN_TILE_M), _div128(n, FFN_TILE_N)

    def body(do_ref, wd_ref, g_ref, u_ref, dg_ref, du_ref):
        d = _dot(do_ref[...].astype(BF16), wd_ref[...], _NT)
        g, u = g_ref[...], u_ref[...]
        sig = 1.0 / (1.0 + jnp.exp(-g))
        dg_ref[...] = (d * u * (sig * (1.0 + g * (1.0 - sig)))).astype(BF16)
        du_ref[...] = (d * (g * sig)).astype(BF16)

    ospec = pl.BlockSpec((bm, bn), lambda i, j: (i, j))
    return pl.pallas_call(
        body, name=name, grid=(m // bm, n // bn),
        in_specs=[pl.BlockSpec((bm, k), lambda i, j: (i, 0)), pl.BlockSpec((bn, k), lambda i, j: (j, 0)), ospec, ospec],
        out_specs=(ospec, ospec), out_shape=(jax.ShapeDtypeStruct((m, n), BF16),) * 2,
        compiler_params=_params("parallel", "parallel"),
    )(dout, wd, gmat, umat)


def _make_ffn(name):
    def forward(x, g, wg, wu, wd):
        h = _rms_fwd(x, g, name + "_norm")
        gmat, umat, a = _gate_up_act(h, wg, wu, name + "_gate_up")
        return _mm(a, wd, "nn", name + "_down", res=x), (x, g, h, gmat, umat, a, wg, wu, wd)

    @jax.custom_vjp
    def op(x, g, wg_slot, wu_slot, wd_slot, wg, wu, wd):
        return forward(x, g, wg, wu, wd)[0]

    def fwd(x, g, wg_slot, wu_slot, wd_slot, wg, wu, wd):
        return forward(x, g, wg, wu, wd)

    def bwd(saved, dout):
        x, g, h, gmat, umat, a, wg, wu, wd = saved
        dgm, dum = _down_bwd_act(dout, wd, gmat, umat, name + "_da_act")
        dwd = _mm(a, dout, "tn", name + "_dwd", out_dtype=GRAD_WIRE_DTYPE)
        dwg = _grad_w(h, dgm, "t", name + "_dwg")
        dwu = _grad_w(h, dum, "t", name + "_dwu")
        dh = _times_wt(dum, wu, "t", name + "_dh_u", res=_times_wt(dgm, wg, "t", name + "_dh_g"))
        dx, dg = _rms_bwd(x, g, dh, name + "_norm_bwd", dres=dout)
        return dx, dg, dwg, dwu, dwd, None, None, None

    op.defvjp(fwd, bwd)
    return op


def _make_final_loss(name):
    def run(x, g, tgt):
        s, d = x.shape
        bs = _pick(s, (512, 256, 128))

        def body(x_ref, g_ref, t_ref, loss_ref, dx_ref, dg_ref):
            i = pl.program_id(0)
            xv = x_ref[...]
            gv = g_ref[...]
            r = lax.rsqrt(jnp.mean(xv * xv, axis=-1, keepdims=True) + NORM_EPS)
            xh = xv * r
            e = xh * gv - t_ref[...]
            dy = e * (1.0 / d)
            dxh = dy * gv
            dx_ref[...] = r * (dxh - xh * jnp.mean(dxh * xh, axis=-1, keepdims=True))
            part = 0.5 * jnp.sum(jnp.sum(e * e, axis=-1, keepdims=True) * (1.0 / d), axis=0, keepdims=True)

            @pl.when(i == 0)
            def _():
                loss_ref[...] = jnp.zeros_like(loss_ref)
                dg_ref[...] = jnp.zeros_like(dg_ref)

            loss_ref[...] += jnp.broadcast_to(part, loss_ref.shape)
            dg_ref[...] += jnp.sum(dy * xh, axis=0, keepdims=True)

        row = pl.BlockSpec((bs, d), lambda i: (i, 0))
        vec = pl.BlockSpec((1, d), lambda i: (0, 0))
        loss, dx, dg = pl.pallas_call(
            body, name=name, grid=(s // bs,), in_specs=[row, vec, row],
            out_specs=(pl.BlockSpec((8, LANES), lambda i: (0, 0)), row, vec),
            out_shape=(jax.ShapeDtypeStruct((8, LANES), F32), jax.ShapeDtypeStruct((s, d), F32),
                       jax.ShapeDtypeStruct((1, d), F32)),
            compiler_params=_params("arbitrary"),
        )(x, g.reshape(1, d), tgt)
        return loss[0, 0], dx, dg.reshape(d)

    @jax.custom_vjp
    def op(x, g, tgt):
        return run(x, g, tgt)[0]

    def fwd(x, g, tgt):
        loss, dx, dg = run(x, g, tgt)
        return loss, (dx, dg)

    def bwd(saved, ct):
        dx, dg = saved
        return dx * ct, dg * ct, None

    op.defvjp(fwd, bwd)
    return op


def _model_loss(diff, consts):
    x = diff["x"]
    w = consts["w"]
    slot = diff["slots"]
    vec = diff["vec"]
    tab64, tab_mla = consts["tab64"], consts["tab_mla"]
    mem = consts["mem"]
    s = x.shape[0]

    rope64 = lambda t, nm: _make_rope(HEAD_DIM // 2, nm)(t, *tab64)

    def nl(nm, inp, gain, wnames, through=False, ropes=None):
        orients = tuple(_orient(n) for n in wnames)
        kinds = ropes or (None,) * len(wnames)
        halves = tuple({None: None, "64": HEAD_DIM // 2, "mla": MLA_ROPE_DIM // 2}[r] for r in kinds)
        tabs = tuple({None: None, "64": tab64, "mla": tab_mla}[r] for r in kinds)
        op = _make_norm_linear(nm, orients, through, halves)
        return op(inp, gain, tuple(slot[n] for n in wnames), tuple(w[n] for n in wnames), tabs)

    def lin_res(nm, a, wname, res):
        return _make_linear_res(nm, _orient(wname))(a, slot[wname], w[wname], res)

    def cross(layer, xin):
        p = "l%d_" % layer
        q, xin = nl(p + "xq", xin, vec[p + "x_norm"], (p + "w_xq",), through=True)
        kv, = nl(p + "xkv", mem, vec[p + "mem_norm"], (p + "w_xkv",))
        half = X_HEADS * X_HEAD_DIM
        o = _make_memory_attention(X_HEAD_DIM ** -0.5, p + "xattn")(q, kv[:, :half], kv[:, half:])
        return lin_res(p + "xo", o, p + "w_xo", xin)

    def ffn(layer, xin):
        p = "l%d_" % layer
        names = (p + "w_gate", p + "w_up", p + "w_down")
        return _make_ffn(p + "ffn")(xin, vec[p + "ffn_norm"], *(slot[n] for n in names), *(w[n] for n in names))

    in_parts = tuple(name for name, _, _ in _IN_PARTS) + ("l0_w_in_kr",)
    qa, kva, cq, ckv, kr_lanes, x = nl("l0_in", x, vec["l0_mix_norm"], in_parts, through=True,
                                       ropes=("64", None, None, None, None))
    ka = rope64(kva[:, :A_KV], "l0_rope_ka")
    va = kva[:, A_KV:]
    rep = SWA_HEADS // SWA_KV_HEADS
    expand = lambda t: jnp.broadcast_to(t.reshape(s, SWA_KV_HEADS, 1, HEAD_DIM),
                                        (s, SWA_KV_HEADS, rep, HEAD_DIM)).reshape(s, A_Q)
    swa = _make_band_attention(HEAD_DIM ** -0.5, SWA_WINDOW - 1, BAND_UNITS_PER_STEP, "l0_swa")
    oa = swa(qa, expand(ka), expand(va), vec["l0_sinks"])

    qfull, = nl("l0_uq", cq, vec["l0_q_norm"], ("l0_w_uq_heads",), ropes=("mla",))
    kvb, knope = nl("l0_ukv", ckv, vec["l0_kv_norm"], ("l0_w_ukv", "l0_w_uk_heads"))
    kfull = _make_rope_shared(MLA_ROPE_DIM // 2, "l0_rope_k")(knope, kr_lanes, *tab_mla)
    mla = _make_causal_attention((MLA_NOPE_DIM + MLA_ROPE_DIM) ** -0.5, _pick(s, (1024, 512, 256, 128)), "l0_mla")
    ob, gathered1 = mla(qfull, kfull, kvb, consts["shard1"])
    w = {**w, **_layer1_weights(_unpack_gathered(gathered1, LAYER1))}
    x = lin_res("l0_out_a", oa, "l0_w_out_swa", x)
    x = lin_res("l0_out_b", ob, "l0_w_out_mla", x)
    x = cross(0, x)
    x = ffn(0, x)

    q, k, v, x = nl("l1_qkv", x, vec["l1_mix_norm"], ("l1_w_q", "l1_w_k", "l1_w_v"), through=True,
                    ropes=("64", "64", None))
    o = _make_dilated("l1_dil")(q, k, v)
    x = lin_res("l1_out", o, "l1_w_out", x)
    x = cross(1, x)
    x = ffn(1, x)

    return _make_final_loss("final_loss")(x, vec["final_norm"], consts["target"])


MESH_IDS = pl.DeviceIdType.MESH
HBM_SPEC = pl.BlockSpec(memory_space=pltpu.HBM)


def _my_place():
    return lax.axis_index("x"), lax.axis_index("y"), lax.axis_index("c")


def _flip(v, bit):
    return 1 - v if bit else v


def _all_gather_rows(shard):
    r, c_ = shard.shape

    def body(x_ref, out_ref, send_sems, recv_sems, local_sem):
        x, y, c = _my_place()
        me, sibling = (x, y, c), (x, y, 1 - c)
        chips = [(1 - x, y), (x, 1 - y), (1 - x, 1 - y)]

        def slot(px, py, pc):
            return out_ref.at[4 * px + 2 * py + pc]

        def copy(k, block, to, src=None):
            return pltpu.make_async_remote_copy(
                src_ref=slot(*block) if src is None else src, dst_ref=slot(*block), send_sem=send_sems.at[k],
                recv_sem=recv_sems.at[k], device_id=to, device_id_type=MESH_IDS)

        mine = pltpu.make_async_copy(x_ref, slot(*me), local_sem)
        mine.start()
        first = [copy(0, me, sibling, src=x_ref)]
        first += [copy(1 + j, me, (*chip, c), src=x_ref) for j, chip in enumerate(chips)]
        for cp in first:
            cp.start()
        passed = [copy(4 + j, (*chip, c), sibling) for j, chip in enumerate(chips)]
        for j, chip in enumerate(chips):
            copy(1 + j, (*chip, c), me).wait_recv()
            passed[j].start()
        copy(0, sibling, me).wait_recv()
        for j, chip in enumerate(chips):
            copy(4 + j, (*chip, 1 - c), me).wait_recv()
        for cp in first + passed:
            cp.wait_send()
        mine.wait()

    return pl.pallas_call(
        body, name="weights_all_gather", out_shape=jax.ShapeDtypeStruct((N_DEV, r, c_), shard.dtype),
        in_specs=[HBM_SPEC], out_specs=HBM_SPEC,
        scratch_shapes=[pltpu.SemaphoreType.DMA((7,)), pltpu.SemaphoreType.DMA((7,)), pltpu.SemaphoreType.DMA],
    )(shard)


N_CHIPS = 4


def _exchange_with_sibling(slabs):
    _, nq, r, c_ = slabs.shape

    def body(p_ref, out_ref, send_sem, recv_sem):
        x, y, c = _my_place()
        cp = pltpu.make_async_remote_copy(
            src_ref=p_ref.at[1 - c], dst_ref=out_ref, send_sem=send_sem, recv_sem=recv_sem,
            device_id=(x, y, 1 - c), device_id_type=MESH_IDS)
        cp.start()
        cp.wait_recv()
        cp.wait_send()

    return pl.pallas_call(
        body, name="grad_exchange_sibling", out_shape=jax.ShapeDtypeStruct((nq, r, c_), slabs.dtype),
        in_specs=[HBM_SPEC], out_specs=HBM_SPEC,
        scratch_shapes=[pltpu.SemaphoreType.DMA, pltpu.SemaphoreType.DMA],
    )(slabs)


def _add_pairs(a, b):
    nq, r, c_ = a.shape
    br = _pick(r, (256, 128, 64, 32, 16, 8))

    def body(a_ref, b_ref, o_ref):
        o_ref[...] = (a_ref[...].astype(F32) + b_ref[...].astype(F32)).astype(o_ref.dtype)

    blk = pl.BlockSpec((1, br, c_), lambda q, i: (q, i, 0))
    return pl.pallas_call(
        body, name="grad_chip_sum", grid=(nq, r // br), in_specs=[blk, blk], out_specs=blk,
        out_shape=jax.ShapeDtypeStruct(a.shape, a.dtype), compiler_params=_params("parallel", "parallel"),
    )(a, b)


def _exchange_between_chips(slabs):
    nq, r, c_ = slabs.shape

    def body(t_ref, out_ref, send_sems, recv_sems, local_sem):
        x, y, c = _my_place()
        myq = 2 * x + y
        local = pltpu.make_async_copy(t_ref.at[myq], out_ref.at[myq], local_sem)
        local.start()
        sends, recvs = [], []
        for k in range(1, N_CHIPS):
            px, py = _flip(x, k & 2), _flip(y, k & 1)
            peer = 2 * px + py
            sends.append(pltpu.make_async_remote_copy(
                src_ref=t_ref.at[peer], dst_ref=out_ref.at[myq], send_sem=send_sems.at[k - 1],
                recv_sem=recv_sems.at[k - 1], device_id=(px, py, c), device_id_type=MESH_IDS))
            recvs.append(pltpu.make_async_remote_copy(
                src_ref=t_ref.at[myq], dst_ref=out_ref.at[peer], send_sem=send_sems.at[k - 1],
                recv_sem=recv_sems.at[k - 1], device_id=(px, py, c), device_id_type=MESH_IDS))
        for cp in sends:
            cp.start()
        for cp in recvs:
            cp.wait_recv()
        for cp in sends:
            cp.wait_send()
        local.wait()

    return pl.pallas_call(
        body, name="grad_exchange_chips", out_shape=jax.ShapeDtypeStruct(slabs.shape, slabs.dtype),
        in_specs=[HBM_SPEC], out_specs=HBM_SPEC,
        scratch_shapes=[pltpu.SemaphoreType.DMA((N_CHIPS - 1,)), pltpu.SemaphoreType.DMA((N_CHIPS - 1,)),
                        pltpu.SemaphoreType.DMA],
    )(slabs)


def _all_reduce_small(v):
    r, c_ = v.shape

    def body(v_ref, out_ref, buf, send_sems, recv_sems):
        x, y, c = _my_place()
        me = 4 * x + 2 * y + c
        buf[me] = v_ref[...]
        sends, recvs = [], []
        for k in range(1, N_DEV):
            px, py, pc = _flip(x, k & 4), _flip(y, k & 2), _flip(c, k & 1)
            peer = 4 * px + 2 * py + pc
            sends.append(pltpu.make_async_remote_copy(
                src_ref=v_ref, dst_ref=buf.at[me], send_sem=send_sems.at[k - 1], recv_sem=recv_sems.at[k - 1],
                device_id=(px, py, pc), device_id_type=MESH_IDS))
            recvs.append(pltpu.make_async_remote_copy(
                src_ref=v_ref, dst_ref=buf.at[peer], send_sem=send_sems.at[k - 1], recv_sem=recv_sems.at[k - 1],
                device_id=(px, py, pc), device_id_type=MESH_IDS))
        for cp in sends:
            cp.start()
        for cp in recvs:
            cp.wait_recv()
        for cp in sends:
            cp.wait_send()
        acc = buf[0]
        for d in range(1, N_DEV):
            acc = acc + buf[d]
        out_ref[...] = acc

    vm = pl.BlockSpec(memory_space=pltpu.VMEM)
    return pl.pallas_call(
        body, name="vector_grad_all_reduce", out_shape=jax.ShapeDtypeStruct((r, c_), F32), in_specs=[vm], out_specs=vm,
        scratch_shapes=[pltpu.VMEM((N_DEV, r, c_), F32), pltpu.SemaphoreType.DMA((7,)), pltpu.SemaphoreType.DMA((7,))],
    )(v)


def _adamw_math(w, g, m, v):
    m = ADAM_B1 * m + (1.0 - ADAM_B1) * g
    v = ADAM_B2 * v + (1.0 - ADAM_B2) * (g * g)
    m_hat = m / (1.0 - ADAM_B1 ** ADAM_STEP)
    v_hat = v / (1.0 - ADAM_B2 ** ADAM_STEP)
    delta = -ADAM_LR * (m_hat / (jnp.sqrt(v_hat) + ADAM_EPS) + ADAM_WD * w)
    return delta, m, v


def _sum_and_adamw(parts, w, m, v):
    nparts, r, c_ = parts.shape
    br = _pick(r, (256, 128, 64, 32, 16, 8))

    def body(p_ref, w_ref, m_ref, v_ref, g_ref, d_ref, nm_ref, nv_ref):
        g = p_ref[0].astype(F32)
        for d in range(1, nparts):
            g = g + p_ref[d].astype(F32)
        g_ref[...] = g
        d_ref[...], nm_ref[...], nv_ref[...] = _adamw_math(w_ref[...], g, m_ref[...], v_ref[...])

    row = pl.BlockSpec((br, c_), lambda i: (i, 0))
    return pl.pallas_call(
        body, name="grad_sum_adamw", grid=(r // br,),
        in_specs=[pl.BlockSpec((nparts, br, c_), lambda i: (0, i, 0)), row, row, row], out_specs=(row,) * 4,
        out_shape=(jax.ShapeDtypeStruct((r, c_), F32),) * 4, compiler_params=_params("parallel"),
    )(parts, w, m, v)


def _adamw_small(w, g, m, v):
    vm = pl.BlockSpec(memory_space=pltpu.VMEM)

    def body(w_ref, g_ref, m_ref, v_ref, d_ref, nm_ref, nv_ref):
        d_ref[...], nm_ref[...], nv_ref[...] = _adamw_math(w_ref[...], g_ref[...], m_ref[...], v_ref[...])

    return pl.pallas_call(
        body, name="vector_adamw", in_specs=[vm] * 4, out_specs=(vm,) * 3,
        out_shape=(jax.ShapeDtypeStruct(w.shape, F32),) * 3,
    )(w, g, m, v)


def _pad_rows(t, axis):
    extra = -t.shape[axis] % PART_ROW_ALIGN
    if extra == 0:
        return t
    widths = [(0, 0)] * t.ndim
    widths[axis] = (0, extra)
    return jnp.pad(t, widths)


def _pack_local(named):
    rows = [_pad_rows((named[n].T if kind == "c" else named[n]).reshape(-1, PACK_COLS), 0)
            for n, kind, _, _ in MATRICES]
    rows.append(jnp.zeros((MAT_ROWS - MAT_ROWS_USED, PACK_COLS), rows[0].dtype))
    return jnp.concatenate(rows, axis=0)


def _unpack_local(packed):
    out, r0 = {}, 0
    for n, kind, k, nn in MATRICES:
        nr = k * nn // N_DEV // PACK_COLS
        part = packed[r0:r0 + nr]
        out[n] = part.reshape(nn // N_DEV, k).T if kind == "c" else part.reshape(k // N_DEV, nn)
        r0 += _part_rows(k, nn)
    return out


def _unpack_gathered(g, matrices):
    out, r0 = {}, 0
    for n, kind, k, nn in matrices:
        nr = k * nn // N_DEV // PACK_COLS
        out[n] = g[:, r0:r0 + nr].reshape((nn, k) if kind == "c" else (k, nn))
        r0 += _part_rows(k, nn)
    return out


def _pack_full_grads(grads):
    rows = []
    for n, _, k, nn in MATRICES:
        gmat = grads[n].reshape(N_CHIPS, 2, -1, PACK_COLS).transpose(1, 0, 2, 3)
        rows.append(_pad_rows(gmat, 2))
    rows.append(jnp.zeros((2, N_CHIPS, MAT_ROWS - MAT_ROWS_USED, PACK_COLS), rows[0].dtype))
    return jnp.concatenate(rows, axis=2)


def _pack_vectors(named):
    rows = [jnp.pad(named[n].astype(F32), (0, PACK_COLS - d)) for n, d in VECTORS]
    rows += [jnp.zeros((PACK_COLS,), F32)] * (VEC_ROWS - len(VECTORS))
    return jnp.stack(rows, axis=0)


def _unpack_vectors(packed):
    return {n: packed[i, :d] for i, (n, d) in enumerate(VECTORS)}


def _step(inputs):
    x = inputs["x"][0]
    mem = inputs["mem"][0]
    positions = inputs["positions"][0]
    target = inputs["loss_target"][0]

    local_w = _pack_local({n: inputs[n] for n, _, _, _ in MATRICES})
    local_bf16 = local_w.astype(BF16)
    w0full = _unpack_gathered(_all_gather_rows(local_bf16[:LAYER0_ROWS]), LAYER0)
    vec = {n: inputs[n] for n, _ in VECTORS}

    loss_part, grad_x, gfull, gvec = _local_grads(w0full, local_bf16[LAYER0_ROWS:], vec, x, mem, positions, target)
    loss = lax.psum(loss_part, ("x", "y", "c"))

    slabs = _pack_full_grads(gfull)
    from_sibling = _exchange_with_sibling(slabs)
    mine = lax.dynamic_index_in_dim(slabs, lax.axis_index("c"), axis=0, keepdims=False)
    parts = _exchange_between_chips(_add_pairs(mine, from_sibling))
    local_m = _pack_local({n: inputs["m_" + n] for n, _, _, _ in MATRICES})
    local_v = _pack_local({n: inputs["v_" + n] for n, _, _, _ in MATRICES})
    g_pk, d_pk, m_pk, v_pk = _sum_and_adamw(parts, local_w, local_m, local_v)
    g_mat, d_mat, m_mat, v_mat = (_unpack_local(t) for t in (g_pk, d_pk, m_pk, v_pk))

    g_vec_pk = _all_reduce_small(_pack_vectors(gvec))
    d_vec_pk, m_vec_pk, v_vec_pk = _adamw_small(
        _pack_vectors(vec), g_vec_pk, _pack_vectors({n: inputs["m_" + n] for n, _ in VECTORS}),
        _pack_vectors({n: inputs["v_" + n] for n, _ in VECTORS}))
    g_vec, d_vec, m_vec, v_vec = (_unpack_vectors(t) for t in (g_vec_pk, d_vec_pk, m_vec_pk, v_vec_pk))

    def pick(mats, vecs, n):
        return mats[n] if n in mats else vecs[n]

    outs = [loss, grad_x[None]]
    for mats, vecs in ((g_mat, g_vec), (d_mat, d_vec), (m_mat, m_vec), (v_mat, v_vec)):
        outs += [pick(mats, vecs, n) for n in WEIGHT_ORDER]
    return tuple(outs)


_KIND = {n: kind for n, kind, _, _ in MATRICES}
_VIEW_OF = {"l0_w_uq_heads": "l0_w_uq", "l0_w_uk_heads": "l0_w_ukv", "l0_w_out_swa": "l0_w_out",
            "l0_w_out_mla": "l0_w_out", "l0_w_in_qa": "l0_w_in", "l0_w_in_kva": "l0_w_in", "l0_w_in_cq": "l0_w_in",
            "l0_w_in_ckv": "l0_w_in", "l0_w_in_kr": "l0_w_in", "l1_w_q": "l1_w_qkv", "l1_w_k": "l1_w_qkv",
            "l1_w_v": "l1_w_qkv"}
_IN_PARTS = (("l0_w_in_qa", 0, A_Q), ("l0_w_in_kva", A_Q, A_Q + 2 * A_KV),
             ("l0_w_in_cq", A_Q + 2 * A_KV, A_Q + 2 * A_KV + MLA_Q_RANK),
             ("l0_w_in_ckv", A_Q + 2 * A_KV + MLA_Q_RANK, EVEN_IN - MLA_ROPE_DIM))
_KR_PAD = (MLA_NOPE_DIM, LANES - MLA_NOPE_DIM - MLA_ROPE_DIM)
_MLA_QK = MLA_NOPE_DIM + MLA_ROPE_DIM


def _orient(name):
    return "t" if _KIND[_VIEW_OF.get(name, name)] == "c" else "n"


def _nope_rows():
    return (np.arange(MLA_HEADS * LANES) % LANES < MLA_NOPE_DIM)[:, None]


def _layer1_weights(wfull):
    w = dict(wfull)
    w_qkv = w.pop("l1_w_qkv")
    for i, name in enumerate(("l1_w_q", "l1_w_k", "l1_w_v")):
        w[name] = w_qkv[i * D_MODEL:(i + 1) * D_MODEL]
    return w


def _layer0_weights(wfull):
    w = dict(wfull)
    w_in = w.pop("l0_w_in")
    for name, r0, r1 in _IN_PARTS:
        w[name] = w_in[r0:r1]
    w["l0_w_in_kr"] = jnp.pad(w_in[EVEN_IN - MLA_ROPE_DIM:], (_KR_PAD, (0, 0)))
    uq = w.pop("l0_w_uq").reshape(MLA_HEADS, _MLA_QK, MLA_Q_RANK)
    w["l0_w_uq_heads"] = jnp.pad(uq, ((0, 0), (0, LANES - _MLA_QK), (0, 0))).reshape(MLA_HEADS * LANES, MLA_Q_RANK)
    w["l0_w_uk_heads"] = jnp.where(_nope_rows(), wfull["l0_w_ukv"], jnp.zeros_like(wfull["l0_w_ukv"]))
    wo = w.pop("l0_w_out")
    w["l0_w_out_swa"] = wo[:A_Q]
    w["l0_w_out_mla"] = jnp.pad(wo[A_Q:].reshape(MLA_HEADS, HEAD_DIM, D_MODEL),
                                ((0, 0), (LANES - HEAD_DIM, 0), (0, 0))).reshape(MLA_HEADS * LANES, D_MODEL)
    return w


def _matrix_grads(g):
    out = {n: g[n] for n, _, _, _ in MATRICES if n in g}
    out["l0_w_in"] = jnp.concatenate([g[name] for name, _, _ in _IN_PARTS]
                                     + [g["l0_w_in_kr"][_KR_PAD[0]:_KR_PAD[0] + MLA_ROPE_DIM]], axis=0)
    out["l1_w_qkv"] = jnp.concatenate([g["l1_w_q"], g["l1_w_k"], g["l1_w_v"]], axis=0)
    out["l0_w_uq"] = g["l0_w_uq_heads"].reshape(MLA_HEADS, LANES, MLA_Q_RANK)[:, :_MLA_QK].reshape(-1, MLA_Q_RANK)
    uk = jnp.where(_nope_rows(), g["l0_w_uk_heads"], jnp.zeros_like(g["l0_w_uk_heads"]))
    out["l0_w_ukv"] = (g["l0_w_ukv"].astype(F32) + uk.astype(F32)).astype(g["l0_w_ukv"].dtype)
    out["l0_w_out"] = jnp.concatenate(
        [g["l0_w_out_swa"],
         g["l0_w_out_mla"].reshape(MLA_HEADS, LANES, D_MODEL)[:, LANES - HEAD_DIM:].reshape(-1, D_MODEL)], axis=0)
    return out


def _local_grads(w0full, shard1, vec, x, mem, positions, target):
    w = _layer0_weights(w0full)
    gathered1 = jax.ShapeDtypeStruct((N_DEV,) + shard1.shape, shard1.dtype)
    shapes = {**w, **jax.eval_shape(lambda g: _layer1_weights(_unpack_gathered(g, LAYER1)), gathered1)}
    slots = {n: jnp.zeros(t.shape, GRAD_WIRE_DTYPE) for n, t in shapes.items()}
    tab64 = _rope_tables(positions, HEAD_DIM, 0, HEAD_DIM)
    tab_mla = _rope_tables(positions, MLA_ROPE_DIM, MLA_NOPE_DIM, LANES)
    diff = {"x": x, "slots": slots, "vec": vec}
    consts = {"w": w, "shard1": shard1, "mem": mem, "tab64": tab64, "tab_mla": tab_mla, "target": target}
    loss_part, grads = jax.value_and_grad(_model_loss)(diff, consts)
    return loss_part, grads["x"], _matrix_grads(grads["slots"]), grads["vec"]


_INPUT_NAMES = (("x", "mem", "positions") + WEIGHT_ORDER + ("loss_target",)
                + tuple("m_" + n for n in WEIGHT_ORDER) + tuple("v_" + n for n in WEIGHT_ORDER))


def kernel(*args):
    assert len(args) == len(_INPUT_NAMES)
    return _step(dict(zip(_INPUT_NAMES, args)))
```

```python
import numpy as np
import jax
import jax.numpy as jnp
from jax import lax
from jax.experimental import pallas as pl
from jax.experimental.pallas import tpu as pltpu

F32 = jnp.float32
BF16 = jnp.bfloat16

LANES = 128
VMEM_LIMIT_BYTES = 56 * 1024 * 1024
MM_VMEM_BUDGET = 40 * 1024 * 1024
MM_MIN_FLOP_PER_STEP = 1e9
BAND_UNITS_PER_STEP = 4
CAUSAL_ROW_CHAIN = 128
BAND_CHAINS_PER_BATCH = 4

D_MODEL = 1024
HEAD_DIM = 64
ROPE_THETA = 10000.0
NORM_EPS = 1e-6
BLOCK = 128
SWA_HEADS = 8
SWA_KV_HEADS = 2
SWA_WINDOW = 128
MLA_HEADS = 8
MLA_Q_RANK = 384
MLA_KV_RANK = 256
MLA_NOPE_DIM = 64
MLA_ROPE_DIM = 32
A_Q = SWA_HEADS * HEAD_DIM
A_KV = SWA_KV_HEADS * HEAD_DIM
EVEN_IN = A_Q + 2 * A_KV + MLA_Q_RANK + MLA_KV_RANK + MLA_ROPE_DIM
DIL_PATTERNS = ((128, 1), (512, 4), (2048, 16))
X_HEADS = 4
X_HEAD_DIM = 128

ADAM_LR = 0.001
ADAM_B1 = 0.9
ADAM_B2 = 0.999
ADAM_EPS = 1e-08
ADAM_WD = 0.01
ADAM_STEP = 10

N_DEV = 8
GRAD_WIRE_DTYPE = BF16
NEG_MASK = -1e30
NEG_INIT = -1e20

MATRICES = (
    ("l0_w_in", "c", 1024, 1440), ("l0_w_uq", "c", 384, 768), ("l0_w_ukv", "c", 256, 1024),
    ("l0_w_out", "r", 1024, 1024), ("l0_w_xq", "r", 1024, 512), ("l0_w_xkv", "r", 1024, 1024),
    ("l0_w_xo", "c", 512, 1024), ("l0_w_gate", "c", 1024, 2816), ("l0_w_up", "c", 1024, 2816),
    ("l0_w_down", "r", 2816, 1024),
    ("l1_w_qkv", "c", 1024, 3072), ("l1_w_out", "r", 1024, 1024), ("l1_w_xq", "r", 1024, 512),
    ("l1_w_xkv", "r", 1024, 1024), ("l1_w_xo", "c", 512, 1024), ("l1_w_gate", "c", 1024, 2816),
    ("l1_w_up", "c", 1024, 2816), ("l1_w_down", "r", 2816, 1024),
)
VECTORS = (
    ("l0_mix_norm", 1024), ("l0_sinks", 8), ("l0_q_norm", 384), ("l0_kv_norm", 256), ("l0_x_norm", 1024),
    ("l0_mem_norm", 1024), ("l0_ffn_norm", 1024), ("l1_mix_norm", 1024), ("l1_x_norm", 1024),
    ("l1_mem_norm", 1024), ("l1_ffn_norm", 1024), ("final_norm", 1024),
)
WEIGHT_ORDER = (
    "l0_mix_norm", "l0_w_in", "l0_sinks", "l0_q_norm", "l0_w_uq", "l0_kv_norm", "l0_w_ukv", "l0_w_out", "l0_x_norm",
    "l0_mem_norm", "l0_w_xq", "l0_w_xkv", "l0_w_xo", "l0_ffn_norm", "l0_w_gate", "l0_w_up", "l0_w_down",
    "l1_mix_norm", "l1_w_qkv", "l1_w_out", "l1_x_norm", "l1_mem_norm", "l1_w_xq", "l1_w_xkv", "l1_w_xo",
    "l1_ffn_norm", "l1_w_gate", "l1_w_up", "l1_w_down", "final_norm",
)
PACK_COLS = 1024
PART_ROW_ALIGN = 16


def _part_rows(k, n):
    return -(-(k * n // N_DEV // PACK_COLS) // PART_ROW_ALIGN) * PART_ROW_ALIGN


LAYER0 = tuple(mat for mat in MATRICES if mat[0].startswith("l0_"))
LAYER1 = tuple(mat for mat in MATRICES if mat[0].startswith("l1_"))
assert MATRICES == LAYER0 + LAYER1
LAYER0_ROWS = sum(_part_rows(k, n) for _, _, k, n in LAYER0)
MAT_ROWS_USED = sum(_part_rows(k, n) for _, _, k, n in MATRICES)
MAT_ROWS = -(-MAT_ROWS_USED // 256) * 256
VEC_ROWS = 16


def _pick(n, cands):
    for c in cands:
        if n % c == 0:
            return c
    return n


def _params(*sem):
    return pltpu.CompilerParams(dimension_semantics=sem, vmem_limit_bytes=VMEM_LIMIT_BYTES)


_DIMS = {"nn": (((1,), (0,)), ((), ())), "nt": (((1,), (1,)), ((), ())), "tn": (((0,), (0,)), ((), ()))}


def _rotate_block(xv, av, bmv, bpv, half, transpose):
    if transpose:
        return xv * av + pltpu.roll(xv * bmv, LANES - half, 1) + pltpu.roll(xv * bpv, half, 1)
    return xv * av + pltpu.roll(xv, half, 1) * bmv + pltpu.roll(xv, LANES - half, 1) * bpv


def _rotate_tile(t, tabs, half, transpose):
    av, bmv, bpv = tabs
    blocks = [_rotate_block(t[:, c:c + LANES], av, bmv, bpv, half, transpose) for c in range(0, t.shape[1], LANES)]
    return blocks[0] if len(blocks) == 1 else jnp.concatenate(blocks, axis=1)


def _div128(n, cap):
    d = (min(n, cap) // LANES) * LANES
    while d >= LANES:
        if n % d == 0:
            return d
        d -= LANES
    return n


def _mm_vmem_bytes(bm, bn, bk, nk, sa, sb, so, has_res):
    est = 2 * (bm * bk * sa + bk * bn * sb + bm * bn * so) + bm * bn * 4
    est += bm * bn * 4 if nk > 1 else 0
    est += 2 * bm * bn * 4 if has_res else 0
    est += bm * bk * 2 if sa == 4 else 0
    est += bk * bn * 2 if sb == 4 else 0
    return est


def _mm_tiles(m, n, k, sa, sb, so, has_res, mode):
    bn = _div128(n, 1536)
    kcap = 2048 if mode == "tn" else k
    for bm_cap in ((1408, 2816) if mode == "tn" else (512, 1024, 2048)):
        bm = _div128(m, bm_cap)
        bk = (min(k, kcap) // LANES) * LANES
        while bk > LANES and (k % bk or _mm_vmem_bytes(bm, bn, bk, k // bk, sa, sb, so, has_res) > MM_VMEM_BUDGET):
            bk -= LANES
        if 2 * bm * bn * bk >= MM_MIN_FLOP_PER_STEP or bm == m:
            break
    return bm, bn, bk


def _mm(a, b, mode, name, out_dtype=F32, res=None, rope=None):
    if mode == "nn":
        (m, k), (k2, n) = a.shape, b.shape
    elif mode == "nt":
        (m, k), (n, k2) = a.shape, b.shape
    else:
        (k, m), (k2, n) = a.shape, b.shape
    assert k == k2, (name, a.shape, b.shape)
    has_res = res is not None
    bm, bn, bk = _mm_tiles(m, n, k, a.dtype.itemsize, b.dtype.itemsize, jnp.dtype(out_dtype).itemsize, has_res, mode)
    nk = k // bk
    dims = _DIMS[mode]
    a_spec = pl.BlockSpec((bk, bm), lambda i, j, kk: (kk, i)) if mode == "tn" else pl.BlockSpec((bm, bk), lambda i, j, kk: (i, kk))
    b_spec = pl.BlockSpec((bn, bk), lambda i, j, kk: (j, kk)) if mode == "nt" else pl.BlockSpec((bk, bn), lambda i, j, kk: (kk, j))
    o_spec = pl.BlockSpec((bm, bn), lambda i, j, kk: (i, j))

    n_in = 2 + (1 if has_res else 0) + (3 if rope is not None else 0)

    def body(*refs):
        a_ref, b_ref = refs[0], refs[1]
        r_ref = refs[2] if has_res else None
        o_ref = refs[n_in]
        part = lax.dot_general(a_ref[...].astype(BF16), b_ref[...].astype(BF16), dims, preferred_element_type=F32)

        def finish(r):
            if has_res:
                r = r + r_ref[...]
            if rope is not None:
                r = _rotate_tile(r, tuple(t[...] for t in refs[n_in - 3:n_in]), rope[1], False)
            o_ref[...] = r.astype(out_dtype)

        if nk == 1:
            finish(part)
            return
        acc = refs[-1]
        kk = pl.program_id(2)

        @pl.when(kk == 0)
        def _():
            acc[...] = part

        @pl.when(jnp.logical_and(kk > 0, kk < nk - 1))
        def _():
            acc[...] += part

        @pl.when(kk == nk - 1)
        def _():
            finish(acc[...] + part)

    args = (a, b, res) if has_res else (a, b)
    in_specs = [a_spec, b_spec] + ([o_spec] if has_res else [])
    if rope is not None:
        args = args + tuple(rope[0])
        in_specs = in_specs + [pl.BlockSpec((bm, LANES), lambda i, j, kk: (i, 0))] * 3
    return pl.pallas_call(
        body, name=name, grid=(m // bm, n // bn, nk), in_specs=in_specs, out_specs=o_spec,
        out_shape=jax.ShapeDtypeStruct((m, n), out_dtype),
        scratch_shapes=[pltpu.VMEM((bm, bn), F32)] if nk > 1 else [],
        compiler_params=_params("parallel", "parallel", "arbitrary"),
    )(*args)


def _rms_fwd(x, g, name, out_dtype=BF16):
    s, d = x.shape
    bs = _pick(s, (512, 256, 128))

    def body(x_ref, g_ref, o_ref):
        xv = x_ref[...]
        r = lax.rsqrt(jnp.mean(xv * xv, axis=-1, keepdims=True) + NORM_EPS)
        o_ref[...] = ((xv * r) * g_ref[...]).astype(out_dtype)

    return pl.pallas_call(
        body, name=name, grid=(s // bs,),
        in_specs=[pl.BlockSpec((bs, d), lambda i: (i, 0)), pl.BlockSpec((1, d), lambda i: (0, 0))],
        out_specs=pl.BlockSpec((bs, d), lambda i: (i, 0)), out_shape=jax.ShapeDtypeStruct((s, d), out_dtype),
        compiler_params=_params("parallel"),
    )(x, g.reshape(1, d))


def _rms_bwd(x, g, dy, name, dres=None):
    s, d = x.shape
    bs = _pick(s, (512, 256, 128))
    has_res = dres is not None

    def body(*refs):
        if has_res:
            x_ref, g_ref, dy_ref, r_ref, dx_ref, dg_ref = refs
        else:
            x_ref, g_ref, dy_ref, dx_ref, dg_ref = refs
        i = pl.program_id(0)
        xv = x_ref[...]
        dy = dy_ref[...]
        r = lax.rsqrt(jnp.mean(xv * xv, axis=-1, keepdims=True) + NORM_EPS)
        xh = xv * r
        dxh = dy * g_ref[...]
        dx = r * (dxh - xh * jnp.mean(dxh * xh, axis=-1, keepdims=True))
        if has_res:
            dx = dx + r_ref[...]
        dx_ref[...] = dx

        @pl.when(i == 0)
        def _():
            dg_ref[...] = jnp.zeros_like(dg_ref)

        dg_ref[...] += jnp.sum(dy * xh, axis=0, keepdims=True)

    row = pl.BlockSpec((bs, d), lambda i: (i, 0))
    vec = pl.BlockSpec((1, d), lambda i: (0, 0))
    args = (x, g.reshape(1, d), dy) + ((dres,) if has_res else ())
    dx, dg = pl.pallas_call(
        body, name=name, grid=(s // bs,), in_specs=[row, vec, row] + ([row] if has_res else []),
        out_specs=(row, vec), out_shape=(jax.ShapeDtypeStruct((s, d), F32), jax.ShapeDtypeStruct((1, d), F32)),
        compiler_params=_params("arbitrary"),
    )(*args)
    return dx, dg.reshape(d)


def _rope_tables(positions, dh, offset, period):
    role = np.zeros(LANES, np.int32)
    for base in range(0, LANES, period):
        role[base + offset:base + offset + dh // 2] = 1
        role[base + offset + dh // 2:base + offset + dh] = 2
    inv_freq = ROPE_THETA ** (-jnp.arange(0, dh, 2, dtype=F32) / dh)
    one_period = jnp.concatenate([jnp.zeros((offset,), F32), inv_freq, inv_freq,
                                  jnp.zeros((period - offset - dh,), F32)])
    ang = positions.astype(F32)[:, None] * jnp.tile(one_period, LANES // period)[None, :]
    c, s = jnp.cos(ang), jnp.sin(ang)
    role = role[None, :]
    a = jnp.where(role == 0, 1.0, c).astype(F32)
    bm = jnp.where(role == 2, s, 0.0).astype(F32)
    bp = jnp.where(role == 1, -s, 0.0).astype(F32)
    return a, bm, bp


def _rope_apply(x, tabs, half, transpose, name, shared=None, sum_blocks=False):
    s, w = x.shape
    bs = _pick(s, (512, 256, 128))
    nc = w // LANES
    a, bm, bp = tabs
    has_shared = shared is not None

    def body(*refs):
        x_ref, a_ref, bm_ref, bp_ref = refs[:4]
        o_ref = refs[5] if has_shared else refs[4]
        av, bmv, bpv = a_ref[...], bm_ref[...], bp_ref[...]
        total = None
        for c in range(nc):
            sl = slice(c * LANES, (c + 1) * LANES)
            xv = x_ref[:, sl]
            if has_shared:
                xv = xv + refs[4][...]
            out = _rotate_block(xv, av, bmv, bpv, half, transpose)
            o_ref[:, sl] = out
            total = out if total is None else total + out
        if sum_blocks:
            refs[-1][...] = total

    row = pl.BlockSpec((bs, w), lambda i: (i, 0))
    tab = pl.BlockSpec((bs, LANES), lambda i: (i, 0))
    out_shape = jax.ShapeDtypeStruct((s, w), F32)
    return pl.pallas_call(
        body, name=name, grid=(s // bs,), in_specs=[row, tab, tab, tab] + ([tab] if has_shared else []),
        out_specs=(row, tab) if sum_blocks else row,
        out_shape=(out_shape, jax.ShapeDtypeStruct((s, LANES), F32)) if sum_blocks else out_shape,
        compiler_params=_params("parallel"),
    )(x, a, bm, bp, *((shared,) if has_shared else ()))


def _make_rope(half, name):
    @jax.custom_vjp
    def rope(x, a, bm, bp):
        return _rope_apply(x, (a, bm, bp), half, False, name + "_fwd")

    def fwd(x, a, bm, bp):
        return rope(x, a, bm, bp), (a, bm, bp)

    def bwd(tabs, dy):
        return _rope_apply(dy, tabs, half, True, name + "_bwd"), None, None, None

    rope.defvjp(fwd, bwd)
    return rope


def _make_rope_shared(half, name):
    @jax.custom_vjp
    def rope(x, shared, a, bm, bp):
        return _rope_apply(x, (a, bm, bp), half, False, name + "_fwd", shared=shared)

    def fwd(x, shared, a, bm, bp):
        return rope(x, shared, a, bm, bp), (a, bm, bp)

    def bwd(tabs, dy):
        dx, dshared = _rope_apply(dy, tabs, half, True, name + "_bwd", sum_blocks=True)
        return dx, dshared, None, None, None

    rope.defvjp(fwd, bwd)
    return rope


def _lane_masks():
    lane = lax.broadcasted_iota(jnp.int32, (1, LANES), 1)
    lo = lane < HEAD_DIM
    return [lo, jnp.logical_not(lo)]


def _sel(mask, v):
    return jnp.where(mask, v, jnp.zeros_like(v))


_NT = (((1,), (1,)), ((), ()))
_NN = (((1,), (0,)), ((), ()))
_TN = (((0,), (0,)), ((), ()))
_BNT = (((2,), (2,)), ((0,), (0,)))
_BNN = (((2,), (1,)), ((0,), (0,)))


def _dot(a, b, dims):
    return lax.dot_general(a, b, dims, preferred_element_type=F32)


def _band_masks(max_dist):
    assert BLOCK - 1 <= max_dist <= BLOCK
    r = lax.broadcasted_iota(jnp.int32, (BLOCK, BLOCK), 0)
    c = lax.broadcasted_iota(jnp.int32, (BLOCK, BLOCK), 1)
    return (BLOCK + r - c) <= max_dist, r >= c


def _stack_heads(t):
    return jnp.concatenate([t, t], axis=0)


def _head_terms(lms, a, prod, lv):
    t = jnp.sum(_sel(lms[a], prod), axis=-1, keepdims=True)
    lse = jnp.max(jnp.where(lms[a], lv, -jnp.inf), axis=-1, keepdims=True)
    return t, lse


class _Residue:
    def __init__(self, ref, r, dil):
        self.ref, self.rows = ref, pl.ds(r, BLOCK, stride=dil)

    def __getitem__(self, idx):
        return self.ref[self.rows, idx[1]]

    def __setitem__(self, idx, val):
        self.ref[self.rows, idx[1]] = val


def _residues(refs, dil):
    if dil == 1:
        return [tuple(refs)]
    return [tuple(_Residue(x, r, dil) for x in refs) for r in range(dil)]


def _band_fwd(q, k, v, sinkrow, scale, max_dist, upb, dil, name):
    sq, w = q.shape
    rb = BLOCK * dil
    nq, nub, wb = sq // rb, w // (LANES * upb), LANES * upb
    has_sink = sinkrow is not None

    def body(*refs):
        s_ref = refs[5] if has_sink else None
        lms = _lane_masks()
        mprev, mcur = _band_masks(max_dist)
        mprev = jnp.logical_and(mprev, pl.program_id(1) > 0)
        mask2 = _stack_heads(jnp.concatenate([mprev, mcur], axis=1))
        chains = [(rr, slice(u * LANES, (u + 1) * LANES))
                  for rr in _residues(refs[:5] + refs[-2:], dil) for u in range(upb)]
        for g0 in range(0, len(chains), BAND_CHAINS_PER_BATCH):
            group = chains[g0:g0 + BAND_CHAINS_PER_BATCH]
            qs, kcat, vcat, sks = [], [], [], []
            for (q_ref, kp_ref, kc_ref, vp_ref, vc_ref, _, _), sl in group:
                qv = (q_ref[:, sl] * scale).astype(BF16)
                qs.append(jnp.concatenate([_sel(lms[0], qv), _sel(lms[1], qv)], axis=0))
                kcat.append(jnp.concatenate([kp_ref[:, sl].astype(BF16), kc_ref[:, sl].astype(BF16)], axis=0))
                vcat.append(jnp.concatenate([vp_ref[:, sl].astype(BF16), vc_ref[:, sl].astype(BF16)], axis=0))
                if has_sink:
                    sks.append(s_ref[sl.start // LANES])
            qs, kcat, vcat = jnp.stack(qs), jnp.stack(kcat), jnp.stack(vcat)
            sc = jnp.where(mask2[None], _dot(qs, kcat, _BNT), NEG_MASK)
            m = jnp.max(sc, axis=-1, keepdims=True)
            p = jnp.exp(sc - m)
            l = jnp.sum(p, axis=-1, keepdims=True)
            pv = _dot(p.astype(BF16), vcat, _BNN)
            if has_sink:
                sk2 = jnp.stack(sks)
                m_all = jnp.maximum(m, sk2)
                shrink = jnp.exp(m - m_all)
                l = l * shrink + jnp.exp(sk2 - m_all)
                pv, m = pv * shrink, m_all
            o2 = pv / l
            lse2 = m + jnp.log(l)
            for gi, ((_, _, _, _, _, o_ref, l_ref), sl) in enumerate(group):
                o_ref[:, sl] = jnp.where(lms[0], o2[gi, :BLOCK], o2[gi, BLOCK:])
                l_ref[:, sl] = jnp.where(lms[0], lse2[gi, :BLOCK], lse2[gi, BLOCK:])

    cur = pl.BlockSpec((rb, wb), lambda ub, i: (i, ub))
    prev = pl.BlockSpec((rb, wb), lambda ub, i: (jnp.maximum(i - 1, 0), ub))
    in_specs = [cur, prev, cur, prev, cur]
    in_specs += [pl.BlockSpec((upb, 2 * BLOCK, 1), lambda ub, i: (ub, 0, 0))] if has_sink else []
    args = (q, k, k, v, v) + ((sinkrow,) if has_sink else ())
    return pl.pallas_call(
        body, name=name, grid=(nub, nq), in_specs=in_specs, out_specs=(cur, cur),
        out_shape=(jax.ShapeDtypeStruct((sq, w), F32), jax.ShapeDtypeStruct((sq, w), F32)),
        compiler_params=_params("parallel", "parallel"),
    )(*args)


def _band_dq(q, k, v, o, lse, do, sinkrow, scale, max_dist, upb, dil, name):
    sq, w = q.shape
    rb = BLOCK * dil
    nq, nub, wb = sq // rb, w // (LANES * upb), LANES * upb
    has_sink = sinkrow is not None

    def body(*refs):
        if has_sink:
            s_ref, dq_block, dsink_ref = refs[8], refs[9], refs[10]
        else:
            dq_block = refs[8]
        i = pl.program_id(1)
        lms = _lane_masks()
        mprev, mcur = _band_masks(max_dist)
        mprev = jnp.logical_and(mprev, i > 0)
        mask2 = _stack_heads(jnp.concatenate([mprev, mcur], axis=1))
        if has_sink:
            @pl.when(i == 0)
            def _():
                dsink_ref[...] = jnp.zeros_like(dsink_ref)

        chains = [(rr, slice(u * LANES, (u + 1) * LANES))
                  for rr in _residues(refs[:8] + (dq_block,), dil) for u in range(upb)]
        for g0 in range(0, len(chains), BAND_CHAINS_PER_BATCH):
            group = chains[g0:g0 + BAND_CHAINS_PER_BATCH]
            qs, dos, kcat, vcat, t2, lse2 = [], [], [], [], [], []
            for (q_ref, kp_ref, kc_ref, vp_ref, vc_ref, o_ref, l_ref, do_ref, _), sl in group:
                qv = (q_ref[:, sl] * scale).astype(BF16)
                dov = do_ref[:, sl]
                prod = dov * o_ref[:, sl]
                dob = dov.astype(BF16)
                lv = l_ref[:, sl]
                (t0, lse0), (t1, lse1) = _head_terms(lms, 0, prod, lv), _head_terms(lms, 1, prod, lv)
                t2.append(jnp.concatenate([t0, t1], axis=0))
                lse2.append(jnp.concatenate([lse0, lse1], axis=0))
                qs.append(jnp.concatenate([_sel(lms[0], qv), _sel(lms[1], qv)], axis=0))
                dos.append(jnp.concatenate([_sel(lms[0], dob), _sel(lms[1], dob)], axis=0))
                kcat.append(jnp.concatenate([kp_ref[:, sl].astype(BF16), kc_ref[:, sl].astype(BF16)], axis=0))
                vcat.append(jnp.concatenate([vp_ref[:, sl].astype(BF16), vc_ref[:, sl].astype(BF16)], axis=0))
                if has_sink:
                    rs = -jnp.exp(s_ref[:, sl] - lv) * jnp.where(lms[0], t0, t1)
                    dsink_ref[0:1, sl] += jnp.sum(rs, axis=0, keepdims=True)
            qs, dos, kcat, vcat = jnp.stack(qs), jnp.stack(dos), jnp.stack(kcat), jnp.stack(vcat)
            p = jnp.exp(jnp.where(mask2[None], _dot(qs, kcat, _BNT), NEG_MASK) - jnp.stack(lse2))
            ds = (p * (_dot(dos, vcat, _BNT) - jnp.stack(t2))).astype(BF16)
            dq2 = _dot(ds, kcat, _BNN) * scale
            for gi, ((_, _, _, _, _, _, _, _, dq_ref), sl) in enumerate(group):
                dq_ref[:, sl] = jnp.where(lms[0], dq2[gi, :BLOCK], dq2[gi, BLOCK:])

    cur = pl.BlockSpec((rb, wb), lambda ub, i: (i, ub))
    prev = pl.BlockSpec((rb, wb), lambda ub, i: (jnp.maximum(i - 1, 0), ub))
    in_specs = [cur, prev, cur, prev, cur, cur, cur, cur]
    args = (q, k, k, v, v, o, lse, do)
    out_specs, out_shape = cur, jax.ShapeDtypeStruct((sq, w), F32)
    sem = ("parallel", "parallel")
    if has_sink:
        in_specs = in_specs + [pl.BlockSpec((1, wb), lambda ub, i: (0, ub))]
        args = args + (sinkrow,)
        out_specs = (cur, pl.BlockSpec((8, wb), lambda ub, i: (0, ub)))
        out_shape = (out_shape, jax.ShapeDtypeStruct((8, w), F32))
        sem = ("parallel", "arbitrary")
    return pl.pallas_call(
        body, name=name, grid=(nub, nq), in_specs=in_specs, out_specs=out_specs, out_shape=out_shape,
        compiler_params=_params(*sem),
    )(*args)


def _band_dkv(q, k, v, o, lse, do, scale, max_dist, upb, dil, name):
    sq, w = q.shape
    rb = BLOCK * dil
    nq, nub, wb = sq // rb, w // (LANES * upb), LANES * upb

    def body(*refs):
        kb = pl.program_id(1)
        lms = _lane_masks()
        key = lax.broadcasted_iota(jnp.int32, (BLOCK, BLOCK), 0)
        qry = lax.broadcasted_iota(jnp.int32, (BLOCK, BLOCK), 1)
        msame = qry >= key
        mnext = jnp.logical_and((BLOCK + qry - key) <= max_dist, kb < nq - 1)
        mask4 = jnp.concatenate([msame, msame, mnext, mnext], axis=1)
        chains =[(rr, slice(u * LANES, (u + 1) * LANES)) for rr in _residues(refs, dil) for u in range(upb)]
        for g0 in range(0, len(chains), BAND_CHAINS_PER_BATCH):
            group = chains[g0:g0 + BAND_CHAINS_PER_BATCH]
            kvs, vvs, qss, doss, t4s, lse4s = [], [], [], [], [], []
            for (k_ref, v_ref, qs_ref, qn_ref, os_ref, on_ref, ls_ref, ln_ref, dos_ref, don_ref, _, _), sl in group:
                kvs.append(k_ref[:, sl].astype(BF16))
                vvs.append(v_ref[:, sl].astype(BF16))
                qparts, doparts, tparts, lparts = [], [], [], []
                for q_ref, o_ref, l_ref, do_ref in ((qs_ref, os_ref, ls_ref, dos_ref),
                                                    (qn_ref, on_ref, ln_ref, don_ref)):
                    qv = (q_ref[:, sl] * scale).astype(BF16)
                    dov = do_ref[:, sl]
                    prod_t = (dov * o_ref[:, sl]).T
                    dob = dov.astype(BF16)
                    lse_t = l_ref[:, sl].T
                    for a in range(2):
                        lanes = slice(a * HEAD_DIM, (a + 1) * HEAD_DIM)
                        qparts.append(_sel(lms[a], qv))
                        doparts.append(_sel(lms[a], dob))
                        tparts.append(jnp.sum(prod_t[lanes, :], axis=0, keepdims=True))
                        lparts.append(lse_t[a * HEAD_DIM:a * HEAD_DIM + 1, :])
                qss.append(jnp.concatenate(qparts, axis=0))
                doss.append(jnp.concatenate(doparts, axis=0))
                t4s.append(jnp.concatenate(tparts, axis=1))
                lse4s.append(jnp.concatenate(lparts, axis=1))
            kv, vv, qs, dos = jnp.stack(kvs), jnp.stack(vvs), jnp.stack(qss), jnp.stack(doss)
            p = jnp.exp(jnp.where(mask4[None], _dot(kv, qs, _BNT), NEG_MASK) - jnp.stack(lse4s))
            ds = (p * (_dot(vv, dos, _BNT) - jnp.stack(t4s))).astype(BF16)
            dv = _dot(p.astype(BF16), dos, _BNN)
            dk = _dot(ds, qs, _BNN)
            for gi, (rr, sl) in enumerate(group):
                rr[-1][:, sl] = dv[gi]
                rr[-2][:, sl] = dk[gi]

    same = pl.BlockSpec((rb, wb), lambda ub, kb: (kb, ub))
    nxt = pl.BlockSpec((rb, wb), lambda ub, kb: (jnp.minimum(kb + 1, nq - 1), ub))
    return pl.pallas_call(
        body, name=name, grid=(nub, nq), in_specs=[same, same, same, nxt, same, nxt, same, nxt, same, nxt],
        out_specs=(same, same),
        out_shape=(jax.ShapeDtypeStruct((sq, w), F32), jax.ShapeDtypeStruct((sq, w), F32)),
        compiler_params=_params("parallel", "parallel"),
    )(k, v, q, q, o, o, lse, lse, do, do)


def _make_band_attention(scale, max_dist, upb, name):
    @jax.custom_vjp
    def attn(q, k, v, sinks):
        return _band_fwd(q, k, v, _sink_col(sinks), scale, max_dist, upb, 1, name + "_fwd")[0]

    def fwd(q, k, v, sinks):
        o, lse = _band_fwd(q, k, v, _sink_col(sinks), scale, max_dist, upb, 1, name + "_fwd")
        return o, (q, k, v, o, lse, sinks)

    def bwd(res, do):
        q, k, v, o, lse, sinks = res
        dq, dsink = _band_dq(q, k, v, o, lse, do, _sink_row(sinks), scale, max_dist, upb, 1, name + "_dq")
        dk, dv = _band_dkv(q, k, v, o, lse, do, scale, max_dist, upb, 1, name + "_dkv")
        return dq, dk, dv, dsink[0].reshape(-1, HEAD_DIM)[:, 0]

    attn.defvjp(fwd, bwd)
    return attn


def _triangle(n, by_key):
    if by_key:
        pairs = [(i, kb) for kb in range(n) for i in range(kb, n)]
    else:
        pairs = [(i, j) for i in range(n) for j in range(i + 1)]
    qi = np.asarray([p[0] for p in pairs], np.int32)
    kj = np.asarray([p[1] for p in pairs], np.int32)
    return jnp.asarray(qi), jnp.asarray(kj)


def _gather_copies(shard_ref, out_ref, send_sems, recv_sems, arrivals):
    x, y, c = _my_place()
    me = 4 * x + 2 * y + c
    sends, recvs = [], []
    for k in range(1, N_DEV):
        px, py, pc = _flip(x, k & 4), _flip(y, k & 2), _flip(c, k & 1)
        peer = 4 * px + 2 * py + pc
        for slot, into in ((me, sends),) + (((peer, recvs),) if arrivals else ()):
            into.append(pltpu.make_async_remote_copy(
                src_ref=shard_ref, dst_ref=out_ref.at[slot], send_sem=send_sems.at[k - 1],
                recv_sem=recv_sems.at[k - 1], device_id=(px, py, pc), device_id_type=MESH_IDS))
    return me, sends, recvs


def _causal_fwd(q, k, v, scale, blk, name, shard=None):
    s, w = q.shape
    nq, nub = s // blk, w // LANES
    qi, kj = _triangle(nq, by_key=False)
    nsteps = qi.shape[0]
    gathers = shard is not None

    def body(qi_ref, kj_ref, q_ref, k_ref, v_ref, *rest):
        if gathers:
            shard_ref, o_ref, l_ref, gath_ref, m_sc, l_sc, acc_sc, send_sems, recv_sems, local_sem = rest
        else:
            o_ref, l_ref, m_sc, l_sc, acc_sc = rest
        t = pl.program_id(1)
        i, j = qi_ref[t], kj_ref[t]

        if gathers:
            ub = pl.program_id(0)

            @pl.when(jnp.logical_and(ub == 0, t == 0))
            def _():
                me, sends, _ = _gather_copies(shard_ref, gath_ref, send_sems, recv_sems, arrivals=False)
                pltpu.make_async_copy(shard_ref, gath_ref.at[me], local_sem).start()
                for cp in sends:
                    cp.start()

        @pl.when(j == 0)
        def _():
            m_sc[...] = jnp.full_like(m_sc, NEG_INIT)
            l_sc[...] = jnp.zeros_like(l_sc)
            acc_sc[...] = jnp.zeros_like(acc_sc)

        def step(diagonal):
            kv, vv = k_ref[...].astype(BF16), v_ref[...].astype(BF16)
            chains = range(0, blk, CAUSAL_ROW_CHAIN)
            scs = [_dot((q_ref[c0:c0 + CAUSAL_ROW_CHAIN, :] * scale).astype(BF16), kv, _NT) for c0 in chains]
            m_all, l_all, acc_all = m_sc[...], l_sc[...], acc_sc[...]
            m_out, l_out, acc_out = [], [], []
            for sc, c0 in zip(scs, chains):
                rows = slice(c0, c0 + CAUSAL_ROW_CHAIN)
                if diagonal:
                    r = c0 + lax.broadcasted_iota(jnp.int32, (CAUSAL_ROW_CHAIN, blk), 0)
                    c = lax.broadcasted_iota(jnp.int32, (CAUSAL_ROW_CHAIN, blk), 1)
                    sc = jnp.where(r >= c, sc, NEG_MASK)
                m_prev = m_all[rows]
                m_new = jnp.maximum(m_prev, jnp.max(sc, axis=-1, keepdims=True))
                alpha = jnp.exp(m_prev - m_new)
                p = jnp.exp(sc - m_new)
                l_out.append(alpha * l_all[rows] + jnp.sum(p, axis=-1, keepdims=True))
                m_out.append(m_new)
                acc_out.append(acc_all[rows] * alpha + _dot(p.astype(BF16), vv, _NN))
            m_sc[...] = jnp.concatenate(m_out, axis=0)
            l_sc[...] = jnp.concatenate(l_out, axis=0)
            acc_sc[...] = jnp.concatenate(acc_out, axis=0)

        @pl.when(j < i)
        def _():
            step(False)

        @pl.when(j == i)
        def _():
            step(True)
            lf = l_sc[...]
            o_ref[...] = acc_sc[...] / lf
            l_ref[...] = jnp.broadcast_to(m_sc[...] + jnp.log(lf), (blk, LANES))

        if gathers:
            @pl.when(jnp.logical_and(pl.program_id(0) == nub - 1, t == nsteps - 1))
            def _():
                me, sends, recvs = _gather_copies(shard_ref, gath_ref, send_sems, recv_sems, arrivals=True)
                for cp in recvs:
                    cp.wait_recv()
                for cp in sends:
                    cp.wait_send()
                pltpu.make_async_copy(shard_ref, gath_ref.at[me], local_sem).wait()

    qspec = pl.BlockSpec((blk, LANES), lambda ub, t, qi_ref, kj_ref: (qi_ref[t], ub))
    kspec = pl.BlockSpec((blk, LANES), lambda ub, t, qi_ref, kj_ref: (kj_ref[t], ub))
    in_specs, out_specs = [qspec, kspec, kspec], (qspec, qspec)
    out_shape = (jax.ShapeDtypeStruct((s, w), F32), jax.ShapeDtypeStruct((s, w), F32))
    scratch = [pltpu.VMEM((blk, 1), F32), pltpu.VMEM((blk, 1), F32), pltpu.VMEM((blk, LANES), F32)]
    args = (qi, kj, q, k, v)
    if gathers:
        in_specs, out_specs = in_specs + [HBM_SPEC], out_specs + (HBM_SPEC,)
        out_shape = out_shape + (jax.ShapeDtypeStruct((N_DEV,) + shard.shape, shard.dtype),)
        scratch = scratch + [pltpu.SemaphoreType.DMA((N_DEV - 1,)), pltpu.SemaphoreType.DMA((N_DEV - 1,)),
                             pltpu.SemaphoreType.DMA]
        args = args + (shard,)
    return pl.pallas_call(
        body, name=name,
        grid_spec=pltpu.PrefetchScalarGridSpec(
            num_scalar_prefetch=2, grid=(nub, nsteps), in_specs=in_specs, out_specs=out_specs, scratch_shapes=scratch),
        out_shape=out_shape, compiler_params=_params("arbitrary", "arbitrary"),
    )(*args)


def _causal_bwd(q, k, v, o, lse, do, scale, blk, name, chip_sums):
    s, w = q.shape
    nq, nub = s // blk, w // LANES
    qi, kj = _triangle(nq, by_key=True)
    nsteps = qi.shape[0]

    def body(qi_ref, kj_ref, q_ref, k_ref, v_ref, o_ref, l_ref, do_ref, t_ref, dq_ref, dk_ref, dv_ref, parts_ref,
             dk_acc, dv_acc, send_sems, recv_sems, local_sem):
        t = pl.program_id(1)
        i, kb = qi_ref[t], kj_ref[t]

        @pl.when(jnp.logical_and(pl.program_id(0) == 0, t == 0))
        def _():
            local, sends, _ = _chip_exchange_copies(t_ref, parts_ref, send_sems, recv_sems, local_sem, arrivals=False)
            local.start()
            for cp in sends:
                cp.start()

        @pl.when(t == 0)
        def _():
            dq_ref[...] = jnp.zeros_like(dq_ref)

        @pl.when(i == kb)
        def _():
            dk_acc[...] = jnp.zeros_like(dk_acc)
            dv_acc[...] = jnp.zeros_like(dv_acc)

        def step(diagonal):
            qv = (q_ref[...] * scale).astype(BF16)
            kv, vv = k_ref[...].astype(BF16), v_ref[...].astype(BF16)
            dov = do_ref[...]
            tsum = jnp.sum(dov * o_ref[...], axis=-1, keepdims=True)
            dob = dov.astype(BF16)
            sc = _dot(qv, kv, _NT)
            if diagonal:
                r = lax.broadcasted_iota(jnp.int32, (blk, blk), 0)
                c = lax.broadcasted_iota(jnp.int32, (blk, blk), 1)
                sc = jnp.where(r >= c, sc, NEG_MASK)
            p = jnp.exp(sc - l_ref[:, 0:1])
            ds = (p * (_dot(dob, vv, _NT) - tsum)).astype(BF16)
            dv_acc[...] += _dot(p.astype(BF16), dob, _TN)
            dk_acc[...] += _dot(ds, qv, _TN)
            rows = pl.ds(pl.multiple_of(i * blk, blk), blk)
            dq_ref[rows, :] += _dot(ds, kv, _NN) * scale

        @pl.when(i == kb)
        def _():
            step(True)

        @pl.when(i > kb)
        def _():
            step(False)

        @pl.when(i == nq - 1)
        def _():
            dk_ref[...] = dk_acc[...]
            dv_ref[...] = dv_acc[...]

        @pl.when(jnp.logical_and(pl.program_id(0) == nub - 1, t == nsteps - 1))
        def _():
            local, sends, recvs = _chip_exchange_copies(t_ref, parts_ref, send_sems, recv_sems, local_sem, arrivals=True)
            for cp in recvs:
                cp.wait_recv()
            for cp in sends:
                cp.wait_send()
            local.wait()

    qspec = pl.BlockSpec((blk, LANES), lambda ub, t, qi_ref, kj_ref: (qi_ref[t], ub))
    kspec = pl.BlockSpec((blk, LANES), lambda ub, t, qi_ref, kj_ref: (kj_ref[t], ub))
    whole = pl.BlockSpec((s, LANES), lambda ub, t, qi_ref, kj_ref: (0, ub))
    out = jax.ShapeDtypeStruct((s, w), F32)
    return pl.pallas_call(
        body, name=name,
        grid_spec=pltpu.PrefetchScalarGridSpec(
            num_scalar_prefetch=2, grid=(nub, nsteps), in_specs=[qspec, kspec, kspec, qspec, qspec, qspec, HBM_SPEC],
            out_specs=(whole, kspec, kspec, HBM_SPEC),
            scratch_shapes=[pltpu.VMEM((blk, LANES), F32), pltpu.VMEM((blk, LANES), F32),
                            pltpu.SemaphoreType.DMA((N_CHIPS - 1,)), pltpu.SemaphoreType.DMA((N_CHIPS - 1,)),
                            pltpu.SemaphoreType.DMA]),
        out_shape=(out, out, out, jax.ShapeDtypeStruct(chip_sums.shape, chip_sums.dtype)),
        compiler_params=_params("arbitrary", "arbitrary"),
    )(qi, kj, q, k, v, o, lse, do, chip_sums)


def _make_causal_attention(scale, blk, name, late_shapes, reduce_late):
    def forward(q, k, v, shard):
        o, lse, gathered = _causal_fwd(q, k, v, scale, blk, name + "_fwd", shard=shard)
        late = {n: jnp.zeros(t.shape, GRAD_WIRE_DTYPE) for n, t in late_shapes.items()}
        return (o, gathered, late), (q, k, v, o, lse)

    @jax.custom_vjp
    def attn(q, k, v, shard, parts_slot):
        return forward(q, k, v, shard)[0]

    def fwd(q, k, v, shard, parts_slot):
        return forward(q, k, v, shard)

    def bwd(res, cts):
        q, k, v, o, lse = res
        do, _, late_grads = cts
        dq, dk, dv, parts = _causal_bwd(q, k, v, o, lse, do, scale, blk, name + "_bwd", reduce_late(late_grads))
        return dq, dk, dv, None, parts

    attn.defvjp(fwd, bwd)
    return attn


_BTN = (((1,), (1,)), ((0,), (0,)))


def _heads(ref, scale=None):
    blocks = []
    for c in range(0, ref.shape[1], LANES):
        t = ref[:, c:c + LANES]
        blocks.append((t if scale is None else t * scale).astype(BF16))
    return jnp.stack(blocks)


def _memory_fwd(q, k, v, scale, name):
    s, w = q.shape
    m = k.shape[0]
    bq = _pick(s, (512, 256, 128))

    def body(q_ref, k_ref, v_ref, o_ref, l_ref):
        sc = _dot(_heads(q_ref, scale), _heads(k_ref), _BNT)
        mx = jnp.max(sc, axis=-1, keepdims=True)
        p = jnp.exp(sc - mx)
        l = jnp.sum(p, axis=-1, keepdims=True)
        o = _dot(p.astype(BF16), _heads(v_ref), _BNN) / l
        lse = mx + jnp.log(l)
        for h in range(w // LANES):
            o_ref[:, h * LANES:(h + 1) * LANES] = o[h]
            l_ref[:, h * LANES:(h + 1) * LANES] = jnp.broadcast_to(lse[h], (bq, LANES))

    row = pl.BlockSpec((bq, w), lambda i: (i, 0))
    mem = pl.BlockSpec((m, w), lambda i: (0, 0))
    return pl.pallas_call(
        body, name=name, grid=(s // bq,), in_specs=[row, mem, mem], out_specs=(row, row),
        out_shape=(jax.ShapeDtypeStruct((s, w), F32), jax.ShapeDtypeStruct((s, w), F32)),
        compiler_params=_params("parallel"),
    )(q, k, v)


def _memory_bwd(q, k, v, o, lse, do, scale, name):
    s, w = q.shape
    m = k.shape[0]
    nh = w // LANES
    bq = _pick(s, (512, 256, 128))

    def body(q_ref, k_ref, v_ref, o_ref, l_ref, do_ref, dq_ref, dk_ref, dv_ref):
        qs, ks, vs = _heads(q_ref, scale), _heads(k_ref), _heads(v_ref)
        dos = _heads(do_ref)
        t = jnp.stack([jnp.sum(do_ref[:, h * LANES:(h + 1) * LANES] * o_ref[:, h * LANES:(h + 1) * LANES],
                               axis=-1, keepdims=True) for h in range(nh)])
        lse = jnp.stack([l_ref[:, h * LANES:h * LANES + 1] for h in range(nh)])
        p = jnp.exp(_dot(qs, ks, _BNT) - lse)
        ds = (p * (_dot(dos, vs, _BNT) - t)).astype(BF16)
        dq = _dot(ds, ks, _BNN) * scale
        dk = _dot(ds, qs, _BTN)
        dv = _dot(p.astype(BF16), dos, _BTN)

        @pl.when(pl.program_id(0) == 0)
        def _():
            dk_ref[...] = jnp.zeros_like(dk_ref)
            dv_ref[...] = jnp.zeros_like(dv_ref)

        for h in range(nh):
            sl = slice(h * LANES, (h + 1) * LANES)
            dq_ref[:, sl] = dq[h]
            dk_ref[:, sl] += dk[h]
            dv_ref[:, sl] += dv[h]

    row = pl.BlockSpec((bq, w), lambda i: (i, 0))
    mem = pl.BlockSpec((m, w), lambda i: (0, 0))
    return pl.pallas_call(
        body, name=name, grid=(s // bq,), in_specs=[row, mem, mem, row, row, row], out_specs=(row, mem, mem),
        out_shape=(jax.ShapeDtypeStruct((s, w), F32), jax.ShapeDtypeStruct((m, w), F32),
                   jax.ShapeDtypeStruct((m, w), F32)),
        compiler_params=_params("arbitrary"),
    )(q, k, v, o, lse, do)


def _make_memory_attention(scale, name):
    @jax.custom_vjp
    def attn(q, k, v):
        return _memory_fwd(q, k, v, scale, name + "_fwd")[0]

    def fwd(q, k, v):
        o, lse = _memory_fwd(q, k, v, scale, name + "_fwd")
        return o, (q, k, v, o, lse)

    def bwd(res, do):
        q, k, v, o, lse = res
        return _memory_bwd(q, k, v, o, lse, do, scale, name + "_bwd")

    attn.defvjp(fwd, bwd)
    return attn


def _sink_row(sinks):
    return jnp.repeat(sinks.astype(F32), HEAD_DIM).reshape(1, -1)


def _sink_col(sinks):
    return jnp.repeat(sinks.astype(F32).reshape(-1, 2, 1), BLOCK, axis=1)


def _merge3(os_, ls_, name):
    s, w = os_[0].shape
    bs = _pick(s, (256, 128))

    def body(o1, o2, o3, l1, l2, l3, out_ref, lse_ref):
        a1, a2, a3 = l1[...], l2[...], l3[...]
        m = jnp.maximum(jnp.maximum(a1, a2), a3)
        e1, e2, e3 = jnp.exp(a1 - m), jnp.exp(a2 - m), jnp.exp(a3 - m)
        z = e1 + e2 + e3
        out_ref[...] = (e1 * o1[...] + e2 * o2[...] + e3 * o3[...]) / z
        lse_ref[...] = m + jnp.log(z)

    row = pl.BlockSpec((bs, w), lambda i: (i, 0))
    return pl.pallas_call(
        body, name=name, grid=(s // bs,), in_specs=[row] * 6, out_specs=(row, row),
        out_shape=(jax.ShapeDtypeStruct((s, w), F32), jax.ShapeDtypeStruct((s, w), F32)),
        compiler_params=_params("parallel"),
    )(*os_, *ls_)


def _add3(a, b, c, name):
    s, w = a.shape
    bs = _pick(s, (512, 256, 128))

    def body(a_ref, b_ref, c_ref, o_ref):
        o_ref[...] = (a_ref[...] + b_ref[...]) + c_ref[...]

    row = pl.BlockSpec((bs, w), lambda i: (i, 0))
    return pl.pallas_call(
        body, name=name, grid=(s // bs,), in_specs=[row] * 3, out_specs=row,
        out_shape=jax.ShapeDtypeStruct((s, w), F32), compiler_params=_params("parallel"),
    )(a, b, c)


def _make_dilated(name):
    scale, max_dist = HEAD_DIM ** -0.5, BLOCK

    def upb_of(dil):
        return 2 * BAND_UNITS_PER_STEP if dil == 1 else 1

    def forward(q, k, v):
        os_, ls_ = [], []
        for n, (_, dil) in enumerate(DIL_PATTERNS):
            o, l = _band_fwd(q, k, v, None, scale, max_dist, upb_of(dil), dil, "%s_b%d_fwd" % (name, n))
            os_.append(o)
            ls_.append(l)
        return _merge3(os_, ls_, name + "_merge")

    @jax.custom_vjp
    def dilated(q, k, v):
        return forward(q, k, v)[0]

    def fwd(q, k, v):
        out, lse = forward(q, k, v)
        return out, (q, k, v, out, lse)

    def bwd(res, do):
        q, k, v, out, lse = res
        dqs, dks, dvs = [], [], []
        for n, (_, dil) in enumerate(DIL_PATTERNS):
            args = (q, k, v, out, lse, do)
            dqs.append(_band_dq(*args, None, scale, max_dist, upb_of(dil), dil, "%s_b%d_dq" % (name, n)))
            dk, dv = _band_dkv(*args, scale, max_dist, upb_of(dil), dil, "%s_b%d_dkv" % (name, n))
            dks.append(dk)
            dvs.append(dv)
        return (_add3(*dqs, name + "_dq_sum"), _add3(*dks, name + "_dk_sum"), _add3(*dvs, name + "_dv_sum"))

    dilated.defvjp(fwd, bwd)
    return dilated


def _times_w(a, w, orient, name, res=None, rope=None):
    return _mm(a, w, "nn" if orient == "n" else "nt", name, res=res, rope=rope)


def _times_wt(dz, w, orient, name, res=None):
    return _mm(dz, w, "nt" if orient == "n" else "nn", name, res=res)


def _grad_w(a, dz, orient, name):
    if orient == "n":
        return _mm(a, dz, "tn", name, out_dtype=GRAD_WIRE_DTYPE)
    return _mm(dz, a, "tn", name, out_dtype=GRAD_WIRE_DTYPE)


def _make_norm_linear(name, orients, through=False, rope_halves=None):
    nw = len(orients)
    halves = rope_halves or (None,) * nw

    def rope_of(i, ropes):
        return None if halves[i] is None else (ropes[i], halves[i])

    def forward(x, g, ws, ropes):
        h = _rms_fwd(x, g, name + "_norm")
        zs = tuple(_times_w(h, w, o, "%s_mm%d" % (name, i), rope=rope_of(i, ropes))
                   for i, (w, o) in enumerate(zip(ws, orients)))
        return zs + ((x,) if through else ()), h

    @jax.custom_vjp
    def op(x, g, slots, ws, ropes):
        return forward(x, g, ws, ropes)[0]

    def fwd(x, g, slots, ws, ropes):
        outs, h = forward(x, g, ws, ropes)
        return outs, (x, g, h, ws, ropes)

    def bwd(res, cts):
        x, g, h, ws, ropes = res
        dzs = [cts[i] if halves[i] is None else
               _rope_apply(cts[i], ropes[i], halves[i], True, "%s_unrope%d" % (name, i)) for i in range(nw)]
        dh = None
        for i, (w, o) in enumerate(zip(ws, orients)):
            dh = _times_wt(dzs[i], w, o, "%s_dh%d" % (name, i), res=dh)
        dws = tuple(_grad_w(h, dzs[i], o, "%s_dw%d" % (name, i)) for i, o in enumerate(orients))
        dx, dg = _rms_bwd(x, g, dh, name + "_norm_bwd", dres=cts[nw] if through else None)
        return dx, dg, dws, (None,) * nw, tuple(None if r is None else (None,) * len(r) for r in ropes)

    op.defvjp(fwd, bwd)
    return op


def _make_linear_res(name, orient):
    @jax.custom_vjp
    def op(a, wslot, w, res):
        return _times_w(a, w, orient, name + "_mm", res=res)

    def fwd(a, wslot, w, res):
        return _times_w(a, w, orient, name + "_mm", res=res), (a, w)

    def bwd(saved, dout):
        a, w = saved
        return _times_wt(dout, w, orient, name + "_da"), _grad_w(a, dout, orient, name + "_dw"), None, dout

    op.defvjp(fwd, bwd)
    return op


FFN_TILE_M, FFN_TILE_N = 512, 1408


def _gate_up_act(h, wg, wu, name):
    m, k = h.shape
    n = wg.shape[0]
    bm, bn = _div128(m, FFN_TILE_M), _div128(n, FFN_TILE_N)

    def body(h_ref, wg_ref, wu_ref, g_ref, u_ref, a_ref):
        hv = h_ref[...]
        g = _dot(hv, wg_ref[...], _NT)
        u = _dot(hv, wu_ref[...], _NT)
        g_ref[...] = g
        u_ref[...] = u
        a_ref[...] = (g / (1.0 + jnp.exp(-g)) * u).astype(BF16)

    wspec = pl.BlockSpec((bn, k), lambda i, j: (j, 0))
    ospec = pl.BlockSpec((bm, bn), lambda i, j: (i, j))
    return pl.pallas_call(
        body, name=name, grid=(m // bm, n // bn), in_specs=[pl.BlockSpec((bm, k), lambda i, j: (i, 0)), wspec, wspec],
        out_specs=(ospec, ospec, ospec),
        out_shape=(jax.ShapeDtypeStruct((m, n), F32), jax.ShapeDtypeStruct((m, n), F32),
                   jax.ShapeDtypeStruct((m, n), BF16)),
        compiler_params=_params("parallel", "parallel"),
    )(h, wg, wu)


def _down_bwd_act(dout, wd, gmat, umat, name):
    m, k = dout.shape
    n = wd.shape[0]
    bm, bn = _div128(m, FFN_TILE_M), _div128(n, FFN_TILE_N)

    def body(do_ref, wd_ref, g_ref, u_ref, dg_ref, du_ref):
        d = _dot(do_ref[...].astype(BF16), wd_ref[...], _NT)
        g, u = g_ref[...], u_ref[...]
        sig = 1.0 / (1.0 + jnp.exp(-g))
        dg_ref[...] = (d * u * (sig * (1.0 + g * (1.0 - sig)))).astype(BF16)
        du_ref[...] = (d * (g * sig)).astype(BF16)

    ospec = pl.BlockSpec((bm, bn), lambda i, j: (i, j))
    return pl.pallas_call(
        body, name=name, grid=(m // bm, n // bn),
        in_specs=[pl.BlockSpec((bm, k), lambda i, j: (i, 0)), pl.BlockSpec((bn, k), lambda i, j: (j, 0)), ospec, ospec],
        out_specs=(ospec, ospec), out_shape=(jax.ShapeDtypeStruct((m, n), BF16),) * 2,
        compiler_params=_params("parallel", "parallel"),
    )(dout, wd, gmat, umat)


def _make_ffn(name):
    def forward(x, g, wg, wu, wd):
        h = _rms_fwd(x, g, name + "_norm")
        gmat, umat, a = _gate_up_act(h, wg, wu, name + "_gate_up")
        return _mm(a, wd, "nn", name + "_down", res=x), (x, g, h, gmat, umat, a, wg, wu, wd)

    @jax.custom_vjp
    def op(x, g, wg_slot, wu_slot, wd_slot, wg, wu, wd):
        return forward(x, g, wg, wu, wd)[0]

    def fwd(x, g, wg_slot, wu_slot, wd_slot, wg, wu, wd):
        return forward(x, g, wg, wu, wd)

    def bwd(saved, dout):
        x, g, h, gmat, umat, a, wg, wu, wd = saved
        dgm, dum = _down_bwd_act(dout, wd, gmat, umat, name + "_da_act")
        dwd = _mm(a, dout, "tn", name + "_dwd", out_dtype=GRAD_WIRE_DTYPE)
        dwg = _grad_w(h, dgm, "t", name + "_dwg")
        dwu = _grad_w(h, dum, "t", name + "_dwu")
        dh = _times_wt(dum, wu, "t", name + "_dh_u", res=_times_wt(dgm, wg, "t", name + "_dh_g"))
        dx, dg = _rms_bwd(x, g, dh, name + "_norm_bwd", dres=dout)
        return dx, dg, dwg, dwu, dwd, None, None, None

    op.defvjp(fwd, bwd)
    return op


def _make_final_loss(name):
    def run(x, g, tgt):
        s, d = x.shape
        bs = _pick(s, (512, 256, 128))

        def body(x_ref, g_ref, t_ref, loss_ref, dx_ref, dg_ref):
            i = pl.program_id(0)
            xv = x_ref[...]
            gv = g_ref[...]
            r = lax.rsqrt(jnp.mean(xv * xv, axis=-1, keepdims=True) + NORM_EPS)
            xh = xv * r
            e = xh * gv - t_ref[...]
            dy = e * (1.0 / d)
            dxh = dy * gv
            dx_ref[...] = r * (dxh - xh * jnp.mean(dxh * xh, axis=-1, keepdims=True))
            part = 0.5 * jnp.sum(jnp.sum(e * e, axis=-1, keepdims=True) * (1.0 / d), axis=0, keepdims=True)

            @pl.when(i == 0)
            def _():
                loss_ref[...] = jnp.zeros_like(loss_ref)
                dg_ref[...] = jnp.zeros_like(dg_ref)

            loss_ref[...] += jnp.broadcast_to(part, loss_ref.shape)
            dg_ref[...] += jnp.sum(dy * xh, axis=0, keepdims=True)

        row = pl.BlockSpec((bs, d), lambda i: (i, 0))
        vec = pl.BlockSpec((1, d), lambda i: (0, 0))
        loss, dx, dg = pl.pallas_call(
            body, name=name, grid=(s // bs,), in_specs=[row, vec, row],
            out_specs=(pl.BlockSpec((8, LANES), lambda i: (0, 0)), row, vec),
            out_shape=(jax.ShapeDtypeStruct((8, LANES), F32), jax.ShapeDtypeStruct((s, d), F32),
                       jax.ShapeDtypeStruct((1, d), F32)),
            compiler_params=_params("arbitrary"),
        )(x, g.reshape(1, d), tgt)
        return loss[0, 0], dx, dg.reshape(d)

    @jax.custom_vjp
    def op(x, g, tgt):
        return run(x, g, tgt)[0]

    def fwd(x, g, tgt):
        loss, dx, dg = run(x, g, tgt)
        return loss, (dx, dg)

    def bwd(saved, ct):
        dx, dg = saved
        return dx * ct, dg * ct, None

    op.defvjp(fwd, bwd)
    return op


def _model_loss(diff, consts):
    x = diff["x"]
    w = consts["w"]
    slot = diff["slots"]
    vec = diff["vec"]
    tab64, tab_mla = consts["tab64"], consts["tab_mla"]
    mem = consts["mem"]
    s = x.shape[0]

    rope64 = lambda t, nm: _make_rope(HEAD_DIM // 2, nm)(t, *tab64)

    def nl(nm, inp, gain, wnames, through=False, ropes=None):
        orients = tuple(_orient(n) for n in wnames)
        kinds = ropes or (None,) * len(wnames)
        halves = tuple({None: None, "64": HEAD_DIM // 2, "mla": MLA_ROPE_DIM // 2}[r] for r in kinds)
        tabs = tuple({None: None, "64": tab64, "mla": tab_mla}[r] for r in kinds)
        op = _make_norm_linear(nm, orients, through, halves)
        return op(inp, gain, tuple(slot[n] for n in wnames), tuple(w[n] for n in wnames), tabs)

    def lin_res(nm, a, wname, res):
        return _make_linear_res(nm, _orient(wname))(a, slot[wname], w[wname], res)

    def cross(layer, xin):
        p = "l%d_" % layer
        q, xin = nl(p + "xq", xin, vec[p + "x_norm"], (p + "w_xq",), through=True)
        kv, = nl(p + "xkv", mem, vec[p + "mem_norm"], (p + "w_xkv",))
        half = X_HEADS * X_HEAD_DIM
        o = _make_memory_attention(X_HEAD_DIM ** -0.5, p + "xattn")(q, kv[:, :half], kv[:, half:])
        return lin_res(p + "xo", o, p + "w_xo", xin)

    def ffn(layer, xin):
        p = "l%d_" % layer
        names = (p + "w_gate", p + "w_up", p + "w_down")
        return _make_ffn(p + "ffn")(xin, vec[p + "ffn_norm"], *(slot[n] for n in names), *(w[n] for n in names))

    in_parts = tuple(name for name, _, _ in _IN_PARTS) + ("l0_w_in_kr",)
    qa, kva, cq, ckv, kr_lanes, x = nl("l0_in", x, vec["l0_mix_norm"], in_parts, through=True,
                                       ropes=("64", None, None, None, None))
    ka = rope64(kva[:, :A_KV], "l0_rope_ka")
    va = kva[:, A_KV:]
    rep = SWA_HEADS // SWA_KV_HEADS
    expand = lambda t: jnp.broadcast_to(t.reshape(s, SWA_KV_HEADS, 1, HEAD_DIM),
                                        (s, SWA_KV_HEADS, rep, HEAD_DIM)).reshape(s, A_Q)
    swa = _make_band_attention(HEAD_DIM ** -0.5, SWA_WINDOW - 1, BAND_UNITS_PER_STEP, "l0_swa")
    oa = swa(qa, expand(ka), expand(va), vec["l0_sinks"])

    qfull, = nl("l0_uq", cq, vec["l0_q_norm"], ("l0_w_uq_heads",), ropes=("mla",))
    kvb, knope = nl("l0_ukv", ckv, vec["l0_kv_norm"], ("l0_w_ukv", "l0_w_uk_heads"))
    kfull = _make_rope_shared(MLA_ROPE_DIM // 2, "l0_rope_k")(knope, kr_lanes, *tab_mla)
    def reduce_layer1(late_grads):
        return _chip_sums(_layer1_matrix_grads(late_grads), LAYER1, MAT_ROWS - LAYER0_ROWS, "l1_grads")

    mla = _make_causal_attention((MLA_NOPE_DIM + MLA_ROPE_DIM) ** -0.5, _pick(s, (1024, 512, 256, 128)), "l0_mla",
                                 consts["layer1_shapes"], reduce_layer1)
    ob, gathered1, slots1 = mla(qfull, kfull, kvb, consts["shard1"], diff["parts1_slot"])
    w = {**w, **_layer1_weights(_unpack_gathered(gathered1, LAYER1))}
    slot = {**slot, **slots1}
    x = lin_res("l0_out_a", oa, "l0_w_out_swa", x)
    x = lin_res("l0_out_b", ob, "l0_w_out_mla", x)
    x = cross(0, x)
    x = ffn(0, x)

    q, k, v, x = nl("l1_qkv", x, vec["l1_mix_norm"], ("l1_w_q", "l1_w_k", "l1_w_v"), through=True,
                    ropes=("64", "64", None))
    o = _make_dilated("l1_dil")(q, k, v)
    x = lin_res("l1_out", o, "l1_w_out", x)
    x = cross(1, x)
    x = ffn(1, x)

    return _make_final_loss("final_loss")(x, vec["final_norm"], consts["target"])


MESH_IDS = pl.DeviceIdType.MESH
HBM_SPEC = pl.BlockSpec(memory_space=pltpu.HBM)


def _my_place():
    return lax.axis_index("x"), lax.axis_index("y"), lax.axis_index("c")


def _flip(v, bit):
    return 1 - v if bit else v


def _all_gather_rows(shard):
    r, c_ = shard.shape

    def body(x_ref, out_ref, send_sems, recv_sems, local_sem):
        x, y, c = _my_place()
        me, sibling = (x, y, c), (x, y, 1 - c)
        chips = [(1 - x, y), (x, 1 - y), (1 - x, 1 - y)]

        def slot(px, py, pc):
            return out_ref.at[4 * px + 2 * py + pc]

        def copy(k, block, to, src=None):
            return pltpu.make_async_remote_copy(
                src_ref=slot(*block) if src is None else src, dst_ref=slot(*block), send_sem=send_sems.at[k],
                recv_sem=recv_sems.at[k], device_id=to, device_id_type=MESH_IDS)

        mine = pltpu.make_async_copy(x_ref, slot(*me), local_sem)
        mine.start()
        first = [copy(0, me, sibling, src=x_ref)]
        first += [copy(1 + j, me, (*chip, c), src=x_ref) for j, chip in enumerate(chips)]
        for cp in first:
            cp.start()
        passed = [copy(4 + j, (*chip, c), sibling) for j, chip in enumerate(chips)]
        for j, chip in enumerate(chips):
            copy(1 + j, (*chip, c), me).wait_recv()
            passed[j].start()
        copy(0, sibling, me).wait_recv()
        for j, chip in enumerate(chips):
            copy(4 + j, (*chip, 1 - c), me).wait_recv()
        for cp in first + passed:
            cp.wait_send()
        mine.wait()

    return pl.pallas_call(
        body, name="weights_all_gather", out_shape=jax.ShapeDtypeStruct((N_DEV, r, c_), shard.dtype),
        in_specs=[HBM_SPEC], out_specs=HBM_SPEC,
        scratch_shapes=[pltpu.SemaphoreType.DMA((7,)), pltpu.SemaphoreType.DMA((7,)), pltpu.SemaphoreType.DMA],
    )(shard)


N_CHIPS = 4


def _exchange_with_sibling(slabs, name):
    _, nq, r, c_ = slabs.shape

    def body(p_ref, out_ref, send_sem, recv_sem):
        x, y, c = _my_place()
        cp = pltpu.make_async_remote_copy(
            src_ref=p_ref.at[1 - c], dst_ref=out_ref, send_sem=send_sem, recv_sem=recv_sem,
            device_id=(x, y, 1 - c), device_id_type=MESH_IDS)
        cp.start()
        cp.wait_recv()
        cp.wait_send()

    return pl.pallas_call(
        body, name=name, out_shape=jax.ShapeDtypeStruct((nq, r, c_), slabs.dtype),
        in_specs=[HBM_SPEC], out_specs=HBM_SPEC,
        scratch_shapes=[pltpu.SemaphoreType.DMA, pltpu.SemaphoreType.DMA],
    )(slabs)


def _add_pairs(a, b, name):
    nq, r, c_ = a.shape
    br = _pick(r, (256, 128, 64, 32, 16, 8))

    def body(a_ref, b_ref, o_ref):
        o_ref[...] = (a_ref[...].astype(F32) + b_ref[...].astype(F32)).astype(o_ref.dtype)

    blk = pl.BlockSpec((1, br, c_), lambda q, i: (q, i, 0))
    return pl.pallas_call(
        body, name=name, grid=(nq, r // br), in_specs=[blk, blk], out_specs=blk,
        out_shape=jax.ShapeDtypeStruct(a.shape, a.dtype), compiler_params=_params("parallel", "parallel"),
    )(a, b)


def _chip_exchange_copies(t_ref, out_ref, send_sems, recv_sems, local_sem, arrivals):
    x, y, c = _my_place()
    myq = 2 * x + y
    local = pltpu.make_async_copy(t_ref.at[myq], out_ref.at[myq], local_sem)
    sends, recvs = [], []
    for k in range(1, N_CHIPS):
        px, py = _flip(x, k & 2), _flip(y, k & 1)
        peer = 2 * px + py
        for src, dst, into in ((peer, myq, sends),) + (((myq, peer, recvs),) if arrivals else ()):
            into.append(pltpu.make_async_remote_copy(
                src_ref=t_ref.at[src], dst_ref=out_ref.at[dst], send_sem=send_sems.at[k - 1],
                recv_sem=recv_sems.at[k - 1], device_id=(px, py, c), device_id_type=MESH_IDS))
    return local, sends, recvs


def _exchange_between_chips(slabs):
    def body(t_ref, out_ref, send_sems, recv_sems, local_sem):
        local, sends, recvs = _chip_exchange_copies(t_ref, out_ref, send_sems, recv_sems, local_sem, arrivals=True)
        local.start()
        for cp in sends:
            cp.start()
        for cp in recvs:
            cp.wait_recv()
        for cp in sends:
            cp.wait_send()
        local.wait()

    return pl.pallas_call(
        body, name="grad_exchange_chips", out_shape=jax.ShapeDtypeStruct(slabs.shape, slabs.dtype),
        in_specs=[HBM_SPEC], out_specs=HBM_SPEC,
        scratch_shapes=[pltpu.SemaphoreType.DMA((N_CHIPS - 1,)), pltpu.SemaphoreType.DMA((N_CHIPS - 1,)),
                        pltpu.SemaphoreType.DMA],
    )(slabs)


def _all_reduce_small(v):
    r, c_ = v.shape

    def body(v_ref, out_ref, buf, send_sems, recv_sems):
        x, y, c = _my_place()
        me = 4 * x + 2 * y + c
        buf[me] = v_ref[...]
        sends, recvs = [], []
        for k in range(1, N_DEV):
            px, py, pc = _flip(x, k & 4), _flip(y, k & 2), _flip(c, k & 1)
            peer = 4 * px + 2 * py + pc
            sends.append(pltpu.make_async_remote_copy(
                src_ref=v_ref, dst_ref=buf.at[me], send_sem=send_sems.at[k - 1], recv_sem=recv_sems.at[k - 1],
                device_id=(px, py, pc), device_id_type=MESH_IDS))
            recvs.append(pltpu.make_async_remote_copy(
                src_ref=v_ref, dst_ref=buf.at[peer], send_sem=send_sems.at[k - 1], recv_sem=recv_sems.at[k - 1],
                device_id=(px, py, pc), device_id_type=MESH_IDS))
        for cp in sends:
            cp.start()
        for cp in recvs:
            cp.wait_recv()
        for cp in sends:
            cp.wait_send()
        acc = buf[0]
        for d in range(1, N_DEV):
            acc = acc + buf[d]
        out_ref[...] = acc

    vm = pl.BlockSpec(memory_space=pltpu.VMEM)
    return pl.pallas_call(
        body, name="vector_grad_all_reduce", out_shape=jax.ShapeDtypeStruct((r, c_), F32), in_specs=[vm], out_specs=vm,
        scratch_shapes=[pltpu.VMEM((N_DEV, r, c_), F32), pltpu.SemaphoreType.DMA((7,)), pltpu.SemaphoreType.DMA((7,))],
    )(v)


def _adamw_math(w, g, m, v):
    m = ADAM_B1 * m + (1.0 - ADAM_B1) * g
    v = ADAM_B2 * v + (1.0 - ADAM_B2) * (g * g)
    m_hat = m / (1.0 - ADAM_B1 ** ADAM_STEP)
    v_hat = v / (1.0 - ADAM_B2 ** ADAM_STEP)
    delta = -ADAM_LR * (m_hat / (jnp.sqrt(v_hat) + ADAM_EPS) + ADAM_WD * w)
    return delta, m, v


def _sum_and_adamw(parts, w, m, v):
    nparts, r, c_ = parts.shape
    br = _pick(r, (256, 128, 64, 32, 16, 8))

    def body(p_ref, w_ref, m_ref, v_ref, g_ref, d_ref, nm_ref, nv_ref):
        g = p_ref[0].astype(F32)
        for d in range(1, nparts):
            g = g + p_ref[d].astype(F32)
        g_ref[...] = g
        d_ref[...], nm_ref[...], nv_ref[...] = _adamw_math(w_ref[...], g, m_ref[...], v_ref[...])

    row = pl.BlockSpec((br, c_), lambda i: (i, 0))
    return pl.pallas_call(
        body, name="grad_sum_adamw", grid=(r // br,),
        in_specs=[pl.BlockSpec((nparts, br, c_), lambda i: (0, i, 0)), row, row, row], out_specs=(row,) * 4,
        out_shape=(jax.ShapeDtypeStruct((r, c_), F32),) * 4, compiler_params=_params("parallel"),
    )(parts, w, m, v)


def _adamw_small(w, g, m, v):
    vm = pl.BlockSpec(memory_space=pltpu.VMEM)

    def body(w_ref, g_ref, m_ref, v_ref, d_ref, nm_ref, nv_ref):
        d_ref[...], nm_ref[...], nv_ref[...] = _adamw_math(w_ref[...], g_ref[...], m_ref[...], v_ref[...])

    return pl.pallas_call(
        body, name="vector_adamw", in_specs=[vm] * 4, out_specs=(vm,) * 3,
        out_shape=(jax.ShapeDtypeStruct(w.shape, F32),) * 3,
    )(w, g, m, v)


def _pad_rows(t, axis):
    extra = -t.shape[axis] % PART_ROW_ALIGN
    if extra == 0:
        return t
    widths = [(0, 0)] * t.ndim
    widths[axis] = (0, extra)
    return jnp.pad(t, widths)


def _pack_local(named):
    rows = [_pad_rows((named[n].T if kind == "c" else named[n]).reshape(-1, PACK_COLS), 0)
            for n, kind, _, _ in MATRICES]
    rows.append(jnp.zeros((MAT_ROWS - MAT_ROWS_USED, PACK_COLS), rows[0].dtype))
    return jnp.concatenate(rows, axis=0)


def _unpack_local(packed):
    out, r0 = {}, 0
    for n, kind, k, nn in MATRICES:
        nr = k * nn // N_DEV // PACK_COLS
        part = packed[r0:r0 + nr]
        out[n] = part.reshape(nn // N_DEV, k).T if kind == "c" else part.reshape(k // N_DEV, nn)
        r0 += _part_rows(k, nn)
    return out


def _unpack_gathered(g, matrices):
    out, r0 = {}, 0
    for n, kind, k, nn in matrices:
        nr = k * nn // N_DEV // PACK_COLS
        out[n] = g[:, r0:r0 + nr].reshape((nn, k) if kind == "c" else (k, nn))
        r0 += _part_rows(k, nn)
    return out


def _chip_sums(grads, matrices, rows, name):
    slabs = _pack_full_grads(grads, matrices, rows)
    from_sibling = _exchange_with_sibling(slabs, name + "_exchange_sibling")
    mine = lax.dynamic_index_in_dim(slabs, lax.axis_index("c"), axis=0, keepdims=False)
    return _add_pairs(mine, from_sibling, name + "_chip_sum")


def _pack_full_grads(grads, matrices, rows_total):
    rows = []
    for n, _, k, nn in matrices:
        gmat = grads[n].reshape(N_CHIPS, 2, -1, PACK_COLS).transpose(1, 0, 2, 3)
        rows.append(_pad_rows(gmat, 2))
    used = sum(_part_rows(k, nn) for _, _, k, nn in matrices)
    if rows_total > used:
        rows.append(jnp.zeros((2, N_CHIPS, rows_total - used, PACK_COLS), rows[0].dtype))
    return jnp.concatenate(rows, axis=2)


def _pack_vectors(named):
    rows = [jnp.pad(named[n].astype(F32), (0, PACK_COLS - d)) for n, d in VECTORS]
    rows += [jnp.zeros((PACK_COLS,), F32)] * (VEC_ROWS - len(VECTORS))
    return jnp.stack(rows, axis=0)


def _unpack_vectors(packed):
    return {n: packed[i, :d] for i, (n, d) in enumerate(VECTORS)}


def _step(inputs):
    x = inputs["x"][0]
    mem = inputs["mem"][0]
    positions = inputs["positions"][0]
    target = inputs["loss_target"][0]

    local_w = _pack_local({n: inputs[n] for n, _, _, _ in MATRICES})
    local_bf16 = local_w.astype(BF16)
    w0full = _unpack_gathered(_all_gather_rows(local_bf16[:LAYER0_ROWS]), LAYER0)
    vec = {n: inputs[n] for n, _ in VECTORS}

    loss_part, grad_x, g0full, parts1, gvec = _local_grads(w0full, local_bf16[LAYER0_ROWS:], vec, x, mem, positions,
                                                          target)
    loss = lax.psum(loss_part, ("x", "y", "c"))

    parts0 = _exchange_between_chips(_chip_sums(g0full, LAYER0, LAYER0_ROWS, "l0_grads"))
    parts = jnp.concatenate([parts0, parts1], axis=1)
    local_m = _pack_local({n: inputs["m_" + n] for n, _, _, _ in MATRICES})
    local_v = _pack_local({n: inputs["v_" + n] for n, _, _, _ in MATRICES})
    g_pk, d_pk, m_pk, v_pk = _sum_and_adamw(parts, local_w, local_m, local_v)
    g_mat, d_mat, m_mat, v_mat = (_unpack_local(t) for t in (g_pk, d_pk, m_pk, v_pk))

    g_vec_pk = _all_reduce_small(_pack_vectors(gvec))
    d_vec_pk, m_vec_pk, v_vec_pk = _adamw_small(
        _pack_vectors(vec), g_vec_pk, _pack_vectors({n: inputs["m_" + n] for n, _ in VECTORS}),
        _pack_vectors({n: inputs["v_" + n] for n, _ in VECTORS}))
    g_vec, d_vec, m_vec, v_vec = (_unpack_vectors(t) for t in (g_vec_pk, d_vec_pk, m_vec_pk, v_vec_pk))

    def pick(mats, vecs, n):
        return mats[n] if n in mats else vecs[n]

    outs = [loss, grad_x[None]]
    for mats, vecs in ((g_mat, g_vec), (d_mat, d_vec), (m_mat, m_vec), (v_mat, v_vec)):
        outs += [pick(mats, vecs, n) for n in WEIGHT_ORDER]
    return tuple(outs)


_KIND = {n: kind for n, kind, _, _ in MATRICES}
_VIEW_OF = {"l0_w_uq_heads": "l0_w_uq", "l0_w_uk_heads": "l0_w_ukv", "l0_w_out_swa": "l0_w_out",
            "l0_w_out_mla": "l0_w_out", "l0_w_in_qa": "l0_w_in", "l0_w_in_kva": "l0_w_in", "l0_w_in_cq": "l0_w_in",
            "l0_w_in_ckv": "l0_w_in", "l0_w_in_kr": "l0_w_in", "l1_w_q": "l1_w_qkv", "l1_w_k": "l1_w_qkv",
            "l1_w_v": "l1_w_qkv"}
_IN_PARTS = (("l0_w_in_qa", 0, A_Q), ("l0_w_in_kva", A_Q, A_Q + 2 * A_KV),
             ("l0_w_in_cq", A_Q + 2 * A_KV, A_Q + 2 * A_KV + MLA_Q_RANK),
             ("l0_w_in_ckv", A_Q + 2 * A_KV + MLA_Q_RANK, EVEN_IN - MLA_ROPE_DIM))
_KR_PAD = (MLA_NOPE_DIM, LANES - MLA_NOPE_DIM - MLA_ROPE_DIM)
_MLA_QK = MLA_NOPE_DIM + MLA_ROPE_DIM


def _orient(name):
    return "t" if _KIND[_VIEW_OF.get(name, name)] == "c" else "n"


def _nope_rows():
    return (np.arange(MLA_HEADS * LANES) % LANES < MLA_NOPE_DIM)[:, None]


def _layer1_weights(wfull):
    w = dict(wfull)
    w_qkv = w.pop("l1_w_qkv")
    for i, name in enumerate(("l1_w_q", "l1_w_k", "l1_w_v")):
        w[name] = w_qkv[i * D_MODEL:(i + 1) * D_MODEL]
    return w


def _layer0_weights(wfull):
    w = dict(wfull)
    w_in = w.pop("l0_w_in")
    for name, r0, r1 in _IN_PARTS:
        w[name] = w_in[r0:r1]
    w["l0_w_in_kr"] = jnp.pad(w_in[EVEN_IN - MLA_ROPE_DIM:], (_KR_PAD, (0, 0)))
    uq = w.pop("l0_w_uq").reshape(MLA_HEADS, _MLA_QK, MLA_Q_RANK)
    w["l0_w_uq_heads"] = jnp.pad(uq, ((0, 0), (0, LANES - _MLA_QK), (0, 0))).reshape(MLA_HEADS * LANES, MLA_Q_RANK)
    w["l0_w_uk_heads"] = jnp.where(_nope_rows(), wfull["l0_w_ukv"], jnp.zeros_like(wfull["l0_w_ukv"]))
    wo = w.pop("l0_w_out")
    w["l0_w_out_swa"] = wo[:A_Q]
    w["l0_w_out_mla"] = jnp.pad(wo[A_Q:].reshape(MLA_HEADS, HEAD_DIM, D_MODEL),
                                ((0, 0), (LANES - HEAD_DIM, 0), (0, 0))).reshape(MLA_HEADS * LANES, D_MODEL)
    return w


def _layer1_matrix_grads(g):
    out = {n: g[n] for n, _, _, _ in LAYER1 if n in g}
    out["l1_w_qkv"] = jnp.concatenate([g["l1_w_q"], g["l1_w_k"], g["l1_w_v"]], axis=0)
    return out


def _layer0_matrix_grads(g):
    out = {n: g[n] for n, _, _, _ in LAYER0 if n in g}
    out["l0_w_in"] = jnp.concatenate([g[name] for name, _, _ in _IN_PARTS]
                                     + [g["l0_w_in_kr"][_KR_PAD[0]:_KR_PAD[0] + MLA_ROPE_DIM]], axis=0)
    out["l0_w_uq"] = g["l0_w_uq_heads"].reshape(MLA_HEADS, LANES, MLA_Q_RANK)[:, :_MLA_QK].reshape(-1, MLA_Q_RANK)
    uk = jnp.where(_nope_rows(), g["l0_w_uk_heads"], jnp.zeros_like(g["l0_w_uk_heads"]))
    out["l0_w_ukv"] = (g["l0_w_ukv"].astype(F32) + uk.astype(F32)).astype(g["l0_w_ukv"].dtype)
    out["l0_w_out"] = jnp.concatenate(
        [g["l0_w_out_swa"],
         g["l0_w_out_mla"].reshape(MLA_HEADS, LANES, D_MODEL)[:, LANES - HEAD_DIM:].reshape(-1, D_MODEL)], axis=0)
    return out


def _local_grads(w0full, shard1, vec, x, mem, positions, target):
    w = _layer0_weights(w0full)
    gathered1 = jax.ShapeDtypeStruct((N_DEV,) + shard1.shape, shard1.dtype)
    layer1_shapes = jax.eval_shape(lambda g: _layer1_weights(_unpack_gathered(g, LAYER1)), gathered1)
    slots = {n: jnp.zeros(t.shape, GRAD_WIRE_DTYPE) for n, t in w.items()}
    tab64 = _rope_tables(positions, HEAD_DIM, 0, HEAD_DIM)
    tab_mla = _rope_tables(positions, MLA_ROPE_DIM, MLA_NOPE_DIM, LANES)
    parts1_slot = jnp.zeros((N_CHIPS,) + shard1.shape, GRAD_WIRE_DTYPE)
    diff = {"x": x, "slots": slots, "vec": vec, "parts1_slot": parts1_slot}
    consts = {"w": w, "shard1": shard1, "mem": mem, "tab64": tab64, "tab_mla": tab_mla, "target": target,
              "layer1_shapes": layer1_shapes}
    loss_part, grads = jax.value_and_grad(lambda d: _model_loss(d, consts))(diff)
    return loss_part, grads["x"], _layer0_matrix_grads(grads["slots"]), grads["parts1_slot"], grads["vec"]


_INPUT_NAMES = (("x", "mem", "positions") + WEIGHT_ORDER + ("loss_target",)
                + tuple("m_" + n for n in WEIGHT_ORDER) + tuple("v_" + n for n in WEIGHT_ORDER))


def kernel(*args):
    assert len(args) == len(_INPUT_NAMES)
    return _step(dict(zip(_INPUT_NAMES, args)))
```

```python
import numpy as np
import jax
import jax.numpy as jnp
from jax import lax
from jax.experimental import pallas as pl
from jax.experimental.pallas import tpu as pltpu

F32 = jnp.float32
BF16 = jnp.bfloat16

LANES = 128
VMEM_LIMIT_BYTES = 56 * 1024 * 1024
MM_VMEM_BUDGET = 40 * 1024 * 1024
MM_MIN_FLOP_PER_STEP = 1e9
BAND_UNITS_PER_STEP = 4
CAUSAL_ROW_CHAIN = 128
BAND_CHAINS_PER_BATCH = 4

D_MODEL = 1024
HEAD_DIM = 64
ROPE_THETA = 10000.0
NORM_EPS = 1e-6
BLOCK = 128
SWA_HEADS = 8
SWA_KV_HEADS = 2
SWA_WINDOW = 128
MLA_HEADS = 8
MLA_Q_RANK = 384
MLA_KV_RANK = 256
MLA_NOPE_DIM = 64
MLA_ROPE_DIM = 32
A_Q = SWA_HEADS * HEAD_DIM
A_KV = SWA_KV_HEADS * HEAD_DIM
EVEN_IN = A_Q + 2 * A_KV + MLA_Q_RANK + MLA_KV_RANK + MLA_ROPE_DIM
DIL_PATTERNS = ((128, 1), (512, 4), (2048, 16))
X_HEADS = 4
X_HEAD_DIM = 128

ADAM_LR = 0.001
ADAM_B1 = 0.9
ADAM_B2 = 0.999
ADAM_EPS = 1e-08
ADAM_WD = 0.01
ADAM_STEP = 10

N_DEV = 8
GRAD_WIRE_DTYPE = BF16
NEG_MASK = -1e30
NEG_INIT = -1e20

MATRICES = (
    ("l0_w_in", "c", 1024, 1440), ("l0_w_uq", "c", 384, 768), ("l0_w_ukv", "c", 256, 1024),
    ("l0_w_out", "r", 1024, 1024), ("l0_w_xq", "r", 1024, 512), ("l0_w_xkv", "r", 1024, 1024),
    ("l0_w_xo", "c", 512, 1024), ("l0_w_gate", "c", 1024, 2816), ("l0_w_up", "c", 1024, 2816),
    ("l0_w_down", "r", 2816, 1024),
    ("l1_w_qkv", "c", 1024, 3072), ("l1_w_out", "r", 1024, 1024), ("l1_w_xq", "r", 1024, 512),
    ("l1_w_xkv", "r", 1024, 1024), ("l1_w_xo", "c", 512, 1024), ("l1_w_gate", "c", 1024, 2816),
    ("l1_w_up", "c", 1024, 2816), ("l1_w_down", "r", 2816, 1024),
)
VECTORS = (
    ("l0_mix_norm", 1024), ("l0_sinks", 8), ("l0_q_norm", 384), ("l0_kv_norm", 256), ("l0_x_norm", 1024),
    ("l0_mem_norm", 1024), ("l0_ffn_norm", 1024), ("l1_mix_norm", 1024), ("l1_x_norm", 1024),
    ("l1_mem_norm", 1024), ("l1_ffn_norm", 1024), ("final_norm", 1024),
)
WEIGHT_ORDER = (
    "l0_mix_norm", "l0_w_in", "l0_sinks", "l0_q_norm", "l0_w_uq", "l0_kv_norm", "l0_w_ukv", "l0_w_out", "l0_x_norm",
    "l0_mem_norm", "l0_w_xq", "l0_w_xkv", "l0_w_xo", "l0_ffn_norm", "l0_w_gate", "l0_w_up", "l0_w_down",
    "l1_mix_norm", "l1_w_qkv", "l1_w_out", "l1_x_norm", "l1_mem_norm", "l1_w_xq", "l1_w_xkv", "l1_w_xo",
    "l1_ffn_norm", "l1_w_gate", "l1_w_up", "l1_w_down", "final_norm",
)
PACK_COLS = 1024
PART_ROW_ALIGN = 16
ADD_PAIRS_MAX_ROWS = 2048


def _part_rows(k, n):
    return -(-(k * n // N_DEV // PACK_COLS) // PART_ROW_ALIGN) * PART_ROW_ALIGN


LAYER0 = tuple(mat for mat in MATRICES if mat[0].startswith("l0_"))
LAYER1 = tuple(mat for mat in MATRICES if mat[0].startswith("l1_"))
assert MATRICES == LAYER0 + LAYER1
LAYER0_ROWS = sum(_part_rows(k, n) for _, _, k, n in LAYER0)
MAT_ROWS_USED = sum(_part_rows(k, n) for _, _, k, n in MATRICES)
MAT_ROWS = -(-MAT_ROWS_USED // 256) * 256
VEC_ROWS = 16


def _pick(n, cands):
    for c in cands:
        if n % c == 0:
            return c
    return n


def _params(*sem):
    return pltpu.CompilerParams(dimension_semantics=sem, vmem_limit_bytes=VMEM_LIMIT_BYTES)


_DIMS = {"nn": (((1,), (0,)), ((), ())), "nt": (((1,), (1,)), ((), ())), "tn": (((0,), (0,)), ((), ()))}


def _rotate_block(xv, av, bmv, bpv, half, transpose):
    if transpose:
        return xv * av + pltpu.roll(xv * bmv, LANES - half, 1) + pltpu.roll(xv * bpv, half, 1)
    return xv * av + pltpu.roll(xv, half, 1) * bmv + pltpu.roll(xv, LANES - half, 1) * bpv


def _rotate_tile(t, tabs, half, transpose):
    av, bmv, bpv = tabs
    blocks = [_rotate_block(t[:, c:c + LANES], av, bmv, bpv, half, transpose) for c in range(0, t.shape[1], LANES)]
    return blocks[0] if len(blocks) == 1 else jnp.concatenate(blocks, axis=1)


def _div128(n, cap):
    d = (min(n, cap) // LANES) * LANES
    while d >= LANES:
        if n % d == 0:
            return d
        d -= LANES
    return n


def _mm_vmem_bytes(bm, bn, bk, nk, sa, sb, so, has_res):
    est = 2 * (bm * bk * sa + bk * bn * sb + bm * bn * so) + bm * bn * 4
    est += bm * bn * 4 if nk > 1 else 0
    est += 2 * bm * bn * 4 if has_res else 0
    est += bm * bk * 2 if sa == 4 else 0
    est += bk * bn * 2 if sb == 4 else 0
    return est


def _mm_tiles(m, n, k, sa, sb, so, has_res, mode):
    bn = _div128(n, 1536)
    kcap = 2048 if mode == "tn" else k
    for bm_cap in ((1408, 2816) if mode == "tn" else (512, 1024, 2048)):
        bm = _div128(m, bm_cap)
        bk = (min(k, kcap) // LANES) * LANES
        while bk > LANES and (k % bk or _mm_vmem_bytes(bm, bn, bk, k // bk, sa, sb, so, has_res) > MM_VMEM_BUDGET):
            bk -= LANES
        if 2 * bm * bn * bk >= MM_MIN_FLOP_PER_STEP or bm == m:
            break
    return bm, bn, bk


def _mm(a, b, mode, name, out_dtype=F32, res=None, rope=None):
    if mode == "nn":
        (m, k), (k2, n) = a.shape, b.shape
    elif mode == "nt":
        (m, k), (n, k2) = a.shape, b.shape
    else:
        (k, m), (k2, n) = a.shape, b.shape
    assert k == k2, (name, a.shape, b.shape)
    has_res = res is not None
    bm, bn, bk = _mm_tiles(m, n, k, a.dtype.itemsize, b.dtype.itemsize, jnp.dtype(out_dtype).itemsize, has_res, mode)
    nk = k // bk
    dims = _DIMS[mode]
    a_spec = pl.BlockSpec((bk, bm), lambda i, j, kk: (kk, i)) if mode == "tn" else pl.BlockSpec((bm, bk), lambda i, j, kk: (i, kk))
    b_spec = pl.BlockSpec((bn, bk), lambda i, j, kk: (j, kk)) if mode == "nt" else pl.BlockSpec((bk, bn), lambda i, j, kk: (kk, j))
    o_spec = pl.BlockSpec((bm, bn), lambda i, j, kk: (i, j))

    n_in = 2 + (1 if has_res else 0) + (3 if rope is not None else 0)

    def body(*refs):
        a_ref, b_ref = refs[0], refs[1]
        r_ref = refs[2] if has_res else None
        o_ref = refs[n_in]
        part = lax.dot_general(a_ref[...].astype(BF16), b_ref[...].astype(BF16), dims, preferred_element_type=F32)

        def finish(r):
            if has_res:
                r = r + r_ref[...]
            if rope is not None:
                r = _rotate_tile(r, tuple(t[...] for t in refs[n_in - 3:n_in]), rope[1], False)
            o_ref[...] = r.astype(out_dtype)

        if nk == 1:
            finish(part)
            return
        acc = refs[-1]
        kk = pl.program_id(2)

        @pl.when(kk == 0)
        def _():
            acc[...] = part

        @pl.when(jnp.logical_and(kk > 0, kk < nk - 1))
        def _():
            acc[...] += part

        @pl.when(kk == nk - 1)
        def _():
            finish(acc[...] + part)

    args = (a, b, res) if has_res else (a, b)
    in_specs = [a_spec, b_spec] + ([o_spec] if has_res else [])
    if rope is not None:
        args = args + tuple(rope[0])
        in_specs = in_specs + [pl.BlockSpec((bm, LANES), lambda i, j, kk: (i, 0))] * 3
    return pl.pallas_call(
        body, name=name, grid=(m // bm, n // bn, nk), in_specs=in_specs, out_specs=o_spec,
        out_shape=jax.ShapeDtypeStruct((m, n), out_dtype),
        scratch_shapes=[pltpu.VMEM((bm, bn), F32)] if nk > 1 else [],
        compiler_params=_params("parallel", "parallel", "arbitrary"),
    )(*args)


def _rms_fwd(x, g, name, out_dtype=BF16):
    s, d = x.shape
    bs = _pick(s, (512, 256, 128))

    def body(x_ref, g_ref, o_ref):
        xv = x_ref[...]
        r = lax.rsqrt(jnp.mean(xv * xv, axis=-1, keepdims=True) + NORM_EPS)
        o_ref[...] = ((xv * r) * g_ref[...]).astype(out_dtype)

    return pl.pallas_call(
        body, name=name, grid=(s // bs,),
        in_specs=[pl.BlockSpec((bs, d), lambda i: (i, 0)), pl.BlockSpec((1, d), lambda i: (0, 0))],
        out_specs=pl.BlockSpec((bs, d), lambda i: (i, 0)), out_shape=jax.ShapeDtypeStruct((s, d), out_dtype),
        compiler_params=_params("parallel"),
    )(x, g.reshape(1, d))


def _rms_bwd(x, g, dy, name, dres=None):
    s, d = x.shape
    bs = _pick(s, (512, 256, 128))
    has_res = dres is not None

    def body(*refs):
        if has_res:
            x_ref, g_ref, dy_ref, r_ref, dx_ref, dg_ref = refs
        else:
            x_ref, g_ref, dy_ref, dx_ref, dg_ref = refs
        i = pl.program_id(0)
        xv = x_ref[...]
        dy = dy_ref[...]
        r = lax.rsqrt(jnp.mean(xv * xv, axis=-1, keepdims=True) + NORM_EPS)
        xh = xv * r
        dxh = dy * g_ref[...]
        dx = r * (dxh - xh * jnp.mean(dxh * xh, axis=-1, keepdims=True))
        if has_res:
            dx = dx + r_ref[...]
        dx_ref[...] = dx

        @pl.when(i == 0)
        def _():
            dg_ref[...] = jnp.zeros_like(dg_ref)

        dg_ref[...] += jnp.sum(dy * xh, axis=0, keepdims=True)

    row = pl.BlockSpec((bs, d), lambda i: (i, 0))
    vec = pl.BlockSpec((1, d), lambda i: (0, 0))
    args = (x, g.reshape(1, d), dy) + ((dres,) if has_res else ())
    dx, dg = pl.pallas_call(
        body, name=name, grid=(s // bs,), in_specs=[row, vec, row] + ([row] if has_res else []),
        out_specs=(row, vec), out_shape=(jax.ShapeDtypeStruct((s, d), F32), jax.ShapeDtypeStruct((1, d), F32)),
        compiler_params=_params("arbitrary"),
    )(*args)
    return dx, dg.reshape(d)


def _rope_tables(positions, dh, offset, period):
    role = np.zeros(LANES, np.int32)
    for base in range(0, LANES, period):
        role[base + offset:base + offset + dh // 2] = 1
        role[base + offset + dh // 2:base + offset + dh] = 2
    inv_freq = ROPE_THETA ** (-jnp.arange(0, dh, 2, dtype=F32) / dh)
    one_period = jnp.concatenate([jnp.zeros((offset,), F32), inv_freq, inv_freq,
                                  jnp.zeros((period - offset - dh,), F32)])
    ang = positions.astype(F32)[:, None] * jnp.tile(one_period, LANES // period)[None, :]
    c, s = jnp.cos(ang), jnp.sin(ang)
    role = role[None, :]
    a = jnp.where(role == 0, 1.0, c).astype(F32)
    bm = jnp.where(role == 2, s, 0.0).astype(F32)
    bp = jnp.where(role == 1, -s, 0.0).astype(F32)
    return a, bm, bp


def _rope_apply(x, tabs, half, transpose, name, shared=None, sum_blocks=False):
    s, w = x.shape
    bs = _pick(s, (512, 256, 128))
    nc = w // LANES
    a, bm, bp = tabs
    has_shared = shared is not None

    def body(*refs):
        x_ref, a_ref, bm_ref, bp_ref = refs[:4]
        o_ref = refs[5] if has_shared else refs[4]
        av, bmv, bpv = a_ref[...], bm_ref[...], bp_ref[...]
        total = None
        for c in range(nc):
            sl = slice(c * LANES, (c + 1) * LANES)
            xv = x_ref[:, sl]
            if has_shared:
                xv = xv + refs[4][...]
            out = _rotate_block(xv, av, bmv, bpv, half, transpose)
            o_ref[:, sl] = out
            total = out if total is None else total + out
        if sum_blocks:
            refs[-1][...] = total

    row = pl.BlockSpec((bs, w), lambda i: (i, 0))
    tab = pl.BlockSpec((bs, LANES), lambda i: (i, 0))
    out_shape = jax.ShapeDtypeStruct((s, w), F32)
    return pl.pallas_call(
        body, name=name, grid=(s // bs,), in_specs=[row, tab, tab, tab] + ([tab] if has_shared else []),
        out_specs=(row, tab) if sum_blocks else row,
        out_shape=(out_shape, jax.ShapeDtypeStruct((s, LANES), F32)) if sum_blocks else out_shape,
        compiler_params=_params("parallel"),
    )(x, a, bm, bp, *((shared,) if has_shared else ()))


def _make_rope(half, name):
    @jax.custom_vjp
    def rope(x, a, bm, bp):
        return _rope_apply(x, (a, bm, bp), half, False, name + "_fwd")

    def fwd(x, a, bm, bp):
        return rope(x, a, bm, bp), (a, bm, bp)

    def bwd(tabs, dy):
        return _rope_apply(dy, tabs, half, True, name + "_bwd"), None, None, None

    rope.defvjp(fwd, bwd)
    return rope


def _make_rope_shared(half, name):
    @jax.custom_vjp
    def rope(x, shared, a, bm, bp):
        return _rope_apply(x, (a, bm, bp), half, False, name + "_fwd", shared=shared)

    def fwd(x, shared, a, bm, bp):
        return rope(x, shared, a, bm, bp), (a, bm, bp)

    def bwd(tabs, dy):
        dx, dshared = _rope_apply(dy, tabs, half, True, name + "_bwd", sum_blocks=True)
        return dx, dshared, None, None, None

    rope.defvjp(fwd, bwd)
    return rope


def _lane_masks():
    lane = lax.broadcasted_iota(jnp.int32, (1, LANES), 1)
    lo = lane < HEAD_DIM
    return [lo, jnp.logical_not(lo)]


def _sel(mask, v):
    return jnp.where(mask, v, jnp.zeros_like(v))


_NT = (((1,), (1,)), ((), ()))
_NN = (((1,), (0,)), ((), ()))
_TN = (((0,), (0,)), ((), ()))
_BNT = (((2,), (2,)), ((0,), (0,)))
_BNN = (((2,), (1,)), ((0,), (0,)))


def _dot(a, b, dims):
    return lax.dot_general(a, b, dims, preferred_element_type=F32)


def _band_masks(max_dist):
    assert BLOCK - 1 <= max_dist <= BLOCK
    r = lax.broadcasted_iota(jnp.int32, (BLOCK, BLOCK), 0)
    c = lax.broadcasted_iota(jnp.int32, (BLOCK, BLOCK), 1)
    return (BLOCK + r - c) <= max_dist, r >= c


def _stack_heads(t):
    return jnp.concatenate([t, t], axis=0)


def _head_terms(lms, a, prod, lv):
    t = jnp.sum(_sel(lms[a], prod), axis=-1, keepdims=True)
    lse = jnp.max(jnp.where(lms[a], lv, -jnp.inf), axis=-1, keepdims=True)
    return t, lse


class _Residue:
    def __init__(self, ref, r, dil):
        self.ref, self.rows = ref, pl.ds(r, BLOCK, stride=dil)

    def __getitem__(self, idx):
        return self.ref[self.rows, idx[1]]

    def __setitem__(self, idx, val):
        self.ref[self.rows, idx[1]] = val


def _residues(refs, dil):
    if dil == 1:
        return [tuple(refs)]
    return [tuple(_Residue(x, r, dil) for x in refs) for r in range(dil)]


def _band_fwd(q, k, v, sinkrow, scale, max_dist, upb, dil, name):
    sq, w = q.shape
    rb = BLOCK * dil
    nq, nub, wb = sq // rb, w // (LANES * upb), LANES * upb
    has_sink = sinkrow is not None

    def body(*refs):
        s_ref = refs[5] if has_sink else None
        lms = _lane_masks()
        mprev, mcur = _band_masks(max_dist)
        mprev = jnp.logical_and(mprev, pl.program_id(1) > 0)
        mask2 = _stack_heads(jnp.concatenate([mprev, mcur], axis=1))
        chains = [(rr, slice(u * LANES, (u + 1) * LANES))
                  for rr in _residues(refs[:5] + refs[-2:], dil) for u in range(upb)]
        for g0 in range(0, len(chains), BAND_CHAINS_PER_BATCH):
            group = chains[g0:g0 + BAND_CHAINS_PER_BATCH]
            qs, kcat, vcat, sks = [], [], [], []
            for (q_ref, kp_ref, kc_ref, vp_ref, vc_ref, _, _), sl in group:
                qv = (q_ref[:, sl] * scale).astype(BF16)
                qs.append(jnp.concatenate([_sel(lms[0], qv), _sel(lms[1], qv)], axis=0))
                kcat.append(jnp.concatenate([kp_ref[:, sl].astype(BF16), kc_ref[:, sl].astype(BF16)], axis=0))
                vcat.append(jnp.concatenate([vp_ref[:, sl].astype(BF16), vc_ref[:, sl].astype(BF16)], axis=0))
                if has_sink:
                    sks.append(s_ref[sl.start // LANES])
            qs, kcat, vcat = jnp.stack(qs), jnp.stack(kcat), jnp.stack(vcat)
            sc = jnp.where(mask2[None], _dot(qs, kcat, _BNT), NEG_MASK)
            m = jnp.max(sc, axis=-1, keepdims=True)
            p = jnp.exp(sc - m)
            l = jnp.sum(p, axis=-1, keepdims=True)
            pv = _dot(p.astype(BF16), vcat, _BNN)
            if has_sink:
                sk2 = jnp.stack(sks)
                m_all = jnp.maximum(m, sk2)
                shrink = jnp.exp(m - m_all)
                l = l * shrink + jnp.exp(sk2 - m_all)
                pv, m = pv * shrink, m_all
            o2 = pv / l
            lse2 = m + jnp.log(l)
            for gi, ((_, _, _, _, _, o_ref, l_ref), sl) in enumerate(group):
                o_ref[:, sl] = jnp.where(lms[0], o2[gi, :BLOCK], o2[gi, BLOCK:])
                l_ref[:, sl] = jnp.where(lms[0], lse2[gi, :BLOCK], lse2[gi, BLOCK:])

    cur = pl.BlockSpec((rb, wb), lambda ub, i: (i, ub))
    prev = pl.BlockSpec((rb, wb), lambda ub, i: (jnp.maximum(i - 1, 0), ub))
    in_specs = [cur, prev, cur, prev, cur]
    in_specs += [pl.BlockSpec((upb, 2 * BLOCK, 1), lambda ub, i: (ub, 0, 0))] if has_sink else []
    args = (q, k, k, v, v) + ((sinkrow,) if has_sink else ())
    return pl.pallas_call(
        body, name=name, grid=(nub, nq), in_specs=in_specs, out_specs=(cur, cur),
        out_shape=(jax.ShapeDtypeStruct((sq, w), F32), jax.ShapeDtypeStruct((sq, w), F32)),
        compiler_params=_params("parallel", "parallel"),
    )(*args)


def _band_dq(q, k, v, o, lse, do, sinkrow, scale, max_dist, upb, dil, name):
    sq, w = q.shape
    rb = BLOCK * dil
    nq, nub, wb = sq // rb, w // (LANES * upb), LANES * upb
    has_sink = sinkrow is not None

    def body(*refs):
        if has_sink:
            s_ref, dq_block, dsink_ref = refs[8], refs[9], refs[10]
        else:
            dq_block = refs[8]
        i = pl.program_id(1)
        lms = _lane_masks()
        mprev, mcur = _band_masks(max_dist)
        mprev = jnp.logical_and(mprev, i > 0)
        mask2 = _stack_heads(jnp.concatenate([mprev, mcur], axis=1))
        if has_sink:
            @pl.when(i == 0)
            def _():
                dsink_ref[...] = jnp.zeros_like(dsink_ref)

        chains = [(rr, slice(u * LANES, (u + 1) * LANES))
                  for rr in _residues(refs[:8] + (dq_block,), dil) for u in range(upb)]
        for g0 in range(0, len(chains), BAND_CHAINS_PER_BATCH):
            group = chains[g0:g0 + BAND_CHAINS_PER_BATCH]
            qs, dos, kcat, vcat, t2, lse2 = [], [], [], [], [], []
            for (q_ref, kp_ref, kc_ref, vp_ref, vc_ref, o_ref, l_ref, do_ref, _), sl in group:
                qv = (q_ref[:, sl] * scale).astype(BF16)
                dov = do_ref[:, sl]
                prod = dov * o_ref[:, sl]
                dob = dov.astype(BF16)
                lv = l_ref[:, sl]
                (t0, lse0), (t1, lse1) = _head_terms(lms, 0, prod, lv), _head_terms(lms, 1, prod, lv)
                t2.append(jnp.concatenate([t0, t1], axis=0))
                lse2.append(jnp.concatenate([lse0, lse1], axis=0))
                qs.append(jnp.concatenate([_sel(lms[0], qv), _sel(lms[1], qv)], axis=0))
                dos.append(jnp.concatenate([_sel(lms[0], dob), _sel(lms[1], dob)], axis=0))
                kcat.append(jnp.concatenate([kp_ref[:, sl].astype(BF16), kc_ref[:, sl].astype(BF16)], axis=0))
                vcat.append(jnp.concatenate([vp_ref[:, sl].astype(BF16), vc_ref[:, sl].astype(BF16)], axis=0))
                if has_sink:
                    rs = -jnp.exp(s_ref[:, sl] - lv) * jnp.where(lms[0], t0, t1)
                    dsink_ref[0:1, sl] += jnp.sum(rs, axis=0, keepdims=True)
            qs, dos, kcat, vcat = jnp.stack(qs), jnp.stack(dos), jnp.stack(kcat), jnp.stack(vcat)
            p = jnp.exp(jnp.where(mask2[None], _dot(qs, kcat, _BNT), NEG_MASK) - jnp.stack(lse2))
            ds = (p * (_dot(dos, vcat, _BNT) - jnp.stack(t2))).astype(BF16)
            dq2 = _dot(ds, kcat, _BNN) * scale
            for gi, ((_, _, _, _, _, _, _, _, dq_ref), sl) in enumerate(group):
                dq_ref[:, sl] = jnp.where(lms[0], dq2[gi, :BLOCK], dq2[gi, BLOCK:])

    cur = pl.BlockSpec((rb, wb), lambda ub, i: (i, ub))
    prev = pl.BlockSpec((rb, wb), lambda ub, i: (jnp.maximum(i - 1, 0), ub))
    in_specs = [cur, prev, cur, prev, cur, cur, cur, cur]
    args = (q, k, k, v, v, o, lse, do)
    out_specs, out_shape = cur, jax.ShapeDtypeStruct((sq, w), F32)
    sem = ("parallel", "parallel")
    if has_sink:
        in_specs = in_specs + [pl.BlockSpec((1, wb), lambda ub, i: (0, ub))]
        args = args + (sinkrow,)
        out_specs = (cur, pl.BlockSpec((8, wb), lambda ub, i: (0, ub)))
        out_shape = (out_shape, jax.ShapeDtypeStruct((8, w), F32))
        sem = ("parallel", "arbitrary")
    return pl.pallas_call(
        body, name=name, grid=(nub, nq), in_specs=in_specs, out_specs=out_specs, out_shape=out_shape,
        compiler_params=_params(*sem),
    )(*args)


def _band_dkv(q, k, v, o, lse, do, scale, max_dist, upb, dil, name):
    sq, w = q.shape
    rb = BLOCK * dil
    nq, nub, wb = sq // rb, w // (LANES * upb), LANES * upb

    def body(*refs):
        kb = pl.program_id(1)
        lms = _lane_masks()
        key = lax.broadcasted_iota(jnp.int32, (BLOCK, BLOCK), 0)
        qry = lax.broadcasted_iota(jnp.int32, (BLOCK, BLOCK), 1)
        msame = qry >= key
        mnext = jnp.logical_and((BLOCK + qry - key) <= max_dist, kb < nq - 1)
        mask4 = jnp.concatenate([msame, msame, mnext, mnext], axis=1)
        chains =[(rr, slice(u * LANES, (u + 1) * LANES)) for rr in _residues(refs, dil) for u in range(upb)]
        for g0 in range(0, len(chains), BAND_CHAINS_PER_BATCH):
            group = chains[g0:g0 + BAND_CHAINS_PER_BATCH]
            kvs, vvs, qss, doss, t4s, lse4s = [], [], [], [], [], []
            for (k_ref, v_ref, qs_ref, qn_ref, os_ref, on_ref, ls_ref, ln_ref, dos_ref, don_ref, _, _), sl in group:
                kvs.append(k_ref[:, sl].astype(BF16))
                vvs.append(v_ref[:, sl].astype(BF16))
                qparts, doparts, tparts, lparts = [], [], [], []
                for q_ref, o_ref, l_ref, do_ref in ((qs_ref, os_ref, ls_ref, dos_ref),
                                                    (qn_ref, on_ref, ln_ref, don_ref)):
                    qv = (q_ref[:, sl] * scale).astype(BF16)
                    dov = do_ref[:, sl]
                    prod_t = (dov * o_ref[:, sl]).T
                    dob = dov.astype(BF16)
                    lse_t = l_ref[:, sl].T
                    for a in range(2):
                        lanes = slice(a * HEAD_DIM, (a + 1) * HEAD_DIM)
                        qparts.append(_sel(lms[a], qv))
                        doparts.append(_sel(lms[a], dob))
                        tparts.append(jnp.sum(prod_t[lanes, :], axis=0, keepdims=True))
                        lparts.append(lse_t[a * HEAD_DIM:a * HEAD_DIM + 1, :])
                qss.append(jnp.concatenate(qparts, axis=0))
                doss.append(jnp.concatenate(doparts, axis=0))
                t4s.append(jnp.concatenate(tparts, axis=1))
                lse4s.append(jnp.concatenate(lparts, axis=1))
            kv, vv, qs, dos = jnp.stack(kvs), jnp.stack(vvs), jnp.stack(qss), jnp.stack(doss)
            p = jnp.exp(jnp.where(mask4[None], _dot(kv, qs, _BNT), NEG_MASK) - jnp.stack(lse4s))
            ds = (p * (_dot(vv, dos, _BNT) - jnp.stack(t4s))).astype(BF16)
            dv = _dot(p.astype(BF16), dos, _BNN)
            dk = _dot(ds, qs, _BNN)
            for gi, (rr, sl) in enumerate(group):
                rr[-1][:, sl] = dv[gi]
                rr[-2][:, sl] = dk[gi]

    same = pl.BlockSpec((rb, wb), lambda ub, kb: (kb, ub))
    nxt = pl.BlockSpec((rb, wb), lambda ub, kb: (jnp.minimum(kb + 1, nq - 1), ub))
    return pl.pallas_call(
        body, name=name, grid=(nub, nq), in_specs=[same, same, same, nxt, same, nxt, same, nxt, same, nxt],
        out_specs=(same, same),
        out_shape=(jax.ShapeDtypeStruct((sq, w), F32), jax.ShapeDtypeStruct((sq, w), F32)),
        compiler_params=_params("parallel", "parallel"),
    )(k, v, q, q, o, o, lse, lse, do, do)


def _make_band_attention(scale, max_dist, upb, name):
    @jax.custom_vjp
    def attn(q, k, v, sinks):
        return _band_fwd(q, k, v, _sink_col(sinks), scale, max_dist, upb, 1, name + "_fwd")[0]

    def fwd(q, k, v, sinks):
        o, lse = _band_fwd(q, k, v, _sink_col(sinks), scale, max_dist, upb, 1, name + "_fwd")
        return o, (q, k, v, o, lse, sinks)

    def bwd(res, do):
        q, k, v, o, lse, sinks = res
        dq, dsink = _band_dq(q, k, v, o, lse, do, _sink_row(sinks), scale, max_dist, upb, 1, name + "_dq")
        dk, dv = _band_dkv(q, k, v, o, lse, do, scale, max_dist, upb, 1, name + "_dkv")
        return dq, dk, dv, dsink[0].reshape(-1, HEAD_DIM)[:, 0]

    attn.defvjp(fwd, bwd)
    return attn


def _triangle(n, by_key):
    if by_key:
        pairs = [(i, kb) for kb in range(n) for i in range(kb, n)]
    else:
        pairs = [(i, j) for i in range(n) for j in range(i + 1)]
    qi = np.asarray([p[0] for p in pairs], np.int32)
    kj = np.asarray([p[1] for p in pairs], np.int32)
    return jnp.asarray(qi), jnp.asarray(kj)


def _gather_copies(shard_ref, out_ref, send_sems, recv_sems, arrivals):
    x, y, c = _my_place()
    me = 4 * x + 2 * y + c
    sends, recvs = [], []
    for k in range(1, N_DEV):
        px, py, pc = _flip(x, k & 4), _flip(y, k & 2), _flip(c, k & 1)
        peer = 4 * px + 2 * py + pc
        for slot, into in ((me, sends),) + (((peer, recvs),) if arrivals else ()):
            into.append(pltpu.make_async_remote_copy(
                src_ref=shard_ref, dst_ref=out_ref.at[slot], send_sem=send_sems.at[k - 1],
                recv_sem=recv_sems.at[k - 1], device_id=(px, py, pc), device_id_type=MESH_IDS))
    return me, sends, recvs


def _causal_fwd(q, k, v, scale, blk, name, shard=None):
    s, w = q.shape
    nq, nub = s // blk, w // LANES
    qi, kj = _triangle(nq, by_key=False)
    nsteps = qi.shape[0]
    gathers = shard is not None

    def body(qi_ref, kj_ref, q_ref, k_ref, v_ref, *rest):
        if gathers:
            shard_ref, o_ref, l_ref, gath_ref, m_sc, l_sc, acc_sc, send_sems, recv_sems, local_sem = rest
        else:
            o_ref, l_ref, m_sc, l_sc, acc_sc = rest
        t = pl.program_id(1)
        i, j = qi_ref[t], kj_ref[t]

        if gathers:
            ub = pl.program_id(0)

            @pl.when(jnp.logical_and(ub == 0, t == 0))
            def _():
                me, sends, _ = _gather_copies(shard_ref, gath_ref, send_sems, recv_sems, arrivals=False)
                pltpu.make_async_copy(shard_ref, gath_ref.at[me], local_sem).start()
                for cp in sends:
                    cp.start()

        @pl.when(j == 0)
        def _():
            m_sc[...] = jnp.full_like(m_sc, NEG_INIT)
            l_sc[...] = jnp.zeros_like(l_sc)
            acc_sc[...] = jnp.zeros_like(acc_sc)

        def step(diagonal):
            kv, vv = k_ref[...].astype(BF16), v_ref[...].astype(BF16)
            chains = range(0, blk, CAUSAL_ROW_CHAIN)
            scs = [_dot((q_ref[c0:c0 + CAUSAL_ROW_CHAIN, :] * scale).astype(BF16), kv, _NT) for c0 in chains]
            m_all, l_all, acc_all = m_sc[...], l_sc[...], acc_sc[...]
            m_out, l_out, acc_out = [], [], []
            for sc, c0 in zip(scs, chains):
                rows = slice(c0, c0 + CAUSAL_ROW_CHAIN)
                if diagonal:
                    r = c0 + lax.broadcasted_iota(jnp.int32, (CAUSAL_ROW_CHAIN, blk), 0)
                    c = lax.broadcasted_iota(jnp.int32, (CAUSAL_ROW_CHAIN, blk), 1)
                    sc = jnp.where(r >= c, sc, NEG_MASK)
                m_prev = m_all[rows]
                m_new = jnp.maximum(m_prev, jnp.max(sc, axis=-1, keepdims=True))
                alpha = jnp.exp(m_prev - m_new)
                p = jnp.exp(sc - m_new)
                l_out.append(alpha * l_all[rows] + jnp.sum(p, axis=-1, keepdims=True))
                m_out.append(m_new)
                acc_out.append(acc_all[rows] * alpha + _dot(p.astype(BF16), vv, _NN))
            m_sc[...] = jnp.concatenate(m_out, axis=0)
            l_sc[...] = jnp.concatenate(l_out, axis=0)
            acc_sc[...] = jnp.concatenate(acc_out, axis=0)

        @pl.when(j < i)
        def _():
            step(False)

        @pl.when(j == i)
        def _():
            step(True)
            lf = l_sc[...]
            o_ref[...] = acc_sc[...] / lf
            l_ref[...] = jnp.broadcast_to(m_sc[...] + jnp.log(lf), (blk, LANES))

        if gathers:
            @pl.when(jnp.logical_and(pl.program_id(0) == nub - 1, t == nsteps - 1))
            def _():
                me, sends, recvs = _gather_copies(shard_ref, gath_ref, send_sems, recv_sems, arrivals=True)
                for cp in recvs:
                    cp.wait_recv()
                for cp in sends:
                    cp.wait_send()
                pltpu.make_async_copy(shard_ref, gath_ref.at[me], local_sem).wait()

    qspec = pl.BlockSpec((blk, LANES), lambda ub, t, qi_ref, kj_ref: (qi_ref[t], ub))
    kspec = pl.BlockSpec((blk, LANES), lambda ub, t, qi_ref, kj_ref: (kj_ref[t], ub))
    in_specs, out_specs = [qspec, kspec, kspec], (qspec, qspec)
    out_shape = (jax.ShapeDtypeStruct((s, w), F32), jax.ShapeDtypeStruct((s, w), F32))
    scratch = [pltpu.VMEM((blk, 1), F32), pltpu.VMEM((blk, 1), F32), pltpu.VMEM((blk, LANES), F32)]
    args = (qi, kj, q, k, v)
    if gathers:
        in_specs, out_specs = in_specs + [HBM_SPEC], out_specs + (HBM_SPEC,)
        out_shape = out_shape + (jax.ShapeDtypeStruct((N_DEV,) + shard.shape, shard.dtype),)
        scratch = scratch + [pltpu.SemaphoreType.DMA((N_DEV - 1,)), pltpu.SemaphoreType.DMA((N_DEV - 1,)),
                             pltpu.SemaphoreType.DMA]
        args = args + (shard,)
    return pl.pallas_call(
        body, name=name,
        grid_spec=pltpu.PrefetchScalarGridSpec(
            num_scalar_prefetch=2, grid=(nub, nsteps), in_specs=in_specs, out_specs=out_specs, scratch_shapes=scratch),
        out_shape=out_shape, compiler_params=_params("arbitrary", "arbitrary"),
    )(*args)


def _causal_bwd(q, k, v, o, lse, do, scale, blk, name, chip_sums):
    s, w = q.shape
    nq, nub = s // blk, w // LANES
    qi, kj = _triangle(nq, by_key=True)
    nsteps = qi.shape[0]

    def body(qi_ref, kj_ref, q_ref, k_ref, v_ref, o_ref, l_ref, do_ref, t_ref, dq_ref, dk_ref, dv_ref, parts_ref,
             dk_acc, dv_acc, send_sems, recv_sems, local_sem):
        t = pl.program_id(1)
        i, kb = qi_ref[t], kj_ref[t]

        @pl.when(jnp.logical_and(pl.program_id(0) == 0, t == 0))
        def _():
            local, sends, _ = _chip_exchange_copies(t_ref, parts_ref, send_sems, recv_sems, local_sem, arrivals=False)
            local.start()
            for cp in sends:
                cp.start()

        @pl.when(t == 0)
        def _():
            dq_ref[...] = jnp.zeros_like(dq_ref)

        @pl.when(i == kb)
        def _():
            dk_acc[...] = jnp.zeros_like(dk_acc)
            dv_acc[...] = jnp.zeros_like(dv_acc)

        def step(diagonal):
            qv = (q_ref[...] * scale).astype(BF16)
            kv, vv = k_ref[...].astype(BF16), v_ref[...].astype(BF16)
            dov = do_ref[...]
            tsum = jnp.sum(dov * o_ref[...], axis=-1, keepdims=True)
            dob = dov.astype(BF16)
            sc = _dot(qv, kv, _NT)
            if diagonal:
                r = lax.broadcasted_iota(jnp.int32, (blk, blk), 0)
                c = lax.broadcasted_iota(jnp.int32, (blk, blk), 1)
                sc = jnp.where(r >= c, sc, NEG_MASK)
            p = jnp.exp(sc - l_ref[:, 0:1])
            ds = (p * (_dot(dob, vv, _NT) - tsum)).astype(BF16)
            dv_acc[...] += _dot(p.astype(BF16), dob, _TN)
            dk_acc[...] += _dot(ds, qv, _TN)
            rows = pl.ds(pl.multiple_of(i * blk, blk), blk)
            dq_ref[rows, :] += _dot(ds, kv, _NN) * scale

        @pl.when(i == kb)
        def _():
            step(True)

        @pl.when(i > kb)
        def _():
            step(False)

        @pl.when(i == nq - 1)
        def _():
            dk_ref[...] = dk_acc[...]
            dv_ref[...] = dv_acc[...]

        @pl.when(jnp.logical_and(pl.program_id(0) == nub - 1, t == nsteps - 1))
        def _():
            local, sends, recvs = _chip_exchange_copies(t_ref, parts_ref, send_sems, recv_sems, local_sem, arrivals=True)
            for cp in recvs:
                cp.wait_recv()
            for cp in sends:
                cp.wait_send()
            local.wait()

    qspec = pl.BlockSpec((blk, LANES), lambda ub, t, qi_ref, kj_ref: (qi_ref[t], ub))
    kspec = pl.BlockSpec((blk, LANES), lambda ub, t, qi_ref, kj_ref: (kj_ref[t], ub))
    whole = pl.BlockSpec((s, LANES), lambda ub, t, qi_ref, kj_ref: (0, ub))
    out = jax.ShapeDtypeStruct((s, w), F32)
    return pl.pallas_call(
        body, name=name,
        grid_spec=pltpu.PrefetchScalarGridSpec(
            num_scalar_prefetch=2, grid=(nub, nsteps), in_specs=[qspec, kspec, kspec, qspec, qspec, qspec, HBM_SPEC],
            out_specs=(whole, kspec, kspec, HBM_SPEC),
            scratch_shapes=[pltpu.VMEM((blk, LANES), F32), pltpu.VMEM((blk, LANES), F32),
                            pltpu.SemaphoreType.DMA((N_CHIPS - 1,)), pltpu.SemaphoreType.DMA((N_CHIPS - 1,)),
                            pltpu.SemaphoreType.DMA]),
        out_shape=(out, out, out, jax.ShapeDtypeStruct(chip_sums.shape, chip_sums.dtype)),
        compiler_params=_params("arbitrary", "arbitrary"),
    )(qi, kj, q, k, v, o, lse, do, chip_sums)


def _make_causal_attention(scale, blk, name, late_shapes, reduce_late):
    def forward(q, k, v, shard):
        o, lse, gathered = _causal_fwd(q, k, v, scale, blk, name + "_fwd", shard=shard)
        late = {n: jnp.zeros(t.shape, GRAD_WIRE_DTYPE) for n, t in late_shapes.items()}
        return (o, gathered, late), (q, k, v, o, lse)

    @jax.custom_vjp
    def attn(q, k, v, shard, parts_slot):
        return forward(q, k, v, shard)[0]

    def fwd(q, k, v, shard, parts_slot):
        return forward(q, k, v, shard)

    def bwd(res, cts):
        q, k, v, o, lse = res
        do, _, late_grads = cts
        dq, dk, dv, parts = _causal_bwd(q, k, v, o, lse, do, scale, blk, name + "_bwd", reduce_late(late_grads))
        return dq, dk, dv, None, parts

    attn.defvjp(fwd, bwd)
    return attn


_BTN = (((1,), (1,)), ((0,), (0,)))


def _heads(ref, scale=None):
    blocks = []
    for c in range(0, ref.shape[1], LANES):
        t = ref[:, c:c + LANES]
        blocks.append((t if scale is None else t * scale).astype(BF16))
    return jnp.stack(blocks)


def _memory_fwd(q, k, v, scale, name):
    s, w = q.shape
    m = k.shape[0]
    bq = _pick(s, (512, 256, 128))

    def body(q_ref, k_ref, v_ref, o_ref, l_ref):
        sc = _dot(_heads(q_ref, scale), _heads(k_ref), _BNT)
        mx = jnp.max(sc, axis=-1, keepdims=True)
        p = jnp.exp(sc - mx)
        l = jnp.sum(p, axis=-1, keepdims=True)
        o = _dot(p.astype(BF16), _heads(v_ref), _BNN) / l
        lse = mx + jnp.log(l)
        for h in range(w // LANES):
            o_ref[:, h * LANES:(h + 1) * LANES] = o[h]
            l_ref[:, h * LANES:(h + 1) * LANES] = jnp.broadcast_to(lse[h], (bq, LANES))

    row = pl.BlockSpec((bq, w), lambda i: (i, 0))
    mem = pl.BlockSpec((m, w), lambda i: (0, 0))
    return pl.pallas_call(
        body, name=name, grid=(s // bq,), in_specs=[row, mem, mem], out_specs=(row, row),
        out_shape=(jax.ShapeDtypeStruct((s, w), F32), jax.ShapeDtypeStruct((s, w), F32)),
        compiler_params=_params("parallel"),
    )(q, k, v)


def _memory_bwd(q, k, v, o, lse, do, scale, name):
    s, w = q.shape
    m = k.shape[0]
    nh = w // LANES
    bq = _pick(s, (512, 256, 128))

    def body(q_ref, k_ref, v_ref, o_ref, l_ref, do_ref, dq_ref, dk_ref, dv_ref):
        qs, ks, vs = _heads(q_ref, scale), _heads(k_ref), _heads(v_ref)
        dos = _heads(do_ref)
        t = jnp.stack([jnp.sum(do_ref[:, h * LANES:(h + 1) * LANES] * o_ref[:, h * LANES:(h + 1) * LANES],
                               axis=-1, keepdims=True) for h in range(nh)])
        lse = jnp.stack([l_ref[:, h * LANES:h * LANES + 1] for h in range(nh)])
        p = jnp.exp(_dot(qs, ks, _BNT) - lse)
        ds = (p * (_dot(dos, vs, _BNT) - t)).astype(BF16)
        dq = _dot(ds, ks, _BNN) * scale
        dk = _dot(ds, qs, _BTN)
        dv = _dot(p.astype(BF16), dos, _BTN)

        @pl.when(pl.program_id(0) == 0)
        def _():
            dk_ref[...] = jnp.zeros_like(dk_ref)
            dv_ref[...] = jnp.zeros_like(dv_ref)

        for h in range(nh):
            sl = slice(h * LANES, (h + 1) * LANES)
            dq_ref[:, sl] = dq[h]
            dk_ref[:, sl] += dk[h]
            dv_ref[:, sl] += dv[h]

    row = pl.BlockSpec((bq, w), lambda i: (i, 0))
    mem = pl.BlockSpec((m, w), lambda i: (0, 0))
    return pl.pallas_call(
        body, name=name, grid=(s // bq,), in_specs=[row, mem, mem, row, row, row], out_specs=(row, mem, mem),
        out_shape=(jax.ShapeDtypeStruct((s, w), F32), jax.ShapeDtypeStruct((m, w), F32),
                   jax.ShapeDtypeStruct((m, w), F32)),
        compiler_params=_params("arbitrary"),
    )(q, k, v, o, lse, do)


def _make_memory_attention(scale, name):
    @jax.custom_vjp
    def attn(q, k, v):
        return _memory_fwd(q, k, v, scale, name + "_fwd")[0]

    def fwd(q, k, v):
        o, lse = _memory_fwd(q, k, v, scale, name + "_fwd")
        return o, (q, k, v, o, lse)

    def bwd(res, do):
        q, k, v, o, lse = res
        return _memory_bwd(q, k, v, o, lse, do, scale, name + "_bwd")

    attn.defvjp(fwd, bwd)
    return attn


def _sink_row(sinks):
    return jnp.repeat(sinks.astype(F32), HEAD_DIM).reshape(1, -1)


def _sink_col(sinks):
    return jnp.repeat(sinks.astype(F32).reshape(-1, 2, 1), BLOCK, axis=1)


def _merge3(os_, ls_, name):
    s, w = os_[0].shape
    bs = _pick(s, (256, 128))

    def body(o1, o2, o3, l1, l2, l3, out_ref, lse_ref):
        a1, a2, a3 = l1[...], l2[...], l3[...]
        m = jnp.maximum(jnp.maximum(a1, a2), a3)
        e1, e2, e3 = jnp.exp(a1 - m), jnp.exp(a2 - m), jnp.exp(a3 - m)
        z = e1 + e2 + e3
        out_ref[...] = (e1 * o1[...] + e2 * o2[...] + e3 * o3[...]) / z
        lse_ref[...] = m + jnp.log(z)

    row = pl.BlockSpec((bs, w), lambda i: (i, 0))
    return pl.pallas_call(
        body, name=name, grid=(s // bs,), in_specs=[row] * 6, out_specs=(row, row),
        out_shape=(jax.ShapeDtypeStruct((s, w), F32), jax.ShapeDtypeStruct((s, w), F32)),
        compiler_params=_params("parallel"),
    )(*os_, *ls_)


def _add3(a, b, c, name):
    s, w = a.shape
    bs = _pick(s, (512, 256, 128))

    def body(a_ref, b_ref, c_ref, o_ref):
        o_ref[...] = (a_ref[...] + b_ref[...]) + c_ref[...]

    row = pl.BlockSpec((bs, w), lambda i: (i, 0))
    return pl.pallas_call(
        body, name=name, grid=(s // bs,), in_specs=[row] * 3, out_specs=row,
        out_shape=jax.ShapeDtypeStruct((s, w), F32), compiler_params=_params("parallel"),
    )(a, b, c)


def _make_dilated(name):
    scale, max_dist = HEAD_DIM ** -0.5, BLOCK

    def upb_of(dil):
        return 2 * BAND_UNITS_PER_STEP if dil == 1 else 1

    def forward(q, k, v):
        os_, ls_ = [], []
        for n, (_, dil) in enumerate(DIL_PATTERNS):
            o, l = _band_fwd(q, k, v, None, scale, max_dist, upb_of(dil), dil, "%s_b%d_fwd" % (name, n))
            os_.append(o)
            ls_.append(l)
        return _merge3(os_, ls_, name + "_merge")

    @jax.custom_vjp
    def dilated(q, k, v):
        return forward(q, k, v)[0]

    def fwd(q, k, v):
        out, lse = forward(q, k, v)
        return out, (q, k, v, out, lse)

    def bwd(res, do):
        q, k, v, out, lse = res
        dqs, dks, dvs = [], [], []
        for n, (_, dil) in enumerate(DIL_PATTERNS):
            args = (q, k, v, out, lse, do)
            dqs.append(_band_dq(*args, None, scale, max_dist, upb_of(dil), dil, "%s_b%d_dq" % (name, n)))
            dk, dv = _band_dkv(*args, scale, max_dist, upb_of(dil), dil, "%s_b%d_dkv" % (name, n))
            dks.append(dk)
            dvs.append(dv)
        return (_add3(*dqs, name + "_dq_sum"), _add3(*dks, name + "_dk_sum"), _add3(*dvs, name + "_dv_sum"))

    dilated.defvjp(fwd, bwd)
    return dilated


def _times_w(a, w, orient, name, res=None, rope=None):
    return _mm(a, w, "nn" if orient == "n" else "nt", name, res=res, rope=rope)


def _times_wt(dz, w, orient, name, res=None):
    return _mm(dz, w, "nt" if orient == "n" else "nn", name, res=res)


def _grad_w(a, dz, orient, name):
    if orient == "n":
        return _mm(a, dz, "tn", name, out_dtype=GRAD_WIRE_DTYPE)
    return _mm(dz, a, "tn", name, out_dtype=GRAD_WIRE_DTYPE)


def _make_norm_linear(name, orients, through=False, rope_halves=None):
    nw = len(orients)
    halves = rope_halves or (None,) * nw

    def rope_of(i, ropes):
        return None if halves[i] is None else (ropes[i], halves[i])

    def forward(x, g, ws, ropes):
        h = _rms_fwd(x, g, name + "_norm")
        zs = tuple(_times_w(h, w, o, "%s_mm%d" % (name, i), rope=rope_of(i, ropes))
                   for i, (w, o) in enumerate(zip(ws, orients)))
        return zs + ((x,) if through else ()), h

    @jax.custom_vjp
    def op(x, g, slots, ws, ropes):
        return forward(x, g, ws, ropes)[0]

    def fwd(x, g, slots, ws, ropes):
        outs, h = forward(x, g, ws, ropes)
        return outs, (x, g, h, ws, ropes)

    def bwd(res, cts):
        x, g, h, ws, ropes = res
        dzs = [cts[i] if halves[i] is None else
               _rope_apply(cts[i], ropes[i], halves[i], True, "%s_unrope%d" % (name, i)) for i in range(nw)]
        dh = None
        for i, (w, o) in enumerate(zip(ws, orients)):
            dh = _times_wt(dzs[i], w, o, "%s_dh%d" % (name, i), res=dh)
        dws = tuple(_grad_w(h, dzs[i], o, "%s_dw%d" % (name, i)) for i, o in enumerate(orients))
        dx, dg = _rms_bwd(x, g, dh, name + "_norm_bwd", dres=cts[nw] if through else None)
        return dx, dg, dws, (None,) * nw, tuple(None if r is None else (None,) * len(r) for r in ropes)

    op.defvjp(fwd, bwd)
    return op


def _make_linear_res(name, orient):
    @jax.custom_vjp
    def op(a, wslot, w, res):
        return _times_w(a, w, orient, name + "_mm", res=res)

    def fwd(a, wslot, w, res):
        return _times_w(a, w, orient, name + "_mm", res=res), (a, w)

    def bwd(saved, dout):
        a, w = saved
        return _times_wt(dout, w, orient, name + "_da"), _grad_w(a, dout, orient, name + "_dw"), None, dout

    op.defvjp(fwd, bwd)
    return op


FFN_TILE_M, FFN_TILE_N = 512, 1408


def _gate_up_act(h, wg, wu, name):
    m, k = h.shape
    n = wg.shape[0]
    bm, bn = _div128(m, FFN_TILE_M), _div128(n, FFN_TILE_N)

    def body(h_ref, wg_ref, wu_ref, g_ref, u_ref, a_ref):
        hv = h_ref[...]
        g = _dot(hv, wg_ref[...], _NT)
        u = _dot(hv, wu_ref[...], _NT)
        g_ref[...] = g
        u_ref[...] = u
        a_ref[...] = (g / (1.0 + jnp.exp(-g)) * u).astype(BF16)

    wspec = pl.BlockSpec((bn, k), lambda i, j: (j, 0))
    ospec = pl.BlockSpec((bm, bn), lambda i, j: (i, j))
    return pl.pallas_call(
        body, name=name, grid=(m // bm, n // bn), in_specs=[pl.BlockSpec((bm, k), lambda i, j: (i, 0)), wspec, wspec],
        out_specs=(ospec, ospec, ospec),
        out_shape=(jax.ShapeDtypeStruct((m, n), F32), jax.ShapeDtypeStruct((m, n), F32),
                   jax.ShapeDtypeStruct((m, n), BF16)),
        compiler_params=_params("parallel", "parallel"),
    )(h, wg, wu)


def _down_bwd_act(dout, wd, gmat, umat, name):
    m, k = dout.shape
    n = wd.shape[0]
    bm, bn = _div128(m, FFN_TILE_M), _div128(n, FFN_TILE_N)

    def body(do_ref, wd_ref, g_ref, u_ref, dg_ref, du_ref):
        d = _dot(do_ref[...].astype(BF16), wd_ref[...], _NT)
        g, u = g_ref[...], u_ref[...]
        sig = 1.0 / (1.0 + jnp.exp(-g))
        dg_ref[...] = (d * u * (sig * (1.0 + g * (1.0 - sig)))).astype(BF16)
        du_ref[...] = (d * (g * sig)).astype(BF16)

    ospec = pl.BlockSpec((bm, bn), lambda i, j: (i, j))
    return pl.pallas_call(
        body, name=name, grid=(m // bm, n // bn),
        in_specs=[pl.BlockSpec((bm, k), lambda i, j: (i, 0)), pl.BlockSpec((bn, k), lambda i, j: (j, 0)), ospec, ospec],
        out_specs=(ospec, ospec), out_shape=(jax.ShapeDtypeStruct((m, n), BF16),) * 2,
        compiler_params=_params("parallel", "parallel"),
    )(dout, wd, gmat, umat)


def _make_ffn(name):
    def forward(x, g, wg, wu, wd):
        h = _rms_fwd(x, g, name + "_norm")
        gmat, umat, a = _gate_up_act(h, wg, wu, name + "_gate_up")
        return _mm(a, wd, "nn", name + "_down", res=x), (x, g, h, gmat, umat, a, wg, wu, wd)

    @jax.custom_vjp
    def op(x, g, wg_slot, wu_slot, wd_slot, wg, wu, wd):
        return forward(x, g, wg, wu, wd)[0]

    def fwd(x, g, wg_slot, wu_slot, wd_slot, wg, wu, wd):
        return forward(x, g, wg, wu, wd)

    def bwd(saved, dout):
        x, g, h, gmat, umat, a, wg, wu, wd = saved
        dgm, dum = _down_bwd_act(dout, wd, gmat, umat, name + "_da_act")
        dwd = _mm(a, dout, "tn", name + "_dwd", out_dtype=GRAD_WIRE_DTYPE)
        dwg = _grad_w(h, dgm, "t", name + "_dwg")
        dwu = _grad_w(h, dum, "t", name + "_dwu")
        dh = _times_wt(dum, wu, "t", name + "_dh_u", res=_times_wt(dgm, wg, "t", name + "_dh_g"))
        dx, dg = _rms_bwd(x, g, dh, name + "_norm_bwd", dres=dout)
        return dx, dg, dwg, dwu, dwd, None, None, None

    op.defvjp(fwd, bwd)
    return op


def _make_final_loss(name):
    def run(x, g, tgt):
        s, d = x.shape
        bs = _pick(s, (512, 256, 128))

        def body(x_ref, g_ref, t_ref, loss_ref, dx_ref, dg_ref):
            i = pl.program_id(0)
            xv = x_ref[...]
            gv = g_ref[...]
            r = lax.rsqrt(jnp.mean(xv * xv, axis=-1, keepdims=True) + NORM_EPS)
            xh = xv * r
            e = xh * gv - t_ref[...]
            dy = e * (1.0 / d)
            dxh = dy * gv
            dx_ref[...] = r * (dxh - xh * jnp.mean(dxh * xh, axis=-1, keepdims=True))
            part = 0.5 * jnp.sum(jnp.sum(e * e, axis=-1, keepdims=True) * (1.0 / d), axis=0, keepdims=True)

            @pl.when(i == 0)
            def _():
                loss_ref[...] = jnp.zeros_like(loss_ref)
                dg_ref[...] = jnp.zeros_like(dg_ref)

            loss_ref[...] += jnp.broadcast_to(part, loss_ref.shape)
            dg_ref[...] += jnp.sum(dy * xh, axis=0, keepdims=True)

        row = pl.BlockSpec((bs, d), lambda i: (i, 0))
        vec = pl.BlockSpec((1, d), lambda i: (0, 0))
        loss, dx, dg = pl.pallas_call(
            body, name=name, grid=(s // bs,), in_specs=[row, vec, row],
            out_specs=(pl.BlockSpec((8, LANES), lambda i: (0, 0)), row, vec),
            out_shape=(jax.ShapeDtypeStruct((8, LANES), F32), jax.ShapeDtypeStruct((s, d), F32),
                       jax.ShapeDtypeStruct((1, d), F32)),
            compiler_params=_params("arbitrary"),
        )(x, g.reshape(1, d), tgt)
        return loss[0, 0], dx, dg.reshape(d)

    @jax.custom_vjp
    def op(x, g, tgt):
        return run(x, g, tgt)[0]

    def fwd(x, g, tgt):
        loss, dx, dg = run(x, g, tgt)
        return loss, (dx, dg)

    def bwd(saved, ct):
        dx, dg = saved
        return dx * ct, dg * ct, None

    op.defvjp(fwd, bwd)
    return op


def _model_loss(diff, consts):
    x = diff["x"]
    w = consts["w"]
    slot = diff["slots"]
    vec = diff["vec"]
    tab64, tab_mla = consts["tab64"], consts["tab_mla"]
    mem = consts["mem"]
    s = x.shape[0]

    rope64 = lambda t, nm: _make_rope(HEAD_DIM // 2, nm)(t, *tab64)

    def nl(nm, inp, gain, wnames, through=False, ropes=None):
        orients = tuple(_orient(n) for n in wnames)
        kinds = ropes or (None,) * len(wnames)
        halves = tuple({None: None, "64": HEAD_DIM // 2, "mla": MLA_ROPE_DIM // 2}[r] for r in kinds)
        tabs = tuple({None: None, "64": tab64, "mla": tab_mla}[r] for r in kinds)
        op = _make_norm_linear(nm, orients, through, halves)
        return op(inp, gain, tuple(slot[n] for n in wnames), tuple(w[n] for n in wnames), tabs)

    def lin_res(nm, a, wname, res):
        return _make_linear_res(nm, _orient(wname))(a, slot[wname], w[wname], res)

    def cross(layer, xin):
        p = "l%d_" % layer
        q, xin = nl(p + "xq", xin, vec[p + "x_norm"], (p + "w_xq",), through=True)
        kv, = nl(p + "xkv", mem, vec[p + "mem_norm"], (p + "w_xkv",))
        half = X_HEADS * X_HEAD_DIM
        o = _make_memory_attention(X_HEAD_DIM ** -0.5, p + "xattn")(q, kv[:, :half], kv[:, half:])
        return lin_res(p + "xo", o, p + "w_xo", xin)

    def ffn(layer, xin):
        p = "l%d_" % layer
        names = (p + "w_gate", p + "w_up", p + "w_down")
        return _make_ffn(p + "ffn")(xin, vec[p + "ffn_norm"], *(slot[n] for n in names), *(w[n] for n in names))

    in_parts = tuple(name for name, _, _ in _IN_PARTS) + ("l0_w_in_kr",)
    qa, kva, cq, ckv, kr_lanes, x = nl("l0_in", x, vec["l0_mix_norm"], in_parts, through=True,
                                       ropes=("64", None, None, None, None))
    ka = rope64(kva[:, :A_KV], "l0_rope_ka")
    va = kva[:, A_KV:]
    rep = SWA_HEADS // SWA_KV_HEADS
    expand = lambda t: jnp.broadcast_to(t.reshape(s, SWA_KV_HEADS, 1, HEAD_DIM),
                                        (s, SWA_KV_HEADS, rep, HEAD_DIM)).reshape(s, A_Q)
    swa = _make_band_attention(HEAD_DIM ** -0.5, SWA_WINDOW - 1, BAND_UNITS_PER_STEP, "l0_swa")
    oa = swa(qa, expand(ka), expand(va), vec["l0_sinks"])

    qfull, = nl("l0_uq", cq, vec["l0_q_norm"], ("l0_w_uq_heads",), ropes=("mla",))
    kvb, knope = nl("l0_ukv", ckv, vec["l0_kv_norm"], ("l0_w_ukv", "l0_w_uk_heads"))
    kfull = _make_rope_shared(MLA_ROPE_DIM // 2, "l0_rope_k")(knope, kr_lanes, *tab_mla)
    def reduce_layer1(late_grads):
        return _chip_sums(_layer1_matrix_grads(late_grads), LAYER1, MAT_ROWS - LAYER0_ROWS, "l1_grads")

    mla = _make_causal_attention((MLA_NOPE_DIM + MLA_ROPE_DIM) ** -0.5, _pick(s, (1024, 512, 256, 128)), "l0_mla",
                                 consts["layer1_shapes"], reduce_layer1)
    ob, gathered1, slots1 = mla(qfull, kfull, kvb, consts["shard1"], diff["parts1_slot"])
    w = {**w, **_layer1_weights(_unpack_gathered(gathered1, LAYER1))}
    slot = {**slot, **slots1}
    x = lin_res("l0_out_a", oa, "l0_w_out_swa", x)
    x = lin_res("l0_out_b", ob, "l0_w_out_mla", x)
    x = cross(0, x)
    x = ffn(0, x)

    q, k, v, x = nl("l1_qkv", x, vec["l1_mix_norm"], ("l1_w_q", "l1_w_k", "l1_w_v"), through=True,
                    ropes=("64", "64", None))
    o = _make_dilated("l1_dil")(q, k, v)
    x = lin_res("l1_out", o, "l1_w_out", x)
    x = cross(1, x)
    x = ffn(1, x)

    return _make_final_loss("final_loss")(x, vec["final_norm"], consts["target"])


MESH_IDS = pl.DeviceIdType.MESH
HBM_SPEC = pl.BlockSpec(memory_space=pltpu.HBM)


def _my_place():
    return lax.axis_index("x"), lax.axis_index("y"), lax.axis_index("c")


def _flip(v, bit):
    return 1 - v if bit else v


def _all_gather_rows(shard):
    r, c_ = shard.shape

    def body(x_ref, out_ref, send_sems, recv_sems, local_sem):
        x, y, c = _my_place()
        me, sibling = (x, y, c), (x, y, 1 - c)
        chips = [(1 - x, y), (x, 1 - y), (1 - x, 1 - y)]

        def slot(px, py, pc):
            return out_ref.at[4 * px + 2 * py + pc]

        def copy(k, block, to, src=None):
            return pltpu.make_async_remote_copy(
                src_ref=slot(*block) if src is None else src, dst_ref=slot(*block), send_sem=send_sems.at[k],
                recv_sem=recv_sems.at[k], device_id=to, device_id_type=MESH_IDS)

        mine = pltpu.make_async_copy(x_ref, slot(*me), local_sem)
        mine.start()
        first = [copy(0, me, sibling, src=x_ref)]
        first += [copy(1 + j, me, (*chip, c), src=x_ref) for j, chip in enumerate(chips)]
        for cp in first:
            cp.start()
        passed = [copy(4 + j, (*chip, c), sibling) for j, chip in enumerate(chips)]
        for j, chip in enumerate(chips):
            copy(1 + j, (*chip, c), me).wait_recv()
            passed[j].start()
        copy(0, sibling, me).wait_recv()
        for j, chip in enumerate(chips):
            copy(4 + j, (*chip, 1 - c), me).wait_recv()
        for cp in first + passed:
            cp.wait_send()
        mine.wait()

    return pl.pallas_call(
        body, name="weights_all_gather", out_shape=jax.ShapeDtypeStruct((N_DEV, r, c_), shard.dtype),
        in_specs=[HBM_SPEC], out_specs=HBM_SPEC,
        scratch_shapes=[pltpu.SemaphoreType.DMA((7,)), pltpu.SemaphoreType.DMA((7,)), pltpu.SemaphoreType.DMA],
    )(shard)


N_CHIPS = 4


def _exchange_with_sibling(slabs, name):
    _, nq, r, c_ = slabs.shape

    def body(p_ref, out_ref, send_sem, recv_sem):
        x, y, c = _my_place()
        cp = pltpu.make_async_remote_copy(
            src_ref=p_ref.at[1 - c], dst_ref=out_ref, send_sem=send_sem, recv_sem=recv_sem,
            device_id=(x, y, 1 - c), device_id_type=MESH_IDS)
        cp.start()
        cp.wait_recv()
        cp.wait_send()

    return pl.pallas_call(
        body, name=name, out_shape=jax.ShapeDtypeStruct((nq, r, c_), slabs.dtype),
        in_specs=[HBM_SPEC], out_specs=HBM_SPEC,
        scratch_shapes=[pltpu.SemaphoreType.DMA, pltpu.SemaphoreType.DMA],
    )(slabs)


def _add_pairs(a, b, name):
    nq, r, c_ = a.shape
    br = max(d for d in range(PART_ROW_ALIGN, r + 1, PART_ROW_ALIGN) if r % d == 0 and d <= ADD_PAIRS_MAX_ROWS)

    def body(a_ref, b_ref, o_ref):
        o_ref[...] = (a_ref[...].astype(F32) + b_ref[...].astype(F32)).astype(o_ref.dtype)

    blk = pl.BlockSpec((1, br, c_), lambda q, i: (q, i, 0))
    return pl.pallas_call(
        body, name=name, grid=(nq, r // br), in_specs=[blk, blk], out_specs=blk,
        out_shape=jax.ShapeDtypeStruct(a.shape, a.dtype), compiler_params=_params("parallel", "parallel"),
    )(a, b)


def _chip_exchange_copies(t_ref, out_ref, send_sems, recv_sems, local_sem, arrivals):
    x, y, c = _my_place()
    myq = 2 * x + y
    local = pltpu.make_async_copy(t_ref.at[myq], out_ref.at[myq], local_sem)
    sends, recvs = [], []
    for k in range(1, N_CHIPS):
        px, py = _flip(x, k & 2), _flip(y, k & 1)
        peer = 2 * px + py
        for src, dst, into in ((peer, myq, sends),) + (((myq, peer, recvs),) if arrivals else ()):
            into.append(pltpu.make_async_remote_copy(
                src_ref=t_ref.at[src], dst_ref=out_ref.at[dst], send_sem=send_sems.at[k - 1],
                recv_sem=recv_sems.at[k - 1], device_id=(px, py, c), device_id_type=MESH_IDS))
    return local, sends, recvs


def _exchange_between_chips(slabs):
    def body(t_ref, out_ref, send_sems, recv_sems, local_sem):
        local, sends, recvs = _chip_exchange_copies(t_ref, out_ref, send_sems, recv_sems, local_sem, arrivals=True)
        local.start()
        for cp in sends:
            cp.start()
        for cp in recvs:
            cp.wait_recv()
        for cp in sends:
            cp.wait_send()
        local.wait()

    return pl.pallas_call(
        body, name="grad_exchange_chips", out_shape=jax.ShapeDtypeStruct(slabs.shape, slabs.dtype),
        in_specs=[HBM_SPEC], out_specs=HBM_SPEC,
        scratch_shapes=[pltpu.SemaphoreType.DMA((N_CHIPS - 1,)), pltpu.SemaphoreType.DMA((N_CHIPS - 1,)),
                        pltpu.SemaphoreType.DMA],
    )(slabs)


def _all_reduce_small(v):
    r, c_ = v.shape

    def body(v_ref, out_ref, buf, send_sems, recv_sems):
        x, y, c = _my_place()
        me = 4 * x + 2 * y + c
        buf[me] = v_ref[...]
        sends, recvs = [], []
        for k in range(1, N_DEV):
            px, py, pc = _flip(x, k & 4), _flip(y, k & 2), _flip(c, k & 1)
            peer = 4 * px + 2 * py + pc
            sends.append(pltpu.make_async_remote_copy(
                src_ref=v_ref, dst_ref=buf.at[me], send_sem=send_sems.at[k - 1], recv_sem=recv_sems.at[k - 1],
                device_id=(px, py, pc), device_id_type=MESH_IDS))
            recvs.append(pltpu.make_async_remote_copy(
                src_ref=v_ref, dst_ref=buf.at[peer], send_sem=send_sems.at[k - 1], recv_sem=recv_sems.at[k - 1],
                device_id=(px, py, pc), device_id_type=MESH_IDS))
        for cp in sends:
            cp.start()
        for cp in recvs:
            cp.wait_recv()
        for cp in sends:
            cp.wait_send()
        acc = buf[0]
        for d in range(1, N_DEV):
            acc = acc + buf[d]
        out_ref[...] = acc

    vm = pl.BlockSpec(memory_space=pltpu.VMEM)
    return pl.pallas_call(
        body, name="vector_grad_all_reduce", out_shape=jax.ShapeDtypeStruct((r, c_), F32), in_specs=[vm], out_specs=vm,
        scratch_shapes=[pltpu.VMEM((N_DEV, r, c_), F32), pltpu.SemaphoreType.DMA((7,)), pltpu.SemaphoreType.DMA((7,))],
    )(v)


def _adamw_math(w, g, m, v):
    m = ADAM_B1 * m + (1.0 - ADAM_B1) * g
    v = ADAM_B2 * v + (1.0 - ADAM_B2) * (g * g)
    m_hat = m / (1.0 - ADAM_B1 ** ADAM_STEP)
    v_hat = v / (1.0 - ADAM_B2 ** ADAM_STEP)
    delta = -ADAM_LR * (m_hat / (jnp.sqrt(v_hat) + ADAM_EPS) + ADAM_WD * w)
    return delta, m, v


def _sum_and_adamw(parts, w, m, v):
    nparts, r, c_ = parts.shape
    br = _pick(r, (256, 128, 64, 32, 16, 8))

    def body(p_ref, w_ref, m_ref, v_ref, g_ref, d_ref, nm_ref, nv_ref):
        g = p_ref[0].astype(F32)
        for d in range(1, nparts):
            g = g + p_ref[d].astype(F32)
        g_ref[...] = g
        d_ref[...], nm_ref[...], nv_ref[...] = _adamw_math(w_ref[...], g, m_ref[...], v_ref[...])

    row = pl.BlockSpec((br, c_), lambda i: (i, 0))
    return pl.pallas_call(
        body, name="grad_sum_adamw", grid=(r // br,),
        in_specs=[pl.BlockSpec((nparts, br, c_), lambda i: (0, i, 0)), row, row, row], out_specs=(row,) * 4,
        out_shape=(jax.ShapeDtypeStruct((r, c_), F32),) * 4, compiler_params=_params("parallel"),
    )(parts, w, m, v)


def _adamw_small(w, g, m, v):
    vm = pl.BlockSpec(memory_space=pltpu.VMEM)

    def body(w_ref, g_ref, m_ref, v_ref, d_ref, nm_ref, nv_ref):
        d_ref[...], nm_ref[...], nv_ref[...] = _adamw_math(w_ref[...], g_ref[...], m_ref[...], v_ref[...])

    return pl.pallas_call(
        body, name="vector_adamw", in_specs=[vm] * 4, out_specs=(vm,) * 3,
        out_shape=(jax.ShapeDtypeStruct(w.shape, F32),) * 3,
    )(w, g, m, v)


def _pad_rows(t, axis):
    extra = -t.shape[axis] % PART_ROW_ALIGN
    if extra == 0:
        return t
    widths = [(0, 0)] * t.ndim
    widths[axis] = (0, extra)
    return jnp.pad(t, widths)


def _pack_local(named):
    rows = [_pad_rows((named[n].T if kind == "c" else named[n]).reshape(-1, PACK_COLS), 0)
            for n, kind, _, _ in MATRICES]
    rows.append(jnp.zeros((MAT_ROWS - MAT_ROWS_USED, PACK_COLS), rows[0].dtype))
    return jnp.concatenate(rows, axis=0)


def _unpack_local(packed):
    out, r0 = {}, 0
    for n, kind, k, nn in MATRICES:
        nr = k * nn // N_DEV // PACK_COLS
        part = packed[r0:r0 + nr]
        out[n] = part.reshape(nn // N_DEV, k).T if kind == "c" else part.reshape(k // N_DEV, nn)
        r0 += _part_rows(k, nn)
    return out


def _unpack_gathered(g, matrices):
    out, r0 = {}, 0
    for n, kind, k, nn in matrices:
        nr = k * nn // N_DEV // PACK_COLS
        out[n] = g[:, r0:r0 + nr].reshape((nn, k) if kind == "c" else (k, nn))
        r0 += _part_rows(k, nn)
    return out


def _chip_sums(grads, matrices, rows, name):
    slabs = _pack_full_grads(grads, matrices, rows)
    from_sibling = _exchange_with_sibling(slabs, name + "_exchange_sibling")
    mine = lax.dynamic_index_in_dim(slabs, lax.axis_index("c"), axis=0, keepdims=False)
    return _add_pairs(mine, from_sibling, name + "_chip_sum")


def _pack_full_grads(grads, matrices, rows_total):
    rows = []
    for n, _, k, nn in matrices:
        gmat = grads[n].reshape(N_CHIPS, 2, -1, PACK_COLS).transpose(1, 0, 2, 3)
        rows.append(_pad_rows(gmat, 2))
    used = sum(_part_rows(k, nn) for _, _, k, nn in matrices)
    if rows_total > used:
        rows.append(jnp.zeros((2, N_CHIPS, rows_total - used, PACK_COLS), rows[0].dtype))
    return jnp.concatenate(rows, axis=2)


def _pack_vectors(named):
    rows = [jnp.pad(named[n].astype(F32), (0, PACK_COLS - d)) for n, d in VECTORS]
    rows += [jnp.zeros((PACK_COLS,), F32)] * (VEC_ROWS - len(VECTORS))
    return jnp.stack(rows, axis=0)


def _unpack_vectors(packed):
    return {n: packed[i, :d] for i, (n, d) in enumerate(VECTORS)}


def _step(inputs):
    x = inputs["x"][0]
    mem = inputs["mem"][0]
    positions = inputs["positions"][0]
    target = inputs["loss_target"][0]

    local_w = _pack_local({n: inputs[n] for n, _, _, _ in MATRICES})
    local_bf16 = local_w.astype(BF16)
    w0full = _unpack_gathered(_all_gather_rows(local_bf16[:LAYER0_ROWS]), LAYER0)
    vec = {n: inputs[n] for n, _ in VECTORS}

    loss_part, grad_x, g0full, parts1, gvec = _local_grads(w0full, local_bf16[LAYER0_ROWS:], vec, x, mem, positions,
                                                          target)
    loss = lax.psum(loss_part, ("x", "y", "c"))

    parts0 = _exchange_between_chips(_chip_sums(g0full, LAYER0, LAYER0_ROWS, "l0_grads"))
    parts = jnp.concatenate([parts0, parts1], axis=1)
    local_m = _pack_local({n: inputs["m_" + n] for n, _, _, _ in MATRICES})
    local_v = _pack_local({n: inputs["v_" + n] for n, _, _, _ in MATRICES})
    g_pk, d_pk, m_pk, v_pk = _sum_and_adamw(parts, local_w, local_m, local_v)
    g_mat, d_mat, m_mat, v_mat = (_unpack_local(t) for t in (g_pk, d_pk, m_pk, v_pk))

    g_vec_pk = _all_reduce_small(_pack_vectors(gvec))
    d_vec_pk, m_vec_pk, v_vec_pk = _adamw_small(
        _pack_vectors(vec), g_vec_pk, _pack_vectors({n: inputs["m_" + n] for n, _ in VECTORS}),
        _pack_vectors({n: inputs["v_" + n] for n, _ in VECTORS}))
    g_vec, d_vec, m_vec, v_vec = (_unpack_vectors(t) for t in (g_vec_pk, d_vec_pk, m_vec_pk, v_vec_pk))

    def pick(mats, vecs, n):
        return mats[n] if n in mats else vecs[n]

    outs = [loss, grad_x[None]]
    for mats, vecs in ((g_mat, g_vec), (d_mat, d_vec), (m_mat, m_vec), (v_mat, v_vec)):
        outs += [pick(mats, vecs, n) for n in WEIGHT_ORDER]
    return tuple(outs)


_KIND = {n: kind for n, kind, _, _ in MATRICES}
_VIEW_OF = {"l0_w_uq_heads": "l0_w_uq", "l0_w_uk_heads": "l0_w_ukv", "l0_w_out_swa": "l0_w_out",
            "l0_w_out_mla": "l0_w_out", "l0_w_in_qa": "l0_w_in", "l0_w_in_kva": "l0_w_in", "l0_w_in_cq": "l0_w_in",
            "l0_w_in_ckv": "l0_w_in", "l0_w_in_kr": "l0_w_in", "l1_w_q": "l1_w_qkv", "l1_w_k": "l1_w_qkv",
            "l1_w_v": "l1_w_qkv"}
_IN_PARTS = (("l0_w_in_qa", 0, A_Q), ("l0_w_in_kva", A_Q, A_Q + 2 * A_KV),
             ("l0_w_in_cq", A_Q + 2 * A_KV, A_Q + 2 * A_KV + MLA_Q_RANK),
             ("l0_w_in_ckv", A_Q + 2 * A_KV + MLA_Q_RANK, EVEN_IN - MLA_ROPE_DIM))
_KR_PAD = (MLA_NOPE_DIM, LANES - MLA_NOPE_DIM - MLA_ROPE_DIM)
_MLA_QK = MLA_NOPE_DIM + MLA_ROPE_DIM


def _orient(name):
    return "t" if _KIND[_VIEW_OF.get(name, name)] == "c" else "n"


def _nope_rows():
    return (np.arange(MLA_HEADS * LANES) % LANES < MLA_NOPE_DIM)[:, None]


def _layer1_weights(wfull):
    w = dict(wfull)
    w_qkv = w.pop("l1_w_qkv")
    for i, name in enumerate(("l1_w_q", "l1_w_k", "l1_w_v")):
        w[name] = w_qkv[i * D_MODEL:(i + 1) * D_MODEL]
    return w


def _layer0_weights(wfull):
    w = dict(wfull)
    w_in = w.pop("l0_w_in")
    for name, r0, r1 in _IN_PARTS:
        w[name] = w_in[r0:r1]
    w["l0_w_in_kr"] = jnp.pad(w_in[EVEN_IN - MLA_ROPE_DIM:], (_KR_PAD, (0, 0)))
    uq = w.pop("l0_w_uq").reshape(MLA_HEADS, _MLA_QK, MLA_Q_RANK)
    w["l0_w_uq_heads"] = jnp.pad(uq, ((0, 0), (0, LANES - _MLA_QK), (0, 0))).reshape(MLA_HEADS * LANES, MLA_Q_RANK)
    w["l0_w_uk_heads"] = jnp.where(_nope_rows(), wfull["l0_w_ukv"], jnp.zeros_like(wfull["l0_w_ukv"]))
    wo = w.pop("l0_w_out")
    w["l0_w_out_swa"] = wo[:A_Q]
    w["l0_w_out_mla"] = jnp.pad(wo[A_Q:].reshape(MLA_HEADS, HEAD_DIM, D_MODEL),
                                ((0, 0), (LANES - HEAD_DIM, 0), (0, 0))).reshape(MLA_HEADS * LANES, D_MODEL)
    return w


def _layer1_matrix_grads(g):
    out = {n: g[n] for n, _, _, _ in LAYER1 if n in g}
    out["l1_w_qkv"] = jnp.concatenate([g["l1_w_q"], g["l1_w_k"], g["l1_w_v"]], axis=0)
    return out


def _layer0_matrix_grads(g):
    out = {n: g[n] for n, _, _, _ in LAYER0 if n in g}
    out["l0_w_in"] = jnp.concatenate([g[name] for name, _, _ in _IN_PARTS]
                                     + [g["l0_w_in_kr"][_KR_PAD[0]:_KR_PAD[0] + MLA_ROPE_DIM]], axis=0)
    out["l0_w_uq"] = g["l0_w_uq_heads"].reshape(MLA_HEADS, LANES, MLA_Q_RANK)[:, :_MLA_QK].reshape(-1, MLA_Q_RANK)
    uk = jnp.where(_nope_rows(), g["l0_w_uk_heads"], jnp.zeros_like(g["l0_w_uk_heads"]))
    out["l0_w_ukv"] = (g["l0_w_ukv"].astype(F32) + uk.astype(F32)).astype(g["l0_w_ukv"].dtype)
    out["l0_w_out"] = jnp.concatenate(
        [g["l0_w_out_swa"],
         g["l0_w_out_mla"].reshape(MLA_HEADS, LANES, D_MODEL)[:, LANES - HEAD_DIM:].reshape(-1, D_MODEL)], axis=0)
    return out


def _local_grads(w0full, shard1, vec, x, mem, positions, target):
    w = _layer0_weights(w0full)
    gathered1 = jax.ShapeDtypeStruct((N_DEV,) + shard1.shape, shard1.dtype)
    layer1_shapes = jax.eval_shape(lambda g: _layer1_weights(_unpack_gathered(g, LAYER1)), gathered1)
    slots = {n: jnp.zeros(t.shape, GRAD_WIRE_DTYPE) for n, t in w.items()}
    tab64 = _rope_tables(positions, HEAD_DIM, 0, HEAD_DIM)
    tab_mla = _rope_tables(positions, MLA_ROPE_DIM, MLA_NOPE_DIM, LANES)
    parts1_slot = jnp.zeros((N_CHIPS,) + shard1.shape, GRAD_WIRE_DTYPE)
    diff = {"x": x, "slots": slots, "vec": vec, "parts1_slot": parts1_slot}
    consts = {"w": w, "shard1": shard1, "mem": mem, "tab64": tab64, "tab_mla": tab_mla, "target": target,
              "layer1_shapes": layer1_shapes}
    loss_part, grads = jax.value_and_grad(lambda d: _model_loss(d, consts))(diff)
    return loss_part, grads["x"], _layer0_matrix_grads(grads["slots"]), grads["parts1_slot"], grads["vec"]


_INPUT_NAMES = (("x", "mem", "positions") + WEIGHT_ORDER + ("loss_target",)
                + tuple("m_" + n for n in WEIGHT_ORDER) + tuple("v_" + n for n in WEIGHT_ORDER))


def kernel(*args):
    assert len(args) == len(_INPUT_NAMES)
    return _step(dict(zip(_INPUT_NAMES, args)))
```

```python
import numpy as np
import jax
import jax.numpy as jnp
from jax import lax
from jax.experimental import pallas as pl
from jax.experimental.pallas import tpu as pltpu

F32 = jnp.float32
BF16 = jnp.bfloat16

LANES = 128
VMEM_LIMIT_BYTES = 56 * 1024 * 1024
MM_VMEM_BUDGET = 40 * 1024 * 1024
MM_MIN_FLOP_PER_STEP = 1e9
BAND_UNITS_PER_STEP = 4
CAUSAL_ROW_CHAIN = 128
BAND_CHAINS_PER_BATCH = 4

D_MODEL = 1024
HEAD_DIM = 64
ROPE_THETA = 10000.0
NORM_EPS = 1e-6
BLOCK = 128
SWA_HEADS = 8
SWA_KV_HEADS = 2
SWA_WINDOW = 128
MLA_HEADS = 8
MLA_Q_RANK = 384
MLA_KV_RANK = 256
MLA_NOPE_DIM = 64
MLA_ROPE_DIM = 32
A_Q = SWA_HEADS * HEAD_DIM
A_KV = SWA_KV_HEADS * HEAD_DIM
EVEN_IN = A_Q + 2 * A_KV + MLA_Q_RANK + MLA_KV_RANK + MLA_ROPE_DIM
DIL_PATTERNS = ((128, 1), (512, 4), (2048, 16))
X_HEADS = 4
X_HEAD_DIM = 128

ADAM_LR = 0.001
ADAM_B1 = 0.9
ADAM_B2 = 0.999
ADAM_EPS = 1e-08
ADAM_WD = 0.01
ADAM_STEP = 10

N_DEV = 8
GRAD_WIRE_DTYPE = BF16
NEG_MASK = -1e30
NEG_INIT = -1e20

MATRICES = (
    ("l0_w_in", "c", 1024, 1440), ("l0_w_uq", "c", 384, 768), ("l0_w_ukv", "c", 256, 1024),
    ("l0_w_out", "r", 1024, 1024), ("l0_w_xq", "r", 1024, 512), ("l0_w_xkv", "r", 1024, 1024),
    ("l0_w_xo", "c", 512, 1024), ("l0_w_gate", "c", 1024, 2816), ("l0_w_up", "c", 1024, 2816),
    ("l0_w_down", "r", 2816, 1024),
    ("l1_w_qkv", "c", 1024, 3072), ("l1_w_out", "r", 1024, 1024), ("l1_w_xq", "r", 1024, 512),
    ("l1_w_xkv", "r", 1024, 1024), ("l1_w_xo", "c", 512, 1024), ("l1_w_gate", "c", 1024, 2816),
    ("l1_w_up", "c", 1024, 2816), ("l1_w_down", "r", 2816, 1024),
)
VECTORS = (
    ("l0_mix_norm", 1024), ("l0_sinks", 8), ("l0_q_norm", 384), ("l0_kv_norm", 256), ("l0_x_norm", 1024),
    ("l0_mem_norm", 1024), ("l0_ffn_norm", 1024), ("l1_mix_norm", 1024), ("l1_x_norm", 1024),
    ("l1_mem_norm", 1024), ("l1_ffn_norm", 1024), ("final_norm", 1024),
)
WEIGHT_ORDER = (
    "l0_mix_norm", "l0_w_in", "l0_sinks", "l0_q_norm", "l0_w_uq", "l0_kv_norm", "l0_w_ukv", "l0_w_out", "l0_x_norm",
    "l0_mem_norm", "l0_w_xq", "l0_w_xkv", "l0_w_xo", "l0_ffn_norm", "l0_w_gate", "l0_w_up", "l0_w_down",
    "l1_mix_norm", "l1_w_qkv", "l1_w_out", "l1_x_norm", "l1_mem_norm", "l1_w_xq", "l1_w_xkv", "l1_w_xo",
    "l1_ffn_norm", "l1_w_gate", "l1_w_up", "l1_w_down", "final_norm",
)
PACK_COLS = 1024
PART_ROW_ALIGN = 16
ADD_PAIRS_MAX_ROWS = 2048


def _part_rows(k, n):
    return -(-(k * n // N_DEV // PACK_COLS) // PART_ROW_ALIGN) * PART_ROW_ALIGN


LAYER0 = tuple(mat for mat in MATRICES if mat[0].startswith("l0_"))
LAYER1 = tuple(mat for mat in MATRICES if mat[0].startswith("l1_"))
assert MATRICES == LAYER0 + LAYER1
LAYER0_ROWS = sum(_part_rows(k, n) for _, _, k, n in LAYER0)
MAT_ROWS_USED = sum(_part_rows(k, n) for _, _, k, n in MATRICES)
MAT_ROWS = -(-MAT_ROWS_USED // 256) * 256
VEC_ROWS = 16


def _pick(n, cands):
    for c in cands:
        if n % c == 0:
            return c
    return n


def _params(*sem):
    return pltpu.CompilerParams(dimension_semantics=sem, vmem_limit_bytes=VMEM_LIMIT_BYTES)


_DIMS = {"nn": (((1,), (0,)), ((), ())), "nt": (((1,), (1,)), ((), ())), "tn": (((0,), (0,)), ((), ()))}


def _rotate_block(xv, av, bmv, bpv, half, transpose):
    if transpose:
        return xv * av + pltpu.roll(xv * bmv, LANES - half, 1) + pltpu.roll(xv * bpv, half, 1)
    return xv * av + pltpu.roll(xv, half, 1) * bmv + pltpu.roll(xv, LANES - half, 1) * bpv


def _rotate_tile(t, tabs, half, transpose):
    av, bmv, bpv = tabs
    blocks = [_rotate_block(t[:, c:c + LANES], av, bmv, bpv, half, transpose) for c in range(0, t.shape[1], LANES)]
    return blocks[0] if len(blocks) == 1 else jnp.concatenate(blocks, axis=1)


def _div128(n, cap):
    d = (min(n, cap) // LANES) * LANES
    while d >= LANES:
        if n % d == 0:
            return d
        d -= LANES
    return n


def _mm_vmem_bytes(bm, bn, bk, nk, sa, sb, so, has_res):
    est = 2 * (bm * bk * sa + bk * bn * sb + bm * bn * so) + bm * bn * 4
    est += bm * bn * 4 if nk > 1 else 0
    est += 2 * bm * bn * 4 if has_res else 0
    est += bm * bk * 2 if sa == 4 else 0
    est += bk * bn * 2 if sb == 4 else 0
    return est


def _mm_tiles(m, n, k, sa, sb, so, has_res, mode):
    bn = _div128(n, 1536)
    kcap = 2048 if mode == "tn" else k
    for bm_cap in ((1408, 2816) if mode == "tn" else (512, 1024, 2048)):
        bm = _div128(m, bm_cap)
        bk = (min(k, kcap) // LANES) * LANES
        while bk > LANES and (k % bk or _mm_vmem_bytes(bm, bn, bk, k // bk, sa, sb, so, has_res) > MM_VMEM_BUDGET):
            bk -= LANES
        if 2 * bm * bn * bk >= MM_MIN_FLOP_PER_STEP or bm == m:
            break
    return bm, bn, bk


def _mm(a, b, mode, name, out_dtype=F32, res=None, rope=None):
    if mode == "nn":
        (m, k), (k2, n) = a.shape, b.shape
    elif mode == "nt":
        (m, k), (n, k2) = a.shape, b.shape
    else:
        (k, m), (k2, n) = a.shape, b.shape
    assert k == k2, (name, a.shape, b.shape)
    has_res = res is not None
    bm, bn, bk = _mm_tiles(m, n, k, a.dtype.itemsize, b.dtype.itemsize, jnp.dtype(out_dtype).itemsize, has_res, mode)
    nk = k // bk
    dims = _DIMS[mode]
    a_spec = pl.BlockSpec((bk, bm), lambda i, j, kk: (kk, i)) if mode == "tn" else pl.BlockSpec((bm, bk), lambda i, j, kk: (i, kk))
    b_spec = pl.BlockSpec((bn, bk), lambda i, j, kk: (j, kk)) if mode == "nt" else pl.BlockSpec((bk, bn), lambda i, j, kk: (kk, j))
    o_spec = pl.BlockSpec((bm, bn), lambda i, j, kk: (i, j))

    n_in = 2 + (1 if has_res else 0) + (3 if rope is not None else 0)

    def body(*refs):
        a_ref, b_ref = refs[0], refs[1]
        r_ref = refs[2] if has_res else None
        o_ref = refs[n_in]
        part = lax.dot_general(a_ref[...].astype(BF16), b_ref[...].astype(BF16), dims, preferred_element_type=F32)

        def finish(r):
            if has_res:
                r = r + r_ref[...]
            if rope is not None:
                r = _rotate_tile(r, tuple(t[...] for t in refs[n_in - 3:n_in]), rope[1], False)
            o_ref[...] = r.astype(out_dtype)

        if nk == 1:
            finish(part)
            return
        acc = refs[-1]
        kk = pl.program_id(2)

        @pl.when(kk == 0)
        def _():
            acc[...] = part

        @pl.when(jnp.logical_and(kk > 0, kk < nk - 1))
        def _():
            acc[...] += part

        @pl.when(kk == nk - 1)
        def _():
            finish(acc[...] + part)

    args = (a, b, res) if has_res else (a, b)
    in_specs = [a_spec, b_spec] + ([o_spec] if has_res else [])
    if rope is not None:
        args = args + tuple(rope[0])
        in_specs = in_specs + [pl.BlockSpec((bm, LANES), lambda i, j, kk: (i, 0))] * 3
    return pl.pallas_call(
        body, name=name, grid=(m // bm, n // bn, nk), in_specs=in_specs, out_specs=o_spec,
        out_shape=jax.ShapeDtypeStruct((m, n), out_dtype),
        scratch_shapes=[pltpu.VMEM((bm, bn), F32)] if nk > 1 else [],
        compiler_params=_params("parallel", "parallel", "arbitrary"),
    )(*args)


def _rms_fwd(x, g, name, out_dtype=BF16):
    s, d = x.shape
    bs = _pick(s, (512, 256, 128))

    def body(x_ref, g_ref, o_ref):
        xv = x_ref[...]
        r = lax.rsqrt(jnp.mean(xv * xv, axis=-1, keepdims=True) + NORM_EPS)
        o_ref[...] = ((xv * r) * g_ref[...]).astype(out_dtype)

    return pl.pallas_call(
        body, name=name, grid=(s // bs,),
        in_specs=[pl.BlockSpec((bs, d), lambda i: (i, 0)), pl.BlockSpec((1, d), lambda i: (0, 0))],
        out_specs=pl.BlockSpec((bs, d), lambda i: (i, 0)), out_shape=jax.ShapeDtypeStruct((s, d), out_dtype),
        compiler_params=_params("parallel"),
    )(x, g.reshape(1, d))


def _rms_bwd(x, g, dy, name, dres=None):
    s, d = x.shape
    bs = _pick(s, (512, 256, 128))
    has_res = dres is not None

    def body(*refs):
        if has_res:
            x_ref, g_ref, dy_ref, r_ref, dx_ref, dg_ref = refs
        else:
            x_ref, g_ref, dy_ref, dx_ref, dg_ref = refs
        i = pl.program_id(0)
        xv = x_ref[...]
        dy = dy_ref[...]
        r = lax.rsqrt(jnp.mean(xv * xv, axis=-1, keepdims=True) + NORM_EPS)
        xh = xv * r
        dxh = dy * g_ref[...]
        dx = r * (dxh - xh * jnp.mean(dxh * xh, axis=-1, keepdims=True))
        if has_res:
            dx = dx + r_ref[...]
        dx_ref[...] = dx

        @pl.when(i == 0)
        def _():
            dg_ref[...] = jnp.zeros_like(dg_ref)

        dg_ref[...] += jnp.sum(dy * xh, axis=0, keepdims=True)

    row = pl.BlockSpec((bs, d), lambda i: (i, 0))
    vec = pl.BlockSpec((1, d), lambda i: (0, 0))
    args = (x, g.reshape(1, d), dy) + ((dres,) if has_res else ())
    dx, dg = pl.pallas_call(
        body, name=name, grid=(s // bs,), in_specs=[row, vec, row] + ([row] if has_res else []),
        out_specs=(row, vec), out_shape=(jax.ShapeDtypeStruct((s, d), F32), jax.ShapeDtypeStruct((1, d), F32)),
        compiler_params=_params("arbitrary"),
    )(*args)
    return dx, dg.reshape(d)


def _rope_tables(positions, dh, offset, period):
    role = np.zeros(LANES, np.int32)
    for base in range(0, LANES, period):
        role[base + offset:base + offset + dh // 2] = 1
        role[base + offset + dh // 2:base + offset + dh] = 2
    inv_freq = ROPE_THETA ** (-jnp.arange(0, dh, 2, dtype=F32) / dh)
    one_period = jnp.concatenate([jnp.zeros((offset,), F32), inv_freq, inv_freq,
                                  jnp.zeros((period - offset - dh,), F32)])
    ang = positions.astype(F32)[:, None] * jnp.tile(one_period, LANES // period)[None, :]
    c, s = jnp.cos(ang), jnp.sin(ang)
    role = role[None, :]
    a = jnp.where(role == 0, 1.0, c).astype(F32)
    bm = jnp.where(role == 2, s, 0.0).astype(F32)
    bp = jnp.where(role == 1, -s, 0.0).astype(F32)
    return a, bm, bp


def _rope_apply(x, tabs, half, transpose, name, shared=None, sum_blocks=False):
    s, w = x.shape
    bs = _pick(s, (512, 256, 128))
    nc = w // LANES
    a, bm, bp = tabs
    has_shared = shared is not None

    def body(*refs):
        x_ref, a_ref, bm_ref, bp_ref = refs[:4]
        o_ref = refs[5] if has_shared else refs[4]
        av, bmv, bpv = a_ref[...], bm_ref[...], bp_ref[...]
        total = None
        for c in range(nc):
            sl = slice(c * LANES, (c + 1) * LANES)
            xv = x_ref[:, sl]
            if has_shared:
                xv = xv + refs[4][...]
            out = _rotate_block(xv, av, bmv, bpv, half, transpose)
            o_ref[:, sl] = out
            total = out if total is None else total + out
        if sum_blocks:
            refs[-1][...] = total

    row = pl.BlockSpec((bs, w), lambda i: (i, 0))
    tab = pl.BlockSpec((bs, LANES), lambda i: (i, 0))
    out_shape = jax.ShapeDtypeStruct((s, w), F32)
    return pl.pallas_call(
        body, name=name, grid=(s // bs,), in_specs=[row, tab, tab, tab] + ([tab] if has_shared else []),
        out_specs=(row, tab) if sum_blocks else row,
        out_shape=(out_shape, jax.ShapeDtypeStruct((s, LANES), F32)) if sum_blocks else out_shape,
        compiler_params=_params("parallel"),
    )(x, a, bm, bp, *((shared,) if has_shared else ()))


def _make_rope(half, name):
    @jax.custom_vjp
    def rope(x, a, bm, bp):
        return _rope_apply(x, (a, bm, bp), half, False, name + "_fwd")

    def fwd(x, a, bm, bp):
        return rope(x, a, bm, bp), (a, bm, bp)

    def bwd(tabs, dy):
        return _rope_apply(dy, tabs, half, True, name + "_bwd"), None, None, None

    rope.defvjp(fwd, bwd)
    return rope


def _make_rope_shared(half, name):
    @jax.custom_vjp
    def rope(x, shared, a, bm, bp):
        return _rope_apply(x, (a, bm, bp), half, False, name + "_fwd", shared=shared)

    def fwd(x, shared, a, bm, bp):
        return rope(x, shared, a, bm, bp), (a, bm, bp)

    def bwd(tabs, dy):
        dx, dshared = _rope_apply(dy, tabs, half, True, name + "_bwd", sum_blocks=True)
        return dx, dshared, None, None, None

    rope.defvjp(fwd, bwd)
    return rope


def _lane_masks():
    lane = lax.broadcasted_iota(jnp.int32, (1, LANES), 1)
    lo = lane < HEAD_DIM
    return [lo, jnp.logical_not(lo)]


def _sel(mask, v):
    return jnp.where(mask, v, jnp.zeros_like(v))


_NT = (((1,), (1,)), ((), ()))
_NN = (((1,), (0,)), ((), ()))
_TN = (((0,), (0,)), ((), ()))
_BNT = (((2,), (2,)), ((0,), (0,)))
_BNN = (((2,), (1,)), ((0,), (0,)))


def _dot(a, b, dims):
    return lax.dot_general(a, b, dims, preferred_element_type=F32)


def _band_masks(max_dist):
    assert BLOCK - 1 <= max_dist <= BLOCK
    r = lax.broadcasted_iota(jnp.int32, (BLOCK, BLOCK), 0)
    c = lax.broadcasted_iota(jnp.int32, (BLOCK, BLOCK), 1)
    return (BLOCK + r - c) <= max_dist, r >= c


def _stack_heads(t):
    return jnp.concatenate([t, t], axis=0)


def _head_terms(lms, a, prod, lv):
    t = jnp.sum(_sel(lms[a], prod), axis=-1, keepdims=True)
    lse = jnp.max(jnp.where(lms[a], lv, -jnp.inf), axis=-1, keepdims=True)
    return t, lse


class _Residue:
    def __init__(self, ref, r, dil):
        self.ref, self.rows = ref, pl.ds(r, BLOCK, stride=dil)

    def __getitem__(self, idx):
        return self.ref[self.rows, idx[1]]

    def __setitem__(self, idx, val):
        self.ref[self.rows, idx[1]] = val


def _residues(refs, dil):
    if dil == 1:
        return [tuple(refs)]
    return [tuple(_Residue(x, r, dil) for x in refs) for r in range(dil)]


def _band_fwd(q, k, v, sinkrow, scale, max_dist, upb, dil, name):
    sq, w = q.shape
    rb = BLOCK * dil
    nq, nub, wb = sq // rb, w // (LANES * upb), LANES * upb
    has_sink = sinkrow is not None

    def body(*refs):
        s_ref = refs[5] if has_sink else None
        lms = _lane_masks()
        mprev, mcur = _band_masks(max_dist)
        mprev = jnp.logical_and(mprev, pl.program_id(1) > 0)
        mask2 = _stack_heads(jnp.concatenate([mprev, mcur], axis=1))
        chains = [(rr, slice(u * LANES, (u + 1) * LANES))
                  for rr in _residues(refs[:5] + refs[-2:], dil) for u in range(upb)]
        for g0 in range(0, len(chains), BAND_CHAINS_PER_BATCH):
            group = chains[g0:g0 + BAND_CHAINS_PER_BATCH]
            qs, kcat, vcat, sks = [], [], [], []
            for (q_ref, kp_ref, kc_ref, vp_ref, vc_ref, _, _), sl in group:
                qv = (q_ref[:, sl] * scale).astype(BF16)
                qs.append(jnp.concatenate([_sel(lms[0], qv), _sel(lms[1], qv)], axis=0))
                kcat.append(jnp.concatenate([kp_ref[:, sl].astype(BF16), kc_ref[:, sl].astype(BF16)], axis=0))
                vcat.append(jnp.concatenate([vp_ref[:, sl].astype(BF16), vc_ref[:, sl].astype(BF16)], axis=0))
                if has_sink:
                    sks.append(s_ref[sl.start // LANES])
            qs, kcat, vcat = jnp.stack(qs), jnp.stack(kcat), jnp.stack(vcat)
            sc = jnp.where(mask2[None], _dot(qs, kcat, _BNT), NEG_MASK)
            m = jnp.max(sc, axis=-1, keepdims=True)
            p = jnp.exp(sc - m)
            l = jnp.sum(p, axis=-1, keepdims=True)
            pv = _dot(p.astype(BF16), vcat, _BNN)
            if has_sink:
                sk2 = jnp.stack(sks)
                m_all = jnp.maximum(m, sk2)
                shrink = jnp.exp(m - m_all)
                l = l * shrink + jnp.exp(sk2 - m_all)
                pv, m = pv * shrink, m_all
            o2 = pv / l
            lse2 = m + jnp.log(l)
            for gi, ((_, _, _, _, _, o_ref, l_ref), sl) in enumerate(group):
                o_ref[:, sl] = jnp.where(lms[0], o2[gi, :BLOCK], o2[gi, BLOCK:])
                l_ref[:, sl] = jnp.where(lms[0], lse2[gi, :BLOCK], lse2[gi, BLOCK:])

    cur = pl.BlockSpec((rb, wb), lambda ub, i: (i, ub))
    prev = pl.BlockSpec((rb, wb), lambda ub, i: (jnp.maximum(i - 1, 0), ub))
    in_specs = [cur, prev, cur, prev, cur]
    in_specs += [pl.BlockSpec((upb, 2 * BLOCK, 1), lambda ub, i: (ub, 0, 0))] if has_sink else []
    args = (q, k, k, v, v) + ((sinkrow,) if has_sink else ())
    return pl.pallas_call(
        body, name=name, grid=(nub, nq), in_specs=in_specs, out_specs=(cur, cur),
        out_shape=(jax.ShapeDtypeStruct((sq, w), F32), jax.ShapeDtypeStruct((sq, w), F32)),
        compiler_params=_params("parallel", "parallel"),
    )(*args)


def _band_dq(q, k, v, o, lse, do, sinkrow, scale, max_dist, upb, dil, name, acc=None):
    sq, w = q.shape
    rb = BLOCK * dil
    nq, nub, wb = sq // rb, w // (LANES * upb), LANES * upb
    has_sink = sinkrow is not None

    def body(*refs):
        if has_sink:
            s_ref, dq_block, dsink_ref = refs[8], refs[9], refs[10]
        else:
            dq_block = refs[-1]
        acc_block = refs[8] if acc is not None else dq_block
        i = pl.program_id(1)
        lms = _lane_masks()
        mprev, mcur = _band_masks(max_dist)
        mprev = jnp.logical_and(mprev, i > 0)
        mask2 = _stack_heads(jnp.concatenate([mprev, mcur], axis=1))
        if has_sink:
            @pl.when(i == 0)
            def _():
                dsink_ref[...] = jnp.zeros_like(dsink_ref)

        chains = [(rr, slice(u * LANES, (u + 1) * LANES))
                  for rr in _residues(refs[:8] + (acc_block, dq_block), dil) for u in range(upb)]
        for g0 in range(0, len(chains), BAND_CHAINS_PER_BATCH):
            group = chains[g0:g0 + BAND_CHAINS_PER_BATCH]
            qs, dos, kcat, vcat, t2, lse2 = [], [], [], [], [], []
            for (q_ref, kp_ref, kc_ref, vp_ref, vc_ref, o_ref, l_ref, do_ref, _, _), sl in group:
                qv = (q_ref[:, sl] * scale).astype(BF16)
                dov = do_ref[:, sl]
                prod = dov * o_ref[:, sl]
                dob = dov.astype(BF16)
                lv = l_ref[:, sl]
                (t0, lse0), (t1, lse1) = _head_terms(lms, 0, prod, lv), _head_terms(lms, 1, prod, lv)
                t2.append(jnp.concatenate([t0, t1], axis=0))
                lse2.append(jnp.concatenate([lse0, lse1], axis=0))
                qs.append(jnp.concatenate([_sel(lms[0], qv), _sel(lms[1], qv)], axis=0))
                dos.append(jnp.concatenate([_sel(lms[0], dob), _sel(lms[1], dob)], axis=0))
                kcat.append(jnp.concatenate([kp_ref[:, sl].astype(BF16), kc_ref[:, sl].astype(BF16)], axis=0))
                vcat.append(jnp.concatenate([vp_ref[:, sl].astype(BF16), vc_ref[:, sl].astype(BF16)], axis=0))
                if has_sink:
                    rs = -jnp.exp(s_ref[:, sl] - lv) * jnp.where(lms[0], t0, t1)
                    dsink_ref[0:1, sl] += jnp.sum(rs, axis=0, keepdims=True)
            qs, dos, kcat, vcat = jnp.stack(qs), jnp.stack(dos), jnp.stack(kcat), jnp.stack(vcat)
            p = jnp.exp(jnp.where(mask2[None], _dot(qs, kcat, _BNT), NEG_MASK) - jnp.stack(lse2))
            ds = (p * (_dot(dos, vcat, _BNT) - jnp.stack(t2))).astype(BF16)
            dq2 = _dot(ds, kcat, _BNN) * scale
            for gi, (rr, sl) in enumerate(group):
                dq = jnp.where(lms[0], dq2[gi, :BLOCK], dq2[gi, BLOCK:])
                rr[-1][:, sl] = dq if acc is None else dq + rr[-2][:, sl]

    cur = pl.BlockSpec((rb, wb), lambda ub, i: (i, ub))
    prev = pl.BlockSpec((rb, wb), lambda ub, i: (jnp.maximum(i - 1, 0), ub))
    in_specs = [cur, prev, cur, prev, cur, cur, cur, cur]
    args = (q, k, k, v, v, o, lse, do)
    out_specs, out_shape = cur, jax.ShapeDtypeStruct((sq, w), F32)
    sem = ("parallel", "parallel")
    if has_sink:
        in_specs = in_specs + [pl.BlockSpec((1, wb), lambda ub, i: (0, ub))]
        args = args + (sinkrow,)
        out_specs = (cur, pl.BlockSpec((8, wb), lambda ub, i: (0, ub)))
        out_shape = (out_shape, jax.ShapeDtypeStruct((8, w), F32))
        sem = ("parallel", "arbitrary")
    aliases = {}
    if acc is not None:
        assert not has_sink
        in_specs, args, aliases = in_specs + [cur], args + (acc,), {len(args): 0}
    return pl.pallas_call(
        body, name=name, grid=(nub, nq), in_specs=in_specs, out_specs=out_specs, out_shape=out_shape,
        input_output_aliases=aliases, compiler_params=_params(*sem),
    )(*args)


def _band_dkv(q, k, v, o, lse, do, scale, max_dist, upb, dil, name, accs=None):
    sq, w = q.shape
    rb = BLOCK * dil
    nq, nub, wb = sq // rb, w // (LANES * upb), LANES * upb

    def body(*refs):
        kb = pl.program_id(1)
        lms = _lane_masks()
        key = lax.broadcasted_iota(jnp.int32, (BLOCK, BLOCK), 0)
        qry = lax.broadcasted_iota(jnp.int32, (BLOCK, BLOCK), 1)
        msame = qry >= key
        mnext = jnp.logical_and((BLOCK + qry - key) <= max_dist, kb < nq - 1)
        mask4 = jnp.concatenate([msame, msame, mnext, mnext], axis=1)
        chains =[(rr, slice(u * LANES, (u + 1) * LANES)) for rr in _residues(refs, dil) for u in range(upb)]
        for g0 in range(0, len(chains), BAND_CHAINS_PER_BATCH):
            group = chains[g0:g0 + BAND_CHAINS_PER_BATCH]
            kvs, vvs, qss, doss, t4s, lse4s = [], [], [], [], [], []
            for rr, sl in group:
                k_ref, v_ref, qs_ref, qn_ref, os_ref, on_ref, ls_ref, ln_ref, dos_ref, don_ref = rr[:10]
                kvs.append(k_ref[:, sl].astype(BF16))
                vvs.append(v_ref[:, sl].astype(BF16))
                qparts, doparts, tparts, lparts = [], [], [], []
                for q_ref, o_ref, l_ref, do_ref in ((qs_ref, os_ref, ls_ref, dos_ref),
                                                    (qn_ref, on_ref, ln_ref, don_ref)):
                    qv = (q_ref[:, sl] * scale).astype(BF16)
                    dov = do_ref[:, sl]
                    prod_t = (dov * o_ref[:, sl]).T
                    dob = dov.astype(BF16)
                    lse_t = l_ref[:, sl].T
                    for a in range(2):
                        lanes = slice(a * HEAD_DIM, (a + 1) * HEAD_DIM)
                        qparts.append(_sel(lms[a], qv))
                        doparts.append(_sel(lms[a], dob))
                        tparts.append(jnp.sum(prod_t[lanes, :], axis=0, keepdims=True))
                        lparts.append(lse_t[a * HEAD_DIM:a * HEAD_DIM + 1, :])
                qss.append(jnp.concatenate(qparts, axis=0))
                doss.append(jnp.concatenate(doparts, axis=0))
                t4s.append(jnp.concatenate(tparts, axis=1))
                lse4s.append(jnp.concatenate(lparts, axis=1))
            kv, vv, qs, dos = jnp.stack(kvs), jnp.stack(vvs), jnp.stack(qss), jnp.stack(doss)
            p = jnp.exp(jnp.where(mask4[None], _dot(kv, qs, _BNT), NEG_MASK) - jnp.stack(lse4s))
            ds = (p * (_dot(vv, dos, _BNT) - jnp.stack(t4s))).astype(BF16)
            dv = _dot(p.astype(BF16), dos, _BNN)
            dk = _dot(ds, qs, _BNN)
            for gi, (rr, sl) in enumerate(group):
                rr[-1][:, sl] = dv[gi] if accs is None else dv[gi] + rr[11][:, sl]
                rr[-2][:, sl] = dk[gi] if accs is None else dk[gi] + rr[10][:, sl]

    same = pl.BlockSpec((rb, wb), lambda ub, kb: (kb, ub))
    nxt = pl.BlockSpec((rb, wb), lambda ub, kb: (jnp.minimum(kb + 1, nq - 1), ub))
    return pl.pallas_call(
        body, name=name, grid=(nub, nq),
        in_specs=[same, same, same, nxt, same, nxt, same, nxt, same, nxt] + ([same, same] if accs else []),
        out_specs=(same, same),
        out_shape=(jax.ShapeDtypeStruct((sq, w), F32), jax.ShapeDtypeStruct((sq, w), F32)),
        input_output_aliases={10: 0, 11: 1} if accs else {},
        compiler_params=_params("parallel", "parallel"),
    )(k, v, q, q, o, o, lse, lse, do, do, *(accs or ()))


def _make_band_attention(scale, max_dist, upb, name):
    @jax.custom_vjp
    def attn(q, k, v, sinks):
        return _band_fwd(q, k, v, _sink_col(sinks), scale, max_dist, upb, 1, name + "_fwd")[0]

    def fwd(q, k, v, sinks):
        o, lse = _band_fwd(q, k, v, _sink_col(sinks), scale, max_dist, upb, 1, name + "_fwd")
        return o, (q, k, v, o, lse, sinks)

    def bwd(res, do):
        q, k, v, o, lse, sinks = res
        dq, dsink = _band_dq(q, k, v, o, lse, do, _sink_row(sinks), scale, max_dist, upb, 1, name + "_dq")
        dk, dv = _band_dkv(q, k, v, o, lse, do, scale, max_dist, upb, 1, name + "_dkv")
        return dq, dk, dv, dsink[0].reshape(-1, HEAD_DIM)[:, 0]

    attn.defvjp(fwd, bwd)
    return attn


def _triangle(n, by_key):
    if by_key:
        pairs = [(i, kb) for kb in range(n) for i in range(kb, n)]
    else:
        pairs = [(i, j) for i in range(n) for j in range(i + 1)]
    qi = np.asarray([p[0] for p in pairs], np.int32)
    kj = np.asarray([p[1] for p in pairs], np.int32)
    return jnp.asarray(qi), jnp.asarray(kj)


def _gather_copies(shard_ref, out_ref, send_sems, recv_sems, arrivals):
    x, y, c = _my_place()
    me = 4 * x + 2 * y + c
    sends, recvs = [], []
    for k in range(1, N_DEV):
        px, py, pc = _flip(x, k & 4), _flip(y, k & 2), _flip(c, k & 1)
        peer = 4 * px + 2 * py + pc
        for slot, into in ((me, sends),) + (((peer, recvs),) if arrivals else ()):
            into.append(pltpu.make_async_remote_copy(
                src_ref=shard_ref, dst_ref=out_ref.at[slot], send_sem=send_sems.at[k - 1],
                recv_sem=recv_sems.at[k - 1], device_id=(px, py, pc), device_id_type=MESH_IDS))
    return me, sends, recvs


def _causal_fwd(q, k, v, scale, blk, name, shard=None):
    s, w = q.shape
    nq, nub = s // blk, w // LANES
    qi, kj = _triangle(nq, by_key=False)
    nsteps = qi.shape[0]
    gathers = shard is not None

    def body(qi_ref, kj_ref, q_ref, k_ref, v_ref, *rest):
        if gathers:
            shard_ref, o_ref, l_ref, gath_ref, m_sc, l_sc, acc_sc, send_sems, recv_sems, local_sem = rest
        else:
            o_ref, l_ref, m_sc, l_sc, acc_sc = rest
        t = pl.program_id(1)
        i, j = qi_ref[t], kj_ref[t]

        if gathers:
            ub = pl.program_id(0)

            @pl.when(jnp.logical_and(ub == 0, t == 0))
            def _():
                me, sends, _ = _gather_copies(shard_ref, gath_ref, send_sems, recv_sems, arrivals=False)
                pltpu.make_async_copy(shard_ref, gath_ref.at[me], local_sem).start()
                for cp in sends:
                    cp.start()

        @pl.when(j == 0)
        def _():
            m_sc[...] = jnp.full_like(m_sc, NEG_INIT)
            l_sc[...] = jnp.zeros_like(l_sc)
            acc_sc[...] = jnp.zeros_like(acc_sc)

        def step(diagonal):
            kv, vv = k_ref[...].astype(BF16), v_ref[...].astype(BF16)
            chains = range(0, blk, CAUSAL_ROW_CHAIN)
            width = {c0: (c0 + CAUSAL_ROW_CHAIN if diagonal else blk) for c0 in chains}
            scs = [_dot((q_ref[c0:c0 + CAUSAL_ROW_CHAIN, :] * scale).astype(BF16), kv[:width[c0]], _NT)
                   for c0 in chains]
            m_all, l_all, acc_all = m_sc[...], l_sc[...], acc_sc[...]
            m_out, l_out, acc_out = [], [], []
            for sc, c0 in zip(scs, chains):
                rows = slice(c0, c0 + CAUSAL_ROW_CHAIN)
                if diagonal:
                    r = c0 + lax.broadcasted_iota(jnp.int32, (CAUSAL_ROW_CHAIN, width[c0]), 0)
                    c = lax.broadcasted_iota(jnp.int32, (CAUSAL_ROW_CHAIN, width[c0]), 1)
                    sc = jnp.where(r >= c, sc, NEG_MASK)
                m_prev = m_all[rows]
                m_new = jnp.maximum(m_prev, jnp.max(sc, axis=-1, keepdims=True))
                alpha = jnp.exp(m_prev - m_new)
                p = jnp.exp(sc - m_new)
                l_out.append(alpha * l_all[rows] + jnp.sum(p, axis=-1, keepdims=True))
                m_out.append(m_new)
                acc_out.append(acc_all[rows] * alpha + _dot(p.astype(BF16), vv[:width[c0]], _NN))
            m_sc[...] = jnp.concatenate(m_out, axis=0)
            l_sc[...] = jnp.concatenate(l_out, axis=0)
            acc_sc[...] = jnp.concatenate(acc_out, axis=0)

        @pl.when(j < i)
        def _():
            step(False)

        @pl.when(j == i)
        def _():
            step(True)
            lf = l_sc[...]
            o_ref[...] = acc_sc[...] / lf
            l_ref[...] = jnp.broadcast_to(m_sc[...] + jnp.log(lf), (blk, LANES))

        if gathers:
            @pl.when(jnp.logical_and(pl.program_id(0) == nub - 1, t == nsteps - 1))
            def _():
                me, sends, recvs = _gather_copies(shard_ref, gath_ref, send_sems, recv_sems, arrivals=True)
                for cp in recvs:
                    cp.wait_recv()
                for cp in sends:
                    cp.wait_send()
                pltpu.make_async_copy(shard_ref, gath_ref.at[me], local_sem).wait()

    qspec = pl.BlockSpec((blk, LANES), lambda ub, t, qi_ref, kj_ref: (qi_ref[t], ub))
    kspec = pl.BlockSpec((blk, LANES), lambda ub, t, qi_ref, kj_ref: (kj_ref[t], ub))
    in_specs, out_specs = [qspec, kspec, kspec], (qspec, qspec)
    out_shape = (jax.ShapeDtypeStruct((s, w), F32), jax.ShapeDtypeStruct((s, w), F32))
    scratch = [pltpu.VMEM((blk, 1), F32), pltpu.VMEM((blk, 1), F32), pltpu.VMEM((blk, LANES), F32)]
    args = (qi, kj, q, k, v)
    if gathers:
        in_specs, out_specs = in_specs + [HBM_SPEC], out_specs + (HBM_SPEC,)
        out_shape = out_shape + (jax.ShapeDtypeStruct((N_DEV,) + shard.shape, shard.dtype),)
        scratch = scratch + [pltpu.SemaphoreType.DMA((N_DEV - 1,)), pltpu.SemaphoreType.DMA((N_DEV - 1,)),
                             pltpu.SemaphoreType.DMA]
        args = args + (shard,)
    return pl.pallas_call(
        body, name=name,
        grid_spec=pltpu.PrefetchScalarGridSpec(
            num_scalar_prefetch=2, grid=(nub, nsteps), in_specs=in_specs, out_specs=out_specs, scratch_shapes=scratch),
        out_shape=out_shape, compiler_params=_params("arbitrary", "arbitrary"),
    )(*args)


def _causal_bwd(q, k, v, o, lse, do, scale, blk, name, chip_sums):
    s, w = q.shape
    nq, nub = s // blk, w // LANES
    qi, kj = _triangle(nq, by_key=True)
    nsteps = qi.shape[0]

    def body(qi_ref, kj_ref, q_ref, k_ref, v_ref, o_ref, l_ref, do_ref, t_ref, dq_ref, dk_ref, dv_ref, parts_ref,
             dk_acc, dv_acc, send_sems, recv_sems, local_sem):
        t = pl.program_id(1)
        i, kb = qi_ref[t], kj_ref[t]

        @pl.when(jnp.logical_and(pl.program_id(0) == 0, t == 0))
        def _():
            local, sends, _ = _chip_exchange_copies(t_ref, parts_ref, send_sems, recv_sems, local_sem, arrivals=False)
            local.start()
            for cp in sends:
                cp.start()

        @pl.when(t == 0)
        def _():
            dq_ref[...] = jnp.zeros_like(dq_ref)

        @pl.when(i == kb)
        def _():
            dk_acc[...] = jnp.zeros_like(dk_acc)
            dv_acc[...] = jnp.zeros_like(dv_acc)

        def step(diagonal):
            qv = (q_ref[...] * scale).astype(BF16)
            kv, vv = k_ref[...].astype(BF16), v_ref[...].astype(BF16)
            dov = do_ref[...]
            tsum = jnp.sum(dov * o_ref[...], axis=-1, keepdims=True)
            dob = dov.astype(BF16)
            sc = _dot(qv, kv, _NT)
            if diagonal:
                r = lax.broadcasted_iota(jnp.int32, (blk, blk), 0)
                c = lax.broadcasted_iota(jnp.int32, (blk, blk), 1)
                sc = jnp.where(r >= c, sc, NEG_MASK)
            p = jnp.exp(sc - l_ref[:, 0:1])
            ds = (p * (_dot(dob, vv, _NT) - tsum)).astype(BF16)
            dv_acc[...] += _dot(p.astype(BF16), dob, _TN)
            dk_acc[...] += _dot(ds, qv, _TN)
            rows = pl.ds(pl.multiple_of(i * blk, blk), blk)
            dq_ref[rows, :] += _dot(ds, kv, _NN) * scale

        @pl.when(i == kb)
        def _():
            step(True)

        @pl.when(i > kb)
        def _():
            step(False)

        @pl.when(i == nq - 1)
        def _():
            dk_ref[...] = dk_acc[...]
            dv_ref[...] = dv_acc[...]

        @pl.when(jnp.logical_and(pl.program_id(0) == nub - 1, t == nsteps - 1))
        def _():
            local, sends, recvs = _chip_exchange_copies(t_ref, parts_ref, send_sems, recv_sems, local_sem, arrivals=True)
            for cp in recvs:
                cp.wait_recv()
            for cp in sends:
                cp.wait_send()
            local.wait()

    qspec = pl.BlockSpec((blk, LANES), lambda ub, t, qi_ref, kj_ref: (qi_ref[t], ub))
    kspec = pl.BlockSpec((blk, LANES), lambda ub, t, qi_ref, kj_ref: (kj_ref[t], ub))
    whole = pl.BlockSpec((s, LANES), lambda ub, t, qi_ref, kj_ref: (0, ub))
    out = jax.ShapeDtypeStruct((s, w), F32)
    return pl.pallas_call(
        body, name=name,
        grid_spec=pltpu.PrefetchScalarGridSpec(
            num_scalar_prefetch=2, grid=(nub, nsteps), in_specs=[qspec, kspec, kspec, qspec, qspec, qspec, HBM_SPEC],
            out_specs=(whole, kspec, kspec, HBM_SPEC),
            scratch_shapes=[pltpu.VMEM((blk, LANES), F32), pltpu.VMEM((blk, LANES), F32),
                            pltpu.SemaphoreType.DMA((N_CHIPS - 1,)), pltpu.SemaphoreType.DMA((N_CHIPS - 1,)),
                            pltpu.SemaphoreType.DMA]),
        out_shape=(out, out, out, jax.ShapeDtypeStruct(chip_sums.shape, chip_sums.dtype)),
        compiler_params=_params("arbitrary", "arbitrary"),
    )(qi, kj, q, k, v, o, lse, do, chip_sums)


def _make_causal_attention(scale, blk, name, late_shapes, reduce_late):
    def forward(q, k, v, shard):
        o, lse, gathered = _causal_fwd(q, k, v, scale, blk, name + "_fwd", shard=shard)
        late = {n: jnp.zeros(t.shape, GRAD_WIRE_DTYPE) for n, t in late_shapes.items()}
        return (o, gathered, late), (q, k, v, o, lse)

    @jax.custom_vjp
    def attn(q, k, v, shard, parts_slot):
        return forward(q, k, v, shard)[0]

    def fwd(q, k, v, shard, parts_slot):
        return forward(q, k, v, shard)

    def bwd(res, cts):
        q, k, v, o, lse = res
        do, _, late_grads = cts
        dq, dk, dv, parts = _causal_bwd(q, k, v, o, lse, do, scale, blk, name + "_bwd", reduce_late(late_grads))
        return dq, dk, dv, None, parts

    attn.defvjp(fwd, bwd)
    return attn


_BTN = (((1,), (1,)), ((0,), (0,)))


def _heads(ref, scale=None):
    blocks = []
    for c in range(0, ref.shape[1], LANES):
        t = ref[:, c:c + LANES]
        blocks.append((t if scale is None else t * scale).astype(BF16))
    return jnp.stack(blocks)


def _memory_fwd(q, k, v, scale, name):
    s, w = q.shape
    m = k.shape[0]
    bq = _pick(s, (512, 256, 128))

    def body(q_ref, k_ref, v_ref, o_ref, l_ref):
        sc = _dot(_heads(q_ref, scale), _heads(k_ref), _BNT)
        mx = jnp.max(sc, axis=-1, keepdims=True)
        p = jnp.exp(sc - mx)
        l = jnp.sum(p, axis=-1, keepdims=True)
        o = _dot(p.astype(BF16), _heads(v_ref), _BNN) / l
        lse = mx + jnp.log(l)
        for h in range(w // LANES):
            o_ref[:, h * LANES:(h + 1) * LANES] = o[h]
            l_ref[:, h * LANES:(h + 1) * LANES] = jnp.broadcast_to(lse[h], (bq, LANES))

    row = pl.BlockSpec((bq, w), lambda i: (i, 0))
    mem = pl.BlockSpec((m, w), lambda i: (0, 0))
    return pl.pallas_call(
        body, name=name, grid=(s // bq,), in_specs=[row, mem, mem], out_specs=(row, row),
        out_shape=(jax.ShapeDtypeStruct((s, w), F32), jax.ShapeDtypeStruct((s, w), F32)),
        compiler_params=_params("parallel"),
    )(q, k, v)


def _memory_bwd(q, k, v, o, lse, do, scale, name):
    s, w = q.shape
    m = k.shape[0]
    nh = w // LANES
    bq = _pick(s, (512, 256, 128))

    def body(q_ref, k_ref, v_ref, o_ref, l_ref, do_ref, dq_ref, dk_ref, dv_ref):
        qs, ks, vs = _heads(q_ref, scale), _heads(k_ref), _heads(v_ref)
        dos = _heads(do_ref)
        t = jnp.stack([jnp.sum(do_ref[:, h * LANES:(h + 1) * LANES] * o_ref[:, h * LANES:(h + 1) * LANES],
                               axis=-1, keepdims=True) for h in range(nh)])
        lse = jnp.stack([l_ref[:, h * LANES:h * LANES + 1] for h in range(nh)])
        p = jnp.exp(_dot(qs, ks, _BNT) - lse)
        ds = (p * (_dot(dos, vs, _BNT) - t)).astype(BF16)
        dq = _dot(ds, ks, _BNN) * scale
        dk = _dot(ds, qs, _BTN)
        dv = _dot(p.astype(BF16), dos, _BTN)

        @pl.when(pl.program_id(0) == 0)
        def _():
            dk_ref[...] = jnp.zeros_like(dk_ref)
            dv_ref[...] = jnp.zeros_like(dv_ref)

        for h in range(nh):
            sl = slice(h * LANES, (h + 1) * LANES)
            dq_ref[:, sl] = dq[h]
            dk_ref[:, sl] += dk[h]
            dv_ref[:, sl] += dv[h]

    row = pl.BlockSpec((bq, w), lambda i: (i, 0))
    mem = pl.BlockSpec((m, w), lambda i: (0, 0))
    return pl.pallas_call(
        body, name=name, grid=(s // bq,), in_specs=[row, mem, mem, row, row, row], out_specs=(row, mem, mem),
        out_shape=(jax.ShapeDtypeStruct((s, w), F32), jax.ShapeDtypeStruct((m, w), F32),
                   jax.ShapeDtypeStruct((m, w), F32)),
        compiler_params=_params("arbitrary"),
    )(q, k, v, o, lse, do)


def _make_memory_attention(scale, name):
    @jax.custom_vjp
    def attn(q, k, v):
        return _memory_fwd(q, k, v, scale, name + "_fwd")[0]

    def fwd(q, k, v):
        o, lse = _memory_fwd(q, k, v, scale, name + "_fwd")
        return o, (q, k, v, o, lse)

    def bwd(res, do):
        q, k, v, o, lse = res
        return _memory_bwd(q, k, v, o, lse, do, scale, name + "_bwd")

    attn.defvjp(fwd, bwd)
    return attn


def _sink_row(sinks):
    return jnp.repeat(sinks.astype(F32), HEAD_DIM).reshape(1, -1)


def _sink_col(sinks):
    return jnp.repeat(sinks.astype(F32).reshape(-1, 2, 1), BLOCK, axis=1)


def _merge3(os_, ls_, name):
    s, w = os_[0].shape
    bs = _pick(s, (256, 128))

    def body(o1, o2, o3, l1, l2, l3, out_ref, lse_ref):
        a1, a2, a3 = l1[...], l2[...], l3[...]
        m = jnp.maximum(jnp.maximum(a1, a2), a3)
        e1, e2, e3 = jnp.exp(a1 - m), jnp.exp(a2 - m), jnp.exp(a3 - m)
        z = e1 + e2 + e3
        out_ref[...] = (e1 * o1[...] + e2 * o2[...] + e3 * o3[...]) / z
        lse_ref[...] = m + jnp.log(z)

    row = pl.BlockSpec((bs, w), lambda i: (i, 0))
    return pl.pallas_call(
        body, name=name, grid=(s // bs,), in_specs=[row] * 6, out_specs=(row, row),
        out_shape=(jax.ShapeDtypeStruct((s, w), F32), jax.ShapeDtypeStruct((s, w), F32)),
        compiler_params=_params("parallel"),
    )(*os_, *ls_)


def _make_dilated(name):
    scale, max_dist = HEAD_DIM ** -0.5, BLOCK

    def upb_of(dil):
        return 2 * BAND_UNITS_PER_STEP if dil == 1 else 1

    def forward(q, k, v):
        os_, ls_ = [], []
        for n, (_, dil) in enumerate(DIL_PATTERNS):
            o, l = _band_fwd(q, k, v, None, scale, max_dist, upb_of(dil), dil, "%s_b%d_fwd" % (name, n))
            os_.append(o)
            ls_.append(l)
        return _merge3(os_, ls_, name + "_merge")

    @jax.custom_vjp
    def dilated(q, k, v):
        return forward(q, k, v)[0]

    def fwd(q, k, v):
        out, lse = forward(q, k, v)
        return out, (q, k, v, out, lse)

    def bwd(res, do):
        q, k, v, out, lse = res
        dq, dkv = None, None
        for n, (_, dil) in enumerate(DIL_PATTERNS):
            args = (q, k, v, out, lse, do)
            dq = _band_dq(*args, None, scale, max_dist, upb_of(dil), dil, "%s_b%d_dq" % (name, n), acc=dq)
            dkv = _band_dkv(*args, scale, max_dist, upb_of(dil), dil, "%s_b%d_dkv" % (name, n), accs=dkv)
        return dq, dkv[0], dkv[1]

    dilated.defvjp(fwd, bwd)
    return dilated


def _times_w(a, w, orient, name, res=None, rope=None):
    return _mm(a, w, "nn" if orient == "n" else "nt", name, res=res, rope=rope)


def _times_wt(dz, w, orient, name, res=None):
    return _mm(dz, w, "nt" if orient == "n" else "nn", name, res=res)


def _grad_w(a, dz, orient, name):
    if orient == "n":
        return _mm(a, dz, "tn", name, out_dtype=GRAD_WIRE_DTYPE)
    return _mm(dz, a, "tn", name, out_dtype=GRAD_WIRE_DTYPE)


def _make_norm_linear(name, orients, through=False, rope_halves=None):
    nw = len(orients)
    halves = rope_halves or (None,) * nw

    def rope_of(i, ropes):
        return None if halves[i] is None else (ropes[i], halves[i])

    def forward(x, g, ws, ropes):
        h = _rms_fwd(x, g, name + "_norm")
        zs = tuple(_times_w(h, w, o, "%s_mm%d" % (name, i), rope=rope_of(i, ropes))
                   for i, (w, o) in enumerate(zip(ws, orients)))
        return zs + ((x,) if through else ()), h

    @jax.custom_vjp
    def op(x, g, slots, ws, ropes):
        return forward(x, g, ws, ropes)[0]

    def fwd(x, g, slots, ws, ropes):
        outs, h = forward(x, g, ws, ropes)
        return outs, (x, g, h, ws, ropes)

    def bwd(res, cts):
        x, g, h, ws, ropes = res
        dzs = [cts[i] if halves[i] is None else
               _rope_apply(cts[i], ropes[i], halves[i], True, "%s_unrope%d" % (name, i)) for i in range(nw)]
        dh = None
        for i, (w, o) in enumerate(zip(ws, orients)):
            dh = _times_wt(dzs[i], w, o, "%s_dh%d" % (name, i), res=dh)
        dws = tuple(_grad_w(h, dzs[i], o, "%s_dw%d" % (name, i)) for i, o in enumerate(orients))
        dx, dg = _rms_bwd(x, g, dh, name + "_norm_bwd", dres=cts[nw] if through else None)
        return dx, dg, dws, (None,) * nw, tuple(None if r is None else (None,) * len(r) for r in ropes)

    op.defvjp(fwd, bwd)
    return op


def _make_linear_res(name, orient):
    @jax.custom_vjp
    def op(a, wslot, w, res):
        return _times_w(a, w, orient, name + "_mm", res=res)

    def fwd(a, wslot, w, res):
        return _times_w(a, w, orient, name + "_mm", res=res), (a, w)

    def bwd(saved, dout):
        a, w = saved
        return _times_wt(dout, w, orient, name + "_da"), _grad_w(a, dout, orient, name + "_dw"), None, dout

    op.defvjp(fwd, bwd)
    return op


FFN_TILE_M, FFN_TILE_N = 512, 1408


def _gate_up_act(h, wg, wu, name):
    m, k = h.shape
    n = wg.shape[0]
    bm, bn = _div128(m, FFN_TILE_M), _div128(n, FFN_TILE_N)

    def body(h_ref, wg_ref, wu_ref, g_ref, u_ref, a_ref):
        hv = h_ref[...]
        g = _dot(hv, wg_ref[...], _NT)
        u = _dot(hv, wu_ref[...], _NT)
        g_ref[...] = g
        u_ref[...] = u
        a_ref[...] = (g / (1.0 + jnp.exp(-g)) * u).astype(BF16)

    wspec = pl.BlockSpec((bn, k), lambda i, j: (j, 0))
    ospec = pl.BlockSpec((bm, bn), lambda i, j: (i, j))
    return pl.pallas_call(
        body, name=name, grid=(m // bm, n // bn), in_specs=[pl.BlockSpec((bm, k), lambda i, j: (i, 0)), wspec, wspec],
        out_specs=(ospec, ospec, ospec),
        out_shape=(jax.ShapeDtypeStruct((m, n), F32), jax.ShapeDtypeStruct((m, n), F32),
                   jax.ShapeDtypeStruct((m, n), BF16)),
        compiler_params=_params("parallel", "parallel"),
    )(h, wg, wu)


def _down_bwd_act(dout, wd, gmat, umat, name):
    m, k = dout.shape
    n = wd.shape[0]
    bm, bn = _div128(m, FFN_TILE_M), _div128(n, FFN_TILE_N)

    def body(do_ref, wd_ref, g_ref, u_ref, dg_ref, du_ref):
        d = _dot(do_ref[...].astype(BF16), wd_ref[...], _NT)
        g, u = g_ref[...], u_ref[...]
        sig = 1.0 / (1.0 + jnp.exp(-g))
        dg_ref[...] = (d * u * (sig * (1.0 + g * (1.0 - sig)))).astype(BF16)
        du_ref[...] = (d * (g * sig)).astype(BF16)

    ospec = pl.BlockSpec((bm, bn), lambda i, j: (i, j))
    return pl.pallas_call(
        body, name=name, grid=(m // bm, n // bn),
        in_specs=[pl.BlockSpec((bm, k), lambda i, j: (i, 0)), pl.BlockSpec((bn, k), lambda i, j: (j, 0)), ospec, ospec],
        out_specs=(ospec, ospec), out_shape=(jax.ShapeDtypeStruct((m, n), BF16),) * 2,
        compiler_params=_params("parallel", "parallel"),
    )(dout, wd, gmat, umat)


def _make_ffn(name):
    def forward(x, g, wg, wu, wd):
        h = _rms_fwd(x, g, name + "_norm")
        gmat, umat, a = _gate_up_act(h, wg, wu, name + "_gate_up")
        return _mm(a, wd, "nn", name + "_down", res=x), (x, g, h, gmat, umat, a, wg, wu, wd)

    @jax.custom_vjp
    def op(x, g, wg_slot, wu_slot, wd_slot, wg, wu, wd):
        return forward(x, g, wg, wu, wd)[0]

    def fwd(x, g, wg_slot, wu_slot, wd_slot, wg, wu, wd):
        return forward(x, g, wg, wu, wd)

    def bwd(saved, dout):
        x, g, h, gmat, umat, a, wg, wu, wd = saved
        dgm, dum = _down_bwd_act(dout, wd, gmat, umat, name + "_da_act")
        dwd = _mm(a, dout, "tn", name + "_dwd", out_dtype=GRAD_WIRE_DTYPE)
        dwg = _grad_w(h, dgm, "t", name + "_dwg")
        dwu = _grad_w(h, dum, "t", name + "_dwu")
        dh = _times_wt(dum, wu, "t", name + "_dh_u", res=_times_wt(dgm, wg, "t", name + "_dh_g"))
        dx, dg = _rms_bwd(x, g, dh, name + "_norm_bwd", dres=dout)
        return dx, dg, dwg, dwu, dwd, None, None, None

    op.defvjp(fwd, bwd)
    return op


def _make_final_loss(name):
    def run(x, g, tgt):
        s, d = x.shape
        bs = _pick(s, (512, 256, 128))

        def body(x_ref, g_ref, t_ref, loss_ref, dx_ref, dg_ref):
            i = pl.program_id(0)
            xv = x_ref[...]
            gv = g_ref[...]
            r = lax.rsqrt(jnp.mean(xv * xv, axis=-1, keepdims=True) + NORM_EPS)
            xh = xv * r
            e = xh * gv - t_ref[...]
            dy = e * (1.0 / d)
            dxh = dy * gv
            dx_ref[...] = r * (dxh - xh * jnp.mean(dxh * xh, axis=-1, keepdims=True))
            part = 0.5 * jnp.sum(jnp.sum(e * e, axis=-1, keepdims=True) * (1.0 / d), axis=0, keepdims=True)

            @pl.when(i == 0)
            def _():
                loss_ref[...] = jnp.zeros_like(loss_ref)
                dg_ref[...] = jnp.zeros_like(dg_ref)

            loss_ref[...] += jnp.broadcast_to(part, loss_ref.shape)
            dg_ref[...] += jnp.sum(dy * xh, axis=0, keepdims=True)

        row = pl.BlockSpec((bs, d), lambda i: (i, 0))
        vec = pl.BlockSpec((1, d), lambda i: (0, 0))
        loss, dx, dg = pl.pallas_call(
            body, name=name, grid=(s // bs,), in_specs=[row, vec, row],
            out_specs=(pl.BlockSpec((8, LANES), lambda i: (0, 0)), row, vec),
            out_shape=(jax.ShapeDtypeStruct((8, LANES), F32), jax.ShapeDtypeStruct((s, d), F32),
                       jax.ShapeDtypeStruct((1, d), F32)),
            compiler_params=_params("arbitrary"),
        )(x, g.reshape(1, d), tgt)
        return loss[0, 0], dx, dg.reshape(d)

    @jax.custom_vjp
    def op(x, g, tgt):
        return run(x, g, tgt)[0]

    def fwd(x, g, tgt):
        loss, dx, dg = run(x, g, tgt)
        return loss, (dx, dg)

    def bwd(saved, ct):
        dx, dg = saved
        return dx * ct, dg * ct, None

    op.defvjp(fwd, bwd)
    return op


def _model_loss(diff, consts):
    x = diff["x"]
    w = consts["w"]
    slot = diff["slots"]
    vec = diff["vec"]
    tab64, tab_mla = consts["tab64"], consts["tab_mla"]
    mem = consts["mem"]
    s = x.shape[0]

    rope64 = lambda t, nm: _make_rope(HEAD_DIM // 2, nm)(t, *tab64)

    def nl(nm, inp, gain, wnames, through=False, ropes=None):
        orients = tuple(_orient(n) for n in wnames)
        kinds = ropes or (None,) * len(wnames)
        halves = tuple({None: None, "64": HEAD_DIM // 2, "mla": MLA_ROPE_DIM // 2}[r] for r in kinds)
        tabs = tuple({None: None, "64": tab64, "mla": tab_mla}[r] for r in kinds)
        op = _make_norm_linear(nm, orients, through, halves)
        return op(inp, gain, tuple(slot[n] for n in wnames), tuple(w[n] for n in wnames), tabs)

    def lin_res(nm, a, wname, res):
        return _make_linear_res(nm, _orient(wname))(a, slot[wname], w[wname], res)

    def cross(layer, xin):
        p = "l%d_" % layer
        q, xin = nl(p + "xq", xin, vec[p + "x_norm"], (p + "w_xq",), through=True)
        kv, = nl(p + "xkv", mem, vec[p + "mem_norm"], (p + "w_xkv",))
        half = X_HEADS * X_HEAD_DIM
        o = _make_memory_attention(X_HEAD_DIM ** -0.5, p + "xattn")(q, kv[:, :half], kv[:, half:])
        return lin_res(p + "xo", o, p + "w_xo", xin)

    def ffn(layer, xin):
        p = "l%d_" % layer
        names = (p + "w_gate", p + "w_up", p + "w_down")
        return _make_ffn(p + "ffn")(xin, vec[p + "ffn_norm"], *(slot[n] for n in names), *(w[n] for n in names))

    in_parts = tuple(name for name, _, _ in _IN_PARTS) + ("l0_w_in_kr",)
    qa, kva, cq, ckv, kr_lanes, x = nl("l0_in", x, vec["l0_mix_norm"], in_parts, through=True,
                                       ropes=("64", None, None, None, None))
    ka = rope64(kva[:, :A_KV], "l0_rope_ka")
    va = kva[:, A_KV:]
    rep = SWA_HEADS // SWA_KV_HEADS
    expand = lambda t: jnp.broadcast_to(t.reshape(s, SWA_KV_HEADS, 1, HEAD_DIM),
                                        (s, SWA_KV_HEADS, rep, HEAD_DIM)).reshape(s, A_Q)
    swa = _make_band_attention(HEAD_DIM ** -0.5, SWA_WINDOW - 1, BAND_UNITS_PER_STEP, "l0_swa")
    oa = swa(qa, expand(ka), expand(va), vec["l0_sinks"])

    qfull, = nl("l0_uq", cq, vec["l0_q_norm"], ("l0_w_uq_heads",), ropes=("mla",))
    kvb, knope = nl("l0_ukv", ckv, vec["l0_kv_norm"], ("l0_w_ukv", "l0_w_uk_heads"))
    kfull = _make_rope_shared(MLA_ROPE_DIM // 2, "l0_rope_k")(knope, kr_lanes, *tab_mla)
    def reduce_layer1(late_grads):
        return _chip_sums(_layer1_matrix_grads(late_grads), LAYER1, MAT_ROWS - LAYER0_ROWS, "l1_grads")

    mla = _make_causal_attention((MLA_NOPE_DIM + MLA_ROPE_DIM) ** -0.5, _pick(s, (1024, 512, 256, 128)), "l0_mla",
                                 consts["layer1_shapes"], reduce_layer1)
    ob, gathered1, slots1 = mla(qfull, kfull, kvb, consts["shard1"], diff["parts1_slot"])
    w = {**w, **_layer1_weights(_unpack_gathered(gathered1, LAYER1))}
    slot = {**slot, **slots1}
    x = lin_res("l0_out_a", oa, "l0_w_out_swa", x)
    x = lin_res("l0_out_b", ob, "l0_w_out_mla", x)
    x = cross(0, x)
    x = ffn(0, x)

    q, k, v, x = nl("l1_qkv", x, vec["l1_mix_norm"], ("l1_w_q", "l1_w_k", "l1_w_v"), through=True,
                    ropes=("64", "64", None))
    o = _make_dilated("l1_dil")(q, k, v)
    x = lin_res("l1_out", o, "l1_w_out", x)
    x = cross(1, x)
    x = ffn(1, x)

    return _make_final_loss("final_loss")(x, vec["final_norm"], consts["target"])


MESH_IDS = pl.DeviceIdType.MESH
HBM_SPEC = pl.BlockSpec(memory_space=pltpu.HBM)


def _my_place():
    return lax.axis_index("x"), lax.axis_index("y"), lax.axis_index("c")


def _flip(v, bit):
    return 1 - v if bit else v


def _all_gather_rows(shard):
    r, c_ = shard.shape

    def body(x_ref, out_ref, send_sems, recv_sems, local_sem):
        x, y, c = _my_place()
        me, sibling = (x, y, c), (x, y, 1 - c)
        chips = [(1 - x, y), (x, 1 - y), (1 - x, 1 - y)]

        def slot(px, py, pc):
            return out_ref.at[4 * px + 2 * py + pc]

        def copy(k, block, to, src=None):
            return pltpu.make_async_remote_copy(
                src_ref=slot(*block) if src is None else src, dst_ref=slot(*block), send_sem=send_sems.at[k],
                recv_sem=recv_sems.at[k], device_id=to, device_id_type=MESH_IDS)

        mine = pltpu.make_async_copy(x_ref, slot(*me), local_sem)
        mine.start()
        first = [copy(0, me, sibling, src=x_ref)]
        first += [copy(1 + j, me, (*chip, c), src=x_ref) for j, chip in enumerate(chips)]
        for cp in first:
            cp.start()
        passed = [copy(4 + j, (*chip, c), sibling) for j, chip in enumerate(chips)]
        for j, chip in enumerate(chips):
            copy(1 + j, (*chip, c), me).wait_recv()
            passed[j].start()
        copy(0, sibling, me).wait_recv()
        for j, chip in enumerate(chips):
            copy(4 + j, (*chip, 1 - c), me).wait_recv()
        for cp in first + passed:
            cp.wait_send()
        mine.wait()

    return pl.pallas_call(
        body, name="weights_all_gather", out_shape=jax.ShapeDtypeStruct((N_DEV, r, c_), shard.dtype),
        in_specs=[HBM_SPEC], out_specs=HBM_SPEC,
        scratch_shapes=[pltpu.SemaphoreType.DMA((7,)), pltpu.SemaphoreType.DMA((7,)), pltpu.SemaphoreType.DMA],
    )(shard)


N_CHIPS = 4


def _exchange_with_sibling(slabs, name):
    _, nq, r, c_ = slabs.shape

    def body(p_ref, out_ref, send_sem, recv_sem):
        x, y, c = _my_place()
        cp = pltpu.make_async_remote_copy(
            src_ref=p_ref.at[1 - c], dst_ref=out_ref, send_sem=send_sem, recv_sem=recv_sem,
            device_id=(x, y, 1 - c), device_id_type=MESH_IDS)
        cp.start()
        cp.wait_recv()
        cp.wait_send()

    return pl.pallas_call(
        body, name=name, out_shape=jax.ShapeDtypeStruct((nq, r, c_), slabs.dtype),
        in_specs=[HBM_SPEC], out_specs=HBM_SPEC,
        scratch_shapes=[pltpu.SemaphoreType.DMA, pltpu.SemaphoreType.DMA],
    )(slabs)


def _add_pairs(a, b, name):
    nq, r, c_ = a.shape
    br = max(d for d in range(PART_ROW_ALIGN, r + 1, PART_ROW_ALIGN) if r % d == 0 and d <= ADD_PAIRS_MAX_ROWS)

    def body(a_ref, b_ref, o_ref):
        o_ref[...] = (a_ref[...].astype(F32) + b_ref[...].astype(F32)).astype(o_ref.dtype)

    blk = pl.BlockSpec((1, br, c_), lambda q, i: (q, i, 0))
    return pl.pallas_call(
        body, name=name, grid=(nq, r // br), in_specs=[blk, blk], out_specs=blk,
        out_shape=jax.ShapeDtypeStruct(a.shape, a.dtype), compiler_params=_params("parallel", "parallel"),
    )(a, b)


def _chip_exchange_copies(t_ref, out_ref, send_sems, recv_sems, local_sem, arrivals):
    x, y, c = _my_place()
    myq = 2 * x + y
    local = pltpu.make_async_copy(t_ref.at[myq], out_ref.at[myq], local_sem)
    sends, recvs = [], []
    for k in range(1, N_CHIPS):
        px, py = _flip(x, k & 2), _flip(y, k & 1)
        peer = 2 * px + py
        for src, dst, into in ((peer, myq, sends),) + (((myq, peer, recvs),) if arrivals else ()):
            into.append(pltpu.make_async_remote_copy(
                src_ref=t_ref.at[src], dst_ref=out_ref.at[dst], send_sem=send_sems.at[k - 1],
                recv_sem=recv_sems.at[k - 1], device_id=(px, py, c), device_id_type=MESH_IDS))
    return local, sends, recvs


def _exchange_between_chips(slabs):
    def body(t_ref, out_ref, send_sems, recv_sems, local_sem):
        local, sends, recvs = _chip_exchange_copies(t_ref, out_ref, send_sems, recv_sems, local_sem, arrivals=True)
        local.start()
        for cp in sends:
            cp.start()
        for cp in recvs:
            cp.wait_recv()
        for cp in sends:
            cp.wait_send()
        local.wait()

    return pl.pallas_call(
        body, name="grad_exchange_chips", out_shape=jax.ShapeDtypeStruct(slabs.shape, slabs.dtype),
        in_specs=[HBM_SPEC], out_specs=HBM_SPEC,
        scratch_shapes=[pltpu.SemaphoreType.DMA((N_CHIPS - 1,)), pltpu.SemaphoreType.DMA((N_CHIPS - 1,)),
                        pltpu.SemaphoreType.DMA],
    )(slabs)


def _all_reduce_small(v):
    r, c_ = v.shape

    def body(v_ref, out_ref, buf, send_sems, recv_sems):
        x, y, c = _my_place()
        me = 4 * x + 2 * y + c
        buf[me] = v_ref[...]
        sends, recvs = [], []
        for k in range(1, N_DEV):
            px, py, pc = _flip(x, k & 4), _flip(y, k & 2), _flip(c, k & 1)
            peer = 4 * px + 2 * py + pc
            sends.append(pltpu.make_async_remote_copy(
                src_ref=v_ref, dst_ref=buf.at[me], send_sem=send_sems.at[k - 1], recv_sem=recv_sems.at[k - 1],
                device_id=(px, py, pc), device_id_type=MESH_IDS))
            recvs.append(pltpu.make_async_remote_copy(
                src_ref=v_ref, dst_ref=buf.at[peer], send_sem=send_sems.at[k - 1], recv_sem=recv_sems.at[k - 1],
                device_id=(px, py, pc), device_id_type=MESH_IDS))
        for cp in sends:
            cp.start()
        for cp in recvs:
            cp.wait_recv()
        for cp in sends:
            cp.wait_send()
        acc = buf[0]
        for d in range(1, N_DEV):
            acc = acc + buf[d]
        out_ref[...] = acc

    vm = pl.BlockSpec(memory_space=pltpu.VMEM)
    return pl.pallas_call(
        body, name="vector_grad_all_reduce", out_shape=jax.ShapeDtypeStruct((r, c_), F32), in_specs=[vm], out_specs=vm,
        scratch_shapes=[pltpu.VMEM((N_DEV, r, c_), F32), pltpu.SemaphoreType.DMA((7,)), pltpu.SemaphoreType.DMA((7,))],
    )(v)


def _adamw_math(w, g, m, v):
    m = ADAM_B1 * m + (1.0 - ADAM_B1) * g
    v = ADAM_B2 * v + (1.0 - ADAM_B2) * (g * g)
    m_hat = m / (1.0 - ADAM_B1 ** ADAM_STEP)
    v_hat = v / (1.0 - ADAM_B2 ** ADAM_STEP)
    delta = -ADAM_LR * (m_hat / (jnp.sqrt(v_hat) + ADAM_EPS) + ADAM_WD * w)
    return delta, m, v


def _sum_and_adamw(parts, w, m, v):
    nparts, r, c_ = parts.shape
    br = _pick(r, (256, 128, 64, 32, 16, 8))

    def body(p_ref, w_ref, m_ref, v_ref, g_ref, d_ref, nm_ref, nv_ref):
        g = p_ref[0].astype(F32)
        for d in range(1, nparts):
            g = g + p_ref[d].astype(F32)
        g_ref[...] = g
        d_ref[...], nm_ref[...], nv_ref[...] = _adamw_math(w_ref[...], g, m_ref[...], v_ref[...])

    row = pl.BlockSpec((br, c_), lambda i: (i, 0))
    return pl.pallas_call(
        body, name="grad_sum_adamw", grid=(r // br,),
        in_specs=[pl.BlockSpec((nparts, br, c_), lambda i: (0, i, 0)), row, row, row], out_specs=(row,) * 4,
        out_shape=(jax.ShapeDtypeStruct((r, c_), F32),) * 4, compiler_params=_params("parallel"),
    )(parts, w, m, v)


def _adamw_small(w, g, m, v):
    vm = pl.BlockSpec(memory_space=pltpu.VMEM)

    def body(w_ref, g_ref, m_ref, v_ref, d_ref, nm_ref, nv_ref):
        d_ref[...], nm_ref[...], nv_ref[...] = _adamw_math(w_ref[...], g_ref[...], m_ref[...], v_ref[...])

    return pl.pallas_call(
        body, name="vector_adamw", in_specs=[vm] * 4, out_specs=(vm,) * 3,
        out_shape=(jax.ShapeDtypeStruct(w.shape, F32),) * 3,
    )(w, g, m, v)


def _pad_rows(t, axis):
    extra = -t.shape[axis] % PART_ROW_ALIGN
    if extra == 0:
        return t
    widths = [(0, 0)] * t.ndim
    widths[axis] = (0, extra)
    return jnp.pad(t, widths)


def _pack_local(named):
    rows = [_pad_rows((named[n].T if kind == "c" else named[n]).reshape(-1, PACK_COLS), 0)
            for n, kind, _, _ in MATRICES]
    rows.append(jnp.zeros((MAT_ROWS - MAT_ROWS_USED, PACK_COLS), rows[0].dtype))
    return jnp.concatenate(rows, axis=0)


def _unpack_local(packed):
    out, r0 = {}, 0
    for n, kind, k, nn in MATRICES:
        nr = k * nn // N_DEV // PACK_COLS
        part = packed[r0:r0 + nr]
        out[n] = part.reshape(nn // N_DEV, k).T if kind == "c" else part.reshape(k // N_DEV, nn)
        r0 += _part_rows(k, nn)
    return out


def _unpack_gathered(g, matrices):
    out, r0 = {}, 0
    for n, kind, k, nn in matrices:
        nr = k * nn // N_DEV // PACK_COLS
        out[n] = g[:, r0:r0 + nr].reshape((nn, k) if kind == "c" else (k, nn))
        r0 += _part_rows(k, nn)
    return out


def _chip_sums(grads, matrices, rows, name):
    slabs = _pack_full_grads(grads, matrices, rows)
    from_sibling = _exchange_with_sibling(slabs, name + "_exchange_sibling")
    mine = lax.dynamic_index_in_dim(slabs, lax.axis_index("c"), axis=0, keepdims=False)
    return _add_pairs(mine, from_sibling, name + "_chip_sum")


def _pack_full_grads(grads, matrices, rows_total):
    rows = []
    for n, _, k, nn in matrices:
        gmat = grads[n].reshape(N_CHIPS, 2, -1, PACK_COLS).transpose(1, 0, 2, 3)
        rows.append(_pad_rows(gmat, 2))
    used = sum(_part_rows(k, nn) for _, _, k, nn in matrices)
    if rows_total > used:
        rows.append(jnp.zeros((2, N_CHIPS, rows_total - used, PACK_COLS), rows[0].dtype))
    return jnp.concatenate(rows, axis=2)


def _pack_vectors(named):
    rows = [jnp.pad(named[n].astype(F32), (0, PACK_COLS - d)) for n, d in VECTORS]
    rows += [jnp.zeros((PACK_COLS,), F32)] * (VEC_ROWS - len(VECTORS))
    return jnp.stack(rows, axis=0)


def _unpack_vectors(packed):
    return {n: packed[i, :d] for i, (n, d) in enumerate(VECTORS)}


def _step(inputs):
    x = inputs["x"][0]
    mem = inputs["mem"][0]
    positions = inputs["positions"][0]
    target = inputs["loss_target"][0]

    local_w = _pack_local({n: inputs[n] for n, _, _, _ in MATRICES})
    local_bf16 = local_w.astype(BF16)
    w0full = _unpack_gathered(_all_gather_rows(local_bf16[:LAYER0_ROWS]), LAYER0)
    vec = {n: inputs[n] for n, _ in VECTORS}

    loss_part, grad_x, g0full, parts1, gvec = _local_grads(w0full, local_bf16[LAYER0_ROWS:], vec, x, mem, positions,
                                                          target)
    loss = lax.psum(loss_part, ("x", "y", "c"))

    parts0 = _exchange_between_chips(_chip_sums(g0full, LAYER0, LAYER0_ROWS, "l0_grads"))
    parts = jnp.concatenate([parts0, parts1], axis=1)
    local_m = _pack_local({n: inputs["m_" + n] for n, _, _, _ in MATRICES})
    local_v = _pack_local({n: inputs["v_" + n] for n, _, _, _ in MATRICES})
    g_pk, d_pk, m_pk, v_pk = _sum_and_adamw(parts, local_w, local_m, local_v)
    g_mat, d_mat, m_mat, v_mat = (_unpack_local(t) for t in (g_pk, d_pk, m_pk, v_pk))

    g_vec_pk = _all_reduce_small(_pack_vectors(gvec))
    d_vec_pk, m_vec_pk, v_vec_pk = _adamw_small(
        _pack_vectors(vec), g_vec_pk, _pack_vectors({n: inputs["m_" + n] for n, _ in VECTORS}),
        _pack_vectors({n: inputs["v_" + n] for n, _ in VECTORS}))
    g_vec, d_vec, m_vec, v_vec = (_unpack_vectors(t) for t in (g_vec_pk, d_vec_pk, m_vec_pk, v_vec_pk))

    def pick(mats, vecs, n):
        return mats[n] if n in mats else vecs[n]

    outs = [loss, grad_x[None]]
    for mats, vecs in ((g_mat, g_vec), (d_mat, d_vec), (m_mat, m_vec), (v_mat, v_vec)):
        outs += [pick(mats, vecs, n) for n in WEIGHT_ORDER]
    return tuple(outs)


_KIND = {n: kind for n, kind, _, _ in MATRICES}
_VIEW_OF = {"l0_w_uq_heads": "l0_w_uq", "l0_w_uk_heads": "l0_w_ukv", "l0_w_out_swa": "l0_w_out",
            "l0_w_out_mla": "l0_w_out", "l0_w_in_qa": "l0_w_in", "l0_w_in_kva": "l0_w_in", "l0_w_in_cq": "l0_w_in",
            "l0_w_in_ckv": "l0_w_in", "l0_w_in_kr": "l0_w_in", "l1_w_q": "l1_w_qkv", "l1_w_k": "l1_w_qkv",
            "l1_w_v": "l1_w_qkv"}
_IN_PARTS = (("l0_w_in_qa", 0, A_Q), ("l0_w_in_kva", A_Q, A_Q + 2 * A_KV),
             ("l0_w_in_cq", A_Q + 2 * A_KV, A_Q + 2 * A_KV + MLA_Q_RANK),
             ("l0_w_in_ckv", A_Q + 2 * A_KV + MLA_Q_RANK, EVEN_IN - MLA_ROPE_DIM))
_KR_PAD = (MLA_NOPE_DIM, LANES - MLA_NOPE_DIM - MLA_ROPE_DIM)
_MLA_QK = MLA_NOPE_DIM + MLA_ROPE_DIM


def _orient(name):
    return "t" if _KIND[_VIEW_OF.get(name, name)] == "c" else "n"


def _nope_rows():
    return (np.arange(MLA_HEADS * LANES) % LANES < MLA_NOPE_DIM)[:, None]


def _layer1_weights(wfull):
    w = dict(wfull)
    w_qkv = w.pop("l1_w_qkv")
    for i, name in enumerate(("l1_w_q", "l1_w_k", "l1_w_v")):
        w[name] = w_qkv[i * D_MODEL:(i + 1) * D_MODEL]
    return w


def _layer0_weights(wfull):
    w = dict(wfull)
    w_in = w.pop("l0_w_in")
    for name, r0, r1 in _IN_PARTS:
        w[name] = w_in[r0:r1]
    w["l0_w_in_kr"] = jnp.pad(w_in[EVEN_IN - MLA_ROPE_DIM:], (_KR_PAD, (0, 0)))
    uq = w.pop("l0_w_uq").reshape(MLA_HEADS, _MLA_QK, MLA_Q_RANK)
    w["l0_w_uq_heads"] = jnp.pad(uq, ((0, 0), (0, LANES - _MLA_QK), (0, 0))).reshape(MLA_HEADS * LANES, MLA_Q_RANK)
    w["l0_w_uk_heads"] = jnp.where(_nope_rows(), wfull["l0_w_ukv"], jnp.zeros_like(wfull["l0_w_ukv"]))
    wo = w.pop("l0_w_out")
    w["l0_w_out_swa"] = wo[:A_Q]
    w["l0_w_out_mla"] = jnp.pad(wo[A_Q:].reshape(MLA_HEADS, HEAD_DIM, D_MODEL),
                                ((0, 0), (LANES - HEAD_DIM, 0), (0, 0))).reshape(MLA_HEADS * LANES, D_MODEL)
    return w


def _layer1_matrix_grads(g):
    out = {n: g[n] for n, _, _, _ in LAYER1 if n in g}
    out["l1_w_qkv"] = jnp.concatenate([g["l1_w_q"], g["l1_w_k"], g["l1_w_v"]], axis=0)
    return out


def _layer0_matrix_grads(g):
    out = {n: g[n] for n, _, _, _ in LAYER0 if n in g}
    out["l0_w_in"] = jnp.concatenate([g[name] for name, _, _ in _IN_PARTS]
                                     + [g["l0_w_in_kr"][_KR_PAD[0]:_KR_PAD[0] + MLA_ROPE_DIM]], axis=0)
    out["l0_w_uq"] = g["l0_w_uq_heads"].reshape(MLA_HEADS, LANES, MLA_Q_RANK)[:, :_MLA_QK].reshape(-1, MLA_Q_RANK)
    uk = jnp.where(_nope_rows(), g["l0_w_uk_heads"], jnp.zeros_like(g["l0_w_uk_heads"]))
    out["l0_w_ukv"] = (g["l0_w_ukv"].astype(F32) + uk.astype(F32)).astype(g["l0_w_ukv"].dtype)
    out["l0_w_out"] = jnp.concatenate(
        [g["l0_w_out_swa"],
         g["l0_w_out_mla"].reshape(MLA_HEADS, LANES, D_MODEL)[:, LANES - HEAD_DIM:].reshape(-1, D_MODEL)], axis=0)
    return out


def _local_grads(w0full, shard1, vec, x, mem, positions, target):
    w = _layer0_weights(w0full)
    gathered1 = jax.ShapeDtypeStruct((N_DEV,) + shard1.shape, shard1.dtype)
    layer1_shapes = jax.eval_shape(lambda g: _layer1_weights(_unpack_gathered(g, LAYER1)), gathered1)
    slots = {n: jnp.zeros(t.shape, GRAD_WIRE_DTYPE) for n, t in w.items()}
    tab64 = _rope_tables(positions, HEAD_DIM, 0, HEAD_DIM)
    tab_mla = _rope_tables(positions, MLA_ROPE_DIM, MLA_NOPE_DIM, LANES)
    parts1_slot = jnp.zeros((N_CHIPS,) + shard1.shape, GRAD_WIRE_DTYPE)
    diff = {"x": x, "slots": slots, "vec": vec, "parts1_slot": parts1_slot}
    consts = {"w": w, "shard1": shard1, "mem": mem, "tab64": tab64, "tab_mla": tab_mla, "target": target,
              "layer1_shapes": layer1_shapes}
    loss_part, grads = jax.value_and_grad(lambda d: _model_loss(d, consts))(diff)
    return loss_part, grads["x"], _layer0_matrix_grads(grads["slots"]), grads["parts1_slot"], grads["vec"]


_INPUT_NAMES = (("x", "mem", "positions") + WEIGHT_ORDER + ("loss_target",)
                + tuple("m_" + n for n in WEIGHT_ORDER) + tuple("v_" + n for n in WEIGHT_ORDER))


def kernel(*args):
    assert len(args) == len(_INPUT_NAMES)
    return _step(dict(zip(_INPUT_NAMES, args)))
```

```python
import numpy as np
import jax
import jax.numpy as jnp
from jax import lax
from jax.experimental import pallas as pl
from jax.experimental.pallas import tpu as pltpu

F32 = jnp.float32
BF16 = jnp.bfloat16

LANES = 128
VMEM_LIMIT_BYTES = 56 * 1024 * 1024
MM_VMEM_BUDGET = 40 * 1024 * 1024
MM_MIN_FLOP_PER_STEP = 1e9
BAND_UNITS_PER_STEP = 4
CAUSAL_ROW_CHAIN = 128
BAND_CHAINS_PER_BATCH = 4

D_MODEL = 1024
HEAD_DIM = 64
ROPE_THETA = 10000.0
NORM_EPS = 1e-6
BLOCK = 128
SWA_HEADS = 8
SWA_KV_HEADS = 2
SWA_WINDOW = 128
MLA_HEADS = 8
MLA_Q_RANK = 384
MLA_KV_RANK = 256
MLA_NOPE_DIM = 64
MLA_ROPE_DIM = 32
A_Q = SWA_HEADS * HEAD_DIM
A_KV = SWA_KV_HEADS * HEAD_DIM
EVEN_IN = A_Q + 2 * A_KV + MLA_Q_RANK + MLA_KV_RANK + MLA_ROPE_DIM
DIL_PATTERNS = ((128, 1), (512, 4), (2048, 16))
X_HEADS = 4
X_HEAD_DIM = 128

ADAM_LR = 0.001
ADAM_B1 = 0.9
ADAM_B2 = 0.999
ADAM_EPS = 1e-08
ADAM_WD = 0.01
ADAM_STEP = 10

N_DEV = 8
GRAD_WIRE_DTYPE = BF16
NEG_MASK = -1e30
NEG_INIT = -1e20

MATRICES = (
    ("l0_w_in", "c", 1024, 1440), ("l0_w_uq", "c", 384, 768), ("l0_w_ukv", "c", 256, 1024),
    ("l0_w_out", "r", 1024, 1024), ("l0_w_xq", "r", 1024, 512), ("l0_w_xkv", "r", 1024, 1024),
    ("l0_w_xo", "c", 512, 1024), ("l0_w_gate", "c", 1024, 2816), ("l0_w_up", "c", 1024, 2816),
    ("l0_w_down", "r", 2816, 1024),
    ("l1_w_qkv", "c", 1024, 3072), ("l1_w_out", "r", 1024, 1024), ("l1_w_xq", "r", 1024, 512),
    ("l1_w_xkv", "r", 1024, 1024), ("l1_w_xo", "c", 512, 1024), ("l1_w_gate", "c", 1024, 2816),
    ("l1_w_up", "c", 1024, 2816), ("l1_w_down", "r", 2816, 1024),
)
VECTORS = (
    ("l0_mix_norm", 1024), ("l0_sinks", 8), ("l0_q_norm", 384), ("l0_kv_norm", 256), ("l0_x_norm", 1024),
    ("l0_mem_norm", 1024), ("l0_ffn_norm", 1024), ("l1_mix_norm", 1024), ("l1_x_norm", 1024),
    ("l1_mem_norm", 1024), ("l1_ffn_norm", 1024), ("final_norm", 1024),
)
WEIGHT_ORDER = (
    "l0_mix_norm", "l0_w_in", "l0_sinks", "l0_q_norm", "l0_w_uq", "l0_kv_norm", "l0_w_ukv", "l0_w_out", "l0_x_norm",
    "l0_mem_norm", "l0_w_xq", "l0_w_xkv", "l0_w_xo", "l0_ffn_norm", "l0_w_gate", "l0_w_up", "l0_w_down",
    "l1_mix_norm", "l1_w_qkv", "l1_w_out", "l1_x_norm", "l1_mem_norm", "l1_w_xq", "l1_w_xkv", "l1_w_xo",
    "l1_ffn_norm", "l1_w_gate", "l1_w_up", "l1_w_down", "final_norm",
)
PACK_COLS = 1024
PART_ROW_ALIGN = 16
ADD_PAIRS_MAX_ROWS = 2048


def _part_rows(k, n):
    return -(-(k * n // N_DEV // PACK_COLS) // PART_ROW_ALIGN) * PART_ROW_ALIGN


LAYER0 = tuple(mat for mat in MATRICES if mat[0].startswith("l0_"))
LAYER1 = tuple(mat for mat in MATRICES if mat[0].startswith("l1_"))
assert MATRICES == LAYER0 + LAYER1
LAYER0_ROWS = sum(_part_rows(k, n) for _, _, k, n in LAYER0)
MAT_ROWS_USED = sum(_part_rows(k, n) for _, _, k, n in MATRICES)
MAT_ROWS = -(-MAT_ROWS_USED // 256) * 256
VEC_ROWS = 16


def _pick(n, cands):
    for c in cands:
        if n % c == 0:
            return c
    return n


def _params(*sem):
    return pltpu.CompilerParams(dimension_semantics=sem, vmem_limit_bytes=VMEM_LIMIT_BYTES)


_DIMS = {"nn": (((1,), (0,)), ((), ())), "nt": (((1,), (1,)), ((), ())), "tn": (((0,), (0,)), ((), ()))}


def _rotate_block(xv, av, bmv, bpv, half, transpose):
    if transpose:
        return xv * av + pltpu.roll(xv * bmv, LANES - half, 1) + pltpu.roll(xv * bpv, half, 1)
    return xv * av + pltpu.roll(xv, half, 1) * bmv + pltpu.roll(xv, LANES - half, 1) * bpv


def _rotate_tile(t, tabs, half, transpose):
    av, bmv, bpv = tabs
    blocks = [_rotate_block(t[:, c:c + LANES], av, bmv, bpv, half, transpose) for c in range(0, t.shape[1], LANES)]
    return blocks[0] if len(blocks) == 1 else jnp.concatenate(blocks, axis=1)


def _div128(n, cap):
    d = (min(n, cap) // LANES) * LANES
    while d >= LANES:
        if n % d == 0:
            return d
        d -= LANES
    return n


def _mm_vmem_bytes(bm, bn, bk, nk, sa, sb, so, has_res):
    est = 2 * (bm * bk * sa + bk * bn * sb + bm * bn * so) + bm * bn * 4
    est += bm * bn * 4 if nk > 1 else 0
    est += 2 * bm * bn * 4 if has_res else 0
    est += bm * bk * 2 if sa == 4 else 0
    est += bk * bn * 2 if sb == 4 else 0
    return est


def _mm_tiles(m, n, k, sa, sb, so, has_res, mode):
    bn = _div128(n, 1536)
    kcap = 2048 if mode == "tn" else k
    for bm_cap in ((1408, 2816) if mode == "tn" else (512, 1024, 2048)):
        bm = _div128(m, bm_cap)
        bk = (min(k, kcap) // LANES) * LANES
        while bk > LANES and (k % bk or _mm_vmem_bytes(bm, bn, bk, k // bk, sa, sb, so, has_res) > MM_VMEM_BUDGET):
            bk -= LANES
        if 2 * bm * bn * bk >= MM_MIN_FLOP_PER_STEP or bm == m:
            break
    return bm, bn, bk


def _mm(a, b, mode, name, out_dtype=F32, res=None, rope=None):
    if mode == "nn":
        (m, k), (k2, n) = a.shape, b.shape
    elif mode == "nt":
        (m, k), (n, k2) = a.shape, b.shape
    else:
        (k, m), (k2, n) = a.shape, b.shape
    assert k == k2, (name, a.shape, b.shape)
    has_res = res is not None
    bm, bn, bk = _mm_tiles(m, n, k, a.dtype.itemsize, b.dtype.itemsize, jnp.dtype(out_dtype).itemsize, has_res, mode)
    nk = k // bk
    dims = _DIMS[mode]
    a_spec = pl.BlockSpec((bk, bm), lambda i, j, kk: (kk, i)) if mode == "tn" else pl.BlockSpec((bm, bk), lambda i, j, kk: (i, kk))
    b_spec = pl.BlockSpec((bn, bk), lambda i, j, kk: (j, kk)) if mode == "nt" else pl.BlockSpec((bk, bn), lambda i, j, kk: (kk, j))
    o_spec = pl.BlockSpec((bm, bn), lambda i, j, kk: (i, j))

    n_in = 2 + (1 if has_res else 0) + (3 if rope is not None else 0)

    def body(*refs):
        a_ref, b_ref = refs[0], refs[1]
        r_ref = refs[2] if has_res else None
        o_ref = refs[n_in]
        part = lax.dot_general(a_ref[...].astype(BF16), b_ref[...].astype(BF16), dims, preferred_element_type=F32)

        def finish(r):
            if has_res:
                r = r + r_ref[...]
            if rope is not None:
                r = _rotate_tile(r, tuple(t[...] for t in refs[n_in - 3:n_in]), rope[1], False)
            o_ref[...] = r.astype(out_dtype)

        if nk == 1:
            finish(part)
            return
        acc = refs[-1]
        kk = pl.program_id(2)

        @pl.when(kk == 0)
        def _():
            acc[...] = part

        @pl.when(jnp.logical_and(kk > 0, kk < nk - 1))
        def _():
            acc[...] += part

        @pl.when(kk == nk - 1)
        def _():
            finish(acc[...] + part)

    args = (a, b, res) if has_res else (a, b)
    in_specs = [a_spec, b_spec] + ([o_spec] if has_res else [])
    if rope is not None:
        args = args + tuple(rope[0])
        in_specs = in_specs + [pl.BlockSpec((bm, LANES), lambda i, j, kk: (i, 0))] * 3
    return pl.pallas_call(
        body, name=name, grid=(m // bm, n // bn, nk), in_specs=in_specs, out_specs=o_spec,
        out_shape=jax.ShapeDtypeStruct((m, n), out_dtype),
        scratch_shapes=[pltpu.VMEM((bm, bn), F32)] if nk > 1 else [],
        compiler_params=_params("parallel", "parallel", "arbitrary"),
    )(*args)


def _rms_fwd(x, g, name, out_dtype=BF16):
    s, d = x.shape
    bs = _pick(s, (512, 256, 128))

    def body(x_ref, g_ref, o_ref):
        xv = x_ref[...]
        r = lax.rsqrt(jnp.mean(xv * xv, axis=-1, keepdims=True) + NORM_EPS)
        o_ref[...] = ((xv * r) * g_ref[...]).astype(out_dtype)

    return pl.pallas_call(
        body, name=name, grid=(s // bs,),
        in_specs=[pl.BlockSpec((bs, d), lambda i: (i, 0)), pl.BlockSpec((1, d), lambda i: (0, 0))],
        out_specs=pl.BlockSpec((bs, d), lambda i: (i, 0)), out_shape=jax.ShapeDtypeStruct((s, d), out_dtype),
        compiler_params=_params("parallel"),
    )(x, g.reshape(1, d))


def _rms_bwd(x, g, dy, name, dres=None):
    s, d = x.shape
    bs = _pick(s, (512, 256, 128))
    has_res = dres is not None

    def body(*refs):
        if has_res:
            x_ref, g_ref, dy_ref, r_ref, dx_ref, dg_ref = refs
        else:
            x_ref, g_ref, dy_ref, dx_ref, dg_ref = refs
        i = pl.program_id(0)
        xv = x_ref[...]
        dy = dy_ref[...]
        r = lax.rsqrt(jnp.mean(xv * xv, axis=-1, keepdims=True) + NORM_EPS)
        xh = xv * r
        dxh = dy * g_ref[...]
        dx = r * (dxh - xh * jnp.mean(dxh * xh, axis=-1, keepdims=True))
        if has_res:
            dx = dx + r_ref[...]
        dx_ref[...] = dx

        @pl.when(i == 0)
        def _():
            dg_ref[...] = jnp.zeros_like(dg_ref)

        dg_ref[...] += jnp.sum(dy * xh, axis=0, keepdims=True)

    row = pl.BlockSpec((bs, d), lambda i: (i, 0))
    vec = pl.BlockSpec((1, d), lambda i: (0, 0))
    args = (x, g.reshape(1, d), dy) + ((dres,) if has_res else ())
    dx, dg = pl.pallas_call(
        body, name=name, grid=(s // bs,), in_specs=[row, vec, row] + ([row] if has_res else []),
        out_specs=(row, vec), out_shape=(jax.ShapeDtypeStruct((s, d), F32), jax.ShapeDtypeStruct((1, d), F32)),
        compiler_params=_params("arbitrary"),
    )(*args)
    return dx, dg.reshape(d)


def _rope_tables(positions, dh, offset, period):
    role = np.zeros(LANES, np.int32)
    for base in range(0, LANES, period):
        role[base + offset:base + offset + dh // 2] = 1
        role[base + offset + dh // 2:base + offset + dh] = 2
    inv_freq = ROPE_THETA ** (-jnp.arange(0, dh, 2, dtype=F32) / dh)
    one_period = jnp.concatenate([jnp.zeros((offset,), F32), inv_freq, inv_freq,
                                  jnp.zeros((period - offset - dh,), F32)])
    ang = positions.astype(F32)[:, None] * jnp.tile(one_period, LANES // period)[None, :]
    c, s = jnp.cos(ang), jnp.sin(ang)
    role = role[None, :]
    a = jnp.where(role == 0, 1.0, c).astype(F32)
    bm = jnp.where(role == 2, s, 0.0).astype(F32)
    bp = jnp.where(role == 1, -s, 0.0).astype(F32)
    return a, bm, bp


def _rope_apply(x, tabs, half, transpose, name, shared=None, sum_blocks=False):
    s, w = x.shape
    bs = _pick(s, (512, 256, 128))
    nc = w // LANES
    a, bm, bp = tabs
    has_shared = shared is not None

    def body(*refs):
        x_ref, a_ref, bm_ref, bp_ref = refs[:4]
        o_ref = refs[5] if has_shared else refs[4]
        av, bmv, bpv = a_ref[...], bm_ref[...], bp_ref[...]
        total = None
        for c in range(nc):
            sl = slice(c * LANES, (c + 1) * LANES)
            xv = x_ref[:, sl]
            if has_shared:
                xv = xv + refs[4][...]
            out = _rotate_block(xv, av, bmv, bpv, half, transpose)
            o_ref[:, sl] = out
            total = out if total is None else total + out
        if sum_blocks:
            refs[-1][...] = total

    row = pl.BlockSpec((bs, w), lambda i: (i, 0))
    tab = pl.BlockSpec((bs, LANES), lambda i: (i, 0))
    out_shape = jax.ShapeDtypeStruct((s, w), F32)
    return pl.pallas_call(
        body, name=name, grid=(s // bs,), in_specs=[row, tab, tab, tab] + ([tab] if has_shared else []),
        out_specs=(row, tab) if sum_blocks else row,
        out_shape=(out_shape, jax.ShapeDtypeStruct((s, LANES), F32)) if sum_blocks else out_shape,
        compiler_params=_params("parallel"),
    )(x, a, bm, bp, *((shared,) if has_shared else ()))


def _make_rope(half, name):
    @jax.custom_vjp
    def rope(x, a, bm, bp):
        return _rope_apply(x, (a, bm, bp), half, False, name + "_fwd")

    def fwd(x, a, bm, bp):
        return rope(x, a, bm, bp), (a, bm, bp)

    def bwd(tabs, dy):
        return _rope_apply(dy, tabs, half, True, name + "_bwd"), None, None, None

    rope.defvjp(fwd, bwd)
    return rope


def _make_rope_shared(half, name):
    @jax.custom_vjp
    def rope(x, shared, a, bm, bp):
        return _rope_apply(x, (a, bm, bp), half, False, name + "_fwd", shared=shared)

    def fwd(x, shared, a, bm, bp):
        return rope(x, shared, a, bm, bp), (a, bm, bp)

    def bwd(tabs, dy):
        dx, dshared = _rope_apply(dy, tabs, half, True, name + "_bwd", sum_blocks=True)
        return dx, dshared, None, None, None

    rope.defvjp(fwd, bwd)
    return rope


def _lane_masks():
    lane = lax.broadcasted_iota(jnp.int32, (1, LANES), 1)
    lo = lane < HEAD_DIM
    return [lo, jnp.logical_not(lo)]


def _sel(mask, v):
    return jnp.where(mask, v, jnp.zeros_like(v))


_NT = (((1,), (1,)), ((), ()))
_NN = (((1,), (0,)), ((), ()))
_TN = (((0,), (0,)), ((), ()))
_BNT = (((2,), (2,)), ((0,), (0,)))
_BNN = (((2,), (1,)), ((0,), (0,)))


def _dot(a, b, dims):
    return lax.dot_general(a, b, dims, preferred_element_type=F32)


def _band_masks(max_dist):
    assert BLOCK - 1 <= max_dist <= BLOCK
    r = lax.broadcasted_iota(jnp.int32, (BLOCK, BLOCK), 0)
    c = lax.broadcasted_iota(jnp.int32, (BLOCK, BLOCK), 1)
    return (BLOCK + r - c) <= max_dist, r >= c


def _stack_heads(t):
    return jnp.concatenate([t, t], axis=0)


def _head_terms(lms, a, prod, lv):
    t = jnp.sum(_sel(lms[a], prod), axis=-1, keepdims=True)
    lse = jnp.max(jnp.where(lms[a], lv, -jnp.inf), axis=-1, keepdims=True)
    return t, lse


class _Residue:
    def __init__(self, ref, r, dil):
        self.ref, self.rows = ref, pl.ds(r, BLOCK, stride=dil)

    def __getitem__(self, idx):
        return self.ref[self.rows, idx[1]]

    def __setitem__(self, idx, val):
        self.ref[self.rows, idx[1]] = val


def _residues(refs, dil):
    if dil == 1:
        return [tuple(refs)]
    return [tuple(_Residue(x, r, dil) for x in refs) for r in range(dil)]


def _band_fwd(q, k, v, sinkrow, scale, max_dist, upb, dil, name):
    sq, w = q.shape
    rb = BLOCK * dil
    nq, nub, wb = sq // rb, w // (LANES * upb), LANES * upb
    has_sink = sinkrow is not None

    def body(*refs):
        s_ref = refs[5] if has_sink else None
        lms = _lane_masks()
        mprev, mcur = _band_masks(max_dist)
        mprev = jnp.logical_and(mprev, pl.program_id(1) > 0)
        mask2 = _stack_heads(jnp.concatenate([mprev, mcur], axis=1))
        chains = [(rr, slice(u * LANES, (u + 1) * LANES))
                  for rr in _residues(refs[:5] + refs[-2:], dil) for u in range(upb)]
        for g0 in range(0, len(chains), BAND_CHAINS_PER_BATCH):
            group = chains[g0:g0 + BAND_CHAINS_PER_BATCH]
            qs, kcat, vcat, sks = [], [], [], []
            for (q_ref, kp_ref, kc_ref, vp_ref, vc_ref, _, _), sl in group:
                qv = (q_ref[:, sl] * scale).astype(BF16)
                qs.append(jnp.concatenate([_sel(lms[0], qv), _sel(lms[1], qv)], axis=0))
                kcat.append(jnp.concatenate([kp_ref[:, sl].astype(BF16), kc_ref[:, sl].astype(BF16)], axis=0))
                vcat.append(jnp.concatenate([vp_ref[:, sl].astype(BF16), vc_ref[:, sl].astype(BF16)], axis=0))
                if has_sink:
                    sks.append(s_ref[sl.start // LANES])
            qs, kcat, vcat = jnp.stack(qs), jnp.stack(kcat), jnp.stack(vcat)
            sc = jnp.where(mask2[None], _dot(qs, kcat, _BNT), NEG_MASK)
            m = jnp.max(sc, axis=-1, keepdims=True)
            p = jnp.exp(sc - m)
            l = jnp.sum(p, axis=-1, keepdims=True)
            pv = _dot(p.astype(BF16), vcat, _BNN)
            if has_sink:
                sk2 = jnp.stack(sks)
                m_all = jnp.maximum(m, sk2)
                shrink = jnp.exp(m - m_all)
                l = l * shrink + jnp.exp(sk2 - m_all)
                pv, m = pv * shrink, m_all
            o2 = pv / l
            lse2 = m + jnp.log(l)
            for gi, ((_, _, _, _, _, o_ref, l_ref), sl) in enumerate(group):
                o_ref[:, sl] = jnp.where(lms[0], o2[gi, :BLOCK], o2[gi, BLOCK:])
                l_ref[:, sl] = jnp.where(lms[0], lse2[gi, :BLOCK], lse2[gi, BLOCK:])

    cur = pl.BlockSpec((rb, wb), lambda ub, i: (i, ub))
    prev = pl.BlockSpec((rb, wb), lambda ub, i: (jnp.maximum(i - 1, 0), ub))
    in_specs = [cur, prev, cur, prev, cur]
    in_specs += [pl.BlockSpec((upb, 2 * BLOCK, 1), lambda ub, i: (ub, 0, 0))] if has_sink else []
    args = (q, k, k, v, v) + ((sinkrow,) if has_sink else ())
    return pl.pallas_call(
        body, name=name, grid=(nub, nq), in_specs=in_specs, out_specs=(cur, cur),
        out_shape=(jax.ShapeDtypeStruct((sq, w), F32), jax.ShapeDtypeStruct((sq, w), F32)),
        compiler_params=_params("parallel", "parallel"),
    )(*args)


def _band_dq(q, k, v, o, lse, do, sinkrow, scale, max_dist, upb, dil, name, acc=None):
    sq, w = q.shape
    rb = BLOCK * dil
    nq, nub, wb = sq // rb, w // (LANES * upb), LANES * upb
    has_sink = sinkrow is not None

    def body(*refs):
        if has_sink:
            s_ref, dq_block, dsink_ref = refs[8], refs[9], refs[10]
        else:
            dq_block = refs[-1]
        acc_block = refs[8] if acc is not None else dq_block
        i = pl.program_id(1)
        lms = _lane_masks()
        mprev, mcur = _band_masks(max_dist)
        mprev = jnp.logical_and(mprev, i > 0)
        mask2 = _stack_heads(jnp.concatenate([mprev, mcur], axis=1))
        if has_sink:
            @pl.when(i == 0)
            def _():
                dsink_ref[...] = jnp.zeros_like(dsink_ref)

        chains = [(rr, slice(u * LANES, (u + 1) * LANES))
                  for rr in _residues(refs[:8] + (acc_block, dq_block), dil) for u in range(upb)]
        for g0 in range(0, len(chains), BAND_CHAINS_PER_BATCH):
            group = chains[g0:g0 + BAND_CHAINS_PER_BATCH]
            qs, dos, kcat, vcat, t2, lse2 = [], [], [], [], [], []
            for (q_ref, kp_ref, kc_ref, vp_ref, vc_ref, o_ref, l_ref, do_ref, _, _), sl in group:
                qv = (q_ref[:, sl] * scale).astype(BF16)
                dov = do_ref[:, sl]
                prod = dov * o_ref[:, sl]
                dob = dov.astype(BF16)
                lv = l_ref[:, sl]
                (t0, lse0), (t1, lse1) = _head_terms(lms, 0, prod, lv), _head_terms(lms, 1, prod, lv)
                t2.append(jnp.concatenate([t0, t1], axis=0))
                lse2.append(jnp.concatenate([lse0, lse1], axis=0))
                qs.append(jnp.concatenate([_sel(lms[0], qv), _sel(lms[1], qv)], axis=0))
                dos.append(jnp.concatenate([_sel(lms[0], dob), _sel(lms[1], dob)], axis=0))
                kcat.append(jnp.concatenate([kp_ref[:, sl].astype(BF16), kc_ref[:, sl].astype(BF16)], axis=0))
                vcat.append(jnp.concatenate([vp_ref[:, sl].astype(BF16), vc_ref[:, sl].astype(BF16)], axis=0))
                if has_sink:
                    rs = -jnp.exp(s_ref[:, sl] - lv) * jnp.where(lms[0], t0, t1)
                    dsink_ref[0:1, sl] += jnp.sum(rs, axis=0, keepdims=True)
            qs, dos, kcat, vcat = jnp.stack(qs), jnp.stack(dos), jnp.stack(kcat), jnp.stack(vcat)
            p = jnp.exp(jnp.where(mask2[None], _dot(qs, kcat, _BNT), NEG_MASK) - jnp.stack(lse2))
            ds = (p * (_dot(dos, vcat, _BNT) - jnp.stack(t2))).astype(BF16)
            dq2 = _dot(ds, kcat, _BNN) * scale
            for gi, (rr, sl) in enumerate(group):
                dq = jnp.where(lms[0], dq2[gi, :BLOCK], dq2[gi, BLOCK:])
                rr[-1][:, sl] = dq if acc is None else dq + rr[-2][:, sl]

    cur = pl.BlockSpec((rb, wb), lambda ub, i: (i, ub))
    prev = pl.BlockSpec((rb, wb), lambda ub, i: (jnp.maximum(i - 1, 0), ub))
    in_specs = [cur, prev, cur, prev, cur, cur, cur, cur]
    args = (q, k, k, v, v, o, lse, do)
    out_specs, out_shape = cur, jax.ShapeDtypeStruct((sq, w), F32)
    sem = ("parallel", "parallel")
    if has_sink:
        in_specs = in_specs + [pl.BlockSpec((1, wb), lambda ub, i: (0, ub))]
        args = args + (sinkrow,)
        out_specs = (cur, pl.BlockSpec((8, wb), lambda ub, i: (0, ub)))
        out_shape = (out_shape, jax.ShapeDtypeStruct((8, w), F32))
        sem = ("parallel", "arbitrary")
    aliases = {}
    if acc is not None:
        assert not has_sink
        in_specs, args, aliases = in_specs + [cur], args + (acc,), {len(args): 0}
    return pl.pallas_call(
        body, name=name, grid=(nub, nq), in_specs=in_specs, out_specs=out_specs, out_shape=out_shape,
        input_output_aliases=aliases, compiler_params=_params(*sem),
    )(*args)


def _band_dkv(q, k, v, o, lse, do, scale, max_dist, upb, dil, name, accs=None):
    sq, w = q.shape
    rb = BLOCK * dil
    nq, nub, wb = sq // rb, w // (LANES * upb), LANES * upb

    def body(*refs):
        kb = pl.program_id(1)
        lms = _lane_masks()
        key = lax.broadcasted_iota(jnp.int32, (BLOCK, BLOCK), 0)
        qry = lax.broadcasted_iota(jnp.int32, (BLOCK, BLOCK), 1)
        msame = qry >= key
        mnext = jnp.logical_and((BLOCK + qry - key) <= max_dist, kb < nq - 1)
        mask4 = jnp.concatenate([msame, msame, mnext, mnext], axis=1)
        chains =[(rr, slice(u * LANES, (u + 1) * LANES)) for rr in _residues(refs, dil) for u in range(upb)]
        for g0 in range(0, len(chains), BAND_CHAINS_PER_BATCH):
            group = chains[g0:g0 + BAND_CHAINS_PER_BATCH]
            kvs, vvs, qss, doss, t4s, lse4s = [], [], [], [], [], []
            for rr, sl in group:
                k_ref, v_ref, qs_ref, qn_ref, os_ref, on_ref, ls_ref, ln_ref, dos_ref, don_ref = rr[:10]
                kvs.append(k_ref[:, sl].astype(BF16))
                vvs.append(v_ref[:, sl].astype(BF16))
                qparts, doparts, tparts, lparts = [], [], [], []
                for q_ref, o_ref, l_ref, do_ref in ((qs_ref, os_ref, ls_ref, dos_ref),
                                                    (qn_ref, on_ref, ln_ref, don_ref)):
                    qv = (q_ref[:, sl] * scale).astype(BF16)
                    dov = do_ref[:, sl]
                    prod_t = (dov * o_ref[:, sl]).T
                    dob = dov.astype(BF16)
                    lse_t = l_ref[:, sl].T
                    for a in range(2):
                        lanes = slice(a * HEAD_DIM, (a + 1) * HEAD_DIM)
                        qparts.append(_sel(lms[a], qv))
                        doparts.append(_sel(lms[a], dob))
                        tparts.append(jnp.sum(prod_t[lanes, :], axis=0, keepdims=True))
                        lparts.append(lse_t[a * HEAD_DIM:a * HEAD_DIM + 1, :])
                qss.append(jnp.concatenate(qparts, axis=0))
                doss.append(jnp.concatenate(doparts, axis=0))
                t4s.append(jnp.concatenate(tparts, axis=1))
                lse4s.append(jnp.concatenate(lparts, axis=1))
            kv, vv, qs, dos = jnp.stack(kvs), jnp.stack(vvs), jnp.stack(qss), jnp.stack(doss)
            p = jnp.exp(jnp.where(mask4[None], _dot(kv, qs, _BNT), NEG_MASK) - jnp.stack(lse4s))
            ds = (p * (_dot(vv, dos, _BNT) - jnp.stack(t4s))).astype(BF16)
            dv = _dot(p.astype(BF16), dos, _BNN)
            dk = _dot(ds, qs, _BNN)
            for gi, (rr, sl) in enumerate(group):
                rr[-1][:, sl] = dv[gi] if accs is None else dv[gi] + rr[11][:, sl]
                rr[-2][:, sl] = dk[gi] if accs is None else dk[gi] + rr[10][:, sl]

    same = pl.BlockSpec((rb, wb), lambda ub, kb: (kb, ub))
    nxt = pl.BlockSpec((rb, wb), lambda ub, kb: (jnp.minimum(kb + 1, nq - 1), ub))
    return pl.pallas_call(
        body, name=name, grid=(nub, nq),
        in_specs=[same, same, same, nxt, same, nxt, same, nxt, same, nxt] + ([same, same] if accs else []),
        out_specs=(same, same),
        out_shape=(jax.ShapeDtypeStruct((sq, w), F32), jax.ShapeDtypeStruct((sq, w), F32)),
        input_output_aliases={10: 0, 11: 1} if accs else {},
        compiler_params=_params("parallel", "parallel"),
    )(k, v, q, q, o, o, lse, lse, do, do, *(accs or ()))


def _make_band_attention(scale, max_dist, upb, name):
    @jax.custom_vjp
    def attn(q, k, v, sinks):
        return _band_fwd(q, k, v, _sink_col(sinks), scale, max_dist, upb, 1, name + "_fwd")[0]

    def fwd(q, k, v, sinks):
        o, lse = _band_fwd(q, k, v, _sink_col(sinks), scale, max_dist, upb, 1, name + "_fwd")
        return o, (q, k, v, o, lse, sinks)

    def bwd(res, do):
        q, k, v, o, lse, sinks = res
        dq, dsink = _band_dq(q, k, v, o, lse, do, _sink_row(sinks), scale, max_dist, upb, 1, name + "_dq")
        dk, dv = _band_dkv(q, k, v, o, lse, do, scale, max_dist, upb, 1, name + "_dkv")
        return dq, dk, dv, dsink[0].reshape(-1, HEAD_DIM)[:, 0]

    attn.defvjp(fwd, bwd)
    return attn


def _triangle(n, by_key):
    if by_key:
        pairs = [(i, kb) for kb in range(n) for i in range(kb, n)]
    else:
        pairs = [(i, j) for i in range(n) for j in range(i + 1)]
    qi = np.asarray([p[0] for p in pairs], np.int32)
    kj = np.asarray([p[1] for p in pairs], np.int32)
    return jnp.asarray(qi), jnp.asarray(kj)


def _gather_copies(shard_ref, out_ref, send_sems, recv_sems, arrivals):
    x, y, c = _my_place()
    me = 4 * x + 2 * y + c
    sends, recvs = [], []
    for k in range(1, N_DEV):
        px, py, pc = _flip(x, k & 4), _flip(y, k & 2), _flip(c, k & 1)
        peer = 4 * px + 2 * py + pc
        for slot, into in ((me, sends),) + (((peer, recvs),) if arrivals else ()):
            into.append(pltpu.make_async_remote_copy(
                src_ref=shard_ref, dst_ref=out_ref.at[slot], send_sem=send_sems.at[k - 1],
                recv_sem=recv_sems.at[k - 1], device_id=(px, py, pc), device_id_type=MESH_IDS))
    return me, sends, recvs


def _causal_fwd(q, k, v, scale, blk, name, shard=None):
    s, w = q.shape
    nq, nub = s // blk, w // LANES
    qi, kj = _triangle(nq, by_key=False)
    nsteps = qi.shape[0]
    gathers = shard is not None

    def body(qi_ref, kj_ref, q_ref, k_ref, v_ref, *rest):
        if gathers:
            shard_ref, o_ref, l_ref, gath_ref, m_sc, l_sc, acc_sc, send_sems, recv_sems, local_sem = rest
        else:
            o_ref, l_ref, m_sc, l_sc, acc_sc = rest
        t = pl.program_id(1)
        i, j = qi_ref[t], kj_ref[t]

        if gathers:
            ub = pl.program_id(0)

            @pl.when(jnp.logical_and(ub == 0, t == 0))
            def _():
                me, sends, _ = _gather_copies(shard_ref, gath_ref, send_sems, recv_sems, arrivals=False)
                pltpu.make_async_copy(shard_ref, gath_ref.at[me], local_sem).start()
                for cp in sends:
                    cp.start()

        @pl.when(j == 0)
        def _():
            m_sc[...] = jnp.full_like(m_sc, NEG_INIT)
            l_sc[...] = jnp.zeros_like(l_sc)
            acc_sc[...] = jnp.zeros_like(acc_sc)

        def step(diagonal):
            kv, vv = k_ref[...].astype(BF16), v_ref[...].astype(BF16)
            chains = range(0, blk, CAUSAL_ROW_CHAIN)
            width = {c0: (c0 + CAUSAL_ROW_CHAIN if diagonal else blk) for c0 in chains}
            scs = [_dot((q_ref[c0:c0 + CAUSAL_ROW_CHAIN, :] * scale).astype(BF16), kv[:width[c0]], _NT)
                   for c0 in chains]
            m_all, l_all, acc_all = m_sc[...], l_sc[...], acc_sc[...]
            m_out, l_out, acc_out = [], [], []
            for sc, c0 in zip(scs, chains):
                rows = slice(c0, c0 + CAUSAL_ROW_CHAIN)
                if diagonal:
                    r = c0 + lax.broadcasted_iota(jnp.int32, (CAUSAL_ROW_CHAIN, width[c0]), 0)
                    c = lax.broadcasted_iota(jnp.int32, (CAUSAL_ROW_CHAIN, width[c0]), 1)
                    sc = jnp.where(r >= c, sc, NEG_MASK)
                m_prev = m_all[rows]
                m_new = jnp.maximum(m_prev, jnp.max(sc, axis=-1, keepdims=True))
                alpha = jnp.exp(m_prev - m_new)
                p = jnp.exp(sc - m_new)
                l_out.append(alpha * l_all[rows] + jnp.sum(p, axis=-1, keepdims=True))
                m_out.append(m_new)
                acc_out.append(acc_all[rows] * alpha + _dot(p.astype(BF16), vv[:width[c0]], _NN))
            m_sc[...] = jnp.concatenate(m_out, axis=0)
            l_sc[...] = jnp.concatenate(l_out, axis=0)
            acc_sc[...] = jnp.concatenate(acc_out, axis=0)

        @pl.when(j < i)
        def _():
            step(False)

        @pl.when(j == i)
        def _():
            step(True)
            lf = l_sc[...]
            o_ref[...] = acc_sc[...] / lf
            l_ref[...] = jnp.broadcast_to(m_sc[...] + jnp.log(lf), (blk, LANES))

        if gathers:
            @pl.when(jnp.logical_and(pl.program_id(0) == nub - 1, t == nsteps - 1))
            def _():
                me, sends, recvs = _gather_copies(shard_ref, gath_ref, send_sems, recv_sems, arrivals=True)
                for cp in recvs:
                    cp.wait_recv()
                for cp in sends:
                    cp.wait_send()
                pltpu.make_async_copy(shard_ref, gath_ref.at[me], local_sem).wait()

    qspec = pl.BlockSpec((blk, LANES), lambda ub, t, qi_ref, kj_ref: (qi_ref[t], ub))
    kspec = pl.BlockSpec((blk, LANES), lambda ub, t, qi_ref, kj_ref: (kj_ref[t], ub))
    in_specs, out_specs = [qspec, kspec, kspec], (qspec, qspec)
    out_shape = (jax.ShapeDtypeStruct((s, w), F32), jax.ShapeDtypeStruct((s, w), F32))
    scratch = [pltpu.VMEM((blk, 1), F32), pltpu.VMEM((blk, 1), F32), pltpu.VMEM((blk, LANES), F32)]
    args = (qi, kj, q, k, v)
    if gathers:
        in_specs, out_specs = in_specs + [HBM_SPEC], out_specs + (HBM_SPEC,)
        out_shape = out_shape + (jax.ShapeDtypeStruct((N_DEV,) + shard.shape, shard.dtype),)
        scratch = scratch + [pltpu.SemaphoreType.DMA((N_DEV - 1,)), pltpu.SemaphoreType.DMA((N_DEV - 1,)),
                             pltpu.SemaphoreType.DMA]
        args = args + (shard,)
    return pl.pallas_call(
        body, name=name,
        grid_spec=pltpu.PrefetchScalarGridSpec(
            num_scalar_prefetch=2, grid=(nub, nsteps), in_specs=in_specs, out_specs=out_specs, scratch_shapes=scratch),
        out_shape=out_shape, compiler_params=_params("arbitrary", "arbitrary"),
    )(*args)


def _causal_bwd(q, k, v, o, lse, do, scale, blk, name, chip_sums):
    s, w = q.shape
    nq, nub = s // blk, w // LANES
    qi, kj = _triangle(nq, by_key=True)
    nsteps = qi.shape[0]

    def body(qi_ref, kj_ref, q_ref, k_ref, v_ref, o_ref, l_ref, do_ref, t_ref, dq_ref, dk_ref, dv_ref, parts_ref,
             dk_acc, dv_acc, send_sems, recv_sems, local_sem):
        t = pl.program_id(1)
        i, kb = qi_ref[t], kj_ref[t]

        @pl.when(jnp.logical_and(pl.program_id(0) == 0, t == 0))
        def _():
            local, sends, _ = _chip_exchange_copies(t_ref, parts_ref, send_sems, recv_sems, local_sem, arrivals=False)
            local.start()
            for cp in sends:
                cp.start()

        @pl.when(t == 0)
        def _():
            dq_ref[...] = jnp.zeros_like(dq_ref)

        @pl.when(i == kb)
        def _():
            dk_acc[...] = jnp.zeros_like(dk_acc)
            dv_acc[...] = jnp.zeros_like(dv_acc)

        def step(diagonal):
            qv = (q_ref[...] * scale).astype(BF16)
            kv, vv = k_ref[...].astype(BF16), v_ref[...].astype(BF16)
            dov = do_ref[...]
            tsum = jnp.sum(dov * o_ref[...], axis=-1, keepdims=True)
            dob = dov.astype(BF16)
            sc = _dot(qv, kv, _NT)
            if diagonal:
                r = lax.broadcasted_iota(jnp.int32, (blk, blk), 0)
                c = lax.broadcasted_iota(jnp.int32, (blk, blk), 1)
                sc = jnp.where(r >= c, sc, NEG_MASK)
            p = jnp.exp(sc - l_ref[:, 0:1])
            ds = (p * (_dot(dob, vv, _NT) - tsum)).astype(BF16)
            dv_acc[...] += _dot(p.astype(BF16), dob, _TN)
            dk_acc[...] += _dot(ds, qv, _TN)
            rows = pl.ds(pl.multiple_of(i * blk, blk), blk)
            dq_ref[rows, :] += _dot(ds, kv, _NN) * scale

        @pl.when(i == kb)
        def _():
            step(True)

        @pl.when(i > kb)
        def _():
            step(False)

        @pl.when(i == nq - 1)
        def _():
            dk_ref[...] = dk_acc[...]
            dv_ref[...] = dv_acc[...]

        @pl.when(jnp.logical_and(pl.program_id(0) == nub - 1, t == nsteps - 1))
        def _():
            local, sends, recvs = _chip_exchange_copies(t_ref, parts_ref, send_sems, recv_sems, local_sem, arrivals=True)
            for cp in recvs:
                cp.wait_recv()
            for cp in sends:
                cp.wait_send()
            local.wait()

    qspec = pl.BlockSpec((blk, LANES), lambda ub, t, qi_ref, kj_ref: (qi_ref[t], ub))
    kspec = pl.BlockSpec((blk, LANES), lambda ub, t, qi_ref, kj_ref: (kj_ref[t], ub))
    whole = pl.BlockSpec((s, LANES), lambda ub, t, qi_ref, kj_ref: (0, ub))
    out = jax.ShapeDtypeStruct((s, w), F32)
    return pl.pallas_call(
        body, name=name,
        grid_spec=pltpu.PrefetchScalarGridSpec(
            num_scalar_prefetch=2, grid=(nub, nsteps), in_specs=[qspec, kspec, kspec, qspec, qspec, qspec, HBM_SPEC],
            out_specs=(whole, kspec, kspec, HBM_SPEC),
            scratch_shapes=[pltpu.VMEM((blk, LANES), F32), pltpu.VMEM((blk, LANES), F32),
                            pltpu.SemaphoreType.DMA((N_CHIPS - 1,)), pltpu.SemaphoreType.DMA((N_CHIPS - 1,)),
                            pltpu.SemaphoreType.DMA]),
        out_shape=(out, out, out, jax.ShapeDtypeStruct(chip_sums.shape, chip_sums.dtype)),
        compiler_params=_params("arbitrary", "arbitrary"),
    )(qi, kj, q, k, v, o, lse, do, chip_sums)


def _make_causal_attention(scale, blk, name, late_shapes, reduce_late):
    def forward(q, k, v, shard):
        o, lse, gathered = _causal_fwd(q, k, v, scale, blk, name + "_fwd", shard=shard)
        late = {n: jnp.zeros(t.shape, GRAD_WIRE_DTYPE) for n, t in late_shapes.items()}
        return (o, gathered, late), (q, k, v, o, lse)

    @jax.custom_vjp
    def attn(q, k, v, shard, parts_slot):
        return forward(q, k, v, shard)[0]

    def fwd(q, k, v, shard, parts_slot):
        return forward(q, k, v, shard)

    def bwd(res, cts):
        q, k, v, o, lse = res
        do, _, late_grads = cts
        dq, dk, dv, parts = _causal_bwd(q, k, v, o, lse, do, scale, blk, name + "_bwd", reduce_late(late_grads))
        return dq, dk, dv, None, parts

    attn.defvjp(fwd, bwd)
    return attn


_BTN = (((1,), (1,)), ((0,), (0,)))


def _heads(ref, scale=None):
    blocks = []
    for c in range(0, ref.shape[1], LANES):
        t = ref[:, c:c + LANES]
        blocks.append((t if scale is None else t * scale).astype(BF16))
    return jnp.stack(blocks)


def _memory_fwd(q, k, v, scale, name):
    s, w = q.shape
    m = k.shape[0]
    bq = _pick(s, (512, 256, 128))

    def body(q_ref, k_ref, v_ref, o_ref, l_ref):
        sc = _dot(_heads(q_ref, scale), _heads(k_ref), _BNT)
        mx = jnp.max(sc, axis=-1, keepdims=True)
        p = jnp.exp(sc - mx)
        l = jnp.sum(p, axis=-1, keepdims=True)
        o = _dot(p.astype(BF16), _heads(v_ref), _BNN) / l
        lse = mx + jnp.log(l)
        for h in range(w // LANES):
            o_ref[:, h * LANES:(h + 1) * LANES] = o[h]
            l_ref[:, h * LANES:(h + 1) * LANES] = jnp.broadcast_to(lse[h], (bq, LANES))

    row = pl.BlockSpec((bq, w), lambda i: (i, 0))
    mem = pl.BlockSpec((m, w), lambda i: (0, 0))
    return pl.pallas_call(
        body, name=name, grid=(s // bq,), in_specs=[row, mem, mem], out_specs=(row, row),
        out_shape=(jax.ShapeDtypeStruct((s, w), F32), jax.ShapeDtypeStruct((s, w), F32)),
        compiler_params=_params("parallel"),
    )(q, k, v)


def _memory_bwd(q, k, v, o, lse, do, scale, name):
    s, w = q.shape
    m = k.shape[0]
    nh = w // LANES
    bq = _pick(s, (512, 256, 128))

    def body(q_ref, k_ref, v_ref, o_ref, l_ref, do_ref, dq_ref, dk_ref, dv_ref):
        qs, ks, vs = _heads(q_ref, scale), _heads(k_ref), _heads(v_ref)
        dos = _heads(do_ref)
        t = jnp.stack([jnp.sum(do_ref[:, h * LANES:(h + 1) * LANES] * o_ref[:, h * LANES:(h + 1) * LANES],
                               axis=-1, keepdims=True) for h in range(nh)])
        lse = jnp.stack([l_ref[:, h * LANES:h * LANES + 1] for h in range(nh)])
        p = jnp.exp(_dot(qs, ks, _BNT) - lse)
        ds = (p * (_dot(dos, vs, _BNT) - t)).astype(BF16)
        dq = _dot(ds, ks, _BNN) * scale
        dk = _dot(ds, qs, _BTN)
        dv = _dot(p.astype(BF16), dos, _BTN)

        @pl.when(pl.program_id(0) == 0)
        def _():
            dk_ref[...] = jnp.zeros_like(dk_ref)
            dv_ref[...] = jnp.zeros_like(dv_ref)

        for h in range(nh):
            sl = slice(h * LANES, (h + 1) * LANES)
            dq_ref[:, sl] = dq[h]
            dk_ref[:, sl] += dk[h]
            dv_ref[:, sl] += dv[h]

    row = pl.BlockSpec((bq, w), lambda i: (i, 0))
    mem = pl.BlockSpec((m, w), lambda i: (0, 0))
    return pl.pallas_call(
        body, name=name, grid=(s // bq,), in_specs=[row, mem, mem, row, row, row], out_specs=(row, mem, mem),
        out_shape=(jax.ShapeDtypeStruct((s, w), F32), jax.ShapeDtypeStruct((m, w), F32),
                   jax.ShapeDtypeStruct((m, w), F32)),
        compiler_params=_params("arbitrary"),
    )(q, k, v, o, lse, do)


def _make_memory_attention(scale, name):
    @jax.custom_vjp
    def attn(q, k, v):
        return _memory_fwd(q, k, v, scale, name + "_fwd")[0]

    def fwd(q, k, v):
        o, lse = _memory_fwd(q, k, v, scale, name + "_fwd")
        return o, (q, k, v, o, lse)

    def bwd(res, do):
        q, k, v, o, lse = res
        return _memory_bwd(q, k, v, o, lse, do, scale, name + "_bwd")

    attn.defvjp(fwd, bwd)
    return attn


def _sink_row(sinks):
    return jnp.repeat(sinks.astype(F32), HEAD_DIM).reshape(1, -1)


def _sink_col(sinks):
    return jnp.repeat(sinks.astype(F32).reshape(-1, 2, 1), BLOCK, axis=1)


def _merge3(os_, ls_, name):
    s, w = os_[0].shape
    bs = _pick(s, (256, 128))

    def body(o1, o2, o3, l1, l2, l3, out_ref, lse_ref):
        a1, a2, a3 = l1[...], l2[...], l3[...]
        m = jnp.maximum(jnp.maximum(a1, a2), a3)
        e1, e2, e3 = jnp.exp(a1 - m), jnp.exp(a2 - m), jnp.exp(a3 - m)
        z = e1 + e2 + e3
        out_ref[...] = (e1 * o1[...] + e2 * o2[...] + e3 * o3[...]) / z
        lse_ref[...] = m + jnp.log(z)

    row = pl.BlockSpec((bs, w), lambda i: (i, 0))
    return pl.pallas_call(
        body, name=name, grid=(s // bs,), in_specs=[row] * 6, out_specs=(row, row),
        out_shape=(jax.ShapeDtypeStruct((s, w), F32), jax.ShapeDtypeStruct((s, w), F32)),
        compiler_params=_params("parallel"),
    )(*os_, *ls_)


def _make_dilated(name):
    scale, max_dist = HEAD_DIM ** -0.5, BLOCK

    def upb_of(dil):
        return 2 * BAND_UNITS_PER_STEP if dil == 1 else 1

    def forward(q, k, v):
        os_, ls_ = [], []
        for n, (_, dil) in enumerate(DIL_PATTERNS):
            o, l = _band_fwd(q, k, v, None, scale, max_dist, upb_of(dil), dil, "%s_b%d_fwd" % (name, n))
            os_.append(o)
            ls_.append(l)
        return _merge3(os_, ls_, name + "_merge")

    @jax.custom_vjp
    def dilated(q, k, v):
        return forward(q, k, v)[0]

    def fwd(q, k, v):
        out, lse = forward(q, k, v)
        return out, (q, k, v, out, lse)

    def bwd(res, do):
        q, k, v, out, lse = res
        dq, dkv = None, None
        for n, (_, dil) in enumerate(DIL_PATTERNS):
            args = (q, k, v, out, lse, do)
            dq = _band_dq(*args, None, scale, max_dist, upb_of(dil), dil, "%s_b%d_dq" % (name, n), acc=dq)
            dkv = _band_dkv(*args, scale, max_dist, upb_of(dil), dil, "%s_b%d_dkv" % (name, n), accs=dkv)
        return dq, dkv[0], dkv[1]

    dilated.defvjp(fwd, bwd)
    return dilated


def _times_w(a, w, orient, name, res=None, rope=None):
    return _mm(a, w, "nn" if orient == "n" else "nt", name, res=res, rope=rope)


def _times_wt(dz, w, orient, name, res=None):
    return _mm(dz, w, "nt" if orient == "n" else "nn", name, res=res)


def _times_wt_sum(dzs, ws, name):
    m, kdim = dzs[0].shape[0], ws[0].shape[1]
    bm = _div128(m, FFN_TILE_M)
    nw = len(ws)

    def body(*refs):
        total = None
        for i in range(nw):
            part = _dot(refs[i][...].astype(BF16), refs[nw + i][...], _NN)
            total = part if total is None else total + part
        refs[-1][...] = total

    in_specs = [pl.BlockSpec((bm, dz.shape[1]), lambda i: (i, 0)) for dz in dzs]
    in_specs += [pl.BlockSpec(w.shape, lambda i: (0, 0)) for w in ws]
    return pl.pallas_call(
        body, name=name, grid=(m // bm,), in_specs=in_specs, out_specs=pl.BlockSpec((bm, kdim), lambda i: (i, 0)),
        out_shape=jax.ShapeDtypeStruct((m, kdim), F32), compiler_params=_params("parallel"),
    )(*dzs, *ws)


def _grad_w(a, dz, orient, name):
    if orient == "n":
        return _mm(a, dz, "tn", name, out_dtype=GRAD_WIRE_DTYPE)
    return _mm(dz, a, "tn", name, out_dtype=GRAD_WIRE_DTYPE)


def _make_norm_linear(name, orients, through=False, rope_halves=None):
    nw = len(orients)
    halves = rope_halves or (None,) * nw

    def rope_of(i, ropes):
        return None if halves[i] is None else (ropes[i], halves[i])

    def forward(x, g, ws, ropes):
        h = _rms_fwd(x, g, name + "_norm")
        zs = tuple(_times_w(h, w, o, "%s_mm%d" % (name, i), rope=rope_of(i, ropes))
                   for i, (w, o) in enumerate(zip(ws, orients)))
        return zs + ((x,) if through else ()), h

    @jax.custom_vjp
    def op(x, g, slots, ws, ropes):
        return forward(x, g, ws, ropes)[0]

    def fwd(x, g, slots, ws, ropes):
        outs, h = forward(x, g, ws, ropes)
        return outs, (x, g, h, ws, ropes)

    def bwd(res, cts):
        x, g, h, ws, ropes = res
        dzs = [cts[i] if halves[i] is None else
               _rope_apply(cts[i], ropes[i], halves[i], True, "%s_unrope%d" % (name, i)) for i in range(nw)]
        if nw > 1 and all(o == "t" for o in orients):
            dh = _times_wt_sum(dzs, ws, name + "_dh")
        else:
            dh = None
            for i, (w, o) in enumerate(zip(ws, orients)):
                dh = _times_wt(dzs[i], w, o, "%s_dh%d" % (name, i), res=dh)
        dws = tuple(_grad_w(h, dzs[i], o, "%s_dw%d" % (name, i)) for i, o in enumerate(orients))
        dx, dg = _rms_bwd(x, g, dh, name + "_norm_bwd", dres=cts[nw] if through else None)
        return dx, dg, dws, (None,) * nw, tuple(None if r is None else (None,) * len(r) for r in ropes)

    op.defvjp(fwd, bwd)
    return op


def _make_linear_res(name, orient):
    @jax.custom_vjp
    def op(a, wslot, w, res):
        return _times_w(a, w, orient, name + "_mm", res=res)

    def fwd(a, wslot, w, res):
        return _times_w(a, w, orient, name + "_mm", res=res), (a, w)

    def bwd(saved, dout):
        a, w = saved
        return _times_wt(dout, w, orient, name + "_da"), _grad_w(a, dout, orient, name + "_dw"), None, dout

    op.defvjp(fwd, bwd)
    return op


FFN_TILE_M, FFN_TILE_N = 512, 1408


def _gate_up_act(h, wg, wu, name):
    m, k = h.shape
    n = wg.shape[0]
    bm, bn = _div128(m, FFN_TILE_M), _div128(n, FFN_TILE_N)

    def body(h_ref, wg_ref, wu_ref, g_ref, u_ref, a_ref):
        hv = h_ref[...]
        g = _dot(hv, wg_ref[...], _NT)
        u = _dot(hv, wu_ref[...], _NT)
        g_ref[...] = g
        u_ref[...] = u
        a_ref[...] = (g / (1.0 + jnp.exp(-g)) * u).astype(BF16)

    wspec = pl.BlockSpec((bn, k), lambda i, j: (j, 0))
    ospec = pl.BlockSpec((bm, bn), lambda i, j: (i, j))
    return pl.pallas_call(
        body, name=name, grid=(m // bm, n // bn), in_specs=[pl.BlockSpec((bm, k), lambda i, j: (i, 0)), wspec, wspec],
        out_specs=(ospec, ospec, ospec),
        out_shape=(jax.ShapeDtypeStruct((m, n), F32), jax.ShapeDtypeStruct((m, n), F32),
                   jax.ShapeDtypeStruct((m, n), BF16)),
        compiler_params=_params("parallel", "parallel"),
    )(h, wg, wu)


def _down_bwd_act(dout, wd, gmat, umat, name):
    m, k = dout.shape
    n = wd.shape[0]
    bm, bn = _div128(m, FFN_TILE_M), _div128(n, FFN_TILE_N)

    def body(do_ref, wd_ref, g_ref, u_ref, dg_ref, du_ref):
        d = _dot(do_ref[...].astype(BF16), wd_ref[...], _NT)
        g, u = g_ref[...], u_ref[...]
        sig = 1.0 / (1.0 + jnp.exp(-g))
        dg_ref[...] = (d * u * (sig * (1.0 + g * (1.0 - sig)))).astype(BF16)
        du_ref[...] = (d * (g * sig)).astype(BF16)

    ospec = pl.BlockSpec((bm, bn), lambda i, j: (i, j))
    return pl.pallas_call(
        body, name=name, grid=(m // bm, n // bn),
        in_specs=[pl.BlockSpec((bm, k), lambda i, j: (i, 0)), pl.BlockSpec((bn, k), lambda i, j: (j, 0)), ospec, ospec],
        out_specs=(ospec, ospec), out_shape=(jax.ShapeDtypeStruct((m, n), BF16),) * 2,
        compiler_params=_params("parallel", "parallel"),
    )(dout, wd, gmat, umat)


def _make_ffn(name):
    def forward(x, g, wg, wu, wd):
        h = _rms_fwd(x, g, name + "_norm")
        gmat, umat, a = _gate_up_act(h, wg, wu, name + "_gate_up")
        return _mm(a, wd, "nn", name + "_down", res=x), (x, g, h, gmat, umat, a, wg, wu, wd)

    @jax.custom_vjp
    def op(x, g, wg_slot, wu_slot, wd_slot, wg, wu, wd):
        return forward(x, g, wg, wu, wd)[0]

    def fwd(x, g, wg_slot, wu_slot, wd_slot, wg, wu, wd):
        return forward(x, g, wg, wu, wd)

    def bwd(saved, dout):
        x, g, h, gmat, umat, a, wg, wu, wd = saved
        dgm, dum = _down_bwd_act(dout, wd, gmat, umat, name + "_da_act")
        dwd = _mm(a, dout, "tn", name + "_dwd", out_dtype=GRAD_WIRE_DTYPE)
        dwg = _grad_w(h, dgm, "t", name + "_dwg")
        dwu = _grad_w(h, dum, "t", name + "_dwu")
        dh = _times_wt(dum, wu, "t", name + "_dh_u", res=_times_wt(dgm, wg, "t", name + "_dh_g"))
        dx, dg = _rms_bwd(x, g, dh, name + "_norm_bwd", dres=dout)
        return dx, dg, dwg, dwu, dwd, None, None, None

    op.defvjp(fwd, bwd)
    return op


def _make_final_loss(name):
    def run(x, g, tgt):
        s, d = x.shape
        bs = _pick(s, (512, 256, 128))

        def body(x_ref, g_ref, t_ref, loss_ref, dx_ref, dg_ref):
            i = pl.program_id(0)
            xv = x_ref[...]
            gv = g_ref[...]
            r = lax.rsqrt(jnp.mean(xv * xv, axis=-1, keepdims=True) + NORM_EPS)
            xh = xv * r
            e = xh * gv - t_ref[...]
            dy = e * (1.0 / d)
            dxh = dy * gv
            dx_ref[...] = r * (dxh - xh * jnp.mean(dxh * xh, axis=-1, keepdims=True))
            part = 0.5 * jnp.sum(jnp.sum(e * e, axis=-1, keepdims=True) * (1.0 / d), axis=0, keepdims=True)

            @pl.when(i == 0)
            def _():
                loss_ref[...] = jnp.zeros_like(loss_ref)
                dg_ref[...] = jnp.zeros_like(dg_ref)

            loss_ref[...] += jnp.broadcast_to(part, loss_ref.shape)
            dg_ref[...] += jnp.sum(dy * xh, axis=0, keepdims=True)

        row = pl.BlockSpec((bs, d), lambda i: (i, 0))
        vec = pl.BlockSpec((1, d), lambda i: (0, 0))
        loss, dx, dg = pl.pallas_call(
            body, name=name, grid=(s // bs,), in_specs=[row, vec, row],
            out_specs=(pl.BlockSpec((8, LANES), lambda i: (0, 0)), row, vec),
            out_shape=(jax.ShapeDtypeStruct((8, LANES), F32), jax.ShapeDtypeStruct((s, d), F32),
                       jax.ShapeDtypeStruct((1, d), F32)),
            compiler_params=_params("arbitrary"),
        )(x, g.reshape(1, d), tgt)
        return loss[0, 0], dx, dg.reshape(d)

    @jax.custom_vjp
    def op(x, g, tgt):
        return run(x, g, tgt)[0]

    def fwd(x, g, tgt):
        loss, dx, dg = run(x, g, tgt)
        return loss, (dx, dg)

    def bwd(saved, ct):
        dx, dg = saved
        return dx * ct, dg * ct, None

    op.defvjp(fwd, bwd)
    return op


def _model_loss(diff, consts):
    x = diff["x"]
    w = consts["w"]
    slot = diff["slots"]
    vec = diff["vec"]
    tab64, tab_mla = consts["tab64"], consts["tab_mla"]
    mem = consts["mem"]
    s = x.shape[0]

    rope64 = lambda t, nm: _make_rope(HEAD_DIM // 2, nm)(t, *tab64)

    def nl(nm, inp, gain, wnames, through=False, ropes=None):
        orients = tuple(_orient(n) for n in wnames)
        kinds = ropes or (None,) * len(wnames)
        halves = tuple({None: None, "64": HEAD_DIM // 2, "mla": MLA_ROPE_DIM // 2}[r] for r in kinds)
        tabs = tuple({None: None, "64": tab64, "mla": tab_mla}[r] for r in kinds)
        op = _make_norm_linear(nm, orients, through, halves)
        return op(inp, gain, tuple(slot[n] for n in wnames), tuple(w[n] for n in wnames), tabs)

    def lin_res(nm, a, wname, res):
        return _make_linear_res(nm, _orient(wname))(a, slot[wname], w[wname], res)

    def cross(layer, xin):
        p = "l%d_" % layer
        q, xin = nl(p + "xq", xin, vec[p + "x_norm"], (p + "w_xq",), through=True)
        kv, = nl(p + "xkv", mem, vec[p + "mem_norm"], (p + "w_xkv",))
        half = X_HEADS * X_HEAD_DIM
        o = _make_memory_attention(X_HEAD_DIM ** -0.5, p + "xattn")(q, kv[:, :half], kv[:, half:])
        return lin_res(p + "xo", o, p + "w_xo", xin)

    def ffn(layer, xin):
        p = "l%d_" % layer
        names = (p + "w_gate", p + "w_up", p + "w_down")
        return _make_ffn(p + "ffn")(xin, vec[p + "ffn_norm"], *(slot[n] for n in names), *(w[n] for n in names))

    in_parts = tuple(name for name, _, _ in _IN_PARTS) + ("l0_w_in_kr",)
    qa, kva, cq, ckv, kr_lanes, x = nl("l0_in", x, vec["l0_mix_norm"], in_parts, through=True,
                                       ropes=("64", None, None, None, None))
    ka = rope64(kva[:, :A_KV], "l0_rope_ka")
    va = kva[:, A_KV:]
    rep = SWA_HEADS // SWA_KV_HEADS
    expand = lambda t: jnp.broadcast_to(t.reshape(s, SWA_KV_HEADS, 1, HEAD_DIM),
                                        (s, SWA_KV_HEADS, rep, HEAD_DIM)).reshape(s, A_Q)
    swa = _make_band_attention(HEAD_DIM ** -0.5, SWA_WINDOW - 1, BAND_UNITS_PER_STEP, "l0_swa")
    oa = swa(qa, expand(ka), expand(va), vec["l0_sinks"])

    qfull, = nl("l0_uq", cq, vec["l0_q_norm"], ("l0_w_uq_heads",), ropes=("mla",))
    kvb, knope = nl("l0_ukv", ckv, vec["l0_kv_norm"], ("l0_w_ukv", "l0_w_uk_heads"))
    kfull = _make_rope_shared(MLA_ROPE_DIM // 2, "l0_rope_k")(knope, kr_lanes, *tab_mla)
    def reduce_layer1(late_grads):
        return _chip_sums(_layer1_matrix_grads(late_grads), LAYER1, MAT_ROWS - LAYER0_ROWS, "l1_grads")

    mla = _make_causal_attention((MLA_NOPE_DIM + MLA_ROPE_DIM) ** -0.5, _pick(s, (1024, 512, 256, 128)), "l0_mla",
                                 consts["layer1_shapes"], reduce_layer1)
    ob, gathered1, slots1 = mla(qfull, kfull, kvb, consts["shard1"], diff["parts1_slot"])
    w = {**w, **_layer1_weights(_unpack_gathered(gathered1, LAYER1))}
    slot = {**slot, **slots1}
    x = lin_res("l0_out_a", oa, "l0_w_out_swa", x)
    x = lin_res("l0_out_b", ob, "l0_w_out_mla", x)
    x = cross(0, x)
    x = ffn(0, x)

    q, k, v, x = nl("l1_qkv", x, vec["l1_mix_norm"], ("l1_w_q", "l1_w_k", "l1_w_v"), through=True,
                    ropes=("64", "64", None))
    o = _make_dilated("l1_dil")(q, k, v)
    x = lin_res("l1_out", o, "l1_w_out", x)
    x = cross(1, x)
    x = ffn(1, x)

    return _make_final_loss("final_loss")(x, vec["final_norm"], consts["target"])


MESH_IDS = pl.DeviceIdType.MESH
HBM_SPEC = pl.BlockSpec(memory_space=pltpu.HBM)


def _my_place():
    return lax.axis_index("x"), lax.axis_index("y"), lax.axis_index("c")


def _flip(v, bit):
    return 1 - v if bit else v


def _all_gather_rows(shard):
    r, c_ = shard.shape

    def body(x_ref, out_ref, send_sems, recv_sems, local_sem):
        x, y, c = _my_place()
        me, sibling = (x, y, c), (x, y, 1 - c)
        chips = [(1 - x, y), (x, 1 - y), (1 - x, 1 - y)]

        def slot(px, py, pc):
            return out_ref.at[4 * px + 2 * py + pc]

        def copy(k, block, to, src=None):
            return pltpu.make_async_remote_copy(
                src_ref=slot(*block) if src is None else src, dst_ref=slot(*block), send_sem=send_sems.at[k],
                recv_sem=recv_sems.at[k], device_id=to, device_id_type=MESH_IDS)

        mine = pltpu.make_async_copy(x_ref, slot(*me), local_sem)
        mine.start()
        first = [copy(0, me, sibling, src=x_ref)]
        first += [copy(1 + j, me, (*chip, c), src=x_ref) for j, chip in enumerate(chips)]
        for cp in first:
            cp.start()
        passed = [copy(4 + j, (*chip, c), sibling) for j, chip in enumerate(chips)]
        for j, chip in enumerate(chips):
            copy(1 + j, (*chip, c), me).wait_recv()
            passed[j].start()
        copy(0, sibling, me).wait_recv()
        for j, chip in enumerate(chips):
            copy(4 + j, (*chip, 1 - c), me).wait_recv()
        for cp in first + passed:
            cp.wait_send()
        mine.wait()

    return pl.pallas_call(
        body, name="weights_all_gather", out_shape=jax.ShapeDtypeStruct((N_DEV, r, c_), shard.dtype),
        in_specs=[HBM_SPEC], out_specs=HBM_SPEC,
        scratch_shapes=[pltpu.SemaphoreType.DMA((7,)), pltpu.SemaphoreType.DMA((7,)), pltpu.SemaphoreType.DMA],
    )(shard)


N_CHIPS = 4


def _exchange_with_sibling(slabs, name):
    _, nq, r, c_ = slabs.shape

    def body(p_ref, out_ref, send_sem, recv_sem):
        x, y, c = _my_place()
        cp = pltpu.make_async_remote_copy(
            src_ref=p_ref.at[1 - c], dst_ref=out_ref, send_sem=send_sem, recv_sem=recv_sem,
            device_id=(x, y, 1 - c), device_id_type=MESH_IDS)
        cp.start()
        cp.wait_recv()
        cp.wait_send()

    return pl.pallas_call(
        body, name=name, out_shape=jax.ShapeDtypeStruct((nq, r, c_), slabs.dtype),
        in_specs=[HBM_SPEC], out_specs=HBM_SPEC,
        scratch_shapes=[pltpu.SemaphoreType.DMA, pltpu.SemaphoreType.DMA],
    )(slabs)


def _add_pairs(a, b, name):
    nq, r, c_ = a.shape
    br = max(d for d in range(PART_ROW_ALIGN, r + 1, PART_ROW_ALIGN) if r % d == 0 and d <= ADD_PAIRS_MAX_ROWS)

    def body(a_ref, b_ref, o_ref):
        o_ref[...] = (a_ref[...].astype(F32) + b_ref[...].astype(F32)).astype(o_ref.dtype)

    blk = pl.BlockSpec((1, br, c_), lambda q, i: (q, i, 0))
    return pl.pallas_call(
        body, name=name, grid=(nq, r // br), in_specs=[blk, blk], out_specs=blk,
        out_shape=jax.ShapeDtypeStruct(a.shape, a.dtype), compiler_params=_params("parallel", "parallel"),
    )(a, b)


def _chip_exchange_copies(t_ref, out_ref, send_sems, recv_sems, local_sem, arrivals):
    x, y, c = _my_place()
    myq = 2 * x + y
    local = pltpu.make_async_copy(t_ref.at[myq], out_ref.at[myq], local_sem)
    sends, recvs = [], []
    for k in range(1, N_CHIPS):
        px, py = _flip(x, k & 2), _flip(y, k & 1)
        peer = 2 * px + py
        for src, dst, into in ((peer, myq, sends),) + (((myq, peer, recvs),) if arrivals else ()):
            into.append(pltpu.make_async_remote_copy(
                src_ref=t_ref.at[src], dst_ref=out_ref.at[dst], send_sem=send_sems.at[k - 1],
                recv_sem=recv_sems.at[k - 1], device_id=(px, py, c), device_id_type=MESH_IDS))
    return local, sends, recvs


def _exchange_between_chips(slabs):
    def body(t_ref, out_ref, send_sems, recv_sems, local_sem):
        local, sends, recvs = _chip_exchange_copies(t_ref, out_ref, send_sems, recv_sems, local_sem, arrivals=True)
        local.start()
        for cp in sends:
            cp.start()
        for cp in recvs:
            cp.wait_recv()
        for cp in sends:
            cp.wait_send()
        local.wait()

    return pl.pallas_call(
        body, name="grad_exchange_chips", out_shape=jax.ShapeDtypeStruct(slabs.shape, slabs.dtype),
        in_specs=[HBM_SPEC], out_specs=HBM_SPEC,
        scratch_shapes=[pltpu.SemaphoreType.DMA((N_CHIPS - 1,)), pltpu.SemaphoreType.DMA((N_CHIPS - 1,)),
                        pltpu.SemaphoreType.DMA],
    )(slabs)


def _all_reduce_small(v):
    r, c_ = v.shape

    def body(v_ref, out_ref, buf, send_sems, recv_sems):
        x, y, c = _my_place()
        me = 4 * x + 2 * y + c
        buf[me] = v_ref[...]
        sends, recvs = [], []
        for k in range(1, N_DEV):
            px, py, pc = _flip(x, k & 4), _flip(y, k & 2), _flip(c, k & 1)
            peer = 4 * px + 2 * py + pc
            sends.append(pltpu.make_async_remote_copy(
                src_ref=v_ref, dst_ref=buf.at[me], send_sem=send_sems.at[k - 1], recv_sem=recv_sems.at[k - 1],
                device_id=(px, py, pc), device_id_type=MESH_IDS))
            recvs.append(pltpu.make_async_remote_copy(
                src_ref=v_ref, dst_ref=buf.at[peer], send_sem=send_sems.at[k - 1], recv_sem=recv_sems.at[k - 1],
                device_id=(px, py, pc), device_id_type=MESH_IDS))
        for cp in sends:
            cp.start()
        for cp in recvs:
            cp.wait_recv()
        for cp in sends:
            cp.wait_send()
        acc = buf[0]
        for d in range(1, N_DEV):
            acc = acc + buf[d]
        out_ref[...] = acc

    vm = pl.BlockSpec(memory_space=pltpu.VMEM)
    return pl.pallas_call(
        body, name="vector_grad_all_reduce", out_shape=jax.ShapeDtypeStruct((r, c_), F32), in_specs=[vm], out_specs=vm,
        scratch_shapes=[pltpu.VMEM((N_DEV, r, c_), F32), pltpu.SemaphoreType.DMA((7,)), pltpu.SemaphoreType.DMA((7,))],
    )(v)


def _adamw_math(w, g, m, v):
    m = ADAM_B1 * m + (1.0 - ADAM_B1) * g
    v = ADAM_B2 * v + (1.0 - ADAM_B2) * (g * g)
    m_hat = m / (1.0 - ADAM_B1 ** ADAM_STEP)
    v_hat = v / (1.0 - ADAM_B2 ** ADAM_STEP)
    delta = -ADAM_LR * (m_hat / (jnp.sqrt(v_hat) + ADAM_EPS) + ADAM_WD * w)
    return delta, m, v


def _sum_and_adamw(parts, w, m, v):
    nparts, r, c_ = parts.shape
    br = _pick(r, (256, 128, 64, 32, 16, 8))

    def body(p_ref, w_ref, m_ref, v_ref, g_ref, d_ref, nm_ref, nv_ref):
        g = p_ref[0].astype(F32)
        for d in range(1, nparts):
            g = g + p_ref[d].astype(F32)
        g_ref[...] = g
        d_ref[...], nm_ref[...], nv_ref[...] = _adamw_math(w_ref[...], g, m_ref[...], v_ref[...])

    row = pl.BlockSpec((br, c_), lambda i: (i, 0))
    return pl.pallas_call(
        body, name="grad_sum_adamw", grid=(r // br,),
        in_specs=[pl.BlockSpec((nparts, br, c_), lambda i: (0, i, 0)), row, row, row], out_specs=(row,) * 4,
        out_shape=(jax.ShapeDtypeStruct((r, c_), F32),) * 4, compiler_params=_params("parallel"),
    )(parts, w, m, v)


def _adamw_small(w, g, m, v):
    vm = pl.BlockSpec(memory_space=pltpu.VMEM)

    def body(w_ref, g_ref, m_ref, v_ref, d_ref, nm_ref, nv_ref):
        d_ref[...], nm_ref[...], nv_ref[...] = _adamw_math(w_ref[...], g_ref[...], m_ref[...], v_ref[...])

    return pl.pallas_call(
        body, name="vector_adamw", in_specs=[vm] * 4, out_specs=(vm,) * 3,
        out_shape=(jax.ShapeDtypeStruct(w.shape, F32),) * 3,
    )(w, g, m, v)


def _pad_rows(t, axis):
    extra = -t.shape[axis] % PART_ROW_ALIGN
    if extra == 0:
        return t
    widths = [(0, 0)] * t.ndim
    widths[axis] = (0, extra)
    return jnp.pad(t, widths)


def _pack_local(named):
    rows = [_pad_rows((named[n].T if kind == "c" else named[n]).reshape(-1, PACK_COLS), 0)
            for n, kind, _, _ in MATRICES]
    rows.append(jnp.zeros((MAT_ROWS - MAT_ROWS_USED, PACK_COLS), rows[0].dtype))
    return jnp.concatenate(rows, axis=0)


def _unpack_local(packed):
    out, r0 = {}, 0
    for n, kind, k, nn in MATRICES:
        nr = k * nn // N_DEV // PACK_COLS
        part = packed[r0:r0 + nr]
        out[n] = part.reshape(nn // N_DEV, k).T if kind == "c" else part.reshape(k // N_DEV, nn)
        r0 += _part_rows(k, nn)
    return out


def _unpack_gathered(g, matrices):
    out, r0 = {}, 0
    for n, kind, k, nn in matrices:
        nr = k * nn // N_DEV // PACK_COLS
        out[n] = g[:, r0:r0 + nr].reshape((nn, k) if kind == "c" else (k, nn))
        r0 += _part_rows(k, nn)
    return out


def _chip_sums(grads, matrices, rows, name):
    slabs = _pack_full_grads(grads, matrices, rows)
    from_sibling = _exchange_with_sibling(slabs, name + "_exchange_sibling")
    mine = lax.dynamic_index_in_dim(slabs, lax.axis_index("c"), axis=0, keepdims=False)
    return _add_pairs(mine, from_sibling, name + "_chip_sum")


def _pack_full_grads(grads, matrices, rows_total):
    rows = []
    for n, _, k, nn in matrices:
        gmat = grads[n].reshape(N_CHIPS, 2, -1, PACK_COLS).transpose(1, 0, 2, 3)
        rows.append(_pad_rows(gmat, 2))
    used = sum(_part_rows(k, nn) for _, _, k, nn in matrices)
    if rows_total > used:
        rows.append(jnp.zeros((2, N_CHIPS, rows_total - used, PACK_COLS), rows[0].dtype))
    return jnp.concatenate(rows, axis=2)


def _pack_vectors(named):
    rows = [jnp.pad(named[n].astype(F32), (0, PACK_COLS - d)) for n, d in VECTORS]
    rows += [jnp.zeros((PACK_COLS,), F32)] * (VEC_ROWS - len(VECTORS))
    return jnp.stack(rows, axis=0)


def _unpack_vectors(packed):
    return {n: packed[i, :d] for i, (n, d) in enumerate(VECTORS)}


def _step(inputs):
    x = inputs["x"][0]
    mem = inputs["mem"][0]
    positions = inputs["positions"][0]
    target = inputs["loss_target"][0]

    local_w = _pack_local({n: inputs[n] for n, _, _, _ in MATRICES})
    local_bf16 = local_w.astype(BF16)
    w0full = _unpack_gathered(_all_gather_rows(local_bf16[:LAYER0_ROWS]), LAYER0)
    vec = {n: inputs[n] for n, _ in VECTORS}

    loss_part, grad_x, g0full, parts1, gvec = _local_grads(w0full, local_bf16[LAYER0_ROWS:], vec, x, mem, positions,
                                                          target)
    loss = lax.psum(loss_part, ("x", "y", "c"))

    parts0 = _exchange_between_chips(_chip_sums(g0full, LAYER0, LAYER0_ROWS, "l0_grads"))
    parts = jnp.concatenate([parts0, parts1], axis=1)
    local_m = _pack_local({n: inputs["m_" + n] for n, _, _, _ in MATRICES})
    local_v = _pack_local({n: inputs["v_" + n] for n, _, _, _ in MATRICES})
    g_pk, d_pk, m_pk, v_pk = _sum_and_adamw(parts, local_w, local_m, local_v)
    g_mat, d_mat, m_mat, v_mat = (_unpack_local(t) for t in (g_pk, d_pk, m_pk, v_pk))

    g_vec_pk = _all_reduce_small(_pack_vectors(gvec))
    d_vec_pk, m_vec_pk, v_vec_pk = _adamw_small(
        _pack_vectors(vec), g_vec_pk, _pack_vectors({n: inputs["m_" + n] for n, _ in VECTORS}),
        _pack_vectors({n: inputs["v_" + n] for n, _ in VECTORS}))
    g_vec, d_vec, m_vec, v_vec = (_unpack_vectors(t) for t in (g_vec_pk, d_vec_pk, m_vec_pk, v_vec_pk))

    def pick(mats, vecs, n):
        return mats[n] if n in mats else vecs[n]

    outs = [loss, grad_x[None]]
    for mats, vecs in ((g_mat, g_vec), (d_mat, d_vec), (m_mat, m_vec), (v_mat, v_vec)):
        outs += [pick(mats, vecs, n) for n in WEIGHT_ORDER]
    return tuple(outs)


_KIND = {n: kind for n, kind, _, _ in MATRICES}
_VIEW_OF = {"l0_w_uq_heads": "l0_w_uq", "l0_w_uk_heads": "l0_w_ukv", "l0_w_out_swa": "l0_w_out",
            "l0_w_out_mla": "l0_w_out", "l0_w_in_qa": "l0_w_in", "l0_w_in_kva": "l0_w_in", "l0_w_in_cq": "l0_w_in",
            "l0_w_in_ckv": "l0_w_in", "l0_w_in_kr": "l0_w_in", "l1_w_q": "l1_w_qkv", "l1_w_k": "l1_w_qkv",
            "l1_w_v": "l1_w_qkv"}
_IN_PARTS = (("l0_w_in_qa", 0, A_Q), ("l0_w_in_kva", A_Q, A_Q + 2 * A_KV),
             ("l0_w_in_cq", A_Q + 2 * A_KV, A_Q + 2 * A_KV + MLA_Q_RANK),
             ("l0_w_in_ckv", A_Q + 2 * A_KV + MLA_Q_RANK, EVEN_IN - MLA_ROPE_DIM))
_KR_PAD = (MLA_NOPE_DIM, LANES - MLA_NOPE_DIM - MLA_ROPE_DIM)
_MLA_QK = MLA_NOPE_DIM + MLA_ROPE_DIM


def _orient(name):
    return "t" if _KIND[_VIEW_OF.get(name, name)] == "c" else "n"


def _nope_rows():
    return (np.arange(MLA_HEADS * LANES) % LANES < MLA_NOPE_DIM)[:, None]


def _layer1_weights(wfull):
    w = dict(wfull)
    w_qkv = w.pop("l1_w_qkv")
    for i, name in enumerate(("l1_w_q", "l1_w_k", "l1_w_v")):
        w[name] = w_qkv[i * D_MODEL:(i + 1) * D_MODEL]
    return w


def _layer0_weights(wfull):
    w = dict(wfull)
    w_in = w.pop("l0_w_in")
    for name, r0, r1 in _IN_PARTS:
        w[name] = w_in[r0:r1]
    w["l0_w_in_kr"] = jnp.pad(w_in[EVEN_IN - MLA_ROPE_DIM:], (_KR_PAD, (0, 0)))
    uq = w.pop("l0_w_uq").reshape(MLA_HEADS, _MLA_QK, MLA_Q_RANK)
    w["l0_w_uq_heads"] = jnp.pad(uq, ((0, 0), (0, LANES - _MLA_QK), (0, 0))).reshape(MLA_HEADS * LANES, MLA_Q_RANK)
    w["l0_w_uk_heads"] = jnp.where(_nope_rows(), wfull["l0_w_ukv"], jnp.zeros_like(wfull["l0_w_ukv"]))
    wo = w.pop("l0_w_out")
    w["l0_w_out_swa"] = wo[:A_Q]
    w["l0_w_out_mla"] = jnp.pad(wo[A_Q:].reshape(MLA_HEADS, HEAD_DIM, D_MODEL),
                                ((0, 0), (LANES - HEAD_DIM, 0), (0, 0))).reshape(MLA_HEADS * LANES, D_MODEL)
    return w


def _layer1_matrix_grads(g):
    out = {n: g[n] for n, _, _, _ in LAYER1 if n in g}
    out["l1_w_qkv"] = jnp.concatenate([g["l1_w_q"], g["l1_w_k"], g["l1_w_v"]], axis=0)
    return out


def _layer0_matrix_grads(g):
    out = {n: g[n] for n, _, _, _ in LAYER0 if n in g}
    out["l0_w_in"] = jnp.concatenate([g[name] for name, _, _ in _IN_PARTS]
                                     + [g["l0_w_in_kr"][_KR_PAD[0]:_KR_PAD[0] + MLA_ROPE_DIM]], axis=0)
    out["l0_w_uq"] = g["l0_w_uq_heads"].reshape(MLA_HEADS, LANES, MLA_Q_RANK)[:, :_MLA_QK].reshape(-1, MLA_Q_RANK)
    uk = jnp.where(_nope_rows(), g["l0_w_uk_heads"], jnp.zeros_like(g["l0_w_uk_heads"]))
    out["l0_w_ukv"] = (g["l0_w_ukv"].astype(F32) + uk.astype(F32)).astype(g["l0_w_ukv"].dtype)
    out["l0_w_out"] = jnp.concatenate(
        [g["l0_w_out_swa"],
         g["l0_w_out_mla"].reshape(MLA_HEADS, LANES, D_MODEL)[:, LANES - HEAD_DIM:].reshape(-1, D_MODEL)], axis=0)
    return out


def _local_grads(w0full, shard1, vec, x, mem, positions, target):
    w = _layer0_weights(w0full)
    gathered1 = jax.ShapeDtypeStruct((N_DEV,) + shard1.shape, shard1.dtype)
    layer1_shapes = jax.eval_shape(lambda g: _layer1_weights(_unpack_gathered(g, LAYER1)), gathered1)
    slots = {n: jnp.zeros(t.shape, GRAD_WIRE_DTYPE) for n, t in w.items()}
    tab64 = _rope_tables(positions, HEAD_DIM, 0, HEAD_DIM)
    tab_mla = _rope_tables(positions, MLA_ROPE_DIM, MLA_NOPE_DIM, LANES)
    parts1_slot = jnp.zeros((N_CHIPS,) + shard1.shape, GRAD_WIRE_DTYPE)
    diff = {"x": x, "slots": slots, "vec": vec, "parts1_slot": parts1_slot}
    consts = {"w": w, "shard1": shard1, "mem": mem, "tab64": tab64, "tab_mla": tab_mla, "target": target,
              "layer1_shapes": layer1_shapes}
    loss_part, grads = jax.value_and_grad(lambda d: _model_loss(d, consts))(diff)
    return loss_part, grads["x"], _layer0_matrix_grads(grads["slots"]), grads["parts1_slot"], grads["vec"]


_INPUT_NAMES = (("x", "mem", "positions") + WEIGHT_ORDER + ("loss_target",)
                + tuple("m_" + n for n in WEIGHT_ORDER) + tuple("v_" + n for n in WEIGHT_ORDER))


def kernel(*args):
    assert len(args) == len(_INPUT_NAMES)
    return _step(dict(zip(_INPUT_NAMES, args)))
```

```python
import numpy as np
import jax
import jax.numpy as jnp
from jax import lax
from jax.experimental import pallas as pl
from jax.experimental.pallas import tpu as pltpu

F32 = jnp.float32
BF16 = jnp.bfloat16

LANES = 128
VMEM_LIMIT_BYTES = 56 * 1024 * 1024
MM_VMEM_BUDGET = 40 * 1024 * 1024
MM_MIN_FLOP_PER_STEP = 1e9
BAND_UNITS_PER_STEP = 4
CAUSAL_ROW_CHAIN = 128
BAND_CHAINS_PER_BATCH = 4

D_MODEL = 1024
HEAD_DIM = 64
ROPE_THETA = 10000.0
NORM_EPS = 1e-6
BLOCK = 128
SWA_HEADS = 8
SWA_KV_HEADS = 2
SWA_WINDOW = 128
MLA_HEADS = 8
MLA_Q_RANK = 384
MLA_KV_RANK = 256
MLA_NOPE_DIM = 64
MLA_ROPE_DIM = 32
A_Q = SWA_HEADS * HEAD_DIM
A_KV = SWA_KV_HEADS * HEAD_DIM
EVEN_IN = A_Q + 2 * A_KV + MLA_Q_RANK + MLA_KV_RANK + MLA_ROPE_DIM
DIL_PATTERNS = ((128, 1), (512, 4), (2048, 16))
X_HEADS = 4
X_HEAD_DIM = 128

ADAM_LR = 0.001
ADAM_B1 = 0.9
ADAM_B2 = 0.999
ADAM_EPS = 1e-08
ADAM_WD = 0.01
ADAM_STEP = 10

N_DEV = 8
GRAD_WIRE_DTYPE = BF16
NEG_MASK = -1e30
NEG_INIT = -1e20

MATRICES = (
    ("l0_w_in", "c", 1024, 1440), ("l0_w_uq", "c", 384, 768), ("l0_w_ukv", "c", 256, 1024),
    ("l0_w_out", "r", 1024, 1024), ("l0_w_xq", "r", 1024, 512), ("l0_w_xkv", "r", 1024, 1024),
    ("l0_w_xo", "c", 512, 1024), ("l0_w_gate", "c", 1024, 2816), ("l0_w_up", "c", 1024, 2816),
    ("l0_w_down", "r", 2816, 1024),
    ("l1_w_qkv", "c", 1024, 3072), ("l1_w_out", "r", 1024, 1024), ("l1_w_xq", "r", 1024, 512),
    ("l1_w_xkv", "r", 1024, 1024), ("l1_w_xo", "c", 512, 1024), ("l1_w_gate", "c", 1024, 2816),
    ("l1_w_up", "c", 1024, 2816), ("l1_w_down", "r", 2816, 1024),
)
VECTORS = (
    ("l0_mix_norm", 1024), ("l0_sinks", 8), ("l0_q_norm", 384), ("l0_kv_norm", 256), ("l0_x_norm", 1024),
    ("l0_mem_norm", 1024), ("l0_ffn_norm", 1024), ("l1_mix_norm", 1024), ("l1_x_norm", 1024),
    ("l1_mem_norm", 1024), ("l1_ffn_norm", 1024), ("final_norm", 1024),
)
WEIGHT_ORDER = (
    "l0_mix_norm", "l0_w_in", "l0_sinks", "l0_q_norm", "l0_w_uq", "l0_kv_norm", "l0_w_ukv", "l0_w_out", "l0_x_norm",
    "l0_mem_norm", "l0_w_xq", "l0_w_xkv", "l0_w_xo", "l0_ffn_norm", "l0_w_gate", "l0_w_up", "l0_w_down",
    "l1_mix_norm", "l1_w_qkv", "l1_w_out", "l1_x_norm", "l1_mem_norm", "l1_w_xq", "l1_w_xkv", "l1_w_xo",
    "l1_ffn_norm", "l1_w_gate", "l1_w_up", "l1_w_down", "final_norm",
)
PACK_COLS = 1024
PART_ROW_ALIGN = 16
ADD_PAIRS_MAX_ROWS = 2048


def _part_rows(k, n):
    return -(-(k * n // N_DEV // PACK_COLS) // PART_ROW_ALIGN) * PART_ROW_ALIGN


LAYER0 = tuple(mat for mat in MATRICES if mat[0].startswith("l0_"))
LAYER1 = tuple(mat for mat in MATRICES if mat[0].startswith("l1_"))
assert MATRICES == LAYER0 + LAYER1
LAYER0_ROWS = sum(_part_rows(k, n) for _, _, k, n in LAYER0)
MAT_ROWS_USED = sum(_part_rows(k, n) for _, _, k, n in MATRICES)
MAT_ROWS = -(-MAT_ROWS_USED // 256) * 256
VEC_ROWS = 16


def _pick(n, cands):
    for c in cands:
        if n % c == 0:
            return c
    return n


def _params(*sem):
    return pltpu.CompilerParams(dimension_semantics=sem, vmem_limit_bytes=VMEM_LIMIT_BYTES)


_DIMS = {"nn": (((1,), (0,)), ((), ())), "nt": (((1,), (1,)), ((), ())), "tn": (((0,), (0,)), ((), ()))}


def _rotate_block(xv, av, bmv, bpv, half, transpose):
    if transpose:
        return xv * av + pltpu.roll(xv * bmv, LANES - half, 1) + pltpu.roll(xv * bpv, half, 1)
    return xv * av + pltpu.roll(xv, half, 1) * bmv + pltpu.roll(xv, LANES - half, 1) * bpv


def _rotate_tile(t, tabs, half, transpose):
    av, bmv, bpv = tabs
    blocks = [_rotate_block(t[:, c:c + LANES], av, bmv, bpv, half, transpose) for c in range(0, t.shape[1], LANES)]
    return blocks[0] if len(blocks) == 1 else jnp.concatenate(blocks, axis=1)


def _div128(n, cap):
    d = (min(n, cap) // LANES) * LANES
    while d >= LANES:
        if n % d == 0:
            return d
        d -= LANES
    return n


def _mm_vmem_bytes(bm, bn, bk, nk, sa, sb, so, has_res):
    est = 2 * (bm * bk * sa + bk * bn * sb + bm * bn * so) + bm * bn * 4
    est += bm * bn * 4 if nk > 1 else 0
    est += 2 * bm * bn * 4 if has_res else 0
    est += bm * bk * 2 if sa == 4 else 0
    est += bk * bn * 2 if sb == 4 else 0
    return est


def _mm_tiles(m, n, k, sa, sb, so, has_res, mode):
    bn = _div128(n, 1536)
    kcap = 2048 if mode == "tn" else k
    for bm_cap in ((1408, 2816) if mode == "tn" else (512, 1024, 2048)):
        bm = _div128(m, bm_cap)
        bk = (min(k, kcap) // LANES) * LANES
        while bk > LANES and (k % bk or _mm_vmem_bytes(bm, bn, bk, k // bk, sa, sb, so, has_res) > MM_VMEM_BUDGET):
            bk -= LANES
        if 2 * bm * bn * bk >= MM_MIN_FLOP_PER_STEP or bm == m:
            break
    return bm, bn, bk


def _mm(a, b, mode, name, out_dtype=F32, res=None, rope=None):
    if mode == "nn":
        (m, k), (k2, n) = a.shape, b.shape
    elif mode == "nt":
        (m, k), (n, k2) = a.shape, b.shape
    else:
        (k, m), (k2, n) = a.shape, b.shape
    assert k == k2, (name, a.shape, b.shape)
    has_res = res is not None
    bm, bn, bk = _mm_tiles(m, n, k, a.dtype.itemsize, b.dtype.itemsize, jnp.dtype(out_dtype).itemsize, has_res, mode)
    nk = k // bk
    dims = _DIMS[mode]
    a_spec = pl.BlockSpec((bk, bm), lambda i, j, kk: (kk, i)) if mode == "tn" else pl.BlockSpec((bm, bk), lambda i, j, kk: (i, kk))
    b_spec = pl.BlockSpec((bn, bk), lambda i, j, kk: (j, kk)) if mode == "nt" else pl.BlockSpec((bk, bn), lambda i, j, kk: (kk, j))
    o_spec = pl.BlockSpec((bm, bn), lambda i, j, kk: (i, j))

    n_in = 2 + (1 if has_res else 0) + (3 if rope is not None else 0)

    def body(*refs):
        a_ref, b_ref = refs[0], refs[1]
        r_ref = refs[2] if has_res else None
        o_ref = refs[n_in]
        part = lax.dot_general(a_ref[...].astype(BF16), b_ref[...].astype(BF16), dims, preferred_element_type=F32)

        def finish(r):
            if has_res:
                r = r + r_ref[...]
            if rope is not None:
                r = _rotate_tile(r, tuple(t[...] for t in refs[n_in - 3:n_in]), rope[1], False)
            o_ref[...] = r.astype(out_dtype)

        if nk == 1:
            finish(part)
            return
        acc = refs[-1]
        kk = pl.program_id(2)

        @pl.when(kk == 0)
        def _():
            acc[...] = part

        @pl.when(jnp.logical_and(kk > 0, kk < nk - 1))
        def _():
            acc[...] += part

        @pl.when(kk == nk - 1)
        def _():
            finish(acc[...] + part)

    args = (a, b, res) if has_res else (a, b)
    in_specs = [a_spec, b_spec] + ([o_spec] if has_res else [])
    if rope is not None:
        args = args + tuple(rope[0])
        in_specs = in_specs + [pl.BlockSpec((bm, LANES), lambda i, j, kk: (i, 0))] * 3
    return pl.pallas_call(
        body, name=name, grid=(m // bm, n // bn, nk), in_specs=in_specs, out_specs=o_spec,
        out_shape=jax.ShapeDtypeStruct((m, n), out_dtype),
        scratch_shapes=[pltpu.VMEM((bm, bn), F32)] if nk > 1 else [],
        compiler_params=_params("parallel", "parallel", "arbitrary"),
    )(*args)


def _rms_fwd(x, g, name, out_dtype=BF16):
    s, d = x.shape
    bs = _pick(s, (512, 256, 128))

    def body(x_ref, g_ref, o_ref):
        xv = x_ref[...]
        r = lax.rsqrt(jnp.mean(xv * xv, axis=-1, keepdims=True) + NORM_EPS)
        o_ref[...] = ((xv * r) * g_ref[...]).astype(out_dtype)

    return pl.pallas_call(
        body, name=name, grid=(s // bs,),
        in_specs=[pl.BlockSpec((bs, d), lambda i: (i, 0)), pl.BlockSpec((1, d), lambda i: (0, 0))],
        out_specs=pl.BlockSpec((bs, d), lambda i: (i, 0)), out_shape=jax.ShapeDtypeStruct((s, d), out_dtype),
        compiler_params=_params("parallel"),
    )(x, g.reshape(1, d))


def _rms_bwd(x, g, dy, name, dres=None):
    s, d = x.shape
    bs = _pick(s, (512, 256, 128))
    has_res = dres is not None

    def body(*refs):
        if has_res:
            x_ref, g_ref, dy_ref, r_ref, dx_ref, dg_ref = refs
        else:
            x_ref, g_ref, dy_ref, dx_ref, dg_ref = refs
        i = pl.program_id(0)
        xv = x_ref[...]
        dy = dy_ref[...]
        r = lax.rsqrt(jnp.mean(xv * xv, axis=-1, keepdims=True) + NORM_EPS)
        xh = xv * r
        dxh = dy * g_ref[...]
        dx = r * (dxh - xh * jnp.mean(dxh * xh, axis=-1, keepdims=True))
        if has_res:
            dx = dx + r_ref[...]
        dx_ref[...] = dx

        @pl.when(i == 0)
        def _():
            dg_ref[...] = jnp.zeros_like(dg_ref)

        dg_ref[...] += jnp.sum(dy * xh, axis=0, keepdims=True)

    row = pl.BlockSpec((bs, d), lambda i: (i, 0))
    vec = pl.BlockSpec((1, d), lambda i: (0, 0))
    args = (x, g.reshape(1, d), dy) + ((dres,) if has_res else ())
    dx, dg = pl.pallas_call(
        body, name=name, grid=(s // bs,), in_specs=[row, vec, row] + ([row] if has_res else []),
        out_specs=(row, vec), out_shape=(jax.ShapeDtypeStruct((s, d), F32), jax.ShapeDtypeStruct((1, d), F32)),
        compiler_params=_params("arbitrary"),
    )(*args)
    return dx, dg.reshape(d)


def _rope_tables(positions, dh, offset, period):
    role = np.zeros(LANES, np.int32)
    for base in range(0, LANES, period):
        role[base + offset:base + offset + dh // 2] = 1
        role[base + offset + dh // 2:base + offset + dh] = 2
    inv_freq = ROPE_THETA ** (-jnp.arange(0, dh, 2, dtype=F32) / dh)
    one_period = jnp.concatenate([jnp.zeros((offset,), F32), inv_freq, inv_freq,
                                  jnp.zeros((period - offset - dh,), F32)])
    ang = positions.astype(F32)[:, None] * jnp.tile(one_period, LANES // period)[None, :]
    c, s = jnp.cos(ang), jnp.sin(ang)
    role = role[None, :]
    a = jnp.where(role == 0, 1.0, c).astype(F32)
    bm = jnp.where(role == 2, s, 0.0).astype(F32)
    bp = jnp.where(role == 1, -s, 0.0).astype(F32)
    return a, bm, bp


def _rope_apply(x, tabs, half, transpose, name, shared=None, sum_blocks=False):
    s, w = x.shape
    bs = _pick(s, (512, 256, 128))
    nc = w // LANES
    a, bm, bp = tabs
    has_shared = shared is not None

    def body(*refs):
        x_ref, a_ref, bm_ref, bp_ref = refs[:4]
        o_ref = refs[5] if has_shared else refs[4]
        av, bmv, bpv = a_ref[...], bm_ref[...], bp_ref[...]
        total = None
        for c in range(nc):
            sl = slice(c * LANES, (c + 1) * LANES)
            xv = x_ref[:, sl]
            if has_shared:
                xv = xv + refs[4][...]
            out = _rotate_block(xv, av, bmv, bpv, half, transpose)
            o_ref[:, sl] = out
            total = out if total is None else total + out
        if sum_blocks:
            refs[-1][...] = total

    row = pl.BlockSpec((bs, w), lambda i: (i, 0))
    tab = pl.BlockSpec((bs, LANES), lambda i: (i, 0))
    out_shape = jax.ShapeDtypeStruct((s, w), F32)
    return pl.pallas_call(
        body, name=name, grid=(s // bs,), in_specs=[row, tab, tab, tab] + ([tab] if has_shared else []),
        out_specs=(row, tab) if sum_blocks else row,
        out_shape=(out_shape, jax.ShapeDtypeStruct((s, LANES), F32)) if sum_blocks else out_shape,
        compiler_params=_params("parallel"),
    )(x, a, bm, bp, *((shared,) if has_shared else ()))


def _make_rope(half, name):
    @jax.custom_vjp
    def rope(x, a, bm, bp):
        return _rope_apply(x, (a, bm, bp), half, False, name + "_fwd")

    def fwd(x, a, bm, bp):
        return rope(x, a, bm, bp), (a, bm, bp)

    def bwd(tabs, dy):
        return _rope_apply(dy, tabs, half, True, name + "_bwd"), None, None, None

    rope.defvjp(fwd, bwd)
    return rope


def _make_rope_shared(half, name):
    @jax.custom_vjp
    def rope(x, shared, a, bm, bp):
        return _rope_apply(x, (a, bm, bp), half, False, name + "_fwd", shared=shared)

    def fwd(x, shared, a, bm, bp):
        return rope(x, shared, a, bm, bp), (a, bm, bp)

    def bwd(tabs, dy):
        dx, dshared = _rope_apply(dy, tabs, half, True, name + "_bwd", sum_blocks=True)
        return dx, dshared, None, None, None

    rope.defvjp(fwd, bwd)
    return rope


def _lane_masks():
    lane = lax.broadcasted_iota(jnp.int32, (1, LANES), 1)
    lo = lane < HEAD_DIM
    return [lo, jnp.logical_not(lo)]


def _sel(mask, v):
    return jnp.where(mask, v, jnp.zeros_like(v))


_NT = (((1,), (1,)), ((), ()))
_NN = (((1,), (0,)), ((), ()))
_TN = (((0,), (0,)), ((), ()))
_BNT = (((2,), (2,)), ((0,), (0,)))
_BNN = (((2,), (1,)), ((0,), (0,)))


def _dot(a, b, dims):
    return lax.dot_general(a, b, dims, preferred_element_type=F32)


def _band_masks(max_dist):
    assert BLOCK - 1 <= max_dist <= BLOCK
    r = lax.broadcasted_iota(jnp.int32, (BLOCK, BLOCK), 0)
    c = lax.broadcasted_iota(jnp.int32, (BLOCK, BLOCK), 1)
    return (BLOCK + r - c) <= max_dist, r >= c


def _stack_heads(t):
    return jnp.concatenate([t, t], axis=0)


def _head_terms(lms, a, prod, lv):
    t = jnp.sum(_sel(lms[a], prod), axis=-1, keepdims=True)
    lse = jnp.max(jnp.where(lms[a], lv, -jnp.inf), axis=-1, keepdims=True)
    return t, lse


class _Residue:
    def __init__(self, ref, r, dil):
        self.ref, self.rows = ref, pl.ds(r, BLOCK, stride=dil)

    def __getitem__(self, idx):
        return self.ref[self.rows, idx[1]]

    def __setitem__(self, idx, val):
        self.ref[self.rows, idx[1]] = val


def _residues(refs, dil):
    if dil == 1:
        return [tuple(refs)]
    return [tuple(_Residue(x, r, dil) for x in refs) for r in range(dil)]


def _band_fwd(q, k, v, sinkrow, scale, max_dist, upb, dil, name):
    sq, w = q.shape
    rb = BLOCK * dil
    nq, nub, wb = sq // rb, w // (LANES * upb), LANES * upb
    has_sink = sinkrow is not None

    def body(*refs):
        s_ref = refs[5] if has_sink else None
        lms = _lane_masks()
        mprev, mcur = _band_masks(max_dist)
        mprev = jnp.logical_and(mprev, pl.program_id(1) > 0)
        mask2 = _stack_heads(jnp.concatenate([mprev, mcur], axis=1))
        chains = [(rr, slice(u * LANES, (u + 1) * LANES))
                  for rr in _residues(refs[:5] + refs[-2:], dil) for u in range(upb)]
        for g0 in range(0, len(chains), BAND_CHAINS_PER_BATCH):
            group = chains[g0:g0 + BAND_CHAINS_PER_BATCH]
            qs, kcat, vcat, sks = [], [], [], []
            for (q_ref, kp_ref, kc_ref, vp_ref, vc_ref, _, _), sl in group:
                qv = (q_ref[:, sl] * scale).astype(BF16)
                qs.append(jnp.concatenate([_sel(lms[0], qv), _sel(lms[1], qv)], axis=0))
                kcat.append(jnp.concatenate([kp_ref[:, sl].astype(BF16), kc_ref[:, sl].astype(BF16)], axis=0))
                vcat.append(jnp.concatenate([vp_ref[:, sl].astype(BF16), vc_ref[:, sl].astype(BF16)], axis=0))
                if has_sink:
                    sks.append(s_ref[sl.start // LANES])
            qs, kcat, vcat = jnp.stack(qs), jnp.stack(kcat), jnp.stack(vcat)
            sc = jnp.where(mask2[None], _dot(qs, kcat, _BNT), NEG_MASK)
            m = jnp.max(sc, axis=-1, keepdims=True)
            p = jnp.exp(sc - m)
            l = jnp.sum(p, axis=-1, keepdims=True)
            pv = _dot(p.astype(BF16), vcat, _BNN)
            if has_sink:
                sk2 = jnp.stack(sks)
                m_all = jnp.maximum(m, sk2)
                shrink = jnp.exp(m - m_all)
                l = l * shrink + jnp.exp(sk2 - m_all)
                pv, m = pv * shrink, m_all
            o2 = pv / l
            lse2 = m + jnp.log(l)
            for gi, ((_, _, _, _, _, o_ref, l_ref), sl) in enumerate(group):
                o_ref[:, sl] = jnp.where(lms[0], o2[gi, :BLOCK], o2[gi, BLOCK:])
                l_ref[:, sl] = jnp.where(lms[0], lse2[gi, :BLOCK], lse2[gi, BLOCK:])

    cur = pl.BlockSpec((rb, wb), lambda ub, i: (i, ub))
    prev = pl.BlockSpec((rb, wb), lambda ub, i: (jnp.maximum(i - 1, 0), ub))
    in_specs = [cur, prev, cur, prev, cur]
    in_specs += [pl.BlockSpec((upb, 2 * BLOCK, 1), lambda ub, i: (ub, 0, 0))] if has_sink else []
    args = (q, k, k, v, v) + ((sinkrow,) if has_sink else ())
    return pl.pallas_call(
        body, name=name, grid=(nub, nq), in_specs=in_specs, out_specs=(cur, cur),
        out_shape=(jax.ShapeDtypeStruct((sq, w), F32), jax.ShapeDtypeStruct((sq, w), F32)),
        compiler_params=_params("parallel", "parallel"),
    )(*args)


def _band_dq(q, k, v, o, lse, do, sinkrow, scale, max_dist, upb, dil, name, acc=None):
    sq, w = q.shape
    rb = BLOCK * dil
    nq, nub, wb = sq // rb, w // (LANES * upb), LANES * upb
    has_sink = sinkrow is not None

    def body(*refs):
        if has_sink:
            s_ref, dq_block, dsink_ref = refs[8], refs[9], refs[10]
        else:
            dq_block = refs[-1]
        acc_block = refs[8] if acc is not None else dq_block
        i = pl.program_id(1)
        lms = _lane_masks()
        mprev, mcur = _band_masks(max_dist)
        mprev = jnp.logical_and(mprev, i > 0)
        mask2 = _stack_heads(jnp.concatenate([mprev, mcur], axis=1))
        if has_sink:
            @pl.when(i == 0)
            def _():
                dsink_ref[...] = jnp.zeros_like(dsink_ref)

        chains = [(rr, slice(u * LANES, (u + 1) * LANES))
                  for rr in _residues(refs[:8] + (acc_block, dq_block), dil) for u in range(upb)]
        for g0 in range(0, len(chains), BAND_CHAINS_PER_BATCH):
            group = chains[g0:g0 + BAND_CHAINS_PER_BATCH]
            qs, dos, kcat, vcat, t2, lse2 = [], [], [], [], [], []
            for (q_ref, kp_ref, kc_ref, vp_ref, vc_ref, o_ref, l_ref, do_ref, _, _), sl in group:
                qv = (q_ref[:, sl] * scale).astype(BF16)
                dov = do_ref[:, sl]
                prod = dov * o_ref[:, sl]
                dob = dov.astype(BF16)
                lv = l_ref[:, sl]
                (t0, lse0), (t1, lse1) = _head_terms(lms, 0, prod, lv), _head_terms(lms, 1, prod, lv)
                t2.append(jnp.concatenate([t0, t1], axis=0))
                lse2.append(jnp.concatenate([lse0, lse1], axis=0))
                qs.append(jnp.concatenate([_sel(lms[0], qv), _sel(lms[1], qv)], axis=0))
                dos.append(jnp.concatenate([_sel(lms[0], dob), _sel(lms[1], dob)], axis=0))
                kcat.append(jnp.concatenate([kp_ref[:, sl].astype(BF16), kc_ref[:, sl].astype(BF16)], axis=0))
                vcat.append(jnp.concatenate([vp_ref[:, sl].astype(BF16), vc_ref[:, sl].astype(BF16)], axis=0))
                if has_sink:
                    rs = -jnp.exp(s_ref[:, sl] - lv) * jnp.where(lms[0], t0, t1)
                    dsink_ref[0:1, sl] += jnp.sum(rs, axis=0, keepdims=True)
            qs, dos, kcat, vcat = jnp.stack(qs), jnp.stack(dos), jnp.stack(kcat), jnp.stack(vcat)
            p = jnp.exp(jnp.where(mask2[None], _dot(qs, kcat, _BNT), NEG_MASK) - jnp.stack(lse2))
            ds = (p * (_dot(dos, vcat, _BNT) - jnp.stack(t2))).astype(BF16)
            dq2 = _dot(ds, kcat, _BNN) * scale
            for gi, (rr, sl) in enumerate(group):
                dq = jnp.where(lms[0], dq2[gi, :BLOCK], dq2[gi, BLOCK:])
                rr[-1][:, sl] = dq if acc is None else dq + rr[-2][:, sl]

    cur = pl.BlockSpec((rb, wb), lambda ub, i: (i, ub))
    prev = pl.BlockSpec((rb, wb), lambda ub, i: (jnp.maximum(i - 1, 0), ub))
    in_specs = [cur, prev, cur, prev, cur, cur, cur, cur]
    args = (q, k, k, v, v, o, lse, do)
    out_specs, out_shape = cur, jax.ShapeDtypeStruct((sq, w), F32)
    sem = ("parallel", "parallel")
    if has_sink:
        in_specs = in_specs + [pl.BlockSpec((1, wb), lambda ub, i: (0, ub))]
        args = args + (sinkrow,)
        out_specs = (cur, pl.BlockSpec((8, wb), lambda ub, i: (0, ub)))
        out_shape = (out_shape, jax.ShapeDtypeStruct((8, w), F32))
        sem = ("parallel", "arbitrary")
    aliases = {}
    if acc is not None:
        assert not has_sink
        in_specs, args, aliases = in_specs + [cur], args + (acc,), {len(args): 0}
    return pl.pallas_call(
        body, name=name, grid=(nub, nq), in_specs=in_specs, out_specs=out_specs, out_shape=out_shape,
        input_output_aliases=aliases, compiler_params=_params(*sem),
    )(*args)


def _band_dkv(q, k, v, o, lse, do, scale, max_dist, upb, dil, name, accs=None):
    sq, w = q.shape
    rb = BLOCK * dil
    nq, nub, wb = sq // rb, w // (LANES * upb), LANES * upb

    def body(*refs):
        kb = pl.program_id(1)
        lms = _lane_masks()
        key = lax.broadcasted_iota(jnp.int32, (BLOCK, BLOCK), 0)
        qry = lax.broadcasted_iota(jnp.int32, (BLOCK, BLOCK), 1)
        msame = qry >= key
        mnext = jnp.logical_and((BLOCK + qry - key) <= max_dist, kb < nq - 1)
        mask4 = jnp.concatenate([msame, msame, mnext, mnext], axis=1)
        chains =[(rr, slice(u * LANES, (u + 1) * LANES)) for rr in _residues(refs, dil) for u in range(upb)]
        for g0 in range(0, len(chains), BAND_CHAINS_PER_BATCH):
            group = chains[g0:g0 + BAND_CHAINS_PER_BATCH]
            kvs, vvs, qss, doss, t4s, lse4s = [], [], [], [], [], []
            for rr, sl in group:
                k_ref, v_ref, qs_ref, qn_ref, os_ref, on_ref, ls_ref, ln_ref, dos_ref, don_ref = rr[:10]
                kvs.append(k_ref[:, sl].astype(BF16))
                vvs.append(v_ref[:, sl].astype(BF16))
                qparts, doparts, tparts, lparts = [], [], [], []
                for q_ref, o_ref, l_ref, do_ref in ((qs_ref, os_ref, ls_ref, dos_ref),
                                                    (qn_ref, on_ref, ln_ref, don_ref)):
                    qv = (q_ref[:, sl] * scale).astype(BF16)
                    dov = do_ref[:, sl]
                    prod_t = (dov * o_ref[:, sl]).T
                    dob = dov.astype(BF16)
                    lse_t = l_ref[:, sl].T
                    for a in range(2):
                        lanes = slice(a * HEAD_DIM, (a + 1) * HEAD_DIM)
                        qparts.append(_sel(lms[a], qv))
                        doparts.append(_sel(lms[a], dob))
                        tparts.append(jnp.sum(prod_t[lanes, :], axis=0, keepdims=True))
                        lparts.append(lse_t[a * HEAD_DIM:a * HEAD_DIM + 1, :])
                qss.append(jnp.concatenate(qparts, axis=0))
                doss.append(jnp.concatenate(doparts, axis=0))
                t4s.append(jnp.concatenate(tparts, axis=1))
                lse4s.append(jnp.concatenate(lparts, axis=1))
            kv, vv, qs, dos = jnp.stack(kvs), jnp.stack(vvs), jnp.stack(qss), jnp.stack(doss)
            p = jnp.exp(jnp.where(mask4[None], _dot(kv, qs, _BNT), NEG_MASK) - jnp.stack(lse4s))
            ds = (p * (_dot(vv, dos, _BNT) - jnp.stack(t4s))).astype(BF16)
            dv = _dot(p.astype(BF16), dos, _BNN)
            dk = _dot(ds, qs, _BNN)
            for gi, (rr, sl) in enumerate(group):
                rr[-1][:, sl] = dv[gi] if accs is None else dv[gi] + rr[11][:, sl]
                rr[-2][:, sl] = dk[gi] if accs is None else dk[gi] + rr[10][:, sl]

    same = pl.BlockSpec((rb, wb), lambda ub, kb: (kb, ub))
    nxt = pl.BlockSpec((rb, wb), lambda ub, kb: (jnp.minimum(kb + 1, nq - 1), ub))
    return pl.pallas_call(
        body, name=name, grid=(nub, nq),
        in_specs=[same, same, same, nxt, same, nxt, same, nxt, same, nxt] + ([same, same] if accs else []),
        out_specs=(same, same),
        out_shape=(jax.ShapeDtypeStruct((sq, w), F32), jax.ShapeDtypeStruct((sq, w), F32)),
        input_output_aliases={10: 0, 11: 1} if accs else {},
        compiler_params=_params("parallel", "parallel"),
    )(k, v, q, q, o, o, lse, lse, do, do, *(accs or ()))


def _make_band_attention(scale, max_dist, upb, name):
    @jax.custom_vjp
    def attn(q, k, v, sinks):
        return _band_fwd(q, k, v, _sink_col(sinks), scale, max_dist, upb, 1, name + "_fwd")[0]

    def fwd(q, k, v, sinks):
        o, lse = _band_fwd(q, k, v, _sink_col(sinks), scale, max_dist, upb, 1, name + "_fwd")
        return o, (q, k, v, o, lse, sinks)

    def bwd(res, do):
        q, k, v, o, lse, sinks = res
        dq, dsink = _band_dq(q, k, v, o, lse, do, _sink_row(sinks), scale, max_dist, upb, 1, name + "_dq")
        dk, dv = _band_dkv(q, k, v, o, lse, do, scale, max_dist, upb, 1, name + "_dkv")
        return dq, dk, dv, dsink[0].reshape(-1, HEAD_DIM)[:, 0]

    attn.defvjp(fwd, bwd)
    return attn


def _triangle(n, by_key):
    if by_key:
        pairs = [(i, kb) for kb in range(n) for i in range(kb, n)]
    else:
        pairs = [(i, j) for i in range(n) for j in range(i + 1)]
    qi = np.asarray([p[0] for p in pairs], np.int32)
    kj = np.asarray([p[1] for p in pairs], np.int32)
    return jnp.asarray(qi), jnp.asarray(kj)


def _gather_copies(shard_ref, out_ref, send_sems, recv_sems, arrivals):
    x, y, c = _my_place()
    me = 4 * x + 2 * y + c
    sends, recvs = [], []
    for k in range(1, N_DEV):
        px, py, pc = _flip(x, k & 4), _flip(y, k & 2), _flip(c, k & 1)
        peer = 4 * px + 2 * py + pc
        for slot, into in ((me, sends),) + (((peer, recvs),) if arrivals else ()):
            into.append(pltpu.make_async_remote_copy(
                src_ref=shard_ref, dst_ref=out_ref.at[slot], send_sem=send_sems.at[k - 1],
                recv_sem=recv_sems.at[k - 1], device_id=(px, py, pc), device_id_type=MESH_IDS))
    return me, sends, recvs


def _causal_fwd(q, k, v, scale, blk, name, shard=None):
    s, w = q.shape
    nq, nub = s // blk, w // LANES
    qi, kj = _triangle(nq, by_key=False)
    nsteps = qi.shape[0]
    gathers = shard is not None

    def body(qi_ref, kj_ref, q_ref, k_ref, v_ref, *rest):
        if gathers:
            shard_ref, o_ref, l_ref, gath_ref, m_sc, l_sc, acc_sc, send_sems, recv_sems, local_sem = rest
        else:
            o_ref, l_ref, m_sc, l_sc, acc_sc = rest
        t = pl.program_id(1)
        i, j = qi_ref[t], kj_ref[t]

        if gathers:
            ub = pl.program_id(0)

            @pl.when(jnp.logical_and(ub == 0, t == 0))
            def _():
                me, sends, _ = _gather_copies(shard_ref, gath_ref, send_sems, recv_sems, arrivals=False)
                pltpu.make_async_copy(shard_ref, gath_ref.at[me], local_sem).start()
                for cp in sends:
                    cp.start()

        @pl.when(j == 0)
        def _():
            m_sc[...] = jnp.full_like(m_sc, NEG_INIT)
            l_sc[...] = jnp.zeros_like(l_sc)
            acc_sc[...] = jnp.zeros_like(acc_sc)

        def step(diagonal):
            kv, vv = k_ref[...].astype(BF16), v_ref[...].astype(BF16)
            chains = range(0, blk, CAUSAL_ROW_CHAIN)
            width = {c0: (c0 + CAUSAL_ROW_CHAIN if diagonal else blk) for c0 in chains}
            scs = [_dot((q_ref[c0:c0 + CAUSAL_ROW_CHAIN, :] * scale).astype(BF16), kv[:width[c0]], _NT)
                   for c0 in chains]
            m_all, l_all, acc_all = m_sc[...], l_sc[...], acc_sc[...]
            m_out, l_out, acc_out = [], [], []
            for sc, c0 in zip(scs, chains):
                rows = slice(c0, c0 + CAUSAL_ROW_CHAIN)
                if diagonal:
                    r = c0 + lax.broadcasted_iota(jnp.int32, (CAUSAL_ROW_CHAIN, width[c0]), 0)
                    c = lax.broadcasted_iota(jnp.int32, (CAUSAL_ROW_CHAIN, width[c0]), 1)
                    sc = jnp.where(r >= c, sc, NEG_MASK)
                m_prev = m_all[rows]
                m_new = jnp.maximum(m_prev, jnp.max(sc, axis=-1, keepdims=True))
                alpha = jnp.exp(m_prev - m_new)
                p = jnp.exp(sc - m_new)
                l_out.append(alpha * l_all[rows] + jnp.sum(p, axis=-1, keepdims=True))
                m_out.append(m_new)
                acc_out.append(acc_all[rows] * alpha + _dot(p.astype(BF16), vv[:width[c0]], _NN))
            m_sc[...] = jnp.concatenate(m_out, axis=0)
            l_sc[...] = jnp.concatenate(l_out, axis=0)
            acc_sc[...] = jnp.concatenate(acc_out, axis=0)

        @pl.when(j < i)
        def _():
            step(False)

        @pl.when(j == i)
        def _():
            step(True)
            lf = l_sc[...]
            o_ref[...] = acc_sc[...] / lf
            l_ref[...] = jnp.broadcast_to(m_sc[...] + jnp.log(lf), (blk, LANES))

        if gathers:
            @pl.when(jnp.logical_and(pl.program_id(0) == nub - 1, t == nsteps - 1))
            def _():
                me, sends, recvs = _gather_copies(shard_ref, gath_ref, send_sems, recv_sems, arrivals=True)
                for cp in recvs:
                    cp.wait_recv()
                for cp in sends:
                    cp.wait_send()
                pltpu.make_async_copy(shard_ref, gath_ref.at[me], local_sem).wait()

    qspec = pl.BlockSpec((blk, LANES), lambda ub, t, qi_ref, kj_ref: (qi_ref[t], ub))
    kspec = pl.BlockSpec((blk, LANES), lambda ub, t, qi_ref, kj_ref: (kj_ref[t], ub))
    in_specs, out_specs = [qspec, kspec, kspec], (qspec, qspec)
    out_shape = (jax.ShapeDtypeStruct((s, w), F32), jax.ShapeDtypeStruct((s, w), F32))
    scratch = [pltpu.VMEM((blk, 1), F32), pltpu.VMEM((blk, 1), F32), pltpu.VMEM((blk, LANES), F32)]
    args = (qi, kj, q, k, v)
    if gathers:
        in_specs, out_specs = in_specs + [HBM_SPEC], out_specs + (HBM_SPEC,)
        out_shape = out_shape + (jax.ShapeDtypeStruct((N_DEV,) + shard.shape, shard.dtype),)
        scratch = scratch + [pltpu.SemaphoreType.DMA((N_DEV - 1,)), pltpu.SemaphoreType.DMA((N_DEV - 1,)),
                             pltpu.SemaphoreType.DMA]
        args = args + (shard,)
    return pl.pallas_call(
        body, name=name,
        grid_spec=pltpu.PrefetchScalarGridSpec(
            num_scalar_prefetch=2, grid=(nub, nsteps), in_specs=in_specs, out_specs=out_specs, scratch_shapes=scratch),
        out_shape=out_shape, compiler_params=_params("arbitrary", "arbitrary"),
    )(*args)


def _causal_bwd(q, k, v, o, lse, do, scale, blk, name, chip_sums):
    s, w = q.shape
    nq, nub = s // blk, w // LANES
    qi, kj = _triangle(nq, by_key=True)
    nsteps = qi.shape[0]

    def body(qi_ref, kj_ref, q_ref, k_ref, v_ref, o_ref, l_ref, do_ref, t_ref, dq_ref, dk_ref, dv_ref, parts_ref,
             dk_acc, dv_acc, send_sems, recv_sems, local_sem):
        t = pl.program_id(1)
        i, kb = qi_ref[t], kj_ref[t]

        @pl.when(jnp.logical_and(pl.program_id(0) == 0, t == 0))
        def _():
            local, sends, _ = _chip_exchange_copies(t_ref, parts_ref, send_sems, recv_sems, local_sem, arrivals=False)
            local.start()
            for cp in sends:
                cp.start()

        @pl.when(t == 0)
        def _():
            dq_ref[...] = jnp.zeros_like(dq_ref)

        @pl.when(i == kb)
        def _():
            dk_acc[...] = jnp.zeros_like(dk_acc)
            dv_acc[...] = jnp.zeros_like(dv_acc)

        def step(diagonal):
            qv = (q_ref[...] * scale).astype(BF16)
            kv, vv = k_ref[...].astype(BF16), v_ref[...].astype(BF16)
            dov = do_ref[...]
            tsum = jnp.sum(dov * o_ref[...], axis=-1, keepdims=True)
            dob = dov.astype(BF16)
            sc = _dot(qv, kv, _NT)
            if diagonal:
                r = lax.broadcasted_iota(jnp.int32, (blk, blk), 0)
                c = lax.broadcasted_iota(jnp.int32, (blk, blk), 1)
                sc = jnp.where(r >= c, sc, NEG_MASK)
            p = jnp.exp(sc - l_ref[:, 0:1])
            ds = (p * (_dot(dob, vv, _NT) - tsum)).astype(BF16)
            dv_acc[...] += _dot(p.astype(BF16), dob, _TN)
            dk_acc[...] += _dot(ds, qv, _TN)
            rows = pl.ds(pl.multiple_of(i * blk, blk), blk)
            dq_ref[rows, :] += _dot(ds, kv, _NN) * scale

        @pl.when(i == kb)
        def _():
            step(True)

        @pl.when(i > kb)
        def _():
            step(False)

        @pl.when(i == nq - 1)
        def _():
            dk_ref[...] = dk_acc[...]
            dv_ref[...] = dv_acc[...]

        @pl.when(jnp.logical_and(pl.program_id(0) == nub - 1, t == nsteps - 1))
        def _():
            local, sends, recvs = _chip_exchange_copies(t_ref, parts_ref, send_sems, recv_sems, local_sem, arrivals=True)
            for cp in recvs:
                cp.wait_recv()
            for cp in sends:
                cp.wait_send()
            local.wait()

    qspec = pl.BlockSpec((blk, LANES), lambda ub, t, qi_ref, kj_ref: (qi_ref[t], ub))
    kspec = pl.BlockSpec((blk, LANES), lambda ub, t, qi_ref, kj_ref: (kj_ref[t], ub))
    whole = pl.BlockSpec((s, LANES), lambda ub, t, qi_ref, kj_ref: (0, ub))
    out = jax.ShapeDtypeStruct((s, w), F32)
    return pl.pallas_call(
        body, name=name,
        grid_spec=pltpu.PrefetchScalarGridSpec(
            num_scalar_prefetch=2, grid=(nub, nsteps), in_specs=[qspec, kspec, kspec, qspec, qspec, qspec, HBM_SPEC],
            out_specs=(whole, kspec, kspec, HBM_SPEC),
            scratch_shapes=[pltpu.VMEM((blk, LANES), F32), pltpu.VMEM((blk, LANES), F32),
                            pltpu.SemaphoreType.DMA((N_CHIPS - 1,)), pltpu.SemaphoreType.DMA((N_CHIPS - 1,)),
                            pltpu.SemaphoreType.DMA]),
        out_shape=(out, out, out, jax.ShapeDtypeStruct(chip_sums.shape, chip_sums.dtype)),
        compiler_params=_params("arbitrary", "arbitrary"),
    )(qi, kj, q, k, v, o, lse, do, chip_sums)


def _make_causal_attention(scale, blk, name, late_shapes, reduce_late):
    def forward(q, k, v, shard):
        o, lse, gathered = _causal_fwd(q, k, v, scale, blk, name + "_fwd", shard=shard)
        late = {n: jnp.zeros(t.shape, GRAD_WIRE_DTYPE) for n, t in late_shapes.items()}
        return (o, gathered, late), (q, k, v, o, lse)

    @jax.custom_vjp
    def attn(q, k, v, shard, parts_slot):
        return forward(q, k, v, shard)[0]

    def fwd(q, k, v, shard, parts_slot):
        return forward(q, k, v, shard)

    def bwd(res, cts):
        q, k, v, o, lse = res
        do, _, late_grads = cts
        dq, dk, dv, parts = _causal_bwd(q, k, v, o, lse, do, scale, blk, name + "_bwd", reduce_late(late_grads))
        return dq, dk, dv, None, parts

    attn.defvjp(fwd, bwd)
    return attn


_BTN = (((1,), (1,)), ((0,), (0,)))


def _heads(ref, scale=None):
    blocks = []
    for c in range(0, ref.shape[1], LANES):
        t = ref[:, c:c + LANES]
        blocks.append((t if scale is None else t * scale).astype(BF16))
    return jnp.stack(blocks)


def _memory_fwd(q, k, v, scale, name):
    s, w = q.shape
    m = k.shape[0]
    bq = _pick(s, (512, 256, 128))

    def body(q_ref, k_ref, v_ref, o_ref, l_ref):
        sc = _dot(_heads(q_ref, scale), _heads(k_ref), _BNT)
        mx = jnp.max(sc, axis=-1, keepdims=True)
        p = jnp.exp(sc - mx)
        l = jnp.sum(p, axis=-1, keepdims=True)
        o = _dot(p.astype(BF16), _heads(v_ref), _BNN) / l
        lse = mx + jnp.log(l)
        for h in range(w // LANES):
            o_ref[:, h * LANES:(h + 1) * LANES] = o[h]
            l_ref[:, h * LANES:(h + 1) * LANES] = jnp.broadcast_to(lse[h], (bq, LANES))

    row = pl.BlockSpec((bq, w), lambda i: (i, 0))
    mem = pl.BlockSpec((m, w), lambda i: (0, 0))
    return pl.pallas_call(
        body, name=name, grid=(s // bq,), in_specs=[row, mem, mem], out_specs=(row, row),
        out_shape=(jax.ShapeDtypeStruct((s, w), F32), jax.ShapeDtypeStruct((s, w), F32)),
        compiler_params=_params("parallel"),
    )(q, k, v)


def _memory_bwd(q, k, v, o, lse, do, scale, name):
    s, w = q.shape
    m = k.shape[0]
    nh = w // LANES
    bq = _pick(s, (512, 256, 128))

    def body(q_ref, k_ref, v_ref, o_ref, l_ref, do_ref, dq_ref, dk_ref, dv_ref):
        qs, ks, vs = _heads(q_ref, scale), _heads(k_ref), _heads(v_ref)
        dos = _heads(do_ref)
        t = jnp.stack([jnp.sum(do_ref[:, h * LANES:(h + 1) * LANES] * o_ref[:, h * LANES:(h + 1) * LANES],
                               axis=-1, keepdims=True) for h in range(nh)])
        lse = jnp.stack([l_ref[:, h * LANES:h * LANES + 1] for h in range(nh)])
        p = jnp.exp(_dot(qs, ks, _BNT) - lse)
        ds = (p * (_dot(dos, vs, _BNT) - t)).astype(BF16)
        dq = _dot(ds, ks, _BNN) * scale
        dk = _dot(ds, qs, _BTN)
        dv = _dot(p.astype(BF16), dos, _BTN)

        @pl.when(pl.program_id(0) == 0)
        def _():
            dk_ref[...] = jnp.zeros_like(dk_ref)
            dv_ref[...] = jnp.zeros_like(dv_ref)

        for h in range(nh):
            sl = slice(h * LANES, (h + 1) * LANES)
            dq_ref[:, sl] = dq[h]
            dk_ref[:, sl] += dk[h]
            dv_ref[:, sl] += dv[h]

    row = pl.BlockSpec((bq, w), lambda i: (i, 0))
    mem = pl.BlockSpec((m, w), lambda i: (0, 0))
    return pl.pallas_call(
        body, name=name, grid=(s // bq,), in_specs=[row, mem, mem, row, row, row], out_specs=(row, mem, mem),
        out_shape=(jax.ShapeDtypeStruct((s, w), F32), jax.ShapeDtypeStruct((m, w), F32),
                   jax.ShapeDtypeStruct((m, w), F32)),
        compiler_params=_params("arbitrary"),
    )(q, k, v, o, lse, do)


def _make_memory_attention(scale, name):
    @jax.custom_vjp
    def attn(q, k, v):
        return _memory_fwd(q, k, v, scale, name + "_fwd")[0]

    def fwd(q, k, v):
        o, lse = _memory_fwd(q, k, v, scale, name + "_fwd")
        return o, (q, k, v, o, lse)

    def bwd(res, do):
        q, k, v, o, lse = res
        return _memory_bwd(q, k, v, o, lse, do, scale, name + "_bwd")

    attn.defvjp(fwd, bwd)
    return attn


def _sink_row(sinks):
    return jnp.repeat(sinks.astype(F32), HEAD_DIM).reshape(1, -1)


def _sink_col(sinks):
    return jnp.repeat(sinks.astype(F32).reshape(-1, 2, 1), BLOCK, axis=1)


def _merge3(os_, ls_, name):
    s, w = os_[0].shape
    bs = _pick(s, (256, 128))

    def body(o1, o2, o3, l1, l2, l3, out_ref, lse_ref):
        a1, a2, a3 = l1[...], l2[...], l3[...]
        m = jnp.maximum(jnp.maximum(a1, a2), a3)
        e1, e2, e3 = jnp.exp(a1 - m), jnp.exp(a2 - m), jnp.exp(a3 - m)
        z = e1 + e2 + e3
        out_ref[...] = (e1 * o1[...] + e2 * o2[...] + e3 * o3[...]) / z
        lse_ref[...] = m + jnp.log(z)

    row = pl.BlockSpec((bs, w), lambda i: (i, 0))
    return pl.pallas_call(
        body, name=name, grid=(s // bs,), in_specs=[row] * 6, out_specs=(row, row),
        out_shape=(jax.ShapeDtypeStruct((s, w), F32), jax.ShapeDtypeStruct((s, w), F32)),
        compiler_params=_params("parallel"),
    )(*os_, *ls_)


def _make_dilated(name):
    scale, max_dist = HEAD_DIM ** -0.5, BLOCK

    def upb_of(dil):
        return 2 * BAND_UNITS_PER_STEP if dil == 1 else 1

    def forward(q, k, v):
        os_, ls_ = [], []
        for n, (_, dil) in enumerate(DIL_PATTERNS):
            o, l = _band_fwd(q, k, v, None, scale, max_dist, upb_of(dil), dil, "%s_b%d_fwd" % (name, n))
            os_.append(o)
            ls_.append(l)
        return _merge3(os_, ls_, name + "_merge")

    @jax.custom_vjp
    def dilated(q, k, v):
        return forward(q, k, v)[0]

    def fwd(q, k, v):
        out, lse = forward(q, k, v)
        return out, (q, k, v, out, lse)

    def bwd(res, do):
        q, k, v, out, lse = res
        dq, dkv = None, None
        for n, (_, dil) in enumerate(DIL_PATTERNS):
            args = (q, k, v, out, lse, do)
            dq = _band_dq(*args, None, scale, max_dist, upb_of(dil), dil, "%s_b%d_dq" % (name, n), acc=dq)
            dkv = _band_dkv(*args, scale, max_dist, upb_of(dil), dil, "%s_b%d_dkv" % (name, n), accs=dkv)
        return dq, dkv[0], dkv[1]

    dilated.defvjp(fwd, bwd)
    return dilated


def _times_w(a, w, orient, name, res=None, rope=None):
    return _mm(a, w, "nn" if orient == "n" else "nt", name, res=res, rope=rope)


def _times_wt(dz, w, orient, name, res=None):
    return _mm(dz, w, "nt" if orient == "n" else "nn", name, res=res)


def _times_w_multi(h, ws, ropes, name):
    m, k = h.shape
    bm = _div128(m, FFN_TILE_M)
    nw = len(ws)
    tabs = next((r[0] for r in ropes if r is not None), None)

    def body(*refs):
        hv = refs[0][...]
        for i in range(nw):
            r = _dot(hv, refs[1 + i][...], _NT)
            if ropes[i] is not None:
                r = _rotate_tile(r, tuple(t[...] for t in refs[1 + nw:4 + nw]), ropes[i][1], False)
            refs[len(refs) - nw + i][...] = r

    in_specs = [pl.BlockSpec((bm, k), lambda i: (i, 0))] + [pl.BlockSpec(w.shape, lambda i: (0, 0)) for w in ws]
    in_specs += [pl.BlockSpec((bm, LANES), lambda i: (i, 0))] * 3 if tabs is not None else []
    return pl.pallas_call(
        body, name=name, grid=(m // bm,), in_specs=in_specs,
        out_specs=tuple(pl.BlockSpec((bm, w.shape[0]), lambda i: (i, 0)) for w in ws),
        out_shape=tuple(jax.ShapeDtypeStruct((m, w.shape[0]), F32) for w in ws), compiler_params=_params("parallel"),
    )(h, *ws, *(tabs or ()))


def _times_wt_sum(dzs, ws, name):
    m, kdim = dzs[0].shape[0], ws[0].shape[1]
    bm = _div128(m, FFN_TILE_M)
    nw = len(ws)

    def body(*refs):
        total = None
        for i in range(nw):
            part = _dot(refs[i][...].astype(BF16), refs[nw + i][...], _NN)
            total = part if total is None else total + part
        refs[-1][...] = total

    in_specs = [pl.BlockSpec((bm, dz.shape[1]), lambda i: (i, 0)) for dz in dzs]
    in_specs += [pl.BlockSpec(w.shape, lambda i: (0, 0)) for w in ws]
    return pl.pallas_call(
        body, name=name, grid=(m // bm,), in_specs=in_specs, out_specs=pl.BlockSpec((bm, kdim), lambda i: (i, 0)),
        out_shape=jax.ShapeDtypeStruct((m, kdim), F32), compiler_params=_params("parallel"),
    )(*dzs, *ws)


def _grad_w(a, dz, orient, name):
    if orient == "n":
        return _mm(a, dz, "tn", name, out_dtype=GRAD_WIRE_DTYPE)
    return _mm(dz, a, "tn", name, out_dtype=GRAD_WIRE_DTYPE)


def _make_norm_linear(name, orients, through=False, rope_halves=None):
    nw = len(orients)
    halves = rope_halves or (None,) * nw

    def rope_of(i, ropes):
        return None if halves[i] is None else (ropes[i], halves[i])

    def forward(x, g, ws, ropes):
        h = _rms_fwd(x, g, name + "_norm")
        if nw > 1 and all(o == "t" for o in orients):
            zs = tuple(_times_w_multi(h, ws, [rope_of(i, ropes) for i in range(nw)], name + "_mm"))
        else:
            zs = tuple(_times_w(h, w, o, "%s_mm%d" % (name, i), rope=rope_of(i, ropes))
                       for i, (w, o) in enumerate(zip(ws, orients)))
        return zs + ((x,) if through else ()), h

    @jax.custom_vjp
    def op(x, g, slots, ws, ropes):
        return forward(x, g, ws, ropes)[0]

    def fwd(x, g, slots, ws, ropes):
        outs, h = forward(x, g, ws, ropes)
        return outs, (x, g, h, ws, ropes)

    def bwd(res, cts):
        x, g, h, ws, ropes = res
        dzs = [cts[i] if halves[i] is None else
               _rope_apply(cts[i], ropes[i], halves[i], True, "%s_unrope%d" % (name, i)) for i in range(nw)]
        if nw > 1 and all(o == "t" for o in orients):
            dh = _times_wt_sum(dzs, ws, name + "_dh")
        else:
            dh = None
            for i, (w, o) in enumerate(zip(ws, orients)):
                dh = _times_wt(dzs[i], w, o, "%s_dh%d" % (name, i), res=dh)
        dws = tuple(_grad_w(h, dzs[i], o, "%s_dw%d" % (name, i)) for i, o in enumerate(orients))
        dx, dg = _rms_bwd(x, g, dh, name + "_norm_bwd", dres=cts[nw] if through else None)
        return dx, dg, dws, (None,) * nw, tuple(None if r is None else (None,) * len(r) for r in ropes)

    op.defvjp(fwd, bwd)
    return op


def _make_linear_res(name, orient):
    @jax.custom_vjp
    def op(a, wslot, w, res):
        return _times_w(a, w, orient, name + "_mm", res=res)

    def fwd(a, wslot, w, res):
        return _times_w(a, w, orient, name + "_mm", res=res), (a, w)

    def bwd(saved, dout):
        a, w = saved
        return _times_wt(dout, w, orient, name + "_da"), _grad_w(a, dout, orient, name + "_dw"), None, dout

    op.defvjp(fwd, bwd)
    return op


FFN_TILE_M, FFN_TILE_N = 512, 1408


def _gate_up_act(h, wg, wu, name):
    m, k = h.shape
    n = wg.shape[0]
    bm, bn = _div128(m, FFN_TILE_M), _div128(n, FFN_TILE_N)

    def body(h_ref, wg_ref, wu_ref, g_ref, u_ref, a_ref):
        hv = h_ref[...]
        g = _dot(hv, wg_ref[...], _NT)
        u = _dot(hv, wu_ref[...], _NT)
        g_ref[...] = g
        u_ref[...] = u
        a_ref[...] = (g / (1.0 + jnp.exp(-g)) * u).astype(BF16)

    wspec = pl.BlockSpec((bn, k), lambda i, j: (j, 0))
    ospec = pl.BlockSpec((bm, bn), lambda i, j: (i, j))
    return pl.pallas_call(
        body, name=name, grid=(m // bm, n // bn), in_specs=[pl.BlockSpec((bm, k), lambda i, j: (i, 0)), wspec, wspec],
        out_specs=(ospec, ospec, ospec),
        out_shape=(jax.ShapeDtypeStruct((m, n), F32), jax.ShapeDtypeStruct((m, n), F32),
                   jax.ShapeDtypeStruct((m, n), BF16)),
        compiler_params=_params("parallel", "parallel"),
    )(h, wg, wu)


def _down_bwd_act(dout, wd, gmat, umat, name):
    m, k = dout.shape
    n = wd.shape[0]
    bm, bn = _div128(m, FFN_TILE_M), _div128(n, FFN_TILE_N)

    def body(do_ref, wd_ref, g_ref, u_ref, dg_ref, du_ref):
        d = _dot(do_ref[...].astype(BF16), wd_ref[...], _NT)
        g, u = g_ref[...], u_ref[...]
        sig = 1.0 / (1.0 + jnp.exp(-g))
        dg_ref[...] = (d * u * (sig * (1.0 + g * (1.0 - sig)))).astype(BF16)
        du_ref[...] = (d * (g * sig)).astype(BF16)

    ospec = pl.BlockSpec((bm, bn), lambda i, j: (i, j))
    return pl.pallas_call(
        body, name=name, grid=(m // bm, n // bn),
        in_specs=[pl.BlockSpec((bm, k), lambda i, j: (i, 0)), pl.BlockSpec((bn, k), lambda i, j: (j, 0)), ospec, ospec],
        out_specs=(ospec, ospec), out_shape=(jax.ShapeDtypeStruct((m, n), BF16),) * 2,
        compiler_params=_params("parallel", "parallel"),
    )(dout, wd, gmat, umat)


def _make_ffn(name):
    def forward(x, g, wg, wu, wd):
        h = _rms_fwd(x, g, name + "_norm")
        gmat, umat, a = _gate_up_act(h, wg, wu, name + "_gate_up")
        return _mm(a, wd, "nn", name + "_down", res=x), (x, g, h, gmat, umat, a, wg, wu, wd)

    @jax.custom_vjp
    def op(x, g, wg_slot, wu_slot, wd_slot, wg, wu, wd):
        return forward(x, g, wg, wu, wd)[0]

    def fwd(x, g, wg_slot, wu_slot, wd_slot, wg, wu, wd):
        return forward(x, g, wg, wu, wd)

    def bwd(saved, dout):
        x, g, h, gmat, umat, a, wg, wu, wd = saved
        dgm, dum = _down_bwd_act(dout, wd, gmat, umat, name + "_da_act")
        dwd = _mm(a, dout, "tn", name + "_dwd", out_dtype=GRAD_WIRE_DTYPE)
        dwg = _grad_w(h, dgm, "t", name + "_dwg")
        dwu = _grad_w(h, dum, "t", name + "_dwu")
        dh = _times_wt(dum, wu, "t", name + "_dh_u", res=_times_wt(dgm, wg, "t", name + "_dh_g"))
        dx, dg = _rms_bwd(x, g, dh, name + "_norm_bwd", dres=dout)
        return dx, dg, dwg, dwu, dwd, None, None, None

    op.defvjp(fwd, bwd)
    return op


def _make_final_loss(name):
    def run(x, g, tgt):
        s, d = x.shape
        bs = _pick(s, (512, 256, 128))

        def body(x_ref, g_ref, t_ref, loss_ref, dx_ref, dg_ref):
            i = pl.program_id(0)
            xv = x_ref[...]
            gv = g_ref[...]
            r = lax.rsqrt(jnp.mean(xv * xv, axis=-1, keepdims=True) + NORM_EPS)
            xh = xv * r
            e = xh * gv - t_ref[...]
            dy = e * (1.0 / d)
            dxh = dy * gv
            dx_ref[...] = r * (dxh - xh * jnp.mean(dxh * xh, axis=-1, keepdims=True))
            part = 0.5 * jnp.sum(jnp.sum(e * e, axis=-1, keepdims=True) * (1.0 / d), axis=0, keepdims=True)

            @pl.when(i == 0)
            def _():
                loss_ref[...] = jnp.zeros_like(loss_ref)
                dg_ref[...] = jnp.zeros_like(dg_ref)

            loss_ref[...] += jnp.broadcast_to(part, loss_ref.shape)
            dg_ref[...] += jnp.sum(dy * xh, axis=0, keepdims=True)

        row = pl.BlockSpec((bs, d), lambda i: (i, 0))
        vec = pl.BlockSpec((1, d), lambda i: (0, 0))
        loss, dx, dg = pl.pallas_call(
            body, name=name, grid=(s // bs,), in_specs=[row, vec, row],
            out_specs=(pl.BlockSpec((8, LANES), lambda i: (0, 0)), row, vec),
            out_shape=(jax.ShapeDtypeStruct((8, LANES), F32), jax.ShapeDtypeStruct((s, d), F32),
                       jax.ShapeDtypeStruct((1, d), F32)),
            compiler_params=_params("arbitrary"),
        )(x, g.reshape(1, d), tgt)
        return loss[0, 0], dx, dg.reshape(d)

    @jax.custom_vjp
    def op(x, g, tgt):
        return run(x, g, tgt)[0]

    def fwd(x, g, tgt):
        loss, dx, dg = run(x, g, tgt)
        return loss, (dx, dg)

    def bwd(saved, ct):
        dx, dg = saved
        return dx * ct, dg * ct, None

    op.defvjp(fwd, bwd)
    return op


def _model_loss(diff, consts):
    x = diff["x"]
    w = consts["w"]
    slot = diff["slots"]
    vec = diff["vec"]
    tab64, tab_mla = consts["tab64"], consts["tab_mla"]
    mem = consts["mem"]
    s = x.shape[0]

    rope64 = lambda t, nm: _make_rope(HEAD_DIM // 2, nm)(t, *tab64)

    def nl(nm, inp, gain, wnames, through=False, ropes=None):
        orients = tuple(_orient(n) for n in wnames)
        kinds = ropes or (None,) * len(wnames)
        halves = tuple({None: None, "64": HEAD_DIM // 2, "mla": MLA_ROPE_DIM // 2}[r] for r in kinds)
        tabs = tuple({None: None, "64": tab64, "mla": tab_mla}[r] for r in kinds)
        op = _make_norm_linear(nm, orients, through, halves)
        return op(inp, gain, tuple(slot[n] for n in wnames), tuple(w[n] for n in wnames), tabs)

    def lin_res(nm, a, wname, res):
        return _make_linear_res(nm, _orient(wname))(a, slot[wname], w[wname], res)

    def cross(layer, xin):
        p = "l%d_" % layer
        q, xin = nl(p + "xq", xin, vec[p + "x_norm"], (p + "w_xq",), through=True)
        kv, = nl(p + "xkv", mem, vec[p + "mem_norm"], (p + "w_xkv",))
        half = X_HEADS * X_HEAD_DIM
        o = _make_memory_attention(X_HEAD_DIM ** -0.5, p + "xattn")(q, kv[:, :half], kv[:, half:])
        return lin_res(p + "xo", o, p + "w_xo", xin)

    def ffn(layer, xin):
        p = "l%d_" % layer
        names = (p + "w_gate", p + "w_up", p + "w_down")
        return _make_ffn(p + "ffn")(xin, vec[p + "ffn_norm"], *(slot[n] for n in names), *(w[n] for n in names))

    in_parts = tuple(name for name, _, _ in _IN_PARTS) + ("l0_w_in_kr",)
    qa, kva, cq, ckv, kr_lanes, x = nl("l0_in", x, vec["l0_mix_norm"], in_parts, through=True,
                                       ropes=("64", None, None, None, None))
    ka = rope64(kva[:, :A_KV], "l0_rope_ka")
    va = kva[:, A_KV:]
    rep = SWA_HEADS // SWA_KV_HEADS
    expand = lambda t: jnp.broadcast_to(t.reshape(s, SWA_KV_HEADS, 1, HEAD_DIM),
                                        (s, SWA_KV_HEADS, rep, HEAD_DIM)).reshape(s, A_Q)
    swa = _make_band_attention(HEAD_DIM ** -0.5, SWA_WINDOW - 1, BAND_UNITS_PER_STEP, "l0_swa")
    oa = swa(qa, expand(ka), expand(va), vec["l0_sinks"])

    qfull, = nl("l0_uq", cq, vec["l0_q_norm"], ("l0_w_uq_heads",), ropes=("mla",))
    kvb, knope = nl("l0_ukv", ckv, vec["l0_kv_norm"], ("l0_w_ukv", "l0_w_uk_heads"))
    kfull = _make_rope_shared(MLA_ROPE_DIM // 2, "l0_rope_k")(knope, kr_lanes, *tab_mla)
    def reduce_layer1(late_grads):
        return _chip_sums(_layer1_matrix_grads(late_grads), LAYER1, MAT_ROWS - LAYER0_ROWS, "l1_grads")

    mla = _make_causal_attention((MLA_NOPE_DIM + MLA_ROPE_DIM) ** -0.5, _pick(s, (1024, 512, 256, 128)), "l0_mla",
                                 consts["layer1_shapes"], reduce_layer1)
    ob, gathered1, slots1 = mla(qfull, kfull, kvb, consts["shard1"], diff["parts1_slot"])
    w = {**w, **_layer1_weights(_unpack_gathered(gathered1, LAYER1))}
    slot = {**slot, **slots1}
    x = lin_res("l0_out_a", oa, "l0_w_out_swa", x)
    x = lin_res("l0_out_b", ob, "l0_w_out_mla", x)
    x = cross(0, x)
    x = ffn(0, x)

    q, k, v, x = nl("l1_qkv", x, vec["l1_mix_norm"], ("l1_w_q", "l1_w_k", "l1_w_v"), through=True,
                    ropes=("64", "64", None))
    o = _make_dilated("l1_dil")(q, k, v)
    x = lin_res("l1_out", o, "l1_w_out", x)
    x = cross(1, x)
    x = ffn(1, x)

    return _make_final_loss("final_loss")(x, vec["final_norm"], consts["target"])


MESH_IDS = pl.DeviceIdType.MESH
HBM_SPEC = pl.BlockSpec(memory_space=pltpu.HBM)


def _my_place():
    return lax.axis_index("x"), lax.axis_index("y"), lax.axis_index("c")


def _flip(v, bit):
    return 1 - v if bit else v


def _all_gather_rows(shard):
    r, c_ = shard.shape

    def body(x_ref, out_ref, send_sems, recv_sems, local_sem):
        x, y, c = _my_place()
        me, sibling = (x, y, c), (x, y, 1 - c)
        chips = [(1 - x, y), (x, 1 - y), (1 - x, 1 - y)]

        def slot(px, py, pc):
            return out_ref.at[4 * px + 2 * py + pc]

        def copy(k, block, to, src=None):
            return pltpu.make_async_remote_copy(
                src_ref=slot(*block) if src is None else src, dst_ref=slot(*block), send_sem=send_sems.at[k],
                recv_sem=recv_sems.at[k], device_id=to, device_id_type=MESH_IDS)

        mine = pltpu.make_async_copy(x_ref, slot(*me), local_sem)
        mine.start()
        first = [copy(0, me, sibling, src=x_ref)]
        first += [copy(1 + j, me, (*chip, c), src=x_ref) for j, chip in enumerate(chips)]
        for cp in first:
            cp.start()
        passed = [copy(4 + j, (*chip, c), sibling) for j, chip in enumerate(chips)]
        for j, chip in enumerate(chips):
            copy(1 + j, (*chip, c), me).wait_recv()
            passed[j].start()
        copy(0, sibling, me).wait_recv()
        for j, chip in enumerate(chips):
            copy(4 + j, (*chip, 1 - c), me).wait_recv()
        for cp in first + passed:
            cp.wait_send()
        mine.wait()

    return pl.pallas_call(
        body, name="weights_all_gather", out_shape=jax.ShapeDtypeStruct((N_DEV, r, c_), shard.dtype),
        in_specs=[HBM_SPEC], out_specs=HBM_SPEC,
        scratch_shapes=[pltpu.SemaphoreType.DMA((7,)), pltpu.SemaphoreType.DMA((7,)), pltpu.SemaphoreType.DMA],
    )(shard)


N_CHIPS = 4


def _exchange_with_sibling(slabs, name):
    _, nq, r, c_ = slabs.shape

    def body(p_ref, out_ref, send_sem, recv_sem):
        x, y, c = _my_place()
        cp = pltpu.make_async_remote_copy(
            src_ref=p_ref.at[1 - c], dst_ref=out_ref, send_sem=send_sem, recv_sem=recv_sem,
            device_id=(x, y, 1 - c), device_id_type=MESH_IDS)
        cp.start()
        cp.wait_recv()
        cp.wait_send()

    return pl.pallas_call(
        body, name=name, out_shape=jax.ShapeDtypeStruct((nq, r, c_), slabs.dtype),
        in_specs=[HBM_SPEC], out_specs=HBM_SPEC,
        scratch_shapes=[pltpu.SemaphoreType.DMA, pltpu.SemaphoreType.DMA],
    )(slabs)


def _add_pairs(a, b, name):
    nq, r, c_ = a.shape
    br = max(d for d in range(PART_ROW_ALIGN, r + 1, PART_ROW_ALIGN) if r % d == 0 and d <= ADD_PAIRS_MAX_ROWS)

    def body(a_ref, b_ref, o_ref):
        o_ref[...] = (a_ref[...].astype(F32) + b_ref[...].astype(F32)).astype(o_ref.dtype)

    blk = pl.BlockSpec((1, br, c_), lambda q, i: (q, i, 0))
    return pl.pallas_call(
        body, name=name, grid=(nq, r // br), in_specs=[blk, blk], out_specs=blk,
        out_shape=jax.ShapeDtypeStruct(a.shape, a.dtype), compiler_params=_params("parallel", "parallel"),
    )(a, b)


def _chip_exchange_copies(t_ref, out_ref, send_sems, recv_sems, local_sem, arrivals):
    x, y, c = _my_place()
    myq = 2 * x + y
    local = pltpu.make_async_copy(t_ref.at[myq], out_ref.at[myq], local_sem)
    sends, recvs = [], []
    for k in range(1, N_CHIPS):
        px, py = _flip(x, k & 2), _flip(y, k & 1)
        peer = 2 * px + py
        for src, dst, into in ((peer, myq, sends),) + (((myq, peer, recvs),) if arrivals else ()):
            into.append(pltpu.make_async_remote_copy(
                src_ref=t_ref.at[src], dst_ref=out_ref.at[dst], send_sem=send_sems.at[k - 1],
                recv_sem=recv_sems.at[k - 1], device_id=(px, py, c), device_id_type=MESH_IDS))
    return local, sends, recvs


def _exchange_between_chips(slabs):
    def body(t_ref, out_ref, send_sems, recv_sems, local_sem):
        local, sends, recvs = _chip_exchange_copies(t_ref, out_ref, send_sems, recv_sems, local_sem, arrivals=True)
        local.start()
        for cp in sends:
            cp.start()
        for cp in recvs:
            cp.wait_recv()
        for cp in sends:
            cp.wait_send()
        local.wait()

    return pl.pallas_call(
        body, name="grad_exchange_chips", out_shape=jax.ShapeDtypeStruct(slabs.shape, slabs.dtype),
        in_specs=[HBM_SPEC], out_specs=HBM_SPEC,
        scratch_shapes=[pltpu.SemaphoreType.DMA((N_CHIPS - 1,)), pltpu.SemaphoreType.DMA((N_CHIPS - 1,)),
                        pltpu.SemaphoreType.DMA],
    )(slabs)


def _all_reduce_small(v):
    r, c_ = v.shape

    def body(v_ref, out_ref, buf, send_sems, recv_sems):
        x, y, c = _my_place()
        me = 4 * x + 2 * y + c
        buf[me] = v_ref[...]
        sends, recvs = [], []
        for k in range(1, N_DEV):
            px, py, pc = _flip(x, k & 4), _flip(y, k & 2), _flip(c, k & 1)
            peer = 4 * px + 2 * py + pc
            sends.append(pltpu.make_async_remote_copy(
                src_ref=v_ref, dst_ref=buf.at[me], send_sem=send_sems.at[k - 1], recv_sem=recv_sems.at[k - 1],
                device_id=(px, py, pc), device_id_type=MESH_IDS))
            recvs.append(pltpu.make_async_remote_copy(
                src_ref=v_ref, dst_ref=buf.at[peer], send_sem=send_sems.at[k - 1], recv_sem=recv_sems.at[k - 1],
                device_id=(px, py, pc), device_id_type=MESH_IDS))
        for cp in sends:
            cp.start()
        for cp in recvs:
            cp.wait_recv()
        for cp in sends:
            cp.wait_send()
        acc = buf[0]
        for d in range(1, N_DEV):
            acc = acc + buf[d]
        out_ref[...] = acc

    vm = pl.BlockSpec(memory_space=pltpu.VMEM)
    return pl.pallas_call(
        body, name="vector_grad_all_reduce", out_shape=jax.ShapeDtypeStruct((r, c_), F32), in_specs=[vm], out_specs=vm,
        scratch_shapes=[pltpu.VMEM((N_DEV, r, c_), F32), pltpu.SemaphoreType.DMA((7,)), pltpu.SemaphoreType.DMA((7,))],
    )(v)


def _adamw_math(w, g, m, v):
    m = ADAM_B1 * m + (1.0 - ADAM_B1) * g
    v = ADAM_B2 * v + (1.0 - ADAM_B2) * (g * g)
    m_hat = m / (1.0 - ADAM_B1 ** ADAM_STEP)
    v_hat = v / (1.0 - ADAM_B2 ** ADAM_STEP)
    delta = -ADAM_LR * (m_hat / (jnp.sqrt(v_hat) + ADAM_EPS) + ADAM_WD * w)
    return delta, m, v


def _sum_and_adamw(parts, w, m, v):
    nparts, r, c_ = parts.shape
    br = _pick(r, (256, 128, 64, 32, 16, 8))

    def body(p_ref, w_ref, m_ref, v_ref, g_ref, d_ref, nm_ref, nv_ref):
        g = p_ref[0].astype(F32)
        for d in range(1, nparts):
            g = g + p_ref[d].astype(F32)
        g_ref[...] = g
        d_ref[...], nm_ref[...], nv_ref[...] = _adamw_math(w_ref[...], g, m_ref[...], v_ref[...])

    row = pl.BlockSpec((br, c_), lambda i: (i, 0))
    return pl.pallas_call(
        body, name="grad_sum_adamw", grid=(r // br,),
        in_specs=[pl.BlockSpec((nparts, br, c_), lambda i: (0, i, 0)), row, row, row], out_specs=(row,) * 4,
        out_shape=(jax.ShapeDtypeStruct((r, c_), F32),) * 4, compiler_params=_params("parallel"),
    )(parts, w, m, v)


def _adamw_small(w, g, m, v):
    vm = pl.BlockSpec(memory_space=pltpu.VMEM)

    def body(w_ref, g_ref, m_ref, v_ref, d_ref, nm_ref, nv_ref):
        d_ref[...], nm_ref[...], nv_ref[...] = _adamw_math(w_ref[...], g_ref[...], m_ref[...], v_ref[...])

    return pl.pallas_call(
        body, name="vector_adamw", in_specs=[vm] * 4, out_specs=(vm,) * 3,
        out_shape=(jax.ShapeDtypeStruct(w.shape, F32),) * 3,
    )(w, g, m, v)


def _pad_rows(t, axis):
    extra = -t.shape[axis] % PART_ROW_ALIGN
    if extra == 0:
        return t
    widths = [(0, 0)] * t.ndim
    widths[axis] = (0, extra)
    return jnp.pad(t, widths)


def _pack_local(named):
    rows = [_pad_rows((named[n].T if kind == "c" else named[n]).reshape(-1, PACK_COLS), 0)
            for n, kind, _, _ in MATRICES]
    rows.append(jnp.zeros((MAT_ROWS - MAT_ROWS_USED, PACK_COLS), rows[0].dtype))
    return jnp.concatenate(rows, axis=0)


def _unpack_local(packed):
    out, r0 = {}, 0
    for n, kind, k, nn in MATRICES:
        nr = k * nn // N_DEV // PACK_COLS
        part = packed[r0:r0 + nr]
        out[n] = part.reshape(nn // N_DEV, k).T if kind == "c" else part.reshape(k // N_DEV, nn)
        r0 += _part_rows(k, nn)
    return out


def _unpack_gathered(g, matrices):
    out, r0 = {}, 0
    for n, kind, k, nn in matrices:
        nr = k * nn // N_DEV // PACK_COLS
        out[n] = g[:, r0:r0 + nr].reshape((nn, k) if kind == "c" else (k, nn))
        r0 += _part_rows(k, nn)
    return out


def _chip_sums(grads, matrices, rows, name):
    slabs = _pack_full_grads(grads, matrices, rows)
    from_sibling = _exchange_with_sibling(slabs, name + "_exchange_sibling")
    mine = lax.dynamic_index_in_dim(slabs, lax.axis_index("c"), axis=0, keepdims=False)
    return _add_pairs(mine, from_sibling, name + "_chip_sum")


def _pack_full_grads(grads, matrices, rows_total):
    rows = []
    for n, _, k, nn in matrices:
        gmat = grads[n].reshape(N_CHIPS, 2, -1, PACK_COLS).transpose(1, 0, 2, 3)
        rows.append(_pad_rows(gmat, 2))
    used = sum(_part_rows(k, nn) for _, _, k, nn in matrices)
    if rows_total > used:
        rows.append(jnp.zeros((2, N_CHIPS, rows_total - used, PACK_COLS), rows[0].dtype))
    return jnp.concatenate(rows, axis=2)


def _pack_vectors(named):
    rows = [jnp.pad(named[n].astype(F32), (0, PACK_COLS - d)) for n, d in VECTORS]
    rows += [jnp.zeros((PACK_COLS,), F32)] * (VEC_ROWS - len(VECTORS))
    return jnp.stack(rows, axis=0)


def _unpack_vectors(packed):
    return {n: packed[i, :d] for i, (n, d) in enumerate(VECTORS)}


def _step(inputs):
    x = inputs["x"][0]
    mem = inputs["mem"][0]
    positions = inputs["positions"][0]
    target = inputs["loss_target"][0]

    local_w = _pack_local({n: inputs[n] for n, _, _, _ in MATRICES})
    local_bf16 = local_w.astype(BF16)
    w0full = _unpack_gathered(_all_gather_rows(local_bf16[:LAYER0_ROWS]), LAYER0)
    vec = {n: inputs[n] for n, _ in VECTORS}

    loss_part, grad_x, g0full, parts1, gvec = _local_grads(w0full, local_bf16[LAYER0_ROWS:], vec, x, mem, positions,
                                                          target)
    loss = lax.psum(loss_part, ("x", "y", "c"))

    parts0 = _exchange_between_chips(_chip_sums(g0full, LAYER0, LAYER0_ROWS, "l0_grads"))
    parts = jnp.concatenate([parts0, parts1], axis=1)
    local_m = _pack_local({n: inputs["m_" + n] for n, _, _, _ in MATRICES})
    local_v = _pack_local({n: inputs["v_" + n] for n, _, _, _ in MATRICES})
    g_pk, d_pk, m_pk, v_pk = _sum_and_adamw(parts, local_w, local_m, local_v)
    g_mat, d_mat, m_mat, v_mat = (_unpack_local(t) for t in (g_pk, d_pk, m_pk, v_pk))

    g_vec_pk = _all_reduce_small(_pack_vectors(gvec))
    d_vec_pk, m_vec_pk, v_vec_pk = _adamw_small(
        _pack_vectors(vec), g_vec_pk, _pack_vectors({n: inputs["m_" + n] for n, _ in VECTORS}),
        _pack_vectors({n: inputs["v_" + n] for n, _ in VECTORS}))
    g_vec, d_vec, m_vec, v_vec = (_unpack_vectors(t) for t in (g_vec_pk, d_vec_pk, m_vec_pk, v_vec_pk))

    def pick(mats, vecs, n):
        return mats[n] if n in mats else vecs[n]

    outs = [loss, grad_x[None]]
    for mats, vecs in ((g_mat, g_vec), (d_mat, d_vec), (m_mat, m_vec), (v_mat, v_vec)):
        outs += [pick(mats, vecs, n) for n in WEIGHT_ORDER]
    return tuple(outs)


_KIND = {n: kind for n, kind, _, _ in MATRICES}
_VIEW_OF = {"l0_w_uq_heads": "l0_w_uq", "l0_w_uk_heads": "l0_w_ukv", "l0_w_out_swa": "l0_w_out",
            "l0_w_out_mla": "l0_w_out", "l0_w_in_qa": "l0_w_in", "l0_w_in_kva": "l0_w_in", "l0_w_in_cq": "l0_w_in",
            "l0_w_in_ckv": "l0_w_in", "l0_w_in_kr": "l0_w_in", "l1_w_q": "l1_w_qkv", "l1_w_k": "l1_w_qkv",
            "l1_w_v": "l1_w_qkv"}
_IN_PARTS = (("l0_w_in_qa", 0, A_Q), ("l0_w_in_kva", A_Q, A_Q + 2 * A_KV),
             ("l0_w_in_cq", A_Q + 2 * A_KV, A_Q + 2 * A_KV + MLA_Q_RANK),
             ("l0_w_in_ckv", A_Q + 2 * A_KV + MLA_Q_RANK, EVEN_IN - MLA_ROPE_DIM))
_KR_PAD = (MLA_NOPE_DIM, LANES - MLA_NOPE_DIM - MLA_ROPE_DIM)
_MLA_QK = MLA_NOPE_DIM + MLA_ROPE_DIM


def _orient(name):
    return "t" if _KIND[_VIEW_OF.get(name, name)] == "c" else "n"


def _nope_rows():
    return (np.arange(MLA_HEADS * LANES) % LANES < MLA_NOPE_DIM)[:, None]


def _layer1_weights(wfull):
    w = dict(wfull)
    w_qkv = w.pop("l1_w_qkv")
    for i, name in enumerate(("l1_w_q", "l1_w_k", "l1_w_v")):
        w[name] = w_qkv[i * D_MODEL:(i + 1) * D_MODEL]
    return w


def _layer0_weights(wfull):
    w = dict(wfull)
    w_in = w.pop("l0_w_in")
    for name, r0, r1 in _IN_PARTS:
        w[name] = w_in[r0:r1]
    w["l0_w_in_kr"] = jnp.pad(w_in[EVEN_IN - MLA_ROPE_DIM:], (_KR_PAD, (0, 0)))
    uq = w.pop("l0_w_uq").reshape(MLA_HEADS, _MLA_QK, MLA_Q_RANK)
    w["l0_w_uq_heads"] = jnp.pad(uq, ((0, 0), (0, LANES - _MLA_QK), (0, 0))).reshape(MLA_HEADS * LANES, MLA_Q_RANK)
    w["l0_w_uk_heads"] = jnp.where(_nope_rows(), wfull["l0_w_ukv"], jnp.zeros_like(wfull["l0_w_ukv"]))
    wo = w.pop("l0_w_out")
    w["l0_w_out_swa"] = wo[:A_Q]
    w["l0_w_out_mla"] = jnp.pad(wo[A_Q:].reshape(MLA_HEADS, HEAD_DIM, D_MODEL),
                                ((0, 0), (LANES - HEAD_DIM, 0), (0, 0))).reshape(MLA_HEADS * LANES, D_MODEL)
    return w


def _layer1_matrix_grads(g):
    out = {n: g[n] for n, _, _, _ in LAYER1 if n in g}
    out["l1_w_qkv"] = jnp.concatenate([g["l1_w_q"], g["l1_w_k"], g["l1_w_v"]], axis=0)
    return out


def _layer0_matrix_grads(g):
    out = {n: g[n] for n, _, _, _ in LAYER0 if n in g}
    out["l0_w_in"] = jnp.concatenate([g[name] for name, _, _ in _IN_PARTS]
                                     + [g["l0_w_in_kr"][_KR_PAD[0]:_KR_PAD[0] + MLA_ROPE_DIM]], axis=0)
    out["l0_w_uq"] = g["l0_w_uq_heads"].reshape(MLA_HEADS, LANES, MLA_Q_RANK)[:, :_MLA_QK].reshape(-1, MLA_Q_RANK)
    uk = jnp.where(_nope_rows(), g["l0_w_uk_heads"], jnp.zeros_like(g["l0_w_uk_heads"]))
    out["l0_w_ukv"] = (g["l0_w_ukv"].astype(F32) + uk.astype(F32)).astype(g["l0_w_ukv"].dtype)
    out["l0_w_out"] = jnp.concatenate(
        [g["l0_w_out_swa"],
         g["l0_w_out_mla"].reshape(MLA_HEADS, LANES, D_MODEL)[:, LANES - HEAD_DIM:].reshape(-1, D_MODEL)], axis=0)
    return out


def _local_grads(w0full, shard1, vec, x, mem, positions, target):
    w = _layer0_weights(w0full)
    gathered1 = jax.ShapeDtypeStruct((N_DEV,) + shard1.shape, shard1.dtype)
    layer1_shapes = jax.eval_shape(lambda g: _layer1_weights(_unpack_gathered(g, LAYER1)), gathered1)
    slots = {n: jnp.zeros(t.shape, GRAD_WIRE_DTYPE) for n, t in w.items()}
    tab64 = _rope_tables(positions, HEAD_DIM, 0, HEAD_DIM)
    tab_mla = _rope_tables(positions, MLA_ROPE_DIM, MLA_NOPE_DIM, LANES)
    parts1_slot = jnp.zeros((N_CHIPS,) + shard1.shape, GRAD_WIRE_DTYPE)
    diff = {"x": x, "slots": slots, "vec": vec, "parts1_slot": parts1_slot}
    consts = {"w": w, "shard1": shard1, "mem": mem, "tab64": tab64, "tab_mla": tab_mla, "target": target,
              "layer1_shapes": layer1_shapes}
    loss_part, grads = jax.value_and_grad(lambda d: _model_loss(d, consts))(diff)
    return loss_part, grads["x"], _layer0_matrix_grads(grads["slots"]), grads["parts1_slot"], grads["vec"]


_INPUT_NAMES = (("x", "mem", "positions") + WEIGHT_ORDER + ("loss_target",)
                + tuple("m_" + n for n in WEIGHT_ORDER) + tuple("v_" + n for n in WEIGHT_ORDER))


def kernel(*args):
    assert len(args) == len(_INPUT_NAMES)
    return _step(dict(zip(_INPUT_NAMES, args)))
```
